```python
import jax, jax.numpy as jnp
from jax import lax
import numpy as np

D_MODEL = 1024
BATCH = 8
SEQ = 8192
DEPTH = 2

ATTN_WIDTH = D_MODEL // 2
HEAD_DIM = 64
N_ATTN_HEADS = ATTN_WIDTH // HEAD_DIM
CONV_WIDTH = D_MODEL - ATTN_WIDTH
CONV_K = 3
DILATED_BRANCHES = ((128, 1), (512, 4), (2048, 16))
BLOCK = 128
D_FF = 2816
N_SUB = 3
N_MOD = 3
IN_COLS = 3 * ATTN_WIDTH + 3 * CONV_WIDTH
EPS = 1e-6
NEG = -1e30

kernel_name = "hymba_dilated_attn_shortconv_macaron"


def _rmsnorm(x, g):
    xf = x.astype(jnp.float32)
    y = xf * lax.rsqrt(jnp.mean(xf * xf, axis=-1, keepdims=True) + EPS)
    return (y * g.astype(jnp.float32)).astype(x.dtype)


def _band_attention(q, k, v, span):
    n, L, h, hd = q.shape
    nb = L // BLOCK
    qb = q.reshape(n, nb, BLOCK, h, hd)
    kb = k.reshape(n, nb, BLOCK, h, hd)
    vb = v.reshape(n, nb, BLOCK, h, hd)
    zk = jnp.zeros_like(kb[:, :1])
    k2 = jnp.concatenate([jnp.concatenate([zk, kb[:, :-1]], axis=1), kb], axis=2)
    v2 = jnp.concatenate([jnp.concatenate([zk, vb[:, :-1]], axis=1), vb], axis=2)
    s = jnp.einsum('nbqhd,nbkhd->nbhqk', qb, k2).astype(jnp.float32) * (hd ** -0.5)
    qi = jnp.arange(BLOCK)[:, None] + BLOCK
    kj = jnp.arange(2 * BLOCK)[None, :]
    dist = qi - kj
    band = (dist >= 0) & (dist <= span)
    blk = jnp.arange(nb)[:, None, None]
    mask = band[None] & ((blk > 0) | (kj[None] >= BLOCK))
    s = jnp.where(mask[None, :, None], s, NEG)
    m = jnp.max(s, axis=-1, keepdims=True)
    p = jnp.exp(s - m)
    l = jnp.sum(p, axis=-1, keepdims=True)
    o = jnp.einsum('nbhqk,nbkhd->nbqhd', (p / l).astype(v.dtype), v2)
    lse = (m + jnp.log(l))[..., 0]
    return o.reshape(n, L, h, hd), lse.transpose(0, 1, 3, 2).reshape(n, L, h)


def _dilated_branch(q, k, v, window, dilation):
    b, s, h, hd = q.shape
    L = s // dilation
    Lp = -(-L // BLOCK) * BLOCK

    def to_res(t):
        t = t.reshape(b, L, dilation, h, hd).transpose(0, 2, 1, 3, 4).reshape(b * dilation, L, h, hd)
        return jnp.pad(t, ((0, 0), (0, Lp - L), (0, 0), (0, 0)))

    o, lse = _band_attention(to_res(q), to_res(k), to_res(v), window // dilation)
    o = o[:, :L].reshape(b, dilation, L, h, hd).transpose(0, 2, 1, 3, 4).reshape(b, s, h, hd)
    lse = lse[:, :L].reshape(b, dilation, L, h).transpose(0, 2, 1, 3).reshape(b, s, h)
    return o, lse


def _dilated_attention(q, k, v):
    outs, lses = [], []
    for window, dilation in DILATED_BRANCHES:
        o, lse = _dilated_branch(q, k, v, window, dilation)
        outs.append(o)
        lses.append(lse)
    wts = jax.nn.softmax(jnp.stack(lses, axis=-1), axis=-1)
    return jnp.einsum('bshr,rbshd->bshd', wts.astype(q.dtype), jnp.stack(outs, axis=0))


def _short_conv(u, w, bias):
    y = lax.conv_general_dilated(
        u, w[:, None, :].astype(u.dtype), window_strides=(1,),
        padding=((CONV_K - 1, 0),), dimension_numbers=('NWC', 'WIO', 'NWC'),
        feature_group_count=u.shape[-1])
    return y + bias


def _swiglu(h, w1, w2):
    g, up = jnp.split(h @ w1, 2, axis=-1)
    return (jax.nn.silu(g) * up) @ w2


def _mixer(h, w_in, q_g, k_g, conv_w, conv_b, w_out):
    b, s, _ = h.shape
    A, C = ATTN_WIDTH, CONV_WIDTH
    proj = h @ w_in
    q, k, v, gb, gc, u = jnp.split(proj, [A, 2 * A, 3 * A, 3 * A + C, 3 * A + 2 * C], axis=-1)
    q = _rmsnorm(q.reshape(b, s, N_ATTN_HEADS, HEAD_DIM), q_g)
    k = _rmsnorm(k.reshape(b, s, N_ATTN_HEADS, HEAD_DIM), k_g)
    v = v.reshape(b, s, N_ATTN_HEADS, HEAD_DIM)
    y_attn = _dilated_attention(q, k, v).reshape(b, s, A)
    y_conv = gb * _short_conv(gc * u, conv_w, conv_b)
    return jnp.concatenate([y_attn, y_conv], axis=-1) @ w_out


def _fwd_setup_inputs(seed: int = 0) -> dict:
    key = jax.random.key(seed)
    ks = jax.random.split(key, 14)
    D = D_MODEL
    nrm = jax.random.normal
    return {
        "x": nrm(ks[0], (BATCH, SEQ, D), jnp.float32),
        "c": nrm(ks[1], (BATCH, D), jnp.float32),
        "w_ada": nrm(ks[2], (DEPTH, D, N_SUB * N_MOD * D), jnp.float32) * (0.5 * D ** -0.5),
        "b_ada": nrm(ks[3], (DEPTH, N_SUB * N_MOD * D), jnp.float32) * 0.02,
        "norm_g": 1.0 + 0.02 * nrm(ks[4], (DEPTH, N_SUB, D), jnp.float32),
        "w_in": nrm(ks[5], (DEPTH, D, IN_COLS), jnp.float32) * D ** -0.5,
        "q_norm_g": 1.0 + 0.02 * nrm(ks[6], (DEPTH, HEAD_DIM), jnp.float32),
        "k_norm_g": 1.0 + 0.02 * nrm(ks[7], (DEPTH, HEAD_DIM), jnp.float32),
        "conv_w": nrm(ks[8], (DEPTH, CONV_K, CONV_WIDTH), jnp.float32) * CONV_K ** -0.5,
        "conv_b": nrm(ks[9], (DEPTH, CONV_WIDTH), jnp.float32) * 0.02,
        "w_out": nrm(ks[10], (DEPTH, D, D), jnp.float32) * D ** -0.5,
        "ffn_w1": nrm(ks[11], (DEPTH, 2, D, 2 * D_FF), jnp.float32) * D ** -0.5,
        "ffn_w2": nrm(ks[12], (DEPTH, 2, D_FF, D), jnp.float32) * D_FF ** -0.5,
    }


def _fwd_reference(x, c, w_ada, b_ada, norm_g, w_in, q_norm_g, k_norm_g, conv_w, conv_b,
              w_out, ffn_w1, ffn_w2):
    b = x.shape[0]
    for layer in range(DEPTH):
        mod = (jax.nn.silu(c) @ w_ada[layer] + b_ada[layer]).reshape(b, N_SUB, N_MOD, D_MODEL)
        shift = mod[:, :, 0, None, :]
        scale = mod[:, :, 1, None, :]
        gate = mod[:, :, 2, None, :]

        def ada(z, i):
            return _rmsnorm(z, norm_g[layer, i]) * (1.0 + scale[:, i]) + shift[:, i]

        x = x + 0.5 * gate[:, 0] * _swiglu(ada(x, 0), ffn_w1[layer, 0], ffn_w2[layer, 0])
        x = x + gate[:, 1] * _mixer(ada(x, 1), w_in[layer], q_norm_g[layer], k_norm_g[layer],
                                    conv_w[layer], conv_b[layer], w_out[layer])
        x = x + 0.5 * gate[:, 2] * _swiglu(ada(x, 2), ffn_w1[layer, 1], ffn_w2[layer, 1])
    return x


import jax as _jax
import jax.numpy as _jnp

TWIN_FORMAT = 'train_step'
FWD_PARAMS = ['x', 'c', 'w_ada', 'b_ada', 'norm_g', 'w_in', 'q_norm_g', 'k_norm_g', 'conv_w', 'conv_b', 'w_out', 'ffn_w1', 'ffn_w2']
TWIN_WEIGHTS = ['w_ada', 'b_ada', 'norm_g', 'w_in', 'q_norm_g', 'k_norm_g', 'conv_w', 'conv_b', 'w_out', 'ffn_w1', 'ffn_w2']
TWIN_DIFF_INPUT = 'x'
TWIN_INPUTS = ['x', 'c', 'w_ada', 'b_ada', 'norm_g', 'w_in', 'q_norm_g', 'k_norm_g', 'conv_w', 'conv_b', 'w_out', 'ffn_w1', 'ffn_w2', 'loss_target', 'm_w_ada', 'm_b_ada', 'm_norm_g', 'm_w_in', 'm_q_norm_g', 'm_k_norm_g', 'm_conv_w', 'm_conv_b', 'm_w_out', 'm_ffn_w1', 'm_ffn_w2', 'v_w_ada', 'v_b_ada', 'v_norm_g', 'v_w_in', 'v_q_norm_g', 'v_k_norm_g', 'v_conv_w', 'v_conv_b', 'v_w_out', 'v_ffn_w1', 'v_ffn_w2']
TWIN_OUTPUTS = ['loss', 'grad_x', 'grad_w_ada', 'grad_b_ada', 'grad_norm_g', 'grad_w_in', 'grad_q_norm_g', 'grad_k_norm_g', 'grad_conv_w', 'grad_conv_b', 'grad_w_out', 'grad_ffn_w1', 'grad_ffn_w2', 'delta_w_ada', 'delta_b_ada', 'delta_norm_g', 'delta_w_in', 'delta_q_norm_g', 'delta_k_norm_g', 'delta_conv_w', 'delta_conv_b', 'delta_w_out', 'delta_ffn_w1', 'delta_ffn_w2', 'new_m_w_ada', 'new_m_b_ada', 'new_m_norm_g', 'new_m_w_in', 'new_m_q_norm_g', 'new_m_k_norm_g', 'new_m_conv_w', 'new_m_conv_b', 'new_m_w_out', 'new_m_ffn_w1', 'new_m_ffn_w2', 'new_v_w_ada', 'new_v_b_ada', 'new_v_norm_g', 'new_v_w_in', 'new_v_q_norm_g', 'new_v_k_norm_g', 'new_v_conv_w', 'new_v_conv_b', 'new_v_w_out', 'new_v_ffn_w1', 'new_v_ffn_w2']
TWIN_LEAF_KINDS = {'loss': 'loss', 'grad_x': 'grad_x', 'grad_w_ada': 'grad_w', 'grad_b_ada': 'grad_w', 'grad_norm_g': 'grad_w', 'grad_w_in': 'grad_w', 'grad_q_norm_g': 'grad_w', 'grad_k_norm_g': 'grad_w', 'grad_conv_w': 'grad_w', 'grad_conv_b': 'grad_w', 'grad_w_out': 'grad_w', 'grad_ffn_w1': 'grad_w', 'grad_ffn_w2': 'grad_w', 'delta_w_ada': 'delta_w', 'delta_b_ada': 'delta_w', 'delta_norm_g': 'delta_w', 'delta_w_in': 'delta_w', 'delta_q_norm_g': 'delta_w', 'delta_k_norm_g': 'delta_w', 'delta_conv_w': 'delta_w', 'delta_conv_b': 'delta_w', 'delta_w_out': 'delta_w', 'delta_ffn_w1': 'delta_w', 'delta_ffn_w2': 'delta_w', 'new_m_w_ada': 'new_m', 'new_m_b_ada': 'new_m', 'new_m_norm_g': 'new_m', 'new_m_w_in': 'new_m', 'new_m_q_norm_g': 'new_m', 'new_m_k_norm_g': 'new_m', 'new_m_conv_w': 'new_m', 'new_m_conv_b': 'new_m', 'new_m_w_out': 'new_m', 'new_m_ffn_w1': 'new_m', 'new_m_ffn_w2': 'new_m', 'new_v_w_ada': 'new_v', 'new_v_b_ada': 'new_v', 'new_v_norm_g': 'new_v', 'new_v_w_in': 'new_v', 'new_v_q_norm_g': 'new_v', 'new_v_k_norm_g': 'new_v', 'new_v_conv_w': 'new_v', 'new_v_conv_b': 'new_v', 'new_v_w_out': 'new_v', 'new_v_ffn_w1': 'new_v', 'new_v_ffn_w2': 'new_v'}


def _forward(args):
    return _fwd_reference(*[args[k] for k in FWD_PARAMS])


def _output_shape():
    def fwd():
        inp = _fwd_setup_inputs(0)
        return _fwd_reference(*[inp[k] for k in FWD_PARAMS])
    out = _jax.eval_shape(fwd)
    return out.shape, out.dtype

N_MICROBATCH = 1
ADAM_LR = 0.001
ADAM_B1 = 0.9
ADAM_B2 = 0.999
ADAM_EPS = 1e-08
ADAM_WD = 0.01
ADAM_STEP = 10
PER_EXAMPLE_BATCH_AXIS = {'x': 0, 'c': 0, 'loss_target': 0}
SHARED_INPUTS = []
_WEIGHT_DTYPES = {'w_ada': _jnp.float32, 'b_ada': _jnp.float32, 'norm_g': _jnp.float32, 'w_in': _jnp.float32, 'q_norm_g': _jnp.float32, 'k_norm_g': _jnp.float32, 'conv_w': _jnp.float32, 'conv_b': _jnp.float32, 'w_out': _jnp.float32, 'ffn_w1': _jnp.float32, 'ffn_w2': _jnp.float32}
MOMENT_SCALE = {'w_ada': 1.738103e+00, 'b_ada': 4.928182e+00, 'norm_g': 7.868359e+00, 'w_in': 2.919841e-01, 'q_norm_g': 3.315548e-01, 'k_norm_g': 3.338738e-01, 'conv_w': 5.495952e+00, 'conv_b': 4.426513e-01, 'w_out': 2.676615e-01, 'ffn_w1': 3.695034e-02, 'ffn_w2': 6.029955e-02}


def _to_microbatches(a, axis):
    t = _jnp.moveaxis(a, axis, 0)
    t = t.reshape((N_MICROBATCH, t.shape[0] // N_MICROBATCH) + t.shape[1:])
    return _jnp.moveaxis(t, 1, axis + 1)


def setup_inputs(seed: int = 0) -> dict:
    inp = _fwd_setup_inputs(seed)
    key = _jax.random.fold_in(_jax.random.key(seed), 7919)
    shape, _ = _output_shape()
    out = dict(inp)
    out["loss_target"] = _jax.random.normal(_jax.random.fold_in(key, 0), shape, _jnp.float32)
    for i, name in enumerate(TWIN_WEIGHTS):
        w = inp[name].astype(_jnp.float32)
        if MOMENT_SCALE is None:
            s = _jnp.sqrt(_jnp.mean(_jnp.square(w)) + 1e-30)
        else:
            s = MOMENT_SCALE[name]
        km, kv = _jax.random.split(_jax.random.fold_in(key, i + 1))
        out[name] = w
        out["m_" + name] = s * _jax.random.normal(km, w.shape, _jnp.float32)
        out["v_" + name] = (s * s) * _jax.random.uniform(kv, w.shape, _jnp.float32, 0.5, 1.5)
    if N_MICROBATCH > 1:
        for name, axis in PER_EXAMPLE_BATCH_AXIS.items():
            out[name] = _to_microbatches(out[name], axis)
    return {'x': out['x'], 'c': out['c'], 'w_ada': out['w_ada'], 'b_ada': out['b_ada'], 'norm_g': out['norm_g'], 'w_in': out['w_in'], 'q_norm_g': out['q_norm_g'], 'k_norm_g': out['k_norm_g'], 'conv_w': out['conv_w'], 'conv_b': out['conv_b'], 'w_out': out['w_out'], 'ffn_w1': out['ffn_w1'], 'ffn_w2': out['ffn_w2'], 'loss_target': out['loss_target'], 'm_w_ada': out['m_w_ada'], 'm_b_ada': out['m_b_ada'], 'm_norm_g': out['m_norm_g'], 'm_w_in': out['m_w_in'], 'm_q_norm_g': out['m_q_norm_g'], 'm_k_norm_g': out['m_k_norm_g'], 'm_conv_w': out['m_conv_w'], 'm_conv_b': out['m_conv_b'], 'm_w_out': out['m_w_out'], 'm_ffn_w1': out['m_ffn_w1'], 'm_ffn_w2': out['m_ffn_w2'], 'v_w_ada': out['v_w_ada'], 'v_b_ada': out['v_b_ada'], 'v_norm_g': out['v_norm_g'], 'v_w_in': out['v_w_in'], 'v_q_norm_g': out['v_q_norm_g'], 'v_k_norm_g': out['v_k_norm_g'], 'v_conv_w': out['v_conv_w'], 'v_conv_b': out['v_conv_b'], 'v_w_out': out['v_w_out'], 'v_ffn_w1': out['v_ffn_w1'], 'v_ffn_w2': out['v_ffn_w2']}


def _loss(weights, diff, rest, loss_target):
    with _jax.named_scope("forward"):
        args = {**rest, TWIN_DIFF_INPUT: diff, **{k: w.astype(_WEIGHT_DTYPES[k]) for k, w in weights.items()}}
        y = _forward(args)
    with _jax.named_scope("loss_head"):
        err = _jnp.square(y.astype(_jnp.float32) - loss_target)
        return 0.5 * _jnp.sum(_jnp.mean(err, axis=-1)) if err.ndim else 0.5 * err


def _adamw(w, g, m, v):
    m = ADAM_B1 * m + (1.0 - ADAM_B1) * g
    v = ADAM_B2 * v + (1.0 - ADAM_B2) * _jnp.square(g)
    m_hat = m / (1.0 - ADAM_B1 ** ADAM_STEP)
    v_hat = v / (1.0 - ADAM_B2 ** ADAM_STEP)
    delta = -ADAM_LR * (m_hat / (_jnp.sqrt(v_hat) + ADAM_EPS) + ADAM_WD * w)
    return delta, m, v


def reference(x, c, w_ada, b_ada, norm_g, w_in, q_norm_g, k_norm_g, conv_w, conv_b, w_out, ffn_w1, ffn_w2, loss_target, m_w_ada, m_b_ada, m_norm_g, m_w_in, m_q_norm_g, m_k_norm_g, m_conv_w, m_conv_b, m_w_out, m_ffn_w1, m_ffn_w2, v_w_ada, v_b_ada, v_norm_g, v_w_in, v_q_norm_g, v_k_norm_g, v_conv_w, v_conv_b, v_w_out, v_ffn_w1, v_ffn_w2):
    given = dict(x=x, c=c, w_ada=w_ada, b_ada=b_ada, norm_g=norm_g, w_in=w_in, q_norm_g=q_norm_g, k_norm_g=k_norm_g, conv_w=conv_w, conv_b=conv_b, w_out=w_out, ffn_w1=ffn_w1, ffn_w2=ffn_w2, loss_target=loss_target, m_w_ada=m_w_ada, m_b_ada=m_b_ada, m_norm_g=m_norm_g, m_w_in=m_w_in, m_q_norm_g=m_q_norm_g, m_k_norm_g=m_k_norm_g, m_conv_w=m_conv_w, m_conv_b=m_conv_b, m_w_out=m_w_out, m_ffn_w1=m_ffn_w1, m_ffn_w2=m_ffn_w2, v_w_ada=v_w_ada, v_b_ada=v_b_ada, v_norm_g=v_norm_g, v_w_in=v_w_in, v_q_norm_g=v_q_norm_g, v_k_norm_g=v_k_norm_g, v_conv_w=v_conv_w, v_conv_b=v_conv_b, v_w_out=v_w_out, v_ffn_w1=v_ffn_w1, v_ffn_w2=v_ffn_w2)
    weights = {n: given[n] for n in TWIN_WEIGHTS}
    shared = {n: given[n] for n in SHARED_INPUTS}
    per_example = {n: given[n] for n in ['x', 'c']}
    grad_fn = _jax.value_and_grad(_loss, argnums=(0, 1))

    def one_microbatch(ex, loss_target):
        ex = dict(ex)
        diff = ex.pop(TWIN_DIFF_INPUT)
        return grad_fn(weights, diff, {**shared, **ex}, loss_target)

    if N_MICROBATCH == 1:
        loss, (grad_w, grad_x) = one_microbatch(per_example, given["loss_target"])
    else:
        def body(carry, xs):
            loss_sum, grad_sum = carry
            l_k, (gw_k, gx_k) = one_microbatch(xs[0], xs[1])
            with _jax.named_scope("update"):
                return (loss_sum + l_k, _jax.tree.map(_jnp.add, grad_sum, gw_k)), gx_k

        init = (_jnp.zeros((), _jnp.float32), _jax.tree.map(_jnp.zeros_like, weights))
        (loss, grad_w), grad_x = _jax.lax.scan(body, init, (per_example, given["loss_target"]))
    with _jax.named_scope("update"):
        delta_w, new_m, new_v = {}, {}, {}
        for n in TWIN_WEIGHTS:
            delta_w[n], new_m[n], new_v[n] = _adamw(weights[n], grad_w[n], given["m_" + n], given["v_" + n])
    return (loss, grad_x, *[grad_w[n] for n in TWIN_WEIGHTS], *[delta_w[n] for n in TWIN_WEIGHTS],
            *[new_m[n] for n in TWIN_WEIGHTS], *[new_v[n] for n in TWIN_WEIGHTS])
```

```python
import functools

import jax
import jax.numpy as jnp
from jax import lax
from jax.experimental import pallas as pl
from jax.experimental.pallas import tpu as pltpu

F32 = jnp.float32
MXU_DTYPE = jnp.bfloat16
ACT_DTYPE = jnp.bfloat16
WIRE_DTYPE = jnp.bfloat16

D = 1024
HD = 64
AW = 512
CW = 512
DFF = 2816
HALF = DFF // 2
INC = 3 * AW + 3 * CW
NCHIP = 4
NDEV = 8
QBLK = 128
DILATIONS = (1, 4, 16)
EPS = 1e-6
NEG = -1e30
LANES = 128
SUBLANES = 8
VMEM_LIMIT = 56 * 1024 * 1024

ADAM_LR = 0.001
ADAM_B1 = 0.9
ADAM_B2 = 0.999
ADAM_EPS = 1e-08
ADAM_WD = 0.01
ADAM_STEP = 10

NT_DIMS = (((1,), (1,)), ((), ()))
TN_DIMS = (((0,), (0,)), ((), ()))


def _params(sem, vmem=VMEM_LIMIT):
    return pltpu.CompilerParams(dimension_semantics=sem, vmem_limit_bytes=vmem)


def _row_tile(n, want):
    t = min(n, want)
    assert n % t == 0
    return t


def _ada(xt, vec_ref):
    ng, sc, sh, gt = vec_ref[0:1, :], vec_ref[1:2, :], vec_ref[2:3, :], vec_ref[3:4, :]
    r = lax.rsqrt(jnp.mean(xt * xt, axis=-1, keepdims=True) + EPS)
    return xt * r, r, ng * (1.0 + sc), ng, sc, sh, gt


def _ada_bwd(dh, xhat, r, gain, ng, sc):
    dshift = jnp.sum(dh, axis=0, keepdims=True)
    dhx = dh * xhat
    dscale = jnp.sum(dhx, axis=0, keepdims=True) * ng
    dng = jnp.sum(dhx, axis=0, keepdims=True) * (1.0 + sc)
    dxhat = dh * gain
    dx = r * (dxhat - xhat * jnp.mean(dxhat * xhat, axis=-1, keepdims=True))
    return dx, dshift, dscale, dng


def _acc_rows(sums_ref, first, rows):
    @pl.when(first)
    def _():
        sums_ref[...] = jnp.zeros_like(sums_ref)
    for k, row in enumerate(rows):
        sums_ref[k:k + 1, :] += row


def ffn_fwd(x, vec, w1p, w2, gs, name):
    S = x.shape[0]
    tm = _row_tile(S, 512)

    def body(x_ref, vec_ref, w1_ref, w2_ref, xn_ref, a_ref, f_ref):
        xt = x_ref[...]
        xhat, _, gain, _, _, sh, gt = _ada(xt, vec_ref)
        h = (xhat * gain + sh).astype(MXU_DTYPE)
        f = jnp.zeros((tm, D), F32)
        for hf in range(2):
            g = jnp.dot(h, w1_ref[hf], preferred_element_type=F32)
            up = jnp.dot(h, w1_ref[2 + hf], preferred_element_type=F32)
            a_ref[:, hf * HALF:(hf + 1) * HALF] = g.astype(a_ref.dtype)
            a_ref[:, DFF + hf * HALF:DFF + (hf + 1) * HALF] = up.astype(a_ref.dtype)
            act = (g * jax.nn.sigmoid(g) * up).astype(MXU_DTYPE)
            f = f + jnp.dot(act, w2_ref[hf * HALF:(hf + 1) * HALF, :], preferred_element_type=F32)
        xn_ref[...] = xt + (gs * gt) * f
        f_ref[...] = f.astype(f_ref.dtype)

    return pl.pallas_call(
        body, name=name, grid=(S // tm,),
        in_specs=[pl.BlockSpec((tm, D), lambda i: (i, 0)),
                  pl.BlockSpec((SUBLANES, D), lambda i: (0, 0)),
                  pl.BlockSpec((NCHIP, D, HALF), lambda i: (0, 0, 0), pipeline_mode=pl.Buffered(1)),
                  pl.BlockSpec((DFF, D), lambda i: (0, 0), pipeline_mode=pl.Buffered(1))],
        out_specs=[pl.BlockSpec((tm, D), lambda i: (i, 0)),
                   pl.BlockSpec((tm, 2 * DFF), lambda i: (i, 0)),
                   pl.BlockSpec((tm, D), lambda i: (i, 0))],
        out_shape=[jax.ShapeDtypeStruct((S, D), F32),
                   jax.ShapeDtypeStruct((S, 2 * DFF), ACT_DTYPE),
                   jax.ShapeDtypeStruct((S, D), ACT_DTYPE)],
        compiler_params=_params(("arbitrary",)),
    )(x, vec, w1p, w2)


def ffn_bwd(dxo, x, a, f, vec, w1p, w2, gs, name):
    S = x.shape[0]
    tm = _row_tile(S, 256)

    def body(dxo_ref, x_ref, a_ref, f_ref, vec_ref, w1_ref, w2_ref,
             dxi_ref, hb_ref, dfb_ref, act_ref, da_ref, sums_ref):
        xt = x_ref[...]
        dxo = dxo_ref[...]
        xhat, r, gain, ng, sc, sh, gt = _ada(xt, vec_ref)
        hb_ref[...] = (xhat * gain + sh).astype(hb_ref.dtype)
        dgate = gs * jnp.sum(dxo * f_ref[...].astype(F32), axis=0, keepdims=True)
        df = ((gs * gt) * dxo).astype(MXU_DTYPE)
        dfb_ref[...] = df
        dh = jnp.zeros((tm, D), F32)
        for hf in range(2):
            lo, hi = hf * HALF, (hf + 1) * HALF
            dact = lax.dot_general(df, w2_ref[lo:hi, :], NT_DIMS, preferred_element_type=F32)
            g = a_ref[:, lo:hi].astype(F32)
            up = a_ref[:, DFF + lo:DFF + hi].astype(F32)
            sg = jax.nn.sigmoid(g)
            si = g * sg
            act_ref[:, lo:hi] = (si * up).astype(act_ref.dtype)
            dg = (dact * up * (sg * (1.0 + g * (1.0 - sg)))).astype(MXU_DTYPE)
            dup = (dact * si).astype(MXU_DTYPE)
            da_ref[:, lo:hi] = dg
            da_ref[:, DFF + lo:DFF + hi] = dup
            dh = dh + lax.dot_general(dg, w1_ref[hf], NT_DIMS, preferred_element_type=F32)
            dh = dh + lax.dot_general(dup, w1_ref[2 + hf], NT_DIMS, preferred_element_type=F32)
        dx, dshift, dscale, dng = _ada_bwd(dh, xhat, r, gain, ng, sc)
        dxi_ref[...] = dxo + dx
        _acc_rows(sums_ref, pl.program_id(0) == 0, (dshift, dscale, dng, dgate))

    return pl.pallas_call(
        body, name=name, grid=(S // tm,),
        in_specs=[pl.BlockSpec((tm, D), lambda i: (i, 0)),
                  pl.BlockSpec((tm, D), lambda i: (i, 0)),
                  pl.BlockSpec((tm, 2 * DFF), lambda i: (i, 0)),
                  pl.BlockSpec((tm, D), lambda i: (i, 0)),
                  pl.BlockSpec((SUBLANES, D), lambda i: (0, 0)),
                  pl.BlockSpec((NCHIP, D, HALF), lambda i: (0, 0, 0), pipeline_mode=pl.Buffered(1)),
                  pl.BlockSpec((DFF, D), lambda i: (0, 0), pipeline_mode=pl.Buffered(1))],
        out_specs=[pl.BlockSpec((tm, D), lambda i: (i, 0)),
                   pl.BlockSpec((tm, D), lambda i: (i, 0)),
                   pl.BlockSpec((tm, D), lambda i: (i, 0)),
                   pl.BlockSpec((tm, DFF), lambda i: (i, 0)),
                   pl.BlockSpec((tm, 2 * DFF), lambda i: (i, 0)),
                   pl.BlockSpec((SUBLANES, D), lambda i: (0, 0))],
        out_shape=[jax.ShapeDtypeStruct((S, D), F32),
                   jax.ShapeDtypeStruct((S, D), MXU_DTYPE),
                   jax.ShapeDtypeStruct((S, D), MXU_DTYPE),
                   jax.ShapeDtypeStruct((S, DFF), MXU_DTYPE),
                   jax.ShapeDtypeStruct((S, 2 * DFF), MXU_DTYPE),
                   jax.ShapeDtypeStruct((SUBLANES, D), F32)],
        compiler_params=_params(("arbitrary",)),
    )(dxo, x, a, f, vec, w1p, w2)


def wgrad(a, b, kt, nt, name):
    T, K = a.shape
    N = b.shape[1]
    pk, pn = K // kt, N // nt
    assert pk == 1 or pn == 1
    tt = _row_tile(T, 1024)
    steps = T // tt

    def body(a_ref, b_ref, o_ref):
        @pl.when(pl.program_id(1) == 0)
        def _():
            o_ref[...] = jnp.zeros_like(o_ref)
        o_ref[...] += lax.dot_general(a_ref[...], b_ref[...], TN_DIMS, preferred_element_type=F32)

    a_map = (lambda p, t: (t, p)) if pk > 1 else (lambda p, t: (t, 0))
    b_map = (lambda p, t: (t, p)) if pn > 1 else (lambda p, t: (t, 0))
    return pl.pallas_call(
        body, name=name, grid=(pk * pn, steps),
        in_specs=[pl.BlockSpec((tt, kt), a_map), pl.BlockSpec((tt, nt), b_map)],
        out_specs=pl.BlockSpec((None, kt, nt), lambda p, t: (p, 0, 0)),
        out_shape=jax.ShapeDtypeStruct((pk * pn, kt, nt), F32),
        compiler_params=_params(("arbitrary", "arbitrary")),
    )(a, b)


def _head_masks(rows):
    lane = lax.broadcasted_iota(jnp.int32, (rows, LANES), 1)
    return lane < HD


def _pair_stat(x, m_a):
    s_a = jnp.sum(jnp.where(m_a, x, 0.0), axis=1, keepdims=True)
    s_b = jnp.sum(jnp.where(m_a, 0.0, x), axis=1, keepdims=True)
    return s_a, s_b


def mixer_in(x, vec, winp, gvec, name):
    S = x.shape[0]
    tm = _row_tile(S, 512)
    pc = INC // NCHIP

    def body(x_ref, vec_ref, w_ref, g_ref, proj_ref, hb_ref, qn_ref, kn_ref, v_ref):
        xt = x_ref[...]
        xhat, _, gain, _, _, sh, _ = _ada(xt, vec_ref)
        h = (xhat * gain + sh).astype(MXU_DTYPE)
        hb_ref[...] = h
        for j in range(NCHIP):
            proj_ref[:, j * pc:(j + 1) * pc] = jnp.dot(h, w_ref[j], preferred_element_type=F32)
        m_a = _head_masks(tm)
        for which, dst in ((0, qn_ref), (1, kn_ref)):
            for p in range(AW // LANES):
                lo = which * AW + p * LANES
                xp = proj_ref[:, lo:lo + LANES]
                s_a, s_b = _pair_stat(xp * xp, m_a)
                rr = jnp.where(m_a, lax.rsqrt(s_a * (1.0 / HD) + EPS), lax.rsqrt(s_b * (1.0 / HD) + EPS))
                gp = g_ref[which:which + 1, p * LANES:(p + 1) * LANES]
                dst[:, p * LANES:(p + 1) * LANES] = (xp * rr * gp).astype(dst.dtype)
        v_ref[...] = proj_ref[:, 2 * AW:3 * AW].astype(v_ref.dtype)

    return pl.pallas_call(
        body, name=name, grid=(S // tm,),
        in_specs=[pl.BlockSpec((tm, D), lambda i: (i, 0)),
                  pl.BlockSpec((SUBLANES, D), lambda i: (0, 0)),
                  pl.BlockSpec((NCHIP, D, pc), lambda i: (0, 0, 0), pipeline_mode=pl.Buffered(1)),
                  pl.BlockSpec((SUBLANES, AW), lambda i: (0, 0))],
        out_specs=[pl.BlockSpec((tm, INC), lambda i: (i, 0)),
                   pl.BlockSpec((tm, D), lambda i: (i, 0)),
                   pl.BlockSpec((tm, AW), lambda i: (i, 0)),
                   pl.BlockSpec((tm, AW), lambda i: (i, 0)),
                   pl.BlockSpec((tm, AW), lambda i: (i, 0))],
        out_shape=[jax.ShapeDtypeStruct((S, INC), F32),
                   jax.ShapeDtypeStruct((S, D), MXU_DTYPE),
                   jax.ShapeDtypeStruct((S, AW), MXU_DTYPE),
                   jax.ShapeDtypeStruct((S, AW), MXU_DTYPE),
                   jax.ShapeDtypeStruct((S, AW), MXU_DTYPE)],
        compiler_params=_params(("arbitrary",)),
    )(x, vec, winp, gvec)


def _band_masks(ncol):
    row = lax.broadcasted_iota(jnp.int32, (QBLK, ncol), 0)
    col = lax.broadcasted_iota(jnp.int32, (QBLK, ncol), 1)
    return row, col


def attn_fwd(qn, kn, v, d, name):
    S = qn.shape[0]
    L = S // d
    tq = _row_tile(L, 512)
    qb = tq // QBLK
    qv, kv, vv = (t.reshape(L, d * AW) for t in (qn, kn, v))

    def body(q_ref, kc_ref, kp_ref, vc_ref, vp_ref, o_ref, lse_ref):
        i = pl.program_id(1)
        m_a = _head_masks(QBLK)
        row, col = _band_masks(2 * QBLK)
        dist = row + QBLK - col
        band = (dist >= 0) & (dist <= QBLK)
        kfull = jnp.concatenate([kp_ref[...], kc_ref[...]], axis=0)
        vfull = jnp.concatenate([vp_ref[...], vc_ref[...]], axis=0)
        for b in range(qb):
            q = q_ref[b * QBLK:(b + 1) * QBLK, :]
            kcat = kfull[b * QBLK:(b + 2) * QBLK]
            vcat = vfull[b * QBLK:(b + 2) * QBLK]
            mask = band & ((i > 0) | (col >= QBLK)) if b == 0 else band
            outs, lses = [], []
            for hm in (m_a, jnp.logical_not(m_a)):
                qh = jnp.where(hm, q, jnp.zeros_like(q))
                s = lax.dot_general(qh, kcat, NT_DIMS, preferred_element_type=F32) * (HD ** -0.5)
                s = jnp.where(mask, s, NEG)
                m = jnp.max(s, axis=1, keepdims=True)
                p = jnp.exp(s - m)
                l = jnp.sum(p, axis=1, keepdims=True)
                outs.append(jnp.dot(p.astype(MXU_DTYPE), vcat, preferred_element_type=F32) / l)
                lses.append(m + jnp.log(l))
            o_ref[b * QBLK:(b + 1) * QBLK, :] = jnp.where(m_a, outs[0], outs[1])
            lse_ref[b * QBLK:(b + 1) * QBLK, 0:LANES] = jnp.broadcast_to(lses[0], (QBLK, LANES))
            lse_ref[b * QBLK:(b + 1) * QBLK, LANES:2 * LANES] = jnp.broadcast_to(lses[1], (QBLK, LANES))

    cur = pl.BlockSpec((tq, LANES), lambda cb, i: (i, cb))
    prev = pl.BlockSpec((QBLK, LANES), lambda cb, i: (jnp.maximum(i * qb - 1, 0), cb))
    o, lse = pl.pallas_call(
        body, name=name, grid=(d * AW // LANES, L // tq),
        in_specs=[cur, cur, prev, cur, prev],
        out_specs=[cur, pl.BlockSpec((tq, 2 * LANES), lambda cb, i: (i, cb))],
        out_shape=[jax.ShapeDtypeStruct((L, d * AW), F32),
                   jax.ShapeDtypeStruct((L, d * 2 * AW), F32)],
        compiler_params=_params(("arbitrary", "arbitrary")),
    )(qv, kv, kv, vv, vv)
    return o.reshape(S, AW), lse.reshape(S, 2 * AW)


def attn_bwd(qn, kn, v, dycat, lse, delta, d, name):
    S = qn.shape[0]
    L = S // d
    tq = _row_tile(L, 512)
    qb = tq // QBLK
    nblk = L // QBLK
    ntile = L // tq
    qv, kv, vv = (t.reshape(L, d * AW) for t in (qn, kn, v))
    dov = dycat.reshape(L, d * D)
    lsev = lse.reshape(L, d * 2 * AW)
    dlv = delta.reshape(L, d * 2 * AW)

    def body(q_ref, qx_ref, kc_ref, kp_ref, vc_ref, vp_ref, do_ref, dox_ref, lse_ref, lsex_ref,
             dl_ref, dlx_ref, dq_ref, dk_ref, dv_ref):
        i = pl.program_id(1)
        has_next = i < ntile - 1
        m_a = _head_masks(QBLK)
        row, col = _band_masks(2 * QBLK)
        dist = row + QBLK - col
        band = (dist >= 0) & (dist <= QBLK)
        row1, col1 = _band_masks(QBLK)
        off_only = (col1 >= row1) & has_next
        kfull = jnp.concatenate([kp_ref[...], kc_ref[...]], axis=0)
        vfull = jnp.concatenate([vp_ref[...], vc_ref[...]], axis=0)
        qfull = jnp.concatenate([q_ref[...], qx_ref[...]], axis=0)
        dofull = jnp.concatenate([do_ref[...], dox_ref[...]], axis=0).astype(MXU_DTYPE)
        lsefull = jnp.concatenate([lse_ref[...], lsex_ref[...]], axis=0)
        dlfull = jnp.concatenate([dl_ref[...], dlx_ref[...]], axis=0)
        dk_acc = [jnp.zeros((QBLK, LANES), F32) for _ in range(qb)]
        dv_acc = [jnp.zeros((QBLK, LANES), F32) for _ in range(qb)]
        for x in range(qb + 1):
            parts = 2 if x < qb else 1
            qx = qfull[x * QBLK:(x + 1) * QBLK]
            dox = dofull[x * QBLK:(x + 1) * QBLK]
            kcat = kfull[x * QBLK:(x + parts) * QBLK]
            vcat = vfull[x * QBLK:(x + parts) * QBLK]
            if x == 0:
                mask = band & ((i > 0) | (col >= QBLK))
            elif x < qb:
                mask = band
            else:
                mask = off_only
            dq_heads = []
            for hd, hm in enumerate((m_a, jnp.logical_not(m_a))):
                qh = jnp.where(hm, qx, jnp.zeros_like(qx))
                doh = jnp.where(hm, dox, jnp.zeros_like(dox))
                s = lax.dot_general(qh, kcat, NT_DIMS, preferred_element_type=F32) * (HD ** -0.5)
                s = jnp.where(mask, s, NEG)
                lse_h = lsefull[x * QBLK:(x + 1) * QBLK, hd * LANES:(hd + 1) * LANES]
                dl_h = dlfull[x * QBLK:(x + 1) * QBLK, hd * LANES:(hd + 1) * LANES]
                if parts == 2:
                    lse_h = jnp.concatenate([lse_h, lse_h], axis=1)
                    dl_h = jnp.concatenate([dl_h, dl_h], axis=1)
                p = jnp.exp(s - lse_h)
                dp = lax.dot_general(doh, vcat, NT_DIMS, preferred_element_type=F32)
                ds = p * (dp - dl_h) * (HD ** -0.5)
                if x < qb:
                    dq_heads.append(jnp.dot(ds.astype(MXU_DTYPE), kcat, preferred_element_type=F32))
                ds_t = ds.T.astype(MXU_DTYPE)
                p_t = p.T.astype(MXU_DTYPE)
                for part in range(parts):
                    kb = x - 1 + part
                    if 0 <= kb < qb:
                        sl = slice(part * QBLK, (part + 1) * QBLK)
                        dk_acc[kb] = dk_acc[kb] + jnp.dot(ds_t[sl], qh, preferred_element_type=F32)
                        dv_acc[kb] = dv_acc[kb] + jnp.dot(p_t[sl], doh, preferred_element_type=F32)
            if x < qb:
                dq_ref[x * QBLK:(x + 1) * QBLK, :] = jnp.where(m_a, dq_heads[0], dq_heads[1])
        for kb in range(qb):
            dk_ref[kb * QBLK:(kb + 1) * QBLK, :] = dk_acc[kb]
            dv_ref[kb * QBLK:(kb + 1) * QBLK, :] = dv_acc[kb]

    def nxt(cb, i):
        return jnp.minimum((i + 1) * qb, nblk - 1)

    def docol(cb):
        return (cb // (AW // LANES)) * (D // LANES) + cb % (AW // LANES)

    cur = pl.BlockSpec((tq, LANES), lambda cb, i: (i, cb))
    prev = pl.BlockSpec((QBLK, LANES), lambda cb, i: (jnp.maximum(i * qb - 1, 0), cb))
    nx = pl.BlockSpec((QBLK, LANES), lambda cb, i: (nxt(cb, i), cb))
    st = pl.BlockSpec((tq, 2 * LANES), lambda cb, i: (i, cb))
    stx = pl.BlockSpec((QBLK, 2 * LANES), lambda cb, i: (nxt(cb, i), cb))
    outs = pl.pallas_call(
        body, name=name, grid=(d * AW // LANES, ntile),
        in_specs=[cur, nx, cur, prev, cur, prev,
                  pl.BlockSpec((tq, LANES), lambda cb, i: (i, docol(cb))),
                  pl.BlockSpec((QBLK, LANES), lambda cb, i: (nxt(cb, i), docol(cb))),
                  st, stx, st, stx],
        out_specs=[cur, cur, cur],
        out_shape=[jax.ShapeDtypeStruct((L, d * AW), F32)] * 3,
        compiler_params=_params(("arbitrary", "arbitrary")),
    )(qv, qv, kv, kv, vv, vv, dov, dov, lsev, lsev, dlv, dlv)
    return tuple(t.reshape(S, AW) for t in outs)


def _shift_down(x, halo_prev, k, row):
    tm = x.shape[0]
    tail = jnp.concatenate([pltpu.roll(halo_prev, k, 0), jnp.zeros((tm - SUBLANES, x.shape[1]), x.dtype)], axis=0)
    return jnp.where(row < k, tail, pltpu.roll(x, k, 0))


def _shift_up(x, halo_next, k, row):
    tm = x.shape[0]
    head = jnp.concatenate([jnp.zeros((tm - SUBLANES, x.shape[1]), x.dtype), pltpu.roll(halo_next, SUBLANES - k, 0)], axis=0)
    return jnp.where(row >= tm - k, head, pltpu.roll(x, tm - k, 0))


def _conv_fwd(cu, halo_cu, cw_ref, row):
    u1 = _shift_down(cu, halo_cu, 1, row)
    u2 = _shift_down(cu, halo_cu, 2, row)
    cv = cw_ref[0:1, :] * u2 + cw_ref[1:2, :] * u1 + cw_ref[2:3, :] * cu + cw_ref[3:4, :]
    return cv, u1, u2


def combine_conv(os_, lses, proj, cw, name):
    S = proj.shape[0]
    tm = _row_tile(S, 512)
    hb = tm // SUBLANES

    def body(o1, o2, o3, l1, l2, l3, pc_ref, ph_ref, cw_ref, ycat_ref, lse_ref):
        i = pl.program_id(0)
        m_a = _head_masks(tm)
        for p in range(AW // LANES):
            tot = []
            for hd in range(2):
                lo = (2 * p + hd) * LANES
                ls = [l[:, lo:lo + LANES] for l in (l1, l2, l3)]
                mx = jnp.maximum(jnp.maximum(ls[0], ls[1]), ls[2])
                t = mx + jnp.log(jnp.exp(ls[0] - mx) + jnp.exp(ls[1] - mx) + jnp.exp(ls[2] - mx))
                lse_ref[:, lo:lo + LANES] = t
                tot.append((ls, t))
            acc = jnp.zeros((tm, LANES), F32)
            for r, o in enumerate((o1, o2, o3)):
                w = jnp.where(m_a, jnp.exp(tot[0][0][r] - tot[0][1]), jnp.exp(tot[1][0][r] - tot[1][1]))
                acc = acc + w * o[:, p * LANES:(p + 1) * LANES]
            ycat_ref[:, p * LANES:(p + 1) * LANES] = acc.astype(ycat_ref.dtype)
        row = lax.broadcasted_iota(jnp.int32, (tm, CW), 0)
        gb, gc, u = pc_ref[:, 0:CW], pc_ref[:, CW:2 * CW], pc_ref[:, 2 * CW:3 * CW]
        halo_cu = jnp.where(i > 0, ph_ref[:, CW:2 * CW] * ph_ref[:, 2 * CW:3 * CW], 0.0)
        cv, _, _ = _conv_fwd(gc * u, halo_cu, cw_ref, row)
        ycat_ref[:, AW:AW + CW] = (gb * cv).astype(ycat_ref.dtype)

    ot = pl.BlockSpec((tm, AW), lambda i: (i, 0))
    lt = pl.BlockSpec((tm, 2 * AW), lambda i: (i, 0))
    return pl.pallas_call(
        body, name=name, grid=(S // tm,),
        in_specs=[ot, ot, ot, lt, lt, lt,
                  pl.BlockSpec((tm, 3 * CW), lambda i: (i, 1)),
                  pl.BlockSpec((SUBLANES, 3 * CW), lambda i: (jnp.maximum(i * hb - 1, 0), 1)),
                  pl.BlockSpec((SUBLANES, CW), lambda i: (0, 0))],
        out_specs=[pl.BlockSpec((tm, D), lambda i: (i, 0)), lt],
        out_shape=[jax.ShapeDtypeStruct((S, D), ACT_DTYPE), jax.ShapeDtypeStruct((S, 2 * AW), F32)],
        compiler_params=_params(("arbitrary",)),
    )(*os_, *lses, proj, proj, cw)


def out_proj(ycat, x, vec, wout, name):
    S = x.shape[0]
    tm = _row_tile(S, 512)

    def body(yc_ref, x_ref, vec_ref, w_ref, xn_ref, y_ref):
        y = jnp.dot(yc_ref[...].astype(MXU_DTYPE), w_ref[...], preferred_element_type=F32)
        xn_ref[...] = x_ref[...] + vec_ref[3:4, :] * y
        y_ref[...] = y.astype(y_ref.dtype)

    t = pl.BlockSpec((tm, D), lambda i: (i, 0))
    return pl.pallas_call(
        body, name=name, grid=(S // tm,),
        in_specs=[t, t, pl.BlockSpec((SUBLANES, D), lambda i: (0, 0)),
                  pl.BlockSpec((D, D), lambda i: (0, 0))],
        out_specs=[t, t],
        out_shape=[jax.ShapeDtypeStruct((S, D), F32), jax.ShapeDtypeStruct((S, D), ACT_DTYPE)],
        compiler_params=_params(("arbitrary",)),
    )(ycat, x, vec, wout)


def out_proj_bwd(dxo, y, ycat, vec, wout, name):
    S = dxo.shape[0]
    tm = _row_tile(S, 512)

    def body(dxo_ref, y_ref, yc_ref, vec_ref, w_ref, dyb_ref, dyc_ref, dl_ref, sums_ref):
        dxo = dxo_ref[...]
        dgate = jnp.sum(dxo * y_ref[...].astype(F32), axis=0, keepdims=True)
        dy = (vec_ref[3:4, :] * dxo).astype(MXU_DTYPE)
        dyb_ref[...] = dy
        dyc_ref[...] = lax.dot_general(dy, w_ref[...], NT_DIMS, preferred_element_type=F32)
        m_a = _head_masks(tm)
        for p in range(AW // LANES):
            prod = dyc_ref[:, p * LANES:(p + 1) * LANES] * yc_ref[:, p * LANES:(p + 1) * LANES].astype(F32)
            s_a, s_b = _pair_stat(prod, m_a)
            dl_ref[:, 2 * p * LANES:(2 * p + 1) * LANES] = jnp.broadcast_to(s_a, (tm, LANES))
            dl_ref[:, (2 * p + 1) * LANES:(2 * p + 2) * LANES] = jnp.broadcast_to(s_b, (tm, LANES))
        _acc_rows(sums_ref, pl.program_id(0) == 0, (dgate,))

    t = pl.BlockSpec((tm, D), lambda i: (i, 0))
    return pl.pallas_call(
        body, name=name, grid=(S // tm,),
        in_specs=[t, t, t, pl.BlockSpec((SUBLANES, D), lambda i: (0, 0)),
                  pl.BlockSpec((D, D), lambda i: (0, 0))],
        out_specs=[t, t, t, pl.BlockSpec((SUBLANES, D), lambda i: (0, 0))],
        out_shape=[jax.ShapeDtypeStruct((S, D), MXU_DTYPE), jax.ShapeDtypeStruct((S, D), F32),
                   jax.ShapeDtypeStruct((S, 2 * AW), F32), jax.ShapeDtypeStruct((SUBLANES, D), F32)],
        compiler_params=_params(("arbitrary",)),
    )(dxo, y, ycat, vec, wout)


def mixer_mid_bwd(dqs, dks, dvs, proj, dycat, gvec, cw, name):
    S = proj.shape[0]
    tm = _row_tile(S, 256)
    hb = tm // SUBLANES
    nsl = S // SUBLANES
    ntile = S // tm

    def body(dq1, dq2, dq3, dk1, dk2, dk3, dv1, dv2, dv3, pr_ref, pp_ref, pn_ref, dyc_ref, dyn_ref,
             g_ref, cw_ref, dp_ref, sums_ref):
        i = pl.program_id(0)
        m_a = _head_masks(tm)
        gsum = []
        for which, parts in ((0, (dq1, dq2, dq3)), (1, (dk1, dk2, dk3))):
            acc_g = []
            for p in range(AW // LANES):
                lo = which * AW + p * LANES
                cs = slice(p * LANES, (p + 1) * LANES)
                xp = pr_ref[:, lo:lo + LANES]
                s_a, s_b = _pair_stat(xp * xp, m_a)
                rr = jnp.where(m_a, lax.rsqrt(s_a * (1.0 / HD) + EPS), lax.rsqrt(s_b * (1.0 / HD) + EPS))
                xh = xp * rr
                dn = parts[0][:, cs] + parts[1][:, cs] + parts[2][:, cs]
                acc_g.append(jnp.sum(dn * xh, axis=0, keepdims=True))
                t = dn * g_ref[which:which + 1, cs]
                t_a, t_b = _pair_stat(t * xh, m_a)
                mean = jnp.where(m_a, t_a, t_b) * (1.0 / HD)
                dp_ref[:, lo:lo + LANES] = (rr * (t - xh * mean)).astype(dp_ref.dtype)
            gsum.append(jnp.concatenate(acc_g, axis=1))
        dp_ref[:, 2 * AW:3 * AW] = (dv1[...] + dv2[...] + dv3[...]).astype(dp_ref.dtype)
        row = lax.broadcasted_iota(jnp.int32, (tm, CW), 0)
        base = 3 * AW
        gb, gc, u = pr_ref[:, base:base + CW], pr_ref[:, base + CW:base + 2 * CW], pr_ref[:, base + 2 * CW:base + 3 * CW]
        cu = gc * u
        halo_cu = jnp.where(i > 0, pp_ref[:, CW:2 * CW] * pp_ref[:, 2 * CW:3 * CW], 0.0)
        cv, u1, u2 = _conv_fwd(cu, halo_cu, cw_ref, row)
        dyc = dyc_ref[...]
        dp_ref[:, base:base + CW] = (dyc * cv).astype(dp_ref.dtype)
        dcv = dyc * gb
        halo_dcv = jnp.where(i < ntile - 1, dyn_ref[...] * pn_ref[:, 0:CW], 0.0)
        d1 = _shift_up(dcv, halo_dcv, 1, row)
        d2 = _shift_up(dcv, halo_dcv, 2, row)
        dcu = cw_ref[2:3, :] * dcv + cw_ref[1:2, :] * d1 + cw_ref[0:1, :] * d2
        dp_ref[:, base + CW:base + 2 * CW] = (dcu * u).astype(dp_ref.dtype)
        dp_ref[:, base + 2 * CW:base + 3 * CW] = (dcu * gc).astype(dp_ref.dtype)
        rows = (gsum[0], gsum[1],
                jnp.sum(dcv * u2, axis=0, keepdims=True), jnp.sum(dcv * u1, axis=0, keepdims=True),
                jnp.sum(dcv * cu, axis=0, keepdims=True), jnp.sum(dcv, axis=0, keepdims=True))
        _acc_rows(sums_ref, i == 0, rows)

    at = pl.BlockSpec((tm, AW), lambda i: (i, 0))
    return pl.pallas_call(
        body, name=name, grid=(ntile,),
        in_specs=[at] * 9 + [
            pl.BlockSpec((tm, INC), lambda i: (i, 0)),
            pl.BlockSpec((SUBLANES, 3 * CW), lambda i: (jnp.maximum(i * hb - 1, 0), 1)),
            pl.BlockSpec((SUBLANES, 3 * CW), lambda i: (jnp.minimum((i + 1) * hb, nsl - 1), 1)),
            pl.BlockSpec((tm, CW), lambda i: (i, 1)),
            pl.BlockSpec((SUBLANES, CW), lambda i: (jnp.minimum((i + 1) * hb, nsl - 1), 1)),
            pl.BlockSpec((SUBLANES, AW), lambda i: (0, 0)),
            pl.BlockSpec((SUBLANES, CW), lambda i: (0, 0))],
        out_specs=[pl.BlockSpec((tm, INC), lambda i: (i, 0)),
                   pl.BlockSpec((SUBLANES, AW), lambda i: (0, 0))],
        out_shape=[jax.ShapeDtypeStruct((S, INC), MXU_DTYPE), jax.ShapeDtypeStruct((SUBLANES, AW), F32)],
        compiler_params=_params(("arbitrary",)),
    )(*dqs, *dks, *dvs, proj, proj, proj, dycat, dycat, gvec, cw)


def mixer_in_bwd(dxo, x, dproj, vec, winp, name):
    S = x.shape[0]
    tm = _row_tile(S, 512)
    pc = INC // NCHIP

    def body(dxo_ref, x_ref, dp_ref, vec_ref, w_ref, dxi_ref, sums_ref):
        xhat, r, gain, ng, sc, _, _ = _ada(x_ref[...], vec_ref)
        dh = jnp.zeros((tm, D), F32)
        for j in range(NCHIP):
            dh = dh + lax.dot_general(dp_ref[:, j * pc:(j + 1) * pc], w_ref[j], NT_DIMS, preferred_element_type=F32)
        dx, dshift, dscale, dng = _ada_bwd(dh, xhat, r, gain, ng, sc)
        dxi_ref[...] = dxo_ref[...] + dx
        _acc_rows(sums_ref, pl.program_id(0) == 0, (dshift, dscale, dng))

    t = pl.BlockSpec((tm, D), lambda i: (i, 0))
    return pl.pallas_call(
        body, name=name, grid=(S // tm,),
        in_specs=[t, t, pl.BlockSpec((tm, INC), lambda i: (i, 0)),
                  pl.BlockSpec((SUBLANES, D), lambda i: (0, 0)),
                  pl.BlockSpec((NCHIP, D, pc), lambda i: (0, 0, 0), pipeline_mode=pl.Buffered(1))],
        out_specs=[t, pl.BlockSpec((SUBLANES, D), lambda i: (0, 0))],
        out_shape=[jax.ShapeDtypeStruct((S, D), F32), jax.ShapeDtypeStruct((SUBLANES, D), F32)],
        compiler_params=_params(("arbitrary",)),
    )(dxo, x, dproj, vec, winp)


def loss_head(xf, target, name):
    S = xf.shape[0]
    tm = _row_tile(S, 1024)

    def body(x_ref, t_ref, dy_ref, l_ref):
        diff = x_ref[...] - t_ref[...]
        dy_ref[...] = diff * (1.0 / D)
        part = jnp.sum(jnp.sum(diff * diff, axis=0, keepdims=True), axis=1, keepdims=True) * (0.5 / D)

        @pl.when(pl.program_id(0) == 0)
        def _():
            l_ref[...] = jnp.zeros_like(l_ref)
        l_ref[...] += jnp.broadcast_to(part, l_ref.shape)

    t = pl.BlockSpec((tm, D), lambda i: (i, 0))
    return pl.pallas_call(
        body, name=name, grid=(S // tm,),
        in_specs=[t, t],
        out_specs=[t, pl.BlockSpec((SUBLANES, LANES), lambda i: (0, 0))],
        out_shape=[jax.ShapeDtypeStruct((S, D), F32), jax.ShapeDtypeStruct((SUBLANES, LANES), F32)],
        compiler_params=_params(("arbitrary",)),
    )(xf, target)


def _vec(mod_l, ng_l, i):
    m = mod_l.reshape(3, 3, D)
    rows = jnp.stack([ng_l[i], m[i, 1], m[i, 0], m[i, 2]])
    return jnp.concatenate([rows, jnp.zeros((SUBLANES - 4, D), F32)], axis=0)


def local_step(x, target, mods, ngs, gvecs, cws, weights):
    saved = []
    h = x
    for l in range(2):
        w = weights[l]
        vecs = [_vec(mods[l], ngs[l], i) for i in range(3)]
        x0 = h
        x1, a0, f0 = ffn_fwd(x0, vecs[0], w["w1"][0], w["w2"][0], 0.5, f"ffn_fwd_l{l}a")
        proj, h1b, qn, kn, v = mixer_in(x1, vecs[1], w["win"], gvecs[l], f"mixer_in_l{l}")
        os_, lses = [], []
        for d in DILATIONS:
            o, lse = attn_fwd(qn, kn, v, d, f"attn_fwd_l{l}_d{d}")
            os_.append(o)
            lses.append(lse)
        ycat, lse = combine_conv(os_, lses, proj, cws[l], f"combine_conv_l{l}")
        x2, y = out_proj(ycat, x1, vecs[1], w["wout"], f"out_proj_l{l}")
        x3, a2, f2 = ffn_fwd(x2, vecs[2], w["w1"][1], w["w2"][1], 0.5, f"ffn_fwd_l{l}b")
        saved.append(dict(vecs=vecs, x0=x0, a0=a0, f0=f0, x1=x1, proj=proj, h1b=h1b, qn=qn, kn=kn, v=v,
                          ycat=ycat, lse=lse, y=y, x2=x2, a2=a2, f2=f2))
        h = x3
    dx, loss_blk = loss_head(h, target, "loss_head")
    grads = [None, None]
    for l in (1, 0):
        w, s = weights[l], saved[l]
        vecs = s["vecs"]
        dx, hb, dfb, act, da, sums2 = ffn_bwd(dx, s["x2"], s["a2"], s["f2"], vecs[2], w["w1"][1], w["w2"][1],
                                              0.5, f"ffn_bwd_l{l}b")
        dw1b = wgrad(hb, da, D, HALF, f"wgrad_w1_l{l}b")
        dw2b = wgrad(act, dfb, HALF, D, f"wgrad_w2_l{l}b")
        dyb, dycat, delta, sums_o = out_proj_bwd(dx, s["y"], s["ycat"], vecs[1], w["wout"], f"out_proj_bwd_l{l}")
        dwout = wgrad(s["ycat"].astype(MXU_DTYPE), dyb, D // 2, D, f"wgrad_wout_l{l}")
        dqs, dks, dvs = [], [], []
        for d in DILATIONS:
            dq, dk, dv = attn_bwd(s["qn"], s["kn"], s["v"], dycat, s["lse"], delta, d, f"attn_bwd_l{l}_d{d}")
            dqs.append(dq)
            dks.append(dk)
            dvs.append(dv)
        dproj, sums_m = mixer_mid_bwd(dqs, dks, dvs, s["proj"], dycat, gvecs[l], cws[l], f"mixer_mid_bwd_l{l}")
        dwin = wgrad(s["h1b"], dproj, D, INC // NCHIP, f"wgrad_win_l{l}")
        dx, sums1 = mixer_in_bwd(dx, s["x1"], dproj, vecs[1], w["win"], f"mixer_in_bwd_l{l}")
        dx, hb, dfb, act, da, sums0 = ffn_bwd(dx, s["x0"], s["a0"], s["f0"], vecs[0], w["w1"][0], w["w2"][0],
                                              0.5, f"ffn_bwd_l{l}a")
        dw1a = wgrad(hb, da, D, HALF, f"wgrad_w1_l{l}a")
        dw2a = wgrad(act, dfb, HALF, D, f"wgrad_w2_l{l}a")
        grads[l] = dict(w1=[dw1a, dw1b], w2=[dw2a, dw2b], win=dwin, wout=dwout,
                        sums=(sums0, sums1, sums_o, sums2, sums_m))
    return loss_blk, dx, grads


MESH = pl.DeviceIdType.MESH
ANY = pl.BlockSpec(memory_space=pl.ANY)


def _here():
    return lax.axis_index("x"), lax.axis_index("y"), lax.axis_index("c")


def small_all_gather(blk, name):
    m_per, n = blk.shape

    def body(x_ref, out_ref, send_sems, recv_sems, local_sem):
        x, y, c = _here()
        me, sibling = (x, y, c), (x, y, 1 - c)
        chips = [(1 - x, y), (x, 1 - y), (1 - x, 1 - y)]

        def rows(px, py, pc):
            return out_ref.at[pl.ds((4 * px + 2 * py + pc) * m_per, m_per), :]

        def copy(k, block, to, src=None):
            return pltpu.make_async_remote_copy(
                src_ref=rows(*block) if src is None else src, dst_ref=rows(*block),
                send_sem=send_sems.at[k], recv_sem=recv_sems.at[k], device_id=to, device_id_type=MESH)

        mine = pltpu.make_async_copy(x_ref, rows(*me), local_sem)
        mine.start()
        first = [copy(0, me, sibling, src=x_ref)]
        first += [copy(1 + j, me, (*chip, c), src=x_ref) for j, chip in enumerate(chips)]
        for cp in first:
            cp.start()
        passed = [copy(4 + j, (*chip, c), sibling) for j, chip in enumerate(chips)]
        for j, chip in enumerate(chips):
            copy(1 + j, (*chip, c), me).wait_recv()
            passed[j].start()
        copy(0, sibling, me).wait_recv()
        for j, chip in enumerate(chips):
            copy(4 + j, (*chip, 1 - c), me).wait_recv()
        for cp in first + passed:
            cp.wait_send()
        mine.wait()

    return pl.pallas_call(
        body, name=name,
        out_shape=jax.ShapeDtypeStruct((NDEV * m_per, n), blk.dtype),
        in_specs=[pl.BlockSpec(memory_space=pltpu.VMEM)],
        out_specs=pl.BlockSpec(memory_space=pltpu.VMEM),
        scratch_shapes=[pltpu.SemaphoreType.DMA((7,)), pltpu.SemaphoreType.DMA((7,)), pltpu.SemaphoreType.DMA],
        compiler_params=pltpu.CompilerParams(vmem_limit_bytes=VMEM_LIMIT),
    )(blk)


def ici_exchange(srcs, scatter, name):
    n = len(srcs)

    def body(*refs):
        src_refs, dst_refs = refs[:n], refs[n:2 * n]
        send_sems, recv_sems, local_sems = refs[2 * n:]
        x, y, c = _here()
        my_chip = 2 * x + y
        peers = [(1 - x, y), (x, 1 - y), (1 - x, 1 - y)]
        copies = []
        for a in range(n):
            own = src_refs[a].at[my_chip] if scatter else src_refs[a]
            loc = pltpu.make_async_copy(own, dst_refs[a].at[my_chip], local_sems.at[a])
            loc.start()
            copies.append(loc)
        remote = []
        for a in range(n):
            for j, (px, py) in enumerate(peers):
                src = src_refs[a].at[2 * px + py] if scatter else src_refs[a]
                cp = pltpu.make_async_remote_copy(
                    src_ref=src, dst_ref=dst_refs[a].at[my_chip],
                    send_sem=send_sems.at[3 * a + j], recv_sem=recv_sems.at[3 * a + j],
                    device_id=(px, py, c), device_id_type=MESH)
                cp.start()
                remote.append(cp)
        for a in range(n):
            for j, (px, py) in enumerate(peers):
                src = src_refs[a].at[my_chip] if scatter else src_refs[a]
                pltpu.make_async_remote_copy(
                    src_ref=src, dst_ref=dst_refs[a].at[2 * px + py],
                    send_sem=send_sems.at[3 * a + j], recv_sem=recv_sems.at[3 * a + j],
                    device_id=(px, py, c), device_id_type=MESH).wait_recv()
        for cp in remote:
            cp.wait_send()
        for cp in copies:
            cp.wait()

    out_shape = [jax.ShapeDtypeStruct(s.shape if scatter else (NCHIP,) + s.shape, s.dtype) for s in srcs]
    return pl.pallas_call(
        body, name=name, out_shape=out_shape,
        in_specs=[ANY] * n, out_specs=[ANY] * n,
        scratch_shapes=[pltpu.SemaphoreType.DMA((3 * n,)), pltpu.SemaphoreType.DMA((3 * n,)),
                        pltpu.SemaphoreType.DMA((n,))],
    )(*srcs)


def d2d_swap(send_if_c0, send_if_c1, name):
    n = len(send_if_c0)

    def body(*refs):
        a_refs, b_refs, dst_refs = refs[:n], refs[n:2 * n], refs[2 * n:3 * n]
        send_sems, recv_sems = refs[3 * n:]
        x, y, c = _here()

        def copy(k, src):
            return pltpu.make_async_remote_copy(
                src_ref=src, dst_ref=dst_refs[k], send_sem=send_sems.at[k], recv_sem=recv_sems.at[k],
                device_id=(x, y, 1 - c), device_id_type=MESH)

        @pl.when(c == 0)
        def _():
            for k in range(n):
                copy(k, a_refs[k]).start()

        @pl.when(c == 1)
        def _():
            for k in range(n):
                copy(k, b_refs[k]).start()

        for k in range(n):
            copy(k, a_refs[k]).wait_recv()
        for k in range(n):
            copy(k, a_refs[k]).wait_send()

    return pl.pallas_call(
        body, name=name, out_shape=[jax.ShapeDtypeStruct(s.shape, s.dtype) for s in send_if_c0],
        in_specs=[ANY] * (2 * n), out_specs=[ANY] * n,
        scratch_shapes=[pltpu.SemaphoreType.DMA((n,)), pltpu.SemaphoreType.DMA((n,))],
    )(*send_if_c0, *send_if_c1)


EW_BLOCK_BYTES = 1 << 20


def _ew_rows(rows, cols):
    want = max(16, EW_BLOCK_BYTES // (4 * cols))
    best = None
    for t in range(16, rows + 1, 16):
        if rows % t == 0 and t <= want:
            best = t
    return best if best is not None else rows


def add_select(g0, g1, recv, cflag, name):
    shape = g0.shape
    cols = shape[-1]
    rows = g0.size // cols
    tr = _ew_rows(rows, cols)

    def body(c_ref, a_ref, b_ref, r_ref, o_ref):
        mine = jnp.where(c_ref[0] == 0, a_ref[...], b_ref[...])
        o_ref[...] = (mine + r_ref[...]).astype(o_ref.dtype)

    t = pl.BlockSpec((tr, cols), lambda i: (i, 0))
    out = pl.pallas_call(
        body, name=name, grid=(rows // tr,),
        in_specs=[pl.BlockSpec(memory_space=pltpu.SMEM), t, t, t], out_specs=t,
        out_shape=jax.ShapeDtypeStruct((rows, cols), WIRE_DTYPE),
        compiler_params=_params(("arbitrary",)),
    )(cflag, g0.reshape(rows, cols), g1.reshape(rows, cols), recv.reshape(rows, cols))
    return out.reshape(shape)


def sum_chips(recv, name):
    _, r, cols = recv.shape
    tr = _ew_rows(r, cols)

    def body(r_ref, o_ref):
        acc = r_ref[0].astype(F32)
        for k in range(1, NCHIP):
            acc = acc + r_ref[k].astype(F32)
        o_ref[...] = acc

    return pl.pallas_call(
        body, name=name, grid=(r // tr,),
        in_specs=[pl.BlockSpec((NCHIP, tr, cols), lambda i: (0, i, 0))],
        out_specs=pl.BlockSpec((tr, cols), lambda i: (i, 0)),
        out_shape=jax.ShapeDtypeStruct((r, cols), F32),
        compiler_params=_params(("arbitrary",)),
    )(recv)


def sum_devices(rows8, name):
    def body(r_ref, o_ref):
        acc = r_ref[0:1, :]
        for k in range(1, NDEV):
            acc = acc + r_ref[k:k + 1, :]
        o_ref[...] = jnp.broadcast_to(acc, o_ref.shape)

    return pl.pallas_call(
        body, name=name, out_shape=jax.ShapeDtypeStruct(rows8.shape, F32),
        in_specs=[pl.BlockSpec(memory_space=pltpu.VMEM)], out_specs=pl.BlockSpec(memory_space=pltpu.VMEM),
        compiler_params=pltpu.CompilerParams(vmem_limit_bytes=VMEM_LIMIT),
    )(rows8)


def adamw(w, m, v, srcs, table, cflag, name):
    planes, r, cols = w.shape
    tr = _ew_rows(r, cols)
    ns = len(srcs)

    def body(c_ref, w_ref, m_ref, v_ref, *rest):
        s_refs, (g_ref, d_ref, mo_ref, vo_ref) = rest[:ns], rest[ns:]
        p = pl.program_id(0)
        on_c0 = c_ref[0] == 0
        want = jnp.int32(0)
        for pp, (t0, t1) in enumerate(table):
            want = jnp.where(p == pp, jnp.where(on_c0, t0, t1), want)
        g = s_refs[0][...]
        for k in range(1, ns):
            g = jnp.where(want == k, s_refs[k][...], g)
        g_ref[...] = g
        m_new = ADAM_B1 * m_ref[...] + (1.0 - ADAM_B1) * g
        v_new = ADAM_B2 * v_ref[...] + (1.0 - ADAM_B2) * (g * g)
        mo_ref[...] = m_new
        vo_ref[...] = v_new
        m_hat = m_new / (1.0 - ADAM_B1 ** ADAM_STEP)
        v_hat = v_new / (1.0 - ADAM_B2 ** ADAM_STEP)
        d_ref[...] = -ADAM_LR * (m_hat / (jnp.sqrt(v_hat) + ADAM_EPS) + ADAM_WD * w_ref[...])

    pt = pl.BlockSpec((None, tr, cols), lambda p, i: (p, i, 0))
    st = pl.BlockSpec((tr, cols), lambda p, i: (i, 0))
    return pl.pallas_call(
        body, name=name, grid=(planes, r // tr),
        in_specs=[pl.BlockSpec(memory_space=pltpu.SMEM), pt, pt, pt] + [st] * ns,
        out_specs=[pt] * 4,
        out_shape=[jax.ShapeDtypeStruct(w.shape, F32)] * 4,
        compiler_params=_params(("arbitrary", "arbitrary")),
    )(cflag, w, m, v, *srcs)


ADA_COLS = 9 * D // NCHIP


def mod_fwd(c_all, w_ada, b_shard, name):
    def body(c_ref, w_ref, b_ref, o_ref):
        cc = c_ref[...]
        sc = cc * jax.nn.sigmoid(cc)
        o_ref[...] = jnp.dot(sc, w_ref[...], preferred_element_type=F32,
                             precision=lax.Precision.HIGHEST) + b_ref[...]

    return pl.pallas_call(
        body, name=name, grid=(2,),
        in_specs=[pl.BlockSpec((NDEV, D), lambda l: (0, 0)),
                  pl.BlockSpec((None, D, ADA_COLS), lambda l: (l, 0, 0)),
                  pl.BlockSpec((None, 1, ADA_COLS), lambda l: (l, 0, 0))],
        out_specs=pl.BlockSpec((None, NDEV, ADA_COLS), lambda l: (l, 0, 0)),
        out_shape=jax.ShapeDtypeStruct((2, NDEV, ADA_COLS), F32),
        compiler_params=_params(("arbitrary",)),
    )(c_all, w_ada, b_shard.reshape(2, 1, ADA_COLS))


def wada_grad(c_all_t, dmod, name):
    ct = ADA_COLS // 3

    def body(c_ref, d_ref, o_ref):
        cc = c_ref[...]
        sc = cc * jax.nn.sigmoid(cc)
        acc = sc[:, 0:1] * d_ref[0:1, :]
        for b in range(1, NDEV):
            acc = acc + sc[:, b:b + 1] * d_ref[b:b + 1, :]
        o_ref[...] = acc

    return pl.pallas_call(
        body, name=name, grid=(2, 3),
        in_specs=[pl.BlockSpec((D, LANES), lambda l, j: (0, 0)),
                  pl.BlockSpec((None, NDEV, ct), lambda l, j: (l, 0, j))],
        out_specs=pl.BlockSpec((None, D, ct), lambda l, j: (l, 0, j)),
        out_shape=jax.ShapeDtypeStruct((2, D, ADA_COLS), F32),
        compiler_params=_params(("arbitrary", "arbitrary")),
    )(c_all_t, dmod)


def _pad_rows(row, rows=SUBLANES):
    return jnp.concatenate([row[None, :], jnp.zeros((rows - 1, row.shape[0]), row.dtype)], axis=0)


def kernel(x, c, w_ada, b_ada, norm_g, w_in, q_norm_g, k_norm_g, conv_w, conv_b, w_out, ffn_w1, ffn_w2, loss_target, m_w_ada, m_b_ada, m_norm_g, m_w_in, m_q_norm_g, m_k_norm_g, m_conv_w, m_conv_b, m_w_out, m_ffn_w1, m_ffn_w2, v_w_ada, v_b_ada, v_norm_g, v_w_in, v_q_norm_g, v_k_norm_g, v_conv_w, v_conv_b, v_w_out, v_ffn_w1, v_ffn_w2):
    ix, iy, ic = lax.axis_index("x"), lax.axis_index("y"), lax.axis_index("c")
    chip = 2 * ix + iy
    dev = 2 * chip + ic
    cflag = jnp.reshape(ic, (1,)).astype(jnp.int32)
    ngw = norm_g.shape[-1]
    cww = conv_w.shape[-1]

    pack = jnp.concatenate([c[0], norm_g.reshape(-1), conv_w.reshape(-1)])
    got = small_all_gather(_pad_rows(pack), "gather_c_normg_convw")[::SUBLANES]
    c_all = got[:, :D]
    per_chip = got[::2]
    ng_full = jnp.concatenate([per_chip[j, D:D + 6 * ngw].reshape(2, 3, ngw) for j in range(NCHIP)], axis=-1)
    cw_full = jnp.concatenate([per_chip[j, D + 6 * ngw:].reshape(2, 3, cww) for j in range(NCHIP)], axis=-1)

    b_shard = lax.dynamic_slice_in_dim(b_ada, chip * ADA_COLS, ADA_COLS, axis=1)
    mod_blk = mod_fwd(c_all, w_ada, b_shard, "mod_fwd").reshape(2 * NDEV, ADA_COLS)
    mod_all = small_all_gather(mod_blk, "gather_mod").reshape(NDEV, 2, NDEV, ADA_COLS)[::2]
    mod_mine = lax.dynamic_index_in_dim(mod_all, dev, axis=2, keepdims=False)
    mods = [mod_mine[:, l, :].reshape(-1) for l in range(2)]

    shards = []
    for l in range(2):
        shards += [ffn_w1[l, 0], ffn_w2[l, 0], w_in[l], w_out[l], ffn_w1[l, 1], ffn_w2[l, 1]]
    full = ici_exchange([s.astype(MXU_DTYPE) for s in shards], False, "gather_weights")
    weights, gvecs, cws = [], [], []
    for l in range(2):
        w1a, w2a, win, wout, w1b, w2b = full[6 * l:6 * l + 6]
        weights.append(dict(w1=[w1a, w1b], w2=[w2a.reshape(DFF, D), w2b.reshape(DFF, D)],
                            win=win, wout=wout.reshape(D, D)))
        gv = jnp.stack([jnp.tile(q_norm_g[l], AW // HD), jnp.tile(k_norm_g[l], AW // HD)])
        gvecs.append(jnp.concatenate([gv, jnp.zeros((SUBLANES - 2, AW), F32)], axis=0))
        cws.append(jnp.concatenate([cw_full[l], conv_b[l][None, :], jnp.zeros((SUBLANES - 4, CW), F32)], axis=0))

    loss_blk, dx, grads = local_step(x[0], loss_target[0], mods, [ng_full[0], ng_full[1]], gvecs, cws, weights)
    loss = lax.psum(loss_blk[0, 0], ("x", "y", "c"))

    def glist(l):
        g = grads[l]
        return [g["w1"][0], g["w2"][0].reshape(NCHIP, DFF // NCHIP, D), g["win"],
                g["wout"].reshape(NCHIP, D // NCHIP, D), g["w1"][1], g["w2"][1].reshape(NCHIP, DFF // NCHIP, D)]

    g0, g1 = glist(0), glist(1)
    from_sib = d2d_swap(g1, g0, "swap_layer_grads")
    wire = [add_select(g0[k], g1[k], from_sib[k], cflag, f"add_sibling_{k}") for k in range(6)]
    landed = ici_exchange(wire, True, "scatter_grads")
    tot_mine = [sum_chips(landed[k], f"sum_chips_{k}") for k in range(6)]
    tot_sib = d2d_swap(tot_mine, tot_mine, "swap_totals")

    dmods, dngs, dqg, dkg, dcw, dcb = [], [], [], [], [], []
    for l in range(2):
        s0, s1, so, s2, sm = grads[l]["sums"]
        dmods.append(jnp.concatenate([s0[0], s0[1], s0[3], s1[0], s1[1], so[0], s2[0], s2[1], s2[3]]))
        dngs.append(jnp.concatenate([s0[2], s1[2], s2[2]]))
        dqg.append(sm[0].reshape(AW // HD, HD).sum(0))
        dkg.append(sm[1].reshape(AW // HD, HD).sum(0))
        dcw.append(sm[2:5].reshape(-1))
        dcb.append(sm[5])
    small = jnp.concatenate(dmods + dngs + dqg + dkg + dcw + dcb)
    small_all = small_all_gather(_pad_rows(small), "gather_small_grads")[::SUBLANES]
    nm = 9 * D
    dmod_all = small_all[:, :2 * nm].reshape(NDEV, 2, NCHIP, ADA_COLS)
    dmod_mine = lax.dynamic_index_in_dim(dmod_all, chip, axis=2, keepdims=False).transpose(1, 0, 2)
    tot = sum_devices(small_all, "sum_small_grads")[0]
    o = 2 * nm
    g_b_ada = tot[:o].reshape(2, nm)
    g_norm_g = lax.dynamic_slice_in_dim(tot[o:o + 6 * D].reshape(2, 3, D), chip * ngw, ngw, axis=2)
    o += 6 * D
    g_qg = tot[o:o + 2 * HD].reshape(2, HD)
    o += 2 * HD
    g_kg = tot[o:o + 2 * HD].reshape(2, HD)
    o += 2 * HD
    g_cw = lax.dynamic_slice_in_dim(tot[o:o + 6 * CW].reshape(2, 3, CW), chip * cww, cww, axis=2)
    o += 6 * CW
    g_cb = tot[o:o + 2 * CW].reshape(2, CW)

    c_all_t = jnp.concatenate([c_all.T, jnp.zeros((D, LANES - NDEV), F32)], axis=1)
    g_wada_src = wada_grad(c_all_t, dmod_mine, "wada_grad")

    same = [(0, 0), (1, 1)]
    by_layer = [(0, 1), (1, 0)]
    by_layer2 = [(0, 2), (1, 3), (2, 0), (3, 1)]
    r_wada = adamw(w_ada, m_w_ada, v_w_ada, [g_wada_src[0], g_wada_src[1]], same, cflag, "adamw_w_ada")
    r_win = adamw(w_in, m_w_in, v_w_in, [tot_mine[2], tot_sib[2]], by_layer, cflag, "adamw_w_in")
    r_wout = adamw(w_out, m_w_out, v_w_out, [tot_mine[3], tot_sib[3]], by_layer, cflag, "adamw_w_out")
    r_w1 = adamw(ffn_w1.reshape(4, D, HALF), m_ffn_w1.reshape(4, D, HALF), v_ffn_w1.reshape(4, D, HALF),
                 [tot_mine[0], tot_mine[4], tot_sib[0], tot_sib[4]], by_layer2, cflag, "adamw_ffn_w1")
    w2r = DFF // NCHIP
    r_w2 = adamw(ffn_w2.reshape(4, w2r, D), m_ffn_w2.reshape(4, w2r, D), v_ffn_w2.reshape(4, w2r, D),
                 [tot_mine[1], tot_mine[5], tot_sib[1], tot_sib[5]], by_layer2, cflag, "adamw_ffn_w2")
    r_w1 = [t.reshape(ffn_w1.shape) for t in r_w1]
    r_w2 = [t.reshape(ffn_w2.shape) for t in r_w2]

    smalls = [("b_ada", b_ada, m_b_ada, v_b_ada, g_b_ada), ("norm_g", norm_g, m_norm_g, v_norm_g, g_norm_g),
              ("q_norm_g", q_norm_g, m_q_norm_g, v_q_norm_g, g_qg), ("k_norm_g", k_norm_g, m_k_norm_g, v_k_norm_g, g_kg),
              ("conv_w", conv_w, m_conv_w, v_conv_w, g_cw), ("conv_b", conv_b, m_conv_b, v_conv_b, g_cb)]
    n_small = sum(t[1].size for t in smalls)
    pad = (-n_small) % (16 * LANES)

    def packed(idx):
        flat = jnp.concatenate([t[idx].reshape(-1) for t in smalls] + [jnp.zeros((pad,), F32)])
        return flat.reshape(-1, LANES)

    r_small = adamw(packed(1)[None], packed(2)[None], packed(3)[None], [packed(4)], [(0, 0)], cflag, "adamw_small")
    small_out = {}
    o = 0
    for name_, w_, _, _, _ in smalls:
        small_out[name_] = [t.reshape(-1)[o:o + w_.size].reshape(w_.shape) for t in r_small]
        o += w_.size

    res = {"w_ada": r_wada, "w_in": r_win, "w_out": r_wout, "ffn_w1": r_w1, "ffn_w2": r_w2, **small_out}
    order = ["w_ada", "b_ada", "norm_g", "w_in", "q_norm_g", "k_norm_g", "conv_w", "conv_b", "w_out", "ffn_w1", "ffn_w2"]
    outs = [loss, dx[None]]
    for k in range(4):
        outs += [res[nm_][k] for nm_ in order]
    return tuple(outs)
```

```python
import functools

import jax
import jax.numpy as jnp
from jax import lax
from jax.experimental import pallas as pl
from jax.experimental.pallas import tpu as pltpu

F32 = jnp.float32
MXU_DTYPE = jnp.bfloat16
ACT_DTYPE = jnp.bfloat16
WIRE_DTYPE = jnp.bfloat16

D = 1024
HD = 64
AW = 512
CW = 512
DFF = 2816
HALF = DFF // 2
INC = 3 * AW + 3 * CW
NCHIP = 4
NDEV = 8
QBLK = 128
ATTN_CHUNK_ROWS = 2048
DILATIONS = (1, 4, 16)
EPS = 1e-6
NEG = -1e30
LANES = 128
SUBLANES = 8
VMEM_LIMIT = 56 * 1024 * 1024

ADAM_LR = 0.001
ADAM_B1 = 0.9
ADAM_B2 = 0.999
ADAM_EPS = 1e-08
ADAM_WD = 0.01
ADAM_STEP = 10

NT_DIMS = (((1,), (1,)), ((), ()))
TN_DIMS = (((0,), (0,)), ((), ()))


def _params(sem, vmem=VMEM_LIMIT):
    return pltpu.CompilerParams(dimension_semantics=sem, vmem_limit_bytes=vmem)


def _row_tile(n, want):
    t = min(n, want)
    assert n % t == 0
    return t


def _ada(xt, vec_ref):
    ng, sc, sh, gt = vec_ref[0:1, :], vec_ref[1:2, :], vec_ref[2:3, :], vec_ref[3:4, :]
    r = lax.rsqrt(jnp.mean(xt * xt, axis=-1, keepdims=True) + EPS)
    return xt * r, r, ng * (1.0 + sc), ng, sc, sh, gt


def _ada_bwd(dh, xhat, r, gain, ng, sc):
    dshift = jnp.sum(dh, axis=0, keepdims=True)
    dhx = dh * xhat
    dscale = jnp.sum(dhx, axis=0, keepdims=True) * ng
    dng = jnp.sum(dhx, axis=0, keepdims=True) * (1.0 + sc)
    dxhat = dh * gain
    dx = r * (dxhat - xhat * jnp.mean(dxhat * xhat, axis=-1, keepdims=True))
    return dx, dshift, dscale, dng


def _acc_rows(sums_ref, first, rows):
    @pl.when(first)
    def _():
        sums_ref[...] = jnp.zeros_like(sums_ref)
    for k, row in enumerate(rows):
        sums_ref[k:k + 1, :] += row


def ffn_fwd(x, vec, w1p, w2, gs, name):
    S = x.shape[0]
    tm = _row_tile(S, 512)

    def body(x_ref, vec_ref, w1_ref, w2_ref, xn_ref, a_ref, f_ref):
        xt = x_ref[...]
        xhat, _, gain, _, _, sh, gt = _ada(xt, vec_ref)
        h = (xhat * gain + sh).astype(MXU_DTYPE)
        f = jnp.zeros((tm, D), F32)
        for hf in range(2):
            g = jnp.dot(h, w1_ref[hf], preferred_element_type=F32)
            up = jnp.dot(h, w1_ref[2 + hf], preferred_element_type=F32)
            a_ref[:, hf * HALF:(hf + 1) * HALF] = g.astype(a_ref.dtype)
            a_ref[:, DFF + hf * HALF:DFF + (hf + 1) * HALF] = up.astype(a_ref.dtype)
            act = (g * jax.nn.sigmoid(g) * up).astype(MXU_DTYPE)
            f = f + jnp.dot(act, w2_ref[hf * HALF:(hf + 1) * HALF, :], preferred_element_type=F32)
        xn_ref[...] = xt + (gs * gt) * f
        f_ref[...] = f.astype(f_ref.dtype)

    return pl.pallas_call(
        body, name=name, grid=(S // tm,),
        in_specs=[pl.BlockSpec((tm, D), lambda i: (i, 0)),
                  pl.BlockSpec((SUBLANES, D), lambda i: (0, 0)),
                  pl.BlockSpec((NCHIP, D, HALF), lambda i: (0, 0, 0), pipeline_mode=pl.Buffered(1)),
                  pl.BlockSpec((DFF, D), lambda i: (0, 0), pipeline_mode=pl.Buffered(1))],
        out_specs=[pl.BlockSpec((tm, D), lambda i: (i, 0)),
                   pl.BlockSpec((tm, 2 * DFF), lambda i: (i, 0)),
                   pl.BlockSpec((tm, D), lambda i: (i, 0))],
        out_shape=[jax.ShapeDtypeStruct((S, D), F32),
                   jax.ShapeDtypeStruct((S, 2 * DFF), ACT_DTYPE),
                   jax.ShapeDtypeStruct((S, D), ACT_DTYPE)],
        compiler_params=_params(("arbitrary",)),
    )(x, vec, w1p, w2)


def ffn_bwd(dxo, x, a, f, vec, w1p, w2, gs, name):
    S = x.shape[0]
    tm = _row_tile(S, 256)

    def body(dxo_ref, x_ref, a_ref, f_ref, vec_ref, w1_ref, w2_ref,
             dxi_ref, hb_ref, dfb_ref, act_ref, da_ref, sums_ref):
        xt = x_ref[...]
        dxo = dxo_ref[...]
        xhat, r, gain, ng, sc, sh, gt = _ada(xt, vec_ref)
        hb_ref[...] = (xhat * gain + sh).astype(hb_ref.dtype)
        dgate = gs * jnp.sum(dxo * f_ref[...].astype(F32), axis=0, keepdims=True)
        df = ((gs * gt) * dxo).astype(MXU_DTYPE)
        dfb_ref[...] = df
        dh = jnp.zeros((tm, D), F32)
        for hf in range(2):
            lo, hi = hf * HALF, (hf + 1) * HALF
            dact = lax.dot_general(df, w2_ref[lo:hi, :], NT_DIMS, preferred_element_type=F32)
            g = a_ref[:, lo:hi].astype(F32)
            up = a_ref[:, DFF + lo:DFF + hi].astype(F32)
            sg = jax.nn.sigmoid(g)
            si = g * sg
            act_ref[:, lo:hi] = (si * up).astype(act_ref.dtype)
            dg = (dact * up * (sg * (1.0 + g * (1.0 - sg)))).astype(MXU_DTYPE)
            dup = (dact * si).astype(MXU_DTYPE)
            da_ref[:, lo:hi] = dg
            da_ref[:, DFF + lo:DFF + hi] = dup
            dh = dh + lax.dot_general(dg, w1_ref[hf], NT_DIMS, preferred_element_type=F32)
            dh = dh + lax.dot_general(dup, w1_ref[2 + hf], NT_DIMS, preferred_element_type=F32)
        dx, dshift, dscale, dng = _ada_bwd(dh, xhat, r, gain, ng, sc)
        dxi_ref[...] = dxo + dx
        _acc_rows(sums_ref, pl.program_id(0) == 0, (dshift, dscale, dng, dgate))

    return pl.pallas_call(
        body, name=name, grid=(S // tm,),
        in_specs=[pl.BlockSpec((tm, D), lambda i: (i, 0)),
                  pl.BlockSpec((tm, D), lambda i: (i, 0)),
                  pl.BlockSpec((tm, 2 * DFF), lambda i: (i, 0)),
                  pl.BlockSpec((tm, D), lambda i: (i, 0)),
                  pl.BlockSpec((SUBLANES, D), lambda i: (0, 0)),
                  pl.BlockSpec((NCHIP, D, HALF), lambda i: (0, 0, 0), pipeline_mode=pl.Buffered(1)),
                  pl.BlockSpec((DFF, D), lambda i: (0, 0), pipeline_mode=pl.Buffered(1))],
        out_specs=[pl.BlockSpec((tm, D), lambda i: (i, 0)),
                   pl.BlockSpec((tm, D), lambda i: (i, 0)),
                   pl.BlockSpec((tm, D), lambda i: (i, 0)),
                   pl.BlockSpec((tm, DFF), lambda i: (i, 0)),
                   pl.BlockSpec((tm, 2 * DFF), lambda i: (i, 0)),
                   pl.BlockSpec((SUBLANES, D), lambda i: (0, 0))],
        out_shape=[jax.ShapeDtypeStruct((S, D), F32),
                   jax.ShapeDtypeStruct((S, D), MXU_DTYPE),
                   jax.ShapeDtypeStruct((S, D), MXU_DTYPE),
                   jax.ShapeDtypeStruct((S, DFF), MXU_DTYPE),
                   jax.ShapeDtypeStruct((S, 2 * DFF), MXU_DTYPE),
                   jax.ShapeDtypeStruct((SUBLANES, D), F32)],
        compiler_params=_params(("arbitrary",)),
    )(dxo, x, a, f, vec, w1p, w2)


def wgrad(a, b, kt, nt, name):
    T, K = a.shape
    N = b.shape[1]
    pk, pn = K // kt, N // nt
    assert pk == 1 or pn == 1
    tt = _row_tile(T, 1024)
    steps = T // tt

    def body(a_ref, b_ref, o_ref):
        @pl.when(pl.program_id(1) == 0)
        def _():
            o_ref[...] = jnp.zeros_like(o_ref)
        o_ref[...] += lax.dot_general(a_ref[...], b_ref[...], TN_DIMS, preferred_element_type=F32)

    a_map = (lambda p, t: (t, p)) if pk > 1 else (lambda p, t: (t, 0))
    b_map = (lambda p, t: (t, p)) if pn > 1 else (lambda p, t: (t, 0))
    return pl.pallas_call(
        body, name=name, grid=(pk * pn, steps),
        in_specs=[pl.BlockSpec((tt, kt), a_map), pl.BlockSpec((tt, nt), b_map)],
        out_specs=pl.BlockSpec((None, kt, nt), lambda p, t: (p, 0, 0)),
        out_shape=jax.ShapeDtypeStruct((pk * pn, kt, nt), F32),
        compiler_params=_params(("arbitrary", "arbitrary")),
    )(a, b)


def _head_masks(rows):
    lane = lax.broadcasted_iota(jnp.int32, (rows, LANES), 1)
    return lane < HD


def _pair_stat(x, m_a):
    s_a = jnp.sum(jnp.where(m_a, x, 0.0), axis=1, keepdims=True)
    s_b = jnp.sum(jnp.where(m_a, 0.0, x), axis=1, keepdims=True)
    return s_a, s_b


def mixer_in(x, vec, winp, gvec, name):
    S = x.shape[0]
    tm = _row_tile(S, 512)
    pc = INC // NCHIP

    def body(x_ref, vec_ref, w_ref, g_ref, proj_ref, hb_ref, qn_ref, kn_ref, v_ref):
        xt = x_ref[...]
        xhat, _, gain, _, _, sh, _ = _ada(xt, vec_ref)
        h = (xhat * gain + sh).astype(MXU_DTYPE)
        hb_ref[...] = h
        for j in range(NCHIP):
            proj_ref[:, j * pc:(j + 1) * pc] = jnp.dot(h, w_ref[j], preferred_element_type=F32)
        m_a = _head_masks(tm)
        for which, dst in ((0, qn_ref), (1, kn_ref)):
            for p in range(AW // LANES):
                lo = which * AW + p * LANES
                xp = proj_ref[:, lo:lo + LANES]
                s_a, s_b = _pair_stat(xp * xp, m_a)
                rr = jnp.where(m_a, lax.rsqrt(s_a * (1.0 / HD) + EPS), lax.rsqrt(s_b * (1.0 / HD) + EPS))
                gp = g_ref[which:which + 1, p * LANES:(p + 1) * LANES]
                dst[:, p * LANES:(p + 1) * LANES] = (xp * rr * gp).astype(dst.dtype)
        v_ref[...] = proj_ref[:, 2 * AW:3 * AW].astype(v_ref.dtype)

    return pl.pallas_call(
        body, name=name, grid=(S // tm,),
        in_specs=[pl.BlockSpec((tm, D), lambda i: (i, 0)),
                  pl.BlockSpec((SUBLANES, D), lambda i: (0, 0)),
                  pl.BlockSpec((NCHIP, D, pc), lambda i: (0, 0, 0), pipeline_mode=pl.Buffered(1)),
                  pl.BlockSpec((SUBLANES, AW), lambda i: (0, 0))],
        out_specs=[pl.BlockSpec((tm, INC), lambda i: (i, 0)),
                   pl.BlockSpec((tm, D), lambda i: (i, 0)),
                   pl.BlockSpec((tm, AW), lambda i: (i, 0)),
                   pl.BlockSpec((tm, AW), lambda i: (i, 0)),
                   pl.BlockSpec((tm, AW), lambda i: (i, 0))],
        out_shape=[jax.ShapeDtypeStruct((S, INC), F32),
                   jax.ShapeDtypeStruct((S, D), MXU_DTYPE),
                   jax.ShapeDtypeStruct((S, AW), F32),
                   jax.ShapeDtypeStruct((S, AW), F32),
                   jax.ShapeDtypeStruct((S, AW), F32)],
        compiler_params=_params(("arbitrary",)),
    )(x, vec, winp, gvec)


def _band_masks(ncol):
    row = lax.broadcasted_iota(jnp.int32, (QBLK, ncol), 0)
    col = lax.broadcasted_iota(jnp.int32, (QBLK, ncol), 1)
    return row, col


def _attn_qb(d):
    return max(1, min(4, ATTN_CHUNK_ROWS // (QBLK * d)))


def _tile_rows(d, b, r):
    if d == 1:
        return pl.ds(b * QBLK, QBLK)
    return pl.ds(b * QBLK * d + r, QBLK, stride=d)


def _per_residue(d, fn):
    if d == 1:
        fn(0)
    else:
        def step(r, carry):
            fn(r)
            return carry
        lax.fori_loop(0, d, step, 0)


def attn_fwd(qn, kn, v, d, name):
    S = qn.shape[0]
    qb = _attn_qb(d)
    halo = QBLK * d
    chunk = qb * halo

    def body(q_ref, kc_ref, kp_ref, vc_ref, vp_ref, o_ref, la_ref, lb_ref):
        i = pl.program_id(1)
        m_a = _head_masks(QBLK)
        row, col = _band_masks(2 * QBLK)
        dist = row + QBLK - col
        band = (dist >= 0) & (dist <= QBLK)
        first = band & ((i > 0) | (col >= QBLK))

        def residue(r):
            kt = [kp_ref[_tile_rows(d, 0, r), :].astype(MXU_DTYPE)]
            vt = [vp_ref[_tile_rows(d, 0, r), :].astype(MXU_DTYPE)]
            for b in range(qb):
                kt.append(kc_ref[_tile_rows(d, b, r), :].astype(MXU_DTYPE))
                vt.append(vc_ref[_tile_rows(d, b, r), :].astype(MXU_DTYPE))
            for b in range(qb):
                rows = _tile_rows(d, b, r)
                q = (q_ref[rows, :] * (HD ** -0.5)).astype(MXU_DTYPE)
                kcat = jnp.concatenate([kt[b], kt[b + 1]], axis=0)
                vcat = jnp.concatenate([vt[b], vt[b + 1]], axis=0)
                mask = first if b == 0 else band
                outs, lses = [], []
                for hm in (m_a, jnp.logical_not(m_a)):
                    qh = jnp.where(hm, q, jnp.zeros_like(q))
                    s = lax.dot_general(qh, kcat, NT_DIMS, preferred_element_type=F32)
                    s = jnp.where(mask, s, NEG)
                    m = jnp.max(s, axis=1, keepdims=True)
                    p = jnp.exp(s - m)
                    l = jnp.sum(p, axis=1, keepdims=True)
                    outs.append(jnp.dot(p.astype(MXU_DTYPE), vcat, preferred_element_type=F32) / l)
                    lses.append(m + jnp.log(l))
                o_ref[rows, :] = jnp.where(m_a, outs[0], outs[1])
                la_ref[rows, :] = jnp.broadcast_to(lses[0], (QBLK, LANES))
                lb_ref[rows, :] = jnp.broadcast_to(lses[1], (QBLK, LANES))

        _per_residue(d, residue)

    cur = pl.BlockSpec((chunk, LANES), lambda hp, i: (i, hp))
    prev = pl.BlockSpec((halo, LANES), lambda hp, i: (jnp.maximum(i * qb - 1, 0), hp))
    return pl.pallas_call(
        body, name=name, grid=(AW // LANES, S // chunk),
        in_specs=[cur, cur, prev, cur, prev],
        out_specs=[cur, cur, cur],
        out_shape=[jax.ShapeDtypeStruct((S, AW), F32)] * 3,
        compiler_params=_params(("arbitrary", "arbitrary")),
    )(qn, kn, kn, v, v)


def attn_bwd(qn, kn, v, dycat, lse_a, lse_b, dl_a, dl_b, d, name):
    S = qn.shape[0]
    qb = _attn_qb(d)
    halo = QBLK * d
    chunk = qb * halo
    nhalo = S // halo
    nchunk = S // chunk

    def body(q_ref, qx_ref, kc_ref, kp_ref, vc_ref, vp_ref, do_ref, dox_ref, la_ref, lax_ref, lb_ref, lbx_ref,
             da_ref, dax_ref, db_ref, dbx_ref, dq_ref, dk_ref, dv_ref):
        i = pl.program_id(1)
        has_next = i < nchunk - 1
        m_a = _head_masks(QBLK)
        row, col = _band_masks(2 * QBLK)
        dist = row + QBLK - col
        band = (dist >= 0) & (dist <= QBLK)
        first = band & ((i > 0) | (col >= QBLK))
        row1, col1 = _band_masks(QBLK)
        off_only = (col1 >= row1) & has_next

        def residue(r):
            def tiles(cur_ref, next_ref, cast):
                out = [cur_ref[_tile_rows(d, b, r), :] for b in range(qb)] + [next_ref[_tile_rows(d, 0, r), :]]
                return [t.astype(MXU_DTYPE) for t in out] if cast else out

            def ktiles(cur_ref, prev_ref):
                out = [prev_ref[_tile_rows(d, 0, r), :]] + [cur_ref[_tile_rows(d, b, r), :] for b in range(qb)]
                return [t.astype(MXU_DTYPE) for t in out]

            qt = [(t * (HD ** -0.5)).astype(MXU_DTYPE) for t in tiles(q_ref, qx_ref, False)]
            dot_ = tiles(do_ref, dox_ref, True)
            stats = [(tiles(la_ref, lax_ref, False), tiles(da_ref, dax_ref, False)),
                     (tiles(lb_ref, lbx_ref, False), tiles(db_ref, dbx_ref, False))]
            kt = ktiles(kc_ref, kp_ref)
            vt = ktiles(vc_ref, vp_ref)
            dk_acc = [jnp.zeros((QBLK, LANES), F32) for _ in range(qb)]
            dv_acc = [jnp.zeros((QBLK, LANES), F32) for _ in range(qb)]
            for x in range(qb + 1):
                parts = 2 if x < qb else 1
                if parts == 2:
                    kcat = jnp.concatenate([kt[x], kt[x + 1]], axis=0)
                    vcat = jnp.concatenate([vt[x], vt[x + 1]], axis=0)
                else:
                    kcat, vcat = kt[x], vt[x]
                mask = first if x == 0 else (band if x < qb else off_only)
                dq_heads = []
                for hd, hm in enumerate((m_a, jnp.logical_not(m_a))):
                    qh = jnp.where(hm, qt[x], jnp.zeros_like(qt[x]))
                    doh = jnp.where(hm, dot_[x], jnp.zeros_like(dot_[x]))
                    s = lax.dot_general(qh, kcat, NT_DIMS, preferred_element_type=F32)
                    s = jnp.where(mask, s, NEG)
                    lse_h, dl_h = stats[hd][0][x], stats[hd][1][x]
                    if parts == 2:
                        lse_h = jnp.concatenate([lse_h, lse_h], axis=1)
                        dl_h = jnp.concatenate([dl_h, dl_h], axis=1)
                    p = jnp.exp(s - lse_h)
                    dp = lax.dot_general(doh, vcat, NT_DIMS, preferred_element_type=F32)
                    ds = p * (dp - dl_h)
                    if x < qb:
                        dq_heads.append(jnp.dot(ds.astype(MXU_DTYPE), kcat, preferred_element_type=F32))
                    ds_t = ds.T.astype(MXU_DTYPE)
                    p_t = p.T.astype(MXU_DTYPE)
                    for part in range(parts):
                        kb = x - 1 + part
                        if 0 <= kb < qb:
                            sl = slice(part * QBLK, (part + 1) * QBLK)
                            dk_acc[kb] = dk_acc[kb] + jnp.dot(ds_t[sl], qh, preferred_element_type=F32)
                            dv_acc[kb] = dv_acc[kb] + jnp.dot(p_t[sl], doh, preferred_element_type=F32)
                if x < qb:
                    dq_ref[_tile_rows(d, x, r), :] = jnp.where(m_a, dq_heads[0], dq_heads[1]) * (HD ** -0.5)
            for kb in range(qb):
                dk_ref[_tile_rows(d, kb, r), :] = dk_acc[kb]
                dv_ref[_tile_rows(d, kb, r), :] = dv_acc[kb]

        _per_residue(d, residue)

    def nxt(i):
        return jnp.minimum((i + 1) * qb, nhalo - 1)

    cur = pl.BlockSpec((chunk, LANES), lambda hp, i: (i, hp))
    prev = pl.BlockSpec((halo, LANES), lambda hp, i: (jnp.maximum(i * qb - 1, 0), hp))
    nx = pl.BlockSpec((halo, LANES), lambda hp, i: (nxt(i), hp))
    return pl.pallas_call(
        body, name=name, grid=(AW // LANES, nchunk),
        in_specs=[cur, nx, cur, prev, cur, prev, cur, nx, cur, nx, cur, nx, cur, nx, cur, nx],
        out_specs=[cur, cur, cur],
        out_shape=[jax.ShapeDtypeStruct((S, AW), F32)] * 3,
        compiler_params=_params(("arbitrary", "arbitrary")),
    )(qn, qn, kn, kn, v, v, dycat, dycat, lse_a, lse_a, lse_b, lse_b, dl_a, dl_a, dl_b, dl_b)


def _shift_down(x, halo_prev, k, row):
    tm = x.shape[0]
    tail = jnp.concatenate([pltpu.roll(halo_prev, k, 0), jnp.zeros((tm - SUBLANES, x.shape[1]), x.dtype)], axis=0)
    return jnp.where(row < k, tail, pltpu.roll(x, k, 0))


def _shift_up(x, halo_next, k, row):
    tm = x.shape[0]
    head = jnp.concatenate([jnp.zeros((tm - SUBLANES, x.shape[1]), x.dtype), pltpu.roll(halo_next, SUBLANES - k, 0)], axis=0)
    return jnp.where(row >= tm - k, head, pltpu.roll(x, tm - k, 0))


def _conv_fwd(cu, halo_cu, cw_ref, row):
    u1 = _shift_down(cu, halo_cu, 1, row)
    u2 = _shift_down(cu, halo_cu, 2, row)
    cv = cw_ref[0:1, :] * u2 + cw_ref[1:2, :] * u1 + cw_ref[2:3, :] * cu + cw_ref[3:4, :]
    return cv, u1, u2


def combine_conv(os_, lses_a, lses_b, proj, cw, name):
    S = proj.shape[0]
    tm = _row_tile(S, 512)
    hb = tm // SUBLANES

    def body(o1, o2, o3, a1, a2, a3, b1, b2, b3, pc_ref, ph_ref, cw_ref, ycat_ref, la_ref, lb_ref):
        i = pl.program_id(0)
        m_a = _head_masks(tm)
        for p in range(AW // LANES):
            cs = slice(p * LANES, (p + 1) * LANES)
            tot = []
            for srcs, dst in (((a1, a2, a3), la_ref), ((b1, b2, b3), lb_ref)):
                ls = [l[:, cs] for l in srcs]
                mx = jnp.maximum(jnp.maximum(ls[0], ls[1]), ls[2])
                t = mx + jnp.log(jnp.exp(ls[0] - mx) + jnp.exp(ls[1] - mx) + jnp.exp(ls[2] - mx))
                dst[:, cs] = t
                tot.append((ls, t))
            acc = jnp.zeros((tm, LANES), F32)
            for r, o in enumerate((o1, o2, o3)):
                w = jnp.where(m_a, jnp.exp(tot[0][0][r] - tot[0][1]), jnp.exp(tot[1][0][r] - tot[1][1]))
                acc = acc + w * o[:, cs]
            ycat_ref[:, cs] = acc.astype(ycat_ref.dtype)
        row = lax.broadcasted_iota(jnp.int32, (tm, CW), 0)
        gb, gc, u = pc_ref[:, 0:CW], pc_ref[:, CW:2 * CW], pc_ref[:, 2 * CW:3 * CW]
        halo_cu = jnp.where(i > 0, ph_ref[:, CW:2 * CW] * ph_ref[:, 2 * CW:3 * CW], 0.0)
        cv, _, _ = _conv_fwd(gc * u, halo_cu, cw_ref, row)
        ycat_ref[:, AW:AW + CW] = (gb * cv).astype(ycat_ref.dtype)

    ot = pl.BlockSpec((tm, AW), lambda i: (i, 0))
    return pl.pallas_call(
        body, name=name, grid=(S // tm,),
        in_specs=[ot] * 9 + [
                  pl.BlockSpec((tm, 3 * CW), lambda i: (i, 1)),
                  pl.BlockSpec((SUBLANES, 3 * CW), lambda i: (jnp.maximum(i * hb - 1, 0), 1)),
                  pl.BlockSpec((SUBLANES, CW), lambda i: (0, 0))],
        out_specs=[pl.BlockSpec((tm, D), lambda i: (i, 0)), ot, ot],
        out_shape=[jax.ShapeDtypeStruct((S, D), ACT_DTYPE), jax.ShapeDtypeStruct((S, AW), F32),
                   jax.ShapeDtypeStruct((S, AW), F32)],
        compiler_params=_params(("arbitrary",)),
    )(*os_, *lses_a, *lses_b, proj, proj, cw)


def out_proj(ycat, x, vec, wout, name):
    S = x.shape[0]
    tm = _row_tile(S, 512)

    def body(yc_ref, x_ref, vec_ref, w_ref, xn_ref, y_ref):
        y = jnp.dot(yc_ref[...].astype(MXU_DTYPE), w_ref[...], preferred_element_type=F32)
        xn_ref[...] = x_ref[...] + vec_ref[3:4, :] * y
        y_ref[...] = y.astype(y_ref.dtype)

    t = pl.BlockSpec((tm, D), lambda i: (i, 0))
    return pl.pallas_call(
        body, name=name, grid=(S // tm,),
        in_specs=[t, t, pl.BlockSpec((SUBLANES, D), lambda i: (0, 0)),
                  pl.BlockSpec((D, D), lambda i: (0, 0))],
        out_specs=[t, t],
        out_shape=[jax.ShapeDtypeStruct((S, D), F32), jax.ShapeDtypeStruct((S, D), ACT_DTYPE)],
        compiler_params=_params(("arbitrary",)),
    )(ycat, x, vec, wout)


def out_proj_bwd(dxo, y, ycat, vec, wout, name):
    S = dxo.shape[0]
    tm = _row_tile(S, 512)

    def body(dxo_ref, y_ref, yc_ref, vec_ref, w_ref, dyb_ref, dyc_ref, da_ref, db_ref, sums_ref):
        dxo = dxo_ref[...]
        dgate = jnp.sum(dxo * y_ref[...].astype(F32), axis=0, keepdims=True)
        dy = (vec_ref[3:4, :] * dxo).astype(MXU_DTYPE)
        dyb_ref[...] = dy
        dyc_ref[...] = lax.dot_general(dy, w_ref[...], NT_DIMS, preferred_element_type=F32)
        m_a = _head_masks(tm)
        for p in range(AW // LANES):
            cs = slice(p * LANES, (p + 1) * LANES)
            s_a, s_b = _pair_stat(dyc_ref[:, cs] * yc_ref[:, cs].astype(F32), m_a)
            da_ref[:, cs] = jnp.broadcast_to(s_a, (tm, LANES))
            db_ref[:, cs] = jnp.broadcast_to(s_b, (tm, LANES))
        _acc_rows(sums_ref, pl.program_id(0) == 0, (dgate,))

    t = pl.BlockSpec((tm, D), lambda i: (i, 0))
    at = pl.BlockSpec((tm, AW), lambda i: (i, 0))
    return pl.pallas_call(
        body, name=name, grid=(S // tm,),
        in_specs=[t, t, t, pl.BlockSpec((SUBLANES, D), lambda i: (0, 0)),
                  pl.BlockSpec((D, D), lambda i: (0, 0))],
        out_specs=[t, t, at, at, pl.BlockSpec((SUBLANES, D), lambda i: (0, 0))],
        out_shape=[jax.ShapeDtypeStruct((S, D), MXU_DTYPE), jax.ShapeDtypeStruct((S, D), F32),
                   jax.ShapeDtypeStruct((S, AW), F32), jax.ShapeDtypeStruct((S, AW), F32),
                   jax.ShapeDtypeStruct((SUBLANES, D), F32)],
        compiler_params=_params(("arbitrary",)),
    )(dxo, y, ycat, vec, wout)


def mixer_mid_bwd(dqs, dks, dvs, proj, dycat, gvec, cw, name):
    S = proj.shape[0]
    tm = _row_tile(S, 256)
    hb = tm // SUBLANES
    nsl = S // SUBLANES
    ntile = S // tm

    def body(dq1, dq2, dq3, dk1, dk2, dk3, dv1, dv2, dv3, pr_ref, pp_ref, pn_ref, dyc_ref, dyn_ref,
             g_ref, cw_ref, dp_ref, sums_ref):
        i = pl.program_id(0)
        m_a = _head_masks(tm)
        gsum = []
        for which, parts in ((0, (dq1, dq2, dq3)), (1, (dk1, dk2, dk3))):
            acc_g = []
            for p in range(AW // LANES):
                lo = which * AW + p * LANES
                cs = slice(p * LANES, (p + 1) * LANES)
                xp = pr_ref[:, lo:lo + LANES]
                s_a, s_b = _pair_stat(xp * xp, m_a)
                rr = jnp.where(m_a, lax.rsqrt(s_a * (1.0 / HD) + EPS), lax.rsqrt(s_b * (1.0 / HD) + EPS))
                xh = xp * rr
                dn = parts[0][:, cs] + parts[1][:, cs] + parts[2][:, cs]
                acc_g.append(jnp.sum(dn * xh, axis=0, keepdims=True))
                t = dn * g_ref[which:which + 1, cs]
                t_a, t_b = _pair_stat(t * xh, m_a)
                mean = jnp.where(m_a, t_a, t_b) * (1.0 / HD)
                dp_ref[:, lo:lo + LANES] = (rr * (t - xh * mean)).astype(dp_ref.dtype)
            gsum.append(jnp.concatenate(acc_g, axis=1))
        dp_ref[:, 2 * AW:3 * AW] = (dv1[...] + dv2[...] + dv3[...]).astype(dp_ref.dtype)
        row = lax.broadcasted_iota(jnp.int32, (tm, CW), 0)
        base = 3 * AW
        gb, gc, u = pr_ref[:, base:base + CW], pr_ref[:, base + CW:base + 2 * CW], pr_ref[:, base + 2 * CW:base + 3 * CW]
        cu = gc * u
        halo_cu = jnp.where(i > 0, pp_ref[:, CW:2 * CW] * pp_ref[:, 2 * CW:3 * CW], 0.0)
        cv, u1, u2 = _conv_fwd(cu, halo_cu, cw_ref, row)
        dyc = dyc_ref[...]
        dp_ref[:, base:base + CW] = (dyc * cv).astype(dp_ref.dtype)
        dcv = dyc * gb
        halo_dcv = jnp.where(i < ntile - 1, dyn_ref[...] * pn_ref[:, 0:CW], 0.0)
        d1 = _shift_up(dcv, halo_dcv, 1, row)
        d2 = _shift_up(dcv, halo_dcv, 2, row)
        dcu = cw_ref[2:3, :] * dcv + cw_ref[1:2, :] * d1 + cw_ref[0:1, :] * d2
        dp_ref[:, base + CW:base + 2 * CW] = (dcu * u).astype(dp_ref.dtype)
        dp_ref[:, base + 2 * CW:base + 3 * CW] = (dcu * gc).astype(dp_ref.dtype)
        rows = (gsum[0], gsum[1],
                jnp.sum(dcv * u2, axis=0, keepdims=True), jnp.sum(dcv * u1, axis=0, keepdims=True),
                jnp.sum(dcv * cu, axis=0, keepdims=True), jnp.sum(dcv, axis=0, keepdims=True))
        _acc_rows(sums_ref, i == 0, rows)

    at = pl.BlockSpec((tm, AW), lambda i: (i, 0))
    return pl.pallas_call(
        body, name=name, grid=(ntile,),
        in_specs=[at] * 9 + [
            pl.BlockSpec((tm, INC), lambda i: (i, 0)),
            pl.BlockSpec((SUBLANES, 3 * CW), lambda i: (jnp.maximum(i * hb - 1, 0), 1)),
            pl.BlockSpec((SUBLANES, 3 * CW), lambda i: (jnp.minimum((i + 1) * hb, nsl - 1), 1)),
            pl.BlockSpec((tm, CW), lambda i: (i, 1)),
            pl.BlockSpec((SUBLANES, CW), lambda i: (jnp.minimum((i + 1) * hb, nsl - 1), 1)),
            pl.BlockSpec((SUBLANES, AW), lambda i: (0, 0)),
            pl.BlockSpec((SUBLANES, CW), lambda i: (0, 0))],
        out_specs=[pl.BlockSpec((tm, INC), lambda i: (i, 0)),
                   pl.BlockSpec((SUBLANES, AW), lambda i: (0, 0))],
        out_shape=[jax.ShapeDtypeStruct((S, INC), MXU_DTYPE), jax.ShapeDtypeStruct((SUBLANES, AW), F32)],
        compiler_params=_params(("arbitrary",)),
    )(*dqs, *dks, *dvs, proj, proj, proj, dycat, dycat, gvec, cw)


def mixer_in_bwd(dxo, x, dproj, vec, winp, name):
    S = x.shape[0]
    tm = _row_tile(S, 512)
    pc = INC // NCHIP

    def body(dxo_ref, x_ref, dp_ref, vec_ref, w_ref, dxi_ref, sums_ref):
        xhat, r, gain, ng, sc, _, _ = _ada(x_ref[...], vec_ref)
        dh = jnp.zeros((tm, D), F32)
        for j in range(NCHIP):
            dh = dh + lax.dot_general(dp_ref[:, j * pc:(j + 1) * pc], w_ref[j], NT_DIMS, preferred_element_type=F32)
        dx, dshift, dscale, dng = _ada_bwd(dh, xhat, r, gain, ng, sc)
        dxi_ref[...] = dxo_ref[...] + dx
        _acc_rows(sums_ref, pl.program_id(0) == 0, (dshift, dscale, dng))

    t = pl.BlockSpec((tm, D), lambda i: (i, 0))
    return pl.pallas_call(
        body, name=name, grid=(S // tm,),
        in_specs=[t, t, pl.BlockSpec((tm, INC), lambda i: (i, 0)),
                  pl.BlockSpec((SUBLANES, D), lambda i: (0, 0)),
                  pl.BlockSpec((NCHIP, D, pc), lambda i: (0, 0, 0), pipeline_mode=pl.Buffered(1))],
        out_specs=[t, pl.BlockSpec((SUBLANES, D), lambda i: (0, 0))],
        out_shape=[jax.ShapeDtypeStruct((S, D), F32), jax.ShapeDtypeStruct((SUBLANES, D), F32)],
        compiler_params=_params(("arbitrary",)),
    )(dxo, x, dproj, vec, winp)


def loss_head(xf, target, name):
    S = xf.shape[0]
    tm = _row_tile(S, 1024)

    def body(x_ref, t_ref, dy_ref, l_ref):
        diff = x_ref[...] - t_ref[...]
        dy_ref[...] = diff * (1.0 / D)
        part = jnp.sum(jnp.sum(diff * diff, axis=0, keepdims=True), axis=1, keepdims=True) * (0.5 / D)

        @pl.when(pl.program_id(0) == 0)
        def _():
            l_ref[...] = jnp.zeros_like(l_ref)
        l_ref[...] += jnp.broadcast_to(part, l_ref.shape)

    t = pl.BlockSpec((tm, D), lambda i: (i, 0))
    return pl.pallas_call(
        body, name=name, grid=(S // tm,),
        in_specs=[t, t],
        out_specs=[t, pl.BlockSpec((SUBLANES, LANES), lambda i: (0, 0))],
        out_shape=[jax.ShapeDtypeStruct((S, D), F32), jax.ShapeDtypeStruct((SUBLANES, LANES), F32)],
        compiler_params=_params(("arbitrary",)),
    )(xf, target)


def _vec(mod_l, ng_l, i):
    m = mod_l.reshape(3, 3, D)
    rows = jnp.stack([ng_l[i], m[i, 1], m[i, 0], m[i, 2]])
    return jnp.concatenate([rows, jnp.zeros((SUBLANES - 4, D), F32)], axis=0)


def local_step(x, target, mods, ngs, gvecs, cws, weights):
    saved = []
    h = x
    for l in range(2):
        w = weights[l]
        vecs = [_vec(mods[l], ngs[l], i) for i in range(3)]
        x0 = h
        x1, a0, f0 = ffn_fwd(x0, vecs[0], w["w1"][0], w["w2"][0], 0.5, f"ffn_fwd_l{l}a")
        proj, h1b, qn, kn, v = mixer_in(x1, vecs[1], w["win"], gvecs[l], f"mixer_in_l{l}")
        os_, lses_a, lses_b = [], [], []
        for d in DILATIONS:
            o, la, lb = attn_fwd(qn, kn, v, d, f"attn_fwd_l{l}_d{d}")
            os_.append(o)
            lses_a.append(la)
            lses_b.append(lb)
        ycat, *lse = combine_conv(os_, lses_a, lses_b, proj, cws[l], f"combine_conv_l{l}")
        x2, y = out_proj(ycat, x1, vecs[1], w["wout"], f"out_proj_l{l}")
        x3, a2, f2 = ffn_fwd(x2, vecs[2], w["w1"][1], w["w2"][1], 0.5, f"ffn_fwd_l{l}b")
        saved.append(dict(vecs=vecs, x0=x0, a0=a0, f0=f0, x1=x1, proj=proj, h1b=h1b, qn=qn, kn=kn, v=v,
                          ycat=ycat, lse=lse, y=y, x2=x2, a2=a2, f2=f2))
        h = x3
    dx, loss_blk = loss_head(h, target, "loss_head")
    grads = [None, None]
    for l in (1, 0):
        w, s = weights[l], saved[l]
        vecs = s["vecs"]
        dx, hb, dfb, act, da, sums2 = ffn_bwd(dx, s["x2"], s["a2"], s["f2"], vecs[2], w["w1"][1], w["w2"][1],
                                              0.5, f"ffn_bwd_l{l}b")
        dw1b = wgrad(hb, da, D, HALF, f"wgrad_w1_l{l}b")
        dw2b = wgrad(act, dfb, HALF, D, f"wgrad_w2_l{l}b")
        dyb, dycat, dl_a, dl_b, sums_o = out_proj_bwd(dx, s["y"], s["ycat"], vecs[1], w["wout"], f"out_proj_bwd_l{l}")
        dwout = wgrad(s["ycat"].astype(MXU_DTYPE), dyb, D // 2, D, f"wgrad_wout_l{l}")
        dqs, dks, dvs = [], [], []
        for d in DILATIONS:
            dq, dk, dv = attn_bwd(s["qn"], s["kn"], s["v"], dycat, s["lse"][0], s["lse"][1], dl_a, dl_b, d,
                                  f"attn_bwd_l{l}_d{d}")
            dqs.append(dq)
            dks.append(dk)
            dvs.append(dv)
        dproj, sums_m = mixer_mid_bwd(dqs, dks, dvs, s["proj"], dycat, gvecs[l], cws[l], f"mixer_mid_bwd_l{l}")
        dwin = wgrad(s["h1b"], dproj, D, INC // NCHIP, f"wgrad_win_l{l}")
        dx, sums1 = mixer_in_bwd(dx, s["x1"], dproj, vecs[1], w["win"], f"mixer_in_bwd_l{l}")
        dx, hb, dfb, act, da, sums0 = ffn_bwd(dx, s["x0"], s["a0"], s["f0"], vecs[0], w["w1"][0], w["w2"][0],
                                              0.5, f"ffn_bwd_l{l}a")
        dw1a = wgrad(hb, da, D, HALF, f"wgrad_w1_l{l}a")
        dw2a = wgrad(act, dfb, HALF, D, f"wgrad_w2_l{l}a")
        grads[l] = dict(w1=[dw1a, dw1b], w2=[dw2a, dw2b], win=dwin, wout=dwout,
                        sums=(sums0, sums1, sums_o, sums2, sums_m))
    return loss_blk, dx, grads


MESH = pl.DeviceIdType.MESH
ANY = pl.BlockSpec(memory_space=pl.ANY)


def _here():
    return lax.axis_index("x"), lax.axis_index("y"), lax.axis_index("c")


def small_all_gather(blk, name):
    m_per, n = blk.shape

    def body(x_ref, out_ref, send_sems, recv_sems, local_sem):
        x, y, c = _here()
        me, sibling = (x, y, c), (x, y, 1 - c)
        chips = [(1 - x, y), (x, 1 - y), (1 - x, 1 - y)]

        def rows(px, py, pc):
            return out_ref.at[pl.ds((4 * px + 2 * py + pc) * m_per, m_per), :]

        def copy(k, block, to, src=None):
            return pltpu.make_async_remote_copy(
                src_ref=rows(*block) if src is None else src, dst_ref=rows(*block),
                send_sem=send_sems.at[k], recv_sem=recv_sems.at[k], device_id=to, device_id_type=MESH)

        mine = pltpu.make_async_copy(x_ref, rows(*me), local_sem)
        mine.start()
        first = [copy(0, me, sibling, src=x_ref)]
        first += [copy(1 + j, me, (*chip, c), src=x_ref) for j, chip in enumerate(chips)]
        for cp in first:
            cp.start()
        passed = [copy(4 + j, (*chip, c), sibling) for j, chip in enumerate(chips)]
        for j, chip in enumerate(chips):
            copy(1 + j, (*chip, c), me).wait_recv()
            passed[j].start()
        copy(0, sibling, me).wait_recv()
        for j, chip in enumerate(chips):
            copy(4 + j, (*chip, 1 - c), me).wait_recv()
        for cp in first + passed:
            cp.wait_send()
        mine.wait()

    return pl.pallas_call(
        body, name=name,
        out_shape=jax.ShapeDtypeStruct((NDEV * m_per, n), blk.dtype),
        in_specs=[pl.BlockSpec(memory_space=pltpu.VMEM)],
        out_specs=pl.BlockSpec(memory_space=pltpu.VMEM),
        scratch_shapes=[pltpu.SemaphoreType.DMA((7,)), pltpu.SemaphoreType.DMA((7,)), pltpu.SemaphoreType.DMA],
        compiler_params=pltpu.CompilerParams(vmem_limit_bytes=VMEM_LIMIT),
    )(blk)


def ici_exchange(srcs, scatter, name):
    n = len(srcs)

    def body(*refs):
        src_refs, dst_refs = refs[:n], refs[n:2 * n]
        send_sems, recv_sems, local_sems = refs[2 * n:]
        x, y, c = _here()
        my_chip = 2 * x + y
        peers = [(1 - x, y), (x, 1 - y), (1 - x, 1 - y)]
        copies = []
        for a in range(n):
            own = src_refs[a].at[my_chip] if scatter else src_refs[a]
            loc = pltpu.make_async_copy(own, dst_refs[a].at[my_chip], local_sems.at[a])
            loc.start()
            copies.append(loc)
        remote = []
        for a in range(n):
            for j, (px, py) in enumerate(peers):
                src = src_refs[a].at[2 * px + py] if scatter else src_refs[a]
                cp = pltpu.make_async_remote_copy(
                    src_ref=src, dst_ref=dst_refs[a].at[my_chip],
                    send_sem=send_sems.at[3 * a + j], recv_sem=recv_sems.at[3 * a + j],
                    device_id=(px, py, c), device_id_type=MESH)
                cp.start()
                remote.append(cp)
        for a in range(n):
            for j, (px, py) in enumerate(peers):
                src = src_refs[a].at[my_chip] if scatter else src_refs[a]
                pltpu.make_async_remote_copy(
                    src_ref=src, dst_ref=dst_refs[a].at[2 * px + py],
                    send_sem=send_sems.at[3 * a + j], recv_sem=recv_sems.at[3 * a + j],
                    device_id=(px, py, c), device_id_type=MESH).wait_recv()
        for cp in remote:
            cp.wait_send()
        for cp in copies:
            cp.wait()

    out_shape = [jax.ShapeDtypeStruct(s.shape if scatter else (NCHIP,) + s.shape, s.dtype) for s in srcs]
    return pl.pallas_call(
        body, name=name, out_shape=out_shape,
        in_specs=[ANY] * n, out_specs=[ANY] * n,
        scratch_shapes=[pltpu.SemaphoreType.DMA((3 * n,)), pltpu.SemaphoreType.DMA((3 * n,)),
                        pltpu.SemaphoreType.DMA((n,))],
    )(*srcs)


def d2d_swap(send_if_c0, send_if_c1, name):
    n = len(send_if_c0)

    def body(*refs):
        a_refs, b_refs, dst_refs = refs[:n], refs[n:2 * n], refs[2 * n:3 * n]
        send_sems, recv_sems = refs[3 * n:]
        x, y, c = _here()

        def copy(k, src):
            return pltpu.make_async_remote_copy(
                src_ref=src, dst_ref=dst_refs[k], send_sem=send_sems.at[k], recv_sem=recv_sems.at[k],
                device_id=(x, y, 1 - c), device_id_type=MESH)

        @pl.when(c == 0)
        def _():
            for k in range(n):
                copy(k, a_refs[k]).start()

        @pl.when(c == 1)
        def _():
            for k in range(n):
                copy(k, b_refs[k]).start()

        for k in range(n):
            copy(k, a_refs[k]).wait_recv()
        for k in range(n):
            copy(k, a_refs[k]).wait_send()

    return pl.pallas_call(
        body, name=name, out_shape=[jax.ShapeDtypeStruct(s.shape, s.dtype) for s in send_if_c0],
        in_specs=[ANY] * (2 * n), out_specs=[ANY] * n,
        scratch_shapes=[pltpu.SemaphoreType.DMA((n,)), pltpu.SemaphoreType.DMA((n,))],
    )(*send_if_c0, *send_if_c1)


EW_BLOCK_BYTES = 1 << 20


def _ew_rows(rows, cols):
    want = max(16, EW_BLOCK_BYTES // (4 * cols))
    best = None
    for t in range(16, rows + 1, 16):
        if rows % t == 0 and t <= want:
            best = t
    return best if best is not None else rows


def add_select(g0, g1, recv, cflag, name):
    shape = g0.shape
    cols = shape[-1]
    rows = g0.size // cols
    tr = _ew_rows(rows, cols)

    def body(c_ref, a_ref, b_ref, r_ref, o_ref):
        mine = jnp.where(c_ref[0] == 0, a_ref[...], b_ref[...])
        o_ref[...] = (mine + r_ref[...]).astype(o_ref.dtype)

    t = pl.BlockSpec((tr, cols), lambda i: (i, 0))
    out = pl.pallas_call(
        body, name=name, grid=(rows // tr,),
        in_specs=[pl.BlockSpec(memory_space=pltpu.SMEM), t, t, t], out_specs=t,
        out_shape=jax.ShapeDtypeStruct((rows, cols), WIRE_DTYPE),
        compiler_params=_params(("arbitrary",)),
    )(cflag, g0.reshape(rows, cols), g1.reshape(rows, cols), recv.reshape(rows, cols))
    return out.reshape(shape)


def sum_chips(recv, name):
    _, r, cols = recv.shape
    tr = _ew_rows(r, cols)

    def body(r_ref, o_ref):
        acc = r_ref[0].astype(F32)
        for k in range(1, NCHIP):
            acc = acc + r_ref[k].astype(F32)
        o_ref[...] = acc

    return pl.pallas_call(
        body, name=name, grid=(r // tr,),
        in_specs=[pl.BlockSpec((NCHIP, tr, cols), lambda i: (0, i, 0))],
        out_specs=pl.BlockSpec((tr, cols), lambda i: (i, 0)),
        out_shape=jax.ShapeDtypeStruct((r, cols), F32),
        compiler_params=_params(("arbitrary",)),
    )(recv)


def sum_devices(rows8, name):
    def body(r_ref, o_ref):
        acc = r_ref[0:1, :]
        for k in range(1, NDEV):
            acc = acc + r_ref[k:k + 1, :]
        o_ref[...] = jnp.broadcast_to(acc, o_ref.shape)

    return pl.pallas_call(
        body, name=name, out_shape=jax.ShapeDtypeStruct(rows8.shape, F32),
        in_specs=[pl.BlockSpec(memory_space=pltpu.VMEM)], out_specs=pl.BlockSpec(memory_space=pltpu.VMEM),
        compiler_params=pltpu.CompilerParams(vmem_limit_bytes=VMEM_LIMIT),
    )(rows8)


def adamw(w, m, v, srcs, table, cflag, name):
    planes, r, cols = w.shape
    tr = _ew_rows(r, cols)
    ns = len(srcs)

    def body(c_ref, w_ref, m_ref, v_ref, *rest):
        s_refs, (g_ref, d_ref, mo_ref, vo_ref) = rest[:ns], rest[ns:]
        p = pl.program_id(0)
        on_c0 = c_ref[0] == 0
        want = jnp.int32(0)
        for pp, (t0, t1) in enumerate(table):
            want = jnp.where(p == pp, jnp.where(on_c0, t0, t1), want)
        g = s_refs[0][...]
        for k in range(1, ns):
            g = jnp.where(want == k, s_refs[k][...], g)
        g_ref[...] = g
        m_new = ADAM_B1 * m_ref[...] + (1.0 - ADAM_B1) * g
        v_new = ADAM_B2 * v_ref[...] + (1.0 - ADAM_B2) * (g * g)
        mo_ref[...] = m_new
        vo_ref[...] = v_new
        m_hat = m_new / (1.0 - ADAM_B1 ** ADAM_STEP)
        v_hat = v_new / (1.0 - ADAM_B2 ** ADAM_STEP)
        d_ref[...] = -ADAM_LR * (m_hat / (jnp.sqrt(v_hat) + ADAM_EPS) + ADAM_WD * w_ref[...])

    pt = pl.BlockSpec((None, tr, cols), lambda p, i: (p, i, 0))
    st = pl.BlockSpec((tr, cols), lambda p, i: (i, 0))
    return pl.pallas_call(
        body, name=name, grid=(planes, r // tr),
        in_specs=[pl.BlockSpec(memory_space=pltpu.SMEM), pt, pt, pt] + [st] * ns,
        out_specs=[pt] * 4,
        out_shape=[jax.ShapeDtypeStruct(w.shape, F32)] * 4,
        compiler_params=_params(("arbitrary", "arbitrary")),
    )(cflag, w, m, v, *srcs)


ADA_COLS = 9 * D // NCHIP


def mod_fwd(c_all, w_ada, b_shard, name):
    def body(c_ref, w_ref, b_ref, o_ref):
        cc = c_ref[...]
        sc = cc * jax.nn.sigmoid(cc)
        o_ref[...] = jnp.dot(sc, w_ref[...], preferred_element_type=F32,
                             precision=lax.Precision.HIGHEST) + b_ref[...]

    return pl.pallas_call(
        body, name=name, grid=(2,),
        in_specs=[pl.BlockSpec((NDEV, D), lambda l: (0, 0)),
                  pl.BlockSpec((None, D, ADA_COLS), lambda l: (l, 0, 0)),
                  pl.BlockSpec((None, 1, ADA_COLS), lambda l: (l, 0, 0))],
        out_specs=pl.BlockSpec((None, NDEV, ADA_COLS), lambda l: (l, 0, 0)),
        out_shape=jax.ShapeDtypeStruct((2, NDEV, ADA_COLS), F32),
        compiler_params=_params(("arbitrary",)),
    )(c_all, w_ada, b_shard.reshape(2, 1, ADA_COLS))


def wada_grad(c_all_t, dmod, name):
    ct = ADA_COLS // 3

    def body(c_ref, d_ref, o_ref):
        cc = c_ref[...]
        sc = cc * jax.nn.sigmoid(cc)
        acc = sc[:, 0:1] * d_ref[0:1, :]
        for b in range(1, NDEV):
            acc = acc + sc[:, b:b + 1] * d_ref[b:b + 1, :]
        o_ref[...] = acc

    return pl.pallas_call(
        body, name=name, grid=(2, 3),
        in_specs=[pl.BlockSpec((D, LANES), lambda l, j: (0, 0)),
                  pl.BlockSpec((None, NDEV, ct), lambda l, j: (l, 0, j))],
        out_specs=pl.BlockSpec((None, D, ct), lambda l, j: (l, 0, j)),
        out_shape=jax.ShapeDtypeStruct((2, D, ADA_COLS), F32),
        compiler_params=_params(("arbitrary", "arbitrary")),
    )(c_all_t, dmod)


def _pad_rows(row, rows=SUBLANES):
    return jnp.concatenate([row[None, :], jnp.zeros((rows - 1, row.shape[0]), row.dtype)], axis=0)


def kernel(x, c, w_ada, b_ada, norm_g, w_in, q_norm_g, k_norm_g, conv_w, conv_b, w_out, ffn_w1, ffn_w2, loss_target, m_w_ada, m_b_ada, m_norm_g, m_w_in, m_q_norm_g, m_k_norm_g, m_conv_w, m_conv_b, m_w_out, m_ffn_w1, m_ffn_w2, v_w_ada, v_b_ada, v_norm_g, v_w_in, v_q_norm_g, v_k_norm_g, v_conv_w, v_conv_b, v_w_out, v_ffn_w1, v_ffn_w2):
    ix, iy, ic = lax.axis_index("x"), lax.axis_index("y"), lax.axis_index("c")
    chip = 2 * ix + iy
    dev = 2 * chip + ic
    cflag = jnp.reshape(ic, (1,)).astype(jnp.int32)
    ngw = norm_g.shape[-1]
    cww = conv_w.shape[-1]

    pack = jnp.concatenate([c[0], norm_g.reshape(-1), conv_w.reshape(-1)])
    got = small_all_gather(_pad_rows(pack), "gather_c_normg_convw")[::SUBLANES]
    c_all = got[:, :D]
    per_chip = got[::2]
    ng_full = jnp.concatenate([per_chip[j, D:D + 6 * ngw].reshape(2, 3, ngw) for j in range(NCHIP)], axis=-1)
    cw_full = jnp.concatenate([per_chip[j, D + 6 * ngw:].reshape(2, 3, cww) for j in range(NCHIP)], axis=-1)

    b_shard = lax.dynamic_slice_in_dim(b_ada, chip * ADA_COLS, ADA_COLS, axis=1)
    mod_blk = mod_fwd(c_all, w_ada, b_shard, "mod_fwd").reshape(2 * NDEV, ADA_COLS)
    mod_all = small_all_gather(mod_blk, "gather_mod").reshape(NDEV, 2, NDEV, ADA_COLS)[::2]
    mod_mine = lax.dynamic_index_in_dim(mod_all, dev, axis=2, keepdims=False)
    mods = [mod_mine[:, l, :].reshape(-1) for l in range(2)]

    shards = []
    for l in range(2):
        shards += [ffn_w1[l, 0], ffn_w2[l, 0], w_in[l], w_out[l], ffn_w1[l, 1], ffn_w2[l, 1]]
    full = ici_exchange([s.astype(MXU_DTYPE) for s in shards], False, "gather_weights")
    weights, gvecs, cws = [], [], []
    for l in range(2):
        w1a, w2a, win, wout, w1b, w2b = full[6 * l:6 * l + 6]
        weights.append(dict(w1=[w1a, w1b], w2=[w2a.reshape(DFF, D), w2b.reshape(DFF, D)],
                            win=win, wout=wout.reshape(D, D)))
        gv = jnp.stack([jnp.tile(q_norm_g[l], AW // HD), jnp.tile(k_norm_g[l], AW // HD)])
        gvecs.append(jnp.concatenate([gv, jnp.zeros((SUBLANES - 2, AW), F32)], axis=0))
        cws.append(jnp.concatenate([cw_full[l], conv_b[l][None, :], jnp.zeros((SUBLANES - 4, CW), F32)], axis=0))

    loss_blk, dx, grads = local_step(x[0], loss_target[0], mods, [ng_full[0], ng_full[1]], gvecs, cws, weights)
    loss = lax.psum(loss_blk[0, 0], ("x", "y", "c"))

    def glist(l):
        g = grads[l]
        return [g["w1"][0], g["w2"][0].reshape(NCHIP, DFF // NCHIP, D), g["win"],
                g["wout"].reshape(NCHIP, D // NCHIP, D), g["w1"][1], g["w2"][1].reshape(NCHIP, DFF // NCHIP, D)]

    g0, g1 = glist(0), glist(1)
    from_sib = d2d_swap(g1, g0, "swap_layer_grads")
    wire = [add_select(g0[k], g1[k], from_sib[k], cflag, f"add_sibling_{k}") for k in range(6)]
    landed = ici_exchange(wire, True, "scatter_grads")
    tot_mine = [sum_chips(landed[k], f"sum_chips_{k}") for k in range(6)]
    tot_sib = d2d_swap(tot_mine, tot_mine, "swap_totals")

    dmods, dngs, dqg, dkg, dcw, dcb = [], [], [], [], [], []
    for l in range(2):
        s0, s1, so, s2, sm = grads[l]["sums"]
        dmods.append(jnp.concatenate([s0[0], s0[1], s0[3], s1[0], s1[1], so[0], s2[0], s2[1], s2[3]]))
        dngs.append(jnp.concatenate([s0[2], s1[2], s2[2]]))
        dqg.append(sm[0].reshape(AW // HD, HD).sum(0))
        dkg.append(sm[1].reshape(AW // HD, HD).sum(0))
        dcw.append(sm[2:5].reshape(-1))
        dcb.append(sm[5])
    small = jnp.concatenate(dmods + dngs + dqg + dkg + dcw + dcb)
    small_all = small_all_gather(_pad_rows(small), "gather_small_grads")[::SUBLANES]
    nm = 9 * D
    dmod_all = small_all[:, :2 * nm].reshape(NDEV, 2, NCHIP, ADA_COLS)
    dmod_mine = lax.dynamic_index_in_dim(dmod_all, chip, axis=2, keepdims=False).transpose(1, 0, 2)
    tot = sum_devices(small_all, "sum_small_grads")[0]
    o = 2 * nm
    g_b_ada = tot[:o].reshape(2, nm)
    g_norm_g = lax.dynamic_slice_in_dim(tot[o:o + 6 * D].reshape(2, 3, D), chip * ngw, ngw, axis=2)
    o += 6 * D
    g_qg = tot[o:o + 2 * HD].reshape(2, HD)
    o += 2 * HD
    g_kg = tot[o:o + 2 * HD].reshape(2, HD)
    o += 2 * HD
    g_cw = lax.dynamic_slice_in_dim(tot[o:o + 6 * CW].reshape(2, 3, CW), chip * cww, cww, axis=2)
    o += 6 * CW
    g_cb = tot[o:o + 2 * CW].reshape(2, CW)

    c_all_t = jnp.concatenate([c_all.T, jnp.zeros((D, LANES - NDEV), F32)], axis=1)
    g_wada_src = wada_grad(c_all_t, dmod_mine, "wada_grad")

    same = [(0, 0), (1, 1)]
    by_layer = [(0, 1), (1, 0)]
    by_layer2 = [(0, 2), (1, 3), (2, 0), (3, 1)]
    r_wada = adamw(w_ada, m_w_ada, v_w_ada, [g_wada_src[0], g_wada_src[1]], same, cflag, "adamw_w_ada")
    r_win = adamw(w_in, m_w_in, v_w_in, [tot_mine[2], tot_sib[2]], by_layer, cflag, "adamw_w_in")
    r_wout = adamw(w_out, m_w_out, v_w_out, [tot_mine[3], tot_sib[3]], by_layer, cflag, "adamw_w_out")
    r_w1 = adamw(ffn_w1.reshape(4, D, HALF), m_ffn_w1.reshape(4, D, HALF), v_ffn_w1.reshape(4, D, HALF),
                 [tot_mine[0], tot_mine[4], tot_sib[0], tot_sib[4]], by_layer2, cflag, "adamw_ffn_w1")
    w2r = DFF // NCHIP
    r_w2 = adamw(ffn_w2.reshape(4, w2r, D), m_ffn_w2.reshape(4, w2r, D), v_ffn_w2.reshape(4, w2r, D),
                 [tot_mine[1], tot_mine[5], tot_sib[1], tot_sib[5]], by_layer2, cflag, "adamw_ffn_w2")
    r_w1 = [t.reshape(ffn_w1.shape) for t in r_w1]
    r_w2 = [t.reshape(ffn_w2.shape) for t in r_w2]

    smalls = [("b_ada", b_ada, m_b_ada, v_b_ada, g_b_ada), ("norm_g", norm_g, m_norm_g, v_norm_g, g_norm_g),
              ("q_norm_g", q_norm_g, m_q_norm_g, v_q_norm_g, g_qg), ("k_norm_g", k_norm_g, m_k_norm_g, v_k_norm_g, g_kg),
              ("conv_w", conv_w, m_conv_w, v_conv_w, g_cw), ("conv_b", conv_b, m_conv_b, v_conv_b, g_cb)]
    n_small = sum(t[1].size for t in smalls)
    pad = (-n_small) % (16 * LANES)

    def packed(idx):
        flat = jnp.concatenate([t[idx].reshape(-1) for t in smalls] + [jnp.zeros((pad,), F32)])
        return flat.reshape(-1, LANES)

    r_small = adamw(packed(1)[None], packed(2)[None], packed(3)[None], [packed(4)], [(0, 0)], cflag, "adamw_small")
    small_out = {}
    o = 0
    for name_, w_, _, _, _ in smalls:
        small_out[name_] = [t.reshape(-1)[o:o + w_.size].reshape(w_.shape) for t in r_small]
        o += w_.size

    res = {"w_ada": r_wada, "w_in": r_win, "w_out": r_wout, "ffn_w1": r_w1, "ffn_w2": r_w2, **small_out}
    order = ["w_ada", "b_ada", "norm_g", "w_in", "q_norm_g", "k_norm_g", "conv_w", "conv_b", "w_out", "ffn_w1", "ffn_w2"]
    outs = [loss, dx[None]]
    for k in range(4):
        outs += [res[nm_][k] for nm_ in order]
    return tuple(outs)
```

```python
import functools

import jax
import jax.numpy as jnp
from jax import lax
from jax.experimental import pallas as pl
from jax.experimental.pallas import tpu as pltpu

F32 = jnp.float32
MXU_DTYPE = jnp.bfloat16
ACT_DTYPE = jnp.bfloat16
WIRE_DTYPE = jnp.bfloat16

D = 1024
HD = 64
AW = 512
CW = 512
DFF = 2816
HALF = DFF // 2
INC = 3 * AW + 3 * CW
NCHIP = 4
NDEV = 8
QBLK = 128
ATTN_CHUNK_ROWS = 2048
DILATIONS = (1, 4, 16)
EPS = 1e-6
NEG = -1e30
LANES = 128
SUBLANES = 8
VMEM_LIMIT = 56 * 1024 * 1024

ADAM_LR = 0.001
ADAM_B1 = 0.9
ADAM_B2 = 0.999
ADAM_EPS = 1e-08
ADAM_WD = 0.01
ADAM_STEP = 10

NT_DIMS = (((1,), (1,)), ((), ()))
TN_DIMS = (((0,), (0,)), ((), ()))


def _params(sem, vmem=VMEM_LIMIT):
    return pltpu.CompilerParams(dimension_semantics=sem, vmem_limit_bytes=vmem)


def _row_tile(n, want):
    t = min(n, want)
    assert n % t == 0
    return t


def _ada(xt, vec_ref):
    ng, sc, sh, gt = vec_ref[0:1, :], vec_ref[1:2, :], vec_ref[2:3, :], vec_ref[3:4, :]
    r = lax.rsqrt(jnp.mean(xt * xt, axis=-1, keepdims=True) + EPS)
    return xt * r, r, ng * (1.0 + sc), ng, sc, sh, gt


def _ada_bwd(dh, xhat, r, gain, ng, sc):
    dshift = jnp.sum(dh, axis=0, keepdims=True)
    dhx = dh * xhat
    dscale = jnp.sum(dhx, axis=0, keepdims=True) * ng
    dng = jnp.sum(dhx, axis=0, keepdims=True) * (1.0 + sc)
    dxhat = dh * gain
    dx = r * (dxhat - xhat * jnp.mean(dxhat * xhat, axis=-1, keepdims=True))
    return dx, dshift, dscale, dng


def _acc_rows(sums_ref, first, rows):
    @pl.when(first)
    def _():
        sums_ref[...] = jnp.zeros_like(sums_ref)
    for k, row in enumerate(rows):
        sums_ref[k:k + 1, :] += row


MESH = pl.DeviceIdType.MESH
ANY = pl.BlockSpec(memory_space=pl.ANY)


def _here():
    return lax.axis_index("x"), lax.axis_index("y"), lax.axis_index("c")


def _ici_copies(src_refs, dst_refs, send_sems, recv_sems, local_sems, scatter):
    x, y, c = _here()
    my_chip = 2 * x + y
    peers = [(1 - x, y), (x, 1 - y), (1 - x, 1 - y)]
    local, out, inc = [], [], []
    for a, (src, dst) in enumerate(zip(src_refs, dst_refs)):
        local.append(pltpu.make_async_copy(src.at[my_chip] if scatter else src, dst.at[my_chip], local_sems.at[a]))
        for j, (px, py) in enumerate(peers):
            sems = dict(send_sem=send_sems.at[3 * a + j], recv_sem=recv_sems.at[3 * a + j],
                        device_id=(px, py, c), device_id_type=MESH)
            out.append(pltpu.make_async_remote_copy(
                src_ref=src.at[2 * px + py] if scatter else src, dst_ref=dst.at[my_chip], **sems))
            inc.append(pltpu.make_async_remote_copy(
                src_ref=src.at[my_chip] if scatter else src, dst_ref=dst.at[2 * px + py], **sems))
    return local, out, inc


def _ici_sems(n):
    return [pltpu.SemaphoreType.DMA((3 * n,)), pltpu.SemaphoreType.DMA((3 * n,)), pltpu.SemaphoreType.DMA((n,))]


def _pcall(body, name, grid, in_specs, out_specs, out_shape, sem, args, carry=()):
    n_in, n_out, nc = len(in_specs), len(out_specs), len(carry)
    if nc == 0:
        outs = pl.pallas_call(body, name=name, grid=grid, in_specs=in_specs, out_specs=out_specs,
                              out_shape=out_shape, compiler_params=_params(sem))(*args)
        return outs, []

    def wrapped(*refs):
        ins, csrc = refs[:n_in], refs[n_in:n_in + nc]
        outs, cdst = refs[n_in + nc:n_in + nc + n_out], refs[n_in + nc + n_out:n_in + 2 * nc + n_out]
        sems = refs[n_in + 2 * nc + n_out:]
        ids = [pl.program_id(a) for a in range(len(grid))]
        first = functools.reduce(jnp.logical_and, [i == 0 for i in ids])
        last = functools.reduce(jnp.logical_and, [i == g - 1 for i, g in zip(ids, grid)])

        @pl.when(first)
        def _():
            local, out, _ = _ici_copies(csrc, cdst, *sems, False)
            for cp in local + out:
                cp.start()

        body(*ins, *outs)

        @pl.when(last)
        def _():
            local, out, inc = _ici_copies(csrc, cdst, *sems, False)
            for cp in inc:
                cp.wait_recv()
            for cp in out:
                cp.wait_send()
            for cp in local:
                cp.wait()

    res = pl.pallas_call(
        wrapped, name=name, grid=grid,
        in_specs=list(in_specs) + [ANY] * nc, out_specs=list(out_specs) + [ANY] * nc,
        out_shape=list(out_shape) + [jax.ShapeDtypeStruct((NCHIP,) + s.shape, s.dtype) for s in carry],
        scratch_shapes=_ici_sems(nc), compiler_params=_params(sem),
    )(*args, *carry)
    return res[:n_out], res[n_out:]


def ffn_fwd(x, vec, w1p, w2, gs, name, carry=()):
    S = x.shape[0]
    tm = _row_tile(S, 512)

    def body(x_ref, vec_ref, w1_ref, w2_ref, xn_ref, a_ref, f_ref):
        xt = x_ref[...]
        xhat, _, gain, _, _, sh, gt = _ada(xt, vec_ref)
        h = (xhat * gain + sh).astype(MXU_DTYPE)
        f = jnp.zeros((tm, D), F32)
        for hf in range(2):
            g = jnp.dot(h, w1_ref[hf], preferred_element_type=F32)
            up = jnp.dot(h, w1_ref[2 + hf], preferred_element_type=F32)
            a_ref[:, hf * HALF:(hf + 1) * HALF] = g.astype(a_ref.dtype)
            a_ref[:, DFF + hf * HALF:DFF + (hf + 1) * HALF] = up.astype(a_ref.dtype)
            act = (g * jax.nn.sigmoid(g) * up).astype(MXU_DTYPE)
            f = f + jnp.dot(act, w2_ref[hf * HALF:(hf + 1) * HALF, :], preferred_element_type=F32)
        xn_ref[...] = xt + (gs * gt) * f
        f_ref[...] = f.astype(f_ref.dtype)

    return _pcall(
        body, name, (S // tm,),
        [pl.BlockSpec((tm, D), lambda i: (i, 0)),
         pl.BlockSpec((SUBLANES, D), lambda i: (0, 0)),
         pl.BlockSpec((NCHIP, D, HALF), lambda i: (0, 0, 0), pipeline_mode=pl.Buffered(1)),
         pl.BlockSpec((DFF, D), lambda i: (0, 0), pipeline_mode=pl.Buffered(1))],
        [pl.BlockSpec((tm, D), lambda i: (i, 0)),
         pl.BlockSpec((tm, 2 * DFF), lambda i: (i, 0)),
         pl.BlockSpec((tm, D), lambda i: (i, 0))],
        [jax.ShapeDtypeStruct((S, D), F32),
         jax.ShapeDtypeStruct((S, 2 * DFF), ACT_DTYPE),
         jax.ShapeDtypeStruct((S, D), ACT_DTYPE)],
        ("arbitrary",), (x, vec, w1p, w2), carry)


def ffn_bwd(dxo, x, a, f, vec, w1p, w2, gs, name):
    S = x.shape[0]
    tm = _row_tile(S, 256)

    def body(dxo_ref, x_ref, a_ref, f_ref, vec_ref, w1_ref, w2_ref,
             dxi_ref, hb_ref, dfb_ref, act_ref, da_ref, sums_ref):
        xt = x_ref[...]
        dxo = dxo_ref[...]
        xhat, r, gain, ng, sc, sh, gt = _ada(xt, vec_ref)
        hb_ref[...] = (xhat * gain + sh).astype(hb_ref.dtype)
        dgate = gs * jnp.sum(dxo * f_ref[...].astype(F32), axis=0, keepdims=True)
        df = ((gs * gt) * dxo).astype(MXU_DTYPE)
        dfb_ref[...] = df
        dh = jnp.zeros((tm, D), F32)
        for hf in range(2):
            lo, hi = hf * HALF, (hf + 1) * HALF
            dact = lax.dot_general(df, w2_ref[lo:hi, :], NT_DIMS, preferred_element_type=F32)
            g = a_ref[:, lo:hi].astype(F32)
            up = a_ref[:, DFF + lo:DFF + hi].astype(F32)
            sg = jax.nn.sigmoid(g)
            si = g * sg
            act_ref[:, lo:hi] = (si * up).astype(act_ref.dtype)
            dg = (dact * up * (sg * (1.0 + g * (1.0 - sg)))).astype(MXU_DTYPE)
            dup = (dact * si).astype(MXU_DTYPE)
            da_ref[:, lo:hi] = dg
            da_ref[:, DFF + lo:DFF + hi] = dup
            dh = dh + lax.dot_general(dg, w1_ref[hf], NT_DIMS, preferred_element_type=F32)
            dh = dh + lax.dot_general(dup, w1_ref[2 + hf], NT_DIMS, preferred_element_type=F32)
        dx, dshift, dscale, dng = _ada_bwd(dh, xhat, r, gain, ng, sc)
        dxi_ref[...] = dxo + dx
        _acc_rows(sums_ref, pl.program_id(0) == 0, (dshift, dscale, dng, dgate))

    return pl.pallas_call(
        body, name=name, grid=(S // tm,),
        in_specs=[pl.BlockSpec((tm, D), lambda i: (i, 0)),
                  pl.BlockSpec((tm, D), lambda i: (i, 0)),
                  pl.BlockSpec((tm, 2 * DFF), lambda i: (i, 0)),
                  pl.BlockSpec((tm, D), lambda i: (i, 0)),
                  pl.BlockSpec((SUBLANES, D), lambda i: (0, 0)),
                  pl.BlockSpec((NCHIP, D, HALF), lambda i: (0, 0, 0), pipeline_mode=pl.Buffered(1)),
                  pl.BlockSpec((DFF, D), lambda i: (0, 0), pipeline_mode=pl.Buffered(1))],
        out_specs=[pl.BlockSpec((tm, D), lambda i: (i, 0)),
                   pl.BlockSpec((tm, D), lambda i: (i, 0)),
                   pl.BlockSpec((tm, D), lambda i: (i, 0)),
                   pl.BlockSpec((tm, DFF), lambda i: (i, 0)),
                   pl.BlockSpec((tm, 2 * DFF), lambda i: (i, 0)),
                   pl.BlockSpec((SUBLANES, D), lambda i: (0, 0))],
        out_shape=[jax.ShapeDtypeStruct((S, D), F32),
                   jax.ShapeDtypeStruct((S, D), MXU_DTYPE),
                   jax.ShapeDtypeStruct((S, D), MXU_DTYPE),
                   jax.ShapeDtypeStruct((S, DFF), MXU_DTYPE),
                   jax.ShapeDtypeStruct((S, 2 * DFF), MXU_DTYPE),
                   jax.ShapeDtypeStruct((SUBLANES, D), F32)],
        compiler_params=_params(("arbitrary",)),
    )(dxo, x, a, f, vec, w1p, w2)


def wgrad(a, b, kt, nt, name):
    T, K = a.shape
    N = b.shape[1]
    pk, pn = K // kt, N // nt
    assert pk == 1 or pn == 1
    tt = _row_tile(T, 1024)
    steps = T // tt

    def body(a_ref, b_ref, o_ref):
        @pl.when(pl.program_id(1) == 0)
        def _():
            o_ref[...] = jnp.zeros_like(o_ref)
        o_ref[...] += lax.dot_general(a_ref[...], b_ref[...], TN_DIMS, preferred_element_type=F32)

    a_map = (lambda p, t: (t, p)) if pk > 1 else (lambda p, t: (t, 0))
    b_map = (lambda p, t: (t, p)) if pn > 1 else (lambda p, t: (t, 0))
    return pl.pallas_call(
        body, name=name, grid=(pk * pn, steps),
        in_specs=[pl.BlockSpec((tt, kt), a_map), pl.BlockSpec((tt, nt), b_map)],
        out_specs=pl.BlockSpec((None, kt, nt), lambda p, t: (p, 0, 0)),
        out_shape=jax.ShapeDtypeStruct((pk * pn, kt, nt), F32),
        compiler_params=_params(("arbitrary", "arbitrary")),
    )(a, b)


def _head_masks(rows):
    lane = lax.broadcasted_iota(jnp.int32, (rows, LANES), 1)
    return lane < HD


def _pair_stat(x, m_a):
    s_a = jnp.sum(jnp.where(m_a, x, 0.0), axis=1, keepdims=True)
    s_b = jnp.sum(jnp.where(m_a, 0.0, x), axis=1, keepdims=True)
    return s_a, s_b


def mixer_in(x, vec, winp, gvec, name):
    S = x.shape[0]
    tm = _row_tile(S, 512)
    pc = INC // NCHIP

    def body(x_ref, vec_ref, w_ref, g_ref, proj_ref, hb_ref, qn_ref, kn_ref, v_ref):
        xt = x_ref[...]
        xhat, _, gain, _, _, sh, _ = _ada(xt, vec_ref)
        h = (xhat * gain + sh).astype(MXU_DTYPE)
        hb_ref[...] = h
        for j in range(NCHIP):
            proj_ref[:, j * pc:(j + 1) * pc] = jnp.dot(h, w_ref[j], preferred_element_type=F32)
        m_a = _head_masks(tm)
        for which, dst in ((0, qn_ref), (1, kn_ref)):
            for p in range(AW // LANES):
                lo = which * AW + p * LANES
                xp = proj_ref[:, lo:lo + LANES]
                s_a, s_b = _pair_stat(xp * xp, m_a)
                rr = jnp.where(m_a, lax.rsqrt(s_a * (1.0 / HD) + EPS), lax.rsqrt(s_b * (1.0 / HD) + EPS))
                gp = g_ref[which:which + 1, p * LANES:(p + 1) * LANES]
                dst[:, p * LANES:(p + 1) * LANES] = (xp * rr * gp).astype(dst.dtype)
        v_ref[...] = proj_ref[:, 2 * AW:3 * AW].astype(v_ref.dtype)

    return pl.pallas_call(
        body, name=name, grid=(S // tm,),
        in_specs=[pl.BlockSpec((tm, D), lambda i: (i, 0)),
                  pl.BlockSpec((SUBLANES, D), lambda i: (0, 0)),
                  pl.BlockSpec((NCHIP, D, pc), lambda i: (0, 0, 0), pipeline_mode=pl.Buffered(1)),
                  pl.BlockSpec((SUBLANES, AW), lambda i: (0, 0))],
        out_specs=[pl.BlockSpec((tm, INC), lambda i: (i, 0)),
                   pl.BlockSpec((tm, D), lambda i: (i, 0)),
                   pl.BlockSpec((tm, AW), lambda i: (i, 0)),
                   pl.BlockSpec((tm, AW), lambda i: (i, 0)),
                   pl.BlockSpec((tm, AW), lambda i: (i, 0))],
        out_shape=[jax.ShapeDtypeStruct((S, INC), F32),
                   jax.ShapeDtypeStruct((S, D), MXU_DTYPE),
                   jax.ShapeDtypeStruct((S, AW), F32),
                   jax.ShapeDtypeStruct((S, AW), F32),
                   jax.ShapeDtypeStruct((S, AW), F32)],
        compiler_params=_params(("arbitrary",)),
    )(x, vec, winp, gvec)


def _band_masks(ncol):
    row = lax.broadcasted_iota(jnp.int32, (QBLK, ncol), 0)
    col = lax.broadcasted_iota(jnp.int32, (QBLK, ncol), 1)
    return row, col


def _attn_qb(d):
    return max(1, min(4, ATTN_CHUNK_ROWS // (QBLK * d)))


def _tile_rows(d, b, r):
    if d == 1:
        return pl.ds(b * QBLK, QBLK)
    return pl.ds(b * QBLK * d + r, QBLK, stride=d)


def _per_residue(d, fn):
    if d == 1:
        fn(0)
    else:
        def step(r, carry):
            fn(r)
            return carry
        lax.fori_loop(0, d, step, 0)


def attn_fwd(qn, kn, v, d, name, carry=()):
    S = qn.shape[0]
    qb = _attn_qb(d)
    halo = QBLK * d
    chunk = qb * halo

    def body(q_ref, kc_ref, kp_ref, vc_ref, vp_ref, o_ref, la_ref, lb_ref):
        i = pl.program_id(1)
        m_a = _head_masks(QBLK)
        row, col = _band_masks(2 * QBLK)
        dist = row + QBLK - col
        band = (dist >= 0) & (dist <= QBLK)
        first = band & ((i > 0) | (col >= QBLK))

        def residue(r):
            kt = [kp_ref[_tile_rows(d, 0, r), :].astype(MXU_DTYPE)]
            vt = [vp_ref[_tile_rows(d, 0, r), :].astype(MXU_DTYPE)]
            for b in range(qb):
                kt.append(kc_ref[_tile_rows(d, b, r), :].astype(MXU_DTYPE))
                vt.append(vc_ref[_tile_rows(d, b, r), :].astype(MXU_DTYPE))
            for b in range(qb):
                rows = _tile_rows(d, b, r)
                q = (q_ref[rows, :] * (HD ** -0.5)).astype(MXU_DTYPE)
                kcat = jnp.concatenate([kt[b], kt[b + 1]], axis=0)
                vcat = jnp.concatenate([vt[b], vt[b + 1]], axis=0)
                mask = first if b == 0 else band
                outs, lses = [], []
                for hm in (m_a, jnp.logical_not(m_a)):
                    qh = jnp.where(hm, q, jnp.zeros_like(q))
                    s = lax.dot_general(qh, kcat, NT_DIMS, preferred_element_type=F32)
                    s = jnp.where(mask, s, NEG)
                    m = jnp.max(s, axis=1, keepdims=True)
                    p = jnp.exp(s - m)
                    l = jnp.sum(p, axis=1, keepdims=True)
                    outs.append(jnp.dot(p.astype(MXU_DTYPE), vcat, preferred_element_type=F32) / l)
                    lses.append(m + jnp.log(l))
                o_ref[rows, :] = jnp.where(m_a, outs[0], outs[1])
                la_ref[rows, :] = jnp.broadcast_to(lses[0], (QBLK, LANES))
                lb_ref[rows, :] = jnp.broadcast_to(lses[1], (QBLK, LANES))

        _per_residue(d, residue)

    cur = pl.BlockSpec((chunk, LANES), lambda hp, i: (i, hp))
    prev = pl.BlockSpec((halo, LANES), lambda hp, i: (jnp.maximum(i * qb - 1, 0), hp))
    return _pcall(body, name, (AW // LANES, S // chunk), [cur, cur, prev, cur, prev], [cur, cur, cur],
                  [jax.ShapeDtypeStruct((S, AW), F32)] * 3, ("arbitrary", "arbitrary"), (qn, kn, kn, v, v), carry)


def attn_bwd(qn, kn, v, dycat, lse_a, lse_b, dl_a, dl_b, d, name):
    S = qn.shape[0]
    qb = _attn_qb(d)
    halo = QBLK * d
    chunk = qb * halo
    nhalo = S // halo
    nchunk = S // chunk

    def body(q_ref, qx_ref, kc_ref, kp_ref, vc_ref, vp_ref, do_ref, dox_ref, la_ref, lax_ref, lb_ref, lbx_ref,
             da_ref, dax_ref, db_ref, dbx_ref, dq_ref, dk_ref, dv_ref):
        i = pl.program_id(1)
        has_next = i < nchunk - 1
        m_a = _head_masks(QBLK)
        row, col = _band_masks(2 * QBLK)
        dist = row + QBLK - col
        band = (dist >= 0) & (dist <= QBLK)
        first = band & ((i > 0) | (col >= QBLK))
        row1, col1 = _band_masks(QBLK)
        off_only = (col1 >= row1) & has_next

        def residue(r):
            def tiles(cur_ref, next_ref, cast):
                out = [cur_ref[_tile_rows(d, b, r), :] for b in range(qb)] + [next_ref[_tile_rows(d, 0, r), :]]
                return [t.astype(MXU_DTYPE) for t in out] if cast else out

            def ktiles(cur_ref, prev_ref):
                out = [prev_ref[_tile_rows(d, 0, r), :]] + [cur_ref[_tile_rows(d, b, r), :] for b in range(qb)]
                return [t.astype(MXU_DTYPE) for t in out]

            qt = [(t * (HD ** -0.5)).astype(MXU_DTYPE) for t in tiles(q_ref, qx_ref, False)]
            dot_ = tiles(do_ref, dox_ref, True)
            stats = [(tiles(la_ref, lax_ref, False), tiles(da_ref, dax_ref, False)),
                     (tiles(lb_ref, lbx_ref, False), tiles(db_ref, dbx_ref, False))]
            kt = ktiles(kc_ref, kp_ref)
            vt = ktiles(vc_ref, vp_ref)
            dk_acc = [jnp.zeros((QBLK, LANES), F32) for _ in range(qb)]
            dv_acc = [jnp.zeros((QBLK, LANES), F32) for _ in range(qb)]
            for x in range(qb + 1):
                parts = 2 if x < qb else 1
                if parts == 2:
                    kcat = jnp.concatenate([kt[x], kt[x + 1]], axis=0)
                    vcat = jnp.concatenate([vt[x], vt[x + 1]], axis=0)
                else:
                    kcat, vcat = kt[x], vt[x]
                mask = first if x == 0 else (band if x < qb else off_only)
                dq_heads = []
                for hd, hm in enumerate((m_a, jnp.logical_not(m_a))):
                    qh = jnp.where(hm, qt[x], jnp.zeros_like(qt[x]))
                    doh = jnp.where(hm, dot_[x], jnp.zeros_like(dot_[x]))
                    s = lax.dot_general(qh, kcat, NT_DIMS, preferred_element_type=F32)
                    s = jnp.where(mask, s, NEG)
                    lse_h, dl_h = stats[hd][0][x], stats[hd][1][x]
                    if parts == 2:
                        lse_h = jnp.concatenate([lse_h, lse_h], axis=1)
                        dl_h = jnp.concatenate([dl_h, dl_h], axis=1)
                    p = jnp.exp(s - lse_h)
                    dp = lax.dot_general(doh, vcat, NT_DIMS, preferred_element_type=F32)
                    ds = p * (dp - dl_h)
                    if x < qb:
                        dq_heads.append(jnp.dot(ds.astype(MXU_DTYPE), kcat, preferred_element_type=F32))
                    ds_t = ds.T.astype(MXU_DTYPE)
                    p_t = p.T.astype(MXU_DTYPE)
                    for part in range(parts):
                        kb = x - 1 + part
                        if 0 <= kb < qb:
                            sl = slice(part * QBLK, (part + 1) * QBLK)
                            dk_acc[kb] = dk_acc[kb] + jnp.dot(ds_t[sl], qh, preferred_element_type=F32)
                            dv_acc[kb] = dv_acc[kb] + jnp.dot(p_t[sl], doh, preferred_element_type=F32)
                if x < qb:
                    dq_ref[_tile_rows(d, x, r), :] = jnp.where(m_a, dq_heads[0], dq_heads[1]) * (HD ** -0.5)
            for kb in range(qb):
                dk_ref[_tile_rows(d, kb, r), :] = dk_acc[kb]
                dv_ref[_tile_rows(d, kb, r), :] = dv_acc[kb]

        _per_residue(d, residue)

    def nxt(i):
        return jnp.minimum((i + 1) * qb, nhalo - 1)

    cur = pl.BlockSpec((chunk, LANES), lambda hp, i: (i, hp))
    prev = pl.BlockSpec((halo, LANES), lambda hp, i: (jnp.maximum(i * qb - 1, 0), hp))
    nx = pl.BlockSpec((halo, LANES), lambda hp, i: (nxt(i), hp))
    return pl.pallas_call(
        body, name=name, grid=(AW // LANES, nchunk),
        in_specs=[cur, nx, cur, prev, cur, prev, cur, nx, cur, nx, cur, nx, cur, nx, cur, nx],
        out_specs=[cur, cur, cur],
        out_shape=[jax.ShapeDtypeStruct((S, AW), F32)] * 3,
        compiler_params=_params(("arbitrary", "arbitrary")),
    )(qn, qn, kn, kn, v, v, dycat, dycat, lse_a, lse_a, lse_b, lse_b, dl_a, dl_a, dl_b, dl_b)


def _shift_down(x, halo_prev, k, row):
    tm = x.shape[0]
    tail = jnp.concatenate([pltpu.roll(halo_prev, k, 0), jnp.zeros((tm - SUBLANES, x.shape[1]), x.dtype)], axis=0)
    return jnp.where(row < k, tail, pltpu.roll(x, k, 0))


def _shift_up(x, halo_next, k, row):
    tm = x.shape[0]
    head = jnp.concatenate([jnp.zeros((tm - SUBLANES, x.shape[1]), x.dtype), pltpu.roll(halo_next, SUBLANES - k, 0)], axis=0)
    return jnp.where(row >= tm - k, head, pltpu.roll(x, tm - k, 0))


def _conv_fwd(cu, halo_cu, cw_ref, row):
    u1 = _shift_down(cu, halo_cu, 1, row)
    u2 = _shift_down(cu, halo_cu, 2, row)
    cv = cw_ref[0:1, :] * u2 + cw_ref[1:2, :] * u1 + cw_ref[2:3, :] * cu + cw_ref[3:4, :]
    return cv, u1, u2


def combine_conv(os_, lses_a, lses_b, proj, cw, name, carry=()):
    S = proj.shape[0]
    tm = _row_tile(S, 512)
    hb = tm // SUBLANES

    def body(o1, o2, o3, a1, a2, a3, b1, b2, b3, pc_ref, ph_ref, cw_ref, ycat_ref, la_ref, lb_ref):
        i = pl.program_id(0)
        m_a = _head_masks(tm)
        for p in range(AW // LANES):
            cs = slice(p * LANES, (p + 1) * LANES)
            tot = []
            for srcs, dst in (((a1, a2, a3), la_ref), ((b1, b2, b3), lb_ref)):
                ls = [l[:, cs] for l in srcs]
                mx = jnp.maximum(jnp.maximum(ls[0], ls[1]), ls[2])
                t = mx + jnp.log(jnp.exp(ls[0] - mx) + jnp.exp(ls[1] - mx) + jnp.exp(ls[2] - mx))
                dst[:, cs] = t
                tot.append((ls, t))
            acc = jnp.zeros((tm, LANES), F32)
            for r, o in enumerate((o1, o2, o3)):
                w = jnp.where(m_a, jnp.exp(tot[0][0][r] - tot[0][1]), jnp.exp(tot[1][0][r] - tot[1][1]))
                acc = acc + w * o[:, cs]
            ycat_ref[:, cs] = acc.astype(ycat_ref.dtype)
        row = lax.broadcasted_iota(jnp.int32, (tm, CW), 0)
        gb, gc, u = pc_ref[:, 0:CW], pc_ref[:, CW:2 * CW], pc_ref[:, 2 * CW:3 * CW]
        halo_cu = jnp.where(i > 0, ph_ref[:, CW:2 * CW] * ph_ref[:, 2 * CW:3 * CW], 0.0)
        cv, _, _ = _conv_fwd(gc * u, halo_cu, cw_ref, row)
        ycat_ref[:, AW:AW + CW] = (gb * cv).astype(ycat_ref.dtype)

    ot = pl.BlockSpec((tm, AW), lambda i: (i, 0))
    return _pcall(
        body, name, (S // tm,),
        [ot] * 9 + [pl.BlockSpec((tm, 3 * CW), lambda i: (i, 1)),
                    pl.BlockSpec((SUBLANES, 3 * CW), lambda i: (jnp.maximum(i * hb - 1, 0), 1)),
                    pl.BlockSpec((SUBLANES, CW), lambda i: (0, 0))],
        [pl.BlockSpec((tm, D), lambda i: (i, 0)), ot, ot],
        [jax.ShapeDtypeStruct((S, D), ACT_DTYPE), jax.ShapeDtypeStruct((S, AW), F32),
         jax.ShapeDtypeStruct((S, AW), F32)],
        ("arbitrary",), (*os_, *lses_a, *lses_b, proj, proj, cw), carry)


def out_proj(ycat, x, vec, wout, name):
    S = x.shape[0]
    tm = _row_tile(S, 512)

    def body(yc_ref, x_ref, vec_ref, w_ref, xn_ref, y_ref):
        y = jnp.dot(yc_ref[...].astype(MXU_DTYPE), w_ref[...], preferred_element_type=F32)
        xn_ref[...] = x_ref[...] + vec_ref[3:4, :] * y
        y_ref[...] = y.astype(y_ref.dtype)

    t = pl.BlockSpec((tm, D), lambda i: (i, 0))
    return pl.pallas_call(
        body, name=name, grid=(S // tm,),
        in_specs=[t, t, pl.BlockSpec((SUBLANES, D), lambda i: (0, 0)),
                  pl.BlockSpec((D, D), lambda i: (0, 0))],
        out_specs=[t, t],
        out_shape=[jax.ShapeDtypeStruct((S, D), F32), jax.ShapeDtypeStruct((S, D), ACT_DTYPE)],
        compiler_params=_params(("arbitrary",)),
    )(ycat, x, vec, wout)


def out_proj_bwd(dxo, y, ycat, vec, wout, name):
    S = dxo.shape[0]
    tm = _row_tile(S, 512)

    def body(dxo_ref, y_ref, yc_ref, vec_ref, w_ref, dyb_ref, dyc_ref, da_ref, db_ref, sums_ref):
        dxo = dxo_ref[...]
        dgate = jnp.sum(dxo * y_ref[...].astype(F32), axis=0, keepdims=True)
        dy = (vec_ref[3:4, :] * dxo).astype(MXU_DTYPE)
        dyb_ref[...] = dy
        dyc_ref[...] = lax.dot_general(dy, w_ref[...], NT_DIMS, preferred_element_type=F32)
        m_a = _head_masks(tm)
        for p in range(AW // LANES):
            cs = slice(p * LANES, (p + 1) * LANES)
            s_a, s_b = _pair_stat(dyc_ref[:, cs] * yc_ref[:, cs].astype(F32), m_a)
            da_ref[:, cs] = jnp.broadcast_to(s_a, (tm, LANES))
            db_ref[:, cs] = jnp.broadcast_to(s_b, (tm, LANES))
        _acc_rows(sums_ref, pl.program_id(0) == 0, (dgate,))

    t = pl.BlockSpec((tm, D), lambda i: (i, 0))
    at = pl.BlockSpec((tm, AW), lambda i: (i, 0))
    return pl.pallas_call(
        body, name=name, grid=(S // tm,),
        in_specs=[t, t, t, pl.BlockSpec((SUBLANES, D), lambda i: (0, 0)),
                  pl.BlockSpec((D, D), lambda i: (0, 0))],
        out_specs=[t, t, at, at, pl.BlockSpec((SUBLANES, D), lambda i: (0, 0))],
        out_shape=[jax.ShapeDtypeStruct((S, D), MXU_DTYPE), jax.ShapeDtypeStruct((S, D), F32),
                   jax.ShapeDtypeStruct((S, AW), F32), jax.ShapeDtypeStruct((S, AW), F32),
                   jax.ShapeDtypeStruct((SUBLANES, D), F32)],
        compiler_params=_params(("arbitrary",)),
    )(dxo, y, ycat, vec, wout)


def mixer_mid_bwd(dqs, dks, dvs, proj, dycat, gvec, cw, name):
    S = proj.shape[0]
    tm = _row_tile(S, 256)
    hb = tm // SUBLANES
    nsl = S // SUBLANES
    ntile = S // tm

    def body(dq1, dq2, dq3, dk1, dk2, dk3, dv1, dv2, dv3, pr_ref, pp_ref, pn_ref, dyc_ref, dyn_ref,
             g_ref, cw_ref, dp_ref, sums_ref):
        i = pl.program_id(0)
        m_a = _head_masks(tm)
        gsum = []
        for which, parts in ((0, (dq1, dq2, dq3)), (1, (dk1, dk2, dk3))):
            acc_g = []
            for p in range(AW // LANES):
                lo = which * AW + p * LANES
                cs = slice(p * LANES, (p + 1) * LANES)
                xp = pr_ref[:, lo:lo + LANES]
                s_a, s_b = _pair_stat(xp * xp, m_a)
                rr = jnp.where(m_a, lax.rsqrt(s_a * (1.0 / HD) + EPS), lax.rsqrt(s_b * (1.0 / HD) + EPS))
                xh = xp * rr
                dn = parts[0][:, cs] + parts[1][:, cs] + parts[2][:, cs]
                acc_g.append(jnp.sum(dn * xh, axis=0, keepdims=True))
                t = dn * g_ref[which:which + 1, cs]
                t_a, t_b = _pair_stat(t * xh, m_a)
                mean = jnp.where(m_a, t_a, t_b) * (1.0 / HD)
                dp_ref[:, lo:lo + LANES] = (rr * (t - xh * mean)).astype(dp_ref.dtype)
            gsum.append(jnp.concatenate(acc_g, axis=1))
        dp_ref[:, 2 * AW:3 * AW] = (dv1[...] + dv2[...] + dv3[...]).astype(dp_ref.dtype)
        row = lax.broadcasted_iota(jnp.int32, (tm, CW), 0)
        base = 3 * AW
        gb, gc, u = pr_ref[:, base:base + CW], pr_ref[:, base + CW:base + 2 * CW], pr_ref[:, base + 2 * CW:base + 3 * CW]
        cu = gc * u
        halo_cu = jnp.where(i > 0, pp_ref[:, CW:2 * CW] * pp_ref[:, 2 * CW:3 * CW], 0.0)
        cv, u1, u2 = _conv_fwd(cu, halo_cu, cw_ref, row)
        dyc = dyc_ref[...]
        dp_ref[:, base:base + CW] = (dyc * cv).astype(dp_ref.dtype)
        dcv = dyc * gb
        halo_dcv = jnp.where(i < ntile - 1, dyn_ref[...] * pn_ref[:, 0:CW], 0.0)
        d1 = _shift_up(dcv, halo_dcv, 1, row)
        d2 = _shift_up(dcv, halo_dcv, 2, row)
        dcu = cw_ref[2:3, :] * dcv + cw_ref[1:2, :] * d1 + cw_ref[0:1, :] * d2
        dp_ref[:, base + CW:base + 2 * CW] = (dcu * u).astype(dp_ref.dtype)
        dp_ref[:, base + 2 * CW:base + 3 * CW] = (dcu * gc).astype(dp_ref.dtype)
        rows = (gsum[0], gsum[1],
                jnp.sum(dcv * u2, axis=0, keepdims=True), jnp.sum(dcv * u1, axis=0, keepdims=True),
                jnp.sum(dcv * cu, axis=0, keepdims=True), jnp.sum(dcv, axis=0, keepdims=True))
        _acc_rows(sums_ref, i == 0, rows)

    at = pl.BlockSpec((tm, AW), lambda i: (i, 0))
    return pl.pallas_call(
        body, name=name, grid=(ntile,),
        in_specs=[at] * 9 + [
            pl.BlockSpec((tm, INC), lambda i: (i, 0)),
            pl.BlockSpec((SUBLANES, 3 * CW), lambda i: (jnp.maximum(i * hb - 1, 0), 1)),
            pl.BlockSpec((SUBLANES, 3 * CW), lambda i: (jnp.minimum((i + 1) * hb, nsl - 1), 1)),
            pl.BlockSpec((tm, CW), lambda i: (i, 1)),
            pl.BlockSpec((SUBLANES, CW), lambda i: (jnp.minimum((i + 1) * hb, nsl - 1), 1)),
            pl.BlockSpec((SUBLANES, AW), lambda i: (0, 0)),
            pl.BlockSpec((SUBLANES, CW), lambda i: (0, 0))],
        out_specs=[pl.BlockSpec((tm, INC), lambda i: (i, 0)),
                   pl.BlockSpec((SUBLANES, AW), lambda i: (0, 0))],
        out_shape=[jax.ShapeDtypeStruct((S, INC), MXU_DTYPE), jax.ShapeDtypeStruct((SUBLANES, AW), F32)],
        compiler_params=_params(("arbitrary",)),
    )(*dqs, *dks, *dvs, proj, proj, proj, dycat, dycat, gvec, cw)


def mixer_in_bwd(dxo, x, dproj, vec, winp, name):
    S = x.shape[0]
    tm = _row_tile(S, 512)
    pc = INC // NCHIP

    def body(dxo_ref, x_ref, dp_ref, vec_ref, w_ref, dxi_ref, sums_ref):
        xhat, r, gain, ng, sc, _, _ = _ada(x_ref[...], vec_ref)
        dh = jnp.zeros((tm, D), F32)
        for j in range(NCHIP):
            dh = dh + lax.dot_general(dp_ref[:, j * pc:(j + 1) * pc], w_ref[j], NT_DIMS, preferred_element_type=F32)
        dx, dshift, dscale, dng = _ada_bwd(dh, xhat, r, gain, ng, sc)
        dxi_ref[...] = dxo_ref[...] + dx
        _acc_rows(sums_ref, pl.program_id(0) == 0, (dshift, dscale, dng))

    t = pl.BlockSpec((tm, D), lambda i: (i, 0))
    return pl.pallas_call(
        body, name=name, grid=(S // tm,),
        in_specs=[t, t, pl.BlockSpec((tm, INC), lambda i: (i, 0)),
                  pl.BlockSpec((SUBLANES, D), lambda i: (0, 0)),
                  pl.BlockSpec((NCHIP, D, pc), lambda i: (0, 0, 0), pipeline_mode=pl.Buffered(1))],
        out_specs=[t, pl.BlockSpec((SUBLANES, D), lambda i: (0, 0))],
        out_shape=[jax.ShapeDtypeStruct((S, D), F32), jax.ShapeDtypeStruct((SUBLANES, D), F32)],
        compiler_params=_params(("arbitrary",)),
    )(dxo, x, dproj, vec, winp)


def loss_head(xf, target, name):
    S = xf.shape[0]
    tm = _row_tile(S, 1024)

    def body(x_ref, t_ref, dy_ref, l_ref):
        diff = x_ref[...] - t_ref[...]
        dy_ref[...] = diff * (1.0 / D)
        part = jnp.sum(jnp.sum(diff * diff, axis=0, keepdims=True), axis=1, keepdims=True) * (0.5 / D)

        @pl.when(pl.program_id(0) == 0)
        def _():
            l_ref[...] = jnp.zeros_like(l_ref)
        l_ref[...] += jnp.broadcast_to(part, l_ref.shape)

    t = pl.BlockSpec((tm, D), lambda i: (i, 0))
    return pl.pallas_call(
        body, name=name, grid=(S // tm,),
        in_specs=[t, t],
        out_specs=[t, pl.BlockSpec((SUBLANES, LANES), lambda i: (0, 0))],
        out_shape=[jax.ShapeDtypeStruct((S, D), F32), jax.ShapeDtypeStruct((SUBLANES, LANES), F32)],
        compiler_params=_params(("arbitrary",)),
    )(xf, target)


def _vec(mod_l, ng_l, i):
    m = mod_l.reshape(3, 3, D)
    rows = jnp.stack([ng_l[i], m[i, 1], m[i, 0], m[i, 2]])
    return jnp.concatenate([rows, jnp.zeros((SUBLANES - 4, D), F32)], axis=0)


def local_step(x, target, mods, ngs, gvecs, cws, shards, w_first):
    saved = []
    weights = [dict(w1=[None, None], w2=[None, None]) for _ in range(2)]
    weights[0]["w1"][0], weights[0]["w2"][0] = w_first[0], w_first[1].reshape(DFF, D)
    h = x
    for l in range(2):
        w, sh = weights[l], shards[l]
        nxt = shards[l + 1] if l == 0 else None
        vecs = [_vec(mods[l], ngs[l], i) for i in range(3)]
        x0 = h
        (x1, a0, f0), (win, wout) = ffn_fwd(x0, vecs[0], w["w1"][0], w["w2"][0], 0.5, f"ffn_fwd_l{l}a",
                                            carry=[sh["win"], sh["wout"]])
        w["win"], w["wout"] = win, wout.reshape(D, D)
        proj, h1b, qn, kn, v = mixer_in(x1, vecs[1], w["win"], gvecs[l], f"mixer_in_l{l}")
        os_, lses_a, lses_b = [], [], []
        for d in DILATIONS:
            carry = {1: [sh["w2"][1]], 16: [sh["w1"][1]]}.get(d, [])
            (o, la, lb), got = attn_fwd(qn, kn, v, d, f"attn_fwd_l{l}_d{d}", carry=carry)
            if d == 1:
                w["w2"][1] = got[0].reshape(DFF, D)
            if d == 16:
                w["w1"][1] = got[0]
            os_.append(o)
            lses_a.append(la)
            lses_b.append(lb)
        (ycat, *lse), got = combine_conv(os_, lses_a, lses_b, proj, cws[l], f"combine_conv_l{l}",
                                         carry=[nxt["w2"][0]] if nxt else [])
        if nxt:
            weights[1]["w2"][0] = got[0].reshape(DFF, D)
        x2, y = out_proj(ycat, x1, vecs[1], w["wout"], f"out_proj_l{l}")
        (x3, a2, f2), got = ffn_fwd(x2, vecs[2], w["w1"][1], w["w2"][1], 0.5, f"ffn_fwd_l{l}b",
                                    carry=[nxt["w1"][0]] if nxt else [])
        if nxt:
            weights[1]["w1"][0] = got[0]
        saved.append(dict(vecs=vecs, x0=x0, a0=a0, f0=f0, x1=x1, proj=proj, h1b=h1b, qn=qn, kn=kn, v=v,
                          ycat=ycat, lse=lse, y=y, x2=x2, a2=a2, f2=f2))
        h = x3
    dx, loss_blk = loss_head(h, target, "loss_head")
    grads = [None, None]
    for l in (1, 0):
        w, s = weights[l], saved[l]
        vecs = s["vecs"]
        dx, hb, dfb, act, da, sums2 = ffn_bwd(dx, s["x2"], s["a2"], s["f2"], vecs[2], w["w1"][1], w["w2"][1],
                                              0.5, f"ffn_bwd_l{l}b")
        dw1b = wgrad(hb, da, D, HALF, f"wgrad_w1_l{l}b")
        dw2b = wgrad(act, dfb, HALF, D, f"wgrad_w2_l{l}b")
        dyb, dycat, dl_a, dl_b, sums_o = out_proj_bwd(dx, s["y"], s["ycat"], vecs[1], w["wout"], f"out_proj_bwd_l{l}")
        dwout = wgrad(s["ycat"].astype(MXU_DTYPE), dyb, D // 2, D, f"wgrad_wout_l{l}")
        dqs, dks, dvs = [], [], []
        for d in DILATIONS:
            dq, dk, dv = attn_bwd(s["qn"], s["kn"], s["v"], dycat, s["lse"][0], s["lse"][1], dl_a, dl_b, d,
                                  f"attn_bwd_l{l}_d{d}")
            dqs.append(dq)
            dks.append(dk)
            dvs.append(dv)
        dproj, sums_m = mixer_mid_bwd(dqs, dks, dvs, s["proj"], dycat, gvecs[l], cws[l], f"mixer_mid_bwd_l{l}")
        dwin = wgrad(s["h1b"], dproj, D, INC // NCHIP, f"wgrad_win_l{l}")
        dx, sums1 = mixer_in_bwd(dx, s["x1"], dproj, vecs[1], w["win"], f"mixer_in_bwd_l{l}")
        dx, hb, dfb, act, da, sums0 = ffn_bwd(dx, s["x0"], s["a0"], s["f0"], vecs[0], w["w1"][0], w["w2"][0],
                                              0.5, f"ffn_bwd_l{l}a")
        dw1a = wgrad(hb, da, D, HALF, f"wgrad_w1_l{l}a")
        dw2a = wgrad(act, dfb, HALF, D, f"wgrad_w2_l{l}a")
        grads[l] = dict(w1=[dw1a, dw1b], w2=[dw2a, dw2b], win=dwin, wout=dwout,
                        sums=(sums0, sums1, sums_o, sums2, sums_m))
    return loss_blk, dx, grads


def small_all_gather(blk, name):
    m_per, n = blk.shape

    def body(x_ref, out_ref, send_sems, recv_sems, local_sem):
        x, y, c = _here()
        me, sibling = (x, y, c), (x, y, 1 - c)
        chips = [(1 - x, y), (x, 1 - y), (1 - x, 1 - y)]

        def rows(px, py, pc):
            return out_ref.at[pl.ds((4 * px + 2 * py + pc) * m_per, m_per), :]

        def copy(k, block, to, src=None):
            return pltpu.make_async_remote_copy(
                src_ref=rows(*block) if src is None else src, dst_ref=rows(*block),
                send_sem=send_sems.at[k], recv_sem=recv_sems.at[k], device_id=to, device_id_type=MESH)

        mine = pltpu.make_async_copy(x_ref, rows(*me), local_sem)
        mine.start()
        first = [copy(0, me, sibling, src=x_ref)]
        first += [copy(1 + j, me, (*chip, c), src=x_ref) for j, chip in enumerate(chips)]
        for cp in first:
            cp.start()
        passed = [copy(4 + j, (*chip, c), sibling) for j, chip in enumerate(chips)]
        for j, chip in enumerate(chips):
            copy(1 + j, (*chip, c), me).wait_recv()
            passed[j].start()
        copy(0, sibling, me).wait_recv()
        for j, chip in enumerate(chips):
            copy(4 + j, (*chip, 1 - c), me).wait_recv()
        for cp in first + passed:
            cp.wait_send()
        mine.wait()

    return pl.pallas_call(
        body, name=name,
        out_shape=jax.ShapeDtypeStruct((NDEV * m_per, n), blk.dtype),
        in_specs=[pl.BlockSpec(memory_space=pltpu.VMEM)],
        out_specs=pl.BlockSpec(memory_space=pltpu.VMEM),
        scratch_shapes=[pltpu.SemaphoreType.DMA((7,)), pltpu.SemaphoreType.DMA((7,)), pltpu.SemaphoreType.DMA],
        compiler_params=pltpu.CompilerParams(vmem_limit_bytes=VMEM_LIMIT),
    )(blk)


def ici_exchange(srcs, scatter, name):
    n = len(srcs)

    def body(*refs):
        local, out, inc = _ici_copies(refs[:n], refs[n:2 * n], *refs[2 * n:], scatter)
        for cp in local + out:
            cp.start()
        for cp in inc:
            cp.wait_recv()
        for cp in out:
            cp.wait_send()
        for cp in local:
            cp.wait()

    out_shape = [jax.ShapeDtypeStruct(s.shape if scatter else (NCHIP,) + s.shape, s.dtype) for s in srcs]
    return pl.pallas_call(
        body, name=name, out_shape=out_shape,
        in_specs=[ANY] * n, out_specs=[ANY] * n, scratch_shapes=_ici_sems(n),
    )(*srcs)


def d2d_swap(send_if_c0, send_if_c1, name):
    n = len(send_if_c0)

    def body(*refs):
        a_refs, b_refs, dst_refs = refs[:n], refs[n:2 * n], refs[2 * n:3 * n]
        send_sems, recv_sems = refs[3 * n:]
        x, y, c = _here()

        def copy(k, src):
            return pltpu.make_async_remote_copy(
                src_ref=src, dst_ref=dst_refs[k], send_sem=send_sems.at[k], recv_sem=recv_sems.at[k],
                device_id=(x, y, 1 - c), device_id_type=MESH)

        @pl.when(c == 0)
        def _():
            for k in range(n):
                copy(k, a_refs[k]).start()

        @pl.when(c == 1)
        def _():
            for k in range(n):
                copy(k, b_refs[k]).start()

        for k in range(n):
            copy(k, a_refs[k]).wait_recv()
        for k in range(n):
            copy(k, a_refs[k]).wait_send()

    return pl.pallas_call(
        body, name=name, out_shape=[jax.ShapeDtypeStruct(s.shape, s.dtype) for s in send_if_c0],
        in_specs=[ANY] * (2 * n), out_specs=[ANY] * n,
        scratch_shapes=[pltpu.SemaphoreType.DMA((n,)), pltpu.SemaphoreType.DMA((n,))],
    )(*send_if_c0, *send_if_c1)


EW_BLOCK_BYTES = 1 << 20


def _ew_rows(rows, cols):
    want = max(16, EW_BLOCK_BYTES // (4 * cols))
    best = None
    for t in range(16, rows + 1, 16):
        if rows % t == 0 and t <= want:
            best = t
    return best if best is not None else rows


def add_select(g0, g1, recv, cflag, name):
    shape = g0.shape
    cols = shape[-1]
    rows = g0.size // cols
    tr = _ew_rows(rows, cols)

    def body(c_ref, a_ref, b_ref, r_ref, o_ref):
        mine = jnp.where(c_ref[0] == 0, a_ref[...], b_ref[...])
        o_ref[...] = (mine + r_ref[...]).astype(o_ref.dtype)

    t = pl.BlockSpec((tr, cols), lambda i: (i, 0))
    out = pl.pallas_call(
        body, name=name, grid=(rows // tr,),
        in_specs=[pl.BlockSpec(memory_space=pltpu.SMEM), t, t, t], out_specs=t,
        out_shape=jax.ShapeDtypeStruct((rows, cols), WIRE_DTYPE),
        compiler_params=_params(("arbitrary",)),
    )(cflag, g0.reshape(rows, cols), g1.reshape(rows, cols), recv.reshape(rows, cols))
    return out.reshape(shape)


def sum_chips(recv, name):
    _, r, cols = recv.shape
    tr = _ew_rows(r, cols)

    def body(r_ref, o_ref):
        acc = r_ref[0].astype(F32)
        for k in range(1, NCHIP):
            acc = acc + r_ref[k].astype(F32)
        o_ref[...] = acc

    return pl.pallas_call(
        body, name=name, grid=(r // tr,),
        in_specs=[pl.BlockSpec((NCHIP, tr, cols), lambda i: (0, i, 0))],
        out_specs=pl.BlockSpec((tr, cols), lambda i: (i, 0)),
        out_shape=jax.ShapeDtypeStruct((r, cols), F32),
        compiler_params=_params(("arbitrary",)),
    )(recv)


def sum_devices(rows8, name):
    def body(r_ref, o_ref):
        acc = r_ref[0:1, :]
        for k in range(1, NDEV):
            acc = acc + r_ref[k:k + 1, :]
        o_ref[...] = jnp.broadcast_to(acc, o_ref.shape)

    return pl.pallas_call(
        body, name=name, out_shape=jax.ShapeDtypeStruct(rows8.shape, F32),
        in_specs=[pl.BlockSpec(memory_space=pltpu.VMEM)], out_specs=pl.BlockSpec(memory_space=pltpu.VMEM),
        compiler_params=pltpu.CompilerParams(vmem_limit_bytes=VMEM_LIMIT),
    )(rows8)


def adamw(w, m, v, srcs, table, cflag, name):
    planes, r, cols = w.shape
    tr = _ew_rows(r, cols)
    ns = len(srcs)

    def body(c_ref, w_ref, m_ref, v_ref, *rest):
        s_refs, (g_ref, d_ref, mo_ref, vo_ref) = rest[:ns], rest[ns:]
        p = pl.program_id(0)
        on_c0 = c_ref[0] == 0
        want = jnp.int32(0)
        for pp, (t0, t1) in enumerate(table):
            want = jnp.where(p == pp, jnp.where(on_c0, t0, t1), want)
        g = s_refs[0][...]
        for k in range(1, ns):
            g = jnp.where(want == k, s_refs[k][...], g)
        g_ref[...] = g
        m_new = ADAM_B1 * m_ref[...] + (1.0 - ADAM_B1) * g
        v_new = ADAM_B2 * v_ref[...] + (1.0 - ADAM_B2) * (g * g)
        mo_ref[...] = m_new
        vo_ref[...] = v_new
        m_hat = m_new / (1.0 - ADAM_B1 ** ADAM_STEP)
        v_hat = v_new / (1.0 - ADAM_B2 ** ADAM_STEP)
        d_ref[...] = -ADAM_LR * (m_hat / (jnp.sqrt(v_hat) + ADAM_EPS) + ADAM_WD * w_ref[...])

    pt = pl.BlockSpec((None, tr, cols), lambda p, i: (p, i, 0))
    st = pl.BlockSpec((tr, cols), lambda p, i: (i, 0))
    return pl.pallas_call(
        body, name=name, grid=(planes, r // tr),
        in_specs=[pl.BlockSpec(memory_space=pltpu.SMEM), pt, pt, pt] + [st] * ns,
        out_specs=[pt] * 4,
        out_shape=[jax.ShapeDtypeStruct(w.shape, F32)] * 4,
        compiler_params=_params(("arbitrary", "arbitrary")),
    )(cflag, w, m, v, *srcs)


ADA_COLS = 9 * D // NCHIP


def mod_fwd(c_all, w_ada, b_shard, name):
    def body(c_ref, w_ref, b_ref, o_ref):
        cc = c_ref[...]
        sc = cc * jax.nn.sigmoid(cc)
        o_ref[...] = jnp.dot(sc, w_ref[...], preferred_element_type=F32,
                             precision=lax.Precision.HIGHEST) + b_ref[...]

    return pl.pallas_call(
        body, name=name, grid=(2,),
        in_specs=[pl.BlockSpec((NDEV, D), lambda l: (0, 0)),
                  pl.BlockSpec((None, D, ADA_COLS), lambda l: (l, 0, 0)),
                  pl.BlockSpec((None, 1, ADA_COLS), lambda l: (l, 0, 0))],
        out_specs=pl.BlockSpec((None, NDEV, ADA_COLS), lambda l: (l, 0, 0)),
        out_shape=jax.ShapeDtypeStruct((2, NDEV, ADA_COLS), F32),
        compiler_params=_params(("arbitrary",)),
    )(c_all, w_ada, b_shard.reshape(2, 1, ADA_COLS))


def wada_grad(c_all_t, dmod, name):
    ct = ADA_COLS // 3

    def body(c_ref, d_ref, o_ref):
        cc = c_ref[...]
        sc = cc * jax.nn.sigmoid(cc)
        acc = sc[:, 0:1] * d_ref[0:1, :]
        for b in range(1, NDEV):
            acc = acc + sc[:, b:b + 1] * d_ref[b:b + 1, :]
        o_ref[...] = acc

    return pl.pallas_call(
        body, name=name, grid=(2, 3),
        in_specs=[pl.BlockSpec((D, LANES), lambda l, j: (0, 0)),
                  pl.BlockSpec((None, NDEV, ct), lambda l, j: (l, 0, j))],
        out_specs=pl.BlockSpec((None, D, ct), lambda l, j: (l, 0, j)),
        out_shape=jax.ShapeDtypeStruct((2, D, ADA_COLS), F32),
        compiler_params=_params(("arbitrary", "arbitrary")),
    )(c_all_t, dmod)


def _pad_rows(row, rows=SUBLANES):
    return jnp.concatenate([row[None, :], jnp.zeros((rows - 1, row.shape[0]), row.dtype)], axis=0)


def kernel(x, c, w_ada, b_ada, norm_g, w_in, q_norm_g, k_norm_g, conv_w, conv_b, w_out, ffn_w1, ffn_w2, loss_target, m_w_ada, m_b_ada, m_norm_g, m_w_in, m_q_norm_g, m_k_norm_g, m_conv_w, m_conv_b, m_w_out, m_ffn_w1, m_ffn_w2, v_w_ada, v_b_ada, v_norm_g, v_w_in, v_q_norm_g, v_k_norm_g, v_conv_w, v_conv_b, v_w_out, v_ffn_w1, v_ffn_w2):
    ix, iy, ic = lax.axis_index("x"), lax.axis_index("y"), lax.axis_index("c")
    chip = 2 * ix + iy
    dev = 2 * chip + ic
    cflag = jnp.reshape(ic, (1,)).astype(jnp.int32)
    ngw = norm_g.shape[-1]
    cww = conv_w.shape[-1]

    pack = jnp.concatenate([c[0], norm_g.reshape(-1), conv_w.reshape(-1)])
    got = small_all_gather(_pad_rows(pack), "gather_c_normg_convw")[::SUBLANES]
    c_all = got[:, :D]
    per_chip = got[::2]
    ng_full = jnp.concatenate([per_chip[j, D:D + 6 * ngw].reshape(2, 3, ngw) for j in range(NCHIP)], axis=-1)
    cw_full = jnp.concatenate([per_chip[j, D + 6 * ngw:].reshape(2, 3, cww) for j in range(NCHIP)], axis=-1)

    b_shard = lax.dynamic_slice_in_dim(b_ada, chip * ADA_COLS, ADA_COLS, axis=1)
    mod_blk = mod_fwd(c_all, w_ada, b_shard, "mod_fwd").reshape(2 * NDEV, ADA_COLS)
    mod_all = small_all_gather(mod_blk, "gather_mod").reshape(NDEV, 2, NDEV, ADA_COLS)[::2]
    mod_mine = lax.dynamic_index_in_dim(mod_all, dev, axis=2, keepdims=False)
    mods = [mod_mine[:, l, :].reshape(-1) for l in range(2)]

    shards, gvecs, cws = [], [], []
    for l in range(2):
        shards.append(dict(w1=[ffn_w1[l, i].astype(MXU_DTYPE) for i in range(2)],
                           w2=[ffn_w2[l, i].astype(MXU_DTYPE) for i in range(2)],
                           win=w_in[l].astype(MXU_DTYPE), wout=w_out[l].astype(MXU_DTYPE)))
        gv = jnp.stack([jnp.tile(q_norm_g[l], AW // HD), jnp.tile(k_norm_g[l], AW // HD)])
        gvecs.append(jnp.concatenate([gv, jnp.zeros((SUBLANES - 2, AW), F32)], axis=0))
        cws.append(jnp.concatenate([cw_full[l], conv_b[l][None, :], jnp.zeros((SUBLANES - 4, CW), F32)], axis=0))
    w_first = ici_exchange([shards[0]["w1"][0], shards[0]["w2"][0]], False, "gather_first_ffn")

    loss_blk, dx, grads = local_step(x[0], loss_target[0], mods, [ng_full[0], ng_full[1]], gvecs, cws, shards, w_first)
    loss = lax.psum(loss_blk[0, 0], ("x", "y", "c"))

    def glist(l):
        g = grads[l]
        return [g["w1"][0], g["w2"][0].reshape(NCHIP, DFF // NCHIP, D), g["win"],
                g["wout"].reshape(NCHIP, D // NCHIP, D), g["w1"][1], g["w2"][1].reshape(NCHIP, DFF // NCHIP, D)]

    g0, g1 = glist(0), glist(1)
    from_sib = d2d_swap(g1, g0, "swap_layer_grads")
    wire = [add_select(g0[k], g1[k], from_sib[k], cflag, f"add_sibling_{k}") for k in range(6)]
    landed = ici_exchange(wire, True, "scatter_grads")
    tot_mine = [sum_chips(landed[k], f"sum_chips_{k}") for k in range(6)]
    tot_sib = d2d_swap(tot_mine, tot_mine, "swap_totals")

    dmods, dngs, dqg, dkg, dcw, dcb = [], [], [], [], [], []
    for l in range(2):
        s0, s1, so, s2, sm = grads[l]["sums"]
        dmods.append(jnp.concatenate([s0[0], s0[1], s0[3], s1[0], s1[1], so[0], s2[0], s2[1], s2[3]]))
        dngs.append(jnp.concatenate([s0[2], s1[2], s2[2]]))
        dqg.append(sm[0].reshape(AW // HD, HD).sum(0))
        dkg.append(sm[1].reshape(AW // HD, HD).sum(0))
        dcw.append(sm[2:5].reshape(-1))
        dcb.append(sm[5])
    small = jnp.concatenate(dmods + dngs + dqg + dkg + dcw + dcb)
    small_all = small_all_gather(_pad_rows(small), "gather_small_grads")[::SUBLANES]
    nm = 9 * D
    dmod_all = small_all[:, :2 * nm].reshape(NDEV, 2, NCHIP, ADA_COLS)
    dmod_mine = lax.dynamic_index_in_dim(dmod_all, chip, axis=2, keepdims=False).transpose(1, 0, 2)
    tot = sum_devices(small_all, "sum_small_grads")[0]
    o = 2 * nm
    g_b_ada = tot[:o].reshape(2, nm)
    g_norm_g = lax.dynamic_slice_in_dim(tot[o:o + 6 * D].reshape(2, 3, D), chip * ngw, ngw, axis=2)
    o += 6 * D
    g_qg = tot[o:o + 2 * HD].reshape(2, HD)
    o += 2 * HD
    g_kg = tot[o:o + 2 * HD].reshape(2, HD)
    o += 2 * HD
    g_cw = lax.dynamic_slice_in_dim(tot[o:o + 6 * CW].reshape(2, 3, CW), chip * cww, cww, axis=2)
    o += 6 * CW
    g_cb = tot[o:o + 2 * CW].reshape(2, CW)

    c_all_t = jnp.concatenate([c_all.T, jnp.zeros((D, LANES - NDEV), F32)], axis=1)
    g_wada_src = wada_grad(c_all_t, dmod_mine, "wada_grad")

    same = [(0, 0), (1, 1)]
    by_layer = [(0, 1), (1, 0)]
    by_layer2 = [(0, 2), (1, 3), (2, 0), (3, 1)]
    r_wada = adamw(w_ada, m_w_ada, v_w_ada, [g_wada_src[0], g_wada_src[1]], same, cflag, "adamw_w_ada")
    r_win = adamw(w_in, m_w_in, v_w_in, [tot_mine[2], tot_sib[2]], by_layer, cflag, "adamw_w_in")
    r_wout = adamw(w_out, m_w_out, v_w_out, [tot_mine[3], tot_sib[3]], by_layer, cflag, "adamw_w_out")
    r_w1 = adamw(ffn_w1.reshape(4, D, HALF), m_ffn_w1.reshape(4, D, HALF), v_ffn_w1.reshape(4, D, HALF),
                 [tot_mine[0], tot_mine[4], tot_sib[0], tot_sib[4]], by_layer2, cflag, "adamw_ffn_w1")
    w2r = DFF // NCHIP
    r_w2 = adamw(ffn_w2.reshape(4, w2r, D), m_ffn_w2.reshape(4, w2r, D), v_ffn_w2.reshape(4, w2r, D),
                 [tot_mine[1], tot_mine[5], tot_sib[1], tot_sib[5]], by_layer2, cflag, "adamw_ffn_w2")
    r_w1 = [t.reshape(ffn_w1.shape) for t in r_w1]
    r_w2 = [t.reshape(ffn_w2.shape) for t in r_w2]

    smalls = [("b_ada", b_ada, m_b_ada, v_b_ada, g_b_ada), ("norm_g", norm_g, m_norm_g, v_norm_g, g_norm_g),
              ("q_norm_g", q_norm_g, m_q_norm_g, v_q_norm_g, g_qg), ("k_norm_g", k_norm_g, m_k_norm_g, v_k_norm_g, g_kg),
              ("conv_w", conv_w, m_conv_w, v_conv_w, g_cw), ("conv_b", conv_b, m_conv_b, v_conv_b, g_cb)]
    n_small = sum(t[1].size for t in smalls)
    pad = (-n_small) % (16 * LANES)

    def packed(idx):
        flat = jnp.concatenate([t[idx].reshape(-1) for t in smalls] + [jnp.zeros((pad,), F32)])
        return flat.reshape(-1, LANES)

    r_small = adamw(packed(1)[None], packed(2)[None], packed(3)[None], [packed(4)], [(0, 0)], cflag, "adamw_small")
    small_out = {}
    o = 0
    for name_, w_, _, _, _ in smalls:
        small_out[name_] = [t.reshape(-1)[o:o + w_.size].reshape(w_.shape) for t in r_small]
        o += w_.size

    res = {"w_ada": r_wada, "w_in": r_win, "w_out": r_wout, "ffn_w1": r_w1, "ffn_w2": r_w2, **small_out}
    order = ["w_ada", "b_ada", "norm_g", "w_in", "q_norm_g", "k_norm_g", "conv_w", "conv_b", "w_out", "ffn_w1", "ffn_w2"]
    outs = [loss, dx[None]]
    for k in range(4):
        outs += [res[nm_][k] for nm_ in order]
    return tuple(outs)
```

```python
import functools

import jax
import jax.numpy as jnp
from jax import lax
from jax.experimental import pallas as pl
from jax.experimental.pallas import tpu as pltpu

F32 = jnp.float32
MXU_DTYPE = jnp.bfloat16
ACT_DTYPE = jnp.bfloat16
WIRE_DTYPE = jnp.bfloat16

D = 1024
HD = 64
AW = 512
CW = 512
DFF = 2816
HALF = DFF // 2
INC = 3 * AW + 3 * CW
NCHIP = 4
NDEV = 8
QBLK = 128
ATTN_CHUNK_ROWS = 2048
DILATIONS = (1, 4, 16)
EPS = 1e-6
NEG = -1e30
LANES = 128
SUBLANES = 8
VMEM_LIMIT = 56 * 1024 * 1024

ADAM_LR = 0.001
ADAM_B1 = 0.9
ADAM_B2 = 0.999
ADAM_EPS = 1e-08
ADAM_WD = 0.01
ADAM_STEP = 10

NT_DIMS = (((1,), (1,)), ((), ()))
TN_DIMS = (((0,), (0,)), ((), ()))


def _params(sem, vmem=VMEM_LIMIT):
    return pltpu.CompilerParams(dimension_semantics=sem, vmem_limit_bytes=vmem)


def _row_tile(n, want):
    t = min(n, want)
    assert n % t == 0
    return t


def _ada(xt, vec_ref):
    ng, sc, sh, gt = vec_ref[0:1, :], vec_ref[1:2, :], vec_ref[2:3, :], vec_ref[3:4, :]
    r = lax.rsqrt(jnp.mean(xt * xt, axis=-1, keepdims=True) + EPS)
    return xt * r, r, ng * (1.0 + sc), ng, sc, sh, gt


def _ada_bwd(dh, xhat, r, gain, ng, sc):
    dshift = jnp.sum(dh, axis=0, keepdims=True)
    dhx = dh * xhat
    dscale = jnp.sum(dhx, axis=0, keepdims=True) * ng
    dng = jnp.sum(dhx, axis=0, keepdims=True) * (1.0 + sc)
    dxhat = dh * gain
    dx = r * (dxhat - xhat * jnp.mean(dxhat * xhat, axis=-1, keepdims=True))
    return dx, dshift, dscale, dng


def _acc_rows(sums_ref, first, rows):
    @pl.when(first)
    def _():
        sums_ref[...] = jnp.zeros_like(sums_ref)
    for k, row in enumerate(rows):
        sums_ref[k:k + 1, :] += row


MESH = pl.DeviceIdType.MESH
ANY = pl.BlockSpec(memory_space=pl.ANY)


def _here():
    return lax.axis_index("x"), lax.axis_index("y"), lax.axis_index("c")


def _ici_copies(src_refs, dst_refs, send_sems, recv_sems, local_sems, scatter):
    x, y, c = _here()
    my_chip = 2 * x + y
    peers = [(1 - x, y), (x, 1 - y), (1 - x, 1 - y)]
    local, out, inc = [], [], []
    for a, (src, dst) in enumerate(zip(src_refs, dst_refs)):
        local.append(pltpu.make_async_copy(src.at[my_chip] if scatter else src, dst.at[my_chip], local_sems.at[a]))
        for j, (px, py) in enumerate(peers):
            sems = dict(send_sem=send_sems.at[3 * a + j], recv_sem=recv_sems.at[3 * a + j],
                        device_id=(px, py, c), device_id_type=MESH)
            out.append(pltpu.make_async_remote_copy(
                src_ref=src.at[2 * px + py] if scatter else src, dst_ref=dst.at[my_chip], **sems))
            inc.append(pltpu.make_async_remote_copy(
                src_ref=src.at[my_chip] if scatter else src, dst_ref=dst.at[2 * px + py], **sems))
    return local, out, inc


def _ici_sems(n):
    return [pltpu.SemaphoreType.DMA((3 * n,)), pltpu.SemaphoreType.DMA((3 * n,)), pltpu.SemaphoreType.DMA((n,))]


def _pcall(body, name, grid, in_specs, out_specs, out_shape, sem, args, carry=()):
    n_in, n_out, nc = len(in_specs), len(out_specs), len(carry)
    if nc == 0:
        outs = pl.pallas_call(body, name=name, grid=grid, in_specs=in_specs, out_specs=out_specs,
                              out_shape=out_shape, compiler_params=_params(sem))(*args)
        return outs, []

    def wrapped(*refs):
        ins, csrc = refs[:n_in], refs[n_in:n_in + nc]
        outs, cdst = refs[n_in + nc:n_in + nc + n_out], refs[n_in + nc + n_out:n_in + 2 * nc + n_out]
        sems = refs[n_in + 2 * nc + n_out:]
        ids = [pl.program_id(a) for a in range(len(grid))]
        first = functools.reduce(jnp.logical_and, [i == 0 for i in ids])
        last = functools.reduce(jnp.logical_and, [i == g - 1 for i, g in zip(ids, grid)])

        @pl.when(first)
        def _():
            local, out, _ = _ici_copies(csrc, cdst, *sems, False)
            for cp in local + out:
                cp.start()

        body(*ins, *outs)

        @pl.when(last)
        def _():
            local, out, inc = _ici_copies(csrc, cdst, *sems, False)
            for cp in inc:
                cp.wait_recv()
            for cp in out:
                cp.wait_send()
            for cp in local:
                cp.wait()

    res = pl.pallas_call(
        wrapped, name=name, grid=grid,
        in_specs=list(in_specs) + [ANY] * nc, out_specs=list(out_specs) + [ANY] * nc,
        out_shape=list(out_shape) + [jax.ShapeDtypeStruct((NCHIP,) + s.shape, s.dtype) for s in carry],
        scratch_shapes=_ici_sems(nc), compiler_params=_params(sem),
    )(*args, *carry)
    return res[:n_out], res[n_out:]


def ffn_fwd(x, vec, w1p, w2, gs, name, carry=()):
    S = x.shape[0]
    tm = _row_tile(S, 512)

    def body(x_ref, vec_ref, w1_ref, w2_ref, xn_ref, a_ref, f_ref):
        xt = x_ref[...]
        xhat, _, gain, _, _, sh, gt = _ada(xt, vec_ref)
        h = (xhat * gain + sh).astype(MXU_DTYPE)
        f = jnp.zeros((tm, D), F32)
        for hf in range(2):
            g = jnp.dot(h, w1_ref[hf], preferred_element_type=F32)
            up = jnp.dot(h, w1_ref[2 + hf], preferred_element_type=F32)
            a_ref[:, hf * HALF:(hf + 1) * HALF] = g.astype(a_ref.dtype)
            a_ref[:, DFF + hf * HALF:DFF + (hf + 1) * HALF] = up.astype(a_ref.dtype)
            act = (g * jax.nn.sigmoid(g) * up).astype(MXU_DTYPE)
            f = f + jnp.dot(act, w2_ref[hf * HALF:(hf + 1) * HALF, :], preferred_element_type=F32)
        xn_ref[...] = xt + (gs * gt) * f
        f_ref[...] = f.astype(f_ref.dtype)

    return _pcall(
        body, name, (S // tm,),
        [pl.BlockSpec((tm, D), lambda i: (i, 0)),
         pl.BlockSpec((SUBLANES, D), lambda i: (0, 0)),
         pl.BlockSpec((NCHIP, D, HALF), lambda i: (0, 0, 0), pipeline_mode=pl.Buffered(1)),
         pl.BlockSpec((DFF, D), lambda i: (0, 0), pipeline_mode=pl.Buffered(1))],
        [pl.BlockSpec((tm, D), lambda i: (i, 0)),
         pl.BlockSpec((tm, 2 * DFF), lambda i: (i, 0)),
         pl.BlockSpec((tm, D), lambda i: (i, 0))],
        [jax.ShapeDtypeStruct((S, D), F32),
         jax.ShapeDtypeStruct((S, 2 * DFF), ACT_DTYPE),
         jax.ShapeDtypeStruct((S, D), ACT_DTYPE)],
        ("arbitrary",), (x, vec, w1p, w2), carry)


def ffn_bwd(dxo, x, a, f, vec, w1p, w2, gs, name):
    S = x.shape[0]
    tm = _row_tile(S, 256)

    def body(dxo_ref, x_ref, a_ref, f_ref, vec_ref, w1_ref, w2_ref,
             dxi_ref, hb_ref, dfb_ref, act_ref, da_ref, sums_ref):
        xt = x_ref[...]
        dxo = dxo_ref[...]
        xhat, r, gain, ng, sc, sh, gt = _ada(xt, vec_ref)
        hb_ref[...] = (xhat * gain + sh).astype(hb_ref.dtype)
        dgate = gs * jnp.sum(dxo * f_ref[...].astype(F32), axis=0, keepdims=True)
        df = ((gs * gt) * dxo).astype(MXU_DTYPE)
        dfb_ref[...] = df
        dh = jnp.zeros((tm, D), F32)
        for hf in range(2):
            lo, hi = hf * HALF, (hf + 1) * HALF
            dact = lax.dot_general(df, w2_ref[lo:hi, :], NT_DIMS, preferred_element_type=F32)
            g = a_ref[:, lo:hi].astype(F32)
            up = a_ref[:, DFF + lo:DFF + hi].astype(F32)
            sg = jax.nn.sigmoid(g)
            si = g * sg
            act_ref[:, lo:hi] = (si * up).astype(act_ref.dtype)
            dg = (dact * up * (sg * (1.0 + g * (1.0 - sg)))).astype(MXU_DTYPE)
            dup = (dact * si).astype(MXU_DTYPE)
            da_ref[:, lo:hi] = dg
            da_ref[:, DFF + lo:DFF + hi] = dup
            dh = dh + lax.dot_general(dg, w1_ref[hf], NT_DIMS, preferred_element_type=F32)
            dh = dh + lax.dot_general(dup, w1_ref[2 + hf], NT_DIMS, preferred_element_type=F32)
        dx, dshift, dscale, dng = _ada_bwd(dh, xhat, r, gain, ng, sc)
        dxi_ref[...] = dxo + dx
        _acc_rows(sums_ref, pl.program_id(0) == 0, (dshift, dscale, dng, dgate))

    return pl.pallas_call(
        body, name=name, grid=(S // tm,),
        in_specs=[pl.BlockSpec((tm, D), lambda i: (i, 0)),
                  pl.BlockSpec((tm, D), lambda i: (i, 0)),
                  pl.BlockSpec((tm, 2 * DFF), lambda i: (i, 0)),
                  pl.BlockSpec((tm, D), lambda i: (i, 0)),
                  pl.BlockSpec((SUBLANES, D), lambda i: (0, 0)),
                  pl.BlockSpec((NCHIP, D, HALF), lambda i: (0, 0, 0), pipeline_mode=pl.Buffered(1)),
                  pl.BlockSpec((DFF, D), lambda i: (0, 0), pipeline_mode=pl.Buffered(1))],
        out_specs=[pl.BlockSpec((tm, D), lambda i: (i, 0)),
                   pl.BlockSpec((tm, D), lambda i: (i, 0)),
                   pl.BlockSpec((tm, D), lambda i: (i, 0)),
                   pl.BlockSpec((tm, DFF), lambda i: (i, 0)),
                   pl.BlockSpec((tm, 2 * DFF), lambda i: (i, 0)),
                   pl.BlockSpec((SUBLANES, D), lambda i: (0, 0))],
        out_shape=[jax.ShapeDtypeStruct((S, D), F32),
                   jax.ShapeDtypeStruct((S, D), MXU_DTYPE),
                   jax.ShapeDtypeStruct((S, D), MXU_DTYPE),
                   jax.ShapeDtypeStruct((S, DFF), MXU_DTYPE),
                   jax.ShapeDtypeStruct((S, 2 * DFF), MXU_DTYPE),
                   jax.ShapeDtypeStruct((SUBLANES, D), F32)],
        compiler_params=_params(("arbitrary",)),
    )(dxo, x, a, f, vec, w1p, w2)


def wgrad(a, b, kt, nt, name):
    T, K = a.shape
    N = b.shape[1]
    pk, pn = K // kt, N // nt
    assert pk == 1 or pn == 1
    tt = _row_tile(T, 1024)
    steps = T // tt

    def body(a_ref, b_ref, o_ref):
        @pl.when(pl.program_id(1) == 0)
        def _():
            o_ref[...] = jnp.zeros_like(o_ref)
        o_ref[...] += lax.dot_general(a_ref[...], b_ref[...], TN_DIMS, preferred_element_type=F32)

    a_map = (lambda p, t: (t, p)) if pk > 1 else (lambda p, t: (t, 0))
    b_map = (lambda p, t: (t, p)) if pn > 1 else (lambda p, t: (t, 0))
    return pl.pallas_call(
        body, name=name, grid=(pk * pn, steps),
        in_specs=[pl.BlockSpec((tt, kt), a_map), pl.BlockSpec((tt, nt), b_map)],
        out_specs=pl.BlockSpec((None, kt, nt), lambda p, t: (p, 0, 0)),
        out_shape=jax.ShapeDtypeStruct((pk * pn, kt, nt), F32),
        compiler_params=_params(("arbitrary", "arbitrary")),
    )(a, b)


def _head_masks(rows):
    lane = lax.broadcasted_iota(jnp.int32, (rows, LANES), 1)
    return lane < HD


def _pair_stat(x, m_a):
    s_a = jnp.sum(jnp.where(m_a, x, 0.0), axis=1, keepdims=True)
    s_b = jnp.sum(jnp.where(m_a, 0.0, x), axis=1, keepdims=True)
    return s_a, s_b


def mixer_in(x, vec, winp, gvec, name):
    S = x.shape[0]
    tm = _row_tile(S, 512)
    pc = INC // NCHIP

    def body(x_ref, vec_ref, w_ref, g_ref, proj_ref, hb_ref, qn_ref, kn_ref, v_ref):
        xt = x_ref[...]
        xhat, _, gain, _, _, sh, _ = _ada(xt, vec_ref)
        h = (xhat * gain + sh).astype(MXU_DTYPE)
        hb_ref[...] = h
        for j in range(NCHIP):
            proj_ref[:, j * pc:(j + 1) * pc] = jnp.dot(h, w_ref[j], preferred_element_type=F32)
        m_a = _head_masks(tm)
        for which, dst in ((0, qn_ref), (1, kn_ref)):
            for p in range(AW // LANES):
                lo = which * AW + p * LANES
                xp = proj_ref[:, lo:lo + LANES]
                s_a, s_b = _pair_stat(xp * xp, m_a)
                rr = jnp.where(m_a, lax.rsqrt(s_a * (1.0 / HD) + EPS), lax.rsqrt(s_b * (1.0 / HD) + EPS))
                gp = g_ref[which:which + 1, p * LANES:(p + 1) * LANES]
                dst[:, p * LANES:(p + 1) * LANES] = (xp * rr * gp).astype(dst.dtype)
        v_ref[...] = proj_ref[:, 2 * AW:3 * AW].astype(v_ref.dtype)

    return pl.pallas_call(
        body, name=name, grid=(S // tm,),
        in_specs=[pl.BlockSpec((tm, D), lambda i: (i, 0)),
                  pl.BlockSpec((SUBLANES, D), lambda i: (0, 0)),
                  pl.BlockSpec((NCHIP, D, pc), lambda i: (0, 0, 0), pipeline_mode=pl.Buffered(1)),
                  pl.BlockSpec((SUBLANES, AW), lambda i: (0, 0))],
        out_specs=[pl.BlockSpec((tm, INC), lambda i: (i, 0)),
                   pl.BlockSpec((tm, D), lambda i: (i, 0)),
                   pl.BlockSpec((tm, AW), lambda i: (i, 0)),
                   pl.BlockSpec((tm, AW), lambda i: (i, 0)),
                   pl.BlockSpec((tm, AW), lambda i: (i, 0))],
        out_shape=[jax.ShapeDtypeStruct((S, INC), F32),
                   jax.ShapeDtypeStruct((S, D), MXU_DTYPE),
                   jax.ShapeDtypeStruct((S, AW), F32),
                   jax.ShapeDtypeStruct((S, AW), F32),
                   jax.ShapeDtypeStruct((S, AW), F32)],
        compiler_params=_params(("arbitrary",)),
    )(x, vec, winp, gvec)


def _band_masks(ncol):
    row = lax.broadcasted_iota(jnp.int32, (2 * QBLK, ncol), 0) & (QBLK - 1)
    col = lax.broadcasted_iota(jnp.int32, (2 * QBLK, ncol), 1)
    return row, col


def _stack_heads(t, m_a):
    zero = jnp.zeros_like(t)
    return jnp.concatenate([jnp.where(m_a, t, zero), jnp.where(m_a, zero, t)], axis=0)


def _attn_qb(d):
    return max(1, min(4, ATTN_CHUNK_ROWS // (QBLK * d)))


def _tile_rows(d, b, r):
    if d == 1:
        return pl.ds(b * QBLK, QBLK)
    return pl.ds(b * QBLK * d + r, QBLK, stride=d)


def _per_residue(d, fn):
    if d == 1:
        fn(0)
    else:
        def step(r, carry):
            fn(r)
            return carry
        lax.fori_loop(0, d, step, 0)


def attn_fwd(qn, kn, v, d, name, carry=()):
    S = qn.shape[0]
    qb = _attn_qb(d)
    halo = QBLK * d
    chunk = qb * halo

    def body(q_ref, kc_ref, kp_ref, vc_ref, vp_ref, o_ref, la_ref, lb_ref):
        i = pl.program_id(1)
        m_a = _head_masks(QBLK)
        row, col = _band_masks(2 * QBLK)
        dist = row + QBLK - col
        band = (dist >= 0) & (dist <= QBLK)
        first = band & ((i > 0) | (col >= QBLK))

        def residue(r):
            kt = [kp_ref[_tile_rows(d, 0, r), :].astype(MXU_DTYPE)]
            vt = [vp_ref[_tile_rows(d, 0, r), :].astype(MXU_DTYPE)]
            for b in range(qb):
                kt.append(kc_ref[_tile_rows(d, b, r), :].astype(MXU_DTYPE))
                vt.append(vc_ref[_tile_rows(d, b, r), :].astype(MXU_DTYPE))
            for b in range(qb):
                rows = _tile_rows(d, b, r)
                q = (q_ref[rows, :] * (HD ** -0.5)).astype(MXU_DTYPE)
                kcat = jnp.concatenate([kt[b], kt[b + 1]], axis=0)
                vcat = jnp.concatenate([vt[b], vt[b + 1]], axis=0)
                mask = first if b == 0 else band
                s = lax.dot_general(_stack_heads(q, m_a), kcat, NT_DIMS, preferred_element_type=F32)
                s = jnp.where(mask, s, NEG)
                m = jnp.max(s, axis=1, keepdims=True)
                p = jnp.exp(s - m)
                l = jnp.sum(p, axis=1, keepdims=True)
                o = jnp.dot(p.astype(MXU_DTYPE), vcat, preferred_element_type=F32) / l
                lse = jnp.broadcast_to(m + jnp.log(l), (2 * QBLK, LANES))
                o_ref[rows, :] = jnp.where(m_a, o[:QBLK], o[QBLK:])
                la_ref[rows, :] = lse[:QBLK]
                lb_ref[rows, :] = lse[QBLK:]

        _per_residue(d, residue)

    cur = pl.BlockSpec((chunk, LANES), lambda hp, i: (i, hp))
    prev = pl.BlockSpec((halo, LANES), lambda hp, i: (jnp.maximum(i * qb - 1, 0), hp))
    return _pcall(body, name, (AW // LANES, S // chunk), [cur, cur, prev, cur, prev], [cur, cur, cur],
                  [jax.ShapeDtypeStruct((S, AW), F32)] * 3, ("arbitrary", "arbitrary"), (qn, kn, kn, v, v), carry)


def attn_bwd(qn, kn, v, dycat, lse_a, lse_b, dl_a, dl_b, d, name):
    S = qn.shape[0]
    qb = _attn_qb(d)
    halo = QBLK * d
    chunk = qb * halo
    nhalo = S // halo
    nchunk = S // chunk

    def body(q_ref, qx_ref, kc_ref, kp_ref, vc_ref, vp_ref, do_ref, dox_ref, la_ref, lax_ref, lb_ref, lbx_ref,
             da_ref, dax_ref, db_ref, dbx_ref, dq_ref, dk_ref, dv_ref):
        i = pl.program_id(1)
        has_next = i < nchunk - 1
        m_a = _head_masks(QBLK)
        row, col = _band_masks(2 * QBLK)
        dist = row + QBLK - col
        band = (dist >= 0) & (dist <= QBLK)
        first = band & ((i > 0) | (col >= QBLK))
        row1, col1 = _band_masks(QBLK)
        off_only = (col1 >= row1) & has_next

        def residue(r):
            def tiles(cur_ref, next_ref, cast):
                out = [cur_ref[_tile_rows(d, b, r), :] for b in range(qb)] + [next_ref[_tile_rows(d, 0, r), :]]
                return [t.astype(MXU_DTYPE) for t in out] if cast else out

            def ktiles(cur_ref, prev_ref):
                out = [prev_ref[_tile_rows(d, 0, r), :]] + [cur_ref[_tile_rows(d, b, r), :] for b in range(qb)]
                return [t.astype(MXU_DTYPE) for t in out]

            qt = [(t * (HD ** -0.5)).astype(MXU_DTYPE) for t in tiles(q_ref, qx_ref, False)]
            dot_ = tiles(do_ref, dox_ref, True)
            lse_t = list(zip(tiles(la_ref, lax_ref, False), tiles(lb_ref, lbx_ref, False)))
            dl_t = list(zip(tiles(da_ref, dax_ref, False), tiles(db_ref, dbx_ref, False)))
            kt = ktiles(kc_ref, kp_ref)
            vt = ktiles(vc_ref, vp_ref)
            dk_acc = [jnp.zeros((QBLK, LANES), F32) for _ in range(qb)]
            dv_acc = [jnp.zeros((QBLK, LANES), F32) for _ in range(qb)]
            for x in range(qb + 1):
                parts = 2 if x < qb else 1
                if parts == 2:
                    kcat = jnp.concatenate([kt[x], kt[x + 1]], axis=0)
                    vcat = jnp.concatenate([vt[x], vt[x + 1]], axis=0)
                else:
                    kcat, vcat = kt[x], vt[x]
                mask = first if x == 0 else (band if x < qb else off_only)
                q2 = _stack_heads(qt[x], m_a)
                do2 = _stack_heads(dot_[x], m_a)
                lse2 = jnp.concatenate(lse_t[x], axis=0)
                dl2 = jnp.concatenate(dl_t[x], axis=0)
                if parts == 2:
                    lse2 = jnp.concatenate([lse2, lse2], axis=1)
                    dl2 = jnp.concatenate([dl2, dl2], axis=1)
                s = lax.dot_general(q2, kcat, NT_DIMS, preferred_element_type=F32)
                p = jnp.exp(jnp.where(mask, s, NEG) - lse2)
                dp = lax.dot_general(do2, vcat, NT_DIMS, preferred_element_type=F32)
                ds = p * (dp - dl2)
                if x < qb:
                    dq = jnp.dot(ds.astype(MXU_DTYPE), kcat, preferred_element_type=F32)
                    dq_ref[_tile_rows(d, x, r), :] = jnp.where(m_a, dq[:QBLK], dq[QBLK:]) * (HD ** -0.5)
                ds_t = ds.T.astype(MXU_DTYPE)
                p_t = p.T.astype(MXU_DTYPE)
                for part in range(parts):
                    kb = x - 1 + part
                    if 0 <= kb < qb:
                        sl = slice(part * QBLK, (part + 1) * QBLK)
                        dk_acc[kb] = dk_acc[kb] + jnp.dot(ds_t[sl], q2, preferred_element_type=F32)
                        dv_acc[kb] = dv_acc[kb] + jnp.dot(p_t[sl], do2, preferred_element_type=F32)
            for kb in range(qb):
                dk_ref[_tile_rows(d, kb, r), :] = dk_acc[kb]
                dv_ref[_tile_rows(d, kb, r), :] = dv_acc[kb]

        _per_residue(d, residue)

    def nxt(i):
        return jnp.minimum((i + 1) * qb, nhalo - 1)

    cur = pl.BlockSpec((chunk, LANES), lambda hp, i: (i, hp))
    prev = pl.BlockSpec((halo, LANES), lambda hp, i: (jnp.maximum(i * qb - 1, 0), hp))
    nx = pl.BlockSpec((halo, LANES), lambda hp, i: (nxt(i), hp))
    return pl.pallas_call(
        body, name=name, grid=(AW // LANES, nchunk),
        in_specs=[cur, nx, cur, prev, cur, prev, cur, nx, cur, nx, cur, nx, cur, nx, cur, nx],
        out_specs=[cur, cur, cur],
        out_shape=[jax.ShapeDtypeStruct((S, AW), F32)] * 3,
        compiler_params=_params(("arbitrary", "arbitrary")),
    )(qn, qn, kn, kn, v, v, dycat, dycat, lse_a, lse_a, lse_b, lse_b, dl_a, dl_a, dl_b, dl_b)


def _shift_down(x, halo_prev, k, row):
    tm = x.shape[0]
    tail = jnp.concatenate([pltpu.roll(halo_prev, k, 0), jnp.zeros((tm - SUBLANES, x.shape[1]), x.dtype)], axis=0)
    return jnp.where(row < k, tail, pltpu.roll(x, k, 0))


def _shift_up(x, halo_next, k, row):
    tm = x.shape[0]
    head = jnp.concatenate([jnp.zeros((tm - SUBLANES, x.shape[1]), x.dtype), pltpu.roll(halo_next, SUBLANES - k, 0)], axis=0)
    return jnp.where(row >= tm - k, head, pltpu.roll(x, tm - k, 0))


def _conv_fwd(cu, halo_cu, cw_ref, row):
    u1 = _shift_down(cu, halo_cu, 1, row)
    u2 = _shift_down(cu, halo_cu, 2, row)
    cv = cw_ref[0:1, :] * u2 + cw_ref[1:2, :] * u1 + cw_ref[2:3, :] * cu + cw_ref[3:4, :]
    return cv, u1, u2


def combine_conv(os_, lses_a, lses_b, proj, cw, name, carry=()):
    S = proj.shape[0]
    tm = _row_tile(S, 512)
    hb = tm // SUBLANES

    def body(o1, o2, o3, a1, a2, a3, b1, b2, b3, pc_ref, ph_ref, cw_ref, ycat_ref, la_ref, lb_ref):
        i = pl.program_id(0)
        m_a = _head_masks(tm)
        for p in range(AW // LANES):
            cs = slice(p * LANES, (p + 1) * LANES)
            tot = []
            for srcs, dst in (((a1, a2, a3), la_ref), ((b1, b2, b3), lb_ref)):
                ls = [l[:, cs] for l in srcs]
                mx = jnp.maximum(jnp.maximum(ls[0], ls[1]), ls[2])
                t = mx + jnp.log(jnp.exp(ls[0] - mx) + jnp.exp(ls[1] - mx) + jnp.exp(ls[2] - mx))
                dst[:, cs] = t
                tot.append((ls, t))
            acc = jnp.zeros((tm, LANES), F32)
            for r, o in enumerate((o1, o2, o3)):
                w = jnp.where(m_a, jnp.exp(tot[0][0][r] - tot[0][1]), jnp.exp(tot[1][0][r] - tot[1][1]))
                acc = acc + w * o[:, cs]
            ycat_ref[:, cs] = acc.astype(ycat_ref.dtype)
        row = lax.broadcasted_iota(jnp.int32, (tm, CW), 0)
        gb, gc, u = pc_ref[:, 0:CW], pc_ref[:, CW:2 * CW], pc_ref[:, 2 * CW:3 * CW]
        halo_cu = jnp.where(i > 0, ph_ref[:, CW:2 * CW] * ph_ref[:, 2 * CW:3 * CW], 0.0)
        cv, _, _ = _conv_fwd(gc * u, halo_cu, cw_ref, row)
        ycat_ref[:, AW:AW + CW] = (gb * cv).astype(ycat_ref.dtype)

    ot = pl.BlockSpec((tm, AW), lambda i: (i, 0))
    return _pcall(
        body, name, (S // tm,),
        [ot] * 9 + [pl.BlockSpec((tm, 3 * CW), lambda i: (i, 1)),
                    pl.BlockSpec((SUBLANES, 3 * CW), lambda i: (jnp.maximum(i * hb - 1, 0), 1)),
                    pl.BlockSpec((SUBLANES, CW), lambda i: (0, 0))],
        [pl.BlockSpec((tm, D), lambda i: (i, 0)), ot, ot],
        [jax.ShapeDtypeStruct((S, D), ACT_DTYPE), jax.ShapeDtypeStruct((S, AW), F32),
         jax.ShapeDtypeStruct((S, AW), F32)],
        ("arbitrary",), (*os_, *lses_a, *lses_b, proj, proj, cw), carry)


def out_proj(ycat, x, vec, wout, name):
    S = x.shape[0]
    tm = _row_tile(S, 512)

    def body(yc_ref, x_ref, vec_ref, w_ref, xn_ref, y_ref):
        y = jnp.dot(yc_ref[...].astype(MXU_DTYPE), w_ref[...], preferred_element_type=F32)
        xn_ref[...] = x_ref[...] + vec_ref[3:4, :] * y
        y_ref[...] = y.astype(y_ref.dtype)

    t = pl.BlockSpec((tm, D), lambda i: (i, 0))
    return pl.pallas_call(
        body, name=name, grid=(S // tm,),
        in_specs=[t, t, pl.BlockSpec((SUBLANES, D), lambda i: (0, 0)),
                  pl.BlockSpec((D, D), lambda i: (0, 0))],
        out_specs=[t, t],
        out_shape=[jax.ShapeDtypeStruct((S, D), F32), jax.ShapeDtypeStruct((S, D), ACT_DTYPE)],
        compiler_params=_params(("arbitrary",)),
    )(ycat, x, vec, wout)


def out_proj_bwd(dxo, y, ycat, vec, wout, name):
    S = dxo.shape[0]
    tm = _row_tile(S, 512)

    def body(dxo_ref, y_ref, yc_ref, vec_ref, w_ref, dyb_ref, dyc_ref, da_ref, db_ref, sums_ref):
        dxo = dxo_ref[...]
        dgate = jnp.sum(dxo * y_ref[...].astype(F32), axis=0, keepdims=True)
        dy = (vec_ref[3:4, :] * dxo).astype(MXU_DTYPE)
        dyb_ref[...] = dy
        dyc_ref[...] = lax.dot_general(dy, w_ref[...], NT_DIMS, preferred_element_type=F32)
        m_a = _head_masks(tm)
        for p in range(AW // LANES):
            cs = slice(p * LANES, (p + 1) * LANES)
            s_a, s_b = _pair_stat(dyc_ref[:, cs] * yc_ref[:, cs].astype(F32), m_a)
            da_ref[:, cs] = jnp.broadcast_to(s_a, (tm, LANES))
            db_ref[:, cs] = jnp.broadcast_to(s_b, (tm, LANES))
        _acc_rows(sums_ref, pl.program_id(0) == 0, (dgate,))

    t = pl.BlockSpec((tm, D), lambda i: (i, 0))
    at = pl.BlockSpec((tm, AW), lambda i: (i, 0))
    return pl.pallas_call(
        body, name=name, grid=(S // tm,),
        in_specs=[t, t, t, pl.BlockSpec((SUBLANES, D), lambda i: (0, 0)),
                  pl.BlockSpec((D, D), lambda i: (0, 0))],
        out_specs=[t, t, at, at, pl.BlockSpec((SUBLANES, D), lambda i: (0, 0))],
        out_shape=[jax.ShapeDtypeStruct((S, D), MXU_DTYPE), jax.ShapeDtypeStruct((S, D), F32),
                   jax.ShapeDtypeStruct((S, AW), F32), jax.ShapeDtypeStruct((S, AW), F32),
                   jax.ShapeDtypeStruct((SUBLANES, D), F32)],
        compiler_params=_params(("arbitrary",)),
    )(dxo, y, ycat, vec, wout)


def mixer_mid_bwd(dqs, dks, dvs, proj, dycat, gvec, cw, name):
    S = proj.shape[0]
    tm = _row_tile(S, 256)
    hb = tm // SUBLANES
    nsl = S // SUBLANES
    ntile = S // tm

    def body(dq1, dq2, dq3, dk1, dk2, dk3, dv1, dv2, dv3, pr_ref, pp_ref, pn_ref, dyc_ref, dyn_ref,
             g_ref, cw_ref, dp_ref, sums_ref):
        i = pl.program_id(0)
        m_a = _head_masks(tm)
        gsum = []
        for which, parts in ((0, (dq1, dq2, dq3)), (1, (dk1, dk2, dk3))):
            acc_g = []
            for p in range(AW // LANES):
                lo = which * AW + p * LANES
                cs = slice(p * LANES, (p + 1) * LANES)
                xp = pr_ref[:, lo:lo + LANES]
                s_a, s_b = _pair_stat(xp * xp, m_a)
                rr = jnp.where(m_a, lax.rsqrt(s_a * (1.0 / HD) + EPS), lax.rsqrt(s_b * (1.0 / HD) + EPS))
                xh = xp * rr
                dn = parts[0][:, cs] + parts[1][:, cs] + parts[2][:, cs]
                acc_g.append(jnp.sum(dn * xh, axis=0, keepdims=True))
                t = dn * g_ref[which:which + 1, cs]
                t_a, t_b = _pair_stat(t * xh, m_a)
                mean = jnp.where(m_a, t_a, t_b) * (1.0 / HD)
                dp_ref[:, lo:lo + LANES] = (rr * (t - xh * mean)).astype(dp_ref.dtype)
            gsum.append(jnp.concatenate(acc_g, axis=1))
        dp_ref[:, 2 * AW:3 * AW] = (dv1[...] + dv2[...] + dv3[...]).astype(dp_ref.dtype)
        row = lax.broadcasted_iota(jnp.int32, (tm, CW), 0)
        base = 3 * AW
        gb, gc, u = pr_ref[:, base:base + CW], pr_ref[:, base + CW:base + 2 * CW], pr_ref[:, base + 2 * CW:base + 3 * CW]
        cu = gc * u
        halo_cu = jnp.where(i > 0, pp_ref[:, CW:2 * CW] * pp_ref[:, 2 * CW:3 * CW], 0.0)
        cv, u1, u2 = _conv_fwd(cu, halo_cu, cw_ref, row)
        dyc = dyc_ref[...]
        dp_ref[:, base:base + CW] = (dyc * cv).astype(dp_ref.dtype)
        dcv = dyc * gb
        halo_dcv = jnp.where(i < ntile - 1, dyn_ref[...] * pn_ref[:, 0:CW], 0.0)
        d1 = _shift_up(dcv, halo_dcv, 1, row)
        d2 = _shift_up(dcv, halo_dcv, 2, row)
        dcu = cw_ref[2:3, :] * dcv + cw_ref[1:2, :] * d1 + cw_ref[0:1, :] * d2
        dp_ref[:, base + CW:base + 2 * CW] = (dcu * u).astype(dp_ref.dtype)
        dp_ref[:, base + 2 * CW:base + 3 * CW] = (dcu * gc).astype(dp_ref.dtype)
        rows = (gsum[0], gsum[1],
                jnp.sum(dcv * u2, axis=0, keepdims=True), jnp.sum(dcv * u1, axis=0, keepdims=True),
                jnp.sum(dcv * cu, axis=0, keepdims=True), jnp.sum(dcv, axis=0, keepdims=True))
        _acc_rows(sums_ref, i == 0, rows)

    at = pl.BlockSpec((tm, AW), lambda i: (i, 0))
    return pl.pallas_call(
        body, name=name, grid=(ntile,),
        in_specs=[at] * 9 + [
            pl.BlockSpec((tm, INC), lambda i: (i, 0)),
            pl.BlockSpec((SUBLANES, 3 * CW), lambda i: (jnp.maximum(i * hb - 1, 0), 1)),
            pl.BlockSpec((SUBLANES, 3 * CW), lambda i: (jnp.minimum((i + 1) * hb, nsl - 1), 1)),
            pl.BlockSpec((tm, CW), lambda i: (i, 1)),
            pl.BlockSpec((SUBLANES, CW), lambda i: (jnp.minimum((i + 1) * hb, nsl - 1), 1)),
            pl.BlockSpec((SUBLANES, AW), lambda i: (0, 0)),
            pl.BlockSpec((SUBLANES, CW), lambda i: (0, 0))],
        out_specs=[pl.BlockSpec((tm, INC), lambda i: (i, 0)),
                   pl.BlockSpec((SUBLANES, AW), lambda i: (0, 0))],
        out_shape=[jax.ShapeDtypeStruct((S, INC), MXU_DTYPE), jax.ShapeDtypeStruct((SUBLANES, AW), F32)],
        compiler_params=_params(("arbitrary",)),
    )(*dqs, *dks, *dvs, proj, proj, proj, dycat, dycat, gvec, cw)


def mixer_in_bwd(dxo, x, dproj, vec, winp, name):
    S = x.shape[0]
    tm = _row_tile(S, 512)
    pc = INC // NCHIP

    def body(dxo_ref, x_ref, dp_ref, vec_ref, w_ref, dxi_ref, sums_ref):
        xhat, r, gain, ng, sc, _, _ = _ada(x_ref[...], vec_ref)
        dh = jnp.zeros((tm, D), F32)
        for j in range(NCHIP):
            dh = dh + lax.dot_general(dp_ref[:, j * pc:(j + 1) * pc], w_ref[j], NT_DIMS, preferred_element_type=F32)
        dx, dshift, dscale, dng = _ada_bwd(dh, xhat, r, gain, ng, sc)
        dxi_ref[...] = dxo_ref[...] + dx
        _acc_rows(sums_ref, pl.program_id(0) == 0, (dshift, dscale, dng))

    t = pl.BlockSpec((tm, D), lambda i: (i, 0))
    return pl.pallas_call(
        body, name=name, grid=(S // tm,),
        in_specs=[t, t, pl.BlockSpec((tm, INC), lambda i: (i, 0)),
                  pl.BlockSpec((SUBLANES, D), lambda i: (0, 0)),
                  pl.BlockSpec((NCHIP, D, pc), lambda i: (0, 0, 0), pipeline_mode=pl.Buffered(1))],
        out_specs=[t, pl.BlockSpec((SUBLANES, D), lambda i: (0, 0))],
        out_shape=[jax.ShapeDtypeStruct((S, D), F32), jax.ShapeDtypeStruct((SUBLANES, D), F32)],
        compiler_params=_params(("arbitrary",)),
    )(dxo, x, dproj, vec, winp)


def loss_head(xf, target, name):
    S = xf.shape[0]
    tm = _row_tile(S, 1024)

    def body(x_ref, t_ref, dy_ref, l_ref):
        diff = x_ref[...] - t_ref[...]
        dy_ref[...] = diff * (1.0 / D)
        part = jnp.sum(jnp.sum(diff * diff, axis=0, keepdims=True), axis=1, keepdims=True) * (0.5 / D)

        @pl.when(pl.program_id(0) == 0)
        def _():
            l_ref[...] = jnp.zeros_like(l_ref)
        l_ref[...] += jnp.broadcast_to(part, l_ref.shape)

    t = pl.BlockSpec((tm, D), lambda i: (i, 0))
    return pl.pallas_call(
        body, name=name, grid=(S // tm,),
        in_specs=[t, t],
        out_specs=[t, pl.BlockSpec((SUBLANES, LANES), lambda i: (0, 0))],
        out_shape=[jax.ShapeDtypeStruct((S, D), F32), jax.ShapeDtypeStruct((SUBLANES, LANES), F32)],
        compiler_params=_params(("arbitrary",)),
    )(xf, target)


def _vec(mod_l, ng_l, i):
    m = mod_l.reshape(3, 3, D)
    rows = jnp.stack([ng_l[i], m[i, 1], m[i, 0], m[i, 2]])
    return jnp.concatenate([rows, jnp.zeros((SUBLANES - 4, D), F32)], axis=0)


def local_step(x, target, mods, ngs, gvecs, cws, shards, w_first):
    saved = []
    weights = [dict(w1=[None, None], w2=[None, None]) for _ in range(2)]
    weights[0]["w1"][0], weights[0]["w2"][0] = w_first[0], w_first[1].reshape(DFF, D)
    h = x
    for l in range(2):
        w, sh = weights[l], shards[l]
        nxt = shards[l + 1] if l == 0 else None
        vecs = [_vec(mods[l], ngs[l], i) for i in range(3)]
        x0 = h
        (x1, a0, f0), (win, wout) = ffn_fwd(x0, vecs[0], w["w1"][0], w["w2"][0], 0.5, f"ffn_fwd_l{l}a",
                                            carry=[sh["win"], sh["wout"]])
        w["win"], w["wout"] = win, wout.reshape(D, D)
        proj, h1b, qn, kn, v = mixer_in(x1, vecs[1], w["win"], gvecs[l], f"mixer_in_l{l}")
        os_, lses_a, lses_b = [], [], []
        for d in DILATIONS:
            carry = {1: [sh["w2"][1]], 16: [sh["w1"][1]]}.get(d, [])
            (o, la, lb), got = attn_fwd(qn, kn, v, d, f"attn_fwd_l{l}_d{d}", carry=carry)
            if d == 1:
                w["w2"][1] = got[0].reshape(DFF, D)
            if d == 16:
                w["w1"][1] = got[0]
            os_.append(o)
            lses_a.append(la)
            lses_b.append(lb)
        (ycat, *lse), got = combine_conv(os_, lses_a, lses_b, proj, cws[l], f"combine_conv_l{l}",
                                         carry=[nxt["w2"][0]] if nxt else [])
        if nxt:
            weights[1]["w2"][0] = got[0].reshape(DFF, D)
        x2, y = out_proj(ycat, x1, vecs[1], w["wout"], f"out_proj_l{l}")
        (x3, a2, f2), got = ffn_fwd(x2, vecs[2], w["w1"][1], w["w2"][1], 0.5, f"ffn_fwd_l{l}b",
                                    carry=[nxt["w1"][0]] if nxt else [])
        if nxt:
            weights[1]["w1"][0] = got[0]
        saved.append(dict(vecs=vecs, x0=x0, a0=a0, f0=f0, x1=x1, proj=proj, h1b=h1b, qn=qn, kn=kn, v=v,
                          ycat=ycat, lse=lse, y=y, x2=x2, a2=a2, f2=f2))
        h = x3
    dx, loss_blk = loss_head(h, target, "loss_head")
    grads = [None, None]
    for l in (1, 0):
        w, s = weights[l], saved[l]
        vecs = s["vecs"]
        dx, hb, dfb, act, da, sums2 = ffn_bwd(dx, s["x2"], s["a2"], s["f2"], vecs[2], w["w1"][1], w["w2"][1],
                                              0.5, f"ffn_bwd_l{l}b")
        dw1b = wgrad(hb, da, D, HALF, f"wgrad_w1_l{l}b")
        dw2b = wgrad(act, dfb, HALF, D, f"wgrad_w2_l{l}b")
        dyb, dycat, dl_a, dl_b, sums_o = out_proj_bwd(dx, s["y"], s["ycat"], vecs[1], w["wout"], f"out_proj_bwd_l{l}")
        dwout = wgrad(s["ycat"].astype(MXU_DTYPE), dyb, D // 2, D, f"wgrad_wout_l{l}")
        dqs, dks, dvs = [], [], []
        for d in DILATIONS:
            dq, dk, dv = attn_bwd(s["qn"], s["kn"], s["v"], dycat, s["lse"][0], s["lse"][1], dl_a, dl_b, d,
                                  f"attn_bwd_l{l}_d{d}")
            dqs.append(dq)
            dks.append(dk)
            dvs.append(dv)
        dproj, sums_m = mixer_mid_bwd(dqs, dks, dvs, s["proj"], dycat, gvecs[l], cws[l], f"mixer_mid_bwd_l{l}")
        dwin = wgrad(s["h1b"], dproj, D, INC // NCHIP, f"wgrad_win_l{l}")
        dx, sums1 = mixer_in_bwd(dx, s["x1"], dproj, vecs[1], w["win"], f"mixer_in_bwd_l{l}")
        dx, hb, dfb, act, da, sums0 = ffn_bwd(dx, s["x0"], s["a0"], s["f0"], vecs[0], w["w1"][0], w["w2"][0],
                                              0.5, f"ffn_bwd_l{l}a")
        dw1a = wgrad(hb, da, D, HALF, f"wgrad_w1_l{l}a")
        dw2a = wgrad(act, dfb, HALF, D, f"wgrad_w2_l{l}a")
        grads[l] = dict(w1=[dw1a, dw1b], w2=[dw2a, dw2b], win=dwin, wout=dwout,
                        sums=(sums0, sums1, sums_o, sums2, sums_m))
    return loss_blk, dx, grads


def small_all_gather(blk, name):
    m_per, n = blk.shape

    def body(x_ref, out_ref, send_sems, recv_sems, local_sem):
        x, y, c = _here()
        me, sibling = (x, y, c), (x, y, 1 - c)
        chips = [(1 - x, y), (x, 1 - y), (1 - x, 1 - y)]

        def rows(px, py, pc):
            return out_ref.at[pl.ds((4 * px + 2 * py + pc) * m_per, m_per), :]

        def copy(k, block, to, src=None):
            return pltpu.make_async_remote_copy(
                src_ref=rows(*block) if src is None else src, dst_ref=rows(*block),
                send_sem=send_sems.at[k], recv_sem=recv_sems.at[k], device_id=to, device_id_type=MESH)

        mine = pltpu.make_async_copy(x_ref, rows(*me), local_sem)
        mine.start()
        first = [copy(0, me, sibling, src=x_ref)]
        first += [copy(1 + j, me, (*chip, c), src=x_ref) for j, chip in enumerate(chips)]
        for cp in first:
            cp.start()
        passed = [copy(4 + j, (*chip, c), sibling) for j, chip in enumerate(chips)]
        for j, chip in enumerate(chips):
            copy(1 + j, (*chip, c), me).wait_recv()
            passed[j].start()
        copy(0, sibling, me).wait_recv()
        for j, chip in enumerate(chips):
            copy(4 + j, (*chip, 1 - c), me).wait_recv()
        for cp in first + passed:
            cp.wait_send()
        mine.wait()

    return pl.pallas_call(
        body, name=name,
        out_shape=jax.ShapeDtypeStruct((NDEV * m_per, n), blk.dtype),
        in_specs=[pl.BlockSpec(memory_space=pltpu.VMEM)],
        out_specs=pl.BlockSpec(memory_space=pltpu.VMEM),
        scratch_shapes=[pltpu.SemaphoreType.DMA((7,)), pltpu.SemaphoreType.DMA((7,)), pltpu.SemaphoreType.DMA],
        compiler_params=pltpu.CompilerParams(vmem_limit_bytes=VMEM_LIMIT),
    )(blk)


def ici_exchange(srcs, scatter, name):
    n = len(srcs)

    def body(*refs):
        local, out, inc = _ici_copies(refs[:n], refs[n:2 * n], *refs[2 * n:], scatter)
        for cp in local + out:
            cp.start()
        for cp in inc:
            cp.wait_recv()
        for cp in out:
            cp.wait_send()
        for cp in local:
            cp.wait()

    out_shape = [jax.ShapeDtypeStruct(s.shape if scatter else (NCHIP,) + s.shape, s.dtype) for s in srcs]
    return pl.pallas_call(
        body, name=name, out_shape=out_shape,
        in_specs=[ANY] * n, out_specs=[ANY] * n, scratch_shapes=_ici_sems(n),
    )(*srcs)


def d2d_swap(send_if_c0, send_if_c1, name):
    n = len(send_if_c0)

    def body(*refs):
        a_refs, b_refs, dst_refs = refs[:n], refs[n:2 * n], refs[2 * n:3 * n]
        send_sems, recv_sems = refs[3 * n:]
        x, y, c = _here()

        def copy(k, src):
            return pltpu.make_async_remote_copy(
                src_ref=src, dst_ref=dst_refs[k], send_sem=send_sems.at[k], recv_sem=recv_sems.at[k],
                device_id=(x, y, 1 - c), device_id_type=MESH)

        @pl.when(c == 0)
        def _():
            for k in range(n):
                copy(k, a_refs[k]).start()

        @pl.when(c == 1)
        def _():
            for k in range(n):
                copy(k, b_refs[k]).start()

        for k in range(n):
            copy(k, a_refs[k]).wait_recv()
        for k in range(n):
            copy(k, a_refs[k]).wait_send()

    return pl.pallas_call(
        body, name=name, out_shape=[jax.ShapeDtypeStruct(s.shape, s.dtype) for s in send_if_c0],
        in_specs=[ANY] * (2 * n), out_specs=[ANY] * n,
        scratch_shapes=[pltpu.SemaphoreType.DMA((n,)), pltpu.SemaphoreType.DMA((n,))],
    )(*send_if_c0, *send_if_c1)


EW_BLOCK_BYTES = 1 << 20


def _ew_rows(rows, cols):
    want = max(16, EW_BLOCK_BYTES // (4 * cols))
    best = None
    for t in range(16, rows + 1, 16):
        if rows % t == 0 and t <= want:
            best = t
    return best if best is not None else rows


def add_select(g0, g1, recv, cflag, name):
    shape = g0.shape
    cols = shape[-1]
    rows = g0.size // cols
    tr = _ew_rows(rows, cols)

    def body(c_ref, a_ref, b_ref, r_ref, o_ref):
        mine = jnp.where(c_ref[0] == 0, a_ref[...], b_ref[...])
        o_ref[...] = (mine + r_ref[...]).astype(o_ref.dtype)

    t = pl.BlockSpec((tr, cols), lambda i: (i, 0))
    out = pl.pallas_call(
        body, name=name, grid=(rows // tr,),
        in_specs=[pl.BlockSpec(memory_space=pltpu.SMEM), t, t, t], out_specs=t,
        out_shape=jax.ShapeDtypeStruct((rows, cols), WIRE_DTYPE),
        compiler_params=_params(("arbitrary",)),
    )(cflag, g0.reshape(rows, cols), g1.reshape(rows, cols), recv.reshape(rows, cols))
    return out.reshape(shape)


def sum_chips(recv, name):
    _, r, cols = recv.shape
    tr = _ew_rows(r, cols)

    def body(r_ref, o_ref):
        acc = r_ref[0].astype(F32)
        for k in range(1, NCHIP):
            acc = acc + r_ref[k].astype(F32)
        o_ref[...] = acc

    return pl.pallas_call(
        body, name=name, grid=(r // tr,),
        in_specs=[pl.BlockSpec((NCHIP, tr, cols), lambda i: (0, i, 0))],
        out_specs=pl.BlockSpec((tr, cols), lambda i: (i, 0)),
        out_shape=jax.ShapeDtypeStruct((r, cols), F32),
        compiler_params=_params(("arbitrary",)),
    )(recv)


def sum_devices(rows8, name):
    def body(r_ref, o_ref):
        acc = r_ref[0:1, :]
        for k in range(1, NDEV):
            acc = acc + r_ref[k:k + 1, :]
        o_ref[...] = jnp.broadcast_to(acc, o_ref.shape)

    return pl.pallas_call(
        body, name=name, out_shape=jax.ShapeDtypeStruct(rows8.shape, F32),
        in_specs=[pl.BlockSpec(memory_space=pltpu.VMEM)], out_specs=pl.BlockSpec(memory_space=pltpu.VMEM),
        compiler_params=pltpu.CompilerParams(vmem_limit_bytes=VMEM_LIMIT),
    )(rows8)


def adamw(w, m, v, srcs, table, cflag, name):
    planes, r, cols = w.shape
    tr = _ew_rows(r, cols)
    ns = len(srcs)

    def body(c_ref, w_ref, m_ref, v_ref, *rest):
        s_refs, (g_ref, d_ref, mo_ref, vo_ref) = rest[:ns], rest[ns:]
        p = pl.program_id(0)
        on_c0 = c_ref[0] == 0
        want = jnp.int32(0)
        for pp, (t0, t1) in enumerate(table):
            want = jnp.where(p == pp, jnp.where(on_c0, t0, t1), want)
        g = s_refs[0][...]
        for k in range(1, ns):
            g = jnp.where(want == k, s_refs[k][...], g)
        g_ref[...] = g
        m_new = ADAM_B1 * m_ref[...] + (1.0 - ADAM_B1) * g
        v_new = ADAM_B2 * v_ref[...] + (1.0 - ADAM_B2) * (g * g)
        mo_ref[...] = m_new
        vo_ref[...] = v_new
        m_hat = m_new / (1.0 - ADAM_B1 ** ADAM_STEP)
        v_hat = v_new / (1.0 - ADAM_B2 ** ADAM_STEP)
        d_ref[...] = -ADAM_LR * (m_hat / (jnp.sqrt(v_hat) + ADAM_EPS) + ADAM_WD * w_ref[...])

    pt = pl.BlockSpec((None, tr, cols), lambda p, i: (p, i, 0))
    st = pl.BlockSpec((tr, cols), lambda p, i: (i, 0))
    return pl.pallas_call(
        body, name=name, grid=(planes, r // tr),
        in_specs=[pl.BlockSpec(memory_space=pltpu.SMEM), pt, pt, pt] + [st] * ns,
        out_specs=[pt] * 4,
        out_shape=[jax.ShapeDtypeStruct(w.shape, F32)] * 4,
        compiler_params=_params(("arbitrary", "arbitrary")),
    )(cflag, w, m, v, *srcs)


ADA_COLS = 9 * D // NCHIP


def mod_fwd(c_all, w_ada, b_shard, name):
    def body(c_ref, w_ref, b_ref, o_ref):
        cc = c_ref[...]
        sc = cc * jax.nn.sigmoid(cc)
        o_ref[...] = jnp.dot(sc, w_ref[...], preferred_element_type=F32,
                             precision=lax.Precision.HIGHEST) + b_ref[...]

    return pl.pallas_call(
        body, name=name, grid=(2,),
        in_specs=[pl.BlockSpec((NDEV, D), lambda l: (0, 0)),
                  pl.BlockSpec((None, D, ADA_COLS), lambda l: (l, 0, 0)),
                  pl.BlockSpec((None, 1, ADA_COLS), lambda l: (l, 0, 0))],
        out_specs=pl.BlockSpec((None, NDEV, ADA_COLS), lambda l: (l, 0, 0)),
        out_shape=jax.ShapeDtypeStruct((2, NDEV, ADA_COLS), F32),
        compiler_params=_params(("arbitrary",)),
    )(c_all, w_ada, b_shard.reshape(2, 1, ADA_COLS))


def wada_grad(c_all_t, dmod, name):
    ct = ADA_COLS // 3

    def body(c_ref, d_ref, o_ref):
        cc = c_ref[...]
        sc = cc * jax.nn.sigmoid(cc)
        acc = sc[:, 0:1] * d_ref[0:1, :]
        for b in range(1, NDEV):
            acc = acc + sc[:, b:b + 1] * d_ref[b:b + 1, :]
        o_ref[...] = acc

    return pl.pallas_call(
        body, name=name, grid=(2, 3),
        in_specs=[pl.BlockSpec((D, LANES), lambda l, j: (0, 0)),
                  pl.BlockSpec((None, NDEV, ct), lambda l, j: (l, 0, j))],
        out_specs=pl.BlockSpec((None, D, ct), lambda l, j: (l, 0, j)),
        out_shape=jax.ShapeDtypeStruct((2, D, ADA_COLS), F32),
        compiler_params=_params(("arbitrary", "arbitrary")),
    )(c_all_t, dmod)


def _pad_rows(row, rows=SUBLANES):
    return jnp.concatenate([row[None, :], jnp.zeros((rows - 1, row.shape[0]), row.dtype)], axis=0)


def kernel(x, c, w_ada, b_ada, norm_g, w_in, q_norm_g, k_norm_g, conv_w, conv_b, w_out, ffn_w1, ffn_w2, loss_target, m_w_ada, m_b_ada, m_norm_g, m_w_in, m_q_norm_g, m_k_norm_g, m_conv_w, m_conv_b, m_w_out, m_ffn_w1, m_ffn_w2, v_w_ada, v_b_ada, v_norm_g, v_w_in, v_q_norm_g, v_k_norm_g, v_conv_w, v_conv_b, v_w_out, v_ffn_w1, v_ffn_w2):
    ix, iy, ic = lax.axis_index("x"), lax.axis_index("y"), lax.axis_index("c")
    chip = 2 * ix + iy
    dev = 2 * chip + ic
    cflag = jnp.reshape(ic, (1,)).astype(jnp.int32)
    ngw = norm_g.shape[-1]
    cww = conv_w.shape[-1]

    pack = jnp.concatenate([c[0], norm_g.reshape(-1), conv_w.reshape(-1)])
    got = small_all_gather(_pad_rows(pack), "gather_c_normg_convw")[::SUBLANES]
    c_all = got[:, :D]
    per_chip = got[::2]
    ng_full = jnp.concatenate([per_chip[j, D:D + 6 * ngw].reshape(2, 3, ngw) for j in range(NCHIP)], axis=-1)
    cw_full = jnp.concatenate([per_chip[j, D + 6 * ngw:].reshape(2, 3, cww) for j in range(NCHIP)], axis=-1)

    b_shard = lax.dynamic_slice_in_dim(b_ada, chip * ADA_COLS, ADA_COLS, axis=1)
    mod_blk = mod_fwd(c_all, w_ada, b_shard, "mod_fwd").reshape(2 * NDEV, ADA_COLS)
    mod_all = small_all_gather(mod_blk, "gather_mod").reshape(NDEV, 2, NDEV, ADA_COLS)[::2]
    mod_mine = lax.dynamic_index_in_dim(mod_all, dev, axis=2, keepdims=False)
    mods = [mod_mine[:, l, :].reshape(-1) for l in range(2)]

    shards, gvecs, cws = [], [], []
    for l in range(2):
        shards.append(dict(w1=[ffn_w1[l, i].astype(MXU_DTYPE) for i in range(2)],
                           w2=[ffn_w2[l, i].astype(MXU_DTYPE) for i in range(2)],
                           win=w_in[l].astype(MXU_DTYPE), wout=w_out[l].astype(MXU_DTYPE)))
        gv = jnp.stack([jnp.tile(q_norm_g[l], AW // HD), jnp.tile(k_norm_g[l], AW // HD)])
        gvecs.append(jnp.concatenate([gv, jnp.zeros((SUBLANES - 2, AW), F32)], axis=0))
        cws.append(jnp.concatenate([cw_full[l], conv_b[l][None, :], jnp.zeros((SUBLANES - 4, CW), F32)], axis=0))
    w_first = ici_exchange([shards[0]["w1"][0], shards[0]["w2"][0]], False, "gather_first_ffn")

    loss_blk, dx, grads = local_step(x[0], loss_target[0], mods, [ng_full[0], ng_full[1]], gvecs, cws, shards, w_first)
    loss = lax.psum(loss_blk[0, 0], ("x", "y", "c"))

    def glist(l):
        g = grads[l]
        return [g["w1"][0], g["w2"][0].reshape(NCHIP, DFF // NCHIP, D), g["win"],
                g["wout"].reshape(NCHIP, D // NCHIP, D), g["w1"][1], g["w2"][1].reshape(NCHIP, DFF // NCHIP, D)]

    g0, g1 = glist(0), glist(1)
    from_sib = d2d_swap(g1, g0, "swap_layer_grads")
    wire = [add_select(g0[k], g1[k], from_sib[k], cflag, f"add_sibling_{k}") for k in range(6)]
    landed = ici_exchange(wire, True, "scatter_grads")
    tot_mine = [sum_chips(landed[k], f"sum_chips_{k}") for k in range(6)]
    tot_sib = d2d_swap(tot_mine, tot_mine, "swap_totals")

    dmods, dngs, dqg, dkg, dcw, dcb = [], [], [], [], [], []
    for l in range(2):
        s0, s1, so, s2, sm = grads[l]["sums"]
        dmods.append(jnp.concatenate([s0[0], s0[1], s0[3], s1[0], s1[1], so[0], s2[0], s2[1], s2[3]]))
        dngs.append(jnp.concatenate([s0[2], s1[2], s2[2]]))
        dqg.append(sm[0].reshape(AW // HD, HD).sum(0))
        dkg.append(sm[1].reshape(AW // HD, HD).sum(0))
        dcw.append(sm[2:5].reshape(-1))
        dcb.append(sm[5])
    small = jnp.concatenate(dmods + dngs + dqg + dkg + dcw + dcb)
    small_all = small_all_gather(_pad_rows(small), "gather_small_grads")[::SUBLANES]
    nm = 9 * D
    dmod_all = small_all[:, :2 * nm].reshape(NDEV, 2, NCHIP, ADA_COLS)
    dmod_mine = lax.dynamic_index_in_dim(dmod_all, chip, axis=2, keepdims=False).transpose(1, 0, 2)
    tot = sum_devices(small_all, "sum_small_grads")[0]
    o = 2 * nm
    g_b_ada = tot[:o].reshape(2, nm)
    g_norm_g = lax.dynamic_slice_in_dim(tot[o:o + 6 * D].reshape(2, 3, D), chip * ngw, ngw, axis=2)
    o += 6 * D
    g_qg = tot[o:o + 2 * HD].reshape(2, HD)
    o += 2 * HD
    g_kg = tot[o:o + 2 * HD].reshape(2, HD)
    o += 2 * HD
    g_cw = lax.dynamic_slice_in_dim(tot[o:o + 6 * CW].reshape(2, 3, CW), chip * cww, cww, axis=2)
    o += 6 * CW
    g_cb = tot[o:o + 2 * CW].reshape(2, CW)

    c_all_t = jnp.concatenate([c_all.T, jnp.zeros((D, LANES - NDEV), F32)], axis=1)
    g_wada_src = wada_grad(c_all_t, dmod_mine, "wada_grad")

    same = [(0, 0), (1, 1)]
    by_layer = [(0, 1), (1, 0)]
    by_layer2 = [(0, 2), (1, 3), (2, 0), (3, 1)]
    r_wada = adamw(w_ada, m_w_ada, v_w_ada, [g_wada_src[0], g_wada_src[1]], same, cflag, "adamw_w_ada")
    r_win = adamw(w_in, m_w_in, v_w_in, [tot_mine[2], tot_sib[2]], by_layer, cflag, "adamw_w_in")
    r_wout = adamw(w_out, m_w_out, v_w_out, [tot_mine[3], tot_sib[3]], by_layer, cflag, "adamw_w_out")
    r_w1 = adamw(ffn_w1.reshape(4, D, HALF), m_ffn_w1.reshape(4, D, HALF), v_ffn_w1.reshape(4, D, HALF),
                 [tot_mine[0], tot_mine[4], tot_sib[0], tot_sib[4]], by_layer2, cflag, "adamw_ffn_w1")
    w2r = DFF // NCHIP
    r_w2 = adamw(ffn_w2.reshape(4, w2r, D), m_ffn_w2.reshape(4, w2r, D), v_ffn_w2.reshape(4, w2r, D),
                 [tot_mine[1], tot_mine[5], tot_sib[1], tot_sib[5]], by_layer2, cflag, "adamw_ffn_w2")
    r_w1 = [t.reshape(ffn_w1.shape) for t in r_w1]
    r_w2 = [t.reshape(ffn_w2.shape) for t in r_w2]

    smalls = [("b_ada", b_ada, m_b_ada, v_b_ada, g_b_ada), ("norm_g", norm_g, m_norm_g, v_norm_g, g_norm_g),
              ("q_norm_g", q_norm_g, m_q_norm_g, v_q_norm_g, g_qg), ("k_norm_g", k_norm_g, m_k_norm_g, v_k_norm_g, g_kg),
              ("conv_w", conv_w, m_conv_w, v_conv_w, g_cw), ("conv_b", conv_b, m_conv_b, v_conv_b, g_cb)]
    n_small = sum(t[1].size for t in smalls)
    pad = (-n_small) % (16 * LANES)

    def packed(idx):
        flat = jnp.concatenate([t[idx].reshape(-1) for t in smalls] + [jnp.zeros((pad,), F32)])
        return flat.reshape(-1, LANES)

    r_small = adamw(packed(1)[None], packed(2)[None], packed(3)[None], [packed(4)], [(0, 0)], cflag, "adamw_small")
    small_out = {}
    o = 0
    for name_, w_, _, _, _ in smalls:
        small_out[name_] = [t.reshape(-1)[o:o + w_.size].reshape(w_.shape) for t in r_small]
        o += w_.size

    res = {"w_ada": r_wada, "w_in": r_win, "w_out": r_wout, "ffn_w1": r_w1, "ffn_w2": r_w2, **small_out}
    order = ["w_ada", "b_ada", "norm_g", "w_in", "q_norm_g", "k_norm_g", "conv_w", "conv_b", "w_out", "ffn_w1", "ffn_w2"]
    outs = [loss, dx[None]]
    for k in range(4):
        outs += [res[nm_][k] for nm_ in order]
    return tuple(outs)
```

```python
import functools

import jax
import jax.numpy as jnp
from jax import lax
from jax.experimental import pallas as pl
from jax.experimental.pallas import tpu as pltpu

F32 = jnp.float32
MXU_DTYPE = jnp.bfloat16
ACT_DTYPE = jnp.bfloat16
WIRE_DTYPE = jnp.bfloat16

D = 1024
HD = 64
AW = 512
CW = 512
DFF = 2816
HALF = DFF // 2
INC = 3 * AW + 3 * CW
NCHIP = 4
NDEV = 8
QBLK = 128
ATTN_CHUNK_ROWS = 2048
DILATIONS = (1, 4, 16)
EPS = 1e-6
NEG = -1e30
LANES = 128
SUBLANES = 8
VMEM_LIMIT = 56 * 1024 * 1024

ADAM_LR = 0.001
ADAM_B1 = 0.9
ADAM_B2 = 0.999
ADAM_EPS = 1e-08
ADAM_WD = 0.01
ADAM_STEP = 10

NT_DIMS = (((1,), (1,)), ((), ()))
TN_DIMS = (((0,), (0,)), ((), ()))


def _params(sem, vmem=VMEM_LIMIT):
    return pltpu.CompilerParams(dimension_semantics=sem, vmem_limit_bytes=vmem)


def _row_tile(n, want):
    t = min(n, want)
    assert n % t == 0
    return t


def _ada(xt, vec_ref):
    ng, sc, sh, gt = vec_ref[0:1, :], vec_ref[1:2, :], vec_ref[2:3, :], vec_ref[3:4, :]
    r = lax.rsqrt(jnp.mean(xt * xt, axis=-1, keepdims=True) + EPS)
    return xt * r, r, ng * (1.0 + sc), ng, sc, sh, gt


def _ada_bwd(dh, xhat, r, gain, ng, sc):
    dshift = jnp.sum(dh, axis=0, keepdims=True)
    dhx = dh * xhat
    dscale = jnp.sum(dhx, axis=0, keepdims=True) * ng
    dng = jnp.sum(dhx, axis=0, keepdims=True) * (1.0 + sc)
    dxhat = dh * gain
    dx = r * (dxhat - xhat * jnp.mean(dxhat * xhat, axis=-1, keepdims=True))
    return dx, dshift, dscale, dng


def _acc_rows(sums_ref, first, rows):
    @pl.when(first)
    def _():
        sums_ref[...] = jnp.zeros_like(sums_ref)
    for k, row in enumerate(rows):
        sums_ref[k:k + 1, :] += row


MESH = pl.DeviceIdType.MESH
ANY = pl.BlockSpec(memory_space=pl.ANY)


def _here():
    return lax.axis_index("x"), lax.axis_index("y"), lax.axis_index("c")


def _ici_copies(src_refs, dst_refs, send_sems, recv_sems, local_sems, scatter):
    x, y, c = _here()
    my_chip = 2 * x + y
    peers = [(1 - x, y), (x, 1 - y), (1 - x, 1 - y)]
    local, out, inc = [], [], []
    for a, (src, dst) in enumerate(zip(src_refs, dst_refs)):
        local.append(pltpu.make_async_copy(src.at[my_chip] if scatter else src, dst.at[my_chip], local_sems.at[a]))
        for j, (px, py) in enumerate(peers):
            sems = dict(send_sem=send_sems.at[3 * a + j], recv_sem=recv_sems.at[3 * a + j],
                        device_id=(px, py, c), device_id_type=MESH)
            out.append(pltpu.make_async_remote_copy(
                src_ref=src.at[2 * px + py] if scatter else src, dst_ref=dst.at[my_chip], **sems))
            inc.append(pltpu.make_async_remote_copy(
                src_ref=src.at[my_chip] if scatter else src, dst_ref=dst.at[2 * px + py], **sems))
    return local, out, inc


def _swap_copies(src_refs, dst_refs, send_sems, recv_sems, halves):
    x, y, c = _here()
    cps = []
    for k, (src, dst) in enumerate(zip(src_refs, dst_refs)):
        if halves:
            r2 = src.shape[1] // 2
            src = src.at[:, pl.ds((1 - c) * r2, r2), :]
        cps.append(pltpu.make_async_remote_copy(
            src_ref=src, dst_ref=dst, send_sem=send_sems.at[k], recv_sem=recv_sems.at[k],
            device_id=(x, y, 1 - c), device_id_type=MESH))
    return cps


class Carry:
    def __init__(self, kind, srcs):
        self.kind, self.srcs, n = kind, list(srcs), len(srcs)
        if kind == "gather":
            shapes = [(NCHIP,) + s.shape for s in srcs]
        elif kind == "swap_halves":
            shapes = [(s.shape[0], s.shape[1] // 2, s.shape[2]) for s in srcs]
        else:
            shapes = [s.shape for s in srcs]
        self.out_shape = [jax.ShapeDtypeStruct(sh, s.dtype) for sh, s in zip(shapes, srcs)]
        dma = pltpu.SemaphoreType.DMA
        self.sems = [dma((3 * n,)), dma((3 * n,)), dma((n,))] if kind in ("gather", "scatter") else [dma((n,)), dma((n,))]

    def start(self, srcs, dsts, sems):
        if self.kind in ("gather", "scatter"):
            local, out, _ = _ici_copies(srcs, dsts, *sems, self.kind == "scatter")
            for cp in local + out:
                cp.start()
        else:
            for cp in _swap_copies(srcs, dsts, *sems, self.kind == "swap_halves"):
                cp.start()

    def wait(self, srcs, dsts, sems):
        if self.kind in ("gather", "scatter"):
            local, out, inc = _ici_copies(srcs, dsts, *sems, self.kind == "scatter")
            for cp in inc:
                cp.wait_recv()
            for cp in out:
                cp.wait_send()
            for cp in local:
                cp.wait()
        else:
            cps = _swap_copies(srcs, dsts, *sems, self.kind == "swap_halves")
            for cp in cps:
                cp.wait_recv()
            for cp in cps:
                cp.wait_send()


def run_carry(carry, name):
    n = len(carry.srcs)

    def body(*refs):
        srcs, dsts, sems = refs[:n], refs[n:2 * n], refs[2 * n:]
        carry.start(srcs, dsts, sems)
        carry.wait(srcs, dsts, sems)

    return pl.pallas_call(body, name=name, out_shape=carry.out_shape, in_specs=[ANY] * n, out_specs=[ANY] * n,
                          scratch_shapes=carry.sems)(*carry.srcs)


def _pcall(body, name, grid, in_specs, out_specs, out_shape, sem, args, carry=None):
    if carry is None:
        outs = pl.pallas_call(body, name=name, grid=grid, in_specs=in_specs, out_specs=out_specs,
                              out_shape=out_shape, compiler_params=_params(sem))(*args)
        return outs, []
    n_in, n_out, nc = len(in_specs), len(out_specs), len(carry.srcs)

    def wrapped(*refs):
        ins, csrc = refs[:n_in], refs[n_in:n_in + nc]
        outs, cdst = refs[n_in + nc:n_in + nc + n_out], refs[n_in + nc + n_out:n_in + 2 * nc + n_out]
        sems = refs[n_in + 2 * nc + n_out:]
        ids = [pl.program_id(a) for a in range(len(grid))]
        first = functools.reduce(jnp.logical_and, [i == 0 for i in ids])
        last = functools.reduce(jnp.logical_and, [i == g - 1 for i, g in zip(ids, grid)])

        @pl.when(first)
        def _():
            carry.start(csrc, cdst, sems)

        body(*ins, *outs)

        @pl.when(last)
        def _():
            carry.wait(csrc, cdst, sems)

    res = pl.pallas_call(
        wrapped, name=name, grid=grid,
        in_specs=list(in_specs) + [ANY] * nc, out_specs=list(out_specs) + [ANY] * nc,
        out_shape=list(out_shape) + carry.out_shape,
        scratch_shapes=carry.sems, compiler_params=_params(sem),
    )(*args, *carry.srcs)
    return res[:n_out], res[n_out:]


def ffn_fwd(x, vec, w1p, w2, gs, name, carry=None):
    S = x.shape[0]
    tm = _row_tile(S, 512)

    def body(x_ref, vec_ref, w1_ref, w2_ref, xn_ref, a_ref, f_ref):
        xt = x_ref[...]
        xhat, _, gain, _, _, sh, gt = _ada(xt, vec_ref)
        h = (xhat * gain + sh).astype(MXU_DTYPE)
        f = jnp.zeros((tm, D), F32)
        for hf in range(2):
            g = jnp.dot(h, w1_ref[hf], preferred_element_type=F32)
            up = jnp.dot(h, w1_ref[2 + hf], preferred_element_type=F32)
            a_ref[:, hf * HALF:(hf + 1) * HALF] = g.astype(a_ref.dtype)
            a_ref[:, DFF + hf * HALF:DFF + (hf + 1) * HALF] = up.astype(a_ref.dtype)
            act = (g * jax.nn.sigmoid(g) * up).astype(MXU_DTYPE)
            f = f + jnp.dot(act, w2_ref[hf * HALF:(hf + 1) * HALF, :], preferred_element_type=F32)
        xn_ref[...] = xt + (gs * gt) * f
        f_ref[...] = f.astype(f_ref.dtype)

    return _pcall(
        body, name, (S // tm,),
        [pl.BlockSpec((tm, D), lambda i: (i, 0)),
         pl.BlockSpec((SUBLANES, D), lambda i: (0, 0)),
         pl.BlockSpec((NCHIP, D, HALF), lambda i: (0, 0, 0), pipeline_mode=pl.Buffered(1)),
         pl.BlockSpec((DFF, D), lambda i: (0, 0), pipeline_mode=pl.Buffered(1))],
        [pl.BlockSpec((tm, D), lambda i: (i, 0)),
         pl.BlockSpec((tm, 2 * DFF), lambda i: (i, 0)),
         pl.BlockSpec((tm, D), lambda i: (i, 0))],
        [jax.ShapeDtypeStruct((S, D), F32),
         jax.ShapeDtypeStruct((S, 2 * DFF), ACT_DTYPE),
         jax.ShapeDtypeStruct((S, D), ACT_DTYPE)],
        ("arbitrary",), (x, vec, w1p, w2), carry)


def ffn_bwd(dxo, x, a, f, vec, w1p, w2, gs, name, carry=None):
    S = x.shape[0]
    tm = _row_tile(S, 256)

    def body(dxo_ref, x_ref, a_ref, f_ref, vec_ref, w1_ref, w2_ref,
             dxi_ref, hb_ref, dfb_ref, act_ref, da_ref, sums_ref):
        xt = x_ref[...]
        dxo = dxo_ref[...]
        xhat, r, gain, ng, sc, sh, gt = _ada(xt, vec_ref)
        hb_ref[...] = (xhat * gain + sh).astype(hb_ref.dtype)
        dgate = gs * jnp.sum(dxo * f_ref[...].astype(F32), axis=0, keepdims=True)
        df = ((gs * gt) * dxo).astype(MXU_DTYPE)
        dfb_ref[...] = df
        dh = jnp.zeros((tm, D), F32)
        for hf in range(2):
            lo, hi = hf * HALF, (hf + 1) * HALF
            dact = lax.dot_general(df, w2_ref[lo:hi, :], NT_DIMS, preferred_element_type=F32)
            g = a_ref[:, lo:hi].astype(F32)
            up = a_ref[:, DFF + lo:DFF + hi].astype(F32)
            sg = jax.nn.sigmoid(g)
            si = g * sg
            act_ref[:, lo:hi] = (si * up).astype(act_ref.dtype)
            dg = (dact * up * (sg * (1.0 + g * (1.0 - sg)))).astype(MXU_DTYPE)
            dup = (dact * si).astype(MXU_DTYPE)
            da_ref[:, lo:hi] = dg
            da_ref[:, DFF + lo:DFF + hi] = dup
            dh = dh + lax.dot_general(dg, w1_ref[hf], NT_DIMS, preferred_element_type=F32)
            dh = dh + lax.dot_general(dup, w1_ref[2 + hf], NT_DIMS, preferred_element_type=F32)
        dx, dshift, dscale, dng = _ada_bwd(dh, xhat, r, gain, ng, sc)
        dxi_ref[...] = dxo + dx
        _acc_rows(sums_ref, pl.program_id(0) == 0, (dshift, dscale, dng, dgate))

    return _pcall(
        body, name, (S // tm,),
        [pl.BlockSpec((tm, D), lambda i: (i, 0)),
         pl.BlockSpec((tm, D), lambda i: (i, 0)),
         pl.BlockSpec((tm, 2 * DFF), lambda i: (i, 0)),
         pl.BlockSpec((tm, D), lambda i: (i, 0)),
         pl.BlockSpec((SUBLANES, D), lambda i: (0, 0)),
         pl.BlockSpec((NCHIP, D, HALF), lambda i: (0, 0, 0), pipeline_mode=pl.Buffered(1)),
         pl.BlockSpec((DFF, D), lambda i: (0, 0), pipeline_mode=pl.Buffered(1))],
        [pl.BlockSpec((tm, D), lambda i: (i, 0)),
         pl.BlockSpec((tm, D), lambda i: (i, 0)),
         pl.BlockSpec((tm, D), lambda i: (i, 0)),
         pl.BlockSpec((tm, DFF), lambda i: (i, 0)),
         pl.BlockSpec((tm, 2 * DFF), lambda i: (i, 0)),
         pl.BlockSpec((SUBLANES, D), lambda i: (0, 0))],
        [jax.ShapeDtypeStruct((S, D), F32),
         jax.ShapeDtypeStruct((S, D), MXU_DTYPE),
         jax.ShapeDtypeStruct((S, D), MXU_DTYPE),
         jax.ShapeDtypeStruct((S, DFF), MXU_DTYPE),
         jax.ShapeDtypeStruct((S, 2 * DFF), MXU_DTYPE),
         jax.ShapeDtypeStruct((SUBLANES, D), F32)],
        ("arbitrary",), (dxo, x, a, f, vec, w1p, w2), carry)


def wgrad(a, b, kt, nt, name):
    T, K = a.shape
    N = b.shape[1]
    pk, pn = K // kt, N // nt
    assert pk == 1 or pn == 1
    tt = _row_tile(T, 1024)
    steps = T // tt

    def body(a_ref, b_ref, o_ref):
        @pl.when(pl.program_id(1) == 0)
        def _():
            o_ref[...] = jnp.zeros_like(o_ref)
        o_ref[...] += lax.dot_general(a_ref[...], b_ref[...], TN_DIMS, preferred_element_type=F32)

    a_map = (lambda p, t: (t, p)) if pk > 1 else (lambda p, t: (t, 0))
    b_map = (lambda p, t: (t, p)) if pn > 1 else (lambda p, t: (t, 0))
    return pl.pallas_call(
        body, name=name, grid=(pk * pn, steps),
        in_specs=[pl.BlockSpec((tt, kt), a_map), pl.BlockSpec((tt, nt), b_map)],
        out_specs=pl.BlockSpec((None, kt, nt), lambda p, t: (p, 0, 0)),
        out_shape=jax.ShapeDtypeStruct((pk * pn, kt, nt), F32),
        compiler_params=_params(("arbitrary", "arbitrary")),
    )(a, b)


def _head_masks(rows):
    lane = lax.broadcasted_iota(jnp.int32, (rows, LANES), 1)
    return lane < HD


def _pair_stat(x, m_a):
    s_a = jnp.sum(jnp.where(m_a, x, 0.0), axis=1, keepdims=True)
    s_b = jnp.sum(jnp.where(m_a, 0.0, x), axis=1, keepdims=True)
    return s_a, s_b


def mixer_in(x, vec, winp, gvec, name):
    S = x.shape[0]
    tm = _row_tile(S, 512)
    pc = INC // NCHIP

    def body(x_ref, vec_ref, w_ref, g_ref, proj_ref, hb_ref, qn_ref, kn_ref, v_ref):
        xt = x_ref[...]
        xhat, _, gain, _, _, sh, _ = _ada(xt, vec_ref)
        h = (xhat * gain + sh).astype(MXU_DTYPE)
        hb_ref[...] = h
        for j in range(NCHIP):
            proj_ref[:, j * pc:(j + 1) * pc] = jnp.dot(h, w_ref[j], preferred_element_type=F32)
        m_a = _head_masks(tm)
        for which, dst in ((0, qn_ref), (1, kn_ref)):
            for p in range(AW // LANES):
                lo = which * AW + p * LANES
                xp = proj_ref[:, lo:lo + LANES]
                s_a, s_b = _pair_stat(xp * xp, m_a)
                rr = jnp.where(m_a, lax.rsqrt(s_a * (1.0 / HD) + EPS), lax.rsqrt(s_b * (1.0 / HD) + EPS))
                gp = g_ref[which:which + 1, p * LANES:(p + 1) * LANES]
                dst[:, p * LANES:(p + 1) * LANES] = (xp * rr * gp).astype(dst.dtype)
        v_ref[...] = proj_ref[:, 2 * AW:3 * AW].astype(v_ref.dtype)

    return pl.pallas_call(
        body, name=name, grid=(S // tm,),
        in_specs=[pl.BlockSpec((tm, D), lambda i: (i, 0)),
                  pl.BlockSpec((SUBLANES, D), lambda i: (0, 0)),
                  pl.BlockSpec((NCHIP, D, pc), lambda i: (0, 0, 0), pipeline_mode=pl.Buffered(1)),
                  pl.BlockSpec((SUBLANES, AW), lambda i: (0, 0))],
        out_specs=[pl.BlockSpec((tm, INC), lambda i: (i, 0)),
                   pl.BlockSpec((tm, D), lambda i: (i, 0)),
                   pl.BlockSpec((tm, AW), lambda i: (i, 0)),
                   pl.BlockSpec((tm, AW), lambda i: (i, 0)),
                   pl.BlockSpec((tm, AW), lambda i: (i, 0))],
        out_shape=[jax.ShapeDtypeStruct((S, INC), F32),
                   jax.ShapeDtypeStruct((S, D), MXU_DTYPE),
                   jax.ShapeDtypeStruct((S, AW), F32),
                   jax.ShapeDtypeStruct((S, AW), F32),
                   jax.ShapeDtypeStruct((S, AW), F32)],
        compiler_params=_params(("arbitrary",)),
    )(x, vec, winp, gvec)


def _band_masks(ncol):
    row = lax.broadcasted_iota(jnp.int32, (2 * QBLK, ncol), 0) & (QBLK - 1)
    col = lax.broadcasted_iota(jnp.int32, (2 * QBLK, ncol), 1)
    return row, col


def _stack_heads(t, m_a):
    zero = jnp.zeros_like(t)
    return jnp.concatenate([jnp.where(m_a, t, zero), jnp.where(m_a, zero, t)], axis=0)


def _attn_qb(d):
    return max(1, min(4, ATTN_CHUNK_ROWS // (QBLK * d)))


def _tile_rows(d, b, r):
    if d == 1:
        return pl.ds(b * QBLK, QBLK)
    return pl.ds(b * QBLK * d + r, QBLK, stride=d)


def _per_residue(d, fn):
    if d == 1:
        fn(0)
    else:
        def step(r, carry):
            fn(r)
            return carry
        lax.fori_loop(0, d, step, 0)


def attn_fwd(qn, kn, v, d, name, carry=None):
    S = qn.shape[0]
    qb = _attn_qb(d)
    halo = QBLK * d
    chunk = qb * halo

    def body(q_ref, kc_ref, kp_ref, vc_ref, vp_ref, o_ref, la_ref, lb_ref):
        i = pl.program_id(1)
        m_a = _head_masks(QBLK)
        row, col = _band_masks(2 * QBLK)
        dist = row + QBLK - col
        band = (dist >= 0) & (dist <= QBLK)
        first = band & ((i > 0) | (col >= QBLK))

        def residue(r):
            kt = [kp_ref[_tile_rows(d, 0, r), :].astype(MXU_DTYPE)]
            vt = [vp_ref[_tile_rows(d, 0, r), :].astype(MXU_DTYPE)]
            for b in range(qb):
                kt.append(kc_ref[_tile_rows(d, b, r), :].astype(MXU_DTYPE))
                vt.append(vc_ref[_tile_rows(d, b, r), :].astype(MXU_DTYPE))
            for b in range(qb):
                rows = _tile_rows(d, b, r)
                q = (q_ref[rows, :] * (HD ** -0.5)).astype(MXU_DTYPE)
                kcat = jnp.concatenate([kt[b], kt[b + 1]], axis=0)
                vcat = jnp.concatenate([vt[b], vt[b + 1]], axis=0)
                mask = first if b == 0 else band
                s = lax.dot_general(_stack_heads(q, m_a), kcat, NT_DIMS, preferred_element_type=F32)
                s = jnp.where(mask, s, NEG)
                m = jnp.max(s, axis=1, keepdims=True)
                p = jnp.exp(s - m)
                l = jnp.sum(p, axis=1, keepdims=True)
                o = jnp.dot(p.astype(MXU_DTYPE), vcat, preferred_element_type=F32) / l
                lse = jnp.broadcast_to(m + jnp.log(l), (2 * QBLK, LANES))
                o_ref[rows, :] = jnp.where(m_a, o[:QBLK], o[QBLK:])
                la_ref[rows, :] = lse[:QBLK]
                lb_ref[rows, :] = lse[QBLK:]

        _per_residue(d, residue)

    cur = pl.BlockSpec((chunk, LANES), lambda hp, i: (i, hp))
    prev = pl.BlockSpec((halo, LANES), lambda hp, i: (jnp.maximum(i * qb - 1, 0), hp))
    return _pcall(body, name, (AW // LANES, S // chunk), [cur, cur, prev, cur, prev], [cur, cur, cur],
                  [jax.ShapeDtypeStruct((S, AW), F32)] * 3, ("arbitrary", "arbitrary"), (qn, kn, kn, v, v), carry)


def attn_bwd(qn, kn, v, dycat, lse_a, lse_b, dl_a, dl_b, d, name, carry=None):
    S = qn.shape[0]
    qb = _attn_qb(d)
    halo = QBLK * d
    chunk = qb * halo
    nhalo = S // halo
    nchunk = S // chunk

    def body(q_ref, qx_ref, kc_ref, kp_ref, vc_ref, vp_ref, do_ref, dox_ref, la_ref, lax_ref, lb_ref, lbx_ref,
             da_ref, dax_ref, db_ref, dbx_ref, dq_ref, dk_ref, dv_ref):
        i = pl.program_id(1)
        has_next = i < nchunk - 1
        m_a = _head_masks(QBLK)
        row, col = _band_masks(2 * QBLK)
        dist = row + QBLK - col
        band = (dist >= 0) & (dist <= QBLK)
        first = band & ((i > 0) | (col >= QBLK))
        row1, col1 = _band_masks(QBLK)
        off_only = (col1 >= row1) & has_next

        def residue(r):
            def tiles(cur_ref, next_ref, cast):
                out = [cur_ref[_tile_rows(d, b, r), :] for b in range(qb)] + [next_ref[_tile_rows(d, 0, r), :]]
                return [t.astype(MXU_DTYPE) for t in out] if cast else out

            def ktiles(cur_ref, prev_ref):
                out = [prev_ref[_tile_rows(d, 0, r), :]] + [cur_ref[_tile_rows(d, b, r), :] for b in range(qb)]
                return [t.astype(MXU_DTYPE) for t in out]

            qt = [(t * (HD ** -0.5)).astype(MXU_DTYPE) for t in tiles(q_ref, qx_ref, False)]
            dot_ = tiles(do_ref, dox_ref, True)
            lse_t = list(zip(tiles(la_ref, lax_ref, False), tiles(lb_ref, lbx_ref, False)))
            dl_t = list(zip(tiles(da_ref, dax_ref, False), tiles(db_ref, dbx_ref, False)))
            kt = ktiles(kc_ref, kp_ref)
            vt = ktiles(vc_ref, vp_ref)
            dk_acc = [jnp.zeros((QBLK, LANES), F32) for _ in range(qb)]
            dv_acc = [jnp.zeros((QBLK, LANES), F32) for _ in range(qb)]
            for x in range(qb + 1):
                parts = 2 if x < qb else 1
                if parts == 2:
                    kcat = jnp.concatenate([kt[x], kt[x + 1]], axis=0)
                    vcat = jnp.concatenate([vt[x], vt[x + 1]], axis=0)
                else:
                    kcat, vcat = kt[x], vt[x]
                mask = first if x == 0 else (band if x < qb else off_only)
                q2 = _stack_heads(qt[x], m_a)
                do2 = _stack_heads(dot_[x], m_a)
                lse2 = jnp.concatenate(lse_t[x], axis=0)
                dl2 = jnp.concatenate(dl_t[x], axis=0)
                if parts == 2:
                    lse2 = jnp.concatenate([lse2, lse2], axis=1)
                    dl2 = jnp.concatenate([dl2, dl2], axis=1)
                s = lax.dot_general(q2, kcat, NT_DIMS, preferred_element_type=F32)
                p = jnp.exp(jnp.where(mask, s, NEG) - lse2)
                dp = lax.dot_general(do2, vcat, NT_DIMS, preferred_element_type=F32)
                ds = p * (dp - dl2)
                if x < qb:
                    dq = jnp.dot(ds.astype(MXU_DTYPE), kcat, preferred_element_type=F32)
                    dq_ref[_tile_rows(d, x, r), :] = jnp.where(m_a, dq[:QBLK], dq[QBLK:]) * (HD ** -0.5)
                ds_t = ds.T.astype(MXU_DTYPE)
                p_t = p.T.astype(MXU_DTYPE)
                for part in range(parts):
                    kb = x - 1 + part
                    if 0 <= kb < qb:
                        sl = slice(part * QBLK, (part + 1) * QBLK)
                        dk_acc[kb] = dk_acc[kb] + jnp.dot(ds_t[sl], q2, preferred_element_type=F32)
                        dv_acc[kb] = dv_acc[kb] + jnp.dot(p_t[sl], do2, preferred_element_type=F32)
            for kb in range(qb):
                dk_ref[_tile_rows(d, kb, r), :] = dk_acc[kb]
                dv_ref[_tile_rows(d, kb, r), :] = dv_acc[kb]

        _per_residue(d, residue)

    def nxt(i):
        return jnp.minimum((i + 1) * qb, nhalo - 1)

    cur = pl.BlockSpec((chunk, LANES), lambda hp, i: (i, hp))
    prev = pl.BlockSpec((halo, LANES), lambda hp, i: (jnp.maximum(i * qb - 1, 0), hp))
    nx = pl.BlockSpec((halo, LANES), lambda hp, i: (nxt(i), hp))
    return _pcall(
        body, name, (AW // LANES, nchunk),
        [cur, nx, cur, prev, cur, prev, cur, nx, cur, nx, cur, nx, cur, nx, cur, nx], [cur, cur, cur],
        [jax.ShapeDtypeStruct((S, AW), F32)] * 3, ("arbitrary", "arbitrary"),
        (qn, qn, kn, kn, v, v, dycat, dycat, lse_a, lse_a, lse_b, lse_b, dl_a, dl_a, dl_b, dl_b), carry)


def _shift_down(x, halo_prev, k, row):
    tm = x.shape[0]
    tail = jnp.concatenate([pltpu.roll(halo_prev, k, 0), jnp.zeros((tm - SUBLANES, x.shape[1]), x.dtype)], axis=0)
    return jnp.where(row < k, tail, pltpu.roll(x, k, 0))


def _shift_up(x, halo_next, k, row):
    tm = x.shape[0]
    head = jnp.concatenate([jnp.zeros((tm - SUBLANES, x.shape[1]), x.dtype), pltpu.roll(halo_next, SUBLANES - k, 0)], axis=0)
    return jnp.where(row >= tm - k, head, pltpu.roll(x, tm - k, 0))


def _conv_fwd(cu, halo_cu, cw_ref, row):
    u1 = _shift_down(cu, halo_cu, 1, row)
    u2 = _shift_down(cu, halo_cu, 2, row)
    cv = cw_ref[0:1, :] * u2 + cw_ref[1:2, :] * u1 + cw_ref[2:3, :] * cu + cw_ref[3:4, :]
    return cv, u1, u2


def combine_conv(os_, lses_a, lses_b, proj, cw, name, carry=None):
    S = proj.shape[0]
    tm = _row_tile(S, 512)
    hb = tm // SUBLANES

    def body(o1, o2, o3, a1, a2, a3, b1, b2, b3, pc_ref, ph_ref, cw_ref, ycat_ref, la_ref, lb_ref):
        i = pl.program_id(0)
        m_a = _head_masks(tm)
        for p in range(AW // LANES):
            cs = slice(p * LANES, (p + 1) * LANES)
            tot = []
            for srcs, dst in (((a1, a2, a3), la_ref), ((b1, b2, b3), lb_ref)):
                ls = [l[:, cs] for l in srcs]
                mx = jnp.maximum(jnp.maximum(ls[0], ls[1]), ls[2])
                t = mx + jnp.log(jnp.exp(ls[0] - mx) + jnp.exp(ls[1] - mx) + jnp.exp(ls[2] - mx))
                dst[:, cs] = t
                tot.append((ls, t))
            acc = jnp.zeros((tm, LANES), F32)
            for r, o in enumerate((o1, o2, o3)):
                w = jnp.where(m_a, jnp.exp(tot[0][0][r] - tot[0][1]), jnp.exp(tot[1][0][r] - tot[1][1]))
                acc = acc + w * o[:, cs]
            ycat_ref[:, cs] = acc.astype(ycat_ref.dtype)
        row = lax.broadcasted_iota(jnp.int32, (tm, CW), 0)
        gb, gc, u = pc_ref[:, 0:CW], pc_ref[:, CW:2 * CW], pc_ref[:, 2 * CW:3 * CW]
        halo_cu = jnp.where(i > 0, ph_ref[:, CW:2 * CW] * ph_ref[:, 2 * CW:3 * CW], 0.0)
        cv, _, _ = _conv_fwd(gc * u, halo_cu, cw_ref, row)
        ycat_ref[:, AW:AW + CW] = (gb * cv).astype(ycat_ref.dtype)

    ot = pl.BlockSpec((tm, AW), lambda i: (i, 0))
    return _pcall(
        body, name, (S // tm,),
        [ot] * 9 + [pl.BlockSpec((tm, 3 * CW), lambda i: (i, 1)),
                    pl.BlockSpec((SUBLANES, 3 * CW), lambda i: (jnp.maximum(i * hb - 1, 0), 1)),
                    pl.BlockSpec((SUBLANES, CW), lambda i: (0, 0))],
        [pl.BlockSpec((tm, D), lambda i: (i, 0)), ot, ot],
        [jax.ShapeDtypeStruct((S, D), ACT_DTYPE), jax.ShapeDtypeStruct((S, AW), F32),
         jax.ShapeDtypeStruct((S, AW), F32)],
        ("arbitrary",), (*os_, *lses_a, *lses_b, proj, proj, cw), carry)


def out_proj(ycat, x, vec, wout, name):
    S = x.shape[0]
    tm = _row_tile(S, 512)

    def body(yc_ref, x_ref, vec_ref, w_ref, xn_ref, y_ref):
        y = jnp.dot(yc_ref[...].astype(MXU_DTYPE), w_ref[...], preferred_element_type=F32)
        xn_ref[...] = x_ref[...] + vec_ref[3:4, :] * y
        y_ref[...] = y.astype(y_ref.dtype)

    t = pl.BlockSpec((tm, D), lambda i: (i, 0))
    return pl.pallas_call(
        body, name=name, grid=(S // tm,),
        in_specs=[t, t, pl.BlockSpec((SUBLANES, D), lambda i: (0, 0)),
                  pl.BlockSpec((D, D), lambda i: (0, 0))],
        out_specs=[t, t],
        out_shape=[jax.ShapeDtypeStruct((S, D), F32), jax.ShapeDtypeStruct((S, D), ACT_DTYPE)],
        compiler_params=_params(("arbitrary",)),
    )(ycat, x, vec, wout)


def out_proj_bwd(dxo, y, ycat, vec, wout, name):
    S = dxo.shape[0]
    tm = _row_tile(S, 512)

    def body(dxo_ref, y_ref, yc_ref, vec_ref, w_ref, dyb_ref, dyc_ref, da_ref, db_ref, sums_ref):
        dxo = dxo_ref[...]
        dgate = jnp.sum(dxo * y_ref[...].astype(F32), axis=0, keepdims=True)
        dy = (vec_ref[3:4, :] * dxo).astype(MXU_DTYPE)
        dyb_ref[...] = dy
        dyc_ref[...] = lax.dot_general(dy, w_ref[...], NT_DIMS, preferred_element_type=F32)
        m_a = _head_masks(tm)
        for p in range(AW // LANES):
            cs = slice(p * LANES, (p + 1) * LANES)
            s_a, s_b = _pair_stat(dyc_ref[:, cs] * yc_ref[:, cs].astype(F32), m_a)
            da_ref[:, cs] = jnp.broadcast_to(s_a, (tm, LANES))
            db_ref[:, cs] = jnp.broadcast_to(s_b, (tm, LANES))
        _acc_rows(sums_ref, pl.program_id(0) == 0, (dgate,))

    t = pl.BlockSpec((tm, D), lambda i: (i, 0))
    at = pl.BlockSpec((tm, AW), lambda i: (i, 0))
    return pl.pallas_call(
        body, name=name, grid=(S // tm,),
        in_specs=[t, t, t, pl.BlockSpec((SUBLANES, D), lambda i: (0, 0)),
                  pl.BlockSpec((D, D), lambda i: (0, 0))],
        out_specs=[t, t, at, at, pl.BlockSpec((SUBLANES, D), lambda i: (0, 0))],
        out_shape=[jax.ShapeDtypeStruct((S, D), MXU_DTYPE), jax.ShapeDtypeStruct((S, D), F32),
                   jax.ShapeDtypeStruct((S, AW), F32), jax.ShapeDtypeStruct((S, AW), F32),
                   jax.ShapeDtypeStruct((SUBLANES, D), F32)],
        compiler_params=_params(("arbitrary",)),
    )(dxo, y, ycat, vec, wout)


def mixer_mid_bwd(dqs, dks, dvs, proj, dycat, gvec, cw, name):
    S = proj.shape[0]
    tm = _row_tile(S, 256)
    hb = tm // SUBLANES
    nsl = S // SUBLANES
    ntile = S // tm

    def body(dq1, dq2, dq3, dk1, dk2, dk3, dv1, dv2, dv3, pr_ref, pp_ref, pn_ref, dyc_ref, dyn_ref,
             g_ref, cw_ref, dp_ref, sums_ref):
        i = pl.program_id(0)
        m_a = _head_masks(tm)
        gsum = []
        for which, parts in ((0, (dq1, dq2, dq3)), (1, (dk1, dk2, dk3))):
            acc_g = []
            for p in range(AW // LANES):
                lo = which * AW + p * LANES
                cs = slice(p * LANES, (p + 1) * LANES)
                xp = pr_ref[:, lo:lo + LANES]
                s_a, s_b = _pair_stat(xp * xp, m_a)
                rr = jnp.where(m_a, lax.rsqrt(s_a * (1.0 / HD) + EPS), lax.rsqrt(s_b * (1.0 / HD) + EPS))
                xh = xp * rr
                dn = parts[0][:, cs] + parts[1][:, cs] + parts[2][:, cs]
                acc_g.append(jnp.sum(dn * xh, axis=0, keepdims=True))
                t = dn * g_ref[which:which + 1, cs]
                t_a, t_b = _pair_stat(t * xh, m_a)
                mean = jnp.where(m_a, t_a, t_b) * (1.0 / HD)
                dp_ref[:, lo:lo + LANES] = (rr * (t - xh * mean)).astype(dp_ref.dtype)
            gsum.append(jnp.concatenate(acc_g, axis=1))
        dp_ref[:, 2 * AW:3 * AW] = (dv1[...] + dv2[...] + dv3[...]).astype(dp_ref.dtype)
        row = lax.broadcasted_iota(jnp.int32, (tm, CW), 0)
        base = 3 * AW
        gb, gc, u = pr_ref[:, base:base + CW], pr_ref[:, base + CW:base + 2 * CW], pr_ref[:, base + 2 * CW:base + 3 * CW]
        cu = gc * u
        halo_cu = jnp.where(i > 0, pp_ref[:, CW:2 * CW] * pp_ref[:, 2 * CW:3 * CW], 0.0)
        cv, u1, u2 = _conv_fwd(cu, halo_cu, cw_ref, row)
        dyc = dyc_ref[...]
        dp_ref[:, base:base + CW] = (dyc * cv).astype(dp_ref.dtype)
        dcv = dyc * gb
        halo_dcv = jnp.where(i < ntile - 1, dyn_ref[...] * pn_ref[:, 0:CW], 0.0)
        d1 = _shift_up(dcv, halo_dcv, 1, row)
        d2 = _shift_up(dcv, halo_dcv, 2, row)
        dcu = cw_ref[2:3, :] * dcv + cw_ref[1:2, :] * d1 + cw_ref[0:1, :] * d2
        dp_ref[:, base + CW:base + 2 * CW] = (dcu * u).astype(dp_ref.dtype)
        dp_ref[:, base + 2 * CW:base + 3 * CW] = (dcu * gc).astype(dp_ref.dtype)
        rows = (gsum[0], gsum[1],
                jnp.sum(dcv * u2, axis=0, keepdims=True), jnp.sum(dcv * u1, axis=0, keepdims=True),
                jnp.sum(dcv * cu, axis=0, keepdims=True), jnp.sum(dcv, axis=0, keepdims=True))
        _acc_rows(sums_ref, i == 0, rows)

    at = pl.BlockSpec((tm, AW), lambda i: (i, 0))
    return pl.pallas_call(
        body, name=name, grid=(ntile,),
        in_specs=[at] * 9 + [
            pl.BlockSpec((tm, INC), lambda i: (i, 0)),
            pl.BlockSpec((SUBLANES, 3 * CW), lambda i: (jnp.maximum(i * hb - 1, 0), 1)),
            pl.BlockSpec((SUBLANES, 3 * CW), lambda i: (jnp.minimum((i + 1) * hb, nsl - 1), 1)),
            pl.BlockSpec((tm, CW), lambda i: (i, 1)),
            pl.BlockSpec((SUBLANES, CW), lambda i: (jnp.minimum((i + 1) * hb, nsl - 1), 1)),
            pl.BlockSpec((SUBLANES, AW), lambda i: (0, 0)),
            pl.BlockSpec((SUBLANES, CW), lambda i: (0, 0))],
        out_specs=[pl.BlockSpec((tm, INC), lambda i: (i, 0)),
                   pl.BlockSpec((SUBLANES, AW), lambda i: (0, 0))],
        out_shape=[jax.ShapeDtypeStruct((S, INC), MXU_DTYPE), jax.ShapeDtypeStruct((SUBLANES, AW), F32)],
        compiler_params=_params(("arbitrary",)),
    )(*dqs, *dks, *dvs, proj, proj, proj, dycat, dycat, gvec, cw)


def mixer_in_bwd(dxo, x, dproj, vec, winp, name):
    S = x.shape[0]
    tm = _row_tile(S, 512)
    pc = INC // NCHIP

    def body(dxo_ref, x_ref, dp_ref, vec_ref, w_ref, dxi_ref, sums_ref):
        xhat, r, gain, ng, sc, _, _ = _ada(x_ref[...], vec_ref)
        dh = jnp.zeros((tm, D), F32)
        for j in range(NCHIP):
            dh = dh + lax.dot_general(dp_ref[:, j * pc:(j + 1) * pc], w_ref[j], NT_DIMS, preferred_element_type=F32)
        dx, dshift, dscale, dng = _ada_bwd(dh, xhat, r, gain, ng, sc)
        dxi_ref[...] = dxo_ref[...] + dx
        _acc_rows(sums_ref, pl.program_id(0) == 0, (dshift, dscale, dng))

    t = pl.BlockSpec((tm, D), lambda i: (i, 0))
    return pl.pallas_call(
        body, name=name, grid=(S // tm,),
        in_specs=[t, t, pl.BlockSpec((tm, INC), lambda i: (i, 0)),
                  pl.BlockSpec((SUBLANES, D), lambda i: (0, 0)),
                  pl.BlockSpec((NCHIP, D, pc), lambda i: (0, 0, 0), pipeline_mode=pl.Buffered(1))],
        out_specs=[t, pl.BlockSpec((SUBLANES, D), lambda i: (0, 0))],
        out_shape=[jax.ShapeDtypeStruct((S, D), F32), jax.ShapeDtypeStruct((SUBLANES, D), F32)],
        compiler_params=_params(("arbitrary",)),
    )(dxo, x, dproj, vec, winp)


def loss_head(xf, target, name):
    S = xf.shape[0]
    tm = _row_tile(S, 1024)

    def body(x_ref, t_ref, dy_ref, l_ref):
        diff = x_ref[...] - t_ref[...]
        dy_ref[...] = diff * (1.0 / D)
        part = jnp.sum(jnp.sum(diff * diff, axis=0, keepdims=True), axis=1, keepdims=True) * (0.5 / D)

        @pl.when(pl.program_id(0) == 0)
        def _():
            l_ref[...] = jnp.zeros_like(l_ref)
        l_ref[...] += jnp.broadcast_to(part, l_ref.shape)

    t = pl.BlockSpec((tm, D), lambda i: (i, 0))
    return pl.pallas_call(
        body, name=name, grid=(S // tm,),
        in_specs=[t, t],
        out_specs=[t, pl.BlockSpec((SUBLANES, LANES), lambda i: (0, 0))],
        out_shape=[jax.ShapeDtypeStruct((S, D), F32), jax.ShapeDtypeStruct((SUBLANES, LANES), F32)],
        compiler_params=_params(("arbitrary",)),
    )(xf, target)


def _vec(mod_l, ng_l, i):
    m = mod_l.reshape(3, 3, D)
    rows = jnp.stack([ng_l[i], m[i, 1], m[i, 0], m[i, 2]])
    return jnp.concatenate([rows, jnp.zeros((SUBLANES - 4, D), F32)], axis=0)


def _grad_list(g):
    w2r = DFF // NCHIP
    return [g["w1"][0], g["w2"][0].reshape(NCHIP, w2r, D), g["win"], g["wout"].reshape(NCHIP, D // NCHIP, D),
            g["w1"][1], g["w2"][1].reshape(NCHIP, w2r, D)]


def local_step(x, target, mods, ngs, gvecs, cws, shards, w_first, cflag):
    saved = []
    weights = [dict(w1=[None, None], w2=[None, None]) for _ in range(2)]
    weights[0]["w1"][0], weights[0]["w2"][0] = w_first[0], w_first[1].reshape(DFF, D)
    h = x
    for l in range(2):
        w, sh = weights[l], shards[l]
        nxt = shards[l + 1] if l == 0 else None
        vecs = [_vec(mods[l], ngs[l], i) for i in range(3)]
        x0 = h
        (x1, a0, f0), (win, wout) = ffn_fwd(x0, vecs[0], w["w1"][0], w["w2"][0], 0.5, f"ffn_fwd_l{l}a",
                                            carry=Carry("gather", [sh["win"], sh["wout"]]))
        w["win"], w["wout"] = win, wout.reshape(D, D)
        proj, h1b, qn, kn, v = mixer_in(x1, vecs[1], w["win"], gvecs[l], f"mixer_in_l{l}")
        os_, lses_a, lses_b = [], [], []
        for d in DILATIONS:
            carry = {1: Carry("gather", [sh["w2"][1]]), 16: Carry("gather", [sh["w1"][1]])}.get(d)
            (o, la, lb), got = attn_fwd(qn, kn, v, d, f"attn_fwd_l{l}_d{d}", carry=carry)
            if d == 1:
                w["w2"][1] = got[0].reshape(DFF, D)
            if d == 16:
                w["w1"][1] = got[0]
            os_.append(o)
            lses_a.append(la)
            lses_b.append(lb)
        (ycat, *lse), got = combine_conv(os_, lses_a, lses_b, proj, cws[l], f"combine_conv_l{l}",
                                         carry=Carry("gather", [nxt["w2"][0]]) if nxt else None)
        if nxt:
            weights[1]["w2"][0] = got[0].reshape(DFF, D)
        x2, y = out_proj(ycat, x1, vecs[1], w["wout"], f"out_proj_l{l}")
        (x3, a2, f2), got = ffn_fwd(x2, vecs[2], w["w1"][1], w["w2"][1], 0.5, f"ffn_fwd_l{l}b",
                                    carry=Carry("gather", [nxt["w1"][0]]) if nxt else None)
        if nxt:
            weights[1]["w1"][0] = got[0]
        saved.append(dict(vecs=vecs, x0=x0, a0=a0, f0=f0, x1=x1, proj=proj, h1b=h1b, qn=qn, kn=kn, v=v,
                          ycat=ycat, lse=lse, y=y, x2=x2, a2=a2, f2=f2))
        h = x3
    dx, loss_blk = loss_head(h, target, "loss_head")
    sums, totals, g_prev = [None, None], [None, None], None
    for l in (1, 0):
        w, s = weights[l], saved[l]
        vecs = s["vecs"]
        ride = g_prev is not None
        (dx, hb, dfb, act, da, sums2), got = ffn_bwd(
            dx, s["x2"], s["a2"], s["f2"], vecs[2], w["w1"][1], w["w2"][1], 0.5, f"ffn_bwd_l{l}b",
            carry=Carry("swap_halves", g_prev) if ride else None)
        dw1b = wgrad(hb, da, D, HALF, f"wgrad_w1_l{l}b")
        dw2b = wgrad(act, dfb, HALF, D, f"wgrad_w2_l{l}b")
        if ride:
            wire = [add_half(g_prev[k], got[k], cflag, f"add_sibling_l{l + 1}_{k}") for k in range(6)]
        dyb, dycat, dl_a, dl_b, sums_o = out_proj_bwd(dx, s["y"], s["ycat"], vecs[1], w["wout"], f"out_proj_bwd_l{l}")
        dwout = wgrad(s["ycat"].astype(MXU_DTYPE), dyb, D // 2, D, f"wgrad_wout_l{l}")
        dqs, dks, dvs, landed = [], [], [], {}
        for d in DILATIONS:
            carry = {1: Carry("scatter", wire[3:]), 16: Carry("scatter", wire[:3])}.get(d) if ride else None
            (dq, dk, dv), landed[d] = attn_bwd(s["qn"], s["kn"], s["v"], dycat, s["lse"][0], s["lse"][1], dl_a, dl_b,
                                               d, f"attn_bwd_l{l}_d{d}", carry=carry)
            dqs.append(dq)
            dks.append(dk)
            dvs.append(dv)
        if ride:
            tot = [sum_chips(t, f"sum_chips_l{l + 1}_{k}") for k, t in enumerate(list(landed[16]) + list(landed[1]))]
        dproj, sums_m = mixer_mid_bwd(dqs, dks, dvs, s["proj"], dycat, gvecs[l], cws[l], f"mixer_mid_bwd_l{l}")
        dwin = wgrad(s["h1b"], dproj, D, INC // NCHIP, f"wgrad_win_l{l}")
        dx, sums1 = mixer_in_bwd(dx, s["x1"], dproj, vecs[1], w["win"], f"mixer_in_bwd_l{l}")
        (dx, hb, dfb, act, da, sums0), got = ffn_bwd(
            dx, s["x0"], s["a0"], s["f0"], vecs[0], w["w1"][0], w["w2"][0], 0.5, f"ffn_bwd_l{l}a",
            carry=Carry("swap", tot) if ride else None)
        if ride:
            totals[l + 1] = (tot, list(got))
        dw1a = wgrad(hb, da, D, HALF, f"wgrad_w1_l{l}a")
        dw2a = wgrad(act, dfb, HALF, D, f"wgrad_w2_l{l}a")
        g_prev = _grad_list(dict(w1=[dw1a, dw1b], w2=[dw2a, dw2b], win=dwin, wout=dwout))
        sums[l] = (sums0, sums1, sums_o, sums2, sums_m)
    from_sib = run_carry(Carry("swap_halves", g_prev), "swap_halves_l0")
    wire = [add_half(g_prev[k], from_sib[k], cflag, f"add_sibling_l0_{k}") for k in range(6)]
    landed = run_carry(Carry("scatter", wire), "scatter_grads_l0")
    tot = [sum_chips(landed[k], f"sum_chips_l0_{k}") for k in range(6)]
    totals[0] = (tot, list(run_carry(Carry("swap", tot), "swap_totals_l0")))
    return loss_blk, dx, totals, sums


def small_all_gather(blk, name):
    m_per, n = blk.shape

    def body(x_ref, out_ref, send_sems, recv_sems, local_sem):
        x, y, c = _here()
        me, sibling = (x, y, c), (x, y, 1 - c)
        chips = [(1 - x, y), (x, 1 - y), (1 - x, 1 - y)]

        def rows(px, py, pc):
            return out_ref.at[pl.ds((4 * px + 2 * py + pc) * m_per, m_per), :]

        def copy(k, block, to, src=None):
            return pltpu.make_async_remote_copy(
                src_ref=rows(*block) if src is None else src, dst_ref=rows(*block),
                send_sem=send_sems.at[k], recv_sem=recv_sems.at[k], device_id=to, device_id_type=MESH)

        mine = pltpu.make_async_copy(x_ref, rows(*me), local_sem)
        mine.start()
        first = [copy(0, me, sibling, src=x_ref)]
        first += [copy(1 + j, me, (*chip, c), src=x_ref) for j, chip in enumerate(chips)]
        for cp in first:
            cp.start()
        passed = [copy(4 + j, (*chip, c), sibling) for j, chip in enumerate(chips)]
        for j, chip in enumerate(chips):
            copy(1 + j, (*chip, c), me).wait_recv()
            passed[j].start()
        copy(0, sibling, me).wait_recv()
        for j, chip in enumerate(chips):
            copy(4 + j, (*chip, 1 - c), me).wait_recv()
        for cp in first + passed:
            cp.wait_send()
        mine.wait()

    return pl.pallas_call(
        body, name=name,
        out_shape=jax.ShapeDtypeStruct((NDEV * m_per, n), blk.dtype),
        in_specs=[pl.BlockSpec(memory_space=pltpu.VMEM)],
        out_specs=pl.BlockSpec(memory_space=pltpu.VMEM),
        scratch_shapes=[pltpu.SemaphoreType.DMA((7,)), pltpu.SemaphoreType.DMA((7,)), pltpu.SemaphoreType.DMA],
        compiler_params=pltpu.CompilerParams(vmem_limit_bytes=VMEM_LIMIT),
    )(blk)


EW_BLOCK_BYTES = 1 << 20


def _ew_rows(rows, cols):
    want = max(16, EW_BLOCK_BYTES // (4 * cols))
    best = None
    for t in range(16, rows + 1, 16):
        if rows % t == 0 and t <= want:
            best = t
    return best if best is not None else rows


def add_half(g, recv, cflag, name):
    pieces, r, cols = g.shape
    r2 = r // 2
    tr = _ew_rows(r2, cols)
    nt = r2 // tr

    def body(c_ref, g_ref, r_ref, o_ref):
        o_ref[...] = (g_ref[...] + r_ref[...]).astype(o_ref.dtype)

    half = pl.BlockSpec((None, tr, cols), lambda j, i, c_ref: (j, i, 0))
    return pl.pallas_call(
        body, name=name,
        grid_spec=pltpu.PrefetchScalarGridSpec(
            num_scalar_prefetch=1, grid=(pieces, nt),
            in_specs=[pl.BlockSpec((None, tr, cols), lambda j, i, c_ref: (j, c_ref[0] * nt + i, 0)), half],
            out_specs=half),
        out_shape=jax.ShapeDtypeStruct((pieces, r2, cols), WIRE_DTYPE),
        compiler_params=_params(("arbitrary", "arbitrary")),
    )(cflag, g, recv)


def sum_chips(recv, name):
    _, r, cols = recv.shape
    tr = _ew_rows(r, cols)

    def body(r_ref, o_ref):
        acc = r_ref[0].astype(F32)
        for k in range(1, NCHIP):
            acc = acc + r_ref[k].astype(F32)
        o_ref[...] = acc

    return pl.pallas_call(
        body, name=name, grid=(r // tr,),
        in_specs=[pl.BlockSpec((NCHIP, tr, cols), lambda i: (0, i, 0))],
        out_specs=pl.BlockSpec((tr, cols), lambda i: (i, 0)),
        out_shape=jax.ShapeDtypeStruct((r, cols), F32),
        compiler_params=_params(("arbitrary",)),
    )(recv)


def sum_devices(rows8, name):
    def body(r_ref, o_ref):
        acc = r_ref[0:1, :]
        for k in range(1, NDEV):
            acc = acc + r_ref[k:k + 1, :]
        o_ref[...] = jnp.broadcast_to(acc, o_ref.shape)

    return pl.pallas_call(
        body, name=name, out_shape=jax.ShapeDtypeStruct(rows8.shape, F32),
        in_specs=[pl.BlockSpec(memory_space=pltpu.VMEM)], out_specs=pl.BlockSpec(memory_space=pltpu.VMEM),
        compiler_params=pltpu.CompilerParams(vmem_limit_bytes=VMEM_LIMIT),
    )(rows8)


def adamw(w, m, v, srcs, cflag, name, halves=False):
    planes, r, cols = w.shape
    rh = r // 2 if halves else r
    tr = _ew_rows(rh, cols)
    nth = rh // tr
    flat = [a for s in srcs for a in (s if halves else (s,))]
    ns = len(flat)
    per = ns // planes

    def body(c_ref, w_ref, m_ref, v_ref, *rest):
        s_refs, (g_ref, d_ref, mo_ref, vo_ref) = rest[:ns], rest[ns:]
        p, i = pl.program_id(0), pl.program_id(1)
        if halves:
            mine = jnp.logical_not(jnp.logical_xor(i >= nth, c_ref[0] == 1))
            blocks = [jnp.where(mine, s_refs[2 * k][...], s_refs[2 * k + 1][...]) for k in range(planes)]
        else:
            blocks = [s[...] for s in s_refs]
        g = blocks[0]
        for k in range(1, planes):
            g = jnp.where(p == k, blocks[k], g)
        g_ref[...] = g
        m_new = ADAM_B1 * m_ref[...] + (1.0 - ADAM_B1) * g
        v_new = ADAM_B2 * v_ref[...] + (1.0 - ADAM_B2) * (g * g)
        mo_ref[...] = m_new
        vo_ref[...] = v_new
        m_hat = m_new / (1.0 - ADAM_B1 ** ADAM_STEP)
        v_hat = v_new / (1.0 - ADAM_B2 ** ADAM_STEP)
        d_ref[...] = -ADAM_LR * (m_hat / (jnp.sqrt(v_hat) + ADAM_EPS) + ADAM_WD * w_ref[...])

    pt = pl.BlockSpec((None, tr, cols), lambda p, i: (p, i, 0))
    st = [pl.BlockSpec((tr, cols), functools.partial(lambda k, p, i: (jnp.where(p == k, i % nth, 0), 0), j // per))
          for j in range(ns)]
    return pl.pallas_call(
        body, name=name, grid=(planes, r // tr),
        in_specs=[pl.BlockSpec(memory_space=pltpu.SMEM), pt, pt, pt] + st,
        out_specs=[pt] * 4,
        out_shape=[jax.ShapeDtypeStruct(w.shape, F32)] * 4,
        compiler_params=_params(("arbitrary", "arbitrary")),
    )(cflag, w, m, v, *flat)


ADA_COLS = 9 * D // NCHIP


def mod_fwd(c_all, w_ada, b_shard, name):
    def body(c_ref, w_ref, b_ref, o_ref):
        cc = c_ref[...]
        sc = cc * jax.nn.sigmoid(cc)
        o_ref[...] = jnp.dot(sc, w_ref[...], preferred_element_type=F32,
                             precision=lax.Precision.HIGHEST) + b_ref[...]

    return pl.pallas_call(
        body, name=name, grid=(2,),
        in_specs=[pl.BlockSpec((NDEV, D), lambda l: (0, 0)),
                  pl.BlockSpec((None, D, ADA_COLS), lambda l: (l, 0, 0)),
                  pl.BlockSpec((None, 1, ADA_COLS), lambda l: (l, 0, 0))],
        out_specs=pl.BlockSpec((None, NDEV, ADA_COLS), lambda l: (l, 0, 0)),
        out_shape=jax.ShapeDtypeStruct((2, NDEV, ADA_COLS), F32),
        compiler_params=_params(("arbitrary",)),
    )(c_all, w_ada, b_shard.reshape(2, 1, ADA_COLS))


def wada_grad(c_all_t, dmod, name):
    ct = ADA_COLS // 3

    def body(c_ref, d_ref, o_ref):
        cc = c_ref[...]
        sc = cc * jax.nn.sigmoid(cc)
        acc = sc[:, 0:1] * d_ref[0:1, :]
        for b in range(1, NDEV):
            acc = acc + sc[:, b:b + 1] * d_ref[b:b + 1, :]
        o_ref[...] = acc

    return pl.pallas_call(
        body, name=name, grid=(2, 3),
        in_specs=[pl.BlockSpec((D, LANES), lambda l, j: (0, 0)),
                  pl.BlockSpec((None, NDEV, ct), lambda l, j: (l, 0, j))],
        out_specs=pl.BlockSpec((None, D, ct), lambda l, j: (l, 0, j)),
        out_shape=jax.ShapeDtypeStruct((2, D, ADA_COLS), F32),
        compiler_params=_params(("arbitrary", "arbitrary")),
    )(c_all_t, dmod)


def _pad_rows(row, rows=SUBLANES):
    return jnp.concatenate([row[None, :], jnp.zeros((rows - 1, row.shape[0]), row.dtype)], axis=0)


def kernel(x, c, w_ada, b_ada, norm_g, w_in, q_norm_g, k_norm_g, conv_w, conv_b, w_out, ffn_w1, ffn_w2, loss_target, m_w_ada, m_b_ada, m_norm_g, m_w_in, m_q_norm_g, m_k_norm_g, m_conv_w, m_conv_b, m_w_out, m_ffn_w1, m_ffn_w2, v_w_ada, v_b_ada, v_norm_g, v_w_in, v_q_norm_g, v_k_norm_g, v_conv_w, v_conv_b, v_w_out, v_ffn_w1, v_ffn_w2):
    ix, iy, ic = lax.axis_index("x"), lax.axis_index("y"), lax.axis_index("c")
    chip = 2 * ix + iy
    dev = 2 * chip + ic
    cflag = jnp.reshape(ic, (1,)).astype(jnp.int32)
    ngw = norm_g.shape[-1]
    cww = conv_w.shape[-1]

    pack = jnp.concatenate([c[0], norm_g.reshape(-1), conv_w.reshape(-1)])
    got = small_all_gather(_pad_rows(pack), "gather_c_normg_convw")[::SUBLANES]
    c_all = got[:, :D]
    per_chip = got[::2]
    ng_full = jnp.concatenate([per_chip[j, D:D + 6 * ngw].reshape(2, 3, ngw) for j in range(NCHIP)], axis=-1)
    cw_full = jnp.concatenate([per_chip[j, D + 6 * ngw:].reshape(2, 3, cww) for j in range(NCHIP)], axis=-1)

    b_shard = lax.dynamic_slice_in_dim(b_ada, chip * ADA_COLS, ADA_COLS, axis=1)
    mod_blk = mod_fwd(c_all, w_ada, b_shard, "mod_fwd").reshape(2 * NDEV, ADA_COLS)
    mod_all = small_all_gather(mod_blk, "gather_mod").reshape(NDEV, 2, NDEV, ADA_COLS)[::2]
    mod_mine = lax.dynamic_index_in_dim(mod_all, dev, axis=2, keepdims=False)
    mods = [mod_mine[:, l, :].reshape(-1) for l in range(2)]

    shards, gvecs, cws = [], [], []
    for l in range(2):
        shards.append(dict(w1=[ffn_w1[l, i].astype(MXU_DTYPE) for i in range(2)],
                           w2=[ffn_w2[l, i].astype(MXU_DTYPE) for i in range(2)],
                           win=w_in[l].astype(MXU_DTYPE), wout=w_out[l].astype(MXU_DTYPE)))
        gv = jnp.stack([jnp.tile(q_norm_g[l], AW // HD), jnp.tile(k_norm_g[l], AW // HD)])
        gvecs.append(jnp.concatenate([gv, jnp.zeros((SUBLANES - 2, AW), F32)], axis=0))
        cws.append(jnp.concatenate([cw_full[l], conv_b[l][None, :], jnp.zeros((SUBLANES - 4, CW), F32)], axis=0))
    w_first = run_carry(Carry("gather", [shards[0]["w1"][0], shards[0]["w2"][0]]), "gather_first_ffn")

    loss_blk, dx, totals, sums = local_step(x[0], loss_target[0], mods, [ng_full[0], ng_full[1]], gvecs, cws,
                                            shards, w_first, cflag)
    loss = lax.psum(loss_blk[0, 0], ("x", "y", "c"))

    dmods, dngs, dqg, dkg, dcw, dcb = [], [], [], [], [], []
    for l in range(2):
        s0, s1, so, s2, sm = sums[l]
        dmods.append(jnp.concatenate([s0[0], s0[1], s0[3], s1[0], s1[1], so[0], s2[0], s2[1], s2[3]]))
        dngs.append(jnp.concatenate([s0[2], s1[2], s2[2]]))
        dqg.append(sm[0].reshape(AW // HD, HD).sum(0))
        dkg.append(sm[1].reshape(AW // HD, HD).sum(0))
        dcw.append(sm[2:5].reshape(-1))
        dcb.append(sm[5])
    small = jnp.concatenate(dmods + dngs + dqg + dkg + dcw + dcb)
    small_all = small_all_gather(_pad_rows(small), "gather_small_grads")[::SUBLANES]
    nm = 9 * D
    dmod_all = small_all[:, :2 * nm].reshape(NDEV, 2, NCHIP, ADA_COLS)
    dmod_mine = lax.dynamic_index_in_dim(dmod_all, chip, axis=2, keepdims=False).transpose(1, 0, 2)
    tot = sum_devices(small_all, "sum_small_grads")[0]
    o = 2 * nm
    g_b_ada = tot[:o].reshape(2, nm)
    g_norm_g = lax.dynamic_slice_in_dim(tot[o:o + 6 * D].reshape(2, 3, D), chip * ngw, ngw, axis=2)
    o += 6 * D
    g_qg = tot[o:o + 2 * HD].reshape(2, HD)
    o += 2 * HD
    g_kg = tot[o:o + 2 * HD].reshape(2, HD)
    o += 2 * HD
    g_cw = lax.dynamic_slice_in_dim(tot[o:o + 6 * CW].reshape(2, 3, CW), chip * cww, cww, axis=2)
    o += 6 * CW
    g_cb = tot[o:o + 2 * CW].reshape(2, CW)

    c_all_t = jnp.concatenate([c_all.T, jnp.zeros((D, LANES - NDEV), F32)], axis=1)
    g_wada_src = wada_grad(c_all_t, dmod_mine, "wada_grad")

    def halves(k_of_plane):
        return [(totals[l][0][k], totals[l][1][k]) for l, k in k_of_plane]

    r_wada = adamw(w_ada, m_w_ada, v_w_ada, [g_wada_src[0], g_wada_src[1]], cflag, "adamw_w_ada")
    r_win = adamw(w_in, m_w_in, v_w_in, halves([(0, 2), (1, 2)]), cflag, "adamw_w_in", halves=True)
    r_wout = adamw(w_out, m_w_out, v_w_out, halves([(0, 3), (1, 3)]), cflag, "adamw_w_out", halves=True)
    r_w1 = adamw(ffn_w1.reshape(4, D, HALF), m_ffn_w1.reshape(4, D, HALF), v_ffn_w1.reshape(4, D, HALF),
                 halves([(0, 0), (0, 4), (1, 0), (1, 4)]), cflag, "adamw_ffn_w1", halves=True)
    w2r = DFF // NCHIP
    r_w2 = adamw(ffn_w2.reshape(4, w2r, D), m_ffn_w2.reshape(4, w2r, D), v_ffn_w2.reshape(4, w2r, D),
                 halves([(0, 1), (0, 5), (1, 1), (1, 5)]), cflag, "adamw_ffn_w2", halves=True)
    r_w1 = [t.reshape(ffn_w1.shape) for t in r_w1]
    r_w2 = [t.reshape(ffn_w2.shape) for t in r_w2]

    smalls = [("b_ada", b_ada, m_b_ada, v_b_ada, g_b_ada), ("norm_g", norm_g, m_norm_g, v_norm_g, g_norm_g),
              ("q_norm_g", q_norm_g, m_q_norm_g, v_q_norm_g, g_qg), ("k_norm_g", k_norm_g, m_k_norm_g, v_k_norm_g, g_kg),
              ("conv_w", conv_w, m_conv_w, v_conv_w, g_cw), ("conv_b", conv_b, m_conv_b, v_conv_b, g_cb)]
    n_small = sum(t[1].size for t in smalls)
    pad = (-n_small) % (16 * LANES)

    def packed(idx):
        flat = jnp.concatenate([t[idx].reshape(-1) for t in smalls] + [jnp.zeros((pad,), F32)])
        return flat.reshape(-1, LANES)

    r_small = adamw(packed(1)[None], packed(2)[None], packed(3)[None], [packed(4)], cflag, "adamw_small")
    small_out = {}
    o = 0
    for name_, w_, _, _, _ in smalls:
        small_out[name_] = [t.reshape(-1)[o:o + w_.size].reshape(w_.shape) for t in r_small]
        o += w_.size

    res = {"w_ada": r_wada, "w_in": r_win, "w_out": r_wout, "ffn_w1": r_w1, "ffn_w2": r_w2, **small_out}
    order = ["w_ada", "b_ada", "norm_g", "w_in", "q_norm_g", "k_norm_g", "conv_w", "conv_b", "w_out", "ffn_w1", "ffn_w2"]
    outs = [loss, dx[None]]
    for k in range(4):
        outs += [res[nm_][k] for nm_ in order]
    return tuple(outs)
```

```python
import functools

import jax
import jax.numpy as jnp
from jax import lax
from jax.experimental import pallas as pl
from jax.experimental.pallas import tpu as pltpu

F32 = jnp.float32
MXU_DTYPE = jnp.bfloat16
ACT_DTYPE = jnp.bfloat16
WIRE_DTYPE = jnp.bfloat16

D = 1024
HD = 64
AW = 512
CW = 512
DFF = 2816
HALF = DFF // 2
INC = 3 * AW + 3 * CW
NCHIP = 4
NDEV = 8
QBLK = 128
ATTN_CHUNK_ROWS = 2048
DILATIONS = (1, 4, 16)
EPS = 1e-6
NEG = -1e30
LANES = 128
SUBLANES = 8
VMEM_LIMIT = 56 * 1024 * 1024

ADAM_LR = 0.001
ADAM_B1 = 0.9
ADAM_B2 = 0.999
ADAM_EPS = 1e-08
ADAM_WD = 0.01
ADAM_STEP = 10

NT_DIMS = (((1,), (1,)), ((), ()))
TN_DIMS = (((0,), (0,)), ((), ()))


def _params(sem, vmem=VMEM_LIMIT):
    return pltpu.CompilerParams(dimension_semantics=sem, vmem_limit_bytes=vmem)


def _row_tile(n, want):
    t = min(n, want)
    assert n % t == 0
    return t


def _ada(xt, vec_ref):
    ng, sc, sh, gt = vec_ref[0:1, :], vec_ref[1:2, :], vec_ref[2:3, :], vec_ref[3:4, :]
    r = lax.rsqrt(jnp.mean(xt * xt, axis=-1, keepdims=True) + EPS)
    return xt * r, r, ng * (1.0 + sc), ng, sc, sh, gt


def _ada_bwd(dh, xhat, r, gain, ng, sc):
    dshift = jnp.sum(dh, axis=0, keepdims=True)
    dhx = dh * xhat
    dscale = jnp.sum(dhx, axis=0, keepdims=True) * ng
    dng = jnp.sum(dhx, axis=0, keepdims=True) * (1.0 + sc)
    dxhat = dh * gain
    dx = r * (dxhat - xhat * jnp.mean(dxhat * xhat, axis=-1, keepdims=True))
    return dx, dshift, dscale, dng


def _acc_rows(sums_ref, first, rows):
    @pl.when(first)
    def _():
        sums_ref[...] = jnp.zeros_like(sums_ref)
    for k, row in enumerate(rows):
        sums_ref[k:k + 1, :] += row


MESH = pl.DeviceIdType.MESH
ANY = pl.BlockSpec(memory_space=pl.ANY)


def _here():
    return lax.axis_index("x"), lax.axis_index("y"), lax.axis_index("c")


def _ici_copies(src_refs, dst_refs, send_sems, recv_sems, local_sems, scatter):
    x, y, c = _here()
    my_chip = 2 * x + y
    peers = [(1 - x, y), (x, 1 - y), (1 - x, 1 - y)]
    local, out, inc = [], [], []
    for a, (src, dst) in enumerate(zip(src_refs, dst_refs)):
        local.append(pltpu.make_async_copy(src.at[my_chip] if scatter else src, dst.at[my_chip], local_sems.at[a]))
        for j, (px, py) in enumerate(peers):
            sems = dict(send_sem=send_sems.at[3 * a + j], recv_sem=recv_sems.at[3 * a + j],
                        device_id=(px, py, c), device_id_type=MESH)
            out.append(pltpu.make_async_remote_copy(
                src_ref=src.at[2 * px + py] if scatter else src, dst_ref=dst.at[my_chip], **sems))
            inc.append(pltpu.make_async_remote_copy(
                src_ref=src.at[my_chip] if scatter else src, dst_ref=dst.at[2 * px + py], **sems))
    return local, out, inc


def _swap_copies(src_refs, dst_refs, send_sems, recv_sems, halves):
    x, y, c = _here()
    cps = []
    for k, (src, dst) in enumerate(zip(src_refs, dst_refs)):
        if halves:
            r2 = src.shape[1] // 2
            src = src.at[:, pl.ds((1 - c) * r2, r2), :]
        cps.append(pltpu.make_async_remote_copy(
            src_ref=src, dst_ref=dst, send_sem=send_sems.at[k], recv_sem=recv_sems.at[k],
            device_id=(x, y, 1 - c), device_id_type=MESH))
    return cps


class Carry:
    def __init__(self, kind, srcs):
        self.kind, self.srcs, n = kind, list(srcs), len(srcs)
        if kind == "gather":
            shapes = [(NCHIP,) + s.shape for s in srcs]
        elif kind == "swap_halves":
            shapes = [(s.shape[0], s.shape[1] // 2, s.shape[2]) for s in srcs]
        else:
            shapes = [s.shape for s in srcs]
        self.out_shape = [jax.ShapeDtypeStruct(sh, s.dtype) for sh, s in zip(shapes, srcs)]
        dma = pltpu.SemaphoreType.DMA
        self.sems = [dma((3 * n,)), dma((3 * n,)), dma((n,))] if kind in ("gather", "scatter") else [dma((n,)), dma((n,))]

    def start(self, srcs, dsts, sems):
        if self.kind in ("gather", "scatter"):
            local, out, _ = _ici_copies(srcs, dsts, *sems, self.kind == "scatter")
            for cp in local + out:
                cp.start()
        else:
            for cp in _swap_copies(srcs, dsts, *sems, self.kind == "swap_halves"):
                cp.start()

    def wait(self, srcs, dsts, sems):
        if self.kind in ("gather", "scatter"):
            local, out, inc = _ici_copies(srcs, dsts, *sems, self.kind == "scatter")
            for cp in inc:
                cp.wait_recv()
            for cp in out:
                cp.wait_send()
            for cp in local:
                cp.wait()
        else:
            cps = _swap_copies(srcs, dsts, *sems, self.kind == "swap_halves")
            for cp in cps:
                cp.wait_recv()
            for cp in cps:
                cp.wait_send()


def run_carry(carry, name):
    n = len(carry.srcs)

    def body(*refs):
        srcs, dsts, sems = refs[:n], refs[n:2 * n], refs[2 * n:]
        carry.start(srcs, dsts, sems)
        carry.wait(srcs, dsts, sems)

    return pl.pallas_call(body, name=name, out_shape=carry.out_shape, in_specs=[ANY] * n, out_specs=[ANY] * n,
                          scratch_shapes=carry.sems)(*carry.srcs)


def _pcall(body, name, grid, in_specs, out_specs, out_shape, sem, args, carry=None):
    if carry is None:
        outs = pl.pallas_call(body, name=name, grid=grid, in_specs=in_specs, out_specs=out_specs,
                              out_shape=out_shape, compiler_params=_params(sem))(*args)
        return outs, []
    n_in, n_out, nc = len(in_specs), len(out_specs), len(carry.srcs)

    def wrapped(*refs):
        ins, csrc = refs[:n_in], refs[n_in:n_in + nc]
        outs, cdst = refs[n_in + nc:n_in + nc + n_out], refs[n_in + nc + n_out:n_in + 2 * nc + n_out]
        sems = refs[n_in + 2 * nc + n_out:]
        ids = [pl.program_id(a) for a in range(len(grid))]
        first = functools.reduce(jnp.logical_and, [i == 0 for i in ids])
        last = functools.reduce(jnp.logical_and, [i == g - 1 for i, g in zip(ids, grid)])

        @pl.when(first)
        def _():
            carry.start(csrc, cdst, sems)

        body(*ins, *outs)

        @pl.when(last)
        def _():
            carry.wait(csrc, cdst, sems)

    res = pl.pallas_call(
        wrapped, name=name, grid=grid,
        in_specs=list(in_specs) + [ANY] * nc, out_specs=list(out_specs) + [ANY] * nc,
        out_shape=list(out_shape) + carry.out_shape,
        scratch_shapes=carry.sems, compiler_params=_params(sem),
    )(*args, *carry.srcs)
    return res[:n_out], res[n_out:]


def ffn_fwd(x, vec, w1p, w2, gs, name, carry=None):
    S = x.shape[0]
    tm = _row_tile(S, 512)

    def body(x_ref, vec_ref, w1_ref, w2_ref, xn_ref, a_ref, f_ref):
        xt = x_ref[...]
        xhat, _, gain, _, _, sh, gt = _ada(xt, vec_ref)
        h = (xhat * gain + sh).astype(MXU_DTYPE)
        f = jnp.zeros((tm, D), F32)
        for hf in range(2):
            g = jnp.dot(h, w1_ref[hf], preferred_element_type=F32)
            up = jnp.dot(h, w1_ref[2 + hf], preferred_element_type=F32)
            a_ref[:, hf * HALF:(hf + 1) * HALF] = g.astype(a_ref.dtype)
            a_ref[:, DFF + hf * HALF:DFF + (hf + 1) * HALF] = up.astype(a_ref.dtype)
            act = (g * jax.nn.sigmoid(g) * up).astype(MXU_DTYPE)
            f = f + jnp.dot(act, w2_ref[hf * HALF:(hf + 1) * HALF, :], preferred_element_type=F32)
        xn_ref[...] = xt + (gs * gt) * f
        f_ref[...] = f.astype(f_ref.dtype)

    return _pcall(
        body, name, (S // tm,),
        [pl.BlockSpec((tm, D), lambda i: (i, 0)),
         pl.BlockSpec((SUBLANES, D), lambda i: (0, 0)),
         pl.BlockSpec((NCHIP, D, HALF), lambda i: (0, 0, 0), pipeline_mode=pl.Buffered(1)),
         pl.BlockSpec((DFF, D), lambda i: (0, 0), pipeline_mode=pl.Buffered(1))],
        [pl.BlockSpec((tm, D), lambda i: (i, 0)),
         pl.BlockSpec((tm, 2 * DFF), lambda i: (i, 0)),
         pl.BlockSpec((tm, D), lambda i: (i, 0))],
        [jax.ShapeDtypeStruct((S, D), F32),
         jax.ShapeDtypeStruct((S, 2 * DFF), ACT_DTYPE),
         jax.ShapeDtypeStruct((S, D), ACT_DTYPE)],
        ("arbitrary",), (x, vec, w1p, w2), carry)


def ffn_bwd(dxo, x, a, f, vec, w1p, w2, gs, name, carry=None):
    S = x.shape[0]
    tm = _row_tile(S, 256)

    def body(dxo_ref, x_ref, a_ref, f_ref, vec_ref, w1_ref, w2_ref,
             dxi_ref, hb_ref, dfb_ref, act_ref, da_ref, sums_ref):
        xt = x_ref[...]
        dxo = dxo_ref[...]
        xhat, r, gain, ng, sc, sh, gt = _ada(xt, vec_ref)
        hb_ref[...] = (xhat * gain + sh).astype(hb_ref.dtype)
        dgate = gs * jnp.sum(dxo * f_ref[...].astype(F32), axis=0, keepdims=True)
        df = ((gs * gt) * dxo).astype(MXU_DTYPE)
        dfb_ref[...] = df
        dh = jnp.zeros((tm, D), F32)
        for hf in range(2):
            lo, hi = hf * HALF, (hf + 1) * HALF
            dact = lax.dot_general(df, w2_ref[lo:hi, :], NT_DIMS, preferred_element_type=F32)
            g = a_ref[:, lo:hi].astype(F32)
            up = a_ref[:, DFF + lo:DFF + hi].astype(F32)
            sg = jax.nn.sigmoid(g)
            si = g * sg
            act_ref[:, lo:hi] = (si * up).astype(act_ref.dtype)
            dg = (dact * up * (sg * (1.0 + g * (1.0 - sg)))).astype(MXU_DTYPE)
            dup = (dact * si).astype(MXU_DTYPE)
            da_ref[:, lo:hi] = dg
            da_ref[:, DFF + lo:DFF + hi] = dup
            dh = dh + lax.dot_general(dg, w1_ref[hf], NT_DIMS, preferred_element_type=F32)
            dh = dh + lax.dot_general(dup, w1_ref[2 + hf], NT_DIMS, preferred_element_type=F32)
        dx, dshift, dscale, dng = _ada_bwd(dh, xhat, r, gain, ng, sc)
        dxi_ref[...] = dxo + dx
        _acc_rows(sums_ref, pl.program_id(0) == 0, (dshift, dscale, dng, dgate))

    return _pcall(
        body, name, (S // tm,),
        [pl.BlockSpec((tm, D), lambda i: (i, 0)),
         pl.BlockSpec((tm, D), lambda i: (i, 0)),
         pl.BlockSpec((tm, 2 * DFF), lambda i: (i, 0)),
         pl.BlockSpec((tm, D), lambda i: (i, 0)),
         pl.BlockSpec((SUBLANES, D), lambda i: (0, 0)),
         pl.BlockSpec((NCHIP, D, HALF), lambda i: (0, 0, 0), pipeline_mode=pl.Buffered(1)),
         pl.BlockSpec((DFF, D), lambda i: (0, 0), pipeline_mode=pl.Buffered(1))],
        [pl.BlockSpec((tm, D), lambda i: (i, 0)),
         pl.BlockSpec((tm, D), lambda i: (i, 0)),
         pl.BlockSpec((tm, D), lambda i: (i, 0)),
         pl.BlockSpec((tm, DFF), lambda i: (i, 0)),
         pl.BlockSpec((tm, 2 * DFF), lambda i: (i, 0)),
         pl.BlockSpec((SUBLANES, D), lambda i: (0, 0))],
        [jax.ShapeDtypeStruct((S, D), F32),
         jax.ShapeDtypeStruct((S, D), MXU_DTYPE),
         jax.ShapeDtypeStruct((S, D), MXU_DTYPE),
         jax.ShapeDtypeStruct((S, DFF), MXU_DTYPE),
         jax.ShapeDtypeStruct((S, 2 * DFF), MXU_DTYPE),
         jax.ShapeDtypeStruct((SUBLANES, D), F32)],
        ("arbitrary",), (dxo, x, a, f, vec, w1p, w2), carry)


def wgrad(a, b, kt, nt, name, carry=None):
    T, K = a.shape
    N = b.shape[1]
    pk, pn = K // kt, N // nt
    assert pk == 1 or pn == 1
    tt = _row_tile(T, 1024)
    steps = T // tt

    def body(a_ref, b_ref, o_ref):
        @pl.when(pl.program_id(1) == 0)
        def _():
            o_ref[...] = jnp.zeros_like(o_ref)
        o_ref[...] += lax.dot_general(a_ref[...], b_ref[...], TN_DIMS, preferred_element_type=F32)

    a_map = (lambda p, t: (t, p)) if pk > 1 else (lambda p, t: (t, 0))
    b_map = (lambda p, t: (t, p)) if pn > 1 else (lambda p, t: (t, 0))
    (out,), got = _pcall(
        body, name, (pk * pn, steps),
        [pl.BlockSpec((tt, kt), a_map), pl.BlockSpec((tt, nt), b_map)],
        [pl.BlockSpec((None, kt, nt), lambda p, t: (p, 0, 0))],
        [jax.ShapeDtypeStruct((pk * pn, kt, nt), F32)], ("arbitrary", "arbitrary"), (a, b), carry)
    return out, got


def _head_masks(rows):
    lane = lax.broadcasted_iota(jnp.int32, (rows, LANES), 1)
    return lane < HD


def _pair_stat(x, m_a):
    s_a = jnp.sum(jnp.where(m_a, x, 0.0), axis=1, keepdims=True)
    s_b = jnp.sum(jnp.where(m_a, 0.0, x), axis=1, keepdims=True)
    return s_a, s_b


def mixer_in(x, vec, winp, gvec, name):
    S = x.shape[0]
    tm = _row_tile(S, 512)
    pc = INC // NCHIP

    def body(x_ref, vec_ref, w_ref, g_ref, proj_ref, hb_ref, qn_ref, kn_ref, v_ref):
        xt = x_ref[...]
        xhat, _, gain, _, _, sh, _ = _ada(xt, vec_ref)
        h = (xhat * gain + sh).astype(MXU_DTYPE)
        hb_ref[...] = h
        for j in range(NCHIP):
            proj_ref[:, j * pc:(j + 1) * pc] = jnp.dot(h, w_ref[j], preferred_element_type=F32)
        m_a = _head_masks(tm)
        for which, dst in ((0, qn_ref), (1, kn_ref)):
            for p in range(AW // LANES):
                lo = which * AW + p * LANES
                xp = proj_ref[:, lo:lo + LANES]
                s_a, s_b = _pair_stat(xp * xp, m_a)
                rr = jnp.where(m_a, lax.rsqrt(s_a * (1.0 / HD) + EPS), lax.rsqrt(s_b * (1.0 / HD) + EPS))
                gp = g_ref[which:which + 1, p * LANES:(p + 1) * LANES]
                dst[:, p * LANES:(p + 1) * LANES] = (xp * rr * gp).astype(dst.dtype)
        v_ref[...] = proj_ref[:, 2 * AW:3 * AW].astype(v_ref.dtype)

    return pl.pallas_call(
        body, name=name, grid=(S // tm,),
        in_specs=[pl.BlockSpec((tm, D), lambda i: (i, 0)),
                  pl.BlockSpec((SUBLANES, D), lambda i: (0, 0)),
                  pl.BlockSpec((NCHIP, D, pc), lambda i: (0, 0, 0), pipeline_mode=pl.Buffered(1)),
                  pl.BlockSpec((SUBLANES, AW), lambda i: (0, 0))],
        out_specs=[pl.BlockSpec((tm, INC), lambda i: (i, 0)),
                   pl.BlockSpec((tm, D), lambda i: (i, 0)),
                   pl.BlockSpec((tm, AW), lambda i: (i, 0)),
                   pl.BlockSpec((tm, AW), lambda i: (i, 0)),
                   pl.BlockSpec((tm, AW), lambda i: (i, 0))],
        out_shape=[jax.ShapeDtypeStruct((S, INC), F32),
                   jax.ShapeDtypeStruct((S, D), MXU_DTYPE),
                   jax.ShapeDtypeStruct((S, AW), F32),
                   jax.ShapeDtypeStruct((S, AW), F32),
                   jax.ShapeDtypeStruct((S, AW), F32)],
        compiler_params=_params(("arbitrary",)),
    )(x, vec, winp, gvec)


def _band_masks(ncol):
    row = lax.broadcasted_iota(jnp.int32, (2 * QBLK, ncol), 0) & (QBLK - 1)
    col = lax.broadcasted_iota(jnp.int32, (2 * QBLK, ncol), 1)
    return row, col


def _stack_heads(t, m_a):
    zero = jnp.zeros_like(t)
    return jnp.concatenate([jnp.where(m_a, t, zero), jnp.where(m_a, zero, t)], axis=0)


def _attn_qb(d):
    return max(1, min(4, ATTN_CHUNK_ROWS // (QBLK * d)))


def _tile_rows(d, b, r):
    if d == 1:
        return pl.ds(b * QBLK, QBLK)
    return pl.ds(b * QBLK * d + r, QBLK, stride=d)


def _per_residue(d, fn):
    if d == 1:
        fn(0)
    else:
        def step(r, carry):
            fn(r)
            return carry
        lax.fori_loop(0, d, step, 0)


def attn_fwd(qn, kn, v, d, name, carry=None):
    S = qn.shape[0]
    qb = _attn_qb(d)
    halo = QBLK * d
    chunk = qb * halo

    def body(q_ref, kc_ref, kp_ref, vc_ref, vp_ref, o_ref, la_ref, lb_ref):
        i = pl.program_id(1)
        m_a = _head_masks(QBLK)
        row, col = _band_masks(2 * QBLK)
        dist = row + QBLK - col
        band = (dist >= 0) & (dist <= QBLK)
        first = band & ((i > 0) | (col >= QBLK))

        def residue(r):
            kt = [kp_ref[_tile_rows(d, 0, r), :].astype(MXU_DTYPE)]
            vt = [vp_ref[_tile_rows(d, 0, r), :].astype(MXU_DTYPE)]
            for b in range(qb):
                kt.append(kc_ref[_tile_rows(d, b, r), :].astype(MXU_DTYPE))
                vt.append(vc_ref[_tile_rows(d, b, r), :].astype(MXU_DTYPE))
            for b in range(qb):
                rows = _tile_rows(d, b, r)
                q = (q_ref[rows, :] * (HD ** -0.5)).astype(MXU_DTYPE)
                kcat = jnp.concatenate([kt[b], kt[b + 1]], axis=0)
                vcat = jnp.concatenate([vt[b], vt[b + 1]], axis=0)
                mask = first if b == 0 else band
                s = lax.dot_general(_stack_heads(q, m_a), kcat, NT_DIMS, preferred_element_type=F32)
                s = jnp.where(mask, s, NEG)
                m = jnp.max(s, axis=1, keepdims=True)
                p = jnp.exp(s - m)
                l = jnp.sum(p, axis=1, keepdims=True)
                o = jnp.dot(p.astype(MXU_DTYPE), vcat, preferred_element_type=F32) / l
                lse = jnp.broadcast_to(m + jnp.log(l), (2 * QBLK, LANES))
                o_ref[rows, :] = jnp.where(m_a, o[:QBLK], o[QBLK:])
                la_ref[rows, :] = lse[:QBLK]
                lb_ref[rows, :] = lse[QBLK:]

        _per_residue(d, residue)

    cur = pl.BlockSpec((chunk, LANES), lambda hp, i: (i, hp))
    prev = pl.BlockSpec((halo, LANES), lambda hp, i: (jnp.maximum(i * qb - 1, 0), hp))
    return _pcall(body, name, (AW // LANES, S // chunk), [cur, cur, prev, cur, prev], [cur, cur, cur],
                  [jax.ShapeDtypeStruct((S, AW), F32)] * 3, ("arbitrary", "arbitrary"), (qn, kn, kn, v, v), carry)


def attn_bwd(qn, kn, v, dycat, lse_a, lse_b, dl_a, dl_b, d, name, carry=None):
    S = qn.shape[0]
    qb = _attn_qb(d)
    halo = QBLK * d
    chunk = qb * halo
    nhalo = S // halo
    nchunk = S // chunk

    def body(q_ref, qx_ref, kc_ref, kp_ref, vc_ref, vp_ref, do_ref, dox_ref, la_ref, lax_ref, lb_ref, lbx_ref,
             da_ref, dax_ref, db_ref, dbx_ref, dq_ref, dk_ref, dv_ref):
        i = pl.program_id(1)
        has_next = i < nchunk - 1
        m_a = _head_masks(QBLK)
        row, col = _band_masks(2 * QBLK)
        dist = row + QBLK - col
        band = (dist >= 0) & (dist <= QBLK)
        first = band & ((i > 0) | (col >= QBLK))
        row1, col1 = _band_masks(QBLK)
        off_only = (col1 >= row1) & has_next

        def residue(r):
            def tiles(cur_ref, next_ref, cast):
                out = [cur_ref[_tile_rows(d, b, r), :] for b in range(qb)] + [next_ref[_tile_rows(d, 0, r), :]]
                return [t.astype(MXU_DTYPE) for t in out] if cast else out

            def ktiles(cur_ref, prev_ref):
                out = [prev_ref[_tile_rows(d, 0, r), :]] + [cur_ref[_tile_rows(d, b, r), :] for b in range(qb)]
                return [t.astype(MXU_DTYPE) for t in out]

            qt = [(t * (HD ** -0.5)).astype(MXU_DTYPE) for t in tiles(q_ref, qx_ref, False)]
            dot_ = tiles(do_ref, dox_ref, True)
            lse_t = list(zip(tiles(la_ref, lax_ref, False), tiles(lb_ref, lbx_ref, False)))
            dl_t = list(zip(tiles(da_ref, dax_ref, False), tiles(db_ref, dbx_ref, False)))
            kt = ktiles(kc_ref, kp_ref)
            vt = ktiles(vc_ref, vp_ref)
            dk_acc = [jnp.zeros((QBLK, LANES), F32) for _ in range(qb)]
            dv_acc = [jnp.zeros((QBLK, LANES), F32) for _ in range(qb)]
            for x in range(qb + 1):
                parts = 2 if x < qb else 1
                if parts == 2:
                    kcat = jnp.concatenate([kt[x], kt[x + 1]], axis=0)
                    vcat = jnp.concatenate([vt[x], vt[x + 1]], axis=0)
                else:
                    kcat, vcat = kt[x], vt[x]
                mask = first if x == 0 else (band if x < qb else off_only)
                q2 = _stack_heads(qt[x], m_a)
                do2 = _stack_heads(dot_[x], m_a)
                lse2 = jnp.concatenate(lse_t[x], axis=0)
                dl2 = jnp.concatenate(dl_t[x], axis=0)
                if parts == 2:
                    lse2 = jnp.concatenate([lse2, lse2], axis=1)
                    dl2 = jnp.concatenate([dl2, dl2], axis=1)
                s = lax.dot_general(q2, kcat, NT_DIMS, preferred_element_type=F32)
                p = jnp.exp(jnp.where(mask, s, NEG) - lse2)
                dp = lax.dot_general(do2, vcat, NT_DIMS, preferred_element_type=F32)
                ds = p * (dp - dl2)
                if x < qb:
                    dq = jnp.dot(ds.astype(MXU_DTYPE), kcat, preferred_element_type=F32)
                    dq_ref[_tile_rows(d, x, r), :] = jnp.where(m_a, dq[:QBLK], dq[QBLK:]) * (HD ** -0.5)
                lo = QBLK if x == 0 else 0
                dk = jnp.dot(ds.T.astype(MXU_DTYPE)[lo:], q2, preferred_element_type=F32)
                dv = jnp.dot(p.T.astype(MXU_DTYPE)[lo:], do2, preferred_element_type=F32)
                for part in range(parts):
                    kb = x - 1 + part
                    if 0 <= kb < qb:
                        sl = slice(part * QBLK - lo, (part + 1) * QBLK - lo)
                        dk_acc[kb] = dk_acc[kb] + dk[sl]
                        dv_acc[kb] = dv_acc[kb] + dv[sl]
            for kb in range(qb):
                dk_ref[_tile_rows(d, kb, r), :] = dk_acc[kb]
                dv_ref[_tile_rows(d, kb, r), :] = dv_acc[kb]

        _per_residue(d, residue)

    def nxt(i):
        return jnp.minimum((i + 1) * qb, nhalo - 1)

    cur = pl.BlockSpec((chunk, LANES), lambda hp, i: (i, hp))
    prev = pl.BlockSpec((halo, LANES), lambda hp, i: (jnp.maximum(i * qb - 1, 0), hp))
    nx = pl.BlockSpec((halo, LANES), lambda hp, i: (nxt(i), hp))
    return _pcall(
        body, name, (AW // LANES, nchunk),
        [cur, nx, cur, prev, cur, prev, cur, nx, cur, nx, cur, nx, cur, nx, cur, nx], [cur, cur, cur],
        [jax.ShapeDtypeStruct((S, AW), F32)] * 3, ("arbitrary", "arbitrary"),
        (qn, qn, kn, kn, v, v, dycat, dycat, lse_a, lse_a, lse_b, lse_b, dl_a, dl_a, dl_b, dl_b), carry)


def _shift_down(x, halo_prev, k, row):
    tm = x.shape[0]
    tail = jnp.concatenate([pltpu.roll(halo_prev, k, 0), jnp.zeros((tm - SUBLANES, x.shape[1]), x.dtype)], axis=0)
    return jnp.where(row < k, tail, pltpu.roll(x, k, 0))


def _shift_up(x, halo_next, k, row):
    tm = x.shape[0]
    head = jnp.concatenate([jnp.zeros((tm - SUBLANES, x.shape[1]), x.dtype), pltpu.roll(halo_next, SUBLANES - k, 0)], axis=0)
    return jnp.where(row >= tm - k, head, pltpu.roll(x, tm - k, 0))


def _conv_fwd(cu, halo_cu, cw_ref, row):
    u1 = _shift_down(cu, halo_cu, 1, row)
    u2 = _shift_down(cu, halo_cu, 2, row)
    cv = cw_ref[0:1, :] * u2 + cw_ref[1:2, :] * u1 + cw_ref[2:3, :] * cu + cw_ref[3:4, :]
    return cv, u1, u2


def combine_conv(os_, lses_a, lses_b, proj, cw, name, carry=None):
    S = proj.shape[0]
    tm = _row_tile(S, 512)
    hb = tm // SUBLANES

    def body(o1, o2, o3, a1, a2, a3, b1, b2, b3, pc_ref, ph_ref, cw_ref, ycat_ref, la_ref, lb_ref):
        i = pl.program_id(0)
        m_a = _head_masks(tm)
        for p in range(AW // LANES):
            cs = slice(p * LANES, (p + 1) * LANES)
            tot = []
            for srcs, dst in (((a1, a2, a3), la_ref), ((b1, b2, b3), lb_ref)):
                ls = [l[:, cs] for l in srcs]
                mx = jnp.maximum(jnp.maximum(ls[0], ls[1]), ls[2])
                t = mx + jnp.log(jnp.exp(ls[0] - mx) + jnp.exp(ls[1] - mx) + jnp.exp(ls[2] - mx))
                dst[:, cs] = t
                tot.append((ls, t))
            acc = jnp.zeros((tm, LANES), F32)
            for r, o in enumerate((o1, o2, o3)):
                w = jnp.where(m_a, jnp.exp(tot[0][0][r] - tot[0][1]), jnp.exp(tot[1][0][r] - tot[1][1]))
                acc = acc + w * o[:, cs]
            ycat_ref[:, cs] = acc.astype(ycat_ref.dtype)
        row = lax.broadcasted_iota(jnp.int32, (tm, CW), 0)
        gb, gc, u = pc_ref[:, 0:CW], pc_ref[:, CW:2 * CW], pc_ref[:, 2 * CW:3 * CW]
        halo_cu = jnp.where(i > 0, ph_ref[:, CW:2 * CW] * ph_ref[:, 2 * CW:3 * CW], 0.0)
        cv, _, _ = _conv_fwd(gc * u, halo_cu, cw_ref, row)
        ycat_ref[:, AW:AW + CW] = (gb * cv).astype(ycat_ref.dtype)

    ot = pl.BlockSpec((tm, AW), lambda i: (i, 0))
    return _pcall(
        body, name, (S // tm,),
        [ot] * 9 + [pl.BlockSpec((tm, 3 * CW), lambda i: (i, 1)),
                    pl.BlockSpec((SUBLANES, 3 * CW), lambda i: (jnp.maximum(i * hb - 1, 0), 1)),
                    pl.BlockSpec((SUBLANES, CW), lambda i: (0, 0))],
        [pl.BlockSpec((tm, D), lambda i: (i, 0)), ot, ot],
        [jax.ShapeDtypeStruct((S, D), ACT_DTYPE), jax.ShapeDtypeStruct((S, AW), F32),
         jax.ShapeDtypeStruct((S, AW), F32)],
        ("arbitrary",), (*os_, *lses_a, *lses_b, proj, proj, cw), carry)


def out_proj(ycat, x, vec, wout, name):
    S = x.shape[0]
    tm = _row_tile(S, 512)

    def body(yc_ref, x_ref, vec_ref, w_ref, xn_ref, y_ref):
        y = jnp.dot(yc_ref[...].astype(MXU_DTYPE), w_ref[...], preferred_element_type=F32)
        xn_ref[...] = x_ref[...] + vec_ref[3:4, :] * y
        y_ref[...] = y.astype(y_ref.dtype)

    t = pl.BlockSpec((tm, D), lambda i: (i, 0))
    return pl.pallas_call(
        body, name=name, grid=(S // tm,),
        in_specs=[t, t, pl.BlockSpec((SUBLANES, D), lambda i: (0, 0)),
                  pl.BlockSpec((D, D), lambda i: (0, 0))],
        out_specs=[t, t],
        out_shape=[jax.ShapeDtypeStruct((S, D), F32), jax.ShapeDtypeStruct((S, D), ACT_DTYPE)],
        compiler_params=_params(("arbitrary",)),
    )(ycat, x, vec, wout)


def out_proj_bwd(dxo, y, ycat, vec, wout, name, carry=None):
    S = dxo.shape[0]
    tm = _row_tile(S, 512)

    def body(dxo_ref, y_ref, yc_ref, vec_ref, w_ref, dyb_ref, dyc_ref, da_ref, db_ref, sums_ref):
        dxo = dxo_ref[...]
        dgate = jnp.sum(dxo * y_ref[...].astype(F32), axis=0, keepdims=True)
        dy = (vec_ref[3:4, :] * dxo).astype(MXU_DTYPE)
        dyb_ref[...] = dy
        dyc_ref[...] = lax.dot_general(dy, w_ref[...], NT_DIMS, preferred_element_type=F32)
        m_a = _head_masks(tm)
        for p in range(AW // LANES):
            cs = slice(p * LANES, (p + 1) * LANES)
            s_a, s_b = _pair_stat(dyc_ref[:, cs] * yc_ref[:, cs].astype(F32), m_a)
            da_ref[:, cs] = jnp.broadcast_to(s_a, (tm, LANES))
            db_ref[:, cs] = jnp.broadcast_to(s_b, (tm, LANES))
        _acc_rows(sums_ref, pl.program_id(0) == 0, (dgate,))

    t = pl.BlockSpec((tm, D), lambda i: (i, 0))
    at = pl.BlockSpec((tm, AW), lambda i: (i, 0))
    return _pcall(
        body, name, (S // tm,),
        [t, t, t, pl.BlockSpec((SUBLANES, D), lambda i: (0, 0)), pl.BlockSpec((D, D), lambda i: (0, 0))],
        [t, t, at, at, pl.BlockSpec((SUBLANES, D), lambda i: (0, 0))],
        [jax.ShapeDtypeStruct((S, D), MXU_DTYPE), jax.ShapeDtypeStruct((S, D), F32),
         jax.ShapeDtypeStruct((S, AW), F32), jax.ShapeDtypeStruct((S, AW), F32),
         jax.ShapeDtypeStruct((SUBLANES, D), F32)],
        ("arbitrary",), (dxo, y, ycat, vec, wout), carry)


def mixer_mid_bwd(dqs, dks, dvs, proj, dycat, gvec, cw, name, carry=None):
    S = proj.shape[0]
    tm = _row_tile(S, 256)
    hb = tm // SUBLANES
    nsl = S // SUBLANES
    ntile = S // tm

    def body(dq1, dq2, dq3, dk1, dk2, dk3, dv1, dv2, dv3, pr_ref, pp_ref, pn_ref, dyc_ref, dyn_ref,
             g_ref, cw_ref, dp_ref, sums_ref):
        i = pl.program_id(0)
        m_a = _head_masks(tm)
        gsum = []
        for which, parts in ((0, (dq1, dq2, dq3)), (1, (dk1, dk2, dk3))):
            acc_g = []
            for p in range(AW // LANES):
                lo = which * AW + p * LANES
                cs = slice(p * LANES, (p + 1) * LANES)
                xp = pr_ref[:, lo:lo + LANES]
                s_a, s_b = _pair_stat(xp * xp, m_a)
                rr = jnp.where(m_a, lax.rsqrt(s_a * (1.0 / HD) + EPS), lax.rsqrt(s_b * (1.0 / HD) + EPS))
                xh = xp * rr
                dn = parts[0][:, cs] + parts[1][:, cs] + parts[2][:, cs]
                acc_g.append(jnp.sum(dn * xh, axis=0, keepdims=True))
                t = dn * g_ref[which:which + 1, cs]
                t_a, t_b = _pair_stat(t * xh, m_a)
                mean = jnp.where(m_a, t_a, t_b) * (1.0 / HD)
                dp_ref[:, lo:lo + LANES] = (rr * (t - xh * mean)).astype(dp_ref.dtype)
            gsum.append(jnp.concatenate(acc_g, axis=1))
        dp_ref[:, 2 * AW:3 * AW] = (dv1[...] + dv2[...] + dv3[...]).astype(dp_ref.dtype)
        row = lax.broadcasted_iota(jnp.int32, (tm, CW), 0)
        base = 3 * AW
        gb, gc, u = pr_ref[:, base:base + CW], pr_ref[:, base + CW:base + 2 * CW], pr_ref[:, base + 2 * CW:base + 3 * CW]
        cu = gc * u
        halo_cu = jnp.where(i > 0, pp_ref[:, CW:2 * CW] * pp_ref[:, 2 * CW:3 * CW], 0.0)
        cv, u1, u2 = _conv_fwd(cu, halo_cu, cw_ref, row)
        dyc = dyc_ref[...]
        dp_ref[:, base:base + CW] = (dyc * cv).astype(dp_ref.dtype)
        dcv = dyc * gb
        halo_dcv = jnp.where(i < ntile - 1, dyn_ref[...] * pn_ref[:, 0:CW], 0.0)
        d1 = _shift_up(dcv, halo_dcv, 1, row)
        d2 = _shift_up(dcv, halo_dcv, 2, row)
        dcu = cw_ref[2:3, :] * dcv + cw_ref[1:2, :] * d1 + cw_ref[0:1, :] * d2
        dp_ref[:, base + CW:base + 2 * CW] = (dcu * u).astype(dp_ref.dtype)
        dp_ref[:, base + 2 * CW:base + 3 * CW] = (dcu * gc).astype(dp_ref.dtype)
        rows = (gsum[0], gsum[1],
                jnp.sum(dcv * u2, axis=0, keepdims=True), jnp.sum(dcv * u1, axis=0, keepdims=True),
                jnp.sum(dcv * cu, axis=0, keepdims=True), jnp.sum(dcv, axis=0, keepdims=True))
        _acc_rows(sums_ref, i == 0, rows)

    at = pl.BlockSpec((tm, AW), lambda i: (i, 0))
    return _pcall(
        body, name, (ntile,),
        [at] * 9 + [
            pl.BlockSpec((tm, INC), lambda i: (i, 0)),
            pl.BlockSpec((SUBLANES, 3 * CW), lambda i: (jnp.maximum(i * hb - 1, 0), 1)),
            pl.BlockSpec((SUBLANES, 3 * CW), lambda i: (jnp.minimum((i + 1) * hb, nsl - 1), 1)),
            pl.BlockSpec((tm, CW), lambda i: (i, 1)),
            pl.BlockSpec((SUBLANES, CW), lambda i: (jnp.minimum((i + 1) * hb, nsl - 1), 1)),
            pl.BlockSpec((SUBLANES, AW), lambda i: (0, 0)),
            pl.BlockSpec((SUBLANES, CW), lambda i: (0, 0))],
        [pl.BlockSpec((tm, INC), lambda i: (i, 0)), pl.BlockSpec((SUBLANES, AW), lambda i: (0, 0))],
        [jax.ShapeDtypeStruct((S, INC), MXU_DTYPE), jax.ShapeDtypeStruct((SUBLANES, AW), F32)],
        ("arbitrary",), (*dqs, *dks, *dvs, proj, proj, proj, dycat, dycat, gvec, cw), carry)


def mixer_in_bwd(dxo, x, dproj, vec, winp, name, carry=None):
    S = x.shape[0]
    tm = _row_tile(S, 512)
    pc = INC // NCHIP

    def body(dxo_ref, x_ref, dp_ref, vec_ref, w_ref, dxi_ref, sums_ref):
        xhat, r, gain, ng, sc, _, _ = _ada(x_ref[...], vec_ref)
        dh = jnp.zeros((tm, D), F32)
        for j in range(NCHIP):
            dh = dh + lax.dot_general(dp_ref[:, j * pc:(j + 1) * pc], w_ref[j], NT_DIMS, preferred_element_type=F32)
        dx, dshift, dscale, dng = _ada_bwd(dh, xhat, r, gain, ng, sc)
        dxi_ref[...] = dxo_ref[...] + dx
        _acc_rows(sums_ref, pl.program_id(0) == 0, (dshift, dscale, dng))

    t = pl.BlockSpec((tm, D), lambda i: (i, 0))
    return _pcall(
        body, name, (S // tm,),
        [t, t, pl.BlockSpec((tm, INC), lambda i: (i, 0)),
         pl.BlockSpec((SUBLANES, D), lambda i: (0, 0)),
         pl.BlockSpec((NCHIP, D, pc), lambda i: (0, 0, 0), pipeline_mode=pl.Buffered(1))],
        [t, pl.BlockSpec((SUBLANES, D), lambda i: (0, 0))],
        [jax.ShapeDtypeStruct((S, D), F32), jax.ShapeDtypeStruct((SUBLANES, D), F32)],
        ("arbitrary",), (dxo, x, dproj, vec, winp), carry)


def loss_head(xf, target, name):
    S = xf.shape[0]
    tm = _row_tile(S, 1024)

    def body(x_ref, t_ref, dy_ref, l_ref):
        diff = x_ref[...] - t_ref[...]
        dy_ref[...] = diff * (1.0 / D)
        part = jnp.sum(jnp.sum(diff * diff, axis=0, keepdims=True), axis=1, keepdims=True) * (0.5 / D)

        @pl.when(pl.program_id(0) == 0)
        def _():
            l_ref[...] = jnp.zeros_like(l_ref)
        l_ref[...] += jnp.broadcast_to(part, l_ref.shape)

    t = pl.BlockSpec((tm, D), lambda i: (i, 0))
    return pl.pallas_call(
        body, name=name, grid=(S // tm,),
        in_specs=[t, t],
        out_specs=[t, pl.BlockSpec((SUBLANES, LANES), lambda i: (0, 0))],
        out_shape=[jax.ShapeDtypeStruct((S, D), F32), jax.ShapeDtypeStruct((SUBLANES, LANES), F32)],
        compiler_params=_params(("arbitrary",)),
    )(xf, target)


def _vec(mod_l, ng_l, i):
    m = mod_l.reshape(3, 3, D)
    rows = jnp.stack([ng_l[i], m[i, 1], m[i, 0], m[i, 2]])
    return jnp.concatenate([rows, jnp.zeros((SUBLANES - 4, D), F32)], axis=0)


def local_step(x, target, mods, ngs, gvecs, cws, shards, w_first, cflag):
    saved = []
    weights = [dict(w1=[None, None], w2=[None, None]) for _ in range(2)]
    weights[0]["w1"][0], weights[0]["w2"][0] = w_first[0], w_first[1].reshape(DFF, D)
    h = x
    for l in range(2):
        w, sh = weights[l], shards[l]
        nxt = shards[l + 1] if l == 0 else None
        vecs = [_vec(mods[l], ngs[l], i) for i in range(3)]
        x0 = h
        (x1, a0, f0), (win, wout) = ffn_fwd(x0, vecs[0], w["w1"][0], w["w2"][0], 0.5, f"ffn_fwd_l{l}a",
                                            carry=Carry("gather", [sh["win"], sh["wout"]]))
        w["win"], w["wout"] = win, wout.reshape(D, D)
        proj, h1b, qn, kn, v = mixer_in(x1, vecs[1], w["win"], gvecs[l], f"mixer_in_l{l}")
        os_, lses_a, lses_b = [], [], []
        for d in DILATIONS:
            carry = {1: Carry("gather", [sh["w2"][1]]), 16: Carry("gather", [sh["w1"][1]])}.get(d)
            (o, la, lb), got = attn_fwd(qn, kn, v, d, f"attn_fwd_l{l}_d{d}", carry=carry)
            if d == 1:
                w["w2"][1] = got[0].reshape(DFF, D)
            if d == 16:
                w["w1"][1] = got[0]
            os_.append(o)
            lses_a.append(la)
            lses_b.append(lb)
        (ycat, *lse), got = combine_conv(os_, lses_a, lses_b, proj, cws[l], f"combine_conv_l{l}",
                                         carry=Carry("gather", [nxt["w2"][0]]) if nxt else None)
        if nxt:
            weights[1]["w2"][0] = got[0].reshape(DFF, D)
        x2, y = out_proj(ycat, x1, vecs[1], w["wout"], f"out_proj_l{l}")
        (x3, a2, f2), got = ffn_fwd(x2, vecs[2], w["w1"][1], w["w2"][1], 0.5, f"ffn_fwd_l{l}b",
                                    carry=Carry("gather", [nxt["w1"][0]]) if nxt else None)
        if nxt:
            weights[1]["w1"][0] = got[0]
        saved.append(dict(vecs=vecs, x0=x0, a0=a0, f0=f0, x1=x1, proj=proj, h1b=h1b, qn=qn, kn=kn, v=v,
                          ycat=ycat, lse=lse, y=y, x2=x2, a2=a2, f2=f2))
        h = x3
    dx, loss_blk = loss_head(h, target, "loss_head")
    sums, totals, g_prev = [None, None], [None, None], None
    w2r = DFF // NCHIP
    for l in (1, 0):
        w, s = weights[l], saved[l]
        vecs = s["vecs"]
        ride = g_prev is not None
        own = l == 0
        mine, other = [None] * 6, [None] * 6

        def half_sum(group, recv, k0):
            return [add_half(g, r, cflag, f"add_sibling_l{l}_{k0 + j}") for j, (g, r) in enumerate(zip(group, recv))]

        def chip_sum(landed, k0):
            return [sum_chips(t, f"sum_chips_l{l}_{k0 + j}") for j, t in enumerate(landed)]

        (dx, hb, dfb, act, da, sums2), got = ffn_bwd(
            dx, s["x2"], s["a2"], s["f2"], vecs[2], w["w1"][1], w["w2"][1], 0.5, f"ffn_bwd_l{l}b",
            carry=Carry("swap_halves", g_prev) if ride else None)
        dw1b, _ = wgrad(hb, da, D, HALF, f"wgrad_w1_l{l}b")
        dw2b, _ = wgrad(act, dfb, HALF, D, f"wgrad_w2_l{l}b")
        if ride:
            wire = [add_half(g_prev[k], got[k], cflag, f"add_sibling_l{l + 1}_{k}") for k in range(6)]
        g_ffn_b = [dw1b, dw2b.reshape(NCHIP, w2r, D)]
        (dyb, dycat, dl_a, dl_b, sums_o), got = out_proj_bwd(
            dx, s["y"], s["ycat"], vecs[1], w["wout"], f"out_proj_bwd_l{l}",
            carry=Carry("swap_halves", g_ffn_b) if own else None)
        dwout, _ = wgrad(s["ycat"].astype(MXU_DTYPE), dyb, D // 2, D, f"wgrad_wout_l{l}")
        if own:
            wire_ffn_b = half_sum(g_ffn_b, got, 4)
        dqs, dks, dvs, landed = [], [], [], {}
        for d in DILATIONS:
            carry = None
            if ride and d == 1:
                carry = Carry("scatter", wire[3:])
            if ride and d == 16:
                carry = Carry("scatter", wire[:3])
            if own and d == 4:
                carry = Carry("scatter", wire_ffn_b)
            (dq, dk, dv), landed[d] = attn_bwd(s["qn"], s["kn"], s["v"], dycat, s["lse"][0], s["lse"][1], dl_a, dl_b,
                                               d, f"attn_bwd_l{l}_d{d}", carry=carry)
            dqs.append(dq)
            dks.append(dk)
            dvs.append(dv)
        if ride:
            tot = [sum_chips(t, f"sum_chips_l{l + 1}_{k}") for k, t in enumerate(list(landed[16]) + list(landed[1]))]
        if own:
            mine[4:6] = chip_sum(landed[4], 4)
        (dproj, sums_m), got = mixer_mid_bwd(dqs, dks, dvs, s["proj"], dycat, gvecs[l], cws[l], f"mixer_mid_bwd_l{l}",
                                             carry=Carry("swap", mine[4:6]) if own else None)
        if own:
            other[4:6] = list(got)
        dwin, _ = wgrad(s["h1b"], dproj, D, INC // NCHIP, f"wgrad_win_l{l}")
        g_mixer = [dwin, dwout.reshape(NCHIP, D // NCHIP, D)]
        (dx, sums1), got = mixer_in_bwd(dx, s["x1"], dproj, vecs[1], w["win"], f"mixer_in_bwd_l{l}",
                                        carry=Carry("swap_halves", g_mixer) if own else None)
        if own:
            wire_mixer = half_sum(g_mixer, got, 2)
        (dx, hb, dfb, act, da, sums0), got = ffn_bwd(
            dx, s["x0"], s["a0"], s["f0"], vecs[0], w["w1"][0], w["w2"][0], 0.5, f"ffn_bwd_l{l}a",
            carry=Carry("swap", tot) if ride else None)
        if ride:
            totals[l + 1] = (tot, list(got))
        dw1a, got = wgrad(hb, da, D, HALF, f"wgrad_w1_l{l}a", carry=Carry("scatter", wire_mixer) if own else None)
        if own:
            mine[2:4] = chip_sum(got, 2)
        dw2a, got = wgrad(act, dfb, HALF, D, f"wgrad_w2_l{l}a", carry=Carry("swap", mine[2:4]) if own else None)
        g_ffn_a = [dw1a, dw2a.reshape(NCHIP, w2r, D)]
        if own:
            other[2:4] = list(got)
            wire_ffn_a = half_sum(g_ffn_a, run_carry(Carry("swap_halves", g_ffn_a), "swap_halves_tail"), 0)
            mine[0:2] = chip_sum(run_carry(Carry("scatter", wire_ffn_a), "scatter_grads_tail"), 0)
            other[0:2] = list(run_carry(Carry("swap", mine[0:2]), "swap_totals_tail"))
            totals[l] = (mine, other)
        g_prev = g_ffn_a + g_mixer + g_ffn_b
        sums[l] = (sums0, sums1, sums_o, sums2, sums_m)
    return loss_blk, dx, totals, sums


def small_all_gather(blk, name):
    m_per, n = blk.shape

    def body(x_ref, out_ref, send_sems, recv_sems, local_sem):
        x, y, c = _here()
        me, sibling = (x, y, c), (x, y, 1 - c)
        chips = [(1 - x, y), (x, 1 - y), (1 - x, 1 - y)]

        def rows(px, py, pc):
            return out_ref.at[pl.ds((4 * px + 2 * py + pc) * m_per, m_per), :]

        def copy(k, block, to, src=None):
            return pltpu.make_async_remote_copy(
                src_ref=rows(*block) if src is None else src, dst_ref=rows(*block),
                send_sem=send_sems.at[k], recv_sem=recv_sems.at[k], device_id=to, device_id_type=MESH)

        mine = pltpu.make_async_copy(x_ref, rows(*me), local_sem)
        mine.start()
        first = [copy(0, me, sibling, src=x_ref)]
        first += [copy(1 + j, me, (*chip, c), src=x_ref) for j, chip in enumerate(chips)]
        for cp in first:
            cp.start()
        passed = [copy(4 + j, (*chip, c), sibling) for j, chip in enumerate(chips)]
        for j, chip in enumerate(chips):
            copy(1 + j, (*chip, c), me).wait_recv()
            passed[j].start()
        copy(0, sibling, me).wait_recv()
        for j, chip in enumerate(chips):
            copy(4 + j, (*chip, 1 - c), me).wait_recv()
        for cp in first + passed:
            cp.wait_send()
        mine.wait()

    return pl.pallas_call(
        body, name=name,
        out_shape=jax.ShapeDtypeStruct((NDEV * m_per, n), blk.dtype),
        in_specs=[pl.BlockSpec(memory_space=pltpu.VMEM)],
        out_specs=pl.BlockSpec(memory_space=pltpu.VMEM),
        scratch_shapes=[pltpu.SemaphoreType.DMA((7,)), pltpu.SemaphoreType.DMA((7,)), pltpu.SemaphoreType.DMA],
        compiler_params=pltpu.CompilerParams(vmem_limit_bytes=VMEM_LIMIT),
    )(blk)


EW_BLOCK_BYTES = 1 << 20


def _ew_rows(rows, cols):
    want = max(16, EW_BLOCK_BYTES // (4 * cols))
    best = None
    for t in range(16, rows + 1, 16):
        if rows % t == 0 and t <= want:
            best = t
    return best if best is not None else rows


def add_half(g, recv, cflag, name):
    pieces, r, cols = g.shape
    r2 = r // 2
    tr = _ew_rows(r2, cols)
    nt = r2 // tr

    def body(c_ref, g_ref, r_ref, o_ref):
        o_ref[...] = (g_ref[...] + r_ref[...]).astype(o_ref.dtype)

    half = pl.BlockSpec((None, tr, cols), lambda j, i, c_ref: (j, i, 0))
    return pl.pallas_call(
        body, name=name,
        grid_spec=pltpu.PrefetchScalarGridSpec(
            num_scalar_prefetch=1, grid=(pieces, nt),
            in_specs=[pl.BlockSpec((None, tr, cols), lambda j, i, c_ref: (j, c_ref[0] * nt + i, 0)), half],
            out_specs=half),
        out_shape=jax.ShapeDtypeStruct((pieces, r2, cols), WIRE_DTYPE),
        compiler_params=_params(("arbitrary", "arbitrary")),
    )(cflag, g, recv)


def sum_chips(recv, name):
    _, r, cols = recv.shape
    tr = _ew_rows(r, cols)

    def body(r_ref, o_ref):
        acc = r_ref[0].astype(F32)
        for k in range(1, NCHIP):
            acc = acc + r_ref[k].astype(F32)
        o_ref[...] = acc

    return pl.pallas_call(
        body, name=name, grid=(r // tr,),
        in_specs=[pl.BlockSpec((NCHIP, tr, cols), lambda i: (0, i, 0))],
        out_specs=pl.BlockSpec((tr, cols), lambda i: (i, 0)),
        out_shape=jax.ShapeDtypeStruct((r, cols), F32),
        compiler_params=_params(("arbitrary",)),
    )(recv)


def sum_devices(rows8, name):
    def body(r_ref, o_ref):
        acc = r_ref[0:1, :]
        for k in range(1, NDEV):
            acc = acc + r_ref[k:k + 1, :]
        o_ref[...] = jnp.broadcast_to(acc, o_ref.shape)

    return pl.pallas_call(
        body, name=name, out_shape=jax.ShapeDtypeStruct(rows8.shape, F32),
        in_specs=[pl.BlockSpec(memory_space=pltpu.VMEM)], out_specs=pl.BlockSpec(memory_space=pltpu.VMEM),
        compiler_params=pltpu.CompilerParams(vmem_limit_bytes=VMEM_LIMIT),
    )(rows8)


def adamw(w, m, v, srcs, cflag, name, halves=False):
    planes, r, cols = w.shape
    rh = r // 2 if halves else r
    tr = _ew_rows(rh, cols)
    nth = rh // tr
    flat = [a for s in srcs for a in (s if halves else (s,))]
    ns = len(flat)
    per = ns // planes

    def body(c_ref, w_ref, m_ref, v_ref, *rest):
        s_refs, (g_ref, d_ref, mo_ref, vo_ref) = rest[:ns], rest[ns:]
        p, i = pl.program_id(0), pl.program_id(1)
        if halves:
            mine = jnp.logical_not(jnp.logical_xor(i >= nth, c_ref[0] == 1))
            blocks = [jnp.where(mine, s_refs[2 * k][...], s_refs[2 * k + 1][...]) for k in range(planes)]
        else:
            blocks = [s[...] for s in s_refs]
        g = blocks[0]
        for k in range(1, planes):
            g = jnp.where(p == k, blocks[k], g)
        g_ref[...] = g
        m_new = ADAM_B1 * m_ref[...] + (1.0 - ADAM_B1) * g
        v_new = ADAM_B2 * v_ref[...] + (1.0 - ADAM_B2) * (g * g)
        mo_ref[...] = m_new
        vo_ref[...] = v_new
        m_hat = m_new / (1.0 - ADAM_B1 ** ADAM_STEP)
        v_hat = v_new / (1.0 - ADAM_B2 ** ADAM_STEP)
        d_ref[...] = -ADAM_LR * (m_hat / (jnp.sqrt(v_hat) + ADAM_EPS) + ADAM_WD * w_ref[...])

    pt = pl.BlockSpec((None, tr, cols), lambda p, i: (p, i, 0))
    st = [pl.BlockSpec((tr, cols), functools.partial(lambda k, p, i: (jnp.where(p == k, i % nth, 0), 0), j // per))
          for j in range(ns)]
    return pl.pallas_call(
        body, name=name, grid=(planes, r // tr),
        in_specs=[pl.BlockSpec(memory_space=pltpu.SMEM), pt, pt, pt] + st,
        out_specs=[pt] * 4,
        out_shape=[jax.ShapeDtypeStruct(w.shape, F32)] * 4,
        compiler_params=_params(("arbitrary", "arbitrary")),
    )(cflag, w, m, v, *flat)


ADA_COLS = 9 * D // NCHIP


def mod_fwd(c_all, w_ada, b_shard, name):
    def body(c_ref, w_ref, b_ref, o_ref):
        cc = c_ref[...]
        sc = cc * jax.nn.sigmoid(cc)
        o_ref[...] = jnp.dot(sc, w_ref[...], preferred_element_type=F32,
                             precision=lax.Precision.HIGHEST) + b_ref[...]

    return pl.pallas_call(
        body, name=name, grid=(2,),
        in_specs=[pl.BlockSpec((NDEV, D), lambda l: (0, 0)),
                  pl.BlockSpec((None, D, ADA_COLS), lambda l: (l, 0, 0)),
                  pl.BlockSpec((None, 1, ADA_COLS), lambda l: (l, 0, 0))],
        out_specs=pl.BlockSpec((None, NDEV, ADA_COLS), lambda l: (l, 0, 0)),
        out_shape=jax.ShapeDtypeStruct((2, NDEV, ADA_COLS), F32),
        compiler_params=_params(("arbitrary",)),
    )(c_all, w_ada, b_shard.reshape(2, 1, ADA_COLS))


def wada_grad(c_all_t, dmod, name):
    ct = ADA_COLS // 3

    def body(c_ref, d_ref, o_ref):
        cc = c_ref[...]
        sc = cc * jax.nn.sigmoid(cc)
        acc = sc[:, 0:1] * d_ref[0:1, :]
        for b in range(1, NDEV):
            acc = acc + sc[:, b:b + 1] * d_ref[b:b + 1, :]
        o_ref[...] = acc

    return pl.pallas_call(
        body, name=name, grid=(2, 3),
        in_specs=[pl.BlockSpec((D, LANES), lambda l, j: (0, 0)),
                  pl.BlockSpec((None, NDEV, ct), lambda l, j: (l, 0, j))],
        out_specs=pl.BlockSpec((None, D, ct), lambda l, j: (l, 0, j)),
        out_shape=jax.ShapeDtypeStruct((2, D, ADA_COLS), F32),
        compiler_params=_params(("arbitrary", "arbitrary")),
    )(c_all_t, dmod)


def _pad_rows(row, rows=SUBLANES):
    return jnp.concatenate([row[None, :], jnp.zeros((rows - 1, row.shape[0]), row.dtype)], axis=0)


def kernel(x, c, w_ada, b_ada, norm_g, w_in, q_norm_g, k_norm_g, conv_w, conv_b, w_out, ffn_w1, ffn_w2, loss_target, m_w_ada, m_b_ada, m_norm_g, m_w_in, m_q_norm_g, m_k_norm_g, m_conv_w, m_conv_b, m_w_out, m_ffn_w1, m_ffn_w2, v_w_ada, v_b_ada, v_norm_g, v_w_in, v_q_norm_g, v_k_norm_g, v_conv_w, v_conv_b, v_w_out, v_ffn_w1, v_ffn_w2):
    ix, iy, ic = lax.axis_index("x"), lax.axis_index("y"), lax.axis_index("c")
    chip = 2 * ix + iy
    dev = 2 * chip + ic
    cflag = jnp.reshape(ic, (1,)).astype(jnp.int32)
    ngw = norm_g.shape[-1]
    cww = conv_w.shape[-1]

    pack = jnp.concatenate([c[0], norm_g.reshape(-1), conv_w.reshape(-1)])
    got = small_all_gather(_pad_rows(pack), "gather_c_normg_convw")[::SUBLANES]
    c_all = got[:, :D]
    per_chip = got[::2]
    ng_full = jnp.concatenate([per_chip[j, D:D + 6 * ngw].reshape(2, 3, ngw) for j in range(NCHIP)], axis=-1)
    cw_full = jnp.concatenate([per_chip[j, D + 6 * ngw:].reshape(2, 3, cww) for j in range(NCHIP)], axis=-1)

    b_shard = lax.dynamic_slice_in_dim(b_ada, chip * ADA_COLS, ADA_COLS, axis=1)
    mod_blk = mod_fwd(c_all, w_ada, b_shard, "mod_fwd").reshape(2 * NDEV, ADA_COLS)
    mod_all = small_all_gather(mod_blk, "gather_mod").reshape(NDEV, 2, NDEV, ADA_COLS)[::2]
    mod_mine = lax.dynamic_index_in_dim(mod_all, dev, axis=2, keepdims=False)
    mods = [mod_mine[:, l, :].reshape(-1) for l in range(2)]

    shards, gvecs, cws = [], [], []
    for l in range(2):
        shards.append(dict(w1=[ffn_w1[l, i].astype(MXU_DTYPE) for i in range(2)],
                           w2=[ffn_w2[l, i].astype(MXU_DTYPE) for i in range(2)],
                           win=w_in[l].astype(MXU_DTYPE), wout=w_out[l].astype(MXU_DTYPE)))
        gv = jnp.stack([jnp.tile(q_norm_g[l], AW // HD), jnp.tile(k_norm_g[l], AW // HD)])
        gvecs.append(jnp.concatenate([gv, jnp.zeros((SUBLANES - 2, AW), F32)], axis=0))
        cws.append(jnp.concatenate([cw_full[l], conv_b[l][None, :], jnp.zeros((SUBLANES - 4, CW), F32)], axis=0))
    w_first = run_carry(Carry("gather", [shards[0]["w1"][0], shards[0]["w2"][0]]), "gather_first_ffn")

    loss_blk, dx, totals, sums = local_step(x[0], loss_target[0], mods, [ng_full[0], ng_full[1]], gvecs, cws,
                                            shards, w_first, cflag)
    loss = lax.psum(loss_blk[0, 0], ("x", "y", "c"))

    dmods, dngs, dqg, dkg, dcw, dcb = [], [], [], [], [], []
    for l in range(2):
        s0, s1, so, s2, sm = sums[l]
        dmods.append(jnp.concatenate([s0[0], s0[1], s0[3], s1[0], s1[1], so[0], s2[0], s2[1], s2[3]]))
        dngs.append(jnp.concatenate([s0[2], s1[2], s2[2]]))
        dqg.append(sm[0].reshape(AW // HD, HD).sum(0))
        dkg.append(sm[1].reshape(AW // HD, HD).sum(0))
        dcw.append(sm[2:5].reshape(-1))
        dcb.append(sm[5])
    small = jnp.concatenate(dmods + dngs + dqg + dkg + dcw + dcb)
    small_all = small_all_gather(_pad_rows(small), "gather_small_grads")[::SUBLANES]
    nm = 9 * D
    dmod_all = small_all[:, :2 * nm].reshape(NDEV, 2, NCHIP, ADA_COLS)
    dmod_mine = lax.dynamic_index_in_dim(dmod_all, chip, axis=2, keepdims=False).transpose(1, 0, 2)
    tot = sum_devices(small_all, "sum_small_grads")[0]
    o = 2 * nm
    g_b_ada = tot[:o].reshape(2, nm)
    g_norm_g = lax.dynamic_slice_in_dim(tot[o:o + 6 * D].reshape(2, 3, D), chip * ngw, ngw, axis=2)
    o += 6 * D
    g_qg = tot[o:o + 2 * HD].reshape(2, HD)
    o += 2 * HD
    g_kg = tot[o:o + 2 * HD].reshape(2, HD)
    o += 2 * HD
    g_cw = lax.dynamic_slice_in_dim(tot[o:o + 6 * CW].reshape(2, 3, CW), chip * cww, cww, axis=2)
    o += 6 * CW
    g_cb = tot[o:o + 2 * CW].reshape(2, CW)

    c_all_t = jnp.concatenate([c_all.T, jnp.zeros((D, LANES - NDEV), F32)], axis=1)
    g_wada_src = wada_grad(c_all_t, dmod_mine, "wada_grad")

    def halves(k_of_plane):
        return [(totals[l][0][k], totals[l][1][k]) for l, k in k_of_plane]

    r_wada = adamw(w_ada, m_w_ada, v_w_ada, [g_wada_src[0], g_wada_src[1]], cflag, "adamw_w_ada")
    r_win = adamw(w_in, m_w_in, v_w_in, halves([(0, 2), (1, 2)]), cflag, "adamw_w_in", halves=True)
    r_wout = adamw(w_out, m_w_out, v_w_out, halves([(0, 3), (1, 3)]), cflag, "adamw_w_out", halves=True)
    r_w1 = adamw(ffn_w1.reshape(4, D, HALF), m_ffn_w1.reshape(4, D, HALF), v_ffn_w1.reshape(4, D, HALF),
                 halves([(0, 0), (0, 4), (1, 0), (1, 4)]), cflag, "adamw_ffn_w1", halves=True)
    w2r = DFF // NCHIP
    r_w2 = adamw(ffn_w2.reshape(4, w2r, D), m_ffn_w2.reshape(4, w2r, D), v_ffn_w2.reshape(4, w2r, D),
                 halves([(0, 1), (0, 5), (1, 1), (1, 5)]), cflag, "adamw_ffn_w2", halves=True)
    r_w1 = [t.reshape(ffn_w1.shape) for t in r_w1]
    r_w2 = [t.reshape(ffn_w2.shape) for t in r_w2]

    smalls = [("b_ada", b_ada, m_b_ada, v_b_ada, g_b_ada), ("norm_g", norm_g, m_norm_g, v_norm_g, g_norm_g),
              ("q_norm_g", q_norm_g, m_q_norm_g, v_q_norm_g, g_qg), ("k_norm_g", k_norm_g, m_k_norm_g, v_k_norm_g, g_kg),
              ("conv_w", conv_w, m_conv_w, v_conv_w, g_cw), ("conv_b", conv_b, m_conv_b, v_conv_b, g_cb)]
    n_small = sum(t[1].size for t in smalls)
    pad = (-n_small) % (16 * LANES)

    def packed(idx):
        flat = jnp.concatenate([t[idx].reshape(-1) for t in smalls] + [jnp.zeros((pad,), F32)])
        return flat.reshape(-1, LANES)

    r_small = adamw(packed(1)[None], packed(2)[None], packed(3)[None], [packed(4)], cflag, "adamw_small")
    small_out = {}
    o = 0
    for name_, w_, _, _, _ in smalls:
        small_out[name_] = [t.reshape(-1)[o:o + w_.size].reshape(w_.shape) for t in r_small]
        o += w_.size

    res = {"w_ada": r_wada, "w_in": r_win, "w_out": r_wout, "ffn_w1": r_w1, "ffn_w2": r_w2, **small_out}
    order = ["w_ada", "b_ada", "norm_g", "w_in", "q_norm_g", "k_norm_g", "conv_w", "conv_b", "w_out", "ffn_w1", "ffn_w2"]
    outs = [loss, dx[None]]
    for k in range(4):
        outs += [res[nm_][k] for nm_ in order]
    return tuple(outs)
```

```python
import functools

import jax
import jax.numpy as jnp
from jax import lax
from jax.experimental import pallas as pl
from jax.experimental.pallas import tpu as pltpu

F32 = jnp.float32
MXU_DTYPE = jnp.bfloat16
ACT_DTYPE = jnp.bfloat16
WIRE_DTYPE = jnp.bfloat16

D = 1024
HD = 64
AW = 512
CW = 512
DFF = 2816
HALF = DFF // 2
INC = 3 * AW + 3 * CW
NCHIP = 4
NDEV = 8
QBLK = 128
ATTN_CHUNK_ROWS = 2048
DILATIONS = (1, 4, 16)
EPS = 1e-6
NEG = -1e30
LANES = 128
SUBLANES = 8
VMEM_LIMIT = 56 * 1024 * 1024

ADAM_LR = 0.001
ADAM_B1 = 0.9
ADAM_B2 = 0.999
ADAM_EPS = 1e-08
ADAM_WD = 0.01
ADAM_STEP = 10

NT_DIMS = (((1,), (1,)), ((), ()))
TN_DIMS = (((0,), (0,)), ((), ()))


def _params(sem, vmem=VMEM_LIMIT):
    return pltpu.CompilerParams(dimension_semantics=sem, vmem_limit_bytes=vmem)


def _row_tile(n, want):
    t = min(n, want)
    assert n % t == 0
    return t


def _ada(xt, vec_ref):
    ng, sc, sh, gt = vec_ref[0:1, :], vec_ref[1:2, :], vec_ref[2:3, :], vec_ref[3:4, :]
    r = lax.rsqrt(jnp.mean(xt * xt, axis=-1, keepdims=True) + EPS)
    return xt * r, r, ng * (1.0 + sc), ng, sc, sh, gt


def _ada_bwd(dh, xhat, r, gain, ng, sc):
    dshift = jnp.sum(dh, axis=0, keepdims=True)
    dhx = dh * xhat
    dscale = jnp.sum(dhx, axis=0, keepdims=True) * ng
    dng = jnp.sum(dhx, axis=0, keepdims=True) * (1.0 + sc)
    dxhat = dh * gain
    dx = r * (dxhat - xhat * jnp.mean(dxhat * xhat, axis=-1, keepdims=True))
    return dx, dshift, dscale, dng


def _acc_rows(sums_ref, first, rows):
    @pl.when(first)
    def _():
        sums_ref[...] = jnp.zeros_like(sums_ref)
    for k, row in enumerate(rows):
        sums_ref[k:k + 1, :] += row


MESH = pl.DeviceIdType.MESH
ANY = pl.BlockSpec(memory_space=pl.ANY)


def _here():
    return lax.axis_index("x"), lax.axis_index("y"), lax.axis_index("c")


def _ici_copies(src_refs, dst_refs, send_sems, recv_sems, local_sems, scatter):
    x, y, c = _here()
    my_chip = 2 * x + y
    peers = [(1 - x, y), (x, 1 - y), (1 - x, 1 - y)]
    local, out, inc = [], [], []
    for a, (src, dst) in enumerate(zip(src_refs, dst_refs)):
        local.append(pltpu.make_async_copy(src.at[my_chip] if scatter else src, dst.at[my_chip], local_sems.at[a]))
        for j, (px, py) in enumerate(peers):
            sems = dict(send_sem=send_sems.at[3 * a + j], recv_sem=recv_sems.at[3 * a + j],
                        device_id=(px, py, c), device_id_type=MESH)
            out.append(pltpu.make_async_remote_copy(
                src_ref=src.at[2 * px + py] if scatter else src, dst_ref=dst.at[my_chip], **sems))
            inc.append(pltpu.make_async_remote_copy(
                src_ref=src.at[my_chip] if scatter else src, dst_ref=dst.at[2 * px + py], **sems))
    return local, out, inc


def _swap_copies(src_refs, dst_refs, send_sems, recv_sems, halves):
    x, y, c = _here()
    cps = []
    for k, (src, dst) in enumerate(zip(src_refs, dst_refs)):
        if halves:
            r2 = src.shape[1] // 2
            src = src.at[:, pl.ds((1 - c) * r2, r2), :]
        cps.append(pltpu.make_async_remote_copy(
            src_ref=src, dst_ref=dst, send_sem=send_sems.at[k], recv_sem=recv_sems.at[k],
            device_id=(x, y, 1 - c), device_id_type=MESH))
    return cps


class Carry:
    def __init__(self, kind, srcs):
        self.kind, self.srcs, n = kind, list(srcs), len(srcs)
        if kind == "gather":
            shapes = [(NCHIP,) + s.shape for s in srcs]
        elif kind == "swap_halves":
            shapes = [(s.shape[0], s.shape[1] // 2, s.shape[2]) for s in srcs]
        else:
            shapes = [s.shape for s in srcs]
        self.out_shape = [jax.ShapeDtypeStruct(sh, s.dtype) for sh, s in zip(shapes, srcs)]
        dma = pltpu.SemaphoreType.DMA
        self.sems = [dma((3 * n,)), dma((3 * n,)), dma((n,))] if kind in ("gather", "scatter") else [dma((n,)), dma((n,))]

    def start(self, srcs, dsts, sems):
        if self.kind in ("gather", "scatter"):
            local, out, _ = _ici_copies(srcs, dsts, *sems, self.kind == "scatter")
            for cp in local + out:
                cp.start()
        else:
            for cp in _swap_copies(srcs, dsts, *sems, self.kind == "swap_halves"):
                cp.start()

    def wait(self, srcs, dsts, sems):
        if self.kind in ("gather", "scatter"):
            local, out, inc = _ici_copies(srcs, dsts, *sems, self.kind == "scatter")
            for cp in inc:
                cp.wait_recv()
            for cp in out:
                cp.wait_send()
            for cp in local:
                cp.wait()
        else:
            cps = _swap_copies(srcs, dsts, *sems, self.kind == "swap_halves")
            for cp in cps:
                cp.wait_recv()
            for cp in cps:
                cp.wait_send()


def run_carry(carry, name):
    n = len(carry.srcs)

    def body(*refs):
        srcs, dsts, sems = refs[:n], refs[n:2 * n], refs[2 * n:]
        carry.start(srcs, dsts, sems)
        carry.wait(srcs, dsts, sems)

    return pl.pallas_call(body, name=name, out_shape=carry.out_shape, in_specs=[ANY] * n, out_specs=[ANY] * n,
                          scratch_shapes=carry.sems)(*carry.srcs)


def _pcall(body, name, grid, in_specs, out_specs, out_shape, sem, args, carry=None, scratch=()):
    if carry is None:
        outs = pl.pallas_call(body, name=name, grid=grid, in_specs=in_specs, out_specs=out_specs,
                              out_shape=out_shape, scratch_shapes=list(scratch), compiler_params=_params(sem))(*args)
        return outs, []
    n_in, n_out, nc, ns = len(in_specs), len(out_specs), len(carry.srcs), len(scratch)

    def wrapped(*refs):
        ins, csrc = refs[:n_in], refs[n_in:n_in + nc]
        outs, cdst = refs[n_in + nc:n_in + nc + n_out], refs[n_in + nc + n_out:n_in + 2 * nc + n_out]
        own = refs[n_in + 2 * nc + n_out:n_in + 2 * nc + n_out + ns]
        sems = refs[n_in + 2 * nc + n_out + ns:]
        ids = [pl.program_id(a) for a in range(len(grid))]
        first = functools.reduce(jnp.logical_and, [i == 0 for i in ids])
        last = functools.reduce(jnp.logical_and, [i == g - 1 for i, g in zip(ids, grid)])

        @pl.when(first)
        def _():
            carry.start(csrc, cdst, sems)

        body(*ins, *outs, *own)

        @pl.when(last)
        def _():
            carry.wait(csrc, cdst, sems)

    res = pl.pallas_call(
        wrapped, name=name, grid=grid,
        in_specs=list(in_specs) + [ANY] * nc, out_specs=list(out_specs) + [ANY] * nc,
        out_shape=list(out_shape) + carry.out_shape,
        scratch_shapes=list(scratch) + carry.sems, compiler_params=_params(sem),
    )(*args, *carry.srcs)
    return res[:n_out], res[n_out:]


def ffn_fwd(x, vec, w1p, w2, gs, name, carry=None):
    S = x.shape[0]
    tm = _row_tile(S, 512)

    def body(x_ref, vec_ref, w1_ref, w2_ref, xn_ref, a_ref, f_ref):
        xt = x_ref[...]
        xhat, _, gain, _, _, sh, gt = _ada(xt, vec_ref)
        h = (xhat * gain + sh).astype(MXU_DTYPE)
        f = jnp.zeros((tm, D), F32)
        for hf in range(2):
            g = jnp.dot(h, w1_ref[hf], preferred_element_type=F32)
            up = jnp.dot(h, w1_ref[2 + hf], preferred_element_type=F32)
            a_ref[:, hf * HALF:(hf + 1) * HALF] = g.astype(a_ref.dtype)
            a_ref[:, DFF + hf * HALF:DFF + (hf + 1) * HALF] = up.astype(a_ref.dtype)
            act = (g * jax.nn.sigmoid(g) * up).astype(MXU_DTYPE)
            f = f + jnp.dot(act, w2_ref[hf * HALF:(hf + 1) * HALF, :], preferred_element_type=F32)
        xn_ref[...] = xt + (gs * gt) * f
        f_ref[...] = f.astype(f_ref.dtype)

    return _pcall(
        body, name, (S // tm,),
        [pl.BlockSpec((tm, D), lambda i: (i, 0)),
         pl.BlockSpec((SUBLANES, D), lambda i: (0, 0)),
         pl.BlockSpec((NCHIP, D, HALF), lambda i: (0, 0, 0), pipeline_mode=pl.Buffered(1)),
         pl.BlockSpec((DFF, D), lambda i: (0, 0), pipeline_mode=pl.Buffered(1))],
        [pl.BlockSpec((tm, D), lambda i: (i, 0)),
         pl.BlockSpec((tm, 2 * DFF), lambda i: (i, 0)),
         pl.BlockSpec((tm, D), lambda i: (i, 0))],
        [jax.ShapeDtypeStruct((S, D), F32),
         jax.ShapeDtypeStruct((S, 2 * DFF), ACT_DTYPE),
         jax.ShapeDtypeStruct((S, D), ACT_DTYPE)],
        ("arbitrary",), (x, vec, w1p, w2), carry)


def ffn_bwd(dxo, x, a, f, vec, w1p, w2, gs, name, carry=None):
    S = x.shape[0]
    tm = _row_tile(S, 256)

    def body(dxo_ref, x_ref, a_ref, f_ref, vec_ref, w1_ref, w2_ref,
             dxi_ref, hb_ref, dfb_ref, act_ref, da_ref, sums_ref):
        xt = x_ref[...]
        dxo = dxo_ref[...]
        xhat, r, gain, ng, sc, sh, gt = _ada(xt, vec_ref)
        hb_ref[...] = (xhat * gain + sh).astype(hb_ref.dtype)
        dgate = gs * jnp.sum(dxo * f_ref[...].astype(F32), axis=0, keepdims=True)
        df = ((gs * gt) * dxo).astype(MXU_DTYPE)
        dfb_ref[...] = df
        dh = jnp.zeros((tm, D), F32)
        for hf in range(2):
            lo, hi = hf * HALF, (hf + 1) * HALF
            dact = lax.dot_general(df, w2_ref[lo:hi, :], NT_DIMS, preferred_element_type=F32)
            g = a_ref[:, lo:hi].astype(F32)
            up = a_ref[:, DFF + lo:DFF + hi].astype(F32)
            sg = jax.nn.sigmoid(g)
            si = g * sg
            act_ref[:, lo:hi] = (si * up).astype(act_ref.dtype)
            dg = (dact * up * (sg * (1.0 + g * (1.0 - sg)))).astype(MXU_DTYPE)
            dup = (dact * si).astype(MXU_DTYPE)
            da_ref[:, lo:hi] = dg
            da_ref[:, DFF + lo:DFF + hi] = dup
            dh = dh + lax.dot_general(dg, w1_ref[hf], NT_DIMS, preferred_element_type=F32)
            dh = dh + lax.dot_general(dup, w1_ref[2 + hf], NT_DIMS, preferred_element_type=F32)
        dx, dshift, dscale, dng = _ada_bwd(dh, xhat, r, gain, ng, sc)
        dxi_ref[...] = dxo + dx
        _acc_rows(sums_ref, pl.program_id(0) == 0, (dshift, dscale, dng, dgate))

    return _pcall(
        body, name, (S // tm,),
        [pl.BlockSpec((tm, D), lambda i: (i, 0)),
         pl.BlockSpec((tm, D), lambda i: (i, 0)),
         pl.BlockSpec((tm, 2 * DFF), lambda i: (i, 0)),
         pl.BlockSpec((tm, D), lambda i: (i, 0)),
         pl.BlockSpec((SUBLANES, D), lambda i: (0, 0)),
         pl.BlockSpec((NCHIP, D, HALF), lambda i: (0, 0, 0), pipeline_mode=pl.Buffered(1)),
         pl.BlockSpec((DFF, D), lambda i: (0, 0), pipeline_mode=pl.Buffered(1))],
        [pl.BlockSpec((tm, D), lambda i: (i, 0)),
         pl.BlockSpec((tm, D), lambda i: (i, 0)),
         pl.BlockSpec((tm, D), lambda i: (i, 0)),
         pl.BlockSpec((tm, DFF), lambda i: (i, 0)),
         pl.BlockSpec((tm, 2 * DFF), lambda i: (i, 0)),
         pl.BlockSpec((SUBLANES, D), lambda i: (0, 0))],
        [jax.ShapeDtypeStruct((S, D), F32),
         jax.ShapeDtypeStruct((S, D), MXU_DTYPE),
         jax.ShapeDtypeStruct((S, D), MXU_DTYPE),
         jax.ShapeDtypeStruct((S, DFF), MXU_DTYPE),
         jax.ShapeDtypeStruct((S, 2 * DFF), MXU_DTYPE),
         jax.ShapeDtypeStruct((SUBLANES, D), F32)],
        ("arbitrary",), (dxo, x, a, f, vec, w1p, w2), carry)


def wgrad(a, b, kt, nt, name, carry=None):
    T, K = a.shape
    N = b.shape[1]
    pk, pn = K // kt, N // nt
    assert pk == 1 or pn == 1
    tt = _row_tile(T, 1024)
    steps = T // tt

    def body(a_ref, b_ref, o_ref):
        @pl.when(pl.program_id(1) == 0)
        def _():
            o_ref[...] = jnp.zeros_like(o_ref)
        o_ref[...] += lax.dot_general(a_ref[...], b_ref[...], TN_DIMS, preferred_element_type=F32)

    a_map = (lambda p, t: (t, p)) if pk > 1 else (lambda p, t: (t, 0))
    b_map = (lambda p, t: (t, p)) if pn > 1 else (lambda p, t: (t, 0))
    (out,), got = _pcall(
        body, name, (pk * pn, steps),
        [pl.BlockSpec((tt, kt), a_map), pl.BlockSpec((tt, nt), b_map)],
        [pl.BlockSpec((None, kt, nt), lambda p, t: (p, 0, 0))],
        [jax.ShapeDtypeStruct((pk * pn, kt, nt), F32)], ("arbitrary", "arbitrary"), (a, b), carry)
    return out, got


def _head_masks(rows):
    lane = lax.broadcasted_iota(jnp.int32, (rows, LANES), 1)
    return lane < HD


def _pair_stat(x, m_a):
    s_a = jnp.sum(jnp.where(m_a, x, 0.0), axis=1, keepdims=True)
    s_b = jnp.sum(jnp.where(m_a, 0.0, x), axis=1, keepdims=True)
    return s_a, s_b


def mixer_in(x, vec, winp, gvec, name):
    S = x.shape[0]
    tm = _row_tile(S, 512)
    pc = INC // NCHIP

    def body(x_ref, vec_ref, w_ref, g_ref, proj_ref, hb_ref, qn_ref, kn_ref, v_ref):
        xt = x_ref[...]
        xhat, _, gain, _, _, sh, _ = _ada(xt, vec_ref)
        h = (xhat * gain + sh).astype(MXU_DTYPE)
        hb_ref[...] = h
        for j in range(NCHIP):
            proj_ref[:, j * pc:(j + 1) * pc] = jnp.dot(h, w_ref[j], preferred_element_type=F32)
        m_a = _head_masks(tm)
        for which, dst in ((0, qn_ref), (1, kn_ref)):
            for p in range(AW // LANES):
                lo = which * AW + p * LANES
                xp = proj_ref[:, lo:lo + LANES]
                s_a, s_b = _pair_stat(xp * xp, m_a)
                rr = jnp.where(m_a, lax.rsqrt(s_a * (1.0 / HD) + EPS), lax.rsqrt(s_b * (1.0 / HD) + EPS))
                gp = g_ref[which:which + 1, p * LANES:(p + 1) * LANES]
                dst[:, p * LANES:(p + 1) * LANES] = (xp * rr * gp).astype(dst.dtype)
        v_ref[...] = proj_ref[:, 2 * AW:3 * AW].astype(v_ref.dtype)

    return pl.pallas_call(
        body, name=name, grid=(S // tm,),
        in_specs=[pl.BlockSpec((tm, D), lambda i: (i, 0)),
                  pl.BlockSpec((SUBLANES, D), lambda i: (0, 0)),
                  pl.BlockSpec((NCHIP, D, pc), lambda i: (0, 0, 0), pipeline_mode=pl.Buffered(1)),
                  pl.BlockSpec((SUBLANES, AW), lambda i: (0, 0))],
        out_specs=[pl.BlockSpec((tm, INC), lambda i: (i, 0)),
                   pl.BlockSpec((tm, D), lambda i: (i, 0)),
                   pl.BlockSpec((tm, AW), lambda i: (i, 0)),
                   pl.BlockSpec((tm, AW), lambda i: (i, 0)),
                   pl.BlockSpec((tm, AW), lambda i: (i, 0))],
        out_shape=[jax.ShapeDtypeStruct((S, INC), F32),
                   jax.ShapeDtypeStruct((S, D), MXU_DTYPE),
                   jax.ShapeDtypeStruct((S, AW), F32),
                   jax.ShapeDtypeStruct((S, AW), F32),
                   jax.ShapeDtypeStruct((S, AW), F32)],
        compiler_params=_params(("arbitrary",)),
    )(x, vec, winp, gvec)


def _band_masks(ncol):
    row = lax.broadcasted_iota(jnp.int32, (2 * QBLK, ncol), 0) & (QBLK - 1)
    col = lax.broadcasted_iota(jnp.int32, (2 * QBLK, ncol), 1)
    return row, col


def _stack_heads(t, m_a):
    zero = jnp.zeros_like(t)
    return jnp.concatenate([jnp.where(m_a, t, zero), jnp.where(m_a, zero, t)], axis=0)


def _attn_qb(d):
    return max(1, min(4, ATTN_CHUNK_ROWS // (QBLK * d)))


def _tile_rows(d, b, r):
    if d == 1:
        return pl.ds(b * QBLK, QBLK)
    return pl.ds(b * QBLK * d + r, QBLK, stride=d)


def _per_residue(d, fn):
    if d == 1:
        fn(0)
    else:
        def step(r, carry):
            fn(r)
            return carry
        lax.fori_loop(0, d, step, 0)


def attn_fwd(qn, kn, v, d, name, carry=None):
    S = qn.shape[0]
    qb = _attn_qb(d)
    halo = QBLK * d
    chunk = qb * halo

    def body(q_ref, kc_ref, kp_ref, vc_ref, vp_ref, o_ref, la_ref, lb_ref):
        i = pl.program_id(1)
        m_a = _head_masks(QBLK)
        row, col = _band_masks(2 * QBLK)
        dist = row + QBLK - col
        band = (dist >= 0) & (dist <= QBLK)
        first = band & ((i > 0) | (col >= QBLK))

        def residue(r):
            kt = [kp_ref[_tile_rows(d, 0, r), :].astype(MXU_DTYPE)]
            vt = [vp_ref[_tile_rows(d, 0, r), :].astype(MXU_DTYPE)]
            for b in range(qb):
                kt.append(kc_ref[_tile_rows(d, b, r), :].astype(MXU_DTYPE))
                vt.append(vc_ref[_tile_rows(d, b, r), :].astype(MXU_DTYPE))
            for b in range(qb):
                rows = _tile_rows(d, b, r)
                q = (q_ref[rows, :] * (HD ** -0.5)).astype(MXU_DTYPE)
                kcat = jnp.concatenate([kt[b], kt[b + 1]], axis=0)
                vcat = jnp.concatenate([vt[b], vt[b + 1]], axis=0)
                mask = first if b == 0 else band
                s = lax.dot_general(_stack_heads(q, m_a), kcat, NT_DIMS, preferred_element_type=F32)
                s = jnp.where(mask, s, NEG)
                m = jnp.max(s, axis=1, keepdims=True)
                p = jnp.exp(s - m)
                l = jnp.sum(p, axis=1, keepdims=True)
                o = jnp.dot(p.astype(MXU_DTYPE), vcat, preferred_element_type=F32) / l
                lse = jnp.broadcast_to(m + jnp.log(l), (2 * QBLK, LANES))
                o_ref[rows, :] = jnp.where(m_a, o[:QBLK], o[QBLK:])
                la_ref[rows, :] = lse[:QBLK]
                lb_ref[rows, :] = lse[QBLK:]

        _per_residue(d, residue)

    cur = pl.BlockSpec((chunk, LANES), lambda hp, i: (i, hp))
    prev = pl.BlockSpec((halo, LANES), lambda hp, i: (jnp.maximum(i * qb - 1, 0), hp))
    return _pcall(body, name, (AW // LANES, S // chunk), [cur, cur, prev, cur, prev], [cur, cur, cur],
                  [jax.ShapeDtypeStruct((S, AW), F32)] * 3, ("arbitrary", "arbitrary"), (qn, kn, kn, v, v), carry)


def attn_bwd(qn, kn, v, dycat, lse_a, lse_b, dl_a, dl_b, d, name, carry=None):
    S = qn.shape[0]
    qb = _attn_qb(d)
    halo = QBLK * d
    chunk = qb * halo
    nchunk = S // chunk

    def body(q_ref, kc_ref, kp_ref, vc_ref, vp_ref, do_ref, la_ref, lb_ref, da_ref, db_ref,
             dq_ref, dk_ref, dv_ref, ck_ref, cv_ref):
        j = pl.program_id(1)
        i = nchunk - 1 - j
        m_a = _head_masks(QBLK)
        row, col = _band_masks(2 * QBLK)
        dist = row + QBLK - col
        band = (dist >= 0) & (dist <= QBLK)
        first = band & ((i > 0) | (col >= QBLK))

        def residue(r):
            def tiles(ref, cast):
                out = [ref[_tile_rows(d, b, r), :] for b in range(qb)]
                return [t.astype(MXU_DTYPE) for t in out] if cast else out

            def ktiles(cur_ref, prev_ref):
                return [prev_ref[_tile_rows(d, 0, r), :].astype(MXU_DTYPE)] + tiles(cur_ref, True)

            qt = [(t * (HD ** -0.5)).astype(MXU_DTYPE) for t in tiles(q_ref, False)]
            dot_ = tiles(do_ref, True)
            lse_t = list(zip(tiles(la_ref, False), tiles(lb_ref, False)))
            dl_t = list(zip(tiles(da_ref, False), tiles(db_ref, False)))
            kt = ktiles(kc_ref, kp_ref)
            vt = ktiles(vc_ref, vp_ref)
            dk_acc = [jnp.zeros((QBLK, LANES), F32) for _ in range(qb)]
            dv_acc = [jnp.zeros((QBLK, LANES), F32) for _ in range(qb)]
            crow = pl.ds(0, QBLK) if d == 1 else pl.ds(pl.multiple_of(r * QBLK, QBLK), QBLK)
            dk_acc[qb - 1] = jnp.where(j > 0, ck_ref[crow, :], 0.0)
            dv_acc[qb - 1] = jnp.where(j > 0, cv_ref[crow, :], 0.0)
            for x in range(qb):
                kcat = jnp.concatenate([kt[x], kt[x + 1]], axis=0)
                vcat = jnp.concatenate([vt[x], vt[x + 1]], axis=0)
                q2 = _stack_heads(qt[x], m_a)
                do2 = _stack_heads(dot_[x], m_a)
                lse2 = jnp.concatenate(lse_t[x], axis=0)
                dl2 = jnp.concatenate(dl_t[x], axis=0)
                lse2 = jnp.concatenate([lse2, lse2], axis=1)
                dl2 = jnp.concatenate([dl2, dl2], axis=1)
                s = lax.dot_general(q2, kcat, NT_DIMS, preferred_element_type=F32)
                p = jnp.exp(jnp.where(first if x == 0 else band, s, NEG) - lse2)
                dp = lax.dot_general(do2, vcat, NT_DIMS, preferred_element_type=F32)
                ds = p * (dp - dl2)
                dq = jnp.dot(ds.astype(MXU_DTYPE), kcat, preferred_element_type=F32)
                dq_ref[_tile_rows(d, x, r), :] = jnp.where(m_a, dq[:QBLK], dq[QBLK:]) * (HD ** -0.5)
                dk = jnp.dot(ds.T.astype(MXU_DTYPE), q2, preferred_element_type=F32)
                dv = jnp.dot(p.T.astype(MXU_DTYPE), do2, preferred_element_type=F32)
                if x == 0:
                    ck_ref[crow, :] = dk[:QBLK]
                    cv_ref[crow, :] = dv[:QBLK]
                else:
                    dk_acc[x - 1] = dk_acc[x - 1] + dk[:QBLK]
                    dv_acc[x - 1] = dv_acc[x - 1] + dv[:QBLK]
                dk_acc[x] = dk_acc[x] + dk[QBLK:]
                dv_acc[x] = dv_acc[x] + dv[QBLK:]
            for kb in range(qb):
                dk_ref[_tile_rows(d, kb, r), :] = dk_acc[kb]
                dv_ref[_tile_rows(d, kb, r), :] = dv_acc[kb]

        _per_residue(d, residue)

    cur = pl.BlockSpec((chunk, LANES), lambda hp, j: (nchunk - 1 - j, hp))
    prev = pl.BlockSpec((halo, LANES), lambda hp, j: (jnp.maximum((nchunk - 1 - j) * qb - 1, 0), hp))
    carried = pltpu.VMEM((d * QBLK, LANES), F32)
    return _pcall(
        body, name, (AW // LANES, nchunk),
        [cur, cur, prev, cur, prev, cur, cur, cur, cur, cur], [cur, cur, cur],
        [jax.ShapeDtypeStruct((S, AW), F32)] * 3, ("arbitrary", "arbitrary"),
        (qn, kn, kn, v, v, dycat, lse_a, lse_b, dl_a, dl_b), carry, scratch=[carried, carried])


def _shift_down(x, halo_prev, k, row):
    tm = x.shape[0]
    tail = jnp.concatenate([pltpu.roll(halo_prev, k, 0), jnp.zeros((tm - SUBLANES, x.shape[1]), x.dtype)], axis=0)
    return jnp.where(row < k, tail, pltpu.roll(x, k, 0))


def _shift_up(x, halo_next, k, row):
    tm = x.shape[0]
    head = jnp.concatenate([jnp.zeros((tm - SUBLANES, x.shape[1]), x.dtype), pltpu.roll(halo_next, SUBLANES - k, 0)], axis=0)
    return jnp.where(row >= tm - k, head, pltpu.roll(x, tm - k, 0))


def _conv_fwd(cu, halo_cu, cw_ref, row):
    u1 = _shift_down(cu, halo_cu, 1, row)
    u2 = _shift_down(cu, halo_cu, 2, row)
    cv = cw_ref[0:1, :] * u2 + cw_ref[1:2, :] * u1 + cw_ref[2:3, :] * cu + cw_ref[3:4, :]
    return cv, u1, u2


def combine_conv(os_, lses_a, lses_b, proj, cw, name, carry=None):
    S = proj.shape[0]
    tm = _row_tile(S, 512)
    hb = tm // SUBLANES

    def body(o1, o2, o3, a1, a2, a3, b1, b2, b3, pc_ref, ph_ref, cw_ref, ycat_ref, la_ref, lb_ref):
        i = pl.program_id(0)
        m_a = _head_masks(tm)
        for p in range(AW // LANES):
            cs = slice(p * LANES, (p + 1) * LANES)
            tot = []
            for srcs, dst in (((a1, a2, a3), la_ref), ((b1, b2, b3), lb_ref)):
                ls = [l[:, cs] for l in srcs]
                mx = jnp.maximum(jnp.maximum(ls[0], ls[1]), ls[2])
                t = mx + jnp.log(jnp.exp(ls[0] - mx) + jnp.exp(ls[1] - mx) + jnp.exp(ls[2] - mx))
                dst[:, cs] = t
                tot.append((ls, t))
            acc = jnp.zeros((tm, LANES), F32)
            for r, o in enumerate((o1, o2, o3)):
                w = jnp.where(m_a, jnp.exp(tot[0][0][r] - tot[0][1]), jnp.exp(tot[1][0][r] - tot[1][1]))
                acc = acc + w * o[:, cs]
            ycat_ref[:, cs] = acc.astype(ycat_ref.dtype)
        row = lax.broadcasted_iota(jnp.int32, (tm, CW), 0)
        gb, gc, u = pc_ref[:, 0:CW], pc_ref[:, CW:2 * CW], pc_ref[:, 2 * CW:3 * CW]
        halo_cu = jnp.where(i > 0, ph_ref[:, CW:2 * CW] * ph_ref[:, 2 * CW:3 * CW], 0.0)
        cv, _, _ = _conv_fwd(gc * u, halo_cu, cw_ref, row)
        ycat_ref[:, AW:AW + CW] = (gb * cv).astype(ycat_ref.dtype)

    ot = pl.BlockSpec((tm, AW), lambda i: (i, 0))
    return _pcall(
        body, name, (S // tm,),
        [ot] * 9 + [pl.BlockSpec((tm, 3 * CW), lambda i: (i, 1)),
                    pl.BlockSpec((SUBLANES, 3 * CW), lambda i: (jnp.maximum(i * hb - 1, 0), 1)),
                    pl.BlockSpec((SUBLANES, CW), lambda i: (0, 0))],
        [pl.BlockSpec((tm, D), lambda i: (i, 0)), ot, ot],
        [jax.ShapeDtypeStruct((S, D), ACT_DTYPE), jax.ShapeDtypeStruct((S, AW), F32),
         jax.ShapeDtypeStruct((S, AW), F32)],
        ("arbitrary",), (*os_, *lses_a, *lses_b, proj, proj, cw), carry)


def out_proj(ycat, x, vec, wout, name):
    S = x.shape[0]
    tm = _row_tile(S, 512)

    def body(yc_ref, x_ref, vec_ref, w_ref, xn_ref, y_ref):
        y = jnp.dot(yc_ref[...].astype(MXU_DTYPE), w_ref[...], preferred_element_type=F32)
        xn_ref[...] = x_ref[...] + vec_ref[3:4, :] * y
        y_ref[...] = y.astype(y_ref.dtype)

    t = pl.BlockSpec((tm, D), lambda i: (i, 0))
    return pl.pallas_call(
        body, name=name, grid=(S // tm,),
        in_specs=[t, t, pl.BlockSpec((SUBLANES, D), lambda i: (0, 0)),
                  pl.BlockSpec((D, D), lambda i: (0, 0))],
        out_specs=[t, t],
        out_shape=[jax.ShapeDtypeStruct((S, D), F32), jax.ShapeDtypeStruct((S, D), ACT_DTYPE)],
        compiler_params=_params(("arbitrary",)),
    )(ycat, x, vec, wout)


def out_proj_bwd(dxo, y, ycat, vec, wout, name, carry=None):
    S = dxo.shape[0]
    tm = _row_tile(S, 512)

    def body(dxo_ref, y_ref, yc_ref, vec_ref, w_ref, dyb_ref, dyc_ref, da_ref, db_ref, sums_ref):
        dxo = dxo_ref[...]
        dgate = jnp.sum(dxo * y_ref[...].astype(F32), axis=0, keepdims=True)
        dy = (vec_ref[3:4, :] * dxo).astype(MXU_DTYPE)
        dyb_ref[...] = dy
        dyc_ref[...] = lax.dot_general(dy, w_ref[...], NT_DIMS, preferred_element_type=F32)
        m_a = _head_masks(tm)
        for p in range(AW // LANES):
            cs = slice(p * LANES, (p + 1) * LANES)
            s_a, s_b = _pair_stat(dyc_ref[:, cs] * yc_ref[:, cs].astype(F32), m_a)
            da_ref[:, cs] = jnp.broadcast_to(s_a, (tm, LANES))
            db_ref[:, cs] = jnp.broadcast_to(s_b, (tm, LANES))
        _acc_rows(sums_ref, pl.program_id(0) == 0, (dgate,))

    t = pl.BlockSpec((tm, D), lambda i: (i, 0))
    at = pl.BlockSpec((tm, AW), lambda i: (i, 0))
    return _pcall(
        body, name, (S // tm,),
        [t, t, t, pl.BlockSpec((SUBLANES, D), lambda i: (0, 0)), pl.BlockSpec((D, D), lambda i: (0, 0))],
        [t, t, at, at, pl.BlockSpec((SUBLANES, D), lambda i: (0, 0))],
        [jax.ShapeDtypeStruct((S, D), MXU_DTYPE), jax.ShapeDtypeStruct((S, D), F32),
         jax.ShapeDtypeStruct((S, AW), F32), jax.ShapeDtypeStruct((S, AW), F32),
         jax.ShapeDtypeStruct((SUBLANES, D), F32)],
        ("arbitrary",), (dxo, y, ycat, vec, wout), carry)


def mixer_mid_bwd(dqs, dks, dvs, proj, dycat, gvec, cw, name, carry=None):
    S = proj.shape[0]
    tm = _row_tile(S, 256)
    hb = tm // SUBLANES
    nsl = S // SUBLANES
    ntile = S // tm

    def body(dq1, dq2, dq3, dk1, dk2, dk3, dv1, dv2, dv3, pr_ref, pp_ref, pn_ref, dyc_ref, dyn_ref,
             g_ref, cw_ref, dp_ref, sums_ref):
        i = pl.program_id(0)
        m_a = _head_masks(tm)
        gsum = []
        for which, parts in ((0, (dq1, dq2, dq3)), (1, (dk1, dk2, dk3))):
            acc_g = []
            for p in range(AW // LANES):
                lo = which * AW + p * LANES
                cs = slice(p * LANES, (p + 1) * LANES)
                xp = pr_ref[:, lo:lo + LANES]
                s_a, s_b = _pair_stat(xp * xp, m_a)
                rr = jnp.where(m_a, lax.rsqrt(s_a * (1.0 / HD) + EPS), lax.rsqrt(s_b * (1.0 / HD) + EPS))
                xh = xp * rr
                dn = parts[0][:, cs] + parts[1][:, cs] + parts[2][:, cs]
                acc_g.append(jnp.sum(dn * xh, axis=0, keepdims=True))
                t = dn * g_ref[which:which + 1, cs]
                t_a, t_b = _pair_stat(t * xh, m_a)
                mean = jnp.where(m_a, t_a, t_b) * (1.0 / HD)
                dp_ref[:, lo:lo + LANES] = (rr * (t - xh * mean)).astype(dp_ref.dtype)
            gsum.append(jnp.concatenate(acc_g, axis=1))
        dp_ref[:, 2 * AW:3 * AW] = (dv1[...] + dv2[...] + dv3[...]).astype(dp_ref.dtype)
        row = lax.broadcasted_iota(jnp.int32, (tm, CW), 0)
        base = 3 * AW
        gb, gc, u = pr_ref[:, base:base + CW], pr_ref[:, base + CW:base + 2 * CW], pr_ref[:, base + 2 * CW:base + 3 * CW]
        cu = gc * u
        halo_cu = jnp.where(i > 0, pp_ref[:, CW:2 * CW] * pp_ref[:, 2 * CW:3 * CW], 0.0)
        cv, u1, u2 = _conv_fwd(cu, halo_cu, cw_ref, row)
        dyc = dyc_ref[...]
        dp_ref[:, base:base + CW] = (dyc * cv).astype(dp_ref.dtype)
        dcv = dyc * gb
        halo_dcv = jnp.where(i < ntile - 1, dyn_ref[...] * pn_ref[:, 0:CW], 0.0)
        d1 = _shift_up(dcv, halo_dcv, 1, row)
        d2 = _shift_up(dcv, halo_dcv, 2, row)
        dcu = cw_ref[2:3, :] * dcv + cw_ref[1:2, :] * d1 + cw_ref[0:1, :] * d2
        dp_ref[:, base + CW:base + 2 * CW] = (dcu * u).astype(dp_ref.dtype)
        dp_ref[:, base + 2 * CW:base + 3 * CW] = (dcu * gc).astype(dp_ref.dtype)
        rows = (gsum[0], gsum[1],
                jnp.sum(dcv * u2, axis=0, keepdims=True), jnp.sum(dcv * u1, axis=0, keepdims=True),
                jnp.sum(dcv * cu, axis=0, keepdims=True), jnp.sum(dcv, axis=0, keepdims=True))
        _acc_rows(sums_ref, i == 0, rows)

    at = pl.BlockSpec((tm, AW), lambda i: (i, 0))
    return _pcall(
        body, name, (ntile,),
        [at] * 9 + [
            pl.BlockSpec((tm, INC), lambda i: (i, 0)),
            pl.BlockSpec((SUBLANES, 3 * CW), lambda i: (jnp.maximum(i * hb - 1, 0), 1)),
            pl.BlockSpec((SUBLANES, 3 * CW), lambda i: (jnp.minimum((i + 1) * hb, nsl - 1), 1)),
            pl.BlockSpec((tm, CW), lambda i: (i, 1)),
            pl.BlockSpec((SUBLANES, CW), lambda i: (jnp.minimum((i + 1) * hb, nsl - 1), 1)),
            pl.BlockSpec((SUBLANES, AW), lambda i: (0, 0)),
            pl.BlockSpec((SUBLANES, CW), lambda i: (0, 0))],
        [pl.BlockSpec((tm, INC), lambda i: (i, 0)), pl.BlockSpec((SUBLANES, AW), lambda i: (0, 0))],
        [jax.ShapeDtypeStruct((S, INC), MXU_DTYPE), jax.ShapeDtypeStruct((SUBLANES, AW), F32)],
        ("arbitrary",), (*dqs, *dks, *dvs, proj, proj, proj, dycat, dycat, gvec, cw), carry)


def mixer_in_bwd(dxo, x, dproj, vec, winp, name, carry=None):
    S = x.shape[0]
    tm = _row_tile(S, 512)
    pc = INC // NCHIP

    def body(dxo_ref, x_ref, dp_ref, vec_ref, w_ref, dxi_ref, sums_ref):
        xhat, r, gain, ng, sc, _, _ = _ada(x_ref[...], vec_ref)
        dh = jnp.zeros((tm, D), F32)
        for j in range(NCHIP):
            dh = dh + lax.dot_general(dp_ref[:, j * pc:(j + 1) * pc], w_ref[j], NT_DIMS, preferred_element_type=F32)
        dx, dshift, dscale, dng = _ada_bwd(dh, xhat, r, gain, ng, sc)
        dxi_ref[...] = dxo_ref[...] + dx
        _acc_rows(sums_ref, pl.program_id(0) == 0, (dshift, dscale, dng))

    t = pl.BlockSpec((tm, D), lambda i: (i, 0))
    return _pcall(
        body, name, (S // tm,),
        [t, t, pl.BlockSpec((tm, INC), lambda i: (i, 0)),
         pl.BlockSpec((SUBLANES, D), lambda i: (0, 0)),
         pl.BlockSpec((NCHIP, D, pc), lambda i: (0, 0, 0), pipeline_mode=pl.Buffered(1))],
        [t, pl.BlockSpec((SUBLANES, D), lambda i: (0, 0))],
        [jax.ShapeDtypeStruct((S, D), F32), jax.ShapeDtypeStruct((SUBLANES, D), F32)],
        ("arbitrary",), (dxo, x, dproj, vec, winp), carry)


def loss_head(xf, target, name):
    S = xf.shape[0]
    tm = _row_tile(S, 1024)

    def body(x_ref, t_ref, dy_ref, l_ref):
        diff = x_ref[...] - t_ref[...]
        dy_ref[...] = diff * (1.0 / D)
        part = jnp.sum(jnp.sum(diff * diff, axis=0, keepdims=True), axis=1, keepdims=True) * (0.5 / D)

        @pl.when(pl.program_id(0) == 0)
        def _():
            l_ref[...] = jnp.zeros_like(l_ref)
        l_ref[...] += jnp.broadcast_to(part, l_ref.shape)

    t = pl.BlockSpec((tm, D), lambda i: (i, 0))
    return pl.pallas_call(
        body, name=name, grid=(S // tm,),
        in_specs=[t, t],
        out_specs=[t, pl.BlockSpec((SUBLANES, LANES), lambda i: (0, 0))],
        out_shape=[jax.ShapeDtypeStruct((S, D), F32), jax.ShapeDtypeStruct((SUBLANES, LANES), F32)],
        compiler_params=_params(("arbitrary",)),
    )(xf, target)


def _vec(mod_l, ng_l, i):
    m = mod_l.reshape(3, 3, D)
    rows = jnp.stack([ng_l[i], m[i, 1], m[i, 0], m[i, 2]])
    return jnp.concatenate([rows, jnp.zeros((SUBLANES - 4, D), F32)], axis=0)


def local_step(x, target, mods, ngs, gvecs, cws, shards, w_first, cflag):
    saved = []
    weights = [dict(w1=[None, None], w2=[None, None]) for _ in range(2)]
    weights[0]["w1"][0], weights[0]["w2"][0] = w_first[0], w_first[1].reshape(DFF, D)
    h = x
    for l in range(2):
        w, sh = weights[l], shards[l]
        nxt = shards[l + 1] if l == 0 else None
        vecs = [_vec(mods[l], ngs[l], i) for i in range(3)]
        x0 = h
        (x1, a0, f0), (win, wout) = ffn_fwd(x0, vecs[0], w["w1"][0], w["w2"][0], 0.5, f"ffn_fwd_l{l}a",
                                            carry=Carry("gather", [sh["win"], sh["wout"]]))
        w["win"], w["wout"] = win, wout.reshape(D, D)
        proj, h1b, qn, kn, v = mixer_in(x1, vecs[1], w["win"], gvecs[l], f"mixer_in_l{l}")
        os_, lses_a, lses_b = [], [], []
        for d in DILATIONS:
            carry = {1: Carry("gather", [sh["w2"][1]]), 16: Carry("gather", [sh["w1"][1]])}.get(d)
            (o, la, lb), got = attn_fwd(qn, kn, v, d, f"attn_fwd_l{l}_d{d}", carry=carry)
            if d == 1:
                w["w2"][1] = got[0].reshape(DFF, D)
            if d == 16:
                w["w1"][1] = got[0]
            os_.append(o)
            lses_a.append(la)
            lses_b.append(lb)
        (ycat, *lse), got = combine_conv(os_, lses_a, lses_b, proj, cws[l], f"combine_conv_l{l}",
                                         carry=Carry("gather", [nxt["w2"][0]]) if nxt else None)
        if nxt:
            weights[1]["w2"][0] = got[0].reshape(DFF, D)
        x2, y = out_proj(ycat, x1, vecs[1], w["wout"], f"out_proj_l{l}")
        (x3, a2, f2), got = ffn_fwd(x2, vecs[2], w["w1"][1], w["w2"][1], 0.5, f"ffn_fwd_l{l}b",
                                    carry=Carry("gather", [nxt["w1"][0]]) if nxt else None)
        if nxt:
            weights[1]["w1"][0] = got[0]
        saved.append(dict(vecs=vecs, x0=x0, a0=a0, f0=f0, x1=x1, proj=proj, h1b=h1b, qn=qn, kn=kn, v=v,
                          ycat=ycat, lse=lse, y=y, x2=x2, a2=a2, f2=f2))
        h = x3
    dx, loss_blk = loss_head(h, target, "loss_head")
    sums, totals, g_prev = [None, None], [None, None], None
    w2r = DFF // NCHIP
    for l in (1, 0):
        w, s = weights[l], saved[l]
        vecs = s["vecs"]
        ride = g_prev is not None
        own = l == 0
        mine, other = [None] * 6, [None] * 6

        def half_sum(group, recv, k0):
            return [add_half(g, r, cflag, f"add_sibling_l{l}_{k0 + j}") for j, (g, r) in enumerate(zip(group, recv))]

        def chip_sum(landed, k0):
            return [sum_chips(t, f"sum_chips_l{l}_{k0 + j}") for j, t in enumerate(landed)]

        (dx, hb, dfb, act, da, sums2), got = ffn_bwd(
            dx, s["x2"], s["a2"], s["f2"], vecs[2], w["w1"][1], w["w2"][1], 0.5, f"ffn_bwd_l{l}b",
            carry=Carry("swap_halves", g_prev) if ride else None)
        dw1b, _ = wgrad(hb, da, D, HALF, f"wgrad_w1_l{l}b")
        dw2b, _ = wgrad(act, dfb, HALF, D, f"wgrad_w2_l{l}b")
        if ride:
            wire = [add_half(g_prev[k], got[k], cflag, f"add_sibling_l{l + 1}_{k}") for k in range(6)]
        g_ffn_b = [dw1b, dw2b.reshape(NCHIP, w2r, D)]
        (dyb, dycat, dl_a, dl_b, sums_o), got = out_proj_bwd(
            dx, s["y"], s["ycat"], vecs[1], w["wout"], f"out_proj_bwd_l{l}",
            carry=Carry("swap_halves", g_ffn_b) if own else None)
        dwout, _ = wgrad(s["ycat"].astype(MXU_DTYPE), dyb, D // 2, D, f"wgrad_wout_l{l}")
        if own:
            wire_ffn_b = half_sum(g_ffn_b, got, 4)
        dqs, dks, dvs, landed = [], [], [], {}
        for d in DILATIONS:
            carry = None
            if ride and d == 1:
                carry = Carry("scatter", wire[3:])
            if ride and d == 16:
                carry = Carry("scatter", wire[:3])
            if own and d == 4:
                carry = Carry("scatter", wire_ffn_b)
            (dq, dk, dv), landed[d] = attn_bwd(s["qn"], s["kn"], s["v"], dycat, s["lse"][0], s["lse"][1], dl_a, dl_b,
                                               d, f"attn_bwd_l{l}_d{d}", carry=carry)
            dqs.append(dq)
            dks.append(dk)
            dvs.append(dv)
        if ride:
            tot = [sum_chips(t, f"sum_chips_l{l + 1}_{k}") for k, t in enumerate(list(landed[16]) + list(landed[1]))]
        if own:
            mine[4:6] = chip_sum(landed[4], 4)
        (dproj, sums_m), got = mixer_mid_bwd(dqs, dks, dvs, s["proj"], dycat, gvecs[l], cws[l], f"mixer_mid_bwd_l{l}",
                                             carry=Carry("swap", mine[4:6]) if own else None)
        if own:
            other[4:6] = list(got)
        dwin, _ = wgrad(s["h1b"], dproj, D, INC // NCHIP, f"wgrad_win_l{l}")
        g_mixer = [dwin, dwout.reshape(NCHIP, D // NCHIP, D)]
        (dx, sums1), got = mixer_in_bwd(dx, s["x1"], dproj, vecs[1], w["win"], f"mixer_in_bwd_l{l}",
                                        carry=Carry("swap_halves", g_mixer) if own else None)
        if own:
            wire_mixer = half_sum(g_mixer, got, 2)
        (dx, hb, dfb, act, da, sums0), got = ffn_bwd(
            dx, s["x0"], s["a0"], s["f0"], vecs[0], w["w1"][0], w["w2"][0], 0.5, f"ffn_bwd_l{l}a",
            carry=Carry("swap", tot) if ride else None)
        if ride:
            totals[l + 1] = (tot, list(got))
        dw1a, got = wgrad(hb, da, D, HALF, f"wgrad_w1_l{l}a", carry=Carry("scatter", wire_mixer) if own else None)
        if own:
            mine[2:4] = chip_sum(got, 2)
        dw2a, got = wgrad(act, dfb, HALF, D, f"wgrad_w2_l{l}a", carry=Carry("swap", mine[2:4]) if own else None)
        g_ffn_a = [dw1a, dw2a.reshape(NCHIP, w2r, D)]
        if own:
            other[2:4] = list(got)
            wire_ffn_a = half_sum(g_ffn_a, run_carry(Carry("swap_halves", g_ffn_a), "swap_halves_tail"), 0)
            mine[0:2] = chip_sum(run_carry(Carry("scatter", wire_ffn_a), "scatter_grads_tail"), 0)
            other[0:2] = list(run_carry(Carry("swap", mine[0:2]), "swap_totals_tail"))
            totals[l] = (mine, other)
        g_prev = g_ffn_a + g_mixer + g_ffn_b
        sums[l] = (sums0, sums1, sums_o, sums2, sums_m)
    return loss_blk, dx, totals, sums


def small_all_gather(blk, name):
    m_per, n = blk.shape

    def body(x_ref, out_ref, send_sems, recv_sems, local_sem):
        x, y, c = _here()
        me, sibling = (x, y, c), (x, y, 1 - c)
        chips = [(1 - x, y), (x, 1 - y), (1 - x, 1 - y)]

        def rows(px, py, pc):
            return out_ref.at[pl.ds((4 * px + 2 * py + pc) * m_per, m_per), :]

        def copy(k, block, to, src=None):
            return pltpu.make_async_remote_copy(
                src_ref=rows(*block) if src is None else src, dst_ref=rows(*block),
                send_sem=send_sems.at[k], recv_sem=recv_sems.at[k], device_id=to, device_id_type=MESH)

        mine = pltpu.make_async_copy(x_ref, rows(*me), local_sem)
        mine.start()
        first = [copy(0, me, sibling, src=x_ref)]
        first += [copy(1 + j, me, (*chip, c), src=x_ref) for j, chip in enumerate(chips)]
        for cp in first:
            cp.start()
        passed = [copy(4 + j, (*chip, c), sibling) for j, chip in enumerate(chips)]
        for j, chip in enumerate(chips):
            copy(1 + j, (*chip, c), me).wait_recv()
            passed[j].start()
        copy(0, sibling, me).wait_recv()
        for j, chip in enumerate(chips):
            copy(4 + j, (*chip, 1 - c), me).wait_recv()
        for cp in first + passed:
            cp.wait_send()
        mine.wait()

    return pl.pallas_call(
        body, name=name,
        out_shape=jax.ShapeDtypeStruct((NDEV * m_per, n), blk.dtype),
        in_specs=[pl.BlockSpec(memory_space=pltpu.VMEM)],
        out_specs=pl.BlockSpec(memory_space=pltpu.VMEM),
        scratch_shapes=[pltpu.SemaphoreType.DMA((7,)), pltpu.SemaphoreType.DMA((7,)), pltpu.SemaphoreType.DMA],
        compiler_params=pltpu.CompilerParams(vmem_limit_bytes=VMEM_LIMIT),
    )(blk)


EW_BLOCK_BYTES = 1 << 20


def _ew_rows(rows, cols):
    want = max(16, EW_BLOCK_BYTES // (4 * cols))
    best = None
    for t in range(16, rows + 1, 16):
        if rows % t == 0 and t <= want:
            best = t
    return best if best is not None else rows


def add_half(g, recv, cflag, name):
    pieces, r, cols = g.shape
    r2 = r // 2
    tr = _ew_rows(r2, cols)
    nt = r2 // tr

    def body(c_ref, g_ref, r_ref, o_ref):
        o_ref[...] = (g_ref[...] + r_ref[...]).astype(o_ref.dtype)

    half = pl.BlockSpec((None, tr, cols), lambda j, i, c_ref: (j, i, 0))
    return pl.pallas_call(
        body, name=name,
        grid_spec=pltpu.PrefetchScalarGridSpec(
            num_scalar_prefetch=1, grid=(pieces, nt),
            in_specs=[pl.BlockSpec((None, tr, cols), lambda j, i, c_ref: (j, c_ref[0] * nt + i, 0)), half],
            out_specs=half),
        out_shape=jax.ShapeDtypeStruct((pieces, r2, cols), WIRE_DTYPE),
        compiler_params=_params(("arbitrary", "arbitrary")),
    )(cflag, g, recv)


def sum_chips(recv, name):
    _, r, cols = recv.shape
    tr = _ew_rows(r, cols)

    def body(r_ref, o_ref):
        acc = r_ref[0].astype(F32)
        for k in range(1, NCHIP):
            acc = acc + r_ref[k].astype(F32)
        o_ref[...] = acc

    return pl.pallas_call(
        body, name=name, grid=(r // tr,),
        in_specs=[pl.BlockSpec((NCHIP, tr, cols), lambda i: (0, i, 0))],
        out_specs=pl.BlockSpec((tr, cols), lambda i: (i, 0)),
        out_shape=jax.ShapeDtypeStruct((r, cols), F32),
        compiler_params=_params(("arbitrary",)),
    )(recv)


def sum_devices(rows8, name):
    def body(r_ref, o_ref):
        acc = r_ref[0:1, :]
        for k in range(1, NDEV):
            acc = acc + r_ref[k:k + 1, :]
        o_ref[...] = jnp.broadcast_to(acc, o_ref.shape)

    return pl.pallas_call(
        body, name=name, out_shape=jax.ShapeDtypeStruct(rows8.shape, F32),
        in_specs=[pl.BlockSpec(memory_space=pltpu.VMEM)], out_specs=pl.BlockSpec(memory_space=pltpu.VMEM),
        compiler_params=pltpu.CompilerParams(vmem_limit_bytes=VMEM_LIMIT),
    )(rows8)


def adamw(w, m, v, srcs, cflag, name, halves=False):
    planes, r, cols = w.shape
    rh = r // 2 if halves else r
    tr = _ew_rows(rh, cols)
    nth = rh // tr
    flat = [a for s in srcs for a in (s if halves else (s,))]
    ns = len(flat)
    per = ns // planes

    def body(c_ref, w_ref, m_ref, v_ref, *rest):
        s_refs, (g_ref, d_ref, mo_ref, vo_ref) = rest[:ns], rest[ns:]
        p, i = pl.program_id(0), pl.program_id(1)
        if halves:
            mine = jnp.logical_not(jnp.logical_xor(i >= nth, c_ref[0] == 1))
            blocks = [jnp.where(mine, s_refs[2 * k][...], s_refs[2 * k + 1][...]) for k in range(planes)]
        else:
            blocks = [s[...] for s in s_refs]
        g = blocks[0]
        for k in range(1, planes):
            g = jnp.where(p == k, blocks[k], g)
        g_ref[...] = g
        m_new = ADAM_B1 * m_ref[...] + (1.0 - ADAM_B1) * g
        v_new = ADAM_B2 * v_ref[...] + (1.0 - ADAM_B2) * (g * g)
        mo_ref[...] = m_new
        vo_ref[...] = v_new
        m_hat = m_new / (1.0 - ADAM_B1 ** ADAM_STEP)
        v_hat = v_new / (1.0 - ADAM_B2 ** ADAM_STEP)
        d_ref[...] = -ADAM_LR * (m_hat / (jnp.sqrt(v_hat) + ADAM_EPS) + ADAM_WD * w_ref[...])

    pt = pl.BlockSpec((None, tr, cols), lambda p, i: (p, i, 0))
    st = [pl.BlockSpec((tr, cols), functools.partial(lambda k, p, i: (jnp.where(p == k, i % nth, 0), 0), j // per))
          for j in range(ns)]
    return pl.pallas_call(
        body, name=name, grid=(planes, r // tr),
        in_specs=[pl.BlockSpec(memory_space=pltpu.SMEM), pt, pt, pt] + st,
        out_specs=[pt] * 4,
        out_shape=[jax.ShapeDtypeStruct(w.shape, F32)] * 4,
        compiler_params=_params(("arbitrary", "arbitrary")),
    )(cflag, w, m, v, *flat)


ADA_COLS = 9 * D // NCHIP


def mod_fwd(c_all, w_ada, b_shard, name):
    def body(c_ref, w_ref, b_ref, o_ref):
        cc = c_ref[...]
        sc = cc * jax.nn.sigmoid(cc)
        o_ref[...] = jnp.dot(sc, w_ref[...], preferred_element_type=F32,
                             precision=lax.Precision.HIGHEST) + b_ref[...]

    return pl.pallas_call(
        body, name=name, grid=(2,),
        in_specs=[pl.BlockSpec((NDEV, D), lambda l: (0, 0)),
                  pl.BlockSpec((None, D, ADA_COLS), lambda l: (l, 0, 0)),
                  pl.BlockSpec((None, 1, ADA_COLS), lambda l: (l, 0, 0))],
        out_specs=pl.BlockSpec((None, NDEV, ADA_COLS), lambda l: (l, 0, 0)),
        out_shape=jax.ShapeDtypeStruct((2, NDEV, ADA_COLS), F32),
        compiler_params=_params(("arbitrary",)),
    )(c_all, w_ada, b_shard.reshape(2, 1, ADA_COLS))


def wada_grad(c_all_t, dmod, name):
    ct = ADA_COLS // 3

    def body(c_ref, d_ref, o_ref):
        cc = c_ref[...]
        sc = cc * jax.nn.sigmoid(cc)
        acc = sc[:, 0:1] * d_ref[0:1, :]
        for b in range(1, NDEV):
            acc = acc + sc[:, b:b + 1] * d_ref[b:b + 1, :]
        o_ref[...] = acc

    return pl.pallas_call(
        body, name=name, grid=(2, 3),
        in_specs=[pl.BlockSpec((D, LANES), lambda l, j: (0, 0)),
                  pl.BlockSpec((None, NDEV, ct), lambda l, j: (l, 0, j))],
        out_specs=pl.BlockSpec((None, D, ct), lambda l, j: (l, 0, j)),
        out_shape=jax.ShapeDtypeStruct((2, D, ADA_COLS), F32),
        compiler_params=_params(("arbitrary", "arbitrary")),
    )(c_all_t, dmod)


def _pad_rows(row, rows=SUBLANES):
    return jnp.concatenate([row[None, :], jnp.zeros((rows - 1, row.shape[0]), row.dtype)], axis=0)


def kernel(x, c, w_ada, b_ada, norm_g, w_in, q_norm_g, k_norm_g, conv_w, conv_b, w_out, ffn_w1, ffn_w2, loss_target, m_w_ada, m_b_ada, m_norm_g, m_w_in, m_q_norm_g, m_k_norm_g, m_conv_w, m_conv_b, m_w_out, m_ffn_w1, m_ffn_w2, v_w_ada, v_b_ada, v_norm_g, v_w_in, v_q_norm_g, v_k_norm_g, v_conv_w, v_conv_b, v_w_out, v_ffn_w1, v_ffn_w2):
    ix, iy, ic = lax.axis_index("x"), lax.axis_index("y"), lax.axis_index("c")
    chip = 2 * ix + iy
    dev = 2 * chip + ic
    cflag = jnp.reshape(ic, (1,)).astype(jnp.int32)
    ngw = norm_g.shape[-1]
    cww = conv_w.shape[-1]

    pack = jnp.concatenate([c[0], norm_g.reshape(-1), conv_w.reshape(-1)])
    got = small_all_gather(_pad_rows(pack), "gather_c_normg_convw")[::SUBLANES]
    c_all = got[:, :D]
    per_chip = got[::2]
    ng_full = jnp.concatenate([per_chip[j, D:D + 6 * ngw].reshape(2, 3, ngw) for j in range(NCHIP)], axis=-1)
    cw_full = jnp.concatenate([per_chip[j, D + 6 * ngw:].reshape(2, 3, cww) for j in range(NCHIP)], axis=-1)

    b_shard = lax.dynamic_slice_in_dim(b_ada, chip * ADA_COLS, ADA_COLS, axis=1)
    mod_blk = mod_fwd(c_all, w_ada, b_shard, "mod_fwd").reshape(2 * NDEV, ADA_COLS)
    mod_all = small_all_gather(mod_blk, "gather_mod").reshape(NDEV, 2, NDEV, ADA_COLS)[::2]
    mod_mine = lax.dynamic_index_in_dim(mod_all, dev, axis=2, keepdims=False)
    mods = [mod_mine[:, l, :].reshape(-1) for l in range(2)]

    shards, gvecs, cws = [], [], []
    for l in range(2):
        shards.append(dict(w1=[ffn_w1[l, i].astype(MXU_DTYPE) for i in range(2)],
                           w2=[ffn_w2[l, i].astype(MXU_DTYPE) for i in range(2)],
                           win=w_in[l].astype(MXU_DTYPE), wout=w_out[l].astype(MXU_DTYPE)))
        gv = jnp.stack([jnp.tile(q_norm_g[l], AW // HD), jnp.tile(k_norm_g[l], AW // HD)])
        gvecs.append(jnp.concatenate([gv, jnp.zeros((SUBLANES - 2, AW), F32)], axis=0))
        cws.append(jnp.concatenate([cw_full[l], conv_b[l][None, :], jnp.zeros((SUBLANES - 4, CW), F32)], axis=0))
    w_first = run_carry(Carry("gather", [shards[0]["w1"][0], shards[0]["w2"][0]]), "gather_first_ffn")

    loss_blk, dx, totals, sums = local_step(x[0], loss_target[0], mods, [ng_full[0], ng_full[1]], gvecs, cws,
                                            shards, w_first, cflag)
    loss = lax.psum(loss_blk[0, 0], ("x", "y", "c"))

    dmods, dngs, dqg, dkg, dcw, dcb = [], [], [], [], [], []
    for l in range(2):
        s0, s1, so, s2, sm = sums[l]
        dmods.append(jnp.concatenate([s0[0], s0[1], s0[3], s1[0], s1[1], so[0], s2[0], s2[1], s2[3]]))
        dngs.append(jnp.concatenate([s0[2], s1[2], s2[2]]))
        dqg.append(sm[0].reshape(AW // HD, HD).sum(0))
        dkg.append(sm[1].reshape(AW // HD, HD).sum(0))
        dcw.append(sm[2:5].reshape(-1))
        dcb.append(sm[5])
    small = jnp.concatenate(dmods + dngs + dqg + dkg + dcw + dcb)
    small_all = small_all_gather(_pad_rows(small), "gather_small_grads")[::SUBLANES]
    nm = 9 * D
    dmod_all = small_all[:, :2 * nm].reshape(NDEV, 2, NCHIP, ADA_COLS)
    dmod_mine = lax.dynamic_index_in_dim(dmod_all, chip, axis=2, keepdims=False).transpose(1, 0, 2)
    tot = sum_devices(small_all, "sum_small_grads")[0]
    o = 2 * nm
    g_b_ada = tot[:o].reshape(2, nm)
    g_norm_g = lax.dynamic_slice_in_dim(tot[o:o + 6 * D].reshape(2, 3, D), chip * ngw, ngw, axis=2)
    o += 6 * D
    g_qg = tot[o:o + 2 * HD].reshape(2, HD)
    o += 2 * HD
    g_kg = tot[o:o + 2 * HD].reshape(2, HD)
    o += 2 * HD
    g_cw = lax.dynamic_slice_in_dim(tot[o:o + 6 * CW].reshape(2, 3, CW), chip * cww, cww, axis=2)
    o += 6 * CW
    g_cb = tot[o:o + 2 * CW].reshape(2, CW)

    c_all_t = jnp.concatenate([c_all.T, jnp.zeros((D, LANES - NDEV), F32)], axis=1)
    g_wada_src = wada_grad(c_all_t, dmod_mine, "wada_grad")

    def halves(k_of_plane):
        return [(totals[l][0][k], totals[l][1][k]) for l, k in k_of_plane]

    r_wada = adamw(w_ada, m_w_ada, v_w_ada, [g_wada_src[0], g_wada_src[1]], cflag, "adamw_w_ada")
    r_win = adamw(w_in, m_w_in, v_w_in, halves([(0, 2), (1, 2)]), cflag, "adamw_w_in", halves=True)
    r_wout = adamw(w_out, m_w_out, v_w_out, halves([(0, 3), (1, 3)]), cflag, "adamw_w_out", halves=True)
    r_w1 = adamw(ffn_w1.reshape(4, D, HALF), m_ffn_w1.reshape(4, D, HALF), v_ffn_w1.reshape(4, D, HALF),
                 halves([(0, 0), (0, 4), (1, 0), (1, 4)]), cflag, "adamw_ffn_w1", halves=True)
    w2r = DFF // NCHIP
    r_w2 = adamw(ffn_w2.reshape(4, w2r, D), m_ffn_w2.reshape(4, w2r, D), v_ffn_w2.reshape(4, w2r, D),
                 halves([(0, 1), (0, 5), (1, 1), (1, 5)]), cflag, "adamw_ffn_w2", halves=True)
    r_w1 = [t.reshape(ffn_w1.shape) for t in r_w1]
    r_w2 = [t.reshape(ffn_w2.shape) for t in r_w2]

    smalls = [("b_ada", b_ada, m_b_ada, v_b_ada, g_b_ada), ("norm_g", norm_g, m_norm_g, v_norm_g, g_norm_g),
              ("q_norm_g", q_norm_g, m_q_norm_g, v_q_norm_g, g_qg), ("k_norm_g", k_norm_g, m_k_norm_g, v_k_norm_g, g_kg),
              ("conv_w", conv_w, m_conv_w, v_conv_w, g_cw), ("conv_b", conv_b, m_conv_b, v_conv_b, g_cb)]
    n_small = sum(t[1].size for t in smalls)
    pad = (-n_small) % (16 * LANES)

    def packed(idx):
        flat = jnp.concatenate([t[idx].reshape(-1) for t in smalls] + [jnp.zeros((pad,), F32)])
        return flat.reshape(-1, LANES)

    r_small = adamw(packed(1)[None], packed(2)[None], packed(3)[None], [packed(4)], cflag, "adamw_small")
    small_out = {}
    o = 0
    for name_, w_, _, _, _ in smalls:
        small_out[name_] = [t.reshape(-1)[o:o + w_.size].reshape(w_.shape) for t in r_small]
        o += w_.size

    res = {"w_ada": r_wada, "w_in": r_win, "w_out": r_wout, "ffn_w1": r_w1, "ffn_w2": r_w2, **small_out}
    order = ["w_ada", "b_ada", "norm_g", "w_in", "q_norm_g", "k_norm_g", "conv_w", "conv_b", "w_out", "ffn_w1", "ffn_w2"]
    outs = [loss, dx[None]]
    for k in range(4):
        outs += [res[nm_][k] for nm_ in order]
    return tuple(outs)
```

```python
import functools

import jax
import jax.numpy as jnp
from jax import lax
from jax.experimental import pallas as pl
from jax.experimental.pallas import tpu as pltpu

F32 = jnp.float32
MXU_DTYPE = jnp.bfloat16
ACT_DTYPE = jnp.bfloat16
WIRE_DTYPE = jnp.bfloat16

D = 1024
HD = 64
AW = 512
CW = 512
DFF = 2816
HALF = DFF // 2
INC = 3 * AW + 3 * CW
NCHIP = 4
NDEV = 8
QBLK = 128
ATTN_QBLOCKS = 4
DILATIONS = (1, 4, 16)
EPS = 1e-6
NEG = -1e30
LANES = 128
SUBLANES = 8
VMEM_LIMIT = 56 * 1024 * 1024

ADAM_LR = 0.001
ADAM_B1 = 0.9
ADAM_B2 = 0.999
ADAM_EPS = 1e-08
ADAM_WD = 0.01
ADAM_STEP = 10

NT_DIMS = (((1,), (1,)), ((), ()))
TN_DIMS = (((0,), (0,)), ((), ()))


def _params(sem, vmem=VMEM_LIMIT):
    return pltpu.CompilerParams(dimension_semantics=sem, vmem_limit_bytes=vmem)


def _row_tile(n, want):
    t = min(n, want)
    assert n % t == 0
    return t


def _ada(xt, vec_ref):
    ng, sc, sh, gt = vec_ref[0:1, :], vec_ref[1:2, :], vec_ref[2:3, :], vec_ref[3:4, :]
    r = lax.rsqrt(jnp.mean(xt * xt, axis=-1, keepdims=True) + EPS)
    return xt * r, r, ng * (1.0 + sc), ng, sc, sh, gt


def _ada_bwd(dh, xhat, r, gain, ng, sc):
    dshift = jnp.sum(dh, axis=0, keepdims=True)
    dhx = dh * xhat
    dscale = jnp.sum(dhx, axis=0, keepdims=True) * ng
    dng = jnp.sum(dhx, axis=0, keepdims=True) * (1.0 + sc)
    dxhat = dh * gain
    dx = r * (dxhat - xhat * jnp.mean(dxhat * xhat, axis=-1, keepdims=True))
    return dx, dshift, dscale, dng


def _acc_rows(sums_ref, first, rows):
    @pl.when(first)
    def _():
        sums_ref[...] = jnp.zeros_like(sums_ref)
    for k, row in enumerate(rows):
        sums_ref[k:k + 1, :] += row


MESH = pl.DeviceIdType.MESH
ANY = pl.BlockSpec(memory_space=pl.ANY)


def _here():
    return lax.axis_index("x"), lax.axis_index("y"), lax.axis_index("c")


def _ici_copies(src_refs, dst_refs, send_sems, recv_sems, local_sems, scatter):
    x, y, c = _here()
    my_chip = 2 * x + y
    peers = [(1 - x, y), (x, 1 - y), (1 - x, 1 - y)]
    local, out, inc = [], [], []
    for a, (src, dst) in enumerate(zip(src_refs, dst_refs)):
        local.append(pltpu.make_async_copy(src.at[my_chip] if scatter else src, dst.at[my_chip], local_sems.at[a]))
        for j, (px, py) in enumerate(peers):
            sems = dict(send_sem=send_sems.at[3 * a + j], recv_sem=recv_sems.at[3 * a + j],
                        device_id=(px, py, c), device_id_type=MESH)
            out.append(pltpu.make_async_remote_copy(
                src_ref=src.at[2 * px + py] if scatter else src, dst_ref=dst.at[my_chip], **sems))
            inc.append(pltpu.make_async_remote_copy(
                src_ref=src.at[my_chip] if scatter else src, dst_ref=dst.at[2 * px + py], **sems))
    return local, out, inc


def _swap_copies(src_refs, dst_refs, send_sems, recv_sems, halves):
    x, y, c = _here()
    cps = []
    for k, (src, dst) in enumerate(zip(src_refs, dst_refs)):
        if halves:
            r2 = src.shape[1] // 2
            src = src.at[:, pl.ds((1 - c) * r2, r2), :]
        cps.append(pltpu.make_async_remote_copy(
            src_ref=src, dst_ref=dst, send_sem=send_sems.at[k], recv_sem=recv_sems.at[k],
            device_id=(x, y, 1 - c), device_id_type=MESH))
    return cps


class Carry:
    def __init__(self, kind, srcs):
        self.kind, self.srcs, n = kind, list(srcs), len(srcs)
        if kind == "gather":
            shapes = [(NCHIP,) + s.shape for s in srcs]
        elif kind == "swap_halves":
            shapes = [(s.shape[0], s.shape[1] // 2, s.shape[2]) for s in srcs]
        else:
            shapes = [s.shape for s in srcs]
        self.out_shape = [jax.ShapeDtypeStruct(sh, s.dtype) for sh, s in zip(shapes, srcs)]
        dma = pltpu.SemaphoreType.DMA
        self.sems = [dma((3 * n,)), dma((3 * n,)), dma((n,))] if kind in ("gather", "scatter") else [dma((n,)), dma((n,))]

    def start(self, srcs, dsts, sems):
        if self.kind in ("gather", "scatter"):
            local, out, _ = _ici_copies(srcs, dsts, *sems, self.kind == "scatter")
            for cp in local + out:
                cp.start()
        else:
            for cp in _swap_copies(srcs, dsts, *sems, self.kind == "swap_halves"):
                cp.start()

    def wait(self, srcs, dsts, sems):
        if self.kind in ("gather", "scatter"):
            local, out, inc = _ici_copies(srcs, dsts, *sems, self.kind == "scatter")
            for cp in inc:
                cp.wait_recv()
            for cp in out:
                cp.wait_send()
            for cp in local:
                cp.wait()
        else:
            cps = _swap_copies(srcs, dsts, *sems, self.kind == "swap_halves")
            for cp in cps:
                cp.wait_recv()
            for cp in cps:
                cp.wait_send()


def run_carry(carry, name):
    n = len(carry.srcs)

    def body(*refs):
        srcs, dsts, sems = refs[:n], refs[n:2 * n], refs[2 * n:]
        carry.start(srcs, dsts, sems)
        carry.wait(srcs, dsts, sems)

    return pl.pallas_call(body, name=name, out_shape=carry.out_shape, in_specs=[ANY] * n, out_specs=[ANY] * n,
                          scratch_shapes=carry.sems)(*carry.srcs)


def _pcall(body, name, grid, in_specs, out_specs, out_shape, sem, args, carry=None, scratch=()):
    if carry is None:
        outs = pl.pallas_call(body, name=name, grid=grid, in_specs=in_specs, out_specs=out_specs,
                              out_shape=out_shape, scratch_shapes=list(scratch), compiler_params=_params(sem))(*args)
        return outs, []
    n_in, n_out, nc, ns = len(in_specs), len(out_specs), len(carry.srcs), len(scratch)

    def wrapped(*refs):
        ins, csrc = refs[:n_in], refs[n_in:n_in + nc]
        outs, cdst = refs[n_in + nc:n_in + nc + n_out], refs[n_in + nc + n_out:n_in + 2 * nc + n_out]
        own = refs[n_in + 2 * nc + n_out:n_in + 2 * nc + n_out + ns]
        sems = refs[n_in + 2 * nc + n_out + ns:]
        ids = [pl.program_id(a) for a in range(len(grid))]
        first = functools.reduce(jnp.logical_and, [i == 0 for i in ids])
        last = functools.reduce(jnp.logical_and, [i == g - 1 for i, g in zip(ids, grid)])

        @pl.when(first)
        def _():
            carry.start(csrc, cdst, sems)

        body(*ins, *outs, *own)

        @pl.when(last)
        def _():
            carry.wait(csrc, cdst, sems)

    res = pl.pallas_call(
        wrapped, name=name, grid=grid,
        in_specs=list(in_specs) + [ANY] * nc, out_specs=list(out_specs) + [ANY] * nc,
        out_shape=list(out_shape) + carry.out_shape,
        scratch_shapes=list(scratch) + carry.sems, compiler_params=_params(sem),
    )(*args, *carry.srcs)
    return res[:n_out], res[n_out:]


def ffn_fwd(x, vec, w1p, w2, gs, name, carry=None):
    S = x.shape[0]
    tm = _row_tile(S, 512)

    def body(x_ref, vec_ref, w1_ref, w2_ref, xn_ref, a_ref, f_ref):
        xt = x_ref[...]
        xhat, _, gain, _, _, sh, gt = _ada(xt, vec_ref)
        h = (xhat * gain + sh).astype(MXU_DTYPE)
        f = jnp.zeros((tm, D), F32)
        for hf in range(2):
            g = jnp.dot(h, w1_ref[hf], preferred_element_type=F32)
            up = jnp.dot(h, w1_ref[2 + hf], preferred_element_type=F32)
            a_ref[:, hf * HALF:(hf + 1) * HALF] = g.astype(a_ref.dtype)
            a_ref[:, DFF + hf * HALF:DFF + (hf + 1) * HALF] = up.astype(a_ref.dtype)
            act = (g * jax.nn.sigmoid(g) * up).astype(MXU_DTYPE)
            f = f + jnp.dot(act, w2_ref[hf * HALF:(hf + 1) * HALF, :], preferred_element_type=F32)
        xn_ref[...] = xt + (gs * gt) * f
        f_ref[...] = f.astype(f_ref.dtype)

    return _pcall(
        body, name, (S // tm,),
        [pl.BlockSpec((tm, D), lambda i: (i, 0)),
         pl.BlockSpec((SUBLANES, D), lambda i: (0, 0)),
         pl.BlockSpec((NCHIP, D, HALF), lambda i: (0, 0, 0), pipeline_mode=pl.Buffered(1)),
         pl.BlockSpec((DFF, D), lambda i: (0, 0), pipeline_mode=pl.Buffered(1))],
        [pl.BlockSpec((tm, D), lambda i: (i, 0)),
         pl.BlockSpec((tm, 2 * DFF), lambda i: (i, 0)),
         pl.BlockSpec((tm, D), lambda i: (i, 0))],
        [jax.ShapeDtypeStruct((S, D), F32),
         jax.ShapeDtypeStruct((S, 2 * DFF), ACT_DTYPE),
         jax.ShapeDtypeStruct((S, D), ACT_DTYPE)],
        ("arbitrary",), (x, vec, w1p, w2), carry)


def ffn_bwd(dxo, x, a, f, vec, w1p, w2, gs, name, carry=None):
    S = x.shape[0]
    tm = _row_tile(S, 256)

    def body(dxo_ref, x_ref, a_ref, f_ref, vec_ref, w1_ref, w2_ref,
             dxi_ref, hb_ref, dfb_ref, act_ref, da_ref, sums_ref):
        xt = x_ref[...]
        dxo = dxo_ref[...]
        xhat, r, gain, ng, sc, sh, gt = _ada(xt, vec_ref)
        hb_ref[...] = (xhat * gain + sh).astype(hb_ref.dtype)
        dgate = gs * jnp.sum(dxo * f_ref[...].astype(F32), axis=0, keepdims=True)
        df = ((gs * gt) * dxo).astype(MXU_DTYPE)
        dfb_ref[...] = df
        dh = jnp.zeros((tm, D), F32)
        for hf in range(2):
            lo, hi = hf * HALF, (hf + 1) * HALF
            dact = lax.dot_general(df, w2_ref[lo:hi, :], NT_DIMS, preferred_element_type=F32)
            g = a_ref[:, lo:hi].astype(F32)
            up = a_ref[:, DFF + lo:DFF + hi].astype(F32)
            sg = jax.nn.sigmoid(g)
            si = g * sg
            act_ref[:, lo:hi] = (si * up).astype(act_ref.dtype)
            dg = (dact * up * (sg * (1.0 + g * (1.0 - sg)))).astype(MXU_DTYPE)
            dup = (dact * si).astype(MXU_DTYPE)
            da_ref[:, lo:hi] = dg
            da_ref[:, DFF + lo:DFF + hi] = dup
            dh = dh + lax.dot_general(dg, w1_ref[hf], NT_DIMS, preferred_element_type=F32)
            dh = dh + lax.dot_general(dup, w1_ref[2 + hf], NT_DIMS, preferred_element_type=F32)
        dx, dshift, dscale, dng = _ada_bwd(dh, xhat, r, gain, ng, sc)
        dxi_ref[...] = dxo + dx
        _acc_rows(sums_ref, pl.program_id(0) == 0, (dshift, dscale, dng, dgate))

    return _pcall(
        body, name, (S // tm,),
        [pl.BlockSpec((tm, D), lambda i: (i, 0)),
         pl.BlockSpec((tm, D), lambda i: (i, 0)),
         pl.BlockSpec((tm, 2 * DFF), lambda i: (i, 0)),
         pl.BlockSpec((tm, D), lambda i: (i, 0)),
         pl.BlockSpec((SUBLANES, D), lambda i: (0, 0)),
         pl.BlockSpec((NCHIP, D, HALF), lambda i: (0, 0, 0), pipeline_mode=pl.Buffered(1)),
         pl.BlockSpec((DFF, D), lambda i: (0, 0), pipeline_mode=pl.Buffered(1))],
        [pl.BlockSpec((tm, D), lambda i: (i, 0)),
         pl.BlockSpec((tm, D), lambda i: (i, 0)),
         pl.BlockSpec((tm, D), lambda i: (i, 0)),
         pl.BlockSpec((tm, DFF), lambda i: (i, 0)),
         pl.BlockSpec((tm, 2 * DFF), lambda i: (i, 0)),
         pl.BlockSpec((SUBLANES, D), lambda i: (0, 0))],
        [jax.ShapeDtypeStruct((S, D), F32),
         jax.ShapeDtypeStruct((S, D), MXU_DTYPE),
         jax.ShapeDtypeStruct((S, D), MXU_DTYPE),
         jax.ShapeDtypeStruct((S, DFF), MXU_DTYPE),
         jax.ShapeDtypeStruct((S, 2 * DFF), MXU_DTYPE),
         jax.ShapeDtypeStruct((SUBLANES, D), F32)],
        ("arbitrary",), (dxo, x, a, f, vec, w1p, w2), carry)


def wgrad(a, b, kt, nt, name, carry=None):
    T, K = a.shape
    N = b.shape[1]
    pk, pn = K // kt, N // nt
    assert pk == 1 or pn == 1
    tt = _row_tile(T, 1024)
    steps = T // tt

    def body(a_ref, b_ref, o_ref):
        @pl.when(pl.program_id(1) == 0)
        def _():
            o_ref[...] = jnp.zeros_like(o_ref)
        o_ref[...] += lax.dot_general(a_ref[...], b_ref[...], TN_DIMS, preferred_element_type=F32)

    a_map = (lambda p, t: (t, p)) if pk > 1 else (lambda p, t: (t, 0))
    b_map = (lambda p, t: (t, p)) if pn > 1 else (lambda p, t: (t, 0))
    (out,), got = _pcall(
        body, name, (pk * pn, steps),
        [pl.BlockSpec((tt, kt), a_map), pl.BlockSpec((tt, nt), b_map)],
        [pl.BlockSpec((None, kt, nt), lambda p, t: (p, 0, 0))],
        [jax.ShapeDtypeStruct((pk * pn, kt, nt), F32)], ("arbitrary", "arbitrary"), (a, b), carry)
    return out, got


def _head_masks(rows):
    lane = lax.broadcasted_iota(jnp.int32, (rows, LANES), 1)
    return lane < HD


def _pair_stat(x, m_a):
    s_a = jnp.sum(jnp.where(m_a, x, 0.0), axis=1, keepdims=True)
    s_b = jnp.sum(jnp.where(m_a, 0.0, x), axis=1, keepdims=True)
    return s_a, s_b


def mixer_in(x, vec, winp, gvec, name):
    S = x.shape[0]
    tm = _row_tile(S, 512)
    pc = INC // NCHIP

    def body(x_ref, vec_ref, w_ref, g_ref, proj_ref, hb_ref, qn_ref, kn_ref, v_ref):
        xt = x_ref[...]
        xhat, _, gain, _, _, sh, _ = _ada(xt, vec_ref)
        h = (xhat * gain + sh).astype(MXU_DTYPE)
        hb_ref[...] = h
        for j in range(NCHIP):
            proj_ref[:, j * pc:(j + 1) * pc] = jnp.dot(h, w_ref[j], preferred_element_type=F32)
        m_a = _head_masks(tm)
        for which, dst in ((0, qn_ref), (1, kn_ref)):
            for p in range(AW // LANES):
                lo = which * AW + p * LANES
                xp = proj_ref[:, lo:lo + LANES]
                s_a, s_b = _pair_stat(xp * xp, m_a)
                rr = jnp.where(m_a, lax.rsqrt(s_a * (1.0 / HD) + EPS), lax.rsqrt(s_b * (1.0 / HD) + EPS))
                gp = g_ref[which:which + 1, p * LANES:(p + 1) * LANES]
                dst[:, p * LANES:(p + 1) * LANES] = (xp * rr * gp).astype(dst.dtype)
        v_ref[...] = proj_ref[:, 2 * AW:3 * AW].astype(v_ref.dtype)

    return pl.pallas_call(
        body, name=name, grid=(S // tm,),
        in_specs=[pl.BlockSpec((tm, D), lambda i: (i, 0)),
                  pl.BlockSpec((SUBLANES, D), lambda i: (0, 0)),
                  pl.BlockSpec((NCHIP, D, pc), lambda i: (0, 0, 0), pipeline_mode=pl.Buffered(1)),
                  pl.BlockSpec((SUBLANES, AW), lambda i: (0, 0))],
        out_specs=[pl.BlockSpec((tm, INC), lambda i: (i, 0)),
                   pl.BlockSpec((tm, D), lambda i: (i, 0)),
                   pl.BlockSpec((tm, AW), lambda i: (i, 0)),
                   pl.BlockSpec((tm, AW), lambda i: (i, 0)),
                   pl.BlockSpec((tm, AW), lambda i: (i, 0))],
        out_shape=[jax.ShapeDtypeStruct((S, INC), F32),
                   jax.ShapeDtypeStruct((S, D), MXU_DTYPE),
                   jax.ShapeDtypeStruct((S, AW), F32),
                   jax.ShapeDtypeStruct((S, AW), F32),
                   jax.ShapeDtypeStruct((S, AW), F32)],
        compiler_params=_params(("arbitrary",)),
    )(x, vec, winp, gvec)


def _band_masks(ncol):
    row = lax.broadcasted_iota(jnp.int32, (2 * QBLK, ncol), 0) & (QBLK - 1)
    col = lax.broadcasted_iota(jnp.int32, (2 * QBLK, ncol), 1)
    return row, col


def _stack_heads(t, m_a):
    zero = jnp.zeros_like(t)
    return jnp.concatenate([jnp.where(m_a, t, zero), jnp.where(m_a, zero, t)], axis=0)


class _AttnLayout:
    def __init__(self, d, S):
        self.d, self.S = d, S
        self.grouped = d % SUBLANES == 0
        self.qb = min(ATTN_QBLOCKS, S // d // QBLK)
        self.nres = SUBLANES if self.grouped else d
        self.nchunk = S // d // (self.qb * QBLK)
        self.grid = (AW // LANES, d // SUBLANES if self.grouped else 1, self.nchunk)

    def view(self, a):
        return a.reshape(self.S // self.d, self.d, a.shape[-1]) if self.grouped else a

    def unview(self, a):
        return a.reshape(self.S, a.shape[-1]) if self.grouped else a

    def out_shape(self, cols):
        shape = (self.S // self.d, self.d, cols) if self.grouped else (self.S, cols)
        return jax.ShapeDtypeStruct(shape, F32)

    def _spec(self, blocks, row_of):
        if self.grouped:
            return pl.BlockSpec((blocks * QBLK, SUBLANES, LANES), lambda hp, g, j: (row_of(j), g, hp))
        return pl.BlockSpec((blocks * QBLK * self.d, LANES), lambda hp, g, j: (row_of(j), hp))

    def cur(self, chunk_of):
        return self._spec(self.qb, chunk_of)

    def prev(self, chunk_of):
        return self._spec(1, lambda j: jnp.maximum(chunk_of(j) * self.qb - 1, 0))

    def idx(self, b, r):
        if self.grouped:
            return (pl.ds(b * QBLK, QBLK), r, slice(None))
        if self.d == 1:
            return (pl.ds(b * QBLK, QBLK), slice(None))
        return (pl.ds(b * QBLK * self.d + r, QBLK, stride=self.d), slice(None))

    def per_residue(self, fn):
        if self.nres == 1:
            fn(0)
        else:
            def step(r, carry):
                fn(r)
                return carry
            lax.fori_loop(0, self.nres, step, 0)


def attn_fwd(qn, kn, v, d, name, carry=None):
    S = qn.shape[0]
    lay = _AttnLayout(d, S)
    qb = lay.qb

    def body(q_ref, kc_ref, kp_ref, vc_ref, vp_ref, o_ref, la_ref, lb_ref):
        i = pl.program_id(2)
        m_a = _head_masks(QBLK)
        row, col = _band_masks(2 * QBLK)
        dist = row + QBLK - col
        band = (dist >= 0) & (dist <= QBLK)
        first = band & ((i > 0) | (col >= QBLK))

        def residue(r):
            kt = [kp_ref[lay.idx(0, r)].astype(MXU_DTYPE)]
            vt = [vp_ref[lay.idx(0, r)].astype(MXU_DTYPE)]
            for b in range(qb):
                kt.append(kc_ref[lay.idx(b, r)].astype(MXU_DTYPE))
                vt.append(vc_ref[lay.idx(b, r)].astype(MXU_DTYPE))
            for b in range(qb):
                rows = lay.idx(b, r)
                q = (q_ref[rows] * (HD ** -0.5)).astype(MXU_DTYPE)
                kcat = jnp.concatenate([kt[b], kt[b + 1]], axis=0)
                vcat = jnp.concatenate([vt[b], vt[b + 1]], axis=0)
                mask = first if b == 0 else band
                s = lax.dot_general(_stack_heads(q, m_a), kcat, NT_DIMS, preferred_element_type=F32)
                s = jnp.where(mask, s, NEG)
                m = jnp.max(s, axis=1, keepdims=True)
                p = jnp.exp(s - m)
                l = jnp.sum(p, axis=1, keepdims=True)
                o = jnp.dot(p.astype(MXU_DTYPE), vcat, preferred_element_type=F32) / l
                lse = jnp.broadcast_to(m + jnp.log(l), (2 * QBLK, LANES))
                o_ref[rows] = jnp.where(m_a, o[:QBLK], o[QBLK:])
                la_ref[rows] = lse[:QBLK]
                lb_ref[rows] = lse[QBLK:]

        lay.per_residue(residue)

    cur, prev = lay.cur(lambda j: j), lay.prev(lambda j: j)
    q3, k3, v3 = lay.view(qn), lay.view(kn), lay.view(v)
    outs, got = _pcall(body, name, lay.grid, [cur, cur, prev, cur, prev], [cur, cur, cur], [lay.out_shape(AW)] * 3,
                       ("arbitrary",) * 3, (q3, k3, k3, v3, v3), carry)
    return [lay.unview(t) for t in outs], got


def attn_bwd(qn, kn, v, dycat, lse_a, lse_b, dl_a, dl_b, d, name, carry=None):
    S = qn.shape[0]
    lay = _AttnLayout(d, S)
    qb, nchunk = lay.qb, lay.nchunk

    def body(q_ref, kc_ref, kp_ref, vc_ref, vp_ref, do_ref, la_ref, lb_ref, da_ref, db_ref,
             dq_ref, dk_ref, dv_ref, ck_ref, cv_ref):
        j = pl.program_id(2)
        i = nchunk - 1 - j
        m_a = _head_masks(QBLK)
        row, col = _band_masks(2 * QBLK)
        dist = row + QBLK - col
        band = (dist >= 0) & (dist <= QBLK)
        first = band & ((i > 0) | (col >= QBLK))

        def residue(r):
            def tiles(ref, cast):
                out = [ref[lay.idx(b, r)] for b in range(qb)]
                return [t.astype(MXU_DTYPE) for t in out] if cast else out

            def ktiles(cur_ref, prev_ref):
                return [prev_ref[lay.idx(0, r)].astype(MXU_DTYPE)] + tiles(cur_ref, True)

            qt = [(t * (HD ** -0.5)).astype(MXU_DTYPE) for t in tiles(q_ref, False)]
            dot_ = tiles(do_ref, True)
            lse_t = list(zip(tiles(la_ref, False), tiles(lb_ref, False)))
            dl_t = list(zip(tiles(da_ref, False), tiles(db_ref, False)))
            kt = ktiles(kc_ref, kp_ref)
            vt = ktiles(vc_ref, vp_ref)
            dk_acc = [jnp.zeros((QBLK, LANES), F32) for _ in range(qb)]
            dv_acc = [jnp.zeros((QBLK, LANES), F32) for _ in range(qb)]
            crow = pl.ds(0, QBLK) if lay.nres == 1 else pl.ds(pl.multiple_of(r * QBLK, QBLK), QBLK)
            dk_acc[qb - 1] = jnp.where(j > 0, ck_ref[crow, :], 0.0)
            dv_acc[qb - 1] = jnp.where(j > 0, cv_ref[crow, :], 0.0)
            for x in range(qb):
                kcat = jnp.concatenate([kt[x], kt[x + 1]], axis=0)
                vcat = jnp.concatenate([vt[x], vt[x + 1]], axis=0)
                q2 = _stack_heads(qt[x], m_a)
                do2 = _stack_heads(dot_[x], m_a)
                lse2 = jnp.concatenate(lse_t[x], axis=0)
                dl2 = jnp.concatenate(dl_t[x], axis=0)
                lse2 = jnp.concatenate([lse2, lse2], axis=1)
                dl2 = jnp.concatenate([dl2, dl2], axis=1)
                s = lax.dot_general(q2, kcat, NT_DIMS, preferred_element_type=F32)
                p = jnp.exp(jnp.where(first if x == 0 else band, s, NEG) - lse2)
                dp = lax.dot_general(do2, vcat, NT_DIMS, preferred_element_type=F32)
                ds = p * (dp - dl2)
                dq = jnp.dot(ds.astype(MXU_DTYPE), kcat, preferred_element_type=F32)
                dq_ref[lay.idx(x, r)] = jnp.where(m_a, dq[:QBLK], dq[QBLK:]) * (HD ** -0.5)
                dk = jnp.dot(ds.T.astype(MXU_DTYPE), q2, preferred_element_type=F32)
                dv = jnp.dot(p.T.astype(MXU_DTYPE), do2, preferred_element_type=F32)
                if x == 0:
                    ck_ref[crow, :] = dk[:QBLK]
                    cv_ref[crow, :] = dv[:QBLK]
                else:
                    dk_acc[x - 1] = dk_acc[x - 1] + dk[:QBLK]
                    dv_acc[x - 1] = dv_acc[x - 1] + dv[:QBLK]
                dk_acc[x] = dk_acc[x] + dk[QBLK:]
                dv_acc[x] = dv_acc[x] + dv[QBLK:]
            for kb in range(qb):
                dk_ref[lay.idx(kb, r)] = dk_acc[kb]
                dv_ref[lay.idx(kb, r)] = dv_acc[kb]

        lay.per_residue(residue)

    cur, prev = lay.cur(lambda j: nchunk - 1 - j), lay.prev(lambda j: nchunk - 1 - j)
    carried = pltpu.VMEM((lay.nres * QBLK, LANES), F32)
    q3, k3, v3 = lay.view(qn), lay.view(kn), lay.view(v)
    outs, got = _pcall(
        body, name, lay.grid, [cur, cur, prev, cur, prev, cur, cur, cur, cur, cur], [cur, cur, cur],
        [lay.out_shape(AW)] * 3, ("arbitrary",) * 3,
        (q3, k3, k3, v3, v3, lay.view(dycat), lay.view(lse_a), lay.view(lse_b), lay.view(dl_a), lay.view(dl_b)),
        carry, scratch=[carried, carried])
    return [lay.unview(t) for t in outs], got


def _shift_down(x, halo_prev, k, row):
    tm = x.shape[0]
    tail = jnp.concatenate([pltpu.roll(halo_prev, k, 0), jnp.zeros((tm - SUBLANES, x.shape[1]), x.dtype)], axis=0)
    return jnp.where(row < k, tail, pltpu.roll(x, k, 0))


def _shift_up(x, halo_next, k, row):
    tm = x.shape[0]
    head = jnp.concatenate([jnp.zeros((tm - SUBLANES, x.shape[1]), x.dtype), pltpu.roll(halo_next, SUBLANES - k, 0)], axis=0)
    return jnp.where(row >= tm - k, head, pltpu.roll(x, tm - k, 0))


def _conv_fwd(cu, halo_cu, cw_ref, row):
    u1 = _shift_down(cu, halo_cu, 1, row)
    u2 = _shift_down(cu, halo_cu, 2, row)
    cv = cw_ref[0:1, :] * u2 + cw_ref[1:2, :] * u1 + cw_ref[2:3, :] * cu + cw_ref[3:4, :]
    return cv, u1, u2


def combine_conv(os_, lses_a, lses_b, proj, cw, name, carry=None):
    S = proj.shape[0]
    tm = _row_tile(S, 512)
    hb = tm // SUBLANES

    def body(o1, o2, o3, a1, a2, a3, b1, b2, b3, pc_ref, ph_ref, cw_ref, ycat_ref, la_ref, lb_ref):
        i = pl.program_id(0)
        m_a = _head_masks(tm)
        for p in range(AW // LANES):
            cs = slice(p * LANES, (p + 1) * LANES)
            tot = []
            for srcs, dst in (((a1, a2, a3), la_ref), ((b1, b2, b3), lb_ref)):
                ls = [l[:, cs] for l in srcs]
                mx = jnp.maximum(jnp.maximum(ls[0], ls[1]), ls[2])
                t = mx + jnp.log(jnp.exp(ls[0] - mx) + jnp.exp(ls[1] - mx) + jnp.exp(ls[2] - mx))
                dst[:, cs] = t
                tot.append((ls, t))
            acc = jnp.zeros((tm, LANES), F32)
            for r, o in enumerate((o1, o2, o3)):
                w = jnp.where(m_a, jnp.exp(tot[0][0][r] - tot[0][1]), jnp.exp(tot[1][0][r] - tot[1][1]))
                acc = acc + w * o[:, cs]
            ycat_ref[:, cs] = acc.astype(ycat_ref.dtype)
        row = lax.broadcasted_iota(jnp.int32, (tm, CW), 0)
        gb, gc, u = pc_ref[:, 0:CW], pc_ref[:, CW:2 * CW], pc_ref[:, 2 * CW:3 * CW]
        halo_cu = jnp.where(i > 0, ph_ref[:, CW:2 * CW] * ph_ref[:, 2 * CW:3 * CW], 0.0)
        cv, _, _ = _conv_fwd(gc * u, halo_cu, cw_ref, row)
        ycat_ref[:, AW:AW + CW] = (gb * cv).astype(ycat_ref.dtype)

    ot = pl.BlockSpec((tm, AW), lambda i: (i, 0))
    return _pcall(
        body, name, (S // tm,),
        [ot] * 9 + [pl.BlockSpec((tm, 3 * CW), lambda i: (i, 1)),
                    pl.BlockSpec((SUBLANES, 3 * CW), lambda i: (jnp.maximum(i * hb - 1, 0), 1)),
                    pl.BlockSpec((SUBLANES, CW), lambda i: (0, 0))],
        [pl.BlockSpec((tm, D), lambda i: (i, 0)), ot, ot],
        [jax.ShapeDtypeStruct((S, D), ACT_DTYPE), jax.ShapeDtypeStruct((S, AW), F32),
         jax.ShapeDtypeStruct((S, AW), F32)],
        ("arbitrary",), (*os_, *lses_a, *lses_b, proj, proj, cw), carry)


def out_proj(ycat, x, vec, wout, name):
    S = x.shape[0]
    tm = _row_tile(S, 512)

    def body(yc_ref, x_ref, vec_ref, w_ref, xn_ref, y_ref):
        y = jnp.dot(yc_ref[...].astype(MXU_DTYPE), w_ref[...], preferred_element_type=F32)
        xn_ref[...] = x_ref[...] + vec_ref[3:4, :] * y
        y_ref[...] = y.astype(y_ref.dtype)

    t = pl.BlockSpec((tm, D), lambda i: (i, 0))
    return pl.pallas_call(
        body, name=name, grid=(S // tm,),
        in_specs=[t, t, pl.BlockSpec((SUBLANES, D), lambda i: (0, 0)),
                  pl.BlockSpec((D, D), lambda i: (0, 0))],
        out_specs=[t, t],
        out_shape=[jax.ShapeDtypeStruct((S, D), F32), jax.ShapeDtypeStruct((S, D), ACT_DTYPE)],
        compiler_params=_params(("arbitrary",)),
    )(ycat, x, vec, wout)


def out_proj_bwd(dxo, y, ycat, vec, wout, name, carry=None):
    S = dxo.shape[0]
    tm = _row_tile(S, 512)

    def body(dxo_ref, y_ref, yc_ref, vec_ref, w_ref, dyb_ref, dyc_ref, da_ref, db_ref, sums_ref):
        dxo = dxo_ref[...]
        dgate = jnp.sum(dxo * y_ref[...].astype(F32), axis=0, keepdims=True)
        dy = (vec_ref[3:4, :] * dxo).astype(MXU_DTYPE)
        dyb_ref[...] = dy
        dyc_ref[...] = lax.dot_general(dy, w_ref[...], NT_DIMS, preferred_element_type=F32)
        m_a = _head_masks(tm)
        for p in range(AW // LANES):
            cs = slice(p * LANES, (p + 1) * LANES)
            s_a, s_b = _pair_stat(dyc_ref[:, cs] * yc_ref[:, cs].astype(F32), m_a)
            da_ref[:, cs] = jnp.broadcast_to(s_a, (tm, LANES))
            db_ref[:, cs] = jnp.broadcast_to(s_b, (tm, LANES))
        _acc_rows(sums_ref, pl.program_id(0) == 0, (dgate,))

    t = pl.BlockSpec((tm, D), lambda i: (i, 0))
    at = pl.BlockSpec((tm, AW), lambda i: (i, 0))
    return _pcall(
        body, name, (S // tm,),
        [t, t, t, pl.BlockSpec((SUBLANES, D), lambda i: (0, 0)), pl.BlockSpec((D, D), lambda i: (0, 0))],
        [t, t, at, at, pl.BlockSpec((SUBLANES, D), lambda i: (0, 0))],
        [jax.ShapeDtypeStruct((S, D), MXU_DTYPE), jax.ShapeDtypeStruct((S, D), F32),
         jax.ShapeDtypeStruct((S, AW), F32), jax.ShapeDtypeStruct((S, AW), F32),
         jax.ShapeDtypeStruct((SUBLANES, D), F32)],
        ("arbitrary",), (dxo, y, ycat, vec, wout), carry)


def mixer_mid_bwd(dqs, dks, dvs, proj, dycat, gvec, cw, name, carry=None):
    S = proj.shape[0]
    tm = _row_tile(S, 256)
    hb = tm // SUBLANES
    nsl = S // SUBLANES
    ntile = S // tm

    def body(dq1, dq2, dq3, dk1, dk2, dk3, dv1, dv2, dv3, pr_ref, pp_ref, pn_ref, dyc_ref, dyn_ref,
             g_ref, cw_ref, dp_ref, sums_ref):
        i = pl.program_id(0)
        m_a = _head_masks(tm)
        gsum = []
        for which, parts in ((0, (dq1, dq2, dq3)), (1, (dk1, dk2, dk3))):
            acc_g = []
            for p in range(AW // LANES):
                lo = which * AW + p * LANES
                cs = slice(p * LANES, (p + 1) * LANES)
                xp = pr_ref[:, lo:lo + LANES]
                s_a, s_b = _pair_stat(xp * xp, m_a)
                rr = jnp.where(m_a, lax.rsqrt(s_a * (1.0 / HD) + EPS), lax.rsqrt(s_b * (1.0 / HD) + EPS))
                xh = xp * rr
                dn = parts[0][:, cs] + parts[1][:, cs] + parts[2][:, cs]
                acc_g.append(jnp.sum(dn * xh, axis=0, keepdims=True))
                t = dn * g_ref[which:which + 1, cs]
                t_a, t_b = _pair_stat(t * xh, m_a)
                mean = jnp.where(m_a, t_a, t_b) * (1.0 / HD)
                dp_ref[:, lo:lo + LANES] = (rr * (t - xh * mean)).astype(dp_ref.dtype)
            gsum.append(jnp.concatenate(acc_g, axis=1))
        dp_ref[:, 2 * AW:3 * AW] = (dv1[...] + dv2[...] + dv3[...]).astype(dp_ref.dtype)
        row = lax.broadcasted_iota(jnp.int32, (tm, CW), 0)
        base = 3 * AW
        gb, gc, u = pr_ref[:, base:base + CW], pr_ref[:, base + CW:base + 2 * CW], pr_ref[:, base + 2 * CW:base + 3 * CW]
        cu = gc * u
        halo_cu = jnp.where(i > 0, pp_ref[:, CW:2 * CW] * pp_ref[:, 2 * CW:3 * CW], 0.0)
        cv, u1, u2 = _conv_fwd(cu, halo_cu, cw_ref, row)
        dyc = dyc_ref[...]
        dp_ref[:, base:base + CW] = (dyc * cv).astype(dp_ref.dtype)
        dcv = dyc * gb
        halo_dcv = jnp.where(i < ntile - 1, dyn_ref[...] * pn_ref[:, 0:CW], 0.0)
        d1 = _shift_up(dcv, halo_dcv, 1, row)
        d2 = _shift_up(dcv, halo_dcv, 2, row)
        dcu = cw_ref[2:3, :] * dcv + cw_ref[1:2, :] * d1 + cw_ref[0:1, :] * d2
        dp_ref[:, base + CW:base + 2 * CW] = (dcu * u).astype(dp_ref.dtype)
        dp_ref[:, base + 2 * CW:base + 3 * CW] = (dcu * gc).astype(dp_ref.dtype)
        rows = (gsum[0], gsum[1],
                jnp.sum(dcv * u2, axis=0, keepdims=True), jnp.sum(dcv * u1, axis=0, keepdims=True),
                jnp.sum(dcv * cu, axis=0, keepdims=True), jnp.sum(dcv, axis=0, keepdims=True))
        _acc_rows(sums_ref, i == 0, rows)

    at = pl.BlockSpec((tm, AW), lambda i: (i, 0))
    return _pcall(
        body, name, (ntile,),
        [at] * 9 + [
            pl.BlockSpec((tm, INC), lambda i: (i, 0)),
            pl.BlockSpec((SUBLANES, 3 * CW), lambda i: (jnp.maximum(i * hb - 1, 0), 1)),
            pl.BlockSpec((SUBLANES, 3 * CW), lambda i: (jnp.minimum((i + 1) * hb, nsl - 1), 1)),
            pl.BlockSpec((tm, CW), lambda i: (i, 1)),
            pl.BlockSpec((SUBLANES, CW), lambda i: (jnp.minimum((i + 1) * hb, nsl - 1), 1)),
            pl.BlockSpec((SUBLANES, AW), lambda i: (0, 0)),
            pl.BlockSpec((SUBLANES, CW), lambda i: (0, 0))],
        [pl.BlockSpec((tm, INC), lambda i: (i, 0)), pl.BlockSpec((SUBLANES, AW), lambda i: (0, 0))],
        [jax.ShapeDtypeStruct((S, INC), MXU_DTYPE), jax.ShapeDtypeStruct((SUBLANES, AW), F32)],
        ("arbitrary",), (*dqs, *dks, *dvs, proj, proj, proj, dycat, dycat, gvec, cw), carry)


def mixer_in_bwd(dxo, x, dproj, vec, winp, name, carry=None):
    S = x.shape[0]
    tm = _row_tile(S, 512)
    pc = INC // NCHIP

    def body(dxo_ref, x_ref, dp_ref, vec_ref, w_ref, dxi_ref, sums_ref):
        xhat, r, gain, ng, sc, _, _ = _ada(x_ref[...], vec_ref)
        dh = jnp.zeros((tm, D), F32)
        for j in range(NCHIP):
            dh = dh + lax.dot_general(dp_ref[:, j * pc:(j + 1) * pc], w_ref[j], NT_DIMS, preferred_element_type=F32)
        dx, dshift, dscale, dng = _ada_bwd(dh, xhat, r, gain, ng, sc)
        dxi_ref[...] = dxo_ref[...] + dx
        _acc_rows(sums_ref, pl.program_id(0) == 0, (dshift, dscale, dng))

    t = pl.BlockSpec((tm, D), lambda i: (i, 0))
    return _pcall(
        body, name, (S // tm,),
        [t, t, pl.BlockSpec((tm, INC), lambda i: (i, 0)),
         pl.BlockSpec((SUBLANES, D), lambda i: (0, 0)),
         pl.BlockSpec((NCHIP, D, pc), lambda i: (0, 0, 0), pipeline_mode=pl.Buffered(1))],
        [t, pl.BlockSpec((SUBLANES, D), lambda i: (0, 0))],
        [jax.ShapeDtypeStruct((S, D), F32), jax.ShapeDtypeStruct((SUBLANES, D), F32)],
        ("arbitrary",), (dxo, x, dproj, vec, winp), carry)


def loss_head(xf, target, name):
    S = xf.shape[0]
    tm = _row_tile(S, 1024)

    def body(x_ref, t_ref, dy_ref, l_ref):
        diff = x_ref[...] - t_ref[...]
        dy_ref[...] = diff * (1.0 / D)
        part = jnp.sum(jnp.sum(diff * diff, axis=0, keepdims=True), axis=1, keepdims=True) * (0.5 / D)

        @pl.when(pl.program_id(0) == 0)
        def _():
            l_ref[...] = jnp.zeros_like(l_ref)
        l_ref[...] += jnp.broadcast_to(part, l_ref.shape)

    t = pl.BlockSpec((tm, D), lambda i: (i, 0))
    return pl.pallas_call(
        body, name=name, grid=(S // tm,),
        in_specs=[t, t],
        out_specs=[t, pl.BlockSpec((SUBLANES, LANES), lambda i: (0, 0))],
        out_shape=[jax.ShapeDtypeStruct((S, D), F32), jax.ShapeDtypeStruct((SUBLANES, LANES), F32)],
        compiler_params=_params(("arbitrary",)),
    )(xf, target)


def _vec(mod_l, ng_l, i):
    m = mod_l.reshape(3, 3, D)
    rows = jnp.stack([ng_l[i], m[i, 1], m[i, 0], m[i, 2]])
    return jnp.concatenate([rows, jnp.zeros((SUBLANES - 4, D), F32)], axis=0)


def local_step(x, target, mods, ngs, gvecs, cws, shards, w_first, cflag):
    saved = []
    weights = [dict(w1=[None, None], w2=[None, None]) for _ in range(2)]
    weights[0]["w1"][0], weights[0]["w2"][0] = w_first[0], w_first[1].reshape(DFF, D)
    h = x
    for l in range(2):
        w, sh = weights[l], shards[l]
        nxt = shards[l + 1] if l == 0 else None
        vecs = [_vec(mods[l], ngs[l], i) for i in range(3)]
        x0 = h
        (x1, a0, f0), (win, wout) = ffn_fwd(x0, vecs[0], w["w1"][0], w["w2"][0], 0.5, f"ffn_fwd_l{l}a",
                                            carry=Carry("gather", [sh["win"], sh["wout"]]))
        w["win"], w["wout"] = win, wout.reshape(D, D)
        proj, h1b, qn, kn, v = mixer_in(x1, vecs[1], w["win"], gvecs[l], f"mixer_in_l{l}")
        os_, lses_a, lses_b = [], [], []
        for d in DILATIONS:
            carry = {1: Carry("gather", [sh["w2"][1]]), 16: Carry("gather", [sh["w1"][1]])}.get(d)
            (o, la, lb), got = attn_fwd(qn, kn, v, d, f"attn_fwd_l{l}_d{d}", carry=carry)
            if d == 1:
                w["w2"][1] = got[0].reshape(DFF, D)
            if d == 16:
                w["w1"][1] = got[0]
            os_.append(o)
            lses_a.append(la)
            lses_b.append(lb)
        (ycat, *lse), got = combine_conv(os_, lses_a, lses_b, proj, cws[l], f"combine_conv_l{l}",
                                         carry=Carry("gather", [nxt["w2"][0]]) if nxt else None)
        if nxt:
            weights[1]["w2"][0] = got[0].reshape(DFF, D)
        x2, y = out_proj(ycat, x1, vecs[1], w["wout"], f"out_proj_l{l}")
        (x3, a2, f2), got = ffn_fwd(x2, vecs[2], w["w1"][1], w["w2"][1], 0.5, f"ffn_fwd_l{l}b",
                                    carry=Carry("gather", [nxt["w1"][0]]) if nxt else None)
        if nxt:
            weights[1]["w1"][0] = got[0]
        saved.append(dict(vecs=vecs, x0=x0, a0=a0, f0=f0, x1=x1, proj=proj, h1b=h1b, qn=qn, kn=kn, v=v,
                          ycat=ycat, lse=lse, y=y, x2=x2, a2=a2, f2=f2))
        h = x3
    dx, loss_blk = loss_head(h, target, "loss_head")
    sums, totals, g_prev = [None, None], [None, None], None
    w2r = DFF // NCHIP
    for l in (1, 0):
        w, s = weights[l], saved[l]
        vecs = s["vecs"]
        ride = g_prev is not None
        own = l == 0
        mine, other = [None] * 6, [None] * 6

        def half_sum(group, recv, k0):
            return [add_half(g, r, cflag, f"add_sibling_l{l}_{k0 + j}") for j, (g, r) in enumerate(zip(group, recv))]

        def chip_sum(landed, k0):
            return [sum_chips(t, f"sum_chips_l{l}_{k0 + j}") for j, t in enumerate(landed)]

        (dx, hb, dfb, act, da, sums2), got = ffn_bwd(
            dx, s["x2"], s["a2"], s["f2"], vecs[2], w["w1"][1], w["w2"][1], 0.5, f"ffn_bwd_l{l}b",
            carry=Carry("swap_halves", g_prev) if ride else None)
        dw1b, _ = wgrad(hb, da, D, HALF, f"wgrad_w1_l{l}b")
        dw2b, _ = wgrad(act, dfb, HALF, D, f"wgrad_w2_l{l}b")
        if ride:
            wire = [add_half(g_prev[k], got[k], cflag, f"add_sibling_l{l + 1}_{k}") for k in range(6)]
        g_ffn_b = [dw1b, dw2b.reshape(NCHIP, w2r, D)]
        (dyb, dycat, dl_a, dl_b, sums_o), got = out_proj_bwd(
            dx, s["y"], s["ycat"], vecs[1], w["wout"], f"out_proj_bwd_l{l}",
            carry=Carry("swap_halves", g_ffn_b) if own else None)
        dwout, _ = wgrad(s["ycat"].astype(MXU_DTYPE), dyb, D // 2, D, f"wgrad_wout_l{l}")
        if own:
            wire_ffn_b = half_sum(g_ffn_b, got, 4)
        dqs, dks, dvs, landed = [], [], [], {}
        for d in DILATIONS:
            carry = None
            if ride and d == 1:
                carry = Carry("scatter", wire[3:])
            if ride and d == 16:
                carry = Carry("scatter", wire[:3])
            if own and d == 4:
                carry = Carry("scatter", wire_ffn_b)
            (dq, dk, dv), landed[d] = attn_bwd(s["qn"], s["kn"], s["v"], dycat, s["lse"][0], s["lse"][1], dl_a, dl_b,
                                               d, f"attn_bwd_l{l}_d{d}", carry=carry)
            dqs.append(dq)
            dks.append(dk)
            dvs.append(dv)
        if ride:
            tot = [sum_chips(t, f"sum_chips_l{l + 1}_{k}") for k, t in enumerate(list(landed[16]) + list(landed[1]))]
        if own:
            mine[4:6] = chip_sum(landed[4], 4)
        (dproj, sums_m), got = mixer_mid_bwd(dqs, dks, dvs, s["proj"], dycat, gvecs[l], cws[l], f"mixer_mid_bwd_l{l}",
                                             carry=Carry("swap", mine[4:6]) if own else None)
        if own:
            other[4:6] = list(got)
        dwin, _ = wgrad(s["h1b"], dproj, D, INC // NCHIP, f"wgrad_win_l{l}")
        g_mixer = [dwin, dwout.reshape(NCHIP, D // NCHIP, D)]
        (dx, sums1), got = mixer_in_bwd(dx, s["x1"], dproj, vecs[1], w["win"], f"mixer_in_bwd_l{l}",
                                        carry=Carry("swap_halves", g_mixer) if own else None)
        if own:
            wire_mixer = half_sum(g_mixer, got, 2)
        (dx, hb, dfb, act, da, sums0), got = ffn_bwd(
            dx, s["x0"], s["a0"], s["f0"], vecs[0], w["w1"][0], w["w2"][0], 0.5, f"ffn_bwd_l{l}a",
            carry=Carry("swap", tot) if ride else None)
        if ride:
            totals[l + 1] = (tot, list(got))
        dw1a, got = wgrad(hb, da, D, HALF, f"wgrad_w1_l{l}a", carry=Carry("scatter", wire_mixer) if own else None)
        if own:
            mine[2:4] = chip_sum(got, 2)
        dw2a, got = wgrad(act, dfb, HALF, D, f"wgrad_w2_l{l}a", carry=Carry("swap", mine[2:4]) if own else None)
        g_ffn_a = [dw1a, dw2a.reshape(NCHIP, w2r, D)]
        if own:
            other[2:4] = list(got)
            wire_ffn_a = half_sum(g_ffn_a, run_carry(Carry("swap_halves", g_ffn_a), "swap_halves_tail"), 0)
            mine[0:2] = chip_sum(run_carry(Carry("scatter", wire_ffn_a), "scatter_grads_tail"), 0)
            other[0:2] = list(run_carry(Carry("swap", mine[0:2]), "swap_totals_tail"))
            totals[l] = (mine, other)
        g_prev = g_ffn_a + g_mixer + g_ffn_b
        sums[l] = (sums0, sums1, sums_o, sums2, sums_m)
    return loss_blk, dx, totals, sums


def small_all_gather(blk, name):
    m_per, n = blk.shape

    def body(x_ref, out_ref, send_sems, recv_sems, local_sem):
        x, y, c = _here()
        me, sibling = (x, y, c), (x, y, 1 - c)
        chips = [(1 - x, y), (x, 1 - y), (1 - x, 1 - y)]

        def rows(px, py, pc):
            return out_ref.at[pl.ds((4 * px + 2 * py + pc) * m_per, m_per), :]

        def copy(k, block, to, src=None):
            return pltpu.make_async_remote_copy(
                src_ref=rows(*block) if src is None else src, dst_ref=rows(*block),
                send_sem=send_sems.at[k], recv_sem=recv_sems.at[k], device_id=to, device_id_type=MESH)

        mine = pltpu.make_async_copy(x_ref, rows(*me), local_sem)
        mine.start()
        first = [copy(0, me, sibling, src=x_ref)]
        first += [copy(1 + j, me, (*chip, c), src=x_ref) for j, chip in enumerate(chips)]
        for cp in first:
            cp.start()
        passed = [copy(4 + j, (*chip, c), sibling) for j, chip in enumerate(chips)]
        for j, chip in enumerate(chips):
            copy(1 + j, (*chip, c), me).wait_recv()
            passed[j].start()
        copy(0, sibling, me).wait_recv()
        for j, chip in enumerate(chips):
            copy(4 + j, (*chip, 1 - c), me).wait_recv()
        for cp in first + passed:
            cp.wait_send()
        mine.wait()

    return pl.pallas_call(
        body, name=name,
        out_shape=jax.ShapeDtypeStruct((NDEV * m_per, n), blk.dtype),
        in_specs=[pl.BlockSpec(memory_space=pltpu.VMEM)],
        out_specs=pl.BlockSpec(memory_space=pltpu.VMEM),
        scratch_shapes=[pltpu.SemaphoreType.DMA((7,)), pltpu.SemaphoreType.DMA((7,)), pltpu.SemaphoreType.DMA],
        compiler_params=pltpu.CompilerParams(vmem_limit_bytes=VMEM_LIMIT),
    )(blk)


EW_BLOCK_BYTES = 1 << 20


def _ew_rows(rows, cols):
    want = max(16, EW_BLOCK_BYTES // (4 * cols))
    best = None
    for t in range(16, rows + 1, 16):
        if rows % t == 0 and t <= want:
            best = t
    return best if best is not None else rows


def add_half(g, recv, cflag, name):
    pieces, r, cols = g.shape
    r2 = r // 2
    tr = _ew_rows(r2, cols)
    nt = r2 // tr

    def body(c_ref, g_ref, r_ref, o_ref):
        o_ref[...] = (g_ref[...] + r_ref[...]).astype(o_ref.dtype)

    half = pl.BlockSpec((None, tr, cols), lambda j, i, c_ref: (j, i, 0))
    return pl.pallas_call(
        body, name=name,
        grid_spec=pltpu.PrefetchScalarGridSpec(
            num_scalar_prefetch=1, grid=(pieces, nt),
            in_specs=[pl.BlockSpec((None, tr, cols), lambda j, i, c_ref: (j, c_ref[0] * nt + i, 0)), half],
            out_specs=half),
        out_shape=jax.ShapeDtypeStruct((pieces, r2, cols), WIRE_DTYPE),
        compiler_params=_params(("arbitrary", "arbitrary")),
    )(cflag, g, recv)


def sum_chips(recv, name):
    _, r, cols = recv.shape
    tr = _ew_rows(r, cols)

    def body(r_ref, o_ref):
        acc = r_ref[0].astype(F32)
        for k in range(1, NCHIP):
            acc = acc + r_ref[k].astype(F32)
        o_ref[...] = acc

    return pl.pallas_call(
        body, name=name, grid=(r // tr,),
        in_specs=[pl.BlockSpec((NCHIP, tr, cols), lambda i: (0, i, 0))],
        out_specs=pl.BlockSpec((tr, cols), lambda i: (i, 0)),
        out_shape=jax.ShapeDtypeStruct((r, cols), F32),
        compiler_params=_params(("arbitrary",)),
    )(recv)


def sum_devices(rows8, name):
    def body(r_ref, o_ref):
        acc = r_ref[0:1, :]
        for k in range(1, NDEV):
            acc = acc + r_ref[k:k + 1, :]
        o_ref[...] = jnp.broadcast_to(acc, o_ref.shape)

    return pl.pallas_call(
        body, name=name, out_shape=jax.ShapeDtypeStruct(rows8.shape, F32),
        in_specs=[pl.BlockSpec(memory_space=pltpu.VMEM)], out_specs=pl.BlockSpec(memory_space=pltpu.VMEM),
        compiler_params=pltpu.CompilerParams(vmem_limit_bytes=VMEM_LIMIT),
    )(rows8)


def adamw(w, m, v, srcs, cflag, name, halves=False):
    planes, r, cols = w.shape
    rh = r // 2 if halves else r
    tr = _ew_rows(rh, cols)
    nth = rh // tr
    flat = [a for s in srcs for a in (s if halves else (s,))]
    ns = len(flat)
    per = ns // planes

    def body(c_ref, w_ref, m_ref, v_ref, *rest):
        s_refs, (g_ref, d_ref, mo_ref, vo_ref) = rest[:ns], rest[ns:]
        p, i = pl.program_id(0), pl.program_id(1)
        if halves:
            mine = jnp.logical_not(jnp.logical_xor(i >= nth, c_ref[0] == 1))
            blocks = [jnp.where(mine, s_refs[2 * k][...], s_refs[2 * k + 1][...]) for k in range(planes)]
        else:
            blocks = [s[...] for s in s_refs]
        g = blocks[0]
        for k in range(1, planes):
            g = jnp.where(p == k, blocks[k], g)
        g_ref[...] = g
        m_new = ADAM_B1 * m_ref[...] + (1.0 - ADAM_B1) * g
        v_new = ADAM_B2 * v_ref[...] + (1.0 - ADAM_B2) * (g * g)
        mo_ref[...] = m_new
        vo_ref[...] = v_new
        m_hat = m_new / (1.0 - ADAM_B1 ** ADAM_STEP)
        v_hat = v_new / (1.0 - ADAM_B2 ** ADAM_STEP)
        d_ref[...] = -ADAM_LR * (m_hat / (jnp.sqrt(v_hat) + ADAM_EPS) + ADAM_WD * w_ref[...])

    pt = pl.BlockSpec((None, tr, cols), lambda p, i: (p, i, 0))
    st = [pl.BlockSpec((tr, cols), functools.partial(lambda k, p, i: (jnp.where(p == k, i % nth, 0), 0), j // per))
          for j in range(ns)]
    return pl.pallas_call(
        body, name=name, grid=(planes, r // tr),
        in_specs=[pl.BlockSpec(memory_space=pltpu.SMEM), pt, pt, pt] + st,
        out_specs=[pt] * 4,
        out_shape=[jax.ShapeDtypeStruct(w.shape, F32)] * 4,
        compiler_params=_params(("arbitrary", "arbitrary")),
    )(cflag, w, m, v, *flat)


ADA_COLS = 9 * D // NCHIP


def mod_fwd(c_all, w_ada, b_shard, name):
    def body(c_ref, w_ref, b_ref, o_ref):
        cc = c_ref[...]
        sc = cc * jax.nn.sigmoid(cc)
        o_ref[...] = jnp.dot(sc, w_ref[...], preferred_element_type=F32,
                             precision=lax.Precision.HIGHEST) + b_ref[...]

    return pl.pallas_call(
        body, name=name, grid=(2,),
        in_specs=[pl.BlockSpec((NDEV, D), lambda l: (0, 0)),
                  pl.BlockSpec((None, D, ADA_COLS), lambda l: (l, 0, 0)),
                  pl.BlockSpec((None, 1, ADA_COLS), lambda l: (l, 0, 0))],
        out_specs=pl.BlockSpec((None, NDEV, ADA_COLS), lambda l: (l, 0, 0)),
        out_shape=jax.ShapeDtypeStruct((2, NDEV, ADA_COLS), F32),
        compiler_params=_params(("arbitrary",)),
    )(c_all, w_ada, b_shard.reshape(2, 1, ADA_COLS))


def wada_grad(c_all_t, dmod, name):
    ct = ADA_COLS // 3

    def body(c_ref, d_ref, o_ref):
        cc = c_ref[...]
        sc = cc * jax.nn.sigmoid(cc)
        acc = sc[:, 0:1] * d_ref[0:1, :]
        for b in range(1, NDEV):
            acc = acc + sc[:, b:b + 1] * d_ref[b:b + 1, :]
        o_ref[...] = acc

    return pl.pallas_call(
        body, name=name, grid=(2, 3),
        in_specs=[pl.BlockSpec((D, LANES), lambda l, j: (0, 0)),
                  pl.BlockSpec((None, NDEV, ct), lambda l, j: (l, 0, j))],
        out_specs=pl.BlockSpec((None, D, ct), lambda l, j: (l, 0, j)),
        out_shape=jax.ShapeDtypeStruct((2, D, ADA_COLS), F32),
        compiler_params=_params(("arbitrary", "arbitrary")),
    )(c_all_t, dmod)


def _pad_rows(row, rows=SUBLANES):
    return jnp.concatenate([row[None, :], jnp.zeros((rows - 1, row.shape[0]), row.dtype)], axis=0)


def kernel(x, c, w_ada, b_ada, norm_g, w_in, q_norm_g, k_norm_g, conv_w, conv_b, w_out, ffn_w1, ffn_w2, loss_target, m_w_ada, m_b_ada, m_norm_g, m_w_in, m_q_norm_g, m_k_norm_g, m_conv_w, m_conv_b, m_w_out, m_ffn_w1, m_ffn_w2, v_w_ada, v_b_ada, v_norm_g, v_w_in, v_q_norm_g, v_k_norm_g, v_conv_w, v_conv_b, v_w_out, v_ffn_w1, v_ffn_w2):
    ix, iy, ic = lax.axis_index("x"), lax.axis_index("y"), lax.axis_index("c")
    chip = 2 * ix + iy
    dev = 2 * chip + ic
    cflag = jnp.reshape(ic, (1,)).astype(jnp.int32)
    ngw = norm_g.shape[-1]
    cww = conv_w.shape[-1]

    pack = jnp.concatenate([c[0], norm_g.reshape(-1), conv_w.reshape(-1)])
    got = small_all_gather(_pad_rows(pack), "gather_c_normg_convw")[::SUBLANES]
    c_all = got[:, :D]
    per_chip = got[::2]
    ng_full = jnp.concatenate([per_chip[j, D:D + 6 * ngw].reshape(2, 3, ngw) for j in range(NCHIP)], axis=-1)
    cw_full = jnp.concatenate([per_chip[j, D + 6 * ngw:].reshape(2, 3, cww) for j in range(NCHIP)], axis=-1)

    b_shard = lax.dynamic_slice_in_dim(b_ada, chip * ADA_COLS, ADA_COLS, axis=1)
    mod_blk = mod_fwd(c_all, w_ada, b_shard, "mod_fwd").reshape(2 * NDEV, ADA_COLS)
    mod_all = small_all_gather(mod_blk, "gather_mod").reshape(NDEV, 2, NDEV, ADA_COLS)[::2]
    mod_mine = lax.dynamic_index_in_dim(mod_all, dev, axis=2, keepdims=False)
    mods = [mod_mine[:, l, :].reshape(-1) for l in range(2)]

    shards, gvecs, cws = [], [], []
    for l in range(2):
        shards.append(dict(w1=[ffn_w1[l, i].astype(MXU_DTYPE) for i in range(2)],
                           w2=[ffn_w2[l, i].astype(MXU_DTYPE) for i in range(2)],
                           win=w_in[l].astype(MXU_DTYPE), wout=w_out[l].astype(MXU_DTYPE)))
        gv = jnp.stack([jnp.tile(q_norm_g[l], AW // HD), jnp.tile(k_norm_g[l], AW // HD)])
        gvecs.append(jnp.concatenate([gv, jnp.zeros((SUBLANES - 2, AW), F32)], axis=0))
        cws.append(jnp.concatenate([cw_full[l], conv_b[l][None, :], jnp.zeros((SUBLANES - 4, CW), F32)], axis=0))
    w_first = run_carry(Carry("gather", [shards[0]["w1"][0], shards[0]["w2"][0]]), "gather_first_ffn")

    loss_blk, dx, totals, sums = local_step(x[0], loss_target[0], mods, [ng_full[0], ng_full[1]], gvecs, cws,
                                            shards, w_first, cflag)
    loss = lax.psum(loss_blk[0, 0], ("x", "y", "c"))

    dmods, dngs, dqg, dkg, dcw, dcb = [], [], [], [], [], []
    for l in range(2):
        s0, s1, so, s2, sm = sums[l]
        dmods.append(jnp.concatenate([s0[0], s0[1], s0[3], s1[0], s1[1], so[0], s2[0], s2[1], s2[3]]))
        dngs.append(jnp.concatenate([s0[2], s1[2], s2[2]]))
        dqg.append(sm[0].reshape(AW // HD, HD).sum(0))
        dkg.append(sm[1].reshape(AW // HD, HD).sum(0))
        dcw.append(sm[2:5].reshape(-1))
        dcb.append(sm[5])
    small = jnp.concatenate(dmods + dngs + dqg + dkg + dcw + dcb)
    small_all = small_all_gather(_pad_rows(small), "gather_small_grads")[::SUBLANES]
    nm = 9 * D
    dmod_all = small_all[:, :2 * nm].reshape(NDEV, 2, NCHIP, ADA_COLS)
    dmod_mine = lax.dynamic_index_in_dim(dmod_all, chip, axis=2, keepdims=False).transpose(1, 0, 2)
    tot = sum_devices(small_all, "sum_small_grads")[0]
    o = 2 * nm
    g_b_ada = tot[:o].reshape(2, nm)
    g_norm_g = lax.dynamic_slice_in_dim(tot[o:o + 6 * D].reshape(2, 3, D), chip * ngw, ngw, axis=2)
    o += 6 * D
    g_qg = tot[o:o + 2 * HD].reshape(2, HD)
    o += 2 * HD
    g_kg = tot[o:o + 2 * HD].reshape(2, HD)
    o += 2 * HD
    g_cw = lax.dynamic_slice_in_dim(tot[o:o + 6 * CW].reshape(2, 3, CW), chip * cww, cww, axis=2)
    o += 6 * CW
    g_cb = tot[o:o + 2 * CW].reshape(2, CW)

    c_all_t = jnp.concatenate([c_all.T, jnp.zeros((D, LANES - NDEV), F32)], axis=1)
    g_wada_src = wada_grad(c_all_t, dmod_mine, "wada_grad")

    def halves(k_of_plane):
        return [(totals[l][0][k], totals[l][1][k]) for l, k in k_of_plane]

    r_wada = adamw(w_ada, m_w_ada, v_w_ada, [g_wada_src[0], g_wada_src[1]], cflag, "adamw_w_ada")
    r_win = adamw(w_in, m_w_in, v_w_in, halves([(0, 2), (1, 2)]), cflag, "adamw_w_in", halves=True)
    r_wout = adamw(w_out, m_w_out, v_w_out, halves([(0, 3), (1, 3)]), cflag, "adamw_w_out", halves=True)
    r_w1 = adamw(ffn_w1.reshape(4, D, HALF), m_ffn_w1.reshape(4, D, HALF), v_ffn_w1.reshape(4, D, HALF),
                 halves([(0, 0), (0, 4), (1, 0), (1, 4)]), cflag, "adamw_ffn_w1", halves=True)
    w2r = DFF // NCHIP
    r_w2 = adamw(ffn_w2.reshape(4, w2r, D), m_ffn_w2.reshape(4, w2r, D), v_ffn_w2.reshape(4, w2r, D),
                 halves([(0, 1), (0, 5), (1, 1), (1, 5)]), cflag, "adamw_ffn_w2", halves=True)
    r_w1 = [t.reshape(ffn_w1.shape) for t in r_w1]
    r_w2 = [t.reshape(ffn_w2.shape) for t in r_w2]

    smalls = [("b_ada", b_ada, m_b_ada, v_b_ada, g_b_ada), ("norm_g", norm_g, m_norm_g, v_norm_g, g_norm_g),
              ("q_norm_g", q_norm_g, m_q_norm_g, v_q_norm_g, g_qg), ("k_norm_g", k_norm_g, m_k_norm_g, v_k_norm_g, g_kg),
              ("conv_w", conv_w, m_conv_w, v_conv_w, g_cw), ("conv_b", conv_b, m_conv_b, v_conv_b, g_cb)]
    n_small = sum(t[1].size for t in smalls)
    pad = (-n_small) % (16 * LANES)

    def packed(idx):
        flat = jnp.concatenate([t[idx].reshape(-1) for t in smalls] + [jnp.zeros((pad,), F32)])
        return flat.reshape(-1, LANES)

    r_small = adamw(packed(1)[None], packed(2)[None], packed(3)[None], [packed(4)], cflag, "adamw_small")
    small_out = {}
    o = 0
    for name_, w_, _, _, _ in smalls:
        small_out[name_] = [t.reshape(-1)[o:o + w_.size].reshape(w_.shape) for t in r_small]
        o += w_.size

    res = {"w_ada": r_wada, "w_in": r_win, "w_out": r_wout, "ffn_w1": r_w1, "ffn_w2": r_w2, **small_out}
    order = ["w_ada", "b_ada", "norm_g", "w_in", "q_norm_g", "k_norm_g", "conv_w", "conv_b", "w_out", "ffn_w1", "ffn_w2"]
    outs = [loss, dx[None]]
    for k in range(4):
        outs += [res[nm_][k] for nm_ in order]
    return tuple(outs)
```

```python
import functools

import jax
import jax.numpy as jnp
from jax import lax
from jax.experimental import pallas as pl
from jax.experimental.pallas import tpu as pltpu

F32 = jnp.float32
MXU_DTYPE = jnp.bfloat16
ACT_DTYPE = jnp.bfloat16
WIRE_DTYPE = jnp.bfloat16

D = 1024
HD = 64
AW = 512
CW = 512
DFF = 2816
HALF = DFF // 2
INC = 3 * AW + 3 * CW
NCHIP = 4
NDEV = 8
QBLK = 128
ATTN_QBLOCKS = 4
ATTN_CHUNK_ROWS = 2048
DILATIONS = (1, 4, 16)
EPS = 1e-6
NEG = -1e30
LANES = 128
SUBLANES = 8
VMEM_LIMIT = 56 * 1024 * 1024

ADAM_LR = 0.001
ADAM_B1 = 0.9
ADAM_B2 = 0.999
ADAM_EPS = 1e-08
ADAM_WD = 0.01
ADAM_STEP = 10

NT_DIMS = (((1,), (1,)), ((), ()))
TN_DIMS = (((0,), (0,)), ((), ()))


def _params(sem, vmem=VMEM_LIMIT):
    return pltpu.CompilerParams(dimension_semantics=sem, vmem_limit_bytes=vmem)


def _row_tile(n, want):
    t = min(n, want)
    assert n % t == 0
    return t


def _ada(xt, vec_ref):
    ng, sc, sh, gt = vec_ref[0:1, :], vec_ref[1:2, :], vec_ref[2:3, :], vec_ref[3:4, :]
    r = lax.rsqrt(jnp.mean(xt * xt, axis=-1, keepdims=True) + EPS)
    return xt * r, r, ng * (1.0 + sc), ng, sc, sh, gt


def _ada_bwd(dh, xhat, r, gain, ng, sc):
    dshift = jnp.sum(dh, axis=0, keepdims=True)
    dhx = dh * xhat
    dscale = jnp.sum(dhx, axis=0, keepdims=True) * ng
    dng = jnp.sum(dhx, axis=0, keepdims=True) * (1.0 + sc)
    dxhat = dh * gain
    dx = r * (dxhat - xhat * jnp.mean(dxhat * xhat, axis=-1, keepdims=True))
    return dx, dshift, dscale, dng


def _acc_rows(sums_ref, first, rows):
    @pl.when(first)
    def _():
        sums_ref[...] = jnp.zeros_like(sums_ref)
    for k, row in enumerate(rows):
        sums_ref[k:k + 1, :] += row


MESH = pl.DeviceIdType.MESH
ANY = pl.BlockSpec(memory_space=pl.ANY)


def _here():
    return lax.axis_index("x"), lax.axis_index("y"), lax.axis_index("c")


def _ici_copies(src_refs, dst_refs, send_sems, recv_sems, local_sems, scatter):
    x, y, c = _here()
    my_chip = 2 * x + y
    peers = [(1 - x, y), (x, 1 - y), (1 - x, 1 - y)]
    local, out, inc = [], [], []
    for a, (src, dst) in enumerate(zip(src_refs, dst_refs)):
        local.append(pltpu.make_async_copy(src.at[my_chip] if scatter else src, dst.at[my_chip], local_sems.at[a]))
        for j, (px, py) in enumerate(peers):
            sems = dict(send_sem=send_sems.at[3 * a + j], recv_sem=recv_sems.at[3 * a + j],
                        device_id=(px, py, c), device_id_type=MESH)
            out.append(pltpu.make_async_remote_copy(
                src_ref=src.at[2 * px + py] if scatter else src, dst_ref=dst.at[my_chip], **sems))
            inc.append(pltpu.make_async_remote_copy(
                src_ref=src.at[my_chip] if scatter else src, dst_ref=dst.at[2 * px + py], **sems))
    return local, out, inc


def _swap_copies(src_refs, dst_refs, send_sems, recv_sems, halves):
    x, y, c = _here()
    cps = []
    for k, (src, dst) in enumerate(zip(src_refs, dst_refs)):
        if halves:
            r2 = src.shape[1] // 2
            src = src.at[:, pl.ds((1 - c) * r2, r2), :]
        cps.append(pltpu.make_async_remote_copy(
            src_ref=src, dst_ref=dst, send_sem=send_sems.at[k], recv_sem=recv_sems.at[k],
            device_id=(x, y, 1 - c), device_id_type=MESH))
    return cps


class Carry:
    def __init__(self, kind, srcs):
        self.kind, self.srcs, n = kind, list(srcs), len(srcs)
        if kind == "gather":
            shapes = [(NCHIP,) + s.shape for s in srcs]
        elif kind == "swap_halves":
            shapes = [(s.shape[0], s.shape[1] // 2, s.shape[2]) for s in srcs]
        else:
            shapes = [s.shape for s in srcs]
        self.out_shape = [jax.ShapeDtypeStruct(sh, s.dtype) for sh, s in zip(shapes, srcs)]
        dma = pltpu.SemaphoreType.DMA
        self.sems = [dma((3 * n,)), dma((3 * n,)), dma((n,))] if kind in ("gather", "scatter") else [dma((n,)), dma((n,))]

    def start(self, srcs, dsts, sems):
        if self.kind in ("gather", "scatter"):
            local, out, _ = _ici_copies(srcs, dsts, *sems, self.kind == "scatter")
            for cp in local + out:
                cp.start()
        else:
            for cp in _swap_copies(srcs, dsts, *sems, self.kind == "swap_halves"):
                cp.start()

    def wait(self, srcs, dsts, sems):
        if self.kind in ("gather", "scatter"):
            local, out, inc = _ici_copies(srcs, dsts, *sems, self.kind == "scatter")
            for cp in inc:
                cp.wait_recv()
            for cp in out:
                cp.wait_send()
            for cp in local:
                cp.wait()
        else:
            cps = _swap_copies(srcs, dsts, *sems, self.kind == "swap_halves")
            for cp in cps:
                cp.wait_recv()
            for cp in cps:
                cp.wait_send()


def run_carry(carry, name):
    n = len(carry.srcs)

    def body(*refs):
        srcs, dsts, sems = refs[:n], refs[n:2 * n], refs[2 * n:]
        carry.start(srcs, dsts, sems)
        carry.wait(srcs, dsts, sems)

    return pl.pallas_call(body, name=name, out_shape=carry.out_shape, in_specs=[ANY] * n, out_specs=[ANY] * n,
                          scratch_shapes=carry.sems)(*carry.srcs)


def _pcall(body, name, grid, in_specs, out_specs, out_shape, sem, args, carry=None, scratch=()):
    if carry is None:
        outs = pl.pallas_call(body, name=name, grid=grid, in_specs=in_specs, out_specs=out_specs,
                              out_shape=out_shape, scratch_shapes=list(scratch), compiler_params=_params(sem))(*args)
        return outs, []
    n_in, n_out, nc, ns = len(in_specs), len(out_specs), len(carry.srcs), len(scratch)

    def wrapped(*refs):
        ins, csrc = refs[:n_in], refs[n_in:n_in + nc]
        outs, cdst = refs[n_in + nc:n_in + nc + n_out], refs[n_in + nc + n_out:n_in + 2 * nc + n_out]
        own = refs[n_in + 2 * nc + n_out:n_in + 2 * nc + n_out + ns]
        sems = refs[n_in + 2 * nc + n_out + ns:]
        ids = [pl.program_id(a) for a in range(len(grid))]
        first = functools.reduce(jnp.logical_and, [i == 0 for i in ids])
        last = functools.reduce(jnp.logical_and, [i == g - 1 for i, g in zip(ids, grid)])

        @pl.when(first)
        def _():
            carry.start(csrc, cdst, sems)

        body(*ins, *outs, *own)

        @pl.when(last)
        def _():
            carry.wait(csrc, cdst, sems)

    res = pl.pallas_call(
        wrapped, name=name, grid=grid,
        in_specs=list(in_specs) + [ANY] * nc, out_specs=list(out_specs) + [ANY] * nc,
        out_shape=list(out_shape) + carry.out_shape,
        scratch_shapes=list(scratch) + carry.sems, compiler_params=_params(sem),
    )(*args, *carry.srcs)
    return res[:n_out], res[n_out:]


def ffn_fwd(x, vec, w1p, w2, gs, name, carry=None):
    S = x.shape[0]
    tm = _row_tile(S, 512)

    def body(x_ref, vec_ref, w1_ref, w2_ref, xn_ref, a_ref, f_ref):
        xt = x_ref[...]
        xhat, _, gain, _, _, sh, gt = _ada(xt, vec_ref)
        h = (xhat * gain + sh).astype(MXU_DTYPE)
        f = jnp.zeros((tm, D), F32)
        for hf in range(2):
            g = jnp.dot(h, w1_ref[hf], preferred_element_type=F32)
            up = jnp.dot(h, w1_ref[2 + hf], preferred_element_type=F32)
            a_ref[:, hf * HALF:(hf + 1) * HALF] = g.astype(a_ref.dtype)
            a_ref[:, DFF + hf * HALF:DFF + (hf + 1) * HALF] = up.astype(a_ref.dtype)
            act = (g * jax.nn.sigmoid(g) * up).astype(MXU_DTYPE)
            f = f + jnp.dot(act, w2_ref[hf * HALF:(hf + 1) * HALF, :], preferred_element_type=F32)
        xn_ref[...] = xt + (gs * gt) * f
        f_ref[...] = f.astype(f_ref.dtype)

    return _pcall(
        body, name, (S // tm,),
        [pl.BlockSpec((tm, D), lambda i: (i, 0)),
         pl.BlockSpec((SUBLANES, D), lambda i: (0, 0)),
         pl.BlockSpec((NCHIP, D, HALF), lambda i: (0, 0, 0), pipeline_mode=pl.Buffered(1)),
         pl.BlockSpec((DFF, D), lambda i: (0, 0), pipeline_mode=pl.Buffered(1))],
        [pl.BlockSpec((tm, D), lambda i: (i, 0)),
         pl.BlockSpec((tm, 2 * DFF), lambda i: (i, 0)),
         pl.BlockSpec((tm, D), lambda i: (i, 0))],
        [jax.ShapeDtypeStruct((S, D), F32),
         jax.ShapeDtypeStruct((S, 2 * DFF), ACT_DTYPE),
         jax.ShapeDtypeStruct((S, D), ACT_DTYPE)],
        ("arbitrary",), (x, vec, w1p, w2), carry)


def ffn_bwd(dxo, x, a, f, vec, w1p, w2, gs, name, carry=None):
    S = x.shape[0]
    tm = _row_tile(S, 256)

    def body(dxo_ref, x_ref, a_ref, f_ref, vec_ref, w1_ref, w2_ref,
             dxi_ref, hb_ref, dfb_ref, act_ref, da_ref, sums_ref):
        xt = x_ref[...]
        dxo = dxo_ref[...]
        xhat, r, gain, ng, sc, sh, gt = _ada(xt, vec_ref)
        hb_ref[...] = (xhat * gain + sh).astype(hb_ref.dtype)
        dgate = gs * jnp.sum(dxo * f_ref[...].astype(F32), axis=0, keepdims=True)
        df = ((gs * gt) * dxo).astype(MXU_DTYPE)
        dfb_ref[...] = df
        dh = jnp.zeros((tm, D), F32)
        for hf in range(2):
            lo, hi = hf * HALF, (hf + 1) * HALF
            dact = lax.dot_general(df, w2_ref[lo:hi, :], NT_DIMS, preferred_element_type=F32)
            g = a_ref[:, lo:hi].astype(F32)
            up = a_ref[:, DFF + lo:DFF + hi].astype(F32)
            sg = jax.nn.sigmoid(g)
            si = g * sg
            act_ref[:, lo:hi] = (si * up).astype(act_ref.dtype)
            dg = (dact * up * (sg * (1.0 + g * (1.0 - sg)))).astype(MXU_DTYPE)
            dup = (dact * si).astype(MXU_DTYPE)
            da_ref[:, lo:hi] = dg
            da_ref[:, DFF + lo:DFF + hi] = dup
            dh = dh + lax.dot_general(dg, w1_ref[hf], NT_DIMS, preferred_element_type=F32)
            dh = dh + lax.dot_general(dup, w1_ref[2 + hf], NT_DIMS, preferred_element_type=F32)
        dx, dshift, dscale, dng = _ada_bwd(dh, xhat, r, gain, ng, sc)
        dxi_ref[...] = dxo + dx
        _acc_rows(sums_ref, pl.program_id(0) == 0, (dshift, dscale, dng, dgate))

    return _pcall(
        body, name, (S // tm,),
        [pl.BlockSpec((tm, D), lambda i: (i, 0)),
         pl.BlockSpec((tm, D), lambda i: (i, 0)),
         pl.BlockSpec((tm, 2 * DFF), lambda i: (i, 0)),
         pl.BlockSpec((tm, D), lambda i: (i, 0)),
         pl.BlockSpec((SUBLANES, D), lambda i: (0, 0)),
         pl.BlockSpec((NCHIP, D, HALF), lambda i: (0, 0, 0), pipeline_mode=pl.Buffered(1)),
         pl.BlockSpec((DFF, D), lambda i: (0, 0), pipeline_mode=pl.Buffered(1))],
        [pl.BlockSpec((tm, D), lambda i: (i, 0)),
         pl.BlockSpec((tm, D), lambda i: (i, 0)),
         pl.BlockSpec((tm, D), lambda i: (i, 0)),
         pl.BlockSpec((tm, DFF), lambda i: (i, 0)),
         pl.BlockSpec((tm, 2 * DFF), lambda i: (i, 0)),
         pl.BlockSpec((SUBLANES, D), lambda i: (0, 0))],
        [jax.ShapeDtypeStruct((S, D), F32),
         jax.ShapeDtypeStruct((S, D), MXU_DTYPE),
         jax.ShapeDtypeStruct((S, D), MXU_DTYPE),
         jax.ShapeDtypeStruct((S, DFF), MXU_DTYPE),
         jax.ShapeDtypeStruct((S, 2 * DFF), MXU_DTYPE),
         jax.ShapeDtypeStruct((SUBLANES, D), F32)],
        ("arbitrary",), (dxo, x, a, f, vec, w1p, w2), carry)


def wgrad(a, b, kt, nt, name, carry=None):
    T, K = a.shape
    N = b.shape[1]
    pk, pn = K // kt, N // nt
    assert pk == 1 or pn == 1
    tt = _row_tile(T, 1024)
    steps = T // tt

    def body(a_ref, b_ref, o_ref):
        @pl.when(pl.program_id(1) == 0)
        def _():
            o_ref[...] = jnp.zeros_like(o_ref)
        o_ref[...] += lax.dot_general(a_ref[...], b_ref[...], TN_DIMS, preferred_element_type=F32)

    a_map = (lambda p, t: (t, p)) if pk > 1 else (lambda p, t: (t, 0))
    b_map = (lambda p, t: (t, p)) if pn > 1 else (lambda p, t: (t, 0))
    (out,), got = _pcall(
        body, name, (pk * pn, steps),
        [pl.BlockSpec((tt, kt), a_map), pl.BlockSpec((tt, nt), b_map)],
        [pl.BlockSpec((None, kt, nt), lambda p, t: (p, 0, 0))],
        [jax.ShapeDtypeStruct((pk * pn, kt, nt), F32)], ("arbitrary", "arbitrary"), (a, b), carry)
    return out, got


def _head_masks(rows):
    lane = lax.broadcasted_iota(jnp.int32, (rows, LANES), 1)
    return lane < HD


def _pair_stat(x, m_a):
    s_a = jnp.sum(jnp.where(m_a, x, 0.0), axis=1, keepdims=True)
    s_b = jnp.sum(jnp.where(m_a, 0.0, x), axis=1, keepdims=True)
    return s_a, s_b


def mixer_in(x, vec, winp, gvec, name):
    S = x.shape[0]
    tm = _row_tile(S, 512)
    pc = INC // NCHIP

    def body(x_ref, vec_ref, w_ref, g_ref, proj_ref, hb_ref, qn_ref, kn_ref, v_ref):
        xt = x_ref[...]
        xhat, _, gain, _, _, sh, _ = _ada(xt, vec_ref)
        h = (xhat * gain + sh).astype(MXU_DTYPE)
        hb_ref[...] = h
        for j in range(NCHIP):
            proj_ref[:, j * pc:(j + 1) * pc] = jnp.dot(h, w_ref[j], preferred_element_type=F32)
        m_a = _head_masks(tm)
        for which, dst in ((0, qn_ref), (1, kn_ref)):
            for p in range(AW // LANES):
                lo = which * AW + p * LANES
                xp = proj_ref[:, lo:lo + LANES]
                s_a, s_b = _pair_stat(xp * xp, m_a)
                rr = jnp.where(m_a, lax.rsqrt(s_a * (1.0 / HD) + EPS), lax.rsqrt(s_b * (1.0 / HD) + EPS))
                gp = g_ref[which:which + 1, p * LANES:(p + 1) * LANES]
                dst[:, p * LANES:(p + 1) * LANES] = (xp * rr * gp).astype(dst.dtype)
        v_ref[...] = proj_ref[:, 2 * AW:3 * AW].astype(v_ref.dtype)

    return pl.pallas_call(
        body, name=name, grid=(S // tm,),
        in_specs=[pl.BlockSpec((tm, D), lambda i: (i, 0)),
                  pl.BlockSpec((SUBLANES, D), lambda i: (0, 0)),
                  pl.BlockSpec((NCHIP, D, pc), lambda i: (0, 0, 0), pipeline_mode=pl.Buffered(1)),
                  pl.BlockSpec((SUBLANES, AW), lambda i: (0, 0))],
        out_specs=[pl.BlockSpec((tm, INC), lambda i: (i, 0)),
                   pl.BlockSpec((tm, D), lambda i: (i, 0)),
                   pl.BlockSpec((tm, AW), lambda i: (i, 0)),
                   pl.BlockSpec((tm, AW), lambda i: (i, 0)),
                   pl.BlockSpec((tm, AW), lambda i: (i, 0))],
        out_shape=[jax.ShapeDtypeStruct((S, INC), F32),
                   jax.ShapeDtypeStruct((S, D), MXU_DTYPE),
                   jax.ShapeDtypeStruct((S, AW), F32),
                   jax.ShapeDtypeStruct((S, AW), F32),
                   jax.ShapeDtypeStruct((S, AW), F32)],
        compiler_params=_params(("arbitrary",)),
    )(x, vec, winp, gvec)


def _band_masks(ncol):
    row = lax.broadcasted_iota(jnp.int32, (2 * QBLK, ncol), 0) & (QBLK - 1)
    col = lax.broadcasted_iota(jnp.int32, (2 * QBLK, ncol), 1)
    return row, col


def _stack_heads(t, m_a):
    zero = jnp.zeros_like(t)
    return jnp.concatenate([jnp.where(m_a, t, zero), jnp.where(m_a, zero, t)], axis=0)


class _AttnLayout:
    def __init__(self, d, S):
        self.d, self.S = d, S
        self.qb = max(1, min(ATTN_QBLOCKS, ATTN_CHUNK_ROWS // (QBLK * d)))
        self.nres = d
        self.nchunk = S // (self.qb * QBLK * d)
        self.grid = (AW // LANES, self.nchunk)
        self.unroll = max(1, min(d, ATTN_QBLOCKS // self.qb))

    def _spec(self, blocks, row_of):
        return pl.BlockSpec((blocks * QBLK * self.d, LANES), lambda hp, j: (row_of(j), hp))

    def cur(self, chunk_of):
        return self._spec(self.qb, chunk_of)

    def prev(self, chunk_of):
        return self._spec(1, lambda j: jnp.maximum(chunk_of(j) * self.qb - 1, 0))

    def idx(self, b, r):
        if self.d == 1:
            return (pl.ds(b * QBLK, QBLK), slice(None))
        return (pl.ds(b * QBLK * self.d + r, QBLK, stride=self.d), slice(None))

    def per_residue(self, fn):
        if self.nres == 1:
            fn(0)
        else:
            def step(it, carry):
                for k in range(self.unroll):
                    fn(it * self.unroll + k)
                return carry
            lax.fori_loop(0, self.nres // self.unroll, step, 0)


def attn_fwd(qn, kn, v, d, name, carry=None):
    S = qn.shape[0]
    lay = _AttnLayout(d, S)
    qb = lay.qb

    def body(q_ref, kc_ref, kp_ref, vc_ref, vp_ref, o_ref, la_ref, lb_ref):
        i = pl.program_id(1)
        m_a = _head_masks(QBLK)
        row, col = _band_masks(2 * QBLK)
        dist = row + QBLK - col
        band = (dist >= 0) & (dist <= QBLK)
        first = band & ((i > 0) | (col >= QBLK))

        def residue(r):
            kt = [kp_ref[lay.idx(0, r)].astype(MXU_DTYPE)]
            vt = [vp_ref[lay.idx(0, r)].astype(MXU_DTYPE)]
            for b in range(qb):
                kt.append(kc_ref[lay.idx(b, r)].astype(MXU_DTYPE))
                vt.append(vc_ref[lay.idx(b, r)].astype(MXU_DTYPE))
            for b in range(qb):
                rows = lay.idx(b, r)
                q = (q_ref[rows] * (HD ** -0.5)).astype(MXU_DTYPE)
                kcat = jnp.concatenate([kt[b], kt[b + 1]], axis=0)
                vcat = jnp.concatenate([vt[b], vt[b + 1]], axis=0)
                mask = first if b == 0 else band
                s = lax.dot_general(_stack_heads(q, m_a), kcat, NT_DIMS, preferred_element_type=F32)
                s = jnp.where(mask, s, NEG)
                m = jnp.max(s, axis=1, keepdims=True)
                p = jnp.exp(s - m)
                l = jnp.sum(p, axis=1, keepdims=True)
                o = jnp.dot(p.astype(MXU_DTYPE), vcat, preferred_element_type=F32) / l
                lse = jnp.broadcast_to(m + jnp.log(l), (2 * QBLK, LANES))
                o_ref[rows] = jnp.where(m_a, o[:QBLK], o[QBLK:])
                la_ref[rows] = lse[:QBLK]
                lb_ref[rows] = lse[QBLK:]

        lay.per_residue(residue)

    cur, prev = lay.cur(lambda j: j), lay.prev(lambda j: j)
    return _pcall(body, name, lay.grid, [cur, cur, prev, cur, prev], [cur, cur, cur],
                  [jax.ShapeDtypeStruct((S, AW), F32)] * 3, ("arbitrary", "arbitrary"), (qn, kn, kn, v, v), carry)


def attn_bwd(qn, kn, v, dycat, lse_a, lse_b, dl_a, dl_b, d, name, carry=None):
    S = qn.shape[0]
    lay = _AttnLayout(d, S)
    qb, nchunk = lay.qb, lay.nchunk

    def body(q_ref, kc_ref, kp_ref, vc_ref, vp_ref, do_ref, la_ref, lb_ref, da_ref, db_ref,
             dq_ref, dk_ref, dv_ref, ck_ref, cv_ref):
        j = pl.program_id(1)
        i = nchunk - 1 - j
        m_a = _head_masks(QBLK)
        row, col = _band_masks(2 * QBLK)
        dist = row + QBLK - col
        band = (dist >= 0) & (dist <= QBLK)
        first = band & ((i > 0) | (col >= QBLK))

        def residue(r):
            def tiles(ref, cast):
                out = [ref[lay.idx(b, r)] for b in range(qb)]
                return [t.astype(MXU_DTYPE) for t in out] if cast else out

            def ktiles(cur_ref, prev_ref):
                return [prev_ref[lay.idx(0, r)].astype(MXU_DTYPE)] + tiles(cur_ref, True)

            qt = [(t * (HD ** -0.5)).astype(MXU_DTYPE) for t in tiles(q_ref, False)]
            dot_ = tiles(do_ref, True)
            lse_t = list(zip(tiles(la_ref, False), tiles(lb_ref, False)))
            dl_t = list(zip(tiles(da_ref, False), tiles(db_ref, False)))
            kt = ktiles(kc_ref, kp_ref)
            vt = ktiles(vc_ref, vp_ref)
            dk_acc = [jnp.zeros((QBLK, LANES), F32) for _ in range(qb)]
            dv_acc = [jnp.zeros((QBLK, LANES), F32) for _ in range(qb)]
            crow = pl.ds(0, QBLK) if lay.nres == 1 else pl.ds(pl.multiple_of(r * QBLK, QBLK), QBLK)
            dk_acc[qb - 1] = jnp.where(j > 0, ck_ref[crow, :], 0.0)
            dv_acc[qb - 1] = jnp.where(j > 0, cv_ref[crow, :], 0.0)
            for x in range(qb):
                kcat = jnp.concatenate([kt[x], kt[x + 1]], axis=0)
                vcat = jnp.concatenate([vt[x], vt[x + 1]], axis=0)
                q2 = _stack_heads(qt[x], m_a)
                do2 = _stack_heads(dot_[x], m_a)
                lse2 = jnp.concatenate(lse_t[x], axis=0)
                dl2 = jnp.concatenate(dl_t[x], axis=0)
                lse2 = jnp.concatenate([lse2, lse2], axis=1)
                dl2 = jnp.concatenate([dl2, dl2], axis=1)
                s = lax.dot_general(q2, kcat, NT_DIMS, preferred_element_type=F32)
                p = jnp.exp(jnp.where(first if x == 0 else band, s, NEG) - lse2)
                dp = lax.dot_general(do2, vcat, NT_DIMS, preferred_element_type=F32)
                ds = p * (dp - dl2)
                dq = jnp.dot(ds.astype(MXU_DTYPE), kcat, preferred_element_type=F32)
                dq_ref[lay.idx(x, r)] = jnp.where(m_a, dq[:QBLK], dq[QBLK:]) * (HD ** -0.5)
                dk = jnp.dot(ds.T.astype(MXU_DTYPE), q2, preferred_element_type=F32)
                dv = jnp.dot(p.T.astype(MXU_DTYPE), do2, preferred_element_type=F32)
                if x == 0:
                    ck_ref[crow, :] = dk[:QBLK]
                    cv_ref[crow, :] = dv[:QBLK]
                else:
                    dk_acc[x - 1] = dk_acc[x - 1] + dk[:QBLK]
                    dv_acc[x - 1] = dv_acc[x - 1] + dv[:QBLK]
                dk_acc[x] = dk_acc[x] + dk[QBLK:]
                dv_acc[x] = dv_acc[x] + dv[QBLK:]
            for kb in range(qb):
                dk_ref[lay.idx(kb, r)] = dk_acc[kb]
                dv_ref[lay.idx(kb, r)] = dv_acc[kb]

        lay.per_residue(residue)

    cur, prev = lay.cur(lambda j: nchunk - 1 - j), lay.prev(lambda j: nchunk - 1 - j)
    carried = pltpu.VMEM((lay.nres * QBLK, LANES), F32)
    return _pcall(
        body, name, lay.grid, [cur, cur, prev, cur, prev, cur, cur, cur, cur, cur], [cur, cur, cur],
        [jax.ShapeDtypeStruct((S, AW), F32)] * 3, ("arbitrary", "arbitrary"),
        (qn, kn, kn, v, v, dycat, lse_a, lse_b, dl_a, dl_b), carry, scratch=[carried, carried])


def _shift_down(x, halo_prev, k, row):
    tm = x.shape[0]
    tail = jnp.concatenate([pltpu.roll(halo_prev, k, 0), jnp.zeros((tm - SUBLANES, x.shape[1]), x.dtype)], axis=0)
    return jnp.where(row < k, tail, pltpu.roll(x, k, 0))


def _shift_up(x, halo_next, k, row):
    tm = x.shape[0]
    head = jnp.concatenate([jnp.zeros((tm - SUBLANES, x.shape[1]), x.dtype), pltpu.roll(halo_next, SUBLANES - k, 0)], axis=0)
    return jnp.where(row >= tm - k, head, pltpu.roll(x, tm - k, 0))


def _conv_fwd(cu, halo_cu, cw_ref, row):
    u1 = _shift_down(cu, halo_cu, 1, row)
    u2 = _shift_down(cu, halo_cu, 2, row)
    cv = cw_ref[0:1, :] * u2 + cw_ref[1:2, :] * u1 + cw_ref[2:3, :] * cu + cw_ref[3:4, :]
    return cv, u1, u2


def combine_conv(os_, lses_a, lses_b, proj, cw, name, carry=None):
    S = proj.shape[0]
    tm = _row_tile(S, 512)
    hb = tm // SUBLANES

    def body(o1, o2, o3, a1, a2, a3, b1, b2, b3, pc_ref, ph_ref, cw_ref, ycat_ref, la_ref, lb_ref):
        i = pl.program_id(0)
        m_a = _head_masks(tm)
        for p in range(AW // LANES):
            cs = slice(p * LANES, (p + 1) * LANES)
            tot = []
            for srcs, dst in (((a1, a2, a3), la_ref), ((b1, b2, b3), lb_ref)):
                ls = [l[:, cs] for l in srcs]
                mx = jnp.maximum(jnp.maximum(ls[0], ls[1]), ls[2])
                t = mx + jnp.log(jnp.exp(ls[0] - mx) + jnp.exp(ls[1] - mx) + jnp.exp(ls[2] - mx))
                dst[:, cs] = t
                tot.append((ls, t))
            acc = jnp.zeros((tm, LANES), F32)
            for r, o in enumerate((o1, o2, o3)):
                w = jnp.where(m_a, jnp.exp(tot[0][0][r] - tot[0][1]), jnp.exp(tot[1][0][r] - tot[1][1]))
                acc = acc + w * o[:, cs]
            ycat_ref[:, cs] = acc.astype(ycat_ref.dtype)
        row = lax.broadcasted_iota(jnp.int32, (tm, CW), 0)
        gb, gc, u = pc_ref[:, 0:CW], pc_ref[:, CW:2 * CW], pc_ref[:, 2 * CW:3 * CW]
        halo_cu = jnp.where(i > 0, ph_ref[:, CW:2 * CW] * ph_ref[:, 2 * CW:3 * CW], 0.0)
        cv, _, _ = _conv_fwd(gc * u, halo_cu, cw_ref, row)
        ycat_ref[:, AW:AW + CW] = (gb * cv).astype(ycat_ref.dtype)

    ot = pl.BlockSpec((tm, AW), lambda i: (i, 0))
    return _pcall(
        body, name, (S // tm,),
        [ot] * 9 + [pl.BlockSpec((tm, 3 * CW), lambda i: (i, 1)),
                    pl.BlockSpec((SUBLANES, 3 * CW), lambda i: (jnp.maximum(i * hb - 1, 0), 1)),
                    pl.BlockSpec((SUBLANES, CW), lambda i: (0, 0))],
        [pl.BlockSpec((tm, D), lambda i: (i, 0)), ot, ot],
        [jax.ShapeDtypeStruct((S, D), ACT_DTYPE), jax.ShapeDtypeStruct((S, AW), F32),
         jax.ShapeDtypeStruct((S, AW), F32)],
        ("arbitrary",), (*os_, *lses_a, *lses_b, proj, proj, cw), carry)


def out_proj(ycat, x, vec, wout, name):
    S = x.shape[0]
    tm = _row_tile(S, 512)

    def body(yc_ref, x_ref, vec_ref, w_ref, xn_ref, y_ref):
        y = jnp.dot(yc_ref[...].astype(MXU_DTYPE), w_ref[...], preferred_element_type=F32)
        xn_ref[...] = x_ref[...] + vec_ref[3:4, :] * y
        y_ref[...] = y.astype(y_ref.dtype)

    t = pl.BlockSpec((tm, D), lambda i: (i, 0))
    return pl.pallas_call(
        body, name=name, grid=(S // tm,),
        in_specs=[t, t, pl.BlockSpec((SUBLANES, D), lambda i: (0, 0)),
                  pl.BlockSpec((D, D), lambda i: (0, 0))],
        out_specs=[t, t],
        out_shape=[jax.ShapeDtypeStruct((S, D), F32), jax.ShapeDtypeStruct((S, D), ACT_DTYPE)],
        compiler_params=_params(("arbitrary",)),
    )(ycat, x, vec, wout)


def out_proj_bwd(dxo, y, ycat, vec, wout, name, carry=None):
    S = dxo.shape[0]
    tm = _row_tile(S, 512)

    def body(dxo_ref, y_ref, yc_ref, vec_ref, w_ref, dyb_ref, dyc_ref, da_ref, db_ref, sums_ref):
        dxo = dxo_ref[...]
        dgate = jnp.sum(dxo * y_ref[...].astype(F32), axis=0, keepdims=True)
        dy = (vec_ref[3:4, :] * dxo).astype(MXU_DTYPE)
        dyb_ref[...] = dy
        dyc_ref[...] = lax.dot_general(dy, w_ref[...], NT_DIMS, preferred_element_type=F32)
        m_a = _head_masks(tm)
        for p in range(AW // LANES):
            cs = slice(p * LANES, (p + 1) * LANES)
            s_a, s_b = _pair_stat(dyc_ref[:, cs] * yc_ref[:, cs].astype(F32), m_a)
            da_ref[:, cs] = jnp.broadcast_to(s_a, (tm, LANES))
            db_ref[:, cs] = jnp.broadcast_to(s_b, (tm, LANES))
        _acc_rows(sums_ref, pl.program_id(0) == 0, (dgate,))

    t = pl.BlockSpec((tm, D), lambda i: (i, 0))
    at = pl.BlockSpec((tm, AW), lambda i: (i, 0))
    return _pcall(
        body, name, (S // tm,),
        [t, t, t, pl.BlockSpec((SUBLANES, D), lambda i: (0, 0)), pl.BlockSpec((D, D), lambda i: (0, 0))],
        [t, t, at, at, pl.BlockSpec((SUBLANES, D), lambda i: (0, 0))],
        [jax.ShapeDtypeStruct((S, D), MXU_DTYPE), jax.ShapeDtypeStruct((S, D), F32),
         jax.ShapeDtypeStruct((S, AW), F32), jax.ShapeDtypeStruct((S, AW), F32),
         jax.ShapeDtypeStruct((SUBLANES, D), F32)],
        ("arbitrary",), (dxo, y, ycat, vec, wout), carry)


def mixer_mid_bwd(dqs, dks, dvs, proj, dycat, gvec, cw, name, carry=None):
    S = proj.shape[0]
    tm = _row_tile(S, 256)
    hb = tm // SUBLANES
    nsl = S // SUBLANES
    ntile = S // tm

    def body(dq1, dq2, dq3, dk1, dk2, dk3, dv1, dv2, dv3, pr_ref, pp_ref, pn_ref, dyc_ref, dyn_ref,
             g_ref, cw_ref, dp_ref, sums_ref):
        i = pl.program_id(0)
        m_a = _head_masks(tm)
        gsum = []
        for which, parts in ((0, (dq1, dq2, dq3)), (1, (dk1, dk2, dk3))):
            acc_g = []
            for p in range(AW // LANES):
                lo = which * AW + p * LANES
                cs = slice(p * LANES, (p + 1) * LANES)
                xp = pr_ref[:, lo:lo + LANES]
                s_a, s_b = _pair_stat(xp * xp, m_a)
                rr = jnp.where(m_a, lax.rsqrt(s_a * (1.0 / HD) + EPS), lax.rsqrt(s_b * (1.0 / HD) + EPS))
                xh = xp * rr
                dn = parts[0][:, cs] + parts[1][:, cs] + parts[2][:, cs]
                acc_g.append(jnp.sum(dn * xh, axis=0, keepdims=True))
                t = dn * g_ref[which:which + 1, cs]
                t_a, t_b = _pair_stat(t * xh, m_a)
                mean = jnp.where(m_a, t_a, t_b) * (1.0 / HD)
                dp_ref[:, lo:lo + LANES] = (rr * (t - xh * mean)).astype(dp_ref.dtype)
            gsum.append(jnp.concatenate(acc_g, axis=1))
        dp_ref[:, 2 * AW:3 * AW] = (dv1[...] + dv2[...] + dv3[...]).astype(dp_ref.dtype)
        row = lax.broadcasted_iota(jnp.int32, (tm, CW), 0)
        base = 3 * AW
        gb, gc, u = pr_ref[:, base:base + CW], pr_ref[:, base + CW:base + 2 * CW], pr_ref[:, base + 2 * CW:base + 3 * CW]
        cu = gc * u
        halo_cu = jnp.where(i > 0, pp_ref[:, CW:2 * CW] * pp_ref[:, 2 * CW:3 * CW], 0.0)
        cv, u1, u2 = _conv_fwd(cu, halo_cu, cw_ref, row)
        dyc = dyc_ref[...]
        dp_ref[:, base:base + CW] = (dyc * cv).astype(dp_ref.dtype)
        dcv = dyc * gb
        halo_dcv = jnp.where(i < ntile - 1, dyn_ref[...] * pn_ref[:, 0:CW], 0.0)
        d1 = _shift_up(dcv, halo_dcv, 1, row)
        d2 = _shift_up(dcv, halo_dcv, 2, row)
        dcu = cw_ref[2:3, :] * dcv + cw_ref[1:2, :] * d1 + cw_ref[0:1, :] * d2
        dp_ref[:, base + CW:base + 2 * CW] = (dcu * u).astype(dp_ref.dtype)
        dp_ref[:, base + 2 * CW:base + 3 * CW] = (dcu * gc).astype(dp_ref.dtype)
        rows = (gsum[0], gsum[1],
                jnp.sum(dcv * u2, axis=0, keepdims=True), jnp.sum(dcv * u1, axis=0, keepdims=True),
                jnp.sum(dcv * cu, axis=0, keepdims=True), jnp.sum(dcv, axis=0, keepdims=True))
        _acc_rows(sums_ref, i == 0, rows)

    at = pl.BlockSpec((tm, AW), lambda i: (i, 0))
    return _pcall(
        body, name, (ntile,),
        [at] * 9 + [
            pl.BlockSpec((tm, INC), lambda i: (i, 0)),
            pl.BlockSpec((SUBLANES, 3 * CW), lambda i: (jnp.maximum(i * hb - 1, 0), 1)),
            pl.BlockSpec((SUBLANES, 3 * CW), lambda i: (jnp.minimum((i + 1) * hb, nsl - 1), 1)),
            pl.BlockSpec((tm, CW), lambda i: (i, 1)),
            pl.BlockSpec((SUBLANES, CW), lambda i: (jnp.minimum((i + 1) * hb, nsl - 1), 1)),
            pl.BlockSpec((SUBLANES, AW), lambda i: (0, 0)),
            pl.BlockSpec((SUBLANES, CW), lambda i: (0, 0))],
        [pl.BlockSpec((tm, INC), lambda i: (i, 0)), pl.BlockSpec((SUBLANES, AW), lambda i: (0, 0))],
        [jax.ShapeDtypeStruct((S, INC), MXU_DTYPE), jax.ShapeDtypeStruct((SUBLANES, AW), F32)],
        ("arbitrary",), (*dqs, *dks, *dvs, proj, proj, proj, dycat, dycat, gvec, cw), carry)


def mixer_in_bwd(dxo, x, dproj, vec, winp, name, carry=None):
    S = x.shape[0]
    tm = _row_tile(S, 512)
    pc = INC // NCHIP

    def body(dxo_ref, x_ref, dp_ref, vec_ref, w_ref, dxi_ref, sums_ref):
        xhat, r, gain, ng, sc, _, _ = _ada(x_ref[...], vec_ref)
        dh = jnp.zeros((tm, D), F32)
        for j in range(NCHIP):
            dh = dh + lax.dot_general(dp_ref[:, j * pc:(j + 1) * pc], w_ref[j], NT_DIMS, preferred_element_type=F32)
        dx, dshift, dscale, dng = _ada_bwd(dh, xhat, r, gain, ng, sc)
        dxi_ref[...] = dxo_ref[...] + dx
        _acc_rows(sums_ref, pl.program_id(0) == 0, (dshift, dscale, dng))

    t = pl.BlockSpec((tm, D), lambda i: (i, 0))
    return _pcall(
        body, name, (S // tm,),
        [t, t, pl.BlockSpec((tm, INC), lambda i: (i, 0)),
         pl.BlockSpec((SUBLANES, D), lambda i: (0, 0)),
         pl.BlockSpec((NCHIP, D, pc), lambda i: (0, 0, 0), pipeline_mode=pl.Buffered(1))],
        [t, pl.BlockSpec((SUBLANES, D), lambda i: (0, 0))],
        [jax.ShapeDtypeStruct((S, D), F32), jax.ShapeDtypeStruct((SUBLANES, D), F32)],
        ("arbitrary",), (dxo, x, dproj, vec, winp), carry)


def loss_head(xf, target, name):
    S = xf.shape[0]
    tm = _row_tile(S, 1024)

    def body(x_ref, t_ref, dy_ref, l_ref):
        diff = x_ref[...] - t_ref[...]
        dy_ref[...] = diff * (1.0 / D)
        part = jnp.sum(jnp.sum(diff * diff, axis=0, keepdims=True), axis=1, keepdims=True) * (0.5 / D)

        @pl.when(pl.program_id(0) == 0)
        def _():
            l_ref[...] = jnp.zeros_like(l_ref)
        l_ref[...] += jnp.broadcast_to(part, l_ref.shape)

    t = pl.BlockSpec((tm, D), lambda i: (i, 0))
    return pl.pallas_call(
        body, name=name, grid=(S // tm,),
        in_specs=[t, t],
        out_specs=[t, pl.BlockSpec((SUBLANES, LANES), lambda i: (0, 0))],
        out_shape=[jax.ShapeDtypeStruct((S, D), F32), jax.ShapeDtypeStruct((SUBLANES, LANES), F32)],
        compiler_params=_params(("arbitrary",)),
    )(xf, target)


def _vec(mod_l, ng_l, i):
    m = mod_l.reshape(3, 3, D)
    rows = jnp.stack([ng_l[i], m[i, 1], m[i, 0], m[i, 2]])
    return jnp.concatenate([rows, jnp.zeros((SUBLANES - 4, D), F32)], axis=0)


def local_step(x, target, mods, ngs, gvecs, cws, shards, w_first, cflag):
    saved = []
    weights = [dict(w1=[None, None], w2=[None, None]) for _ in range(2)]
    weights[0]["w1"][0], weights[0]["w2"][0] = w_first[0], w_first[1].reshape(DFF, D)
    h = x
    for l in range(2):
        w, sh = weights[l], shards[l]
        nxt = shards[l + 1] if l == 0 else None
        vecs = [_vec(mods[l], ngs[l], i) for i in range(3)]
        x0 = h
        (x1, a0, f0), (win, wout) = ffn_fwd(x0, vecs[0], w["w1"][0], w["w2"][0], 0.5, f"ffn_fwd_l{l}a",
                                            carry=Carry("gather", [sh["win"], sh["wout"]]))
        w["win"], w["wout"] = win, wout.reshape(D, D)
        proj, h1b, qn, kn, v = mixer_in(x1, vecs[1], w["win"], gvecs[l], f"mixer_in_l{l}")
        os_, lses_a, lses_b = [], [], []
        for d in DILATIONS:
            carry = {1: Carry("gather", [sh["w2"][1]]), 16: Carry("gather", [sh["w1"][1]])}.get(d)
            (o, la, lb), got = attn_fwd(qn, kn, v, d, f"attn_fwd_l{l}_d{d}", carry=carry)
            if d == 1:
                w["w2"][1] = got[0].reshape(DFF, D)
            if d == 16:
                w["w1"][1] = got[0]
            os_.append(o)
            lses_a.append(la)
            lses_b.append(lb)
        (ycat, *lse), got = combine_conv(os_, lses_a, lses_b, proj, cws[l], f"combine_conv_l{l}",
                                         carry=Carry("gather", [nxt["w2"][0]]) if nxt else None)
        if nxt:
            weights[1]["w2"][0] = got[0].reshape(DFF, D)
        x2, y = out_proj(ycat, x1, vecs[1], w["wout"], f"out_proj_l{l}")
        (x3, a2, f2), got = ffn_fwd(x2, vecs[2], w["w1"][1], w["w2"][1], 0.5, f"ffn_fwd_l{l}b",
                                    carry=Carry("gather", [nxt["w1"][0]]) if nxt else None)
        if nxt:
            weights[1]["w1"][0] = got[0]
        saved.append(dict(vecs=vecs, x0=x0, a0=a0, f0=f0, x1=x1, proj=proj, h1b=h1b, qn=qn, kn=kn, v=v,
                          ycat=ycat, lse=lse, y=y, x2=x2, a2=a2, f2=f2))
        h = x3
    dx, loss_blk = loss_head(h, target, "loss_head")
    sums, totals, g_prev = [None, None], [None, None], None
    w2r = DFF // NCHIP
    for l in (1, 0):
        w, s = weights[l], saved[l]
        vecs = s["vecs"]
        ride = g_prev is not None
        own = l == 0
        mine, other = [None] * 6, [None] * 6

        def half_sum(group, recv, k0):
            return [add_half(g, r, cflag, f"add_sibling_l{l}_{k0 + j}") for j, (g, r) in enumerate(zip(group, recv))]

        def chip_sum(landed, k0):
            return [sum_chips(t, f"sum_chips_l{l}_{k0 + j}") for j, t in enumerate(landed)]

        (dx, hb, dfb, act, da, sums2), got = ffn_bwd(
            dx, s["x2"], s["a2"], s["f2"], vecs[2], w["w1"][1], w["w2"][1], 0.5, f"ffn_bwd_l{l}b",
            carry=Carry("swap_halves", g_prev) if ride else None)
        dw1b, _ = wgrad(hb, da, D, HALF, f"wgrad_w1_l{l}b")
        dw2b, _ = wgrad(act, dfb, HALF, D, f"wgrad_w2_l{l}b")
        if ride:
            wire = [add_half(g_prev[k], got[k], cflag, f"add_sibling_l{l + 1}_{k}") for k in range(6)]
        g_ffn_b = [dw1b, dw2b.reshape(NCHIP, w2r, D)]
        (dyb, dycat, dl_a, dl_b, sums_o), got = out_proj_bwd(
            dx, s["y"], s["ycat"], vecs[1], w["wout"], f"out_proj_bwd_l{l}",
            carry=Carry("swap_halves", g_ffn_b) if own else None)
        dwout, _ = wgrad(s["ycat"].astype(MXU_DTYPE), dyb, D // 2, D, f"wgrad_wout_l{l}")
        if own:
            wire_ffn_b = half_sum(g_ffn_b, got, 4)
        dqs, dks, dvs, landed = [], [], [], {}
        for d in DILATIONS:
            carry = None
            if ride and d == 1:
                carry = Carry("scatter", wire[3:])
            if ride and d == 16:
                carry = Carry("scatter", wire[:3])
            if own and d == 4:
                carry = Carry("scatter", wire_ffn_b)
            (dq, dk, dv), landed[d] = attn_bwd(s["qn"], s["kn"], s["v"], dycat, s["lse"][0], s["lse"][1], dl_a, dl_b,
                                               d, f"attn_bwd_l{l}_d{d}", carry=carry)
            dqs.append(dq)
            dks.append(dk)
            dvs.append(dv)
        if ride:
            tot = [sum_chips(t, f"sum_chips_l{l + 1}_{k}") for k, t in enumerate(list(landed[16]) + list(landed[1]))]
        if own:
            mine[4:6] = chip_sum(landed[4], 4)
        (dproj, sums_m), got = mixer_mid_bwd(dqs, dks, dvs, s["proj"], dycat, gvecs[l], cws[l], f"mixer_mid_bwd_l{l}",
                                             carry=Carry("swap", mine[4:6]) if own else None)
        if own:
            other[4:6] = list(got)
        dwin, _ = wgrad(s["h1b"], dproj, D, INC // NCHIP, f"wgrad_win_l{l}")
        g_mixer = [dwin, dwout.reshape(NCHIP, D // NCHIP, D)]
        (dx, sums1), got = mixer_in_bwd(dx, s["x1"], dproj, vecs[1], w["win"], f"mixer_in_bwd_l{l}",
                                        carry=Carry("swap_halves", g_mixer) if own else None)
        if own:
            wire_mixer = half_sum(g_mixer, got, 2)
        (dx, hb, dfb, act, da, sums0), got = ffn_bwd(
            dx, s["x0"], s["a0"], s["f0"], vecs[0], w["w1"][0], w["w2"][0], 0.5, f"ffn_bwd_l{l}a",
            carry=Carry("swap", tot) if ride else None)
        if ride:
            totals[l + 1] = (tot, list(got))
        dw1a, got = wgrad(hb, da, D, HALF, f"wgrad_w1_l{l}a", carry=Carry("scatter", wire_mixer) if own else None)
        if own:
            mine[2:4] = chip_sum(got, 2)
        dw2a, got = wgrad(act, dfb, HALF, D, f"wgrad_w2_l{l}a", carry=Carry("swap", mine[2:4]) if own else None)
        g_ffn_a = [dw1a, dw2a.reshape(NCHIP, w2r, D)]
        if own:
            other[2:4] = list(got)
            wire_ffn_a = half_sum(g_ffn_a, run_carry(Carry("swap_halves", g_ffn_a), "swap_halves_tail"), 0)
            mine[0:2] = chip_sum(run_carry(Carry("scatter", wire_ffn_a), "scatter_grads_tail"), 0)
            other[0:2] = list(run_carry(Carry("swap", mine[0:2]), "swap_totals_tail"))
            totals[l] = (mine, other)
        g_prev = g_ffn_a + g_mixer + g_ffn_b
        sums[l] = (sums0, sums1, sums_o, sums2, sums_m)
    return loss_blk, dx, totals, sums


def small_all_gather(blk, name):
    m_per, n = blk.shape

    def body(x_ref, out_ref, send_sems, recv_sems, local_sem):
        x, y, c = _here()
        me, sibling = (x, y, c), (x, y, 1 - c)
        chips = [(1 - x, y), (x, 1 - y), (1 - x, 1 - y)]

        def rows(px, py, pc):
            return out_ref.at[pl.ds((4 * px + 2 * py + pc) * m_per, m_per), :]

        def copy(k, block, to, src=None):
            return pltpu.make_async_remote_copy(
                src_ref=rows(*block) if src is None else src, dst_ref=rows(*block),
                send_sem=send_sems.at[k], recv_sem=recv_sems.at[k], device_id=to, device_id_type=MESH)

        mine = pltpu.make_async_copy(x_ref, rows(*me), local_sem)
        mine.start()
        first = [copy(0, me, sibling, src=x_ref)]
        first += [copy(1 + j, me, (*chip, c), src=x_ref) for j, chip in enumerate(chips)]
        for cp in first:
            cp.start()
        passed = [copy(4 + j, (*chip, c), sibling) for j, chip in enumerate(chips)]
        for j, chip in enumerate(chips):
            copy(1 + j, (*chip, c), me).wait_recv()
            passed[j].start()
        copy(0, sibling, me).wait_recv()
        for j, chip in enumerate(chips):
            copy(4 + j, (*chip, 1 - c), me).wait_recv()
        for cp in first + passed:
            cp.wait_send()
        mine.wait()

    return pl.pallas_call(
        body, name=name,
        out_shape=jax.ShapeDtypeStruct((NDEV * m_per, n), blk.dtype),
        in_specs=[pl.BlockSpec(memory_space=pltpu.VMEM)],
        out_specs=pl.BlockSpec(memory_space=pltpu.VMEM),
        scratch_shapes=[pltpu.SemaphoreType.DMA((7,)), pltpu.SemaphoreType.DMA((7,)), pltpu.SemaphoreType.DMA],
        compiler_params=pltpu.CompilerParams(vmem_limit_bytes=VMEM_LIMIT),
    )(blk)


EW_BLOCK_BYTES = 1 << 20


def _ew_rows(rows, cols):
    want = max(16, EW_BLOCK_BYTES // (4 * cols))
    best = None
    for t in range(16, rows + 1, 16):
        if rows % t == 0 and t <= want:
            best = t
    return best if best is not None else rows


def add_half(g, recv, cflag, name):
    pieces, r, cols = g.shape
    r2 = r // 2
    tr = _ew_rows(r2, cols)
    nt = r2 // tr

    def body(c_ref, g_ref, r_ref, o_ref):
        o_ref[...] = (g_ref[...] + r_ref[...]).astype(o_ref.dtype)

    half = pl.BlockSpec((None, tr, cols), lambda j, i, c_ref: (j, i, 0))
    return pl.pallas_call(
        body, name=name,
        grid_spec=pltpu.PrefetchScalarGridSpec(
            num_scalar_prefetch=1, grid=(pieces, nt),
            in_specs=[pl.BlockSpec((None, tr, cols), lambda j, i, c_ref: (j, c_ref[0] * nt + i, 0)), half],
            out_specs=half),
        out_shape=jax.ShapeDtypeStruct((pieces, r2, cols), WIRE_DTYPE),
        compiler_params=_params(("arbitrary", "arbitrary")),
    )(cflag, g, recv)


def sum_chips(recv, name):
    _, r, cols = recv.shape
    tr = _ew_rows(r, cols)

    def body(r_ref, o_ref):
        acc = r_ref[0].astype(F32)
        for k in range(1, NCHIP):
            acc = acc + r_ref[k].astype(F32)
        o_ref[...] = acc

    return pl.pallas_call(
        body, name=name, grid=(r // tr,),
        in_specs=[pl.BlockSpec((NCHIP, tr, cols), lambda i: (0, i, 0))],
        out_specs=pl.BlockSpec((tr, cols), lambda i: (i, 0)),
        out_shape=jax.ShapeDtypeStruct((r, cols), F32),
        compiler_params=_params(("arbitrary",)),
    )(recv)


def sum_devices(rows8, name):
    def body(r_ref, o_ref):
        acc = r_ref[0:1, :]
        for k in range(1, NDEV):
            acc = acc + r_ref[k:k + 1, :]
        o_ref[...] = jnp.broadcast_to(acc, o_ref.shape)

    return pl.pallas_call(
        body, name=name, out_shape=jax.ShapeDtypeStruct(rows8.shape, F32),
        in_specs=[pl.BlockSpec(memory_space=pltpu.VMEM)], out_specs=pl.BlockSpec(memory_space=pltpu.VMEM),
        compiler_params=pltpu.CompilerParams(vmem_limit_bytes=VMEM_LIMIT),
    )(rows8)


def adamw(w, m, v, srcs, cflag, name, halves=False):
    planes, r, cols = w.shape
    rh = r // 2 if halves else r
    tr = _ew_rows(rh, cols)
    nth = rh // tr
    flat = [a for s in srcs for a in (s if halves else (s,))]
    ns = len(flat)
    per = ns // planes

    def body(c_ref, w_ref, m_ref, v_ref, *rest):
        s_refs, (g_ref, d_ref, mo_ref, vo_ref) = rest[:ns], rest[ns:]
        p, i = pl.program_id(0), pl.program_id(1)
        if halves:
            mine = jnp.logical_not(jnp.logical_xor(i >= nth, c_ref[0] == 1))
            blocks = [jnp.where(mine, s_refs[2 * k][...], s_refs[2 * k + 1][...]) for k in range(planes)]
        else:
            blocks = [s[...] for s in s_refs]
        g = blocks[0]
        for k in range(1, planes):
            g = jnp.where(p == k, blocks[k], g)
        g_ref[...] = g
        m_new = ADAM_B1 * m_ref[...] + (1.0 - ADAM_B1) * g
        v_new = ADAM_B2 * v_ref[...] + (1.0 - ADAM_B2) * (g * g)
        mo_ref[...] = m_new
        vo_ref[...] = v_new
        m_hat = m_new / (1.0 - ADAM_B1 ** ADAM_STEP)
        v_hat = v_new / (1.0 - ADAM_B2 ** ADAM_STEP)
        d_ref[...] = -ADAM_LR * (m_hat / (jnp.sqrt(v_hat) + ADAM_EPS) + ADAM_WD * w_ref[...])

    pt = pl.BlockSpec((None, tr, cols), lambda p, i: (p, i, 0))
    st = [pl.BlockSpec((tr, cols), functools.partial(lambda k, p, i: (jnp.where(p == k, i % nth, 0), 0), j // per))
          for j in range(ns)]
    return pl.pallas_call(
        body, name=name, grid=(planes, r // tr),
        in_specs=[pl.BlockSpec(memory_space=pltpu.SMEM), pt, pt, pt] + st,
        out_specs=[pt] * 4,
        out_shape=[jax.ShapeDtypeStruct(w.shape, F32)] * 4,
        compiler_params=_params(("arbitrary", "arbitrary")),
    )(cflag, w, m, v, *flat)


ADA_COLS = 9 * D // NCHIP


def mod_fwd(c_all, w_ada, b_shard, name):
    def body(c_ref, w_ref, b_ref, o_ref):
        cc = c_ref[...]
        sc = cc * jax.nn.sigmoid(cc)
        o_ref[...] = jnp.dot(sc, w_ref[...], preferred_element_type=F32,
                             precision=lax.Precision.HIGHEST) + b_ref[...]

    return pl.pallas_call(
        body, name=name, grid=(2,),
        in_specs=[pl.BlockSpec((NDEV, D), lambda l: (0, 0)),
                  pl.BlockSpec((None, D, ADA_COLS), lambda l: (l, 0, 0)),
                  pl.BlockSpec((None, 1, ADA_COLS), lambda l: (l, 0, 0))],
        out_specs=pl.BlockSpec((None, NDEV, ADA_COLS), lambda l: (l, 0, 0)),
        out_shape=jax.ShapeDtypeStruct((2, NDEV, ADA_COLS), F32),
        compiler_params=_params(("arbitrary",)),
    )(c_all, w_ada, b_shard.reshape(2, 1, ADA_COLS))


def wada_grad(c_all_t, dmod, name):
    ct = ADA_COLS // 3

    def body(c_ref, d_ref, o_ref):
        cc = c_ref[...]
        sc = cc * jax.nn.sigmoid(cc)
        acc = sc[:, 0:1] * d_ref[0:1, :]
        for b in range(1, NDEV):
            acc = acc + sc[:, b:b + 1] * d_ref[b:b + 1, :]
        o_ref[...] = acc

    return pl.pallas_call(
        body, name=name, grid=(2, 3),
        in_specs=[pl.BlockSpec((D, LANES), lambda l, j: (0, 0)),
                  pl.BlockSpec((None, NDEV, ct), lambda l, j: (l, 0, j))],
        out_specs=pl.BlockSpec((None, D, ct), lambda l, j: (l, 0, j)),
        out_shape=jax.ShapeDtypeStruct((2, D, ADA_COLS), F32),
        compiler_params=_params(("arbitrary", "arbitrary")),
    )(c_all_t, dmod)


def _pad_rows(row, rows=SUBLANES):
    return jnp.concatenate([row[None, :], jnp.zeros((rows - 1, row.shape[0]), row.dtype)], axis=0)


def kernel(x, c, w_ada, b_ada, norm_g, w_in, q_norm_g, k_norm_g, conv_w, conv_b, w_out, ffn_w1, ffn_w2, loss_target, m_w_ada, m_b_ada, m_norm_g, m_w_in, m_q_norm_g, m_k_norm_g, m_conv_w, m_conv_b, m_w_out, m_ffn_w1, m_ffn_w2, v_w_ada, v_b_ada, v_norm_g, v_w_in, v_q_norm_g, v_k_norm_g, v_conv_w, v_conv_b, v_w_out, v_ffn_w1, v_ffn_w2):
    ix, iy, ic = lax.axis_index("x"), lax.axis_index("y"), lax.axis_index("c")
    chip = 2 * ix + iy
    dev = 2 * chip + ic
    cflag = jnp.reshape(ic, (1,)).astype(jnp.int32)
    ngw = norm_g.shape[-1]
    cww = conv_w.shape[-1]

    pack = jnp.concatenate([c[0], norm_g.reshape(-1), conv_w.reshape(-1)])
    got = small_all_gather(_pad_rows(pack), "gather_c_normg_convw")[::SUBLANES]
    c_all = got[:, :D]
    per_chip = got[::2]
    ng_full = jnp.concatenate([per_chip[j, D:D + 6 * ngw].reshape(2, 3, ngw) for j in range(NCHIP)], axis=-1)
    cw_full = jnp.concatenate([per_chip[j, D + 6 * ngw:].reshape(2, 3, cww) for j in range(NCHIP)], axis=-1)

    b_shard = lax.dynamic_slice_in_dim(b_ada, chip * ADA_COLS, ADA_COLS, axis=1)
    mod_blk = mod_fwd(c_all, w_ada, b_shard, "mod_fwd").reshape(2 * NDEV, ADA_COLS)
    mod_all = small_all_gather(mod_blk, "gather_mod").reshape(NDEV, 2, NDEV, ADA_COLS)[::2]
    mod_mine = lax.dynamic_index_in_dim(mod_all, dev, axis=2, keepdims=False)
    mods = [mod_mine[:, l, :].reshape(-1) for l in range(2)]

    shards, gvecs, cws = [], [], []
    for l in range(2):
        shards.append(dict(w1=[ffn_w1[l, i].astype(MXU_DTYPE) for i in range(2)],
                           w2=[ffn_w2[l, i].astype(MXU_DTYPE) for i in range(2)],
                           win=w_in[l].astype(MXU_DTYPE), wout=w_out[l].astype(MXU_DTYPE)))
        gv = jnp.stack([jnp.tile(q_norm_g[l], AW // HD), jnp.tile(k_norm_g[l], AW // HD)])
        gvecs.append(jnp.concatenate([gv, jnp.zeros((SUBLANES - 2, AW), F32)], axis=0))
        cws.append(jnp.concatenate([cw_full[l], conv_b[l][None, :], jnp.zeros((SUBLANES - 4, CW), F32)], axis=0))
    w_first = run_carry(Carry("gather", [shards[0]["w1"][0], shards[0]["w2"][0]]), "gather_first_ffn")

    loss_blk, dx, totals, sums = local_step(x[0], loss_target[0], mods, [ng_full[0], ng_full[1]], gvecs, cws,
                                            shards, w_first, cflag)
    loss = lax.psum(loss_blk[0, 0], ("x", "y", "c"))

    dmods, dngs, dqg, dkg, dcw, dcb = [], [], [], [], [], []
    for l in range(2):
        s0, s1, so, s2, sm = sums[l]
        dmods.append(jnp.concatenate([s0[0], s0[1], s0[3], s1[0], s1[1], so[0], s2[0], s2[1], s2[3]]))
        dngs.append(jnp.concatenate([s0[2], s1[2], s2[2]]))
        dqg.append(sm[0].reshape(AW // HD, HD).sum(0))
        dkg.append(sm[1].reshape(AW // HD, HD).sum(0))
        dcw.append(sm[2:5].reshape(-1))
        dcb.append(sm[5])
    small = jnp.concatenate(dmods + dngs + dqg + dkg + dcw + dcb)
    small_all = small_all_gather(_pad_rows(small), "gather_small_grads")[::SUBLANES]
    nm = 9 * D
    dmod_all = small_all[:, :2 * nm].reshape(NDEV, 2, NCHIP, ADA_COLS)
    dmod_mine = lax.dynamic_index_in_dim(dmod_all, chip, axis=2, keepdims=False).transpose(1, 0, 2)
    tot = sum_devices(small_all, "sum_small_grads")[0]
    o = 2 * nm
    g_b_ada = tot[:o].reshape(2, nm)
    g_norm_g = lax.dynamic_slice_in_dim(tot[o:o + 6 * D].reshape(2, 3, D), chip * ngw, ngw, axis=2)
    o += 6 * D
    g_qg = tot[o:o + 2 * HD].reshape(2, HD)
    o += 2 * HD
    g_kg = tot[o:o + 2 * HD].reshape(2, HD)
    o += 2 * HD
    g_cw = lax.dynamic_slice_in_dim(tot[o:o + 6 * CW].reshape(2, 3, CW), chip * cww, cww, axis=2)
    o += 6 * CW
    g_cb = tot[o:o + 2 * CW].reshape(2, CW)

    c_all_t = jnp.concatenate([c_all.T, jnp.zeros((D, LANES - NDEV), F32)], axis=1)
    g_wada_src = wada_grad(c_all_t, dmod_mine, "wada_grad")

    def halves(k_of_plane):
        return [(totals[l][0][k], totals[l][1][k]) for l, k in k_of_plane]

    r_wada = adamw(w_ada, m_w_ada, v_w_ada, [g_wada_src[0], g_wada_src[1]], cflag, "adamw_w_ada")
    r_win = adamw(w_in, m_w_in, v_w_in, halves([(0, 2), (1, 2)]), cflag, "adamw_w_in", halves=True)
    r_wout = adamw(w_out, m_w_out, v_w_out, halves([(0, 3), (1, 3)]), cflag, "adamw_w_out", halves=True)
    r_w1 = adamw(ffn_w1.reshape(4, D, HALF), m_ffn_w1.reshape(4, D, HALF), v_ffn_w1.reshape(4, D, HALF),
                 halves([(0, 0), (0, 4), (1, 0), (1, 4)]), cflag, "adamw_ffn_w1", halves=True)
    w2r = DFF // NCHIP
    r_w2 = adamw(ffn_w2.reshape(4, w2r, D), m_ffn_w2.reshape(4, w2r, D), v_ffn_w2.reshape(4, w2r, D),
                 halves([(0, 1), (0, 5), (1, 1), (1, 5)]), cflag, "adamw_ffn_w2", halves=True)
    r_w1 = [t.reshape(ffn_w1.shape) for t in r_w1]
    r_w2 = [t.reshape(ffn_w2.shape) for t in r_w2]

    smalls = [("b_ada", b_ada, m_b_ada, v_b_ada, g_b_ada), ("norm_g", norm_g, m_norm_g, v_norm_g, g_norm_g),
              ("q_norm_g", q_norm_g, m_q_norm_g, v_q_norm_g, g_qg), ("k_norm_g", k_norm_g, m_k_norm_g, v_k_norm_g, g_kg),
              ("conv_w", conv_w, m_conv_w, v_conv_w, g_cw), ("conv_b", conv_b, m_conv_b, v_conv_b, g_cb)]
    n_small = sum(t[1].size for t in smalls)
    pad = (-n_small) % (16 * LANES)

    def packed(idx):
        flat = jnp.concatenate([t[idx].reshape(-1) for t in smalls] + [jnp.zeros((pad,), F32)])
        return flat.reshape(-1, LANES)

    r_small = adamw(packed(1)[None], packed(2)[None], packed(3)[None], [packed(4)], cflag, "adamw_small")
    small_out = {}
    o = 0
    for name_, w_, _, _, _ in smalls:
        small_out[name_] = [t.reshape(-1)[o:o + w_.size].reshape(w_.shape) for t in r_small]
        o += w_.size

    res = {"w_ada": r_wada, "w_in": r_win, "w_out": r_wout, "ffn_w1": r_w1, "ffn_w2": r_w2, **small_out}
    order = ["w_ada", "b_ada", "norm_g", "w_in", "q_norm_g", "k_norm_g", "conv_w", "conv_b", "w_out", "ffn_w1", "ffn_w2"]
    outs = [loss, dx[None]]
    for k in range(4):
        outs += [res[nm_][k] for nm_ in order]
    return tuple(outs)
```

```python
import functools

import jax
import jax.numpy as jnp
from jax import lax
from jax.experimental import pallas as pl
from jax.experimental.pallas import tpu as pltpu

F32 = jnp.float32
MXU_DTYPE = jnp.bfloat16
ACT_DTYPE = jnp.bfloat16
WIRE_DTYPE = jnp.bfloat16

D = 1024
HD = 64
AW = 512
CW = 512
DFF = 2816
HALF = DFF // 2
INC = 3 * AW + 3 * CW
NCHIP = 4
NDEV = 8
QBLK = 128
ATTN_QBLOCKS = 4
ATTN_CHUNK_ROWS = 2048
DILATIONS = (1, 4, 16)
EPS = 1e-6
NEG = -1e30
LANES = 128
SUBLANES = 8
VMEM_LIMIT = 56 * 1024 * 1024

ADAM_LR = 0.001
ADAM_B1 = 0.9
ADAM_B2 = 0.999
ADAM_EPS = 1e-08
ADAM_WD = 0.01
ADAM_STEP = 10

NT_DIMS = (((1,), (1,)), ((), ()))
TN_DIMS = (((0,), (0,)), ((), ()))


def _params(sem, vmem=VMEM_LIMIT):
    return pltpu.CompilerParams(dimension_semantics=sem, vmem_limit_bytes=vmem)


def _row_tile(n, want):
    t = min(n, want)
    assert n % t == 0
    return t


def _ada(xt, vec_ref):
    ng, sc, sh, gt = vec_ref[0:1, :], vec_ref[1:2, :], vec_ref[2:3, :], vec_ref[3:4, :]
    r = lax.rsqrt(jnp.mean(xt * xt, axis=-1, keepdims=True) + EPS)
    return xt * r, r, ng * (1.0 + sc), ng, sc, sh, gt


def _ada_bwd(dh, xhat, r, gain, ng, sc):
    dshift = jnp.sum(dh, axis=0, keepdims=True)
    dhx = dh * xhat
    dscale = jnp.sum(dhx, axis=0, keepdims=True) * ng
    dng = jnp.sum(dhx, axis=0, keepdims=True) * (1.0 + sc)
    dxhat = dh * gain
    dx = r * (dxhat - xhat * jnp.mean(dxhat * xhat, axis=-1, keepdims=True))
    return dx, dshift, dscale, dng


def _acc_rows(sums_ref, first, rows):
    @pl.when(first)
    def _():
        sums_ref[...] = jnp.zeros_like(sums_ref)
    for k, row in enumerate(rows):
        sums_ref[k:k + 1, :] += row


MESH = pl.DeviceIdType.MESH
ANY = pl.BlockSpec(memory_space=pl.ANY)


def _here():
    return lax.axis_index("x"), lax.axis_index("y"), lax.axis_index("c")


def _ici_copies(src_refs, dst_refs, send_sems, recv_sems, local_sems, scatter):
    x, y, c = _here()
    my_chip = 2 * x + y
    peers = [(1 - x, y), (x, 1 - y), (1 - x, 1 - y)]
    local, out, inc = [], [], []
    for a, (src, dst) in enumerate(zip(src_refs, dst_refs)):
        local.append(pltpu.make_async_copy(src.at[my_chip] if scatter else src, dst.at[my_chip], local_sems.at[a]))
        for j, (px, py) in enumerate(peers):
            sems = dict(send_sem=send_sems.at[3 * a + j], recv_sem=recv_sems.at[3 * a + j],
                        device_id=(px, py, c), device_id_type=MESH)
            out.append(pltpu.make_async_remote_copy(
                src_ref=src.at[2 * px + py] if scatter else src, dst_ref=dst.at[my_chip], **sems))
            inc.append(pltpu.make_async_remote_copy(
                src_ref=src.at[my_chip] if scatter else src, dst_ref=dst.at[2 * px + py], **sems))
    return local, out, inc


def _swap_copies(src_refs, dst_refs, send_sems, recv_sems, halves):
    x, y, c = _here()
    cps = []
    for k, (src, dst) in enumerate(zip(src_refs, dst_refs)):
        if halves:
            r2 = src.shape[1] // 2
            src = src.at[:, pl.ds((1 - c) * r2, r2), :]
        cps.append(pltpu.make_async_remote_copy(
            src_ref=src, dst_ref=dst, send_sem=send_sems.at[k], recv_sem=recv_sems.at[k],
            device_id=(x, y, 1 - c), device_id_type=MESH))
    return cps


class Carry:
    def __init__(self, kind, srcs):
        self.kind, self.srcs, n = kind, list(srcs), len(srcs)
        if kind == "gather":
            shapes = [(NCHIP,) + s.shape for s in srcs]
        elif kind == "swap_halves":
            shapes = [(s.shape[0], s.shape[1] // 2, s.shape[2]) for s in srcs]
        else:
            shapes = [s.shape for s in srcs]
        self.out_shape = [jax.ShapeDtypeStruct(sh, s.dtype) for sh, s in zip(shapes, srcs)]
        dma = pltpu.SemaphoreType.DMA
        self.sems = [dma((3 * n,)), dma((3 * n,)), dma((n,))] if kind in ("gather", "scatter") else [dma((n,)), dma((n,))]

    def start(self, srcs, dsts, sems):
        if self.kind in ("gather", "scatter"):
            local, out, _ = _ici_copies(srcs, dsts, *sems, self.kind == "scatter")
            for cp in local + out:
                cp.start()
        else:
            for cp in _swap_copies(srcs, dsts, *sems, self.kind == "swap_halves"):
                cp.start()

    def wait(self, srcs, dsts, sems):
        if self.kind in ("gather", "scatter"):
            local, out, inc = _ici_copies(srcs, dsts, *sems, self.kind == "scatter")
            for cp in inc:
                cp.wait_recv()
            for cp in out:
                cp.wait_send()
            for cp in local:
                cp.wait()
        else:
            cps = _swap_copies(srcs, dsts, *sems, self.kind == "swap_halves")
            for cp in cps:
                cp.wait_recv()
            for cp in cps:
                cp.wait_send()


def run_carry(carry, name):
    n = len(carry.srcs)

    def body(*refs):
        srcs, dsts, sems = refs[:n], refs[n:2 * n], refs[2 * n:]
        carry.start(srcs, dsts, sems)
        carry.wait(srcs, dsts, sems)

    return pl.pallas_call(body, name=name, out_shape=carry.out_shape, in_specs=[ANY] * n, out_specs=[ANY] * n,
                          scratch_shapes=carry.sems)(*carry.srcs)


def _pcall(body, name, grid, in_specs, out_specs, out_shape, sem, args, carry=None, scratch=()):
    if carry is None:
        outs = pl.pallas_call(body, name=name, grid=grid, in_specs=in_specs, out_specs=out_specs,
                              out_shape=out_shape, scratch_shapes=list(scratch), compiler_params=_params(sem))(*args)
        return outs, []
    n_in, n_out, nc, ns = len(in_specs), len(out_specs), len(carry.srcs), len(scratch)

    def wrapped(*refs):
        ins, csrc = refs[:n_in], refs[n_in:n_in + nc]
        outs, cdst = refs[n_in + nc:n_in + nc + n_out], refs[n_in + nc + n_out:n_in + 2 * nc + n_out]
        own = refs[n_in + 2 * nc + n_out:n_in + 2 * nc + n_out + ns]
        sems = refs[n_in + 2 * nc + n_out + ns:]
        ids = [pl.program_id(a) for a in range(len(grid))]
        first = functools.reduce(jnp.logical_and, [i == 0 for i in ids])
        last = functools.reduce(jnp.logical_and, [i == g - 1 for i, g in zip(ids, grid)])

        @pl.when(first)
        def _():
            carry.start(csrc, cdst, sems)

        body(*ins, *outs, *own)

        @pl.when(last)
        def _():
            carry.wait(csrc, cdst, sems)

    res = pl.pallas_call(
        wrapped, name=name, grid=grid,
        in_specs=list(in_specs) + [ANY] * nc, out_specs=list(out_specs) + [ANY] * nc,
        out_shape=list(out_shape) + carry.out_shape,
        scratch_shapes=list(scratch) + carry.sems, compiler_params=_params(sem),
    )(*args, *carry.srcs)
    return res[:n_out], res[n_out:]


def ffn_fwd(x, vec, w1p, w2, gs, name, carry=None):
    S = x.shape[0]
    tm = _row_tile(S, 512)

    def body(x_ref, vec_ref, w1_ref, w2_ref, xn_ref, a_ref, f_ref):
        xt = x_ref[...]
        xhat, _, gain, _, _, sh, gt = _ada(xt, vec_ref)
        h = (xhat * gain + sh).astype(MXU_DTYPE)
        f = jnp.zeros((tm, D), F32)
        for hf in range(2):
            g = jnp.dot(h, w1_ref[hf], preferred_element_type=F32)
            up = jnp.dot(h, w1_ref[2 + hf], preferred_element_type=F32)
            a_ref[:, hf * HALF:(hf + 1) * HALF] = g.astype(a_ref.dtype)
            a_ref[:, DFF + hf * HALF:DFF + (hf + 1) * HALF] = up.astype(a_ref.dtype)
            act = (g * jax.nn.sigmoid(g) * up).astype(MXU_DTYPE)
            f = f + jnp.dot(act, w2_ref[hf * HALF:(hf + 1) * HALF, :], preferred_element_type=F32)
        xn_ref[...] = xt + (gs * gt) * f
        f_ref[...] = f.astype(f_ref.dtype)

    return _pcall(
        body, name, (S // tm,),
        [pl.BlockSpec((tm, D), lambda i: (i, 0)),
         pl.BlockSpec((SUBLANES, D), lambda i: (0, 0)),
         pl.BlockSpec((NCHIP, D, HALF), lambda i: (0, 0, 0), pipeline_mode=pl.Buffered(1)),
         pl.BlockSpec((DFF, D), lambda i: (0, 0), pipeline_mode=pl.Buffered(1))],
        [pl.BlockSpec((tm, D), lambda i: (i, 0)),
         pl.BlockSpec((tm, 2 * DFF), lambda i: (i, 0)),
         pl.BlockSpec((tm, D), lambda i: (i, 0))],
        [jax.ShapeDtypeStruct((S, D), F32),
         jax.ShapeDtypeStruct((S, 2 * DFF), ACT_DTYPE),
         jax.ShapeDtypeStruct((S, D), ACT_DTYPE)],
        ("arbitrary",), (x, vec, w1p, w2), carry)


def ffn_bwd(dxo, x, a, f, vec, w1p, w2, gs, name, carry=None):
    S = x.shape[0]
    tm = _row_tile(S, 256)

    def body(dxo_ref, x_ref, a_ref, f_ref, vec_ref, w1_ref, w2_ref,
             dxi_ref, hb_ref, dfb_ref, act_ref, da_ref, sums_ref):
        xt = x_ref[...]
        dxo = dxo_ref[...]
        xhat, r, gain, ng, sc, sh, gt = _ada(xt, vec_ref)
        hb_ref[...] = (xhat * gain + sh).astype(hb_ref.dtype)
        dgate = gs * jnp.sum(dxo * f_ref[...].astype(F32), axis=0, keepdims=True)
        df = ((gs * gt) * dxo).astype(MXU_DTYPE)
        dfb_ref[...] = df
        dh = jnp.zeros((tm, D), F32)
        for hf in range(2):
            lo, hi = hf * HALF, (hf + 1) * HALF
            dact = lax.dot_general(df, w2_ref[lo:hi, :], NT_DIMS, preferred_element_type=F32)
            g = a_ref[:, lo:hi].astype(F32)
            up = a_ref[:, DFF + lo:DFF + hi].astype(F32)
            sg = jax.nn.sigmoid(g)
            si = g * sg
            act_ref[:, lo:hi] = (si * up).astype(act_ref.dtype)
            dg = (dact * up * (sg * (1.0 + g * (1.0 - sg)))).astype(MXU_DTYPE)
            dup = (dact * si).astype(MXU_DTYPE)
            da_ref[:, lo:hi] = dg
            da_ref[:, DFF + lo:DFF + hi] = dup
            dh = dh + lax.dot_general(dg, w1_ref[hf], NT_DIMS, preferred_element_type=F32)
            dh = dh + lax.dot_general(dup, w1_ref[2 + hf], NT_DIMS, preferred_element_type=F32)
        dx, dshift, dscale, dng = _ada_bwd(dh, xhat, r, gain, ng, sc)
        dxi_ref[...] = dxo + dx
        _acc_rows(sums_ref, pl.program_id(0) == 0, (dshift, dscale, dng, dgate))

    return _pcall(
        body, name, (S // tm,),
        [pl.BlockSpec((tm, D), lambda i: (i, 0)),
         pl.BlockSpec((tm, D), lambda i: (i, 0)),
         pl.BlockSpec((tm, 2 * DFF), lambda i: (i, 0)),
         pl.BlockSpec((tm, D), lambda i: (i, 0)),
         pl.BlockSpec((SUBLANES, D), lambda i: (0, 0)),
         pl.BlockSpec((NCHIP, D, HALF), lambda i: (0, 0, 0), pipeline_mode=pl.Buffered(1)),
         pl.BlockSpec((DFF, D), lambda i: (0, 0), pipeline_mode=pl.Buffered(1))],
        [pl.BlockSpec((tm, D), lambda i: (i, 0)),
         pl.BlockSpec((tm, D), lambda i: (i, 0)),
         pl.BlockSpec((tm, D), lambda i: (i, 0)),
         pl.BlockSpec((tm, DFF), lambda i: (i, 0)),
         pl.BlockSpec((tm, 2 * DFF), lambda i: (i, 0)),
         pl.BlockSpec((SUBLANES, D), lambda i: (0, 0))],
        [jax.ShapeDtypeStruct((S, D), F32),
         jax.ShapeDtypeStruct((S, D), MXU_DTYPE),
         jax.ShapeDtypeStruct((S, D), MXU_DTYPE),
         jax.ShapeDtypeStruct((S, DFF), MXU_DTYPE),
         jax.ShapeDtypeStruct((S, 2 * DFF), MXU_DTYPE),
         jax.ShapeDtypeStruct((SUBLANES, D), F32)],
        ("arbitrary",), (dxo, x, a, f, vec, w1p, w2), carry)


def wgrad(a, b, kt, nt, name, carry=None):
    T, K = a.shape
    N = b.shape[1]
    pk, pn = K // kt, N // nt
    assert pk == 1 or pn == 1
    tt = _row_tile(T, 1024)
    steps = T // tt

    def body(a_ref, b_ref, o_ref):
        @pl.when(pl.program_id(1) == 0)
        def _():
            o_ref[...] = jnp.zeros_like(o_ref)
        o_ref[...] += lax.dot_general(a_ref[...], b_ref[...], TN_DIMS, preferred_element_type=F32)

    a_map = (lambda p, t: (t, p)) if pk > 1 else (lambda p, t: (t, 0))
    b_map = (lambda p, t: (t, p)) if pn > 1 else (lambda p, t: (t, 0))
    (out,), got = _pcall(
        body, name, (pk * pn, steps),
        [pl.BlockSpec((tt, kt), a_map), pl.BlockSpec((tt, nt), b_map)],
        [pl.BlockSpec((None, kt, nt), lambda p, t: (p, 0, 0))],
        [jax.ShapeDtypeStruct((pk * pn, kt, nt), F32)], ("arbitrary", "arbitrary"), (a, b), carry)
    return out, got


def _head_masks(rows):
    lane = lax.broadcasted_iota(jnp.int32, (rows, LANES), 1)
    return lane < HD


def _pair_stat(x, m_a):
    s_a = jnp.sum(jnp.where(m_a, x, 0.0), axis=1, keepdims=True)
    s_b = jnp.sum(jnp.where(m_a, 0.0, x), axis=1, keepdims=True)
    return s_a, s_b


def mixer_in(x, vec, winp, gvec, name):
    S = x.shape[0]
    tm = _row_tile(S, 512)
    pc = INC // NCHIP

    def body(x_ref, vec_ref, w_ref, g_ref, proj_ref, hb_ref, qn_ref, kn_ref, v_ref):
        xt = x_ref[...]
        xhat, _, gain, _, _, sh, _ = _ada(xt, vec_ref)
        h = (xhat * gain + sh).astype(MXU_DTYPE)
        hb_ref[...] = h
        for j in range(NCHIP):
            proj_ref[:, j * pc:(j + 1) * pc] = jnp.dot(h, w_ref[j], preferred_element_type=F32)
        m_a = _head_masks(tm)
        for which, dst in ((0, qn_ref), (1, kn_ref)):
            for p in range(AW // LANES):
                lo = which * AW + p * LANES
                xp = proj_ref[:, lo:lo + LANES]
                s_a, s_b = _pair_stat(xp * xp, m_a)
                rr = jnp.where(m_a, lax.rsqrt(s_a * (1.0 / HD) + EPS), lax.rsqrt(s_b * (1.0 / HD) + EPS))
                gp = g_ref[which:which + 1, p * LANES:(p + 1) * LANES]
                dst[:, p * LANES:(p + 1) * LANES] = (xp * rr * gp).astype(dst.dtype)
        v_ref[...] = proj_ref[:, 2 * AW:3 * AW].astype(v_ref.dtype)

    return pl.pallas_call(
        body, name=name, grid=(S // tm,),
        in_specs=[pl.BlockSpec((tm, D), lambda i: (i, 0)),
                  pl.BlockSpec((SUBLANES, D), lambda i: (0, 0)),
                  pl.BlockSpec((NCHIP, D, pc), lambda i: (0, 0, 0), pipeline_mode=pl.Buffered(1)),
                  pl.BlockSpec((SUBLANES, AW), lambda i: (0, 0))],
        out_specs=[pl.BlockSpec((tm, INC), lambda i: (i, 0)),
                   pl.BlockSpec((tm, D), lambda i: (i, 0)),
                   pl.BlockSpec((tm, AW), lambda i: (i, 0)),
                   pl.BlockSpec((tm, AW), lambda i: (i, 0)),
                   pl.BlockSpec((tm, AW), lambda i: (i, 0))],
        out_shape=[jax.ShapeDtypeStruct((S, INC), F32),
                   jax.ShapeDtypeStruct((S, D), MXU_DTYPE),
                   jax.ShapeDtypeStruct((S, AW), F32),
                   jax.ShapeDtypeStruct((S, AW), F32),
                   jax.ShapeDtypeStruct((S, AW), F32)],
        compiler_params=_params(("arbitrary",)),
    )(x, vec, winp, gvec)


def _band_masks(ncol):
    row = lax.broadcasted_iota(jnp.int32, (2 * QBLK, ncol), 0) & (QBLK - 1)
    col = lax.broadcasted_iota(jnp.int32, (2 * QBLK, ncol), 1)
    return row, col


def _stack_heads(t, m_a):
    zero = jnp.zeros_like(t)
    return jnp.concatenate([jnp.where(m_a, t, zero), jnp.where(m_a, zero, t)], axis=0)


class _AttnLayout:
    def __init__(self, d, S):
        self.d, self.S = d, S
        self.qb = max(1, min(ATTN_QBLOCKS, ATTN_CHUNK_ROWS // (QBLK * d)))
        self.nres = d
        self.nchunk = S // (self.qb * QBLK * d)
        self.grid = (AW // LANES, self.nchunk)
        self.unroll = max(1, min(d, ATTN_QBLOCKS // self.qb))

    def _spec(self, blocks, row_of):
        return pl.BlockSpec((blocks * QBLK * self.d, LANES), lambda hp, j: (row_of(j), hp))

    def cur(self, chunk_of):
        return self._spec(self.qb, chunk_of)

    def prev(self, chunk_of):
        return self._spec(1, lambda j: jnp.maximum(chunk_of(j) * self.qb - 1, 0))

    def idx(self, b, r):
        if self.d == 1:
            return (pl.ds(b * QBLK, QBLK), slice(None))
        return (pl.ds(b * QBLK * self.d + r, QBLK, stride=self.d), slice(None))

    def per_residue(self, fn):
        if self.nres == 1:
            fn(0)
        else:
            def step(it, carry):
                for k in range(self.unroll):
                    fn(it * self.unroll + k)
                return carry
            lax.fori_loop(0, self.nres // self.unroll, step, 0)


def attn_fwd(qn, kn, v, d, name, carry=None):
    S = qn.shape[0]
    lay = _AttnLayout(d, S)
    qb = lay.qb

    def body(q_ref, kc_ref, kp_ref, vc_ref, vp_ref, o_ref, lse_ref):
        i = pl.program_id(1)
        m_a = _head_masks(QBLK)
        row, col = _band_masks(2 * QBLK)
        dist = row + QBLK - col
        band = (dist >= 0) & (dist <= QBLK)
        first = band & ((i > 0) | (col >= QBLK))

        def residue(r):
            kt = [kp_ref[lay.idx(0, r)].astype(MXU_DTYPE)]
            vt = [vp_ref[lay.idx(0, r)].astype(MXU_DTYPE)]
            for b in range(qb):
                kt.append(kc_ref[lay.idx(b, r)].astype(MXU_DTYPE))
                vt.append(vc_ref[lay.idx(b, r)].astype(MXU_DTYPE))
            for b in range(qb):
                rows = lay.idx(b, r)
                q = (q_ref[rows] * (HD ** -0.5)).astype(MXU_DTYPE)
                kcat = jnp.concatenate([kt[b], kt[b + 1]], axis=0)
                vcat = jnp.concatenate([vt[b], vt[b + 1]], axis=0)
                mask = first if b == 0 else band
                s = lax.dot_general(_stack_heads(q, m_a), kcat, NT_DIMS, preferred_element_type=F32)
                s = jnp.where(mask, s, NEG)
                m = jnp.max(s, axis=1, keepdims=True)
                p = jnp.exp(s - m)
                l = jnp.sum(p, axis=1, keepdims=True)
                o = jnp.dot(p.astype(MXU_DTYPE), vcat, preferred_element_type=F32) / l
                lse = jnp.broadcast_to(m + jnp.log(l), (2 * QBLK, LANES))
                o_ref[rows] = jnp.where(m_a, o[:QBLK], o[QBLK:])
                lse_ref[rows] = jnp.where(m_a, lse[:QBLK], lse[QBLK:])

        lay.per_residue(residue)

    cur, prev = lay.cur(lambda j: j), lay.prev(lambda j: j)
    return _pcall(body, name, lay.grid, [cur, cur, prev, cur, prev], [cur, cur],
                  [jax.ShapeDtypeStruct((S, AW), F32)] * 2, ("arbitrary", "arbitrary"), (qn, kn, kn, v, v), carry)


def _both_heads(t, m_a):
    other = pltpu.roll(t, HD, 1)
    return jnp.concatenate([jnp.where(m_a, t, other), jnp.where(m_a, other, t)], axis=0)


def attn_bwd(qn, kn, v, dycat, lse, delta, d, name, carry=None):
    S = qn.shape[0]
    lay = _AttnLayout(d, S)
    qb, nchunk = lay.qb, lay.nchunk

    def body(q_ref, kc_ref, kp_ref, vc_ref, vp_ref, do_ref, lse_ref, dl_ref,
             dq_ref, dk_ref, dv_ref, ck_ref, cv_ref):
        j = pl.program_id(1)
        i = nchunk - 1 - j
        m_a = _head_masks(QBLK)
        row, col = _band_masks(2 * QBLK)
        dist = row + QBLK - col
        band = (dist >= 0) & (dist <= QBLK)
        first = band & ((i > 0) | (col >= QBLK))

        def residue(r):
            def tiles(ref, cast):
                out = [ref[lay.idx(b, r)] for b in range(qb)]
                return [t.astype(MXU_DTYPE) for t in out] if cast else out

            def ktiles(cur_ref, prev_ref):
                return [prev_ref[lay.idx(0, r)].astype(MXU_DTYPE)] + tiles(cur_ref, True)

            qt = [(t * (HD ** -0.5)).astype(MXU_DTYPE) for t in tiles(q_ref, False)]
            dot_ = tiles(do_ref, True)
            lse_t = tiles(lse_ref, False)
            dl_t = tiles(dl_ref, False)
            kt = ktiles(kc_ref, kp_ref)
            vt = ktiles(vc_ref, vp_ref)
            dk_acc = [jnp.zeros((QBLK, LANES), F32) for _ in range(qb)]
            dv_acc = [jnp.zeros((QBLK, LANES), F32) for _ in range(qb)]
            crow = pl.ds(0, QBLK) if lay.nres == 1 else pl.ds(pl.multiple_of(r * QBLK, QBLK), QBLK)
            dk_acc[qb - 1] = jnp.where(j > 0, ck_ref[crow, :], 0.0)
            dv_acc[qb - 1] = jnp.where(j > 0, cv_ref[crow, :], 0.0)
            for x in range(qb):
                kcat = jnp.concatenate([kt[x], kt[x + 1]], axis=0)
                vcat = jnp.concatenate([vt[x], vt[x + 1]], axis=0)
                q2 = _stack_heads(qt[x], m_a)
                do2 = _stack_heads(dot_[x], m_a)
                lse2 = _both_heads(lse_t[x], m_a)
                dl2 = _both_heads(dl_t[x], m_a)
                lse2 = jnp.concatenate([lse2, lse2], axis=1)
                dl2 = jnp.concatenate([dl2, dl2], axis=1)
                s = lax.dot_general(q2, kcat, NT_DIMS, preferred_element_type=F32)
                p = jnp.exp(jnp.where(first if x == 0 else band, s, NEG) - lse2)
                dp = lax.dot_general(do2, vcat, NT_DIMS, preferred_element_type=F32)
                ds = p * (dp - dl2)
                dq = jnp.dot(ds.astype(MXU_DTYPE), kcat, preferred_element_type=F32)
                dq_ref[lay.idx(x, r)] = jnp.where(m_a, dq[:QBLK], dq[QBLK:]) * (HD ** -0.5)
                dk = jnp.dot(ds.T.astype(MXU_DTYPE), q2, preferred_element_type=F32)
                dv = jnp.dot(p.T.astype(MXU_DTYPE), do2, preferred_element_type=F32)
                if x == 0:
                    ck_ref[crow, :] = dk[:QBLK]
                    cv_ref[crow, :] = dv[:QBLK]
                else:
                    dk_acc[x - 1] = dk_acc[x - 1] + dk[:QBLK]
                    dv_acc[x - 1] = dv_acc[x - 1] + dv[:QBLK]
                dk_acc[x] = dk_acc[x] + dk[QBLK:]
                dv_acc[x] = dv_acc[x] + dv[QBLK:]
            for kb in range(qb):
                dk_ref[lay.idx(kb, r)] = dk_acc[kb]
                dv_ref[lay.idx(kb, r)] = dv_acc[kb]

        lay.per_residue(residue)

    cur, prev = lay.cur(lambda j: nchunk - 1 - j), lay.prev(lambda j: nchunk - 1 - j)
    carried = pltpu.VMEM((lay.nres * QBLK, LANES), F32)
    return _pcall(
        body, name, lay.grid, [cur, cur, prev, cur, prev, cur, cur, cur], [cur, cur, cur],
        [jax.ShapeDtypeStruct((S, AW), F32)] * 3, ("arbitrary", "arbitrary"),
        (qn, kn, kn, v, v, dycat, lse, delta), carry, scratch=[carried, carried])


def _shift_down(x, halo_prev, k, row):
    tm = x.shape[0]
    tail = jnp.concatenate([pltpu.roll(halo_prev, k, 0), jnp.zeros((tm - SUBLANES, x.shape[1]), x.dtype)], axis=0)
    return jnp.where(row < k, tail, pltpu.roll(x, k, 0))


def _shift_up(x, halo_next, k, row):
    tm = x.shape[0]
    head = jnp.concatenate([jnp.zeros((tm - SUBLANES, x.shape[1]), x.dtype), pltpu.roll(halo_next, SUBLANES - k, 0)], axis=0)
    return jnp.where(row >= tm - k, head, pltpu.roll(x, tm - k, 0))


def _conv_fwd(cu, halo_cu, cw_ref, row):
    u1 = _shift_down(cu, halo_cu, 1, row)
    u2 = _shift_down(cu, halo_cu, 2, row)
    cv = cw_ref[0:1, :] * u2 + cw_ref[1:2, :] * u1 + cw_ref[2:3, :] * cu + cw_ref[3:4, :]
    return cv, u1, u2


def combine_conv(os_, lses, proj, cw, name, carry=None):
    S = proj.shape[0]
    tm = _row_tile(S, 512)
    hb = tm // SUBLANES

    def body(o1, o2, o3, l1, l2, l3, pc_ref, ph_ref, cw_ref, ycat_ref, lse_ref):
        i = pl.program_id(0)
        for p in range(AW // LANES):
            cs = slice(p * LANES, (p + 1) * LANES)
            ls = [l[:, cs] for l in (l1, l2, l3)]
            mx = jnp.maximum(jnp.maximum(ls[0], ls[1]), ls[2])
            t = mx + jnp.log(jnp.exp(ls[0] - mx) + jnp.exp(ls[1] - mx) + jnp.exp(ls[2] - mx))
            lse_ref[:, cs] = t
            acc = jnp.zeros((tm, LANES), F32)
            for l, o in zip(ls, (o1, o2, o3)):
                acc = acc + jnp.exp(l - t) * o[:, cs]
            ycat_ref[:, cs] = acc.astype(ycat_ref.dtype)
        row = lax.broadcasted_iota(jnp.int32, (tm, CW), 0)
        gb, gc, u = pc_ref[:, 0:CW], pc_ref[:, CW:2 * CW], pc_ref[:, 2 * CW:3 * CW]
        halo_cu = jnp.where(i > 0, ph_ref[:, CW:2 * CW] * ph_ref[:, 2 * CW:3 * CW], 0.0)
        cv, _, _ = _conv_fwd(gc * u, halo_cu, cw_ref, row)
        ycat_ref[:, AW:AW + CW] = (gb * cv).astype(ycat_ref.dtype)

    ot = pl.BlockSpec((tm, AW), lambda i: (i, 0))
    return _pcall(
        body, name, (S // tm,),
        [ot] * 6 + [pl.BlockSpec((tm, 3 * CW), lambda i: (i, 1)),
                    pl.BlockSpec((SUBLANES, 3 * CW), lambda i: (jnp.maximum(i * hb - 1, 0), 1)),
                    pl.BlockSpec((SUBLANES, CW), lambda i: (0, 0))],
        [pl.BlockSpec((tm, D), lambda i: (i, 0)), ot],
        [jax.ShapeDtypeStruct((S, D), ACT_DTYPE), jax.ShapeDtypeStruct((S, AW), F32)],
        ("arbitrary",), (*os_, *lses, proj, proj, cw), carry)


def out_proj(ycat, x, vec, wout, name):
    S = x.shape[0]
    tm = _row_tile(S, 512)

    def body(yc_ref, x_ref, vec_ref, w_ref, xn_ref, y_ref):
        y = jnp.dot(yc_ref[...].astype(MXU_DTYPE), w_ref[...], preferred_element_type=F32)
        xn_ref[...] = x_ref[...] + vec_ref[3:4, :] * y
        y_ref[...] = y.astype(y_ref.dtype)

    t = pl.BlockSpec((tm, D), lambda i: (i, 0))
    return pl.pallas_call(
        body, name=name, grid=(S // tm,),
        in_specs=[t, t, pl.BlockSpec((SUBLANES, D), lambda i: (0, 0)),
                  pl.BlockSpec((D, D), lambda i: (0, 0))],
        out_specs=[t, t],
        out_shape=[jax.ShapeDtypeStruct((S, D), F32), jax.ShapeDtypeStruct((S, D), ACT_DTYPE)],
        compiler_params=_params(("arbitrary",)),
    )(ycat, x, vec, wout)


def out_proj_bwd(dxo, y, ycat, vec, wout, name, carry=None):
    S = dxo.shape[0]
    tm = _row_tile(S, 512)

    def body(dxo_ref, y_ref, yc_ref, vec_ref, w_ref, dyb_ref, dyc_ref, dl_ref, sums_ref):
        dxo = dxo_ref[...]
        dgate = jnp.sum(dxo * y_ref[...].astype(F32), axis=0, keepdims=True)
        dy = (vec_ref[3:4, :] * dxo).astype(MXU_DTYPE)
        dyb_ref[...] = dy
        dyc_ref[...] = lax.dot_general(dy, w_ref[...], NT_DIMS, preferred_element_type=F32)
        m_a = _head_masks(tm)
        for p in range(AW // LANES):
            cs = slice(p * LANES, (p + 1) * LANES)
            s_a, s_b = _pair_stat(dyc_ref[:, cs] * yc_ref[:, cs].astype(F32), m_a)
            dl_ref[:, cs] = jnp.where(m_a, s_a, s_b)
        _acc_rows(sums_ref, pl.program_id(0) == 0, (dgate,))

    t = pl.BlockSpec((tm, D), lambda i: (i, 0))
    at = pl.BlockSpec((tm, AW), lambda i: (i, 0))
    return _pcall(
        body, name, (S // tm,),
        [t, t, t, pl.BlockSpec((SUBLANES, D), lambda i: (0, 0)), pl.BlockSpec((D, D), lambda i: (0, 0))],
        [t, t, at, pl.BlockSpec((SUBLANES, D), lambda i: (0, 0))],
        [jax.ShapeDtypeStruct((S, D), MXU_DTYPE), jax.ShapeDtypeStruct((S, D), F32),
         jax.ShapeDtypeStruct((S, AW), F32), jax.ShapeDtypeStruct((SUBLANES, D), F32)],
        ("arbitrary",), (dxo, y, ycat, vec, wout), carry)


def mixer_mid_bwd(dqs, dks, dvs, proj, dycat, gvec, cw, name, carry=None):
    S = proj.shape[0]
    tm = _row_tile(S, 256)
    hb = tm // SUBLANES
    nsl = S // SUBLANES
    ntile = S // tm

    def body(dq1, dq2, dq3, dk1, dk2, dk3, dv1, dv2, dv3, pr_ref, pp_ref, pn_ref, dyc_ref, dyn_ref,
             g_ref, cw_ref, dp_ref, sums_ref):
        i = pl.program_id(0)
        m_a = _head_masks(tm)
        gsum = []
        for which, parts in ((0, (dq1, dq2, dq3)), (1, (dk1, dk2, dk3))):
            acc_g = []
            for p in range(AW // LANES):
                lo = which * AW + p * LANES
                cs = slice(p * LANES, (p + 1) * LANES)
                xp = pr_ref[:, lo:lo + LANES]
                s_a, s_b = _pair_stat(xp * xp, m_a)
                rr = jnp.where(m_a, lax.rsqrt(s_a * (1.0 / HD) + EPS), lax.rsqrt(s_b * (1.0 / HD) + EPS))
                xh = xp * rr
                dn = parts[0][:, cs] + parts[1][:, cs] + parts[2][:, cs]
                acc_g.append(jnp.sum(dn * xh, axis=0, keepdims=True))
                t = dn * g_ref[which:which + 1, cs]
                t_a, t_b = _pair_stat(t * xh, m_a)
                mean = jnp.where(m_a, t_a, t_b) * (1.0 / HD)
                dp_ref[:, lo:lo + LANES] = (rr * (t - xh * mean)).astype(dp_ref.dtype)
            gsum.append(jnp.concatenate(acc_g, axis=1))
        dp_ref[:, 2 * AW:3 * AW] = (dv1[...] + dv2[...] + dv3[...]).astype(dp_ref.dtype)
        row = lax.broadcasted_iota(jnp.int32, (tm, CW), 0)
        base = 3 * AW
        gb, gc, u = pr_ref[:, base:base + CW], pr_ref[:, base + CW:base + 2 * CW], pr_ref[:, base + 2 * CW:base + 3 * CW]
        cu = gc * u
        halo_cu = jnp.where(i > 0, pp_ref[:, CW:2 * CW] * pp_ref[:, 2 * CW:3 * CW], 0.0)
        cv, u1, u2 = _conv_fwd(cu, halo_cu, cw_ref, row)
        dyc = dyc_ref[...]
        dp_ref[:, base:base + CW] = (dyc * cv).astype(dp_ref.dtype)
        dcv = dyc * gb
        halo_dcv = jnp.where(i < ntile - 1, dyn_ref[...] * pn_ref[:, 0:CW], 0.0)
        d1 = _shift_up(dcv, halo_dcv, 1, row)
        d2 = _shift_up(dcv, halo_dcv, 2, row)
        dcu = cw_ref[2:3, :] * dcv + cw_ref[1:2, :] * d1 + cw_ref[0:1, :] * d2
        dp_ref[:, base + CW:base + 2 * CW] = (dcu * u).astype(dp_ref.dtype)
        dp_ref[:, base + 2 * CW:base + 3 * CW] = (dcu * gc).astype(dp_ref.dtype)
        rows = (gsum[0], gsum[1],
                jnp.sum(dcv * u2, axis=0, keepdims=True), jnp.sum(dcv * u1, axis=0, keepdims=True),
                jnp.sum(dcv * cu, axis=0, keepdims=True), jnp.sum(dcv, axis=0, keepdims=True))
        _acc_rows(sums_ref, i == 0, rows)

    at = pl.BlockSpec((tm, AW), lambda i: (i, 0))
    return _pcall(
        body, name, (ntile,),
        [at] * 9 + [
            pl.BlockSpec((tm, INC), lambda i: (i, 0)),
            pl.BlockSpec((SUBLANES, 3 * CW), lambda i: (jnp.maximum(i * hb - 1, 0), 1)),
            pl.BlockSpec((SUBLANES, 3 * CW), lambda i: (jnp.minimum((i + 1) * hb, nsl - 1), 1)),
            pl.BlockSpec((tm, CW), lambda i: (i, 1)),
            pl.BlockSpec((SUBLANES, CW), lambda i: (jnp.minimum((i + 1) * hb, nsl - 1), 1)),
            pl.BlockSpec((SUBLANES, AW), lambda i: (0, 0)),
            pl.BlockSpec((SUBLANES, CW), lambda i: (0, 0))],
        [pl.BlockSpec((tm, INC), lambda i: (i, 0)), pl.BlockSpec((SUBLANES, AW), lambda i: (0, 0))],
        [jax.ShapeDtypeStruct((S, INC), MXU_DTYPE), jax.ShapeDtypeStruct((SUBLANES, AW), F32)],
        ("arbitrary",), (*dqs, *dks, *dvs, proj, proj, proj, dycat, dycat, gvec, cw), carry)


def mixer_in_bwd(dxo, x, dproj, vec, winp, name, carry=None):
    S = x.shape[0]
    tm = _row_tile(S, 512)
    pc = INC // NCHIP

    def body(dxo_ref, x_ref, dp_ref, vec_ref, w_ref, dxi_ref, sums_ref):
        xhat, r, gain, ng, sc, _, _ = _ada(x_ref[...], vec_ref)
        dh = jnp.zeros((tm, D), F32)
        for j in range(NCHIP):
            dh = dh + lax.dot_general(dp_ref[:, j * pc:(j + 1) * pc], w_ref[j], NT_DIMS, preferred_element_type=F32)
        dx, dshift, dscale, dng = _ada_bwd(dh, xhat, r, gain, ng, sc)
        dxi_ref[...] = dxo_ref[...] + dx
        _acc_rows(sums_ref, pl.program_id(0) == 0, (dshift, dscale, dng))

    t = pl.BlockSpec((tm, D), lambda i: (i, 0))
    return _pcall(
        body, name, (S // tm,),
        [t, t, pl.BlockSpec((tm, INC), lambda i: (i, 0)),
         pl.BlockSpec((SUBLANES, D), lambda i: (0, 0)),
         pl.BlockSpec((NCHIP, D, pc), lambda i: (0, 0, 0), pipeline_mode=pl.Buffered(1))],
        [t, pl.BlockSpec((SUBLANES, D), lambda i: (0, 0))],
        [jax.ShapeDtypeStruct((S, D), F32), jax.ShapeDtypeStruct((SUBLANES, D), F32)],
        ("arbitrary",), (dxo, x, dproj, vec, winp), carry)


def loss_head(xf, target, name):
    S = xf.shape[0]
    tm = _row_tile(S, 1024)

    def body(x_ref, t_ref, dy_ref, l_ref):
        diff = x_ref[...] - t_ref[...]
        dy_ref[...] = diff * (1.0 / D)
        part = jnp.sum(jnp.sum(diff * diff, axis=0, keepdims=True), axis=1, keepdims=True) * (0.5 / D)

        @pl.when(pl.program_id(0) == 0)
        def _():
            l_ref[...] = jnp.zeros_like(l_ref)
        l_ref[...] += jnp.broadcast_to(part, l_ref.shape)

    t = pl.BlockSpec((tm, D), lambda i: (i, 0))
    return pl.pallas_call(
        body, name=name, grid=(S // tm,),
        in_specs=[t, t],
        out_specs=[t, pl.BlockSpec((SUBLANES, LANES), lambda i: (0, 0))],
        out_shape=[jax.ShapeDtypeStruct((S, D), F32), jax.ShapeDtypeStruct((SUBLANES, LANES), F32)],
        compiler_params=_params(("arbitrary",)),
    )(xf, target)


def _vec(mod_l, ng_l, i):
    m = mod_l.reshape(3, 3, D)
    rows = jnp.stack([ng_l[i], m[i, 1], m[i, 0], m[i, 2]])
    return jnp.concatenate([rows, jnp.zeros((SUBLANES - 4, D), F32)], axis=0)


def local_step(x, target, mods, ngs, gvecs, cws, shards, w_first, cflag):
    saved = []
    weights = [dict(w1=[None, None], w2=[None, None]) for _ in range(2)]
    weights[0]["w1"][0], weights[0]["w2"][0] = w_first[0], w_first[1].reshape(DFF, D)
    h = x
    for l in range(2):
        w, sh = weights[l], shards[l]
        nxt = shards[l + 1] if l == 0 else None
        vecs = [_vec(mods[l], ngs[l], i) for i in range(3)]
        x0 = h
        (x1, a0, f0), (win, wout, w2b) = ffn_fwd(x0, vecs[0], w["w1"][0], w["w2"][0], 0.5, f"ffn_fwd_l{l}a",
                                                 carry=Carry("gather", [sh["win"], sh["wout"], sh["w2"][1]]))
        w["win"], w["wout"], w["w2"][1] = win, wout.reshape(D, D), w2b.reshape(DFF, D)
        proj, h1b, qn, kn, v = mixer_in(x1, vecs[1], w["win"], gvecs[l], f"mixer_in_l{l}")
        os_, lses, w1b = [], [], {}
        for d in DILATIONS:
            rows = {1: slice(0, D // 2), 16: slice(D // 2, D)}.get(d)
            carry = Carry("gather", [sh["w1"][1][rows]]) if rows else None
            (o, lse_d), w1b[d] = attn_fwd(qn, kn, v, d, f"attn_fwd_l{l}_d{d}", carry=carry)
            os_.append(o)
            lses.append(lse_d)
        w["w1"][1] = jnp.concatenate([w1b[1][0], w1b[16][0]], axis=1)
        (ycat, lse), got = combine_conv(os_, lses, proj, cws[l], f"combine_conv_l{l}",
                                        carry=Carry("gather", [nxt["w2"][0]]) if nxt else None)
        if nxt:
            weights[1]["w2"][0] = got[0].reshape(DFF, D)
        x2, y = out_proj(ycat, x1, vecs[1], w["wout"], f"out_proj_l{l}")
        (x3, a2, f2), got = ffn_fwd(x2, vecs[2], w["w1"][1], w["w2"][1], 0.5, f"ffn_fwd_l{l}b",
                                    carry=Carry("gather", [nxt["w1"][0]]) if nxt else None)
        if nxt:
            weights[1]["w1"][0] = got[0]
        saved.append(dict(vecs=vecs, x0=x0, a0=a0, f0=f0, x1=x1, proj=proj, h1b=h1b, qn=qn, kn=kn, v=v,
                          ycat=ycat, lse=lse, y=y, x2=x2, a2=a2, f2=f2))
        h = x3
    dx, loss_blk = loss_head(h, target, "loss_head")
    sums, totals, g_prev = [None, None], [None, None], None
    w2r = DFF // NCHIP
    for l in (1, 0):
        w, s = weights[l], saved[l]
        vecs = s["vecs"]
        ride = g_prev is not None
        own = l == 0
        mine, other = [None] * 6, [None] * 6

        def half_sum(group, recv, k0):
            return [add_half(g, r, cflag, f"add_sibling_l{l}_{k0 + j}") for j, (g, r) in enumerate(zip(group, recv))]

        def chip_sum(landed, k0):
            return [sum_chips(t, f"sum_chips_l{l}_{k0 + j}") for j, t in enumerate(landed)]

        (dx, hb, dfb, act, da, sums2), got = ffn_bwd(
            dx, s["x2"], s["a2"], s["f2"], vecs[2], w["w1"][1], w["w2"][1], 0.5, f"ffn_bwd_l{l}b",
            carry=Carry("swap_halves", g_prev) if ride else None)
        dw1b, _ = wgrad(hb, da, D, HALF, f"wgrad_w1_l{l}b")
        dw2b, _ = wgrad(act, dfb, HALF, D, f"wgrad_w2_l{l}b")
        if ride:
            wire = [add_half(g_prev[k], got[k], cflag, f"add_sibling_l{l + 1}_{k}") for k in range(6)]
        g_ffn_b = [dw1b, dw2b.reshape(NCHIP, w2r, D)]
        (dyb, dycat, delta, sums_o), got = out_proj_bwd(
            dx, s["y"], s["ycat"], vecs[1], w["wout"], f"out_proj_bwd_l{l}",
            carry=Carry("swap_halves", g_ffn_b) if own else None)
        dwout, _ = wgrad(s["ycat"].astype(MXU_DTYPE), dyb, D // 2, D, f"wgrad_wout_l{l}")
        if own:
            wire_ffn_b = half_sum(g_ffn_b, got, 4)
        dqs, dks, dvs, landed = [], [], [], {}
        for d in DILATIONS:
            carry = None
            if ride and d == 1:
                carry = Carry("scatter", wire[3:])
            if ride and d == 16:
                carry = Carry("scatter", wire[:3])
            if own and d == 4:
                carry = Carry("scatter", wire_ffn_b)
            (dq, dk, dv), landed[d] = attn_bwd(s["qn"], s["kn"], s["v"], dycat, s["lse"], delta, d,
                                               f"attn_bwd_l{l}_d{d}", carry=carry)
            dqs.append(dq)
            dks.append(dk)
            dvs.append(dv)
        if ride:
            tot = [sum_chips(t, f"sum_chips_l{l + 1}_{k}") for k, t in enumerate(list(landed[16]) + list(landed[1]))]
        if own:
            mine[4:6] = chip_sum(landed[4], 4)
        ready = (tot if ride else []) + (mine[4:6] if own else [])
        (dproj, sums_m), got = mixer_mid_bwd(dqs, dks, dvs, s["proj"], dycat, gvecs[l], cws[l], f"mixer_mid_bwd_l{l}",
                                             carry=Carry("swap", ready) if ready else None)
        if ride:
            totals[l + 1] = (tot, list(got[:6]))
        if own:
            other[4:6] = list(got[-2:])
        dwin, _ = wgrad(s["h1b"], dproj, D, INC // NCHIP, f"wgrad_win_l{l}")
        g_mixer = [dwin, dwout.reshape(NCHIP, D // NCHIP, D)]
        (dx, sums1), got = mixer_in_bwd(dx, s["x1"], dproj, vecs[1], w["win"], f"mixer_in_bwd_l{l}",
                                        carry=Carry("swap_halves", g_mixer) if own else None)
        if own:
            wire_mixer = half_sum(g_mixer, got, 2)
        (dx, hb, dfb, act, da, sums0), _ = ffn_bwd(
            dx, s["x0"], s["a0"], s["f0"], vecs[0], w["w1"][0], w["w2"][0], 0.5, f"ffn_bwd_l{l}a")
        dw1a, got = wgrad(hb, da, D, HALF, f"wgrad_w1_l{l}a", carry=Carry("scatter", wire_mixer) if own else None)
        if own:
            mine[2:4] = chip_sum(got, 2)
        dw2a, got = wgrad(act, dfb, HALF, D, f"wgrad_w2_l{l}a", carry=Carry("swap", mine[2:4]) if own else None)
        g_ffn_a = [dw1a, dw2a.reshape(NCHIP, w2r, D)]
        if own:
            other[2:4] = list(got)
            wire_ffn_a = half_sum(g_ffn_a, run_carry(Carry("swap_halves", g_ffn_a), "swap_halves_tail"), 0)
            mine[0:2] = chip_sum(run_carry(Carry("scatter", wire_ffn_a), "scatter_grads_tail"), 0)
            other[0:2] = list(run_carry(Carry("swap", mine[0:2]), "swap_totals_tail"))
            totals[l] = (mine, other)
        g_prev = g_ffn_a + g_mixer + g_ffn_b
        sums[l] = (sums0, sums1, sums_o, sums2, sums_m)
    return loss_blk, dx, totals, sums


def small_all_gather(blk, name):
    m_per, n = blk.shape

    def body(x_ref, out_ref, send_sems, recv_sems, local_sem):
        x, y, c = _here()
        me, sibling = (x, y, c), (x, y, 1 - c)
        chips = [(1 - x, y), (x, 1 - y), (1 - x, 1 - y)]

        def rows(px, py, pc):
            return out_ref.at[pl.ds((4 * px + 2 * py + pc) * m_per, m_per), :]

        def copy(k, block, to, src=None):
            return pltpu.make_async_remote_copy(
                src_ref=rows(*block) if src is None else src, dst_ref=rows(*block),
                send_sem=send_sems.at[k], recv_sem=recv_sems.at[k], device_id=to, device_id_type=MESH)

        mine = pltpu.make_async_copy(x_ref, rows(*me), local_sem)
        mine.start()
        first = [copy(0, me, sibling, src=x_ref)]
        first += [copy(1 + j, me, (*chip, c), src=x_ref) for j, chip in enumerate(chips)]
        for cp in first:
            cp.start()
        passed = [copy(4 + j, (*chip, c), sibling) for j, chip in enumerate(chips)]
        for j, chip in enumerate(chips):
            copy(1 + j, (*chip, c), me).wait_recv()
            passed[j].start()
        copy(0, sibling, me).wait_recv()
        for j, chip in enumerate(chips):
            copy(4 + j, (*chip, 1 - c), me).wait_recv()
        for cp in first + passed:
            cp.wait_send()
        mine.wait()

    return pl.pallas_call(
        body, name=name,
        out_shape=jax.ShapeDtypeStruct((NDEV * m_per, n), blk.dtype),
        in_specs=[pl.BlockSpec(memory_space=pltpu.VMEM)],
        out_specs=pl.BlockSpec(memory_space=pltpu.VMEM),
        scratch_shapes=[pltpu.SemaphoreType.DMA((7,)), pltpu.SemaphoreType.DMA((7,)), pltpu.SemaphoreType.DMA],
        compiler_params=pltpu.CompilerParams(vmem_limit_bytes=VMEM_LIMIT),
    )(blk)


EW_BLOCK_BYTES = 1 << 20


def _ew_rows(rows, cols):
    want = max(16, EW_BLOCK_BYTES // (4 * cols))
    best = None
    for t in range(16, rows + 1, 16):
        if rows % t == 0 and t <= want:
            best = t
    return best if best is not None else rows


def add_half(g, recv, cflag, name):
    pieces, r, cols = g.shape
    r2 = r // 2
    tr = _ew_rows(r2, cols)
    nt = r2 // tr

    def body(c_ref, g_ref, r_ref, o_ref):
        o_ref[...] = (g_ref[...] + r_ref[...]).astype(o_ref.dtype)

    half = pl.BlockSpec((None, tr, cols), lambda j, i, c_ref: (j, i, 0))
    return pl.pallas_call(
        body, name=name,
        grid_spec=pltpu.PrefetchScalarGridSpec(
            num_scalar_prefetch=1, grid=(pieces, nt),
            in_specs=[pl.BlockSpec((None, tr, cols), lambda j, i, c_ref: (j, c_ref[0] * nt + i, 0)), half],
            out_specs=half),
        out_shape=jax.ShapeDtypeStruct((pieces, r2, cols), WIRE_DTYPE),
        compiler_params=_params(("arbitrary", "arbitrary")),
    )(cflag, g, recv)


def sum_chips(recv, name):
    _, r, cols = recv.shape
    tr = _ew_rows(r, cols)

    def body(r_ref, o_ref):
        acc = r_ref[0].astype(F32)
        for k in range(1, NCHIP):
            acc = acc + r_ref[k].astype(F32)
        o_ref[...] = acc

    return pl.pallas_call(
        body, name=name, grid=(r // tr,),
        in_specs=[pl.BlockSpec((NCHIP, tr, cols), lambda i: (0, i, 0))],
        out_specs=pl.BlockSpec((tr, cols), lambda i: (i, 0)),
        out_shape=jax.ShapeDtypeStruct((r, cols), F32),
        compiler_params=_params(("arbitrary",)),
    )(recv)


def sum_devices(rows8, name):
    def body(r_ref, o_ref):
        acc = r_ref[0:1, :]
        for k in range(1, NDEV):
            acc = acc + r_ref[k:k + 1, :]
        o_ref[...] = jnp.broadcast_to(acc, o_ref.shape)

    return pl.pallas_call(
        body, name=name, out_shape=jax.ShapeDtypeStruct(rows8.shape, F32),
        in_specs=[pl.BlockSpec(memory_space=pltpu.VMEM)], out_specs=pl.BlockSpec(memory_space=pltpu.VMEM),
        compiler_params=pltpu.CompilerParams(vmem_limit_bytes=VMEM_LIMIT),
    )(rows8)


def adamw(w, m, v, srcs, cflag, name, halves=False):
    planes, r, cols = w.shape
    rh = r // 2 if halves else r
    tr = _ew_rows(rh, cols)
    nth = rh // tr
    flat = [a for s in srcs for a in (s if halves else (s,))]
    ns = len(flat)
    per = ns // planes

    def body(c_ref, w_ref, m_ref, v_ref, *rest):
        s_refs, (g_ref, d_ref, mo_ref, vo_ref) = rest[:ns], rest[ns:]
        p, i = pl.program_id(0), pl.program_id(1)
        if halves:
            mine = jnp.logical_not(jnp.logical_xor(i >= nth, c_ref[0] == 1))
            blocks = [jnp.where(mine, s_refs[2 * k][...], s_refs[2 * k + 1][...]) for k in range(planes)]
        else:
            blocks = [s[...] for s in s_refs]
        g = blocks[0]
        for k in range(1, planes):
            g = jnp.where(p == k, blocks[k], g)
        g_ref[...] = g
        m_new = ADAM_B1 * m_ref[...] + (1.0 - ADAM_B1) * g
        v_new = ADAM_B2 * v_ref[...] + (1.0 - ADAM_B2) * (g * g)
        mo_ref[...] = m_new
        vo_ref[...] = v_new
        m_hat = m_new / (1.0 - ADAM_B1 ** ADAM_STEP)
        v_hat = v_new / (1.0 - ADAM_B2 ** ADAM_STEP)
        d_ref[...] = -ADAM_LR * (m_hat / (jnp.sqrt(v_hat) + ADAM_EPS) + ADAM_WD * w_ref[...])

    pt = pl.BlockSpec((None, tr, cols), lambda p, i: (p, i, 0))
    st = [pl.BlockSpec((tr, cols), functools.partial(lambda k, p, i: (jnp.where(p == k, i % nth, 0), 0), j // per))
          for j in range(ns)]
    return pl.pallas_call(
        body, name=name, grid=(planes, r // tr),
        in_specs=[pl.BlockSpec(memory_space=pltpu.SMEM), pt, pt, pt] + st,
        out_specs=[pt] * 4,
        out_shape=[jax.ShapeDtypeStruct(w.shape, F32)] * 4,
        compiler_params=_params(("arbitrary", "arbitrary")),
    )(cflag, w, m, v, *flat)


ADA_COLS = 9 * D // NCHIP


def mod_fwd(c_all, w_ada, b_shard, name):
    def body(c_ref, w_ref, b_ref, o_ref):
        cc = c_ref[...]
        sc = cc * jax.nn.sigmoid(cc)
        o_ref[...] = jnp.dot(sc, w_ref[...], preferred_element_type=F32,
                             precision=lax.Precision.HIGHEST) + b_ref[...]

    return pl.pallas_call(
        body, name=name, grid=(2,),
        in_specs=[pl.BlockSpec((NDEV, D), lambda l: (0, 0)),
                  pl.BlockSpec((None, D, ADA_COLS), lambda l: (l, 0, 0)),
                  pl.BlockSpec((None, 1, ADA_COLS), lambda l: (l, 0, 0))],
        out_specs=pl.BlockSpec((None, NDEV, ADA_COLS), lambda l: (l, 0, 0)),
        out_shape=jax.ShapeDtypeStruct((2, NDEV, ADA_COLS), F32),
        compiler_params=_params(("arbitrary",)),
    )(c_all, w_ada, b_shard.reshape(2, 1, ADA_COLS))


def wada_grad(c_all_t, dmod, name):
    ct = ADA_COLS // 3

    def body(c_ref, d_ref, o_ref):
        cc = c_ref[...]
        sc = cc * jax.nn.sigmoid(cc)
        acc = sc[:, 0:1] * d_ref[0:1, :]
        for b in range(1, NDEV):
            acc = acc + sc[:, b:b + 1] * d_ref[b:b + 1, :]
        o_ref[...] = acc

    return pl.pallas_call(
        body, name=name, grid=(2, 3),
        in_specs=[pl.BlockSpec((D, LANES), lambda l, j: (0, 0)),
                  pl.BlockSpec((None, NDEV, ct), lambda l, j: (l, 0, j))],
        out_specs=pl.BlockSpec((None, D, ct), lambda l, j: (l, 0, j)),
        out_shape=jax.ShapeDtypeStruct((2, D, ADA_COLS), F32),
        compiler_params=_params(("arbitrary", "arbitrary")),
    )(c_all_t, dmod)


def _pad_rows(row, rows=SUBLANES):
    return jnp.concatenate([row[None, :], jnp.zeros((rows - 1, row.shape[0]), row.dtype)], axis=0)


def kernel(x, c, w_ada, b_ada, norm_g, w_in, q_norm_g, k_norm_g, conv_w, conv_b, w_out, ffn_w1, ffn_w2, loss_target, m_w_ada, m_b_ada, m_norm_g, m_w_in, m_q_norm_g, m_k_norm_g, m_conv_w, m_conv_b, m_w_out, m_ffn_w1, m_ffn_w2, v_w_ada, v_b_ada, v_norm_g, v_w_in, v_q_norm_g, v_k_norm_g, v_conv_w, v_conv_b, v_w_out, v_ffn_w1, v_ffn_w2):
    ix, iy, ic = lax.axis_index("x"), lax.axis_index("y"), lax.axis_index("c")
    chip = 2 * ix + iy
    dev = 2 * chip + ic
    cflag = jnp.reshape(ic, (1,)).astype(jnp.int32)
    ngw = norm_g.shape[-1]
    cww = conv_w.shape[-1]

    pack = jnp.concatenate([c[0], norm_g.reshape(-1), conv_w.reshape(-1)])
    got = small_all_gather(_pad_rows(pack), "gather_c_normg_convw")[::SUBLANES]
    c_all = got[:, :D]
    per_chip = got[::2]
    ng_full = jnp.concatenate([per_chip[j, D:D + 6 * ngw].reshape(2, 3, ngw) for j in range(NCHIP)], axis=-1)
    cw_full = jnp.concatenate([per_chip[j, D + 6 * ngw:].reshape(2, 3, cww) for j in range(NCHIP)], axis=-1)

    b_shard = lax.dynamic_slice_in_dim(b_ada, chip * ADA_COLS, ADA_COLS, axis=1)
    mod_blk = mod_fwd(c_all, w_ada, b_shard, "mod_fwd").reshape(2 * NDEV, ADA_COLS)
    mod_all = small_all_gather(mod_blk, "gather_mod").reshape(NDEV, 2, NDEV, ADA_COLS)[::2]
    mod_mine = lax.dynamic_index_in_dim(mod_all, dev, axis=2, keepdims=False)
    mods = [mod_mine[:, l, :].reshape(-1) for l in range(2)]

    shards, gvecs, cws = [], [], []
    for l in range(2):
        shards.append(dict(w1=[ffn_w1[l, i].astype(MXU_DTYPE) for i in range(2)],
                           w2=[ffn_w2[l, i].astype(MXU_DTYPE) for i in range(2)],
                           win=w_in[l].astype(MXU_DTYPE), wout=w_out[l].astype(MXU_DTYPE)))
        gv = jnp.stack([jnp.tile(q_norm_g[l], AW // HD), jnp.tile(k_norm_g[l], AW // HD)])
        gvecs.append(jnp.concatenate([gv, jnp.zeros((SUBLANES - 2, AW), F32)], axis=0))
        cws.append(jnp.concatenate([cw_full[l], conv_b[l][None, :], jnp.zeros((SUBLANES - 4, CW), F32)], axis=0))
    w_first = run_carry(Carry("gather", [shards[0]["w1"][0], shards[0]["w2"][0]]), "gather_first_ffn")

    loss_blk, dx, totals, sums = local_step(x[0], loss_target[0], mods, [ng_full[0], ng_full[1]], gvecs, cws,
                                            shards, w_first, cflag)
    loss = lax.psum(loss_blk[0, 0], ("x", "y", "c"))

    dmods, dngs, dqg, dkg, dcw, dcb = [], [], [], [], [], []
    for l in range(2):
        s0, s1, so, s2, sm = sums[l]
        dmods.append(jnp.concatenate([s0[0], s0[1], s0[3], s1[0], s1[1], so[0], s2[0], s2[1], s2[3]]))
        dngs.append(jnp.concatenate([s0[2], s1[2], s2[2]]))
        dqg.append(sm[0].reshape(AW // HD, HD).sum(0))
        dkg.append(sm[1].reshape(AW // HD, HD).sum(0))
        dcw.append(sm[2:5].reshape(-1))
        dcb.append(sm[5])
    small = jnp.concatenate(dmods + dngs + dqg + dkg + dcw + dcb)
    small_all = small_all_gather(_pad_rows(small), "gather_small_grads")[::SUBLANES]
    nm = 9 * D
    dmod_all = small_all[:, :2 * nm].reshape(NDEV, 2, NCHIP, ADA_COLS)
    dmod_mine = lax.dynamic_index_in_dim(dmod_all, chip, axis=2, keepdims=False).transpose(1, 0, 2)
    tot = sum_devices(small_all, "sum_small_grads")[0]
    o = 2 * nm
    g_b_ada = tot[:o].reshape(2, nm)
    g_norm_g = lax.dynamic_slice_in_dim(tot[o:o + 6 * D].reshape(2, 3, D), chip * ngw, ngw, axis=2)
    o += 6 * D
    g_qg = tot[o:o + 2 * HD].reshape(2, HD)
    o += 2 * HD
    g_kg = tot[o:o + 2 * HD].reshape(2, HD)
    o += 2 * HD
    g_cw = lax.dynamic_slice_in_dim(tot[o:o + 6 * CW].reshape(2, 3, CW), chip * cww, cww, axis=2)
    o += 6 * CW
    g_cb = tot[o:o + 2 * CW].reshape(2, CW)

    c_all_t = jnp.concatenate([c_all.T, jnp.zeros((D, LANES - NDEV), F32)], axis=1)
    g_wada_src = wada_grad(c_all_t, dmod_mine, "wada_grad")

    def halves(k_of_plane):
        return [(totals[l][0][k], totals[l][1][k]) for l, k in k_of_plane]

    r_wada = adamw(w_ada, m_w_ada, v_w_ada, [g_wada_src[0], g_wada_src[1]], cflag, "adamw_w_ada")
    r_win = adamw(w_in, m_w_in, v_w_in, halves([(0, 2), (1, 2)]), cflag, "adamw_w_in", halves=True)
    r_wout = adamw(w_out, m_w_out, v_w_out, halves([(0, 3), (1, 3)]), cflag, "adamw_w_out", halves=True)
    r_w1 = adamw(ffn_w1.reshape(4, D, HALF), m_ffn_w1.reshape(4, D, HALF), v_ffn_w1.reshape(4, D, HALF),
                 halves([(0, 0), (0, 4), (1, 0), (1, 4)]), cflag, "adamw_ffn_w1", halves=True)
    w2r = DFF // NCHIP
    r_w2 = adamw(ffn_w2.reshape(4, w2r, D), m_ffn_w2.reshape(4, w2r, D), v_ffn_w2.reshape(4, w2r, D),
                 halves([(0, 1), (0, 5), (1, 1), (1, 5)]), cflag, "adamw_ffn_w2", halves=True)
    r_w1 = [t.reshape(ffn_w1.shape) for t in r_w1]
    r_w2 = [t.reshape(ffn_w2.shape) for t in r_w2]

    smalls = [("b_ada", b_ada, m_b_ada, v_b_ada, g_b_ada), ("norm_g", norm_g, m_norm_g, v_norm_g, g_norm_g),
              ("q_norm_g", q_norm_g, m_q_norm_g, v_q_norm_g, g_qg), ("k_norm_g", k_norm_g, m_k_norm_g, v_k_norm_g, g_kg),
              ("conv_w", conv_w, m_conv_w, v_conv_w, g_cw), ("conv_b", conv_b, m_conv_b, v_conv_b, g_cb)]
    n_small = sum(t[1].size for t in smalls)
    pad = (-n_small) % (16 * LANES)

    def packed(idx):
        flat = jnp.concatenate([t[idx].reshape(-1) for t in smalls] + [jnp.zeros((pad,), F32)])
        return flat.reshape(-1, LANES)

    r_small = adamw(packed(1)[None], packed(2)[None], packed(3)[None], [packed(4)], cflag, "adamw_small")
    small_out = {}
    o = 0
    for name_, w_, _, _, _ in smalls:
        small_out[name_] = [t.reshape(-1)[o:o + w_.size].reshape(w_.shape) for t in r_small]
        o += w_.size

    res = {"w_ada": r_wada, "w_in": r_win, "w_out": r_wout, "ffn_w1": r_w1, "ffn_w2": r_w2, **small_out}
    order = ["w_ada", "b_ada", "norm_g", "w_in", "q_norm_g", "k_norm_g", "conv_w", "conv_b", "w_out", "ffn_w1", "ffn_w2"]
    outs = [loss, dx[None]]
    for k in range(4):
        outs += [res[nm_][k] for nm_ in order]
    return tuple(outs)
```

```python
import functools

import jax
import jax.numpy as jnp
from jax import lax
from jax.experimental import pallas as pl
from jax.experimental.pallas import tpu as pltpu

F32 = jnp.float32
MXU_DTYPE = jnp.bfloat16
ACT_DTYPE = jnp.bfloat16
WIRE_DTYPE = jnp.bfloat16

D = 1024
HD = 64
AW = 512
CW = 512
DFF = 2816
HALF = DFF // 2
INC = 3 * AW + 3 * CW
NCHIP = 4
NDEV = 8
QBLK = 128
ATTN_QBLOCKS = 4
ATTN_CHUNK_ROWS = 2048
DILATIONS = (1, 4, 16)
EPS = 1e-6
NEG = -1e30
LANES = 128
SUBLANES = 8
VMEM_LIMIT = 56 * 1024 * 1024

ADAM_LR = 0.001
ADAM_B1 = 0.9
ADAM_B2 = 0.999
ADAM_EPS = 1e-08
ADAM_WD = 0.01
ADAM_STEP = 10

NT_DIMS = (((1,), (1,)), ((), ()))
TN_DIMS = (((0,), (0,)), ((), ()))


def _params(sem, vmem=VMEM_LIMIT):
    return pltpu.CompilerParams(dimension_semantics=sem, vmem_limit_bytes=vmem)


def _row_tile(n, want):
    t = min(n, want)
    assert n % t == 0
    return t


def _ada(xt, vec_ref):
    ng, sc, sh, gt = vec_ref[0:1, :], vec_ref[1:2, :], vec_ref[2:3, :], vec_ref[3:4, :]
    r = lax.rsqrt(jnp.mean(xt * xt, axis=-1, keepdims=True) + EPS)
    return xt * r, r, ng * (1.0 + sc), ng, sc, sh, gt


def _ada_bwd(dh, xhat, r, gain, ng, sc):
    dshift = jnp.sum(dh, axis=0, keepdims=True)
    dhx = dh * xhat
    dscale = jnp.sum(dhx, axis=0, keepdims=True) * ng
    dng = jnp.sum(dhx, axis=0, keepdims=True) * (1.0 + sc)
    dxhat = dh * gain
    dx = r * (dxhat - xhat * jnp.mean(dxhat * xhat, axis=-1, keepdims=True))
    return dx, dshift, dscale, dng


def _acc_rows(sums_ref, first, rows):
    @pl.when(first)
    def _():
        sums_ref[...] = jnp.zeros_like(sums_ref)
    for k, row in enumerate(rows):
        sums_ref[k:k + 1, :] += row


MESH = pl.DeviceIdType.MESH
ANY = pl.BlockSpec(memory_space=pl.ANY)


def _here():
    return lax.axis_index("x"), lax.axis_index("y"), lax.axis_index("c")


def _ici_copies(src_refs, dst_refs, send_sems, recv_sems, local_sems, scatter):
    x, y, c = _here()
    my_chip = 2 * x + y
    peers = [(1 - x, y), (x, 1 - y), (1 - x, 1 - y)]
    local, out, inc = [], [], []
    for a, (src, dst) in enumerate(zip(src_refs, dst_refs)):
        local.append(pltpu.make_async_copy(src.at[my_chip] if scatter else src, dst.at[my_chip], local_sems.at[a]))
        for j, (px, py) in enumerate(peers):
            sems = dict(send_sem=send_sems.at[3 * a + j], recv_sem=recv_sems.at[3 * a + j],
                        device_id=(px, py, c), device_id_type=MESH)
            out.append(pltpu.make_async_remote_copy(
                src_ref=src.at[2 * px + py] if scatter else src, dst_ref=dst.at[my_chip], **sems))
            inc.append(pltpu.make_async_remote_copy(
                src_ref=src.at[my_chip] if scatter else src, dst_ref=dst.at[2 * px + py], **sems))
    return local, out, inc


def _swap_copies(src_refs, dst_refs, send_sems, recv_sems, halves):
    x, y, c = _here()
    cps = []
    for k, (src, dst) in enumerate(zip(src_refs, dst_refs)):
        if halves:
            r2 = src.shape[1] // 2
            src = src.at[:, pl.ds((1 - c) * r2, r2), :]
        cps.append(pltpu.make_async_remote_copy(
            src_ref=src, dst_ref=dst, send_sem=send_sems.at[k], recv_sem=recv_sems.at[k],
            device_id=(x, y, 1 - c), device_id_type=MESH))
    return cps


class Carry:
    def __init__(self, kind, srcs):
        self.kind, self.srcs, n = kind, list(srcs), len(srcs)
        if kind == "gather":
            shapes = [(NCHIP,) + s.shape for s in srcs]
        elif kind == "swap_halves":
            shapes = [(s.shape[0], s.shape[1] // 2, s.shape[2]) for s in srcs]
        else:
            shapes = [s.shape for s in srcs]
        self.out_shape = [jax.ShapeDtypeStruct(sh, s.dtype) for sh, s in zip(shapes, srcs)]
        dma = pltpu.SemaphoreType.DMA
        self.sems = [dma((3 * n,)), dma((3 * n,)), dma((n,))] if kind in ("gather", "scatter") else [dma((n,)), dma((n,))]

    def start(self, srcs, dsts, sems):
        if self.kind in ("gather", "scatter"):
            local, out, _ = _ici_copies(srcs, dsts, *sems, self.kind == "scatter")
            for cp in local + out:
                cp.start()
        else:
            for cp in _swap_copies(srcs, dsts, *sems, self.kind == "swap_halves"):
                cp.start()

    def wait(self, srcs, dsts, sems):
        if self.kind in ("gather", "scatter"):
            local, out, inc = _ici_copies(srcs, dsts, *sems, self.kind == "scatter")
            for cp in inc:
                cp.wait_recv()
            for cp in out:
                cp.wait_send()
            for cp in local:
                cp.wait()
        else:
            cps = _swap_copies(srcs, dsts, *sems, self.kind == "swap_halves")
            for cp in cps:
                cp.wait_recv()
            for cp in cps:
                cp.wait_send()


def run_carry(carry, name):
    n = len(carry.srcs)

    def body(*refs):
        srcs, dsts, sems = refs[:n], refs[n:2 * n], refs[2 * n:]
        carry.start(srcs, dsts, sems)
        carry.wait(srcs, dsts, sems)

    return pl.pallas_call(body, name=name, out_shape=carry.out_shape, in_specs=[ANY] * n, out_specs=[ANY] * n,
                          scratch_shapes=carry.sems)(*carry.srcs)


def gather_split(srcs, name):
    n = len(srcs)

    def body(*refs):
        src_refs, dst_refs = refs[:n], refs[n:2 * n]
        send_sems, recv_sems, fwd_send, fwd_recv, local_sems = refs[2 * n:]
        x, y, c = _here()
        my_chip = 2 * x + y
        peers = [(1 - x, y), (x, 1 - y), (1 - x, 1 - y)]

        def half(ref, h):
            r2 = ref.shape[0] // 2
            return ref.at[pl.ds(h * r2, r2), :]

        local, out, landed, passed, arriving = [], [], [], [], []
        for a, (src, dst) in enumerate(zip(src_refs, dst_refs)):
            local.append(pltpu.make_async_copy(src, dst.at[my_chip], local_sems.at[a]))
            for j, (px, py) in enumerate(peers):
                k = 3 * a + j
                theirs = dst.at[2 * px + py]
                ici = dict(send_sem=send_sems.at[k], recv_sem=recv_sems.at[k], device_id=(px, py, c), device_id_type=MESH)
                d2d = dict(send_sem=fwd_send.at[k], recv_sem=fwd_recv.at[k], device_id=(x, y, 1 - c), device_id_type=MESH)
                out.append(pltpu.make_async_remote_copy(src_ref=half(src, c), dst_ref=half(dst.at[my_chip], c), **ici))
                landed.append(pltpu.make_async_remote_copy(src_ref=half(src, c), dst_ref=half(theirs, c), **ici))
                passed.append(pltpu.make_async_remote_copy(src_ref=half(theirs, c), dst_ref=half(theirs, c), **d2d))
                arriving.append(pltpu.make_async_remote_copy(src_ref=half(theirs, c), dst_ref=half(theirs, 1 - c), **d2d))
        for cp in local + out:
            cp.start()
        for got, fwd in zip(landed, passed):
            got.wait_recv()
            fwd.start()
        for cp in arriving:
            cp.wait_recv()
        for cp in out + passed:
            cp.wait_send()
        for cp in local:
            cp.wait()

    dma = pltpu.SemaphoreType.DMA
    return pl.pallas_call(
        body, name=name, out_shape=[jax.ShapeDtypeStruct((NCHIP,) + s.shape, s.dtype) for s in srcs],
        in_specs=[ANY] * n, out_specs=[ANY] * n,
        scratch_shapes=[dma((3 * n,)), dma((3 * n,)), dma((3 * n,)), dma((3 * n,)), dma((n,))],
    )(*srcs)


def _pcall(body, name, grid, in_specs, out_specs, out_shape, sem, args, carry=None, scratch=()):
    if carry is None:
        outs = pl.pallas_call(body, name=name, grid=grid, in_specs=in_specs, out_specs=out_specs,
                              out_shape=out_shape, scratch_shapes=list(scratch), compiler_params=_params(sem))(*args)
        return outs, []
    n_in, n_out, nc, ns = len(in_specs), len(out_specs), len(carry.srcs), len(scratch)

    def wrapped(*refs):
        ins, csrc = refs[:n_in], refs[n_in:n_in + nc]
        outs, cdst = refs[n_in + nc:n_in + nc + n_out], refs[n_in + nc + n_out:n_in + 2 * nc + n_out]
        own = refs[n_in + 2 * nc + n_out:n_in + 2 * nc + n_out + ns]
        sems = refs[n_in + 2 * nc + n_out + ns:]
        ids = [pl.program_id(a) for a in range(len(grid))]
        first = functools.reduce(jnp.logical_and, [i == 0 for i in ids])
        last = functools.reduce(jnp.logical_and, [i == g - 1 for i, g in zip(ids, grid)])

        @pl.when(first)
        def _():
            carry.start(csrc, cdst, sems)

        body(*ins, *outs, *own)

        @pl.when(last)
        def _():
            carry.wait(csrc, cdst, sems)

    res = pl.pallas_call(
        wrapped, name=name, grid=grid,
        in_specs=list(in_specs) + [ANY] * nc, out_specs=list(out_specs) + [ANY] * nc,
        out_shape=list(out_shape) + carry.out_shape,
        scratch_shapes=list(scratch) + carry.sems, compiler_params=_params(sem),
    )(*args, *carry.srcs)
    return res[:n_out], res[n_out:]


def ffn_fwd(x, vec, w1p, w2, gs, name, carry=None):
    S = x.shape[0]
    tm = _row_tile(S, 512)

    def body(x_ref, vec_ref, w1_ref, w2_ref, xn_ref, a_ref, f_ref):
        xt = x_ref[...]
        xhat, _, gain, _, _, sh, gt = _ada(xt, vec_ref)
        h = (xhat * gain + sh).astype(MXU_DTYPE)
        f = jnp.zeros((tm, D), F32)
        for hf in range(2):
            g = jnp.dot(h, w1_ref[hf], preferred_element_type=F32)
            up = jnp.dot(h, w1_ref[2 + hf], preferred_element_type=F32)
            a_ref[:, hf * HALF:(hf + 1) * HALF] = g.astype(a_ref.dtype)
            a_ref[:, DFF + hf * HALF:DFF + (hf + 1) * HALF] = up.astype(a_ref.dtype)
            act = (g * jax.nn.sigmoid(g) * up).astype(MXU_DTYPE)
            f = f + jnp.dot(act, w2_ref[hf * HALF:(hf + 1) * HALF, :], preferred_element_type=F32)
        xn_ref[...] = xt + (gs * gt) * f
        f_ref[...] = f.astype(f_ref.dtype)

    return _pcall(
        body, name, (S // tm,),
        [pl.BlockSpec((tm, D), lambda i: (i, 0)),
         pl.BlockSpec((SUBLANES, D), lambda i: (0, 0)),
         pl.BlockSpec((NCHIP, D, HALF), lambda i: (0, 0, 0), pipeline_mode=pl.Buffered(1)),
         pl.BlockSpec((DFF, D), lambda i: (0, 0), pipeline_mode=pl.Buffered(1))],
        [pl.BlockSpec((tm, D), lambda i: (i, 0)),
         pl.BlockSpec((tm, 2 * DFF), lambda i: (i, 0)),
         pl.BlockSpec((tm, D), lambda i: (i, 0))],
        [jax.ShapeDtypeStruct((S, D), F32),
         jax.ShapeDtypeStruct((S, 2 * DFF), ACT_DTYPE),
         jax.ShapeDtypeStruct((S, D), ACT_DTYPE)],
        ("arbitrary",), (x, vec, w1p, w2), carry)


def ffn_bwd(dxo, x, a, f, vec, w1p, w2, gs, name, carry=None):
    S = x.shape[0]
    tm = _row_tile(S, 256)

    def body(dxo_ref, x_ref, a_ref, f_ref, vec_ref, w1_ref, w2_ref,
             dxi_ref, hb_ref, dfb_ref, act_ref, da_ref, sums_ref):
        xt = x_ref[...]
        dxo = dxo_ref[...]
        xhat, r, gain, ng, sc, sh, gt = _ada(xt, vec_ref)
        hb_ref[...] = (xhat * gain + sh).astype(hb_ref.dtype)
        dgate = gs * jnp.sum(dxo * f_ref[...].astype(F32), axis=0, keepdims=True)
        df = ((gs * gt) * dxo).astype(MXU_DTYPE)
        dfb_ref[...] = df
        dh = jnp.zeros((tm, D), F32)
        for hf in range(2):
            lo, hi = hf * HALF, (hf + 1) * HALF
            dact = lax.dot_general(df, w2_ref[lo:hi, :], NT_DIMS, preferred_element_type=F32)
            g = a_ref[:, lo:hi].astype(F32)
            up = a_ref[:, DFF + lo:DFF + hi].astype(F32)
            sg = jax.nn.sigmoid(g)
            si = g * sg
            act_ref[:, lo:hi] = (si * up).astype(act_ref.dtype)
            dg = (dact * up * (sg * (1.0 + g * (1.0 - sg)))).astype(MXU_DTYPE)
            dup = (dact * si).astype(MXU_DTYPE)
            da_ref[:, lo:hi] = dg
            da_ref[:, DFF + lo:DFF + hi] = dup
            dh = dh + lax.dot_general(dg, w1_ref[hf], NT_DIMS, preferred_element_type=F32)
            dh = dh + lax.dot_general(dup, w1_ref[2 + hf], NT_DIMS, preferred_element_type=F32)
        dx, dshift, dscale, dng = _ada_bwd(dh, xhat, r, gain, ng, sc)
        dxi_ref[...] = dxo + dx
        _acc_rows(sums_ref, pl.program_id(0) == 0, (dshift, dscale, dng, dgate))

    return _pcall(
        body, name, (S // tm,),
        [pl.BlockSpec((tm, D), lambda i: (i, 0)),
         pl.BlockSpec((tm, D), lambda i: (i, 0)),
         pl.BlockSpec((tm, 2 * DFF), lambda i: (i, 0)),
         pl.BlockSpec((tm, D), lambda i: (i, 0)),
         pl.BlockSpec((SUBLANES, D), lambda i: (0, 0)),
         pl.BlockSpec((NCHIP, D, HALF), lambda i: (0, 0, 0), pipeline_mode=pl.Buffered(1)),
         pl.BlockSpec((DFF, D), lambda i: (0, 0), pipeline_mode=pl.Buffered(1))],
        [pl.BlockSpec((tm, D), lambda i: (i, 0)),
         pl.BlockSpec((tm, D), lambda i: (i, 0)),
         pl.BlockSpec((tm, D), lambda i: (i, 0)),
         pl.BlockSpec((tm, DFF), lambda i: (i, 0)),
         pl.BlockSpec((tm, 2 * DFF), lambda i: (i, 0)),
         pl.BlockSpec((SUBLANES, D), lambda i: (0, 0))],
        [jax.ShapeDtypeStruct((S, D), F32),
         jax.ShapeDtypeStruct((S, D), MXU_DTYPE),
         jax.ShapeDtypeStruct((S, D), MXU_DTYPE),
         jax.ShapeDtypeStruct((S, DFF), MXU_DTYPE),
         jax.ShapeDtypeStruct((S, 2 * DFF), MXU_DTYPE),
         jax.ShapeDtypeStruct((SUBLANES, D), F32)],
        ("arbitrary",), (dxo, x, a, f, vec, w1p, w2), carry)


def wgrad(a, b, kt, nt, name, carry=None):
    T, K = a.shape
    N = b.shape[1]
    pk, pn = K // kt, N // nt
    assert pk == 1 or pn == 1
    tt = _row_tile(T, 2048)
    steps = T // tt

    def body(a_ref, b_ref, o_ref):
        @pl.when(pl.program_id(1) == 0)
        def _():
            o_ref[...] = jnp.zeros_like(o_ref)
        o_ref[...] += lax.dot_general(a_ref[...], b_ref[...], TN_DIMS, preferred_element_type=F32)

    a_map = (lambda p, t: (t, p)) if pk > 1 else (lambda p, t: (t, 0))
    b_map = (lambda p, t: (t, p)) if pn > 1 else (lambda p, t: (t, 0))
    (out,), got = _pcall(
        body, name, (pk * pn, steps),
        [pl.BlockSpec((tt, kt), a_map), pl.BlockSpec((tt, nt), b_map)],
        [pl.BlockSpec((None, kt, nt), lambda p, t: (p, 0, 0))],
        [jax.ShapeDtypeStruct((pk * pn, kt, nt), F32)], ("arbitrary", "arbitrary"), (a, b), carry)
    return out, got


def _head_masks(rows):
    lane = lax.broadcasted_iota(jnp.int32, (rows, LANES), 1)
    return lane < HD


def _pair_stat(x, m_a):
    s_a = jnp.sum(jnp.where(m_a, x, 0.0), axis=1, keepdims=True)
    s_b = jnp.sum(jnp.where(m_a, 0.0, x), axis=1, keepdims=True)
    return s_a, s_b


def mixer_in(x, vec, winp, gvec, name):
    S = x.shape[0]
    tm = _row_tile(S, 512)
    pc = INC // NCHIP

    def body(x_ref, vec_ref, w_ref, g_ref, proj_ref, hb_ref, qn_ref, kn_ref, v_ref):
        xt = x_ref[...]
        xhat, _, gain, _, _, sh, _ = _ada(xt, vec_ref)
        h = (xhat * gain + sh).astype(MXU_DTYPE)
        hb_ref[...] = h
        for j in range(NCHIP):
            proj_ref[:, j * pc:(j + 1) * pc] = jnp.dot(h, w_ref[j], preferred_element_type=F32)
        m_a = _head_masks(tm)
        for which, dst in ((0, qn_ref), (1, kn_ref)):
            for p in range(AW // LANES):
                lo = which * AW + p * LANES
                xp = proj_ref[:, lo:lo + LANES]
                s_a, s_b = _pair_stat(xp * xp, m_a)
                rr = jnp.where(m_a, lax.rsqrt(s_a * (1.0 / HD) + EPS), lax.rsqrt(s_b * (1.0 / HD) + EPS))
                gp = g_ref[which:which + 1, p * LANES:(p + 1) * LANES]
                dst[:, p * LANES:(p + 1) * LANES] = (xp * rr * gp).astype(dst.dtype)
        v_ref[...] = proj_ref[:, 2 * AW:3 * AW].astype(v_ref.dtype)

    return pl.pallas_call(
        body, name=name, grid=(S // tm,),
        in_specs=[pl.BlockSpec((tm, D), lambda i: (i, 0)),
                  pl.BlockSpec((SUBLANES, D), lambda i: (0, 0)),
                  pl.BlockSpec((NCHIP, D, pc), lambda i: (0, 0, 0), pipeline_mode=pl.Buffered(1)),
                  pl.BlockSpec((SUBLANES, AW), lambda i: (0, 0))],
        out_specs=[pl.BlockSpec((tm, INC), lambda i: (i, 0)),
                   pl.BlockSpec((tm, D), lambda i: (i, 0)),
                   pl.BlockSpec((tm, AW), lambda i: (i, 0)),
                   pl.BlockSpec((tm, AW), lambda i: (i, 0)),
                   pl.BlockSpec((tm, AW), lambda i: (i, 0))],
        out_shape=[jax.ShapeDtypeStruct((S, INC), F32),
                   jax.ShapeDtypeStruct((S, D), MXU_DTYPE),
                   jax.ShapeDtypeStruct((S, AW), F32),
                   jax.ShapeDtypeStruct((S, AW), F32),
                   jax.ShapeDtypeStruct((S, AW), F32)],
        compiler_params=_params(("arbitrary",)),
    )(x, vec, winp, gvec)


def _band_masks(ncol):
    row = lax.broadcasted_iota(jnp.int32, (2 * QBLK, ncol), 0) & (QBLK - 1)
    col = lax.broadcasted_iota(jnp.int32, (2 * QBLK, ncol), 1)
    return row, col


def _stack_heads(t, m_a):
    zero = jnp.zeros_like(t)
    return jnp.concatenate([jnp.where(m_a, t, zero), jnp.where(m_a, zero, t)], axis=0)


class _AttnLayout:
    def __init__(self, d, S):
        self.d, self.S = d, S
        self.qb = max(1, min(ATTN_QBLOCKS, ATTN_CHUNK_ROWS // (QBLK * d)))
        self.nres = d
        self.nchunk = S // (self.qb * QBLK * d)
        self.grid = (AW // LANES, self.nchunk)
        self.unroll = max(1, min(d, ATTN_QBLOCKS // self.qb))

    def _spec(self, blocks, row_of):
        return pl.BlockSpec((blocks * QBLK * self.d, LANES), lambda hp, j: (row_of(j), hp))

    def cur(self, chunk_of):
        return self._spec(self.qb, chunk_of)

    def prev(self, chunk_of):
        return self._spec(1, lambda j: jnp.maximum(chunk_of(j) * self.qb - 1, 0))

    def idx(self, b, r):
        if self.d == 1:
            return (pl.ds(b * QBLK, QBLK), slice(None))
        return (pl.ds(b * QBLK * self.d + r, QBLK, stride=self.d), slice(None))

    def per_residue(self, fn):
        if self.nres == 1:
            fn(0)
        else:
            def step(it, carry):
                for k in range(self.unroll):
                    fn(it * self.unroll + k)
                return carry
            lax.fori_loop(0, self.nres // self.unroll, step, 0)


def attn_fwd(qn, kn, v, d, name, carry=None):
    S = qn.shape[0]
    lay = _AttnLayout(d, S)
    qb = lay.qb

    def body(q_ref, kc_ref, kp_ref, vc_ref, vp_ref, o_ref, lse_ref):
        i = pl.program_id(1)
        m_a = _head_masks(QBLK)
        row, col = _band_masks(2 * QBLK)
        dist = row + QBLK - col
        band = (dist >= 0) & (dist <= QBLK)
        first = band & ((i > 0) | (col >= QBLK))

        def residue(r):
            kt = [kp_ref[lay.idx(0, r)].astype(MXU_DTYPE)]
            vt = [vp_ref[lay.idx(0, r)].astype(MXU_DTYPE)]
            for b in range(qb):
                kt.append(kc_ref[lay.idx(b, r)].astype(MXU_DTYPE))
                vt.append(vc_ref[lay.idx(b, r)].astype(MXU_DTYPE))
            for b in range(qb):
                rows = lay.idx(b, r)
                q = (q_ref[rows] * (HD ** -0.5)).astype(MXU_DTYPE)
                kcat = jnp.concatenate([kt[b], kt[b + 1]], axis=0)
                vcat = jnp.concatenate([vt[b], vt[b + 1]], axis=0)
                mask = first if b == 0 else band
                s = lax.dot_general(_stack_heads(q, m_a), kcat, NT_DIMS, preferred_element_type=F32)
                s = jnp.where(mask, s, NEG)
                m = jnp.max(s, axis=1, keepdims=True)
                p = jnp.exp(s - m)
                l = jnp.sum(p, axis=1, keepdims=True)
                o = jnp.dot(p.astype(MXU_DTYPE), vcat, preferred_element_type=F32) / l
                lse = jnp.broadcast_to(m + jnp.log(l), (2 * QBLK, LANES))
                o_ref[rows] = jnp.where(m_a, o[:QBLK], o[QBLK:])
                lse_ref[rows] = jnp.where(m_a, lse[:QBLK], lse[QBLK:])

        lay.per_residue(residue)

    cur, prev = lay.cur(lambda j: j), lay.prev(lambda j: j)
    return _pcall(body, name, lay.grid, [cur, cur, prev, cur, prev], [cur, cur],
                  [jax.ShapeDtypeStruct((S, AW), F32)] * 2, ("arbitrary", "arbitrary"), (qn, kn, kn, v, v), carry)


def _both_heads(t, m_a):
    other = pltpu.roll(t, HD, 1)
    return jnp.concatenate([jnp.where(m_a, t, other), jnp.where(m_a, other, t)], axis=0)


def attn_bwd(qn, kn, v, dycat, lse, delta, d, name, carry=None):
    S = qn.shape[0]
    lay = _AttnLayout(d, S)
    qb, nchunk = lay.qb, lay.nchunk

    def body(q_ref, kc_ref, kp_ref, vc_ref, vp_ref, do_ref, lse_ref, dl_ref,
             dq_ref, dk_ref, dv_ref, ck_ref, cv_ref):
        j = pl.program_id(1)
        i = nchunk - 1 - j
        m_a = _head_masks(QBLK)
        row, col = _band_masks(2 * QBLK)
        dist = row + QBLK - col
        band = (dist >= 0) & (dist <= QBLK)
        first = band & ((i > 0) | (col >= QBLK))

        def residue(r):
            def tiles(ref, cast):
                out = [ref[lay.idx(b, r)] for b in range(qb)]
                return [t.astype(MXU_DTYPE) for t in out] if cast else out

            def ktiles(cur_ref, prev_ref):
                return [prev_ref[lay.idx(0, r)].astype(MXU_DTYPE)] + tiles(cur_ref, True)

            qt = [(t * (HD ** -0.5)).astype(MXU_DTYPE) for t in tiles(q_ref, False)]
            dot_ = tiles(do_ref, True)
            lse_t = tiles(lse_ref, False)
            dl_t = tiles(dl_ref, False)
            kt = ktiles(kc_ref, kp_ref)
            vt = ktiles(vc_ref, vp_ref)
            dk_acc = [jnp.zeros((QBLK, LANES), F32) for _ in range(qb)]
            dv_acc = [jnp.zeros((QBLK, LANES), F32) for _ in range(qb)]
            crow = pl.ds(0, QBLK) if lay.nres == 1 else pl.ds(pl.multiple_of(r * QBLK, QBLK), QBLK)
            dk_acc[qb - 1] = jnp.where(j > 0, ck_ref[crow, :], 0.0)
            dv_acc[qb - 1] = jnp.where(j > 0, cv_ref[crow, :], 0.0)
            for x in range(qb):
                kcat = jnp.concatenate([kt[x], kt[x + 1]], axis=0)
                vcat = jnp.concatenate([vt[x], vt[x + 1]], axis=0)
                q2 = _stack_heads(qt[x], m_a)
                do2 = _stack_heads(dot_[x], m_a)
                lse2 = _both_heads(lse_t[x], m_a)
                dl2 = _both_heads(dl_t[x], m_a)
                lse2 = jnp.concatenate([lse2, lse2], axis=1)
                dl2 = jnp.concatenate([dl2, dl2], axis=1)
                s = lax.dot_general(q2, kcat, NT_DIMS, preferred_element_type=F32)
                p = jnp.exp(jnp.where(first if x == 0 else band, s, NEG) - lse2)
                dp = lax.dot_general(do2, vcat, NT_DIMS, preferred_element_type=F32)
                ds = p * (dp - dl2)
                dq = jnp.dot(ds.astype(MXU_DTYPE), kcat, preferred_element_type=F32)
                dq_ref[lay.idx(x, r)] = jnp.where(m_a, dq[:QBLK], dq[QBLK:]) * (HD ** -0.5)
                dk = jnp.dot(ds.T.astype(MXU_DTYPE), q2, preferred_element_type=F32)
                dv = jnp.dot(p.T.astype(MXU_DTYPE), do2, preferred_element_type=F32)
                if x == 0:
                    ck_ref[crow, :] = dk[:QBLK]
                    cv_ref[crow, :] = dv[:QBLK]
                else:
                    dk_acc[x - 1] = dk_acc[x - 1] + dk[:QBLK]
                    dv_acc[x - 1] = dv_acc[x - 1] + dv[:QBLK]
                dk_acc[x] = dk_acc[x] + dk[QBLK:]
                dv_acc[x] = dv_acc[x] + dv[QBLK:]
            for kb in range(qb):
                dk_ref[lay.idx(kb, r)] = dk_acc[kb]
                dv_ref[lay.idx(kb, r)] = dv_acc[kb]

        lay.per_residue(residue)

    cur, prev = lay.cur(lambda j: nchunk - 1 - j), lay.prev(lambda j: nchunk - 1 - j)
    carried = pltpu.VMEM((lay.nres * QBLK, LANES), F32)
    return _pcall(
        body, name, lay.grid, [cur, cur, prev, cur, prev, cur, cur, cur], [cur, cur, cur],
        [jax.ShapeDtypeStruct((S, AW), F32)] * 3, ("arbitrary", "arbitrary"),
        (qn, kn, kn, v, v, dycat, lse, delta), carry, scratch=[carried, carried])


def _shift_down(x, halo_prev, k, row):
    tm = x.shape[0]
    tail = jnp.concatenate([pltpu.roll(halo_prev, k, 0), jnp.zeros((tm - SUBLANES, x.shape[1]), x.dtype)], axis=0)
    return jnp.where(row < k, tail, pltpu.roll(x, k, 0))


def _shift_up(x, halo_next, k, row):
    tm = x.shape[0]
    head = jnp.concatenate([jnp.zeros((tm - SUBLANES, x.shape[1]), x.dtype), pltpu.roll(halo_next, SUBLANES - k, 0)], axis=0)
    return jnp.where(row >= tm - k, head, pltpu.roll(x, tm - k, 0))


def _conv_fwd(cu, halo_cu, cw_ref, row):
    u1 = _shift_down(cu, halo_cu, 1, row)
    u2 = _shift_down(cu, halo_cu, 2, row)
    cv = cw_ref[0:1, :] * u2 + cw_ref[1:2, :] * u1 + cw_ref[2:3, :] * cu + cw_ref[3:4, :]
    return cv, u1, u2


def combine_conv(os_, lses, proj, cw, name, carry=None):
    S = proj.shape[0]
    tm = _row_tile(S, 512)
    hb = tm // SUBLANES

    def body(o1, o2, o3, l1, l2, l3, pc_ref, ph_ref, cw_ref, ycat_ref, lse_ref):
        i = pl.program_id(0)
        for p in range(AW // LANES):
            cs = slice(p * LANES, (p + 1) * LANES)
            ls = [l[:, cs] for l in (l1, l2, l3)]
            mx = jnp.maximum(jnp.maximum(ls[0], ls[1]), ls[2])
            t = mx + jnp.log(jnp.exp(ls[0] - mx) + jnp.exp(ls[1] - mx) + jnp.exp(ls[2] - mx))
            lse_ref[:, cs] = t
            acc = jnp.zeros((tm, LANES), F32)
            for l, o in zip(ls, (o1, o2, o3)):
                acc = acc + jnp.exp(l - t) * o[:, cs]
            ycat_ref[:, cs] = acc.astype(ycat_ref.dtype)
        row = lax.broadcasted_iota(jnp.int32, (tm, CW), 0)
        gb, gc, u = pc_ref[:, 0:CW], pc_ref[:, CW:2 * CW], pc_ref[:, 2 * CW:3 * CW]
        halo_cu = jnp.where(i > 0, ph_ref[:, CW:2 * CW] * ph_ref[:, 2 * CW:3 * CW], 0.0)
        cv, _, _ = _conv_fwd(gc * u, halo_cu, cw_ref, row)
        ycat_ref[:, AW:AW + CW] = (gb * cv).astype(ycat_ref.dtype)

    ot = pl.BlockSpec((tm, AW), lambda i: (i, 0))
    return _pcall(
        body, name, (S // tm,),
        [ot] * 6 + [pl.BlockSpec((tm, 3 * CW), lambda i: (i, 1)),
                    pl.BlockSpec((SUBLANES, 3 * CW), lambda i: (jnp.maximum(i * hb - 1, 0), 1)),
                    pl.BlockSpec((SUBLANES, CW), lambda i: (0, 0))],
        [pl.BlockSpec((tm, D), lambda i: (i, 0)), ot],
        [jax.ShapeDtypeStruct((S, D), ACT_DTYPE), jax.ShapeDtypeStruct((S, AW), F32)],
        ("arbitrary",), (*os_, *lses, proj, proj, cw), carry)


def out_proj(ycat, x, vec, wout, name):
    S = x.shape[0]
    tm = _row_tile(S, 512)

    def body(yc_ref, x_ref, vec_ref, w_ref, xn_ref, y_ref):
        y = jnp.dot(yc_ref[...].astype(MXU_DTYPE), w_ref[...], preferred_element_type=F32)
        xn_ref[...] = x_ref[...] + vec_ref[3:4, :] * y
        y_ref[...] = y.astype(y_ref.dtype)

    t = pl.BlockSpec((tm, D), lambda i: (i, 0))
    return pl.pallas_call(
        body, name=name, grid=(S // tm,),
        in_specs=[t, t, pl.BlockSpec((SUBLANES, D), lambda i: (0, 0)),
                  pl.BlockSpec((D, D), lambda i: (0, 0))],
        out_specs=[t, t],
        out_shape=[jax.ShapeDtypeStruct((S, D), F32), jax.ShapeDtypeStruct((S, D), ACT_DTYPE)],
        compiler_params=_params(("arbitrary",)),
    )(ycat, x, vec, wout)


def out_proj_bwd(dxo, y, ycat, vec, wout, name, carry=None):
    S = dxo.shape[0]
    tm = _row_tile(S, 512)

    def body(dxo_ref, y_ref, yc_ref, vec_ref, w_ref, dyb_ref, dyc_ref, dl_ref, sums_ref):
        dxo = dxo_ref[...]
        dgate = jnp.sum(dxo * y_ref[...].astype(F32), axis=0, keepdims=True)
        dy = (vec_ref[3:4, :] * dxo).astype(MXU_DTYPE)
        dyb_ref[...] = dy
        dyc_ref[...] = lax.dot_general(dy, w_ref[...], NT_DIMS, preferred_element_type=F32)
        m_a = _head_masks(tm)
        for p in range(AW // LANES):
            cs = slice(p * LANES, (p + 1) * LANES)
            s_a, s_b = _pair_stat(dyc_ref[:, cs] * yc_ref[:, cs].astype(F32), m_a)
            dl_ref[:, cs] = jnp.where(m_a, s_a, s_b)
        _acc_rows(sums_ref, pl.program_id(0) == 0, (dgate,))

    t = pl.BlockSpec((tm, D), lambda i: (i, 0))
    at = pl.BlockSpec((tm, AW), lambda i: (i, 0))
    return _pcall(
        body, name, (S // tm,),
        [t, t, t, pl.BlockSpec((SUBLANES, D), lambda i: (0, 0)), pl.BlockSpec((D, D), lambda i: (0, 0))],
        [t, t, at, pl.BlockSpec((SUBLANES, D), lambda i: (0, 0))],
        [jax.ShapeDtypeStruct((S, D), MXU_DTYPE), jax.ShapeDtypeStruct((S, D), F32),
         jax.ShapeDtypeStruct((S, AW), F32), jax.ShapeDtypeStruct((SUBLANES, D), F32)],
        ("arbitrary",), (dxo, y, ycat, vec, wout), carry)


def mixer_mid_bwd(dqs, dks, dvs, proj, dycat, gvec, cw, name, carry=None):
    S = proj.shape[0]
    tm = _row_tile(S, 256)
    hb = tm // SUBLANES
    nsl = S // SUBLANES
    ntile = S // tm

    def body(dq1, dq2, dq3, dk1, dk2, dk3, dv1, dv2, dv3, pr_ref, pp_ref, pn_ref, dyc_ref, dyn_ref,
             g_ref, cw_ref, dp_ref, sums_ref):
        i = pl.program_id(0)
        m_a = _head_masks(tm)
        gsum = []
        for which, parts in ((0, (dq1, dq2, dq3)), (1, (dk1, dk2, dk3))):
            acc_g = []
            for p in range(AW // LANES):
                lo = which * AW + p * LANES
                cs = slice(p * LANES, (p + 1) * LANES)
                xp = pr_ref[:, lo:lo + LANES]
                s_a, s_b = _pair_stat(xp * xp, m_a)
                rr = jnp.where(m_a, lax.rsqrt(s_a * (1.0 / HD) + EPS), lax.rsqrt(s_b * (1.0 / HD) + EPS))
                xh = xp * rr
                dn = parts[0][:, cs] + parts[1][:, cs] + parts[2][:, cs]
                acc_g.append(jnp.sum(dn * xh, axis=0, keepdims=True))
                t = dn * g_ref[which:which + 1, cs]
                t_a, t_b = _pair_stat(t * xh, m_a)
                mean = jnp.where(m_a, t_a, t_b) * (1.0 / HD)
                dp_ref[:, lo:lo + LANES] = (rr * (t - xh * mean)).astype(dp_ref.dtype)
            gsum.append(jnp.concatenate(acc_g, axis=1))
        dp_ref[:, 2 * AW:3 * AW] = (dv1[...] + dv2[...] + dv3[...]).astype(dp_ref.dtype)
        row = lax.broadcasted_iota(jnp.int32, (tm, CW), 0)
        base = 3 * AW
        gb, gc, u = pr_ref[:, base:base + CW], pr_ref[:, base + CW:base + 2 * CW], pr_ref[:, base + 2 * CW:base + 3 * CW]
        cu = gc * u
        halo_cu = jnp.where(i > 0, pp_ref[:, CW:2 * CW] * pp_ref[:, 2 * CW:3 * CW], 0.0)
        cv, u1, u2 = _conv_fwd(cu, halo_cu, cw_ref, row)
        dyc = dyc_ref[...]
        dp_ref[:, base:base + CW] = (dyc * cv).astype(dp_ref.dtype)
        dcv = dyc * gb
        halo_dcv = jnp.where(i < ntile - 1, dyn_ref[...] * pn_ref[:, 0:CW], 0.0)
        d1 = _shift_up(dcv, halo_dcv, 1, row)
        d2 = _shift_up(dcv, halo_dcv, 2, row)
        dcu = cw_ref[2:3, :] * dcv + cw_ref[1:2, :] * d1 + cw_ref[0:1, :] * d2
        dp_ref[:, base + CW:base + 2 * CW] = (dcu * u).astype(dp_ref.dtype)
        dp_ref[:, base + 2 * CW:base + 3 * CW] = (dcu * gc).astype(dp_ref.dtype)
        rows = (gsum[0], gsum[1],
                jnp.sum(dcv * u2, axis=0, keepdims=True), jnp.sum(dcv * u1, axis=0, keepdims=True),
                jnp.sum(dcv * cu, axis=0, keepdims=True), jnp.sum(dcv, axis=0, keepdims=True))
        _acc_rows(sums_ref, i == 0, rows)

    at = pl.BlockSpec((tm, AW), lambda i: (i, 0))
    return _pcall(
        body, name, (ntile,),
        [at] * 9 + [
            pl.BlockSpec((tm, INC), lambda i: (i, 0)),
            pl.BlockSpec((SUBLANES, 3 * CW), lambda i: (jnp.maximum(i * hb - 1, 0), 1)),
            pl.BlockSpec((SUBLANES, 3 * CW), lambda i: (jnp.minimum((i + 1) * hb, nsl - 1), 1)),
            pl.BlockSpec((tm, CW), lambda i: (i, 1)),
            pl.BlockSpec((SUBLANES, CW), lambda i: (jnp.minimum((i + 1) * hb, nsl - 1), 1)),
            pl.BlockSpec((SUBLANES, AW), lambda i: (0, 0)),
            pl.BlockSpec((SUBLANES, CW), lambda i: (0, 0))],
        [pl.BlockSpec((tm, INC), lambda i: (i, 0)), pl.BlockSpec((SUBLANES, AW), lambda i: (0, 0))],
        [jax.ShapeDtypeStruct((S, INC), MXU_DTYPE), jax.ShapeDtypeStruct((SUBLANES, AW), F32)],
        ("arbitrary",), (*dqs, *dks, *dvs, proj, proj, proj, dycat, dycat, gvec, cw), carry)


def mixer_in_bwd(dxo, x, dproj, vec, winp, name, carry=None):
    S = x.shape[0]
    tm = _row_tile(S, 512)
    pc = INC // NCHIP

    def body(dxo_ref, x_ref, dp_ref, vec_ref, w_ref, dxi_ref, sums_ref):
        xhat, r, gain, ng, sc, _, _ = _ada(x_ref[...], vec_ref)
        dh = jnp.zeros((tm, D), F32)
        for j in range(NCHIP):
            dh = dh + lax.dot_general(dp_ref[:, j * pc:(j + 1) * pc], w_ref[j], NT_DIMS, preferred_element_type=F32)
        dx, dshift, dscale, dng = _ada_bwd(dh, xhat, r, gain, ng, sc)
        dxi_ref[...] = dxo_ref[...] + dx
        _acc_rows(sums_ref, pl.program_id(0) == 0, (dshift, dscale, dng))

    t = pl.BlockSpec((tm, D), lambda i: (i, 0))
    return _pcall(
        body, name, (S // tm,),
        [t, t, pl.BlockSpec((tm, INC), lambda i: (i, 0)),
         pl.BlockSpec((SUBLANES, D), lambda i: (0, 0)),
         pl.BlockSpec((NCHIP, D, pc), lambda i: (0, 0, 0), pipeline_mode=pl.Buffered(1))],
        [t, pl.BlockSpec((SUBLANES, D), lambda i: (0, 0))],
        [jax.ShapeDtypeStruct((S, D), F32), jax.ShapeDtypeStruct((SUBLANES, D), F32)],
        ("arbitrary",), (dxo, x, dproj, vec, winp), carry)


def loss_head(xf, target, name):
    S = xf.shape[0]
    tm = _row_tile(S, 1024)

    def body(x_ref, t_ref, dy_ref, l_ref):
        diff = x_ref[...] - t_ref[...]
        dy_ref[...] = diff * (1.0 / D)
        part = jnp.sum(jnp.sum(diff * diff, axis=0, keepdims=True), axis=1, keepdims=True) * (0.5 / D)

        @pl.when(pl.program_id(0) == 0)
        def _():
            l_ref[...] = jnp.zeros_like(l_ref)
        l_ref[...] += jnp.broadcast_to(part, l_ref.shape)

    t = pl.BlockSpec((tm, D), lambda i: (i, 0))
    return pl.pallas_call(
        body, name=name, grid=(S // tm,),
        in_specs=[t, t],
        out_specs=[t, pl.BlockSpec((SUBLANES, LANES), lambda i: (0, 0))],
        out_shape=[jax.ShapeDtypeStruct((S, D), F32), jax.ShapeDtypeStruct((SUBLANES, LANES), F32)],
        compiler_params=_params(("arbitrary",)),
    )(xf, target)


def _vec(mod_l, ng_l, i):
    m = mod_l.reshape(3, 3, D)
    rows = jnp.stack([ng_l[i], m[i, 1], m[i, 0], m[i, 2]])
    return jnp.concatenate([rows, jnp.zeros((SUBLANES - 4, D), F32)], axis=0)


def local_step(x, target, mods, ngs, gvecs, cws, shards, w_first, cflag):
    saved = []
    weights = [dict(w1=[None, None], w2=[None, None]) for _ in range(2)]
    weights[0]["w1"][0], weights[0]["w2"][0] = w_first[0], w_first[1].reshape(DFF, D)
    h = x
    for l in range(2):
        w, sh = weights[l], shards[l]
        nxt = shards[l + 1] if l == 0 else None
        vecs = [_vec(mods[l], ngs[l], i) for i in range(3)]
        x0 = h
        (x1, a0, f0), (win, wout, w2b) = ffn_fwd(x0, vecs[0], w["w1"][0], w["w2"][0], 0.5, f"ffn_fwd_l{l}a",
                                                 carry=Carry("gather", [sh["win"], sh["wout"], sh["w2"][1]]))
        w["win"], w["wout"], w["w2"][1] = win, wout.reshape(D, D), w2b.reshape(DFF, D)
        proj, h1b, qn, kn, v = mixer_in(x1, vecs[1], w["win"], gvecs[l], f"mixer_in_l{l}")
        os_, lses, w1b = [], [], {}
        for d in DILATIONS:
            rows = {1: slice(0, D // 2), 16: slice(D // 2, D)}.get(d)
            carry = Carry("gather", [sh["w1"][1][rows]]) if rows else None
            (o, lse_d), w1b[d] = attn_fwd(qn, kn, v, d, f"attn_fwd_l{l}_d{d}", carry=carry)
            os_.append(o)
            lses.append(lse_d)
        w["w1"][1] = jnp.concatenate([w1b[1][0], w1b[16][0]], axis=1)
        (ycat, lse), got = combine_conv(os_, lses, proj, cws[l], f"combine_conv_l{l}",
                                        carry=Carry("gather", [nxt["w2"][0]]) if nxt else None)
        if nxt:
            weights[1]["w2"][0] = got[0].reshape(DFF, D)
        x2, y = out_proj(ycat, x1, vecs[1], w["wout"], f"out_proj_l{l}")
        (x3, a2, f2), got = ffn_fwd(x2, vecs[2], w["w1"][1], w["w2"][1], 0.5, f"ffn_fwd_l{l}b",
                                    carry=Carry("gather", [nxt["w1"][0]]) if nxt else None)
        if nxt:
            weights[1]["w1"][0] = got[0]
        saved.append(dict(vecs=vecs, x0=x0, a0=a0, f0=f0, x1=x1, proj=proj, h1b=h1b, qn=qn, kn=kn, v=v,
                          ycat=ycat, lse=lse, y=y, x2=x2, a2=a2, f2=f2))
        h = x3
    dx, loss_blk = loss_head(h, target, "loss_head")
    sums, totals, g_prev = [None, None], [None, None], None
    w2r = DFF // NCHIP
    for l in (1, 0):
        w, s = weights[l], saved[l]
        vecs = s["vecs"]
        ride = g_prev is not None
        own = l == 0
        mine, other = [None] * 6, [None] * 6

        def half_sum(group, recv, k0):
            return [add_half(g, r, cflag, f"add_sibling_l{l}_{k0 + j}") for j, (g, r) in enumerate(zip(group, recv))]

        def chip_sum(landed, k0):
            return [sum_chips(t, f"sum_chips_l{l}_{k0 + j}") for j, t in enumerate(landed)]

        (dx, hb, dfb, act, da, sums2), got = ffn_bwd(
            dx, s["x2"], s["a2"], s["f2"], vecs[2], w["w1"][1], w["w2"][1], 0.5, f"ffn_bwd_l{l}b",
            carry=Carry("swap_halves", g_prev) if ride else None)
        dw1b, _ = wgrad(hb, da, D, HALF, f"wgrad_w1_l{l}b")
        dw2b, _ = wgrad(act, dfb, HALF, D, f"wgrad_w2_l{l}b")
        if ride:
            wire = [add_half(g_prev[k], got[k], cflag, f"add_sibling_l{l + 1}_{k}") for k in range(6)]
        g_ffn_b = [dw1b, dw2b.reshape(NCHIP, w2r, D)]
        (dyb, dycat, delta, sums_o), got = out_proj_bwd(
            dx, s["y"], s["ycat"], vecs[1], w["wout"], f"out_proj_bwd_l{l}",
            carry=Carry("swap_halves", g_ffn_b) if own else None)
        dwout, _ = wgrad(s["ycat"].astype(MXU_DTYPE), dyb, D // 2, D, f"wgrad_wout_l{l}")
        if own:
            wire_ffn_b = half_sum(g_ffn_b, got, 4)
        dqs, dks, dvs, landed = [], [], [], {}
        for d in DILATIONS:
            carry = None
            if ride and d == 1:
                carry = Carry("scatter", wire[3:])
            if ride and d == 16:
                carry = Carry("scatter", wire[:3])
            if own and d == 4:
                carry = Carry("scatter", wire_ffn_b)
            (dq, dk, dv), landed[d] = attn_bwd(s["qn"], s["kn"], s["v"], dycat, s["lse"], delta, d,
                                               f"attn_bwd_l{l}_d{d}", carry=carry)
            dqs.append(dq)
            dks.append(dk)
            dvs.append(dv)
        if ride:
            tot = [sum_chips(t, f"sum_chips_l{l + 1}_{k}") for k, t in enumerate(list(landed[16]) + list(landed[1]))]
        if own:
            mine[4:6] = chip_sum(landed[4], 4)
        ready = (tot if ride else []) + (mine[4:6] if own else [])
        (dproj, sums_m), got = mixer_mid_bwd(dqs, dks, dvs, s["proj"], dycat, gvecs[l], cws[l], f"mixer_mid_bwd_l{l}",
                                             carry=Carry("swap", ready) if ready else None)
        if ride:
            totals[l + 1] = (tot, list(got[:6]))
        if own:
            other[4:6] = list(got[-2:])
        dwin, _ = wgrad(s["h1b"], dproj, D, INC // NCHIP, f"wgrad_win_l{l}")
        g_mixer = [dwin, dwout.reshape(NCHIP, D // NCHIP, D)]
        (dx, sums1), got = mixer_in_bwd(dx, s["x1"], dproj, vecs[1], w["win"], f"mixer_in_bwd_l{l}",
                                        carry=Carry("swap_halves", g_mixer) if own else None)
        if own:
            wire_mixer = half_sum(g_mixer, got, 2)
        (dx, hb, dfb, act, da, sums0), _ = ffn_bwd(
            dx, s["x0"], s["a0"], s["f0"], vecs[0], w["w1"][0], w["w2"][0], 0.5, f"ffn_bwd_l{l}a")
        dw1a, got = wgrad(hb, da, D, HALF, f"wgrad_w1_l{l}a", carry=Carry("scatter", wire_mixer) if own else None)
        if own:
            mine[2:4] = chip_sum(got, 2)
        dw2a, got = wgrad(act, dfb, HALF, D, f"wgrad_w2_l{l}a", carry=Carry("swap", mine[2:4]) if own else None)
        g_ffn_a = [dw1a, dw2a.reshape(NCHIP, w2r, D)]
        if own:
            other[2:4] = list(got)
            wire_ffn_a = half_sum(g_ffn_a, run_carry(Carry("swap_halves", g_ffn_a), "swap_halves_tail"), 0)
            mine[0:2] = chip_sum(run_carry(Carry("scatter", wire_ffn_a), "scatter_grads_tail"), 0)
            other[0:2] = list(run_carry(Carry("swap", mine[0:2]), "swap_totals_tail"))
            totals[l] = (mine, other)
        g_prev = g_ffn_a + g_mixer + g_ffn_b
        sums[l] = (sums0, sums1, sums_o, sums2, sums_m)
    return loss_blk, dx, totals, sums


def small_all_gather(blk, name):
    m_per, n = blk.shape

    def body(x_ref, out_ref, send_sems, recv_sems, local_sem):
        x, y, c = _here()
        me, sibling = (x, y, c), (x, y, 1 - c)
        chips = [(1 - x, y), (x, 1 - y), (1 - x, 1 - y)]

        def rows(px, py, pc):
            return out_ref.at[pl.ds((4 * px + 2 * py + pc) * m_per, m_per), :]

        def copy(k, block, to, src=None):
            return pltpu.make_async_remote_copy(
                src_ref=rows(*block) if src is None else src, dst_ref=rows(*block),
                send_sem=send_sems.at[k], recv_sem=recv_sems.at[k], device_id=to, device_id_type=MESH)

        mine = pltpu.make_async_copy(x_ref, rows(*me), local_sem)
        mine.start()
        first = [copy(0, me, sibling, src=x_ref)]
        first += [copy(1 + j, me, (*chip, c), src=x_ref) for j, chip in enumerate(chips)]
        for cp in first:
            cp.start()
        passed = [copy(4 + j, (*chip, c), sibling) for j, chip in enumerate(chips)]
        for j, chip in enumerate(chips):
            copy(1 + j, (*chip, c), me).wait_recv()
            passed[j].start()
        copy(0, sibling, me).wait_recv()
        for j, chip in enumerate(chips):
            copy(4 + j, (*chip, 1 - c), me).wait_recv()
        for cp in first + passed:
            cp.wait_send()
        mine.wait()

    return pl.pallas_call(
        body, name=name,
        out_shape=jax.ShapeDtypeStruct((NDEV * m_per, n), blk.dtype),
        in_specs=[pl.BlockSpec(memory_space=pltpu.VMEM)],
        out_specs=pl.BlockSpec(memory_space=pltpu.VMEM),
        scratch_shapes=[pltpu.SemaphoreType.DMA((7,)), pltpu.SemaphoreType.DMA((7,)), pltpu.SemaphoreType.DMA],
        compiler_params=pltpu.CompilerParams(vmem_limit_bytes=VMEM_LIMIT),
    )(blk)


EW_BLOCK_BYTES = 1 << 20


def _ew_rows(rows, cols):
    want = max(16, EW_BLOCK_BYTES // (4 * cols))
    best = None
    for t in range(16, rows + 1, 16):
        if rows % t == 0 and t <= want:
            best = t
    return best if best is not None else rows


def add_half(g, recv, cflag, name):
    pieces, r, cols = g.shape
    r2 = r // 2
    tr = _ew_rows(r2, cols)
    nt = r2 // tr

    def body(c_ref, g_ref, r_ref, o_ref):
        o_ref[...] = (g_ref[...] + r_ref[...]).astype(o_ref.dtype)

    half = pl.BlockSpec((None, tr, cols), lambda j, i, c_ref: (j, i, 0))
    return pl.pallas_call(
        body, name=name,
        grid_spec=pltpu.PrefetchScalarGridSpec(
            num_scalar_prefetch=1, grid=(pieces, nt),
            in_specs=[pl.BlockSpec((None, tr, cols), lambda j, i, c_ref: (j, c_ref[0] * nt + i, 0)), half],
            out_specs=half),
        out_shape=jax.ShapeDtypeStruct((pieces, r2, cols), WIRE_DTYPE),
        compiler_params=_params(("arbitrary", "arbitrary")),
    )(cflag, g, recv)


def sum_chips(recv, name):
    _, r, cols = recv.shape
    tr = _ew_rows(r, cols)

    def body(r_ref, o_ref):
        acc = r_ref[0].astype(F32)
        for k in range(1, NCHIP):
            acc = acc + r_ref[k].astype(F32)
        o_ref[...] = acc

    return pl.pallas_call(
        body, name=name, grid=(r // tr,),
        in_specs=[pl.BlockSpec((NCHIP, tr, cols), lambda i: (0, i, 0))],
        out_specs=pl.BlockSpec((tr, cols), lambda i: (i, 0)),
        out_shape=jax.ShapeDtypeStruct((r, cols), F32),
        compiler_params=_params(("arbitrary",)),
    )(recv)


def sum_devices(rows8, name):
    def body(r_ref, o_ref):
        acc = r_ref[0:1, :]
        for k in range(1, NDEV):
            acc = acc + r_ref[k:k + 1, :]
        o_ref[...] = jnp.broadcast_to(acc, o_ref.shape)

    return pl.pallas_call(
        body, name=name, out_shape=jax.ShapeDtypeStruct(rows8.shape, F32),
        in_specs=[pl.BlockSpec(memory_space=pltpu.VMEM)], out_specs=pl.BlockSpec(memory_space=pltpu.VMEM),
        compiler_params=pltpu.CompilerParams(vmem_limit_bytes=VMEM_LIMIT),
    )(rows8)


def adamw(w, m, v, srcs, cflag, name, halves=False):
    planes, r, cols = w.shape
    rh = r // 2 if halves else r
    tr = _ew_rows(rh, cols)
    nth = rh // tr
    flat = [a for s in srcs for a in (s if halves else (s,))]
    ns = len(flat)
    per = ns // planes

    def body(c_ref, w_ref, m_ref, v_ref, *rest):
        s_refs, (g_ref, d_ref, mo_ref, vo_ref) = rest[:ns], rest[ns:]
        p, i = pl.program_id(0), pl.program_id(1)
        if halves:
            mine = jnp.logical_not(jnp.logical_xor(i >= nth, c_ref[0] == 1))
            blocks = [jnp.where(mine, s_refs[2 * k][...], s_refs[2 * k + 1][...]) for k in range(planes)]
        else:
            blocks = [s[...] for s in s_refs]
        g = blocks[0]
        for k in range(1, planes):
            g = jnp.where(p == k, blocks[k], g)
        g_ref[...] = g
        m_new = ADAM_B1 * m_ref[...] + (1.0 - ADAM_B1) * g
        v_new = ADAM_B2 * v_ref[...] + (1.0 - ADAM_B2) * (g * g)
        mo_ref[...] = m_new
        vo_ref[...] = v_new
        m_hat = m_new / (1.0 - ADAM_B1 ** ADAM_STEP)
        v_hat = v_new / (1.0 - ADAM_B2 ** ADAM_STEP)
        d_ref[...] = -ADAM_LR * (m_hat / (jnp.sqrt(v_hat) + ADAM_EPS) + ADAM_WD * w_ref[...])

    pt = pl.BlockSpec((None, tr, cols), lambda p, i: (p, i, 0))
    st = [pl.BlockSpec((tr, cols), functools.partial(lambda k, p, i: (jnp.where(p == k, i % nth, 0), 0), j // per))
          for j in range(ns)]
    return pl.pallas_call(
        body, name=name, grid=(planes, r // tr),
        in_specs=[pl.BlockSpec(memory_space=pltpu.SMEM), pt, pt, pt] + st,
        out_specs=[pt] * 4,
        out_shape=[jax.ShapeDtypeStruct(w.shape, F32)] * 4,
        compiler_params=_params(("arbitrary", "arbitrary")),
    )(cflag, w, m, v, *flat)


ADA_COLS = 9 * D // NCHIP


def mod_fwd(c_all, w_ada, b_shard, name):
    def body(c_ref, w_ref, b_ref, o_ref):
        cc = c_ref[...]
        sc = cc * jax.nn.sigmoid(cc)
        o_ref[...] = jnp.dot(sc, w_ref[...], preferred_element_type=F32,
                             precision=lax.Precision.HIGHEST) + b_ref[...]

    return pl.pallas_call(
        body, name=name, grid=(2,),
        in_specs=[pl.BlockSpec((NDEV, D), lambda l: (0, 0)),
                  pl.BlockSpec((None, D, ADA_COLS), lambda l: (l, 0, 0)),
                  pl.BlockSpec((None, 1, ADA_COLS), lambda l: (l, 0, 0))],
        out_specs=pl.BlockSpec((None, NDEV, ADA_COLS), lambda l: (l, 0, 0)),
        out_shape=jax.ShapeDtypeStruct((2, NDEV, ADA_COLS), F32),
        compiler_params=_params(("arbitrary",)),
    )(c_all, w_ada, b_shard.reshape(2, 1, ADA_COLS))


def wada_grad(c_all_t, dmod, name):
    ct = ADA_COLS // 3

    def body(c_ref, d_ref, o_ref):
        cc = c_ref[...]
        sc = cc * jax.nn.sigmoid(cc)
        acc = sc[:, 0:1] * d_ref[0:1, :]
        for b in range(1, NDEV):
            acc = acc + sc[:, b:b + 1] * d_ref[b:b + 1, :]
        o_ref[...] = acc

    return pl.pallas_call(
        body, name=name, grid=(2, 3),
        in_specs=[pl.BlockSpec((D, LANES), lambda l, j: (0, 0)),
                  pl.BlockSpec((None, NDEV, ct), lambda l, j: (l, 0, j))],
        out_specs=pl.BlockSpec((None, D, ct), lambda l, j: (l, 0, j)),
        out_shape=jax.ShapeDtypeStruct((2, D, ADA_COLS), F32),
        compiler_params=_params(("arbitrary", "arbitrary")),
    )(c_all_t, dmod)


def _pad_rows(row, rows=SUBLANES):
    return jnp.concatenate([row[None, :], jnp.zeros((rows - 1, row.shape[0]), row.dtype)], axis=0)


def kernel(x, c, w_ada, b_ada, norm_g, w_in, q_norm_g, k_norm_g, conv_w, conv_b, w_out, ffn_w1, ffn_w2, loss_target, m_w_ada, m_b_ada, m_norm_g, m_w_in, m_q_norm_g, m_k_norm_g, m_conv_w, m_conv_b, m_w_out, m_ffn_w1, m_ffn_w2, v_w_ada, v_b_ada, v_norm_g, v_w_in, v_q_norm_g, v_k_norm_g, v_conv_w, v_conv_b, v_w_out, v_ffn_w1, v_ffn_w2):
    ix, iy, ic = lax.axis_index("x"), lax.axis_index("y"), lax.axis_index("c")
    chip = 2 * ix + iy
    dev = 2 * chip + ic
    cflag = jnp.reshape(ic, (1,)).astype(jnp.int32)
    ngw = norm_g.shape[-1]
    cww = conv_w.shape[-1]

    pack = jnp.concatenate([c[0], norm_g.reshape(-1), conv_w.reshape(-1)])
    got = small_all_gather(_pad_rows(pack), "gather_c_normg_convw")[::SUBLANES]
    c_all = got[:, :D]
    per_chip = got[::2]
    ng_full = jnp.concatenate([per_chip[j, D:D + 6 * ngw].reshape(2, 3, ngw) for j in range(NCHIP)], axis=-1)
    cw_full = jnp.concatenate([per_chip[j, D + 6 * ngw:].reshape(2, 3, cww) for j in range(NCHIP)], axis=-1)

    b_shard = lax.dynamic_slice_in_dim(b_ada, chip * ADA_COLS, ADA_COLS, axis=1)
    mod_blk = mod_fwd(c_all, w_ada, b_shard, "mod_fwd").reshape(2 * NDEV, ADA_COLS)
    mod_all = small_all_gather(mod_blk, "gather_mod").reshape(NDEV, 2, NDEV, ADA_COLS)[::2]
    mod_mine = lax.dynamic_index_in_dim(mod_all, dev, axis=2, keepdims=False)
    mods = [mod_mine[:, l, :].reshape(-1) for l in range(2)]

    shards, gvecs, cws = [], [], []
    for l in range(2):
        shards.append(dict(w1=[ffn_w1[l, i].astype(MXU_DTYPE) for i in range(2)],
                           w2=[ffn_w2[l, i].astype(MXU_DTYPE) for i in range(2)],
                           win=w_in[l].astype(MXU_DTYPE), wout=w_out[l].astype(MXU_DTYPE)))
        gv = jnp.stack([jnp.tile(q_norm_g[l], AW // HD), jnp.tile(k_norm_g[l], AW // HD)])
        gvecs.append(jnp.concatenate([gv, jnp.zeros((SUBLANES - 2, AW), F32)], axis=0))
        cws.append(jnp.concatenate([cw_full[l], conv_b[l][None, :], jnp.zeros((SUBLANES - 4, CW), F32)], axis=0))
    w_first = gather_split([shards[0]["w1"][0], shards[0]["w2"][0]], "gather_first_ffn")

    loss_blk, dx, totals, sums = local_step(x[0], loss_target[0], mods, [ng_full[0], ng_full[1]], gvecs, cws,
                                            shards, w_first, cflag)
    loss = lax.psum(loss_blk[0, 0], ("x", "y", "c"))

    dmods, dngs, dqg, dkg, dcw, dcb = [], [], [], [], [], []
    for l in range(2):
        s0, s1, so, s2, sm = sums[l]
        dmods.append(jnp.concatenate([s0[0], s0[1], s0[3], s1[0], s1[1], so[0], s2[0], s2[1], s2[3]]))
        dngs.append(jnp.concatenate([s0[2], s1[2], s2[2]]))
        dqg.append(sm[0].reshape(AW // HD, HD).sum(0))
        dkg.append(sm[1].reshape(AW // HD, HD).sum(0))
        dcw.append(sm[2:5].reshape(-1))
        dcb.append(sm[5])
    small = jnp.concatenate(dmods + dngs + dqg + dkg + dcw + dcb)
    small_all = small_all_gather(_pad_rows(small), "gather_small_grads")[::SUBLANES]
    nm = 9 * D
    dmod_all = small_all[:, :2 * nm].reshape(NDEV, 2, NCHIP, ADA_COLS)
    dmod_mine = lax.dynamic_index_in_dim(dmod_all, chip, axis=2, keepdims=False).transpose(1, 0, 2)
    tot = sum_devices(small_all, "sum_small_grads")[0]
    o = 2 * nm
    g_b_ada = tot[:o].reshape(2, nm)
    g_norm_g = lax.dynamic_slice_in_dim(tot[o:o + 6 * D].reshape(2, 3, D), chip * ngw, ngw, axis=2)
    o += 6 * D
    g_qg = tot[o:o + 2 * HD].reshape(2, HD)
    o += 2 * HD
    g_kg = tot[o:o + 2 * HD].reshape(2, HD)
    o += 2 * HD
    g_cw = lax.dynamic_slice_in_dim(tot[o:o + 6 * CW].reshape(2, 3, CW), chip * cww, cww, axis=2)
    o += 6 * CW
    g_cb = tot[o:o + 2 * CW].reshape(2, CW)

    c_all_t = jnp.concatenate([c_all.T, jnp.zeros((D, LANES - NDEV), F32)], axis=1)
    g_wada_src = wada_grad(c_all_t, dmod_mine, "wada_grad")

    def halves(k_of_plane):
        return [(totals[l][0][k], totals[l][1][k]) for l, k in k_of_plane]

    r_wada = adamw(w_ada, m_w_ada, v_w_ada, [g_wada_src[0], g_wada_src[1]], cflag, "adamw_w_ada")
    r_win = adamw(w_in, m_w_in, v_w_in, halves([(0, 2), (1, 2)]), cflag, "adamw_w_in", halves=True)
    r_wout = adamw(w_out, m_w_out, v_w_out, halves([(0, 3), (1, 3)]), cflag, "adamw_w_out", halves=True)
    r_w1 = adamw(ffn_w1.reshape(4, D, HALF), m_ffn_w1.reshape(4, D, HALF), v_ffn_w1.reshape(4, D, HALF),
                 halves([(0, 0), (0, 4), (1, 0), (1, 4)]), cflag, "adamw_ffn_w1", halves=True)
    w2r = DFF // NCHIP
    r_w2 = adamw(ffn_w2.reshape(4, w2r, D), m_ffn_w2.reshape(4, w2r, D), v_ffn_w2.reshape(4, w2r, D),
                 halves([(0, 1), (0, 5), (1, 1), (1, 5)]), cflag, "adamw_ffn_w2", halves=True)
    r_w1 = [t.reshape(ffn_w1.shape) for t in r_w1]
    r_w2 = [t.reshape(ffn_w2.shape) for t in r_w2]

    smalls = [("b_ada", b_ada, m_b_ada, v_b_ada, g_b_ada), ("norm_g", norm_g, m_norm_g, v_norm_g, g_norm_g),
              ("q_norm_g", q_norm_g, m_q_norm_g, v_q_norm_g, g_qg), ("k_norm_g", k_norm_g, m_k_norm_g, v_k_norm_g, g_kg),
              ("conv_w", conv_w, m_conv_w, v_conv_w, g_cw), ("conv_b", conv_b, m_conv_b, v_conv_b, g_cb)]
    n_small = sum(t[1].size for t in smalls)
    pad = (-n_small) % (16 * LANES)

    def packed(idx):
        flat = jnp.concatenate([t[idx].reshape(-1) for t in smalls] + [jnp.zeros((pad,), F32)])
        return flat.reshape(-1, LANES)

    r_small = adamw(packed(1)[None], packed(2)[None], packed(3)[None], [packed(4)], cflag, "adamw_small")
    small_out = {}
    o = 0
    for name_, w_, _, _, _ in smalls:
        small_out[name_] = [t.reshape(-1)[o:o + w_.size].reshape(w_.shape) for t in r_small]
        o += w_.size

    res = {"w_ada": r_wada, "w_in": r_win, "w_out": r_wout, "ffn_w1": r_w1, "ffn_w2": r_w2, **small_out}
    order = ["w_ada", "b_ada", "norm_g", "w_in", "q_norm_g", "k_norm_g", "conv_w", "conv_b", "w_out", "ffn_w1", "ffn_w2"]
    outs = [loss, dx[None]]
    for k in range(4):
        outs += [res[nm_][k] for nm_ in order]
    return tuple(outs)
```

```python
import functools

import jax
import jax.numpy as jnp
from jax import lax
from jax.experimental import pallas as pl
from jax.experimental.pallas import tpu as pltpu

F32 = jnp.float32
MXU_DTYPE = jnp.bfloat16
ACT_DTYPE = jnp.bfloat16
WIRE_DTYPE = jnp.bfloat16

D = 1024
HD = 64
AW = 512
CW = 512
DFF = 2816
HALF = DFF // 2
INC = 3 * AW + 3 * CW
NCHIP = 4
NDEV = 8
QBLK = 128
ATTN_QBLOCKS = 4
ATTN_CHUNK_ROWS = 2048
DILATIONS = (1, 4, 16)
EPS = 1e-6
NEG = -1e30
LANES = 128
SUBLANES = 8
VMEM_LIMIT = 56 * 1024 * 1024

ADAM_LR = 0.001
ADAM_B1 = 0.9
ADAM_B2 = 0.999
ADAM_EPS = 1e-08
ADAM_WD = 0.01
ADAM_STEP = 10

NT_DIMS = (((1,), (1,)), ((), ()))
TN_DIMS = (((0,), (0,)), ((), ()))


def _params(sem, vmem=VMEM_LIMIT):
    return pltpu.CompilerParams(dimension_semantics=sem, vmem_limit_bytes=vmem)


def _row_tile(n, want):
    t = min(n, want)
    assert n % t == 0
    return t


def _ada(xt, vec_ref):
    ng, sc, sh, gt = vec_ref[0:1, :], vec_ref[1:2, :], vec_ref[2:3, :], vec_ref[3:4, :]
    r = lax.rsqrt(jnp.mean(xt * xt, axis=-1, keepdims=True) + EPS)
    return xt * r, r, ng * (1.0 + sc), ng, sc, sh, gt


def _ada_bwd(dh, xhat, r, gain, ng, sc):
    dshift = jnp.sum(dh, axis=0, keepdims=True)
    dhx = dh * xhat
    dscale = jnp.sum(dhx, axis=0, keepdims=True) * ng
    dng = jnp.sum(dhx, axis=0, keepdims=True) * (1.0 + sc)
    dxhat = dh * gain
    dx = r * (dxhat - xhat * jnp.mean(dxhat * xhat, axis=-1, keepdims=True))
    return dx, dshift, dscale, dng


def _acc_rows(sums_ref, first, rows):
    @pl.when(first)
    def _():
        sums_ref[...] = jnp.zeros_like(sums_ref)
    for k, row in enumerate(rows):
        sums_ref[k:k + 1, :] += row


MESH = pl.DeviceIdType.MESH
ANY = pl.BlockSpec(memory_space=pl.ANY)


def _here():
    return lax.axis_index("x"), lax.axis_index("y"), lax.axis_index("c")


def _ici_copies(src_refs, dst_refs, send_sems, recv_sems, local_sems, scatter):
    x, y, c = _here()
    my_chip = 2 * x + y
    peers = [(1 - x, y), (x, 1 - y), (1 - x, 1 - y)]
    local, out, inc = [], [], []
    for a, (src, dst) in enumerate(zip(src_refs, dst_refs)):
        local.append(pltpu.make_async_copy(src.at[my_chip] if scatter else src, dst.at[my_chip], local_sems.at[a]))
        for j, (px, py) in enumerate(peers):
            sems = dict(send_sem=send_sems.at[3 * a + j], recv_sem=recv_sems.at[3 * a + j],
                        device_id=(px, py, c), device_id_type=MESH)
            out.append(pltpu.make_async_remote_copy(
                src_ref=src.at[2 * px + py] if scatter else src, dst_ref=dst.at[my_chip], **sems))
            inc.append(pltpu.make_async_remote_copy(
                src_ref=src.at[my_chip] if scatter else src, dst_ref=dst.at[2 * px + py], **sems))
    return local, out, inc


def _swap_copies(src_refs, dst_refs, send_sems, recv_sems, halves):
    x, y, c = _here()
    cps = []
    for k, (src, dst) in enumerate(zip(src_refs, dst_refs)):
        if halves:
            r2 = src.shape[1] // 2
            src = src.at[:, pl.ds((1 - c) * r2, r2), :]
        cps.append(pltpu.make_async_remote_copy(
            src_ref=src, dst_ref=dst, send_sem=send_sems.at[k], recv_sem=recv_sems.at[k],
            device_id=(x, y, 1 - c), device_id_type=MESH))
    return cps


class Carry:
    def __init__(self, kind, srcs):
        self.kind, self.srcs, n = kind, list(srcs), len(srcs)
        if kind == "gather":
            shapes = [(NCHIP,) + s.shape for s in srcs]
        elif kind == "swap_halves":
            shapes = [(s.shape[0], s.shape[1] // 2, s.shape[2]) for s in srcs]
        else:
            shapes = [s.shape for s in srcs]
        self.out_shape = [jax.ShapeDtypeStruct(sh, s.dtype) for sh, s in zip(shapes, srcs)]
        dma = pltpu.SemaphoreType.DMA
        self.sems = [dma((3 * n,)), dma((3 * n,)), dma((n,))] if kind in ("gather", "scatter") else [dma((n,)), dma((n,))]

    def start(self, srcs, dsts, sems):
        if self.kind in ("gather", "scatter"):
            local, out, _ = _ici_copies(srcs, dsts, *sems, self.kind == "scatter")
            for cp in local + out:
                cp.start()
        else:
            for cp in _swap_copies(srcs, dsts, *sems, self.kind == "swap_halves"):
                cp.start()

    def wait(self, srcs, dsts, sems):
        if self.kind in ("gather", "scatter"):
            local, out, inc = _ici_copies(srcs, dsts, *sems, self.kind == "scatter")
            for cp in inc:
                cp.wait_recv()
            for cp in out:
                cp.wait_send()
            for cp in local:
                cp.wait()
        else:
            cps = _swap_copies(srcs, dsts, *sems, self.kind == "swap_halves")
            for cp in cps:
                cp.wait_recv()
            for cp in cps:
                cp.wait_send()


def gather_split(srcs, name):
    n = len(srcs)

    def body(*refs):
        src_refs, dst_refs = refs[:n], refs[n:2 * n]
        send_sems, recv_sems, fwd_send, fwd_recv, local_sems = refs[2 * n:]
        x, y, c = _here()
        my_chip = 2 * x + y
        peers = [(1 - x, y), (x, 1 - y), (1 - x, 1 - y)]

        def half(ref, h):
            r2 = ref.shape[0] // 2
            return ref.at[pl.ds(h * r2, r2), :]

        local, out, landed, passed, arriving = [], [], [], [], []
        for a, (src, dst) in enumerate(zip(src_refs, dst_refs)):
            local.append(pltpu.make_async_copy(src, dst.at[my_chip], local_sems.at[a]))
            for j, (px, py) in enumerate(peers):
                k = 3 * a + j
                theirs = dst.at[2 * px + py]
                ici = dict(send_sem=send_sems.at[k], recv_sem=recv_sems.at[k], device_id=(px, py, c), device_id_type=MESH)
                d2d = dict(send_sem=fwd_send.at[k], recv_sem=fwd_recv.at[k], device_id=(x, y, 1 - c), device_id_type=MESH)
                out.append(pltpu.make_async_remote_copy(src_ref=half(src, c), dst_ref=half(dst.at[my_chip], c), **ici))
                landed.append(pltpu.make_async_remote_copy(src_ref=half(src, c), dst_ref=half(theirs, c), **ici))
                passed.append(pltpu.make_async_remote_copy(src_ref=half(theirs, c), dst_ref=half(theirs, c), **d2d))
                arriving.append(pltpu.make_async_remote_copy(src_ref=half(theirs, c), dst_ref=half(theirs, 1 - c), **d2d))
        for cp in local + out:
            cp.start()
        for got, fwd in zip(landed, passed):
            got.wait_recv()
            fwd.start()
        for cp in arriving:
            cp.wait_recv()
        for cp in out + passed:
            cp.wait_send()
        for cp in local:
            cp.wait()

    dma = pltpu.SemaphoreType.DMA
    return pl.pallas_call(
        body, name=name, out_shape=[jax.ShapeDtypeStruct((NCHIP,) + s.shape, s.dtype) for s in srcs],
        in_specs=[ANY] * n, out_specs=[ANY] * n,
        scratch_shapes=[dma((3 * n,)), dma((3 * n,)), dma((3 * n,)), dma((3 * n,)), dma((n,))],
    )(*srcs)


def _pcall(body, name, grid, in_specs, out_specs, out_shape, sem, args, carry=None, scratch=()):
    if carry is None:
        outs = pl.pallas_call(body, name=name, grid=grid, in_specs=in_specs, out_specs=out_specs,
                              out_shape=out_shape, scratch_shapes=list(scratch), compiler_params=_params(sem))(*args)
        return outs, []
    n_in, n_out, nc, ns = len(in_specs), len(out_specs), len(carry.srcs), len(scratch)

    def wrapped(*refs):
        ins, csrc = refs[:n_in], refs[n_in:n_in + nc]
        outs, cdst = refs[n_in + nc:n_in + nc + n_out], refs[n_in + nc + n_out:n_in + 2 * nc + n_out]
        own = refs[n_in + 2 * nc + n_out:n_in + 2 * nc + n_out + ns]
        sems = refs[n_in + 2 * nc + n_out + ns:]
        ids = [pl.program_id(a) for a in range(len(grid))]
        first = functools.reduce(jnp.logical_and, [i == 0 for i in ids])
        last = functools.reduce(jnp.logical_and, [i == g - 1 for i, g in zip(ids, grid)])

        @pl.when(first)
        def _():
            carry.start(csrc, cdst, sems)

        body(*ins, *outs, *own)

        @pl.when(last)
        def _():
            carry.wait(csrc, cdst, sems)

    res = pl.pallas_call(
        wrapped, name=name, grid=grid,
        in_specs=list(in_specs) + [ANY] * nc, out_specs=list(out_specs) + [ANY] * nc,
        out_shape=list(out_shape) + carry.out_shape,
        scratch_shapes=list(scratch) + carry.sems, compiler_params=_params(sem),
    )(*args, *carry.srcs)
    return res[:n_out], res[n_out:]


def ffn_fwd(x, vec, w1p, w2, gs, name, carry=None):
    S = x.shape[0]
    tm = _row_tile(S, 512)

    def body(x_ref, vec_ref, w1_ref, w2_ref, xn_ref, a_ref, f_ref):
        xt = x_ref[...]
        xhat, _, gain, _, _, sh, gt = _ada(xt, vec_ref)
        h = (xhat * gain + sh).astype(MXU_DTYPE)
        f = jnp.zeros((tm, D), F32)
        for hf in range(2):
            g = jnp.dot(h, w1_ref[hf], preferred_element_type=F32)
            up = jnp.dot(h, w1_ref[2 + hf], preferred_element_type=F32)
            a_ref[:, hf * HALF:(hf + 1) * HALF] = g.astype(a_ref.dtype)
            a_ref[:, DFF + hf * HALF:DFF + (hf + 1) * HALF] = up.astype(a_ref.dtype)
            act = (g * jax.nn.sigmoid(g) * up).astype(MXU_DTYPE)
            f = f + jnp.dot(act, w2_ref[hf * HALF:(hf + 1) * HALF, :], preferred_element_type=F32)
        xn_ref[...] = xt + (gs * gt) * f
        f_ref[...] = f.astype(f_ref.dtype)

    return _pcall(
        body, name, (S // tm,),
        [pl.BlockSpec((tm, D), lambda i: (i, 0)),
         pl.BlockSpec((SUBLANES, D), lambda i: (0, 0)),
         pl.BlockSpec((NCHIP, D, HALF), lambda i: (0, 0, 0), pipeline_mode=pl.Buffered(1)),
         pl.BlockSpec((DFF, D), lambda i: (0, 0), pipeline_mode=pl.Buffered(1))],
        [pl.BlockSpec((tm, D), lambda i: (i, 0)),
         pl.BlockSpec((tm, 2 * DFF), lambda i: (i, 0)),
         pl.BlockSpec((tm, D), lambda i: (i, 0))],
        [jax.ShapeDtypeStruct((S, D), F32),
         jax.ShapeDtypeStruct((S, 2 * DFF), ACT_DTYPE),
         jax.ShapeDtypeStruct((S, D), ACT_DTYPE)],
        ("arbitrary",), (x, vec, w1p, w2), carry)


def ffn_bwd(dxo, x, a, f, vec, w1p, w2, gs, name, carry=None):
    S = x.shape[0]
    tm = _row_tile(S, 256)

    def body(dxo_ref, x_ref, a_ref, f_ref, vec_ref, w1_ref, w2_ref,
             dxi_ref, hb_ref, dfb_ref, act_ref, da_ref, sums_ref):
        xt = x_ref[...]
        dxo = dxo_ref[...]
        xhat, r, gain, ng, sc, sh, gt = _ada(xt, vec_ref)
        hb_ref[...] = (xhat * gain + sh).astype(hb_ref.dtype)
        dgate = gs * jnp.sum(dxo * f_ref[...].astype(F32), axis=0, keepdims=True)
        df = ((gs * gt) * dxo).astype(MXU_DTYPE)
        dfb_ref[...] = df
        dh = jnp.zeros((tm, D), F32)
        for hf in range(2):
            lo, hi = hf * HALF, (hf + 1) * HALF
            dact = lax.dot_general(df, w2_ref[lo:hi, :], NT_DIMS, preferred_element_type=F32)
            g = a_ref[:, lo:hi].astype(F32)
            up = a_ref[:, DFF + lo:DFF + hi].astype(F32)
            sg = jax.nn.sigmoid(g)
            si = g * sg
            act_ref[:, lo:hi] = (si * up).astype(act_ref.dtype)
            dg = (dact * up * (sg * (1.0 + g * (1.0 - sg)))).astype(MXU_DTYPE)
            dup = (dact * si).astype(MXU_DTYPE)
            da_ref[:, lo:hi] = dg
            da_ref[:, DFF + lo:DFF + hi] = dup
            dh = dh + lax.dot_general(dg, w1_ref[hf], NT_DIMS, preferred_element_type=F32)
            dh = dh + lax.dot_general(dup, w1_ref[2 + hf], NT_DIMS, preferred_element_type=F32)
        dx, dshift, dscale, dng = _ada_bwd(dh, xhat, r, gain, ng, sc)
        dxi_ref[...] = dxo + dx
        _acc_rows(sums_ref, pl.program_id(0) == 0, (dshift, dscale, dng, dgate))

    return _pcall(
        body, name, (S // tm,),
        [pl.BlockSpec((tm, D), lambda i: (i, 0)),
         pl.BlockSpec((tm, D), lambda i: (i, 0)),
         pl.BlockSpec((tm, 2 * DFF), lambda i: (i, 0)),
         pl.BlockSpec((tm, D), lambda i: (i, 0)),
         pl.BlockSpec((SUBLANES, D), lambda i: (0, 0)),
         pl.BlockSpec((NCHIP, D, HALF), lambda i: (0, 0, 0), pipeline_mode=pl.Buffered(1)),
         pl.BlockSpec((DFF, D), lambda i: (0, 0), pipeline_mode=pl.Buffered(1))],
        [pl.BlockSpec((tm, D), lambda i: (i, 0)),
         pl.BlockSpec((tm, D), lambda i: (i, 0)),
         pl.BlockSpec((tm, D), lambda i: (i, 0)),
         pl.BlockSpec((tm, DFF), lambda i: (i, 0)),
         pl.BlockSpec((tm, 2 * DFF), lambda i: (i, 0)),
         pl.BlockSpec((SUBLANES, D), lambda i: (0, 0))],
        [jax.ShapeDtypeStruct((S, D), F32),
         jax.ShapeDtypeStruct((S, D), MXU_DTYPE),
         jax.ShapeDtypeStruct((S, D), MXU_DTYPE),
         jax.ShapeDtypeStruct((S, DFF), MXU_DTYPE),
         jax.ShapeDtypeStruct((S, 2 * DFF), MXU_DTYPE),
         jax.ShapeDtypeStruct((SUBLANES, D), F32)],
        ("arbitrary",), (dxo, x, a, f, vec, w1p, w2), carry)


def wgrad(a, b, kt, nt, name, carry=None):
    T, K = a.shape
    N = b.shape[1]
    pk, pn = K // kt, N // nt
    assert pk == 1 or pn == 1
    tt = _row_tile(T, 2048)
    steps = T // tt

    def body(a_ref, b_ref, o_ref):
        @pl.when(pl.program_id(1) == 0)
        def _():
            o_ref[...] = jnp.zeros_like(o_ref)
        o_ref[...] += lax.dot_general(a_ref[...], b_ref[...], TN_DIMS, preferred_element_type=F32)

    a_map = (lambda p, t: (t, p)) if pk > 1 else (lambda p, t: (t, 0))
    b_map = (lambda p, t: (t, p)) if pn > 1 else (lambda p, t: (t, 0))
    (out,), got = _pcall(
        body, name, (pk * pn, steps),
        [pl.BlockSpec((tt, kt), a_map), pl.BlockSpec((tt, nt), b_map)],
        [pl.BlockSpec((None, kt, nt), lambda p, t: (p, 0, 0))],
        [jax.ShapeDtypeStruct((pk * pn, kt, nt), F32)], ("arbitrary", "arbitrary"), (a, b), carry)
    return out, got


def _head_masks(rows):
    lane = lax.broadcasted_iota(jnp.int32, (rows, LANES), 1)
    return lane < HD


def _pair_stat(x, m_a):
    s_a = jnp.sum(jnp.where(m_a, x, 0.0), axis=1, keepdims=True)
    s_b = jnp.sum(jnp.where(m_a, 0.0, x), axis=1, keepdims=True)
    return s_a, s_b


def mixer_in(x, vec, winp, gvec, name):
    S = x.shape[0]
    tm = _row_tile(S, 512)
    pc = INC // NCHIP

    def body(x_ref, vec_ref, w_ref, g_ref, proj_ref, hb_ref, qn_ref, kn_ref, v_ref):
        xt = x_ref[...]
        xhat, _, gain, _, _, sh, _ = _ada(xt, vec_ref)
        h = (xhat * gain + sh).astype(MXU_DTYPE)
        hb_ref[...] = h
        for j in range(NCHIP):
            proj_ref[:, j * pc:(j + 1) * pc] = jnp.dot(h, w_ref[j], preferred_element_type=F32)
        m_a = _head_masks(tm)
        for which, dst in ((0, qn_ref), (1, kn_ref)):
            for p in range(AW // LANES):
                lo = which * AW + p * LANES
                xp = proj_ref[:, lo:lo + LANES]
                s_a, s_b = _pair_stat(xp * xp, m_a)
                rr = jnp.where(m_a, lax.rsqrt(s_a * (1.0 / HD) + EPS), lax.rsqrt(s_b * (1.0 / HD) + EPS))
                gp = g_ref[which:which + 1, p * LANES:(p + 1) * LANES]
                dst[:, p * LANES:(p + 1) * LANES] = (xp * rr * gp).astype(dst.dtype)
        v_ref[...] = proj_ref[:, 2 * AW:3 * AW].astype(v_ref.dtype)

    return pl.pallas_call(
        body, name=name, grid=(S // tm,),
        in_specs=[pl.BlockSpec((tm, D), lambda i: (i, 0)),
                  pl.BlockSpec((SUBLANES, D), lambda i: (0, 0)),
                  pl.BlockSpec((NCHIP, D, pc), lambda i: (0, 0, 0), pipeline_mode=pl.Buffered(1)),
                  pl.BlockSpec((SUBLANES, AW), lambda i: (0, 0))],
        out_specs=[pl.BlockSpec((tm, INC), lambda i: (i, 0)),
                   pl.BlockSpec((tm, D), lambda i: (i, 0)),
                   pl.BlockSpec((tm, AW), lambda i: (i, 0)),
                   pl.BlockSpec((tm, AW), lambda i: (i, 0)),
                   pl.BlockSpec((tm, AW), lambda i: (i, 0))],
        out_shape=[jax.ShapeDtypeStruct((S, INC), F32),
                   jax.ShapeDtypeStruct((S, D), MXU_DTYPE),
                   jax.ShapeDtypeStruct((S, AW), F32),
                   jax.ShapeDtypeStruct((S, AW), F32),
                   jax.ShapeDtypeStruct((S, AW), F32)],
        compiler_params=_params(("arbitrary",)),
    )(x, vec, winp, gvec)


def _band_masks(ncol):
    row = lax.broadcasted_iota(jnp.int32, (2 * QBLK, ncol), 0) & (QBLK - 1)
    col = lax.broadcasted_iota(jnp.int32, (2 * QBLK, ncol), 1)
    return row, col


def _stack_heads(t, m_a):
    zero = jnp.zeros_like(t)
    return jnp.concatenate([jnp.where(m_a, t, zero), jnp.where(m_a, zero, t)], axis=0)


class _AttnLayout:
    def __init__(self, d, S):
        self.d, self.S = d, S
        self.qb = max(1, min(ATTN_QBLOCKS, ATTN_CHUNK_ROWS // (QBLK * d)))
        self.nres = d
        self.nchunk = S // (self.qb * QBLK * d)
        self.grid = (AW // LANES, self.nchunk)
        self.unroll = max(1, min(d, ATTN_QBLOCKS // self.qb))

    def _spec(self, blocks, row_of):
        return pl.BlockSpec((blocks * QBLK * self.d, LANES), lambda hp, j: (row_of(j), hp))

    def cur(self, chunk_of):
        return self._spec(self.qb, chunk_of)

    def prev(self, chunk_of):
        return self._spec(1, lambda j: jnp.maximum(chunk_of(j) * self.qb - 1, 0))

    def idx(self, b, r):
        if self.d == 1:
            return (pl.ds(b * QBLK, QBLK), slice(None))
        return (pl.ds(b * QBLK * self.d + r, QBLK, stride=self.d), slice(None))

    def per_residue(self, fn):
        if self.nres == 1:
            fn(0)
        else:
            def step(it, carry):
                for k in range(self.unroll):
                    fn(it * self.unroll + k)
                return carry
            lax.fori_loop(0, self.nres // self.unroll, step, 0)


def attn_fwd(qn, kn, v, d, name, carry=None):
    S = qn.shape[0]
    lay = _AttnLayout(d, S)
    qb = lay.qb

    def body(q_ref, kc_ref, kp_ref, vc_ref, vp_ref, o_ref, lse_ref):
        i = pl.program_id(1)
        m_a = _head_masks(QBLK)
        row, col = _band_masks(2 * QBLK)
        dist = row + QBLK - col
        band = (dist >= 0) & (dist <= QBLK)
        first = band & ((i > 0) | (col >= QBLK))

        def residue(r):
            kt = [kp_ref[lay.idx(0, r)].astype(MXU_DTYPE)]
            vt = [vp_ref[lay.idx(0, r)].astype(MXU_DTYPE)]
            for b in range(qb):
                kt.append(kc_ref[lay.idx(b, r)].astype(MXU_DTYPE))
                vt.append(vc_ref[lay.idx(b, r)].astype(MXU_DTYPE))
            for b in range(qb):
                rows = lay.idx(b, r)
                q = (q_ref[rows] * (HD ** -0.5)).astype(MXU_DTYPE)
                kcat = jnp.concatenate([kt[b], kt[b + 1]], axis=0)
                vcat = jnp.concatenate([vt[b], vt[b + 1]], axis=0)
                mask = first if b == 0 else band
                s = lax.dot_general(_stack_heads(q, m_a), kcat, NT_DIMS, preferred_element_type=F32)
                s = jnp.where(mask, s, NEG)
                m = jnp.max(s, axis=1, keepdims=True)
                p = jnp.exp(s - m)
                l = jnp.sum(p, axis=1, keepdims=True)
                o = jnp.dot(p.astype(MXU_DTYPE), vcat, preferred_element_type=F32) / l
                lse = jnp.broadcast_to(m + jnp.log(l), (2 * QBLK, LANES))
                o_ref[rows] = jnp.where(m_a, o[:QBLK], o[QBLK:])
                lse_ref[rows] = jnp.where(m_a, lse[:QBLK], lse[QBLK:])

        lay.per_residue(residue)

    cur, prev = lay.cur(lambda j: j), lay.prev(lambda j: j)
    return _pcall(body, name, lay.grid, [cur, cur, prev, cur, prev], [cur, cur],
                  [jax.ShapeDtypeStruct((S, AW), F32)] * 2, ("arbitrary", "arbitrary"), (qn, kn, kn, v, v), carry)


def _both_heads(t, m_a):
    other = pltpu.roll(t, HD, 1)
    return jnp.concatenate([jnp.where(m_a, t, other), jnp.where(m_a, other, t)], axis=0)


def attn_bwd(qn, kn, v, dycat, lse, delta, d, name, carry=None):
    S = qn.shape[0]
    lay = _AttnLayout(d, S)
    qb, nchunk = lay.qb, lay.nchunk

    def body(q_ref, kc_ref, kp_ref, vc_ref, vp_ref, do_ref, lse_ref, dl_ref,
             dq_ref, dk_ref, dv_ref, ck_ref, cv_ref):
        j = pl.program_id(1)
        i = nchunk - 1 - j
        m_a = _head_masks(QBLK)
        row, col = _band_masks(2 * QBLK)
        dist = row + QBLK - col
        band = (dist >= 0) & (dist <= QBLK)
        first = band & ((i > 0) | (col >= QBLK))

        def residue(r):
            def tiles(ref, cast):
                out = [ref[lay.idx(b, r)] for b in range(qb)]
                return [t.astype(MXU_DTYPE) for t in out] if cast else out

            def ktiles(cur_ref, prev_ref):
                return [prev_ref[lay.idx(0, r)].astype(MXU_DTYPE)] + tiles(cur_ref, True)

            qt = [(t * (HD ** -0.5)).astype(MXU_DTYPE) for t in tiles(q_ref, False)]
            dot_ = tiles(do_ref, True)
            lse_t = tiles(lse_ref, False)
            dl_t = tiles(dl_ref, False)
            kt = ktiles(kc_ref, kp_ref)
            vt = ktiles(vc_ref, vp_ref)
            dk_acc = [jnp.zeros((QBLK, LANES), F32) for _ in range(qb)]
            dv_acc = [jnp.zeros((QBLK, LANES), F32) for _ in range(qb)]
            crow = pl.ds(0, QBLK) if lay.nres == 1 else pl.ds(pl.multiple_of(r * QBLK, QBLK), QBLK)
            dk_acc[qb - 1] = jnp.where(j > 0, ck_ref[crow, :], 0.0)
            dv_acc[qb - 1] = jnp.where(j > 0, cv_ref[crow, :], 0.0)
            for x in range(qb):
                kcat = jnp.concatenate([kt[x], kt[x + 1]], axis=0)
                vcat = jnp.concatenate([vt[x], vt[x + 1]], axis=0)
                q2 = _stack_heads(qt[x], m_a)
                do2 = _stack_heads(dot_[x], m_a)
                lse2 = _both_heads(lse_t[x], m_a)
                dl2 = _both_heads(dl_t[x], m_a)
                lse2 = jnp.concatenate([lse2, lse2], axis=1)
                dl2 = jnp.concatenate([dl2, dl2], axis=1)
                s = lax.dot_general(q2, kcat, NT_DIMS, preferred_element_type=F32)
                p = jnp.exp(jnp.where(first if x == 0 else band, s, NEG) - lse2)
                dp = lax.dot_general(do2, vcat, NT_DIMS, preferred_element_type=F32)
                ds = p * (dp - dl2)
                dq = jnp.dot(ds.astype(MXU_DTYPE), kcat, preferred_element_type=F32)
                dq_ref[lay.idx(x, r)] = jnp.where(m_a, dq[:QBLK], dq[QBLK:]) * (HD ** -0.5)
                dk = jnp.dot(ds.T.astype(MXU_DTYPE), q2, preferred_element_type=F32)
                dv = jnp.dot(p.T.astype(MXU_DTYPE), do2, preferred_element_type=F32)
                if x == 0:
                    ck_ref[crow, :] = dk[:QBLK]
                    cv_ref[crow, :] = dv[:QBLK]
                else:
                    dk_acc[x - 1] = dk_acc[x - 1] + dk[:QBLK]
                    dv_acc[x - 1] = dv_acc[x - 1] + dv[:QBLK]
                dk_acc[x] = dk_acc[x] + dk[QBLK:]
                dv_acc[x] = dv_acc[x] + dv[QBLK:]
            for kb in range(qb):
                dk_ref[lay.idx(kb, r)] = dk_acc[kb]
                dv_ref[lay.idx(kb, r)] = dv_acc[kb]

        lay.per_residue(residue)

    cur, prev = lay.cur(lambda j: nchunk - 1 - j), lay.prev(lambda j: nchunk - 1 - j)
    carried = pltpu.VMEM((lay.nres * QBLK, LANES), F32)
    return _pcall(
        body, name, lay.grid, [cur, cur, prev, cur, prev, cur, cur, cur], [cur, cur, cur],
        [jax.ShapeDtypeStruct((S, AW), F32)] * 3, ("arbitrary", "arbitrary"),
        (qn, kn, kn, v, v, dycat, lse, delta), carry, scratch=[carried, carried])


def _shift_down(x, halo_prev, k, row):
    tm = x.shape[0]
    tail = jnp.concatenate([pltpu.roll(halo_prev, k, 0), jnp.zeros((tm - SUBLANES, x.shape[1]), x.dtype)], axis=0)
    return jnp.where(row < k, tail, pltpu.roll(x, k, 0))


def _shift_up(x, halo_next, k, row):
    tm = x.shape[0]
    head = jnp.concatenate([jnp.zeros((tm - SUBLANES, x.shape[1]), x.dtype), pltpu.roll(halo_next, SUBLANES - k, 0)], axis=0)
    return jnp.where(row >= tm - k, head, pltpu.roll(x, tm - k, 0))


def _conv_fwd(cu, halo_cu, cw_ref, row):
    u1 = _shift_down(cu, halo_cu, 1, row)
    u2 = _shift_down(cu, halo_cu, 2, row)
    cv = cw_ref[0:1, :] * u2 + cw_ref[1:2, :] * u1 + cw_ref[2:3, :] * cu + cw_ref[3:4, :]
    return cv, u1, u2


def combine_conv(os_, lses, proj, cw, name, carry=None):
    S = proj.shape[0]
    tm = _row_tile(S, 512)
    hb = tm // SUBLANES

    def body(o1, o2, o3, l1, l2, l3, pc_ref, ph_ref, cw_ref, ycat_ref, lse_ref):
        i = pl.program_id(0)
        for p in range(AW // LANES):
            cs = slice(p * LANES, (p + 1) * LANES)
            ls = [l[:, cs] for l in (l1, l2, l3)]
            mx = jnp.maximum(jnp.maximum(ls[0], ls[1]), ls[2])
            t = mx + jnp.log(jnp.exp(ls[0] - mx) + jnp.exp(ls[1] - mx) + jnp.exp(ls[2] - mx))
            lse_ref[:, cs] = t
            acc = jnp.zeros((tm, LANES), F32)
            for l, o in zip(ls, (o1, o2, o3)):
                acc = acc + jnp.exp(l - t) * o[:, cs]
            ycat_ref[:, cs] = acc.astype(ycat_ref.dtype)
        row = lax.broadcasted_iota(jnp.int32, (tm, CW), 0)
        gb, gc, u = pc_ref[:, 0:CW], pc_ref[:, CW:2 * CW], pc_ref[:, 2 * CW:3 * CW]
        halo_cu = jnp.where(i > 0, ph_ref[:, CW:2 * CW] * ph_ref[:, 2 * CW:3 * CW], 0.0)
        cv, _, _ = _conv_fwd(gc * u, halo_cu, cw_ref, row)
        ycat_ref[:, AW:AW + CW] = (gb * cv).astype(ycat_ref.dtype)

    ot = pl.BlockSpec((tm, AW), lambda i: (i, 0))
    return _pcall(
        body, name, (S // tm,),
        [ot] * 6 + [pl.BlockSpec((tm, 3 * CW), lambda i: (i, 1)),
                    pl.BlockSpec((SUBLANES, 3 * CW), lambda i: (jnp.maximum(i * hb - 1, 0), 1)),
                    pl.BlockSpec((SUBLANES, CW), lambda i: (0, 0))],
        [pl.BlockSpec((tm, D), lambda i: (i, 0)), ot],
        [jax.ShapeDtypeStruct((S, D), ACT_DTYPE), jax.ShapeDtypeStruct((S, AW), F32)],
        ("arbitrary",), (*os_, *lses, proj, proj, cw), carry)


def out_proj(ycat, x, vec, wout, name):
    S = x.shape[0]
    tm = _row_tile(S, 512)

    def body(yc_ref, x_ref, vec_ref, w_ref, xn_ref, y_ref):
        y = jnp.dot(yc_ref[...].astype(MXU_DTYPE), w_ref[...], preferred_element_type=F32)
        xn_ref[...] = x_ref[...] + vec_ref[3:4, :] * y
        y_ref[...] = y.astype(y_ref.dtype)

    t = pl.BlockSpec((tm, D), lambda i: (i, 0))
    return pl.pallas_call(
        body, name=name, grid=(S // tm,),
        in_specs=[t, t, pl.BlockSpec((SUBLANES, D), lambda i: (0, 0)),
                  pl.BlockSpec((D, D), lambda i: (0, 0))],
        out_specs=[t, t],
        out_shape=[jax.ShapeDtypeStruct((S, D), F32), jax.ShapeDtypeStruct((S, D), ACT_DTYPE)],
        compiler_params=_params(("arbitrary",)),
    )(ycat, x, vec, wout)


def out_proj_bwd(dxo, y, ycat, vec, wout, name, carry=None):
    S = dxo.shape[0]
    tm = _row_tile(S, 512)

    def body(dxo_ref, y_ref, yc_ref, vec_ref, w_ref, dyb_ref, dyc_ref, dl_ref, sums_ref):
        dxo = dxo_ref[...]
        dgate = jnp.sum(dxo * y_ref[...].astype(F32), axis=0, keepdims=True)
        dy = (vec_ref[3:4, :] * dxo).astype(MXU_DTYPE)
        dyb_ref[...] = dy
        dyc_ref[...] = lax.dot_general(dy, w_ref[...], NT_DIMS, preferred_element_type=F32)
        m_a = _head_masks(tm)
        for p in range(AW // LANES):
            cs = slice(p * LANES, (p + 1) * LANES)
            s_a, s_b = _pair_stat(dyc_ref[:, cs] * yc_ref[:, cs].astype(F32), m_a)
            dl_ref[:, cs] = jnp.where(m_a, s_a, s_b)
        _acc_rows(sums_ref, pl.program_id(0) == 0, (dgate,))

    t = pl.BlockSpec((tm, D), lambda i: (i, 0))
    at = pl.BlockSpec((tm, AW), lambda i: (i, 0))
    return _pcall(
        body, name, (S // tm,),
        [t, t, t, pl.BlockSpec((SUBLANES, D), lambda i: (0, 0)), pl.BlockSpec((D, D), lambda i: (0, 0))],
        [t, t, at, pl.BlockSpec((SUBLANES, D), lambda i: (0, 0))],
        [jax.ShapeDtypeStruct((S, D), MXU_DTYPE), jax.ShapeDtypeStruct((S, D), F32),
         jax.ShapeDtypeStruct((S, AW), F32), jax.ShapeDtypeStruct((SUBLANES, D), F32)],
        ("arbitrary",), (dxo, y, ycat, vec, wout), carry)


def mixer_mid_bwd(dqs, dks, dvs, proj, dycat, gvec, cw, name, carry=None):
    S = proj.shape[0]
    tm = _row_tile(S, 256)
    hb = tm // SUBLANES
    nsl = S // SUBLANES
    ntile = S // tm

    def body(dq1, dq2, dq3, dk1, dk2, dk3, dv1, dv2, dv3, pr_ref, pp_ref, pn_ref, dyc_ref, dyn_ref,
             g_ref, cw_ref, dp_ref, sums_ref):
        i = pl.program_id(0)
        m_a = _head_masks(tm)
        gsum = []
        for which, parts in ((0, (dq1, dq2, dq3)), (1, (dk1, dk2, dk3))):
            acc_g = []
            for p in range(AW // LANES):
                lo = which * AW + p * LANES
                cs = slice(p * LANES, (p + 1) * LANES)
                xp = pr_ref[:, lo:lo + LANES]
                s_a, s_b = _pair_stat(xp * xp, m_a)
                rr = jnp.where(m_a, lax.rsqrt(s_a * (1.0 / HD) + EPS), lax.rsqrt(s_b * (1.0 / HD) + EPS))
                xh = xp * rr
                dn = parts[0][:, cs] + parts[1][:, cs] + parts[2][:, cs]
                acc_g.append(jnp.sum(dn * xh, axis=0, keepdims=True))
                t = dn * g_ref[which:which + 1, cs]
                t_a, t_b = _pair_stat(t * xh, m_a)
                mean = jnp.where(m_a, t_a, t_b) * (1.0 / HD)
                dp_ref[:, lo:lo + LANES] = (rr * (t - xh * mean)).astype(dp_ref.dtype)
            gsum.append(jnp.concatenate(acc_g, axis=1))
        dp_ref[:, 2 * AW:3 * AW] = (dv1[...] + dv2[...] + dv3[...]).astype(dp_ref.dtype)
        row = lax.broadcasted_iota(jnp.int32, (tm, CW), 0)
        base = 3 * AW
        gb, gc, u = pr_ref[:, base:base + CW], pr_ref[:, base + CW:base + 2 * CW], pr_ref[:, base + 2 * CW:base + 3 * CW]
        cu = gc * u
        halo_cu = jnp.where(i > 0, pp_ref[:, CW:2 * CW] * pp_ref[:, 2 * CW:3 * CW], 0.0)
        cv, u1, u2 = _conv_fwd(cu, halo_cu, cw_ref, row)
        dyc = dyc_ref[...]
        dp_ref[:, base:base + CW] = (dyc * cv).astype(dp_ref.dtype)
        dcv = dyc * gb
        halo_dcv = jnp.where(i < ntile - 1, dyn_ref[...] * pn_ref[:, 0:CW], 0.0)
        d1 = _shift_up(dcv, halo_dcv, 1, row)
        d2 = _shift_up(dcv, halo_dcv, 2, row)
        dcu = cw_ref[2:3, :] * dcv + cw_ref[1:2, :] * d1 + cw_ref[0:1, :] * d2
        dp_ref[:, base + CW:base + 2 * CW] = (dcu * u).astype(dp_ref.dtype)
        dp_ref[:, base + 2 * CW:base + 3 * CW] = (dcu * gc).astype(dp_ref.dtype)
        rows = (gsum[0], gsum[1],
                jnp.sum(dcv * u2, axis=0, keepdims=True), jnp.sum(dcv * u1, axis=0, keepdims=True),
                jnp.sum(dcv * cu, axis=0, keepdims=True), jnp.sum(dcv, axis=0, keepdims=True))
        _acc_rows(sums_ref, i == 0, rows)

    at = pl.BlockSpec((tm, AW), lambda i: (i, 0))
    return _pcall(
        body, name, (ntile,),
        [at] * 9 + [
            pl.BlockSpec((tm, INC), lambda i: (i, 0)),
            pl.BlockSpec((SUBLANES, 3 * CW), lambda i: (jnp.maximum(i * hb - 1, 0), 1)),
            pl.BlockSpec((SUBLANES, 3 * CW), lambda i: (jnp.minimum((i + 1) * hb, nsl - 1), 1)),
            pl.BlockSpec((tm, CW), lambda i: (i, 1)),
            pl.BlockSpec((SUBLANES, CW), lambda i: (jnp.minimum((i + 1) * hb, nsl - 1), 1)),
            pl.BlockSpec((SUBLANES, AW), lambda i: (0, 0)),
            pl.BlockSpec((SUBLANES, CW), lambda i: (0, 0))],
        [pl.BlockSpec((tm, INC), lambda i: (i, 0)), pl.BlockSpec((SUBLANES, AW), lambda i: (0, 0))],
        [jax.ShapeDtypeStruct((S, INC), MXU_DTYPE), jax.ShapeDtypeStruct((SUBLANES, AW), F32)],
        ("arbitrary",), (*dqs, *dks, *dvs, proj, proj, proj, dycat, dycat, gvec, cw), carry)


def mixer_in_bwd(dxo, x, dproj, vec, winp, name, carry=None):
    S = x.shape[0]
    tm = _row_tile(S, 512)
    pc = INC // NCHIP

    def body(dxo_ref, x_ref, dp_ref, vec_ref, w_ref, dxi_ref, sums_ref):
        xhat, r, gain, ng, sc, _, _ = _ada(x_ref[...], vec_ref)
        dh = jnp.zeros((tm, D), F32)
        for j in range(NCHIP):
            dh = dh + lax.dot_general(dp_ref[:, j * pc:(j + 1) * pc], w_ref[j], NT_DIMS, preferred_element_type=F32)
        dx, dshift, dscale, dng = _ada_bwd(dh, xhat, r, gain, ng, sc)
        dxi_ref[...] = dxo_ref[...] + dx
        _acc_rows(sums_ref, pl.program_id(0) == 0, (dshift, dscale, dng))

    t = pl.BlockSpec((tm, D), lambda i: (i, 0))
    return _pcall(
        body, name, (S // tm,),
        [t, t, pl.BlockSpec((tm, INC), lambda i: (i, 0)),
         pl.BlockSpec((SUBLANES, D), lambda i: (0, 0)),
         pl.BlockSpec((NCHIP, D, pc), lambda i: (0, 0, 0), pipeline_mode=pl.Buffered(1))],
        [t, pl.BlockSpec((SUBLANES, D), lambda i: (0, 0))],
        [jax.ShapeDtypeStruct((S, D), F32), jax.ShapeDtypeStruct((SUBLANES, D), F32)],
        ("arbitrary",), (dxo, x, dproj, vec, winp), carry)


def loss_head(xf, target, name):
    S = xf.shape[0]
    tm = _row_tile(S, 1024)

    def body(x_ref, t_ref, dy_ref, l_ref):
        diff = x_ref[...] - t_ref[...]
        dy_ref[...] = diff * (1.0 / D)
        part = jnp.sum(jnp.sum(diff * diff, axis=0, keepdims=True), axis=1, keepdims=True) * (0.5 / D)

        @pl.when(pl.program_id(0) == 0)
        def _():
            l_ref[...] = jnp.zeros_like(l_ref)
        l_ref[...] += jnp.broadcast_to(part, l_ref.shape)

    t = pl.BlockSpec((tm, D), lambda i: (i, 0))
    return pl.pallas_call(
        body, name=name, grid=(S // tm,),
        in_specs=[t, t],
        out_specs=[t, pl.BlockSpec((SUBLANES, LANES), lambda i: (0, 0))],
        out_shape=[jax.ShapeDtypeStruct((S, D), F32), jax.ShapeDtypeStruct((SUBLANES, LANES), F32)],
        compiler_params=_params(("arbitrary",)),
    )(xf, target)


def _vec(mod_l, ng_l, i):
    m = mod_l.reshape(3, 3, D)
    rows = jnp.stack([ng_l[i], m[i, 1], m[i, 0], m[i, 2]])
    return jnp.concatenate([rows, jnp.zeros((SUBLANES - 4, D), F32)], axis=0)


def local_step(x, target, mods, ngs, gvecs, cws, shards, w_first, cflag):
    saved = []
    weights = [dict(w1=[None, None], w2=[None, None]) for _ in range(2)]
    weights[0]["w1"][0], weights[0]["w2"][0] = w_first[0], w_first[1].reshape(DFF, D)
    h = x
    for l in range(2):
        w, sh = weights[l], shards[l]
        nxt = shards[l + 1] if l == 0 else None
        vecs = [_vec(mods[l], ngs[l], i) for i in range(3)]
        x0 = h
        (x1, a0, f0), (win, wout, w2b) = ffn_fwd(x0, vecs[0], w["w1"][0], w["w2"][0], 0.5, f"ffn_fwd_l{l}a",
                                                 carry=Carry("gather", [sh["win"], sh["wout"], sh["w2"][1]]))
        w["win"], w["wout"], w["w2"][1] = win, wout.reshape(D, D), w2b.reshape(DFF, D)
        proj, h1b, qn, kn, v = mixer_in(x1, vecs[1], w["win"], gvecs[l], f"mixer_in_l{l}")
        os_, lses, w1b = [], [], {}
        for d in DILATIONS:
            rows = {1: slice(0, D // 2), 16: slice(D // 2, D)}.get(d)
            carry = Carry("gather", [sh["w1"][1][rows]]) if rows else None
            (o, lse_d), w1b[d] = attn_fwd(qn, kn, v, d, f"attn_fwd_l{l}_d{d}", carry=carry)
            os_.append(o)
            lses.append(lse_d)
        w["w1"][1] = jnp.concatenate([w1b[1][0], w1b[16][0]], axis=1)
        (ycat, lse), got = combine_conv(os_, lses, proj, cws[l], f"combine_conv_l{l}",
                                        carry=Carry("gather", [nxt["w2"][0]]) if nxt else None)
        if nxt:
            weights[1]["w2"][0] = got[0].reshape(DFF, D)
        x2, y = out_proj(ycat, x1, vecs[1], w["wout"], f"out_proj_l{l}")
        (x3, a2, f2), got = ffn_fwd(x2, vecs[2], w["w1"][1], w["w2"][1], 0.5, f"ffn_fwd_l{l}b",
                                    carry=Carry("gather", [nxt["w1"][0]]) if nxt else None)
        if nxt:
            weights[1]["w1"][0] = got[0]
        saved.append(dict(vecs=vecs, x0=x0, a0=a0, f0=f0, x1=x1, proj=proj, h1b=h1b, qn=qn, kn=kn, v=v,
                          ycat=ycat, lse=lse, y=y, x2=x2, a2=a2, f2=f2))
        h = x3
    dx, loss_blk = loss_head(h, target, "loss_head")
    sums, totals, g_prev = [None, None], [None, None], None
    w2r = DFF // NCHIP
    for l in (1, 0):
        w, s = weights[l], saved[l]
        vecs = s["vecs"]
        ride = g_prev is not None
        own = l == 0
        mine, other = [None] * 6, [None] * 6

        def half_sum(group, recv, k0):
            return [add_half(g, r, cflag, f"add_sibling_l{l}_{k0 + j}") for j, (g, r) in enumerate(zip(group, recv))]

        def chip_sum(landed, k0):
            return [sum_chips(t, f"sum_chips_l{l}_{k0 + j}") for j, t in enumerate(landed)]

        (dx, hb, dfb, act, da, sums2), got = ffn_bwd(
            dx, s["x2"], s["a2"], s["f2"], vecs[2], w["w1"][1], w["w2"][1], 0.5, f"ffn_bwd_l{l}b",
            carry=Carry("swap_halves", g_prev) if ride else None)
        dw1b, _ = wgrad(hb, da, D, HALF, f"wgrad_w1_l{l}b")
        dw2b, _ = wgrad(act, dfb, HALF, D, f"wgrad_w2_l{l}b")
        if ride:
            wire = [add_half(g_prev[k], got[k], cflag, f"add_sibling_l{l + 1}_{k}") for k in range(6)]
        g_ffn_b = [dw1b, dw2b.reshape(NCHIP, w2r, D)]
        (dyb, dycat, delta, sums_o), got = out_proj_bwd(
            dx, s["y"], s["ycat"], vecs[1], w["wout"], f"out_proj_bwd_l{l}",
            carry=Carry("swap_halves", g_ffn_b) if own else None)
        dwout, _ = wgrad(s["ycat"].astype(MXU_DTYPE), dyb, D // 2, D, f"wgrad_wout_l{l}")
        if own:
            wire_ffn_b = half_sum(g_ffn_b, got, 4)
        dqs, dks, dvs, landed = [], [], [], {}
        for d in DILATIONS:
            carry = None
            if ride and d == 1:
                carry = Carry("scatter", wire[3:])
            if ride and d == 16:
                carry = Carry("scatter", wire[:3])
            if own and d == 4:
                carry = Carry("scatter", wire_ffn_b)
            (dq, dk, dv), landed[d] = attn_bwd(s["qn"], s["kn"], s["v"], dycat, s["lse"], delta, d,
                                               f"attn_bwd_l{l}_d{d}", carry=carry)
            dqs.append(dq)
            dks.append(dk)
            dvs.append(dv)
        if ride:
            tot = [sum_chips(t, f"sum_chips_l{l + 1}_{k}") for k, t in enumerate(list(landed[16]) + list(landed[1]))]
        if own:
            mine[4:6] = chip_sum(landed[4], 4)
        ready = (tot if ride else []) + (mine[4:6] if own else [])
        (dproj, sums_m), got = mixer_mid_bwd(dqs, dks, dvs, s["proj"], dycat, gvecs[l], cws[l], f"mixer_mid_bwd_l{l}",
                                             carry=Carry("swap", ready) if ready else None)
        if ride:
            totals[l + 1] = (tot, list(got[:6]))
        if own:
            other[4:6] = list(got[-2:])
        dwin, _ = wgrad(s["h1b"], dproj, D, INC // NCHIP, f"wgrad_win_l{l}")
        g_mixer = [dwin, dwout.reshape(NCHIP, D // NCHIP, D)]
        (dx, sums1), got = mixer_in_bwd(dx, s["x1"], dproj, vecs[1], w["win"], f"mixer_in_bwd_l{l}",
                                        carry=Carry("swap_halves", g_mixer) if own else None)
        if own:
            wire_mixer = half_sum(g_mixer, got, 2)
        (dx, hb, dfb, act, da, sums0), _ = ffn_bwd(
            dx, s["x0"], s["a0"], s["f0"], vecs[0], w["w1"][0], w["w2"][0], 0.5, f"ffn_bwd_l{l}a")
        dw1a, got = wgrad(hb, da, D, HALF, f"wgrad_w1_l{l}a", carry=Carry("scatter", wire_mixer) if own else None)
        if own:
            mine[2:4] = chip_sum(got, 2)
        dw2a, got = wgrad(act, dfb, HALF, D, f"wgrad_w2_l{l}a", carry=Carry("swap", mine[2:4]) if own else None)
        g_ffn_a = [dw1a, dw2a.reshape(NCHIP, w2r, D)]
        if own:
            other[2:4] = list(got)
            totals[l] = (mine, other)
        g_prev = g_ffn_a + g_mixer + g_ffn_b
        sums[l] = (sums0, sums1, sums_o, sums2, sums_m)
    return loss_blk, dx, totals, sums, g_ffn_a


def small_all_gather(blk, name):
    m_per, n = blk.shape

    def body(x_ref, out_ref, send_sems, recv_sems, local_sem):
        x, y, c = _here()
        me, sibling = (x, y, c), (x, y, 1 - c)
        chips = [(1 - x, y), (x, 1 - y), (1 - x, 1 - y)]

        def rows(px, py, pc):
            return out_ref.at[pl.ds((4 * px + 2 * py + pc) * m_per, m_per), :]

        def copy(k, block, to, src=None):
            return pltpu.make_async_remote_copy(
                src_ref=rows(*block) if src is None else src, dst_ref=rows(*block),
                send_sem=send_sems.at[k], recv_sem=recv_sems.at[k], device_id=to, device_id_type=MESH)

        mine = pltpu.make_async_copy(x_ref, rows(*me), local_sem)
        mine.start()
        first = [copy(0, me, sibling, src=x_ref)]
        first += [copy(1 + j, me, (*chip, c), src=x_ref) for j, chip in enumerate(chips)]
        for cp in first:
            cp.start()
        passed = [copy(4 + j, (*chip, c), sibling) for j, chip in enumerate(chips)]
        for j, chip in enumerate(chips):
            copy(1 + j, (*chip, c), me).wait_recv()
            passed[j].start()
        copy(0, sibling, me).wait_recv()
        for j, chip in enumerate(chips):
            copy(4 + j, (*chip, 1 - c), me).wait_recv()
        for cp in first + passed:
            cp.wait_send()
        mine.wait()

    return pl.pallas_call(
        body, name=name,
        out_shape=jax.ShapeDtypeStruct((NDEV * m_per, n), blk.dtype),
        in_specs=[pl.BlockSpec(memory_space=pltpu.VMEM)],
        out_specs=pl.BlockSpec(memory_space=pltpu.VMEM),
        scratch_shapes=[pltpu.SemaphoreType.DMA((7,)), pltpu.SemaphoreType.DMA((7,)), pltpu.SemaphoreType.DMA],
        compiler_params=pltpu.CompilerParams(vmem_limit_bytes=VMEM_LIMIT),
    )(blk)


EW_BLOCK_BYTES = 1 << 20


def _ew_rows(rows, cols):
    want = max(16, EW_BLOCK_BYTES // (4 * cols))
    best = None
    for t in range(16, rows + 1, 16):
        if rows % t == 0 and t <= want:
            best = t
    return best if best is not None else rows


def add_half(g, recv, cflag, name):
    pieces, r, cols = g.shape
    r2 = r // 2
    tr = _ew_rows(r2, cols)
    nt = r2 // tr

    def body(c_ref, g_ref, r_ref, o_ref):
        o_ref[...] = (g_ref[...] + r_ref[...]).astype(o_ref.dtype)

    half = pl.BlockSpec((None, tr, cols), lambda j, i, c_ref: (j, i, 0))
    return pl.pallas_call(
        body, name=name,
        grid_spec=pltpu.PrefetchScalarGridSpec(
            num_scalar_prefetch=1, grid=(pieces, nt),
            in_specs=[pl.BlockSpec((None, tr, cols), lambda j, i, c_ref: (j, c_ref[0] * nt + i, 0)), half],
            out_specs=half),
        out_shape=jax.ShapeDtypeStruct((pieces, r2, cols), WIRE_DTYPE),
        compiler_params=_params(("arbitrary", "arbitrary")),
    )(cflag, g, recv)


def sum_chips(recv, name):
    _, r, cols = recv.shape
    tr = _ew_rows(r, cols)

    def body(r_ref, o_ref):
        acc = r_ref[0].astype(F32)
        for k in range(1, NCHIP):
            acc = acc + r_ref[k].astype(F32)
        o_ref[...] = acc

    return pl.pallas_call(
        body, name=name, grid=(r // tr,),
        in_specs=[pl.BlockSpec((NCHIP, tr, cols), lambda i: (0, i, 0))],
        out_specs=pl.BlockSpec((tr, cols), lambda i: (i, 0)),
        out_shape=jax.ShapeDtypeStruct((r, cols), F32),
        compiler_params=_params(("arbitrary",)),
    )(recv)


def sum_devices(rows8, name):
    def body(r_ref, o_ref):
        acc = r_ref[0:1, :]
        for k in range(1, NDEV):
            acc = acc + r_ref[k:k + 1, :]
        o_ref[...] = jnp.broadcast_to(acc, o_ref.shape)

    return pl.pallas_call(
        body, name=name, out_shape=jax.ShapeDtypeStruct(rows8.shape, F32),
        in_specs=[pl.BlockSpec(memory_space=pltpu.VMEM)], out_specs=pl.BlockSpec(memory_space=pltpu.VMEM),
        compiler_params=pltpu.CompilerParams(vmem_limit_bytes=VMEM_LIMIT),
    )(rows8)


def adamw(w, m, v, srcs, cflag, name, halves=False, carry=None):
    planes, r, cols = w.shape
    rh = r // 2 if halves else r
    tr = _ew_rows(rh, cols)
    nth = rh // tr
    flat = [a for s in srcs for a in (s if halves else (s,))]
    ns = len(flat)
    per = ns // planes

    def body(c_ref, w_ref, m_ref, v_ref, *rest):
        s_refs, (g_ref, d_ref, mo_ref, vo_ref) = rest[:ns], rest[ns:]
        p, i = pl.program_id(0), pl.program_id(1)
        if halves:
            mine = jnp.logical_not(jnp.logical_xor(i >= nth, c_ref[0] == 1))
            blocks = [jnp.where(mine, s_refs[2 * k][...], s_refs[2 * k + 1][...]) for k in range(planes)]
        else:
            blocks = [s[...] for s in s_refs]
        g = blocks[0]
        for k in range(1, planes):
            g = jnp.where(p == k, blocks[k], g)
        g_ref[...] = g
        m_new = ADAM_B1 * m_ref[...] + (1.0 - ADAM_B1) * g
        v_new = ADAM_B2 * v_ref[...] + (1.0 - ADAM_B2) * (g * g)
        mo_ref[...] = m_new
        vo_ref[...] = v_new
        m_hat = m_new / (1.0 - ADAM_B1 ** ADAM_STEP)
        v_hat = v_new / (1.0 - ADAM_B2 ** ADAM_STEP)
        d_ref[...] = -ADAM_LR * (m_hat / (jnp.sqrt(v_hat) + ADAM_EPS) + ADAM_WD * w_ref[...])

    pt = pl.BlockSpec((None, tr, cols), lambda p, i: (p, i, 0))
    st = [pl.BlockSpec((tr, cols), functools.partial(lambda k, p, i: (jnp.where(p == k, i % nth, 0), 0), j // per))
          for j in range(ns)]
    return _pcall(body, name, (planes, r // tr), [pl.BlockSpec(memory_space=pltpu.SMEM), pt, pt, pt] + st, [pt] * 4,
                  [jax.ShapeDtypeStruct(w.shape, F32)] * 4, ("arbitrary", "arbitrary"), (cflag, w, m, v, *flat), carry)


ADA_COLS = 9 * D // NCHIP


def mod_fwd(c_all, w_ada, b_shard, name):
    def body(c_ref, w_ref, b_ref, o_ref):
        cc = c_ref[...]
        sc = cc * jax.nn.sigmoid(cc)
        o_ref[...] = jnp.dot(sc, w_ref[...], preferred_element_type=F32,
                             precision=lax.Precision.HIGHEST) + b_ref[...]

    return pl.pallas_call(
        body, name=name, grid=(2,),
        in_specs=[pl.BlockSpec((NDEV, D), lambda l: (0, 0)),
                  pl.BlockSpec((None, D, ADA_COLS), lambda l: (l, 0, 0)),
                  pl.BlockSpec((None, 1, ADA_COLS), lambda l: (l, 0, 0))],
        out_specs=pl.BlockSpec((None, NDEV, ADA_COLS), lambda l: (l, 0, 0)),
        out_shape=jax.ShapeDtypeStruct((2, NDEV, ADA_COLS), F32),
        compiler_params=_params(("arbitrary",)),
    )(c_all, w_ada, b_shard.reshape(2, 1, ADA_COLS))


def wada_grad(c_all_t, dmod, name):
    ct = ADA_COLS // 3

    def body(c_ref, d_ref, o_ref):
        cc = c_ref[...]
        sc = cc * jax.nn.sigmoid(cc)
        acc = sc[:, 0:1] * d_ref[0:1, :]
        for b in range(1, NDEV):
            acc = acc + sc[:, b:b + 1] * d_ref[b:b + 1, :]
        o_ref[...] = acc

    return pl.pallas_call(
        body, name=name, grid=(2, 3),
        in_specs=[pl.BlockSpec((D, LANES), lambda l, j: (0, 0)),
                  pl.BlockSpec((None, NDEV, ct), lambda l, j: (l, 0, j))],
        out_specs=pl.BlockSpec((None, D, ct), lambda l, j: (l, 0, j)),
        out_shape=jax.ShapeDtypeStruct((2, D, ADA_COLS), F32),
        compiler_params=_params(("arbitrary", "arbitrary")),
    )(c_all_t, dmod)


def _pad_rows(row, rows=SUBLANES):
    return jnp.concatenate([row[None, :], jnp.zeros((rows - 1, row.shape[0]), row.dtype)], axis=0)


def kernel(x, c, w_ada, b_ada, norm_g, w_in, q_norm_g, k_norm_g, conv_w, conv_b, w_out, ffn_w1, ffn_w2, loss_target, m_w_ada, m_b_ada, m_norm_g, m_w_in, m_q_norm_g, m_k_norm_g, m_conv_w, m_conv_b, m_w_out, m_ffn_w1, m_ffn_w2, v_w_ada, v_b_ada, v_norm_g, v_w_in, v_q_norm_g, v_k_norm_g, v_conv_w, v_conv_b, v_w_out, v_ffn_w1, v_ffn_w2):
    ix, iy, ic = lax.axis_index("x"), lax.axis_index("y"), lax.axis_index("c")
    chip = 2 * ix + iy
    dev = 2 * chip + ic
    cflag = jnp.reshape(ic, (1,)).astype(jnp.int32)
    ngw = norm_g.shape[-1]
    cww = conv_w.shape[-1]

    pack = jnp.concatenate([c[0], norm_g.reshape(-1), conv_w.reshape(-1)])
    got = small_all_gather(_pad_rows(pack), "gather_c_normg_convw")[::SUBLANES]
    c_all = got[:, :D]
    per_chip = got[::2]
    ng_full = jnp.concatenate([per_chip[j, D:D + 6 * ngw].reshape(2, 3, ngw) for j in range(NCHIP)], axis=-1)
    cw_full = jnp.concatenate([per_chip[j, D + 6 * ngw:].reshape(2, 3, cww) for j in range(NCHIP)], axis=-1)

    b_shard = lax.dynamic_slice_in_dim(b_ada, chip * ADA_COLS, ADA_COLS, axis=1)
    mod_blk = mod_fwd(c_all, w_ada, b_shard, "mod_fwd").reshape(2 * NDEV, ADA_COLS)
    mod_all = small_all_gather(mod_blk, "gather_mod").reshape(NDEV, 2, NDEV, ADA_COLS)[::2]
    mod_mine = lax.dynamic_index_in_dim(mod_all, dev, axis=2, keepdims=False)
    mods = [mod_mine[:, l, :].reshape(-1) for l in range(2)]

    shards, gvecs, cws = [], [], []
    for l in range(2):
        shards.append(dict(w1=[ffn_w1[l, i].astype(MXU_DTYPE) for i in range(2)],
                           w2=[ffn_w2[l, i].astype(MXU_DTYPE) for i in range(2)],
                           win=w_in[l].astype(MXU_DTYPE), wout=w_out[l].astype(MXU_DTYPE)))
        gv = jnp.stack([jnp.tile(q_norm_g[l], AW // HD), jnp.tile(k_norm_g[l], AW // HD)])
        gvecs.append(jnp.concatenate([gv, jnp.zeros((SUBLANES - 2, AW), F32)], axis=0))
        cws.append(jnp.concatenate([cw_full[l], conv_b[l][None, :], jnp.zeros((SUBLANES - 4, CW), F32)], axis=0))
    w_first = gather_split([shards[0]["w1"][0], shards[0]["w2"][0]], "gather_first_ffn")

    loss_blk, dx, totals, sums, tail = local_step(x[0], loss_target[0], mods, [ng_full[0], ng_full[1]], gvecs, cws,
                                                  shards, w_first, cflag)

    def halves(k_of_plane):
        return [(totals[l][0][k], totals[l][1][k]) for l, k in k_of_plane]

    r_win, from_sib = adamw(w_in, m_w_in, v_w_in, halves([(0, 2), (1, 2)]), cflag, "adamw_w_in", halves=True,
                            carry=Carry("swap_halves", tail))
    wire_tail = [add_half(g, r_, cflag, f"add_sibling_l0_{j}") for j, (g, r_) in enumerate(zip(tail, from_sib))]

    dmods, dngs, dqg, dkg, dcw, dcb = [], [], [], [], [], []
    for l in range(2):
        s0, s1, so, s2, sm = sums[l]
        dmods.append(jnp.concatenate([s0[0], s0[1], s0[3], s1[0], s1[1], so[0], s2[0], s2[1], s2[3]]))
        dngs.append(jnp.concatenate([s0[2], s1[2], s2[2]]))
        dqg.append(sm[0].reshape(AW // HD, HD).sum(0))
        dkg.append(sm[1].reshape(AW // HD, HD).sum(0))
        dcw.append(sm[2:5].reshape(-1))
        dcb.append(sm[5])
    small = jnp.concatenate(dmods + dngs + dqg + dkg + dcw + dcb + [loss_blk[0]])
    small_all = small_all_gather(_pad_rows(small), "gather_small_grads")[::SUBLANES]
    nm = 9 * D
    dmod_all = small_all[:, :2 * nm].reshape(NDEV, 2, NCHIP, ADA_COLS)
    dmod_mine = lax.dynamic_index_in_dim(dmod_all, chip, axis=2, keepdims=False).transpose(1, 0, 2)
    tot = sum_devices(small_all, "sum_small_grads")[0]
    o = 2 * nm
    g_b_ada = tot[:o].reshape(2, nm)
    g_norm_g = lax.dynamic_slice_in_dim(tot[o:o + 6 * D].reshape(2, 3, D), chip * ngw, ngw, axis=2)
    o += 6 * D
    g_qg = tot[o:o + 2 * HD].reshape(2, HD)
    o += 2 * HD
    g_kg = tot[o:o + 2 * HD].reshape(2, HD)
    o += 2 * HD
    g_cw = lax.dynamic_slice_in_dim(tot[o:o + 6 * CW].reshape(2, 3, CW), chip * cww, cww, axis=2)
    o += 6 * CW
    g_cb = tot[o:o + 2 * CW].reshape(2, CW)
    loss = tot[o + 2 * CW]

    c_all_t = jnp.concatenate([c_all.T, jnp.zeros((D, LANES - NDEV), F32)], axis=1)
    g_wada_src = wada_grad(c_all_t, dmod_mine, "wada_grad")

    r_wada, landed = adamw(w_ada, m_w_ada, v_w_ada, [g_wada_src[0], g_wada_src[1]], cflag, "adamw_w_ada",
                           carry=Carry("scatter", wire_tail))
    tot_tail = [sum_chips(t, f"sum_chips_l0_{j}") for j, t in enumerate(landed)]
    r_wout, oth_tail = adamw(w_out, m_w_out, v_w_out, halves([(0, 3), (1, 3)]), cflag, "adamw_w_out", halves=True,
                             carry=Carry("swap", tot_tail))
    totals[0][0][0:2], totals[0][1][0:2] = tot_tail, list(oth_tail)
    r_w1, _ = adamw(ffn_w1.reshape(4, D, HALF), m_ffn_w1.reshape(4, D, HALF), v_ffn_w1.reshape(4, D, HALF),
                    halves([(0, 0), (0, 4), (1, 0), (1, 4)]), cflag, "adamw_ffn_w1", halves=True)
    w2r = DFF // NCHIP
    r_w2, _ = adamw(ffn_w2.reshape(4, w2r, D), m_ffn_w2.reshape(4, w2r, D), v_ffn_w2.reshape(4, w2r, D),
                    halves([(0, 1), (0, 5), (1, 1), (1, 5)]), cflag, "adamw_ffn_w2", halves=True)
    r_w1 = [t.reshape(ffn_w1.shape) for t in r_w1]
    r_w2 = [t.reshape(ffn_w2.shape) for t in r_w2]

    smalls = [("b_ada", b_ada, m_b_ada, v_b_ada, g_b_ada), ("norm_g", norm_g, m_norm_g, v_norm_g, g_norm_g),
              ("q_norm_g", q_norm_g, m_q_norm_g, v_q_norm_g, g_qg), ("k_norm_g", k_norm_g, m_k_norm_g, v_k_norm_g, g_kg),
              ("conv_w", conv_w, m_conv_w, v_conv_w, g_cw), ("conv_b", conv_b, m_conv_b, v_conv_b, g_cb)]
    n_small = sum(t[1].size for t in smalls)
    pad = (-n_small) % (16 * LANES)

    def packed(idx):
        flat = jnp.concatenate([t[idx].reshape(-1) for t in smalls] + [jnp.zeros((pad,), F32)])
        return flat.reshape(-1, LANES)

    r_small, _ = adamw(packed(1)[None], packed(2)[None], packed(3)[None], [packed(4)], cflag, "adamw_small")
    small_out = {}
    o = 0
    for name_, w_, _, _, _ in smalls:
        small_out[name_] = [t.reshape(-1)[o:o + w_.size].reshape(w_.shape) for t in r_small]
        o += w_.size

    res = {"w_ada": r_wada, "w_in": r_win, "w_out": r_wout, "ffn_w1": r_w1, "ffn_w2": r_w2, **small_out}
    order = ["w_ada", "b_ada", "norm_g", "w_in", "q_norm_g", "k_norm_g", "conv_w", "conv_b", "w_out", "ffn_w1", "ffn_w2"]
    outs = [loss, dx[None]]
    for k in range(4):
        outs += [res[nm_][k] for nm_ in order]
    return tuple(outs)
```

```python
import functools

import jax
import jax.numpy as jnp
from jax import lax
from jax.experimental import pallas as pl
from jax.experimental.pallas import tpu as pltpu

F32 = jnp.float32
MXU_DTYPE = jnp.bfloat16
ACT_DTYPE = jnp.bfloat16
WIRE_DTYPE = jnp.bfloat16

D = 1024
HD = 64
AW = 512
CW = 512
DFF = 2816
HALF = DFF // 2
INC = 3 * AW + 3 * CW
NCHIP = 4
NDEV = 8
QBLK = 128
ATTN_QBLOCKS = 8
ATTN_INTERLEAVE = 4
ATTN_CHUNK_ROWS = 2048
DILATIONS = (1, 4, 16)
EPS = 1e-6
NEG = -1e30
LANES = 128
SUBLANES = 8
VMEM_LIMIT = 56 * 1024 * 1024

ADAM_LR = 0.001
ADAM_B1 = 0.9
ADAM_B2 = 0.999
ADAM_EPS = 1e-08
ADAM_WD = 0.01
ADAM_STEP = 10

NT_DIMS = (((1,), (1,)), ((), ()))
TN_DIMS = (((0,), (0,)), ((), ()))


def _params(sem, vmem=VMEM_LIMIT):
    return pltpu.CompilerParams(dimension_semantics=sem, vmem_limit_bytes=vmem)


def _row_tile(n, want):
    t = min(n, want)
    assert n % t == 0
    return t


def _ada(xt, vec_ref):
    ng, sc, sh, gt = vec_ref[0:1, :], vec_ref[1:2, :], vec_ref[2:3, :], vec_ref[3:4, :]
    r = lax.rsqrt(jnp.mean(xt * xt, axis=-1, keepdims=True) + EPS)
    return xt * r, r, ng * (1.0 + sc), ng, sc, sh, gt


def _ada_bwd(dh, xhat, r, gain, ng, sc):
    dshift = jnp.sum(dh, axis=0, keepdims=True)
    dhx = dh * xhat
    dscale = jnp.sum(dhx, axis=0, keepdims=True) * ng
    dng = jnp.sum(dhx, axis=0, keepdims=True) * (1.0 + sc)
    dxhat = dh * gain
    dx = r * (dxhat - xhat * jnp.mean(dxhat * xhat, axis=-1, keepdims=True))
    return dx, dshift, dscale, dng


def _acc_rows(sums_ref, first, rows):
    @pl.when(first)
    def _():
        sums_ref[...] = jnp.zeros_like(sums_ref)
    for k, row in enumerate(rows):
        sums_ref[k:k + 1, :] += row


MESH = pl.DeviceIdType.MESH
ANY = pl.BlockSpec(memory_space=pl.ANY)


def _here():
    return lax.axis_index("x"), lax.axis_index("y"), lax.axis_index("c")


def _ici_copies(src_refs, dst_refs, send_sems, recv_sems, local_sems, scatter):
    x, y, c = _here()
    my_chip = 2 * x + y
    peers = [(1 - x, y), (x, 1 - y), (1 - x, 1 - y)]
    local, out, inc = [], [], []
    for a, (src, dst) in enumerate(zip(src_refs, dst_refs)):
        local.append(pltpu.make_async_copy(src.at[my_chip] if scatter else src, dst.at[my_chip], local_sems.at[a]))
        for j, (px, py) in enumerate(peers):
            sems = dict(send_sem=send_sems.at[3 * a + j], recv_sem=recv_sems.at[3 * a + j],
                        device_id=(px, py, c), device_id_type=MESH)
            out.append(pltpu.make_async_remote_copy(
                src_ref=src.at[2 * px + py] if scatter else src, dst_ref=dst.at[my_chip], **sems))
            inc.append(pltpu.make_async_remote_copy(
                src_ref=src.at[my_chip] if scatter else src, dst_ref=dst.at[2 * px + py], **sems))
    return local, out, inc


def _swap_copies(src_refs, dst_refs, send_sems, recv_sems, halves):
    x, y, c = _here()
    cps = []
    for k, (src, dst) in enumerate(zip(src_refs, dst_refs)):
        if halves:
            r2 = src.shape[1] // 2
            src = src.at[:, pl.ds((1 - c) * r2, r2), :]
        cps.append(pltpu.make_async_remote_copy(
            src_ref=src, dst_ref=dst, send_sem=send_sems.at[k], recv_sem=recv_sems.at[k],
            device_id=(x, y, 1 - c), device_id_type=MESH))
    return cps


class Carry:
    def __init__(self, kind, srcs):
        self.kind, self.srcs, n = kind, list(srcs), len(srcs)
        if kind == "gather":
            shapes = [(NCHIP,) + s.shape for s in srcs]
        elif kind == "swap_halves":
            shapes = [(s.shape[0], s.shape[1] // 2, s.shape[2]) for s in srcs]
        else:
            shapes = [s.shape for s in srcs]
        self.out_shape = [jax.ShapeDtypeStruct(sh, s.dtype) for sh, s in zip(shapes, srcs)]
        dma = pltpu.SemaphoreType.DMA
        self.sems = [dma((3 * n,)), dma((3 * n,)), dma((n,))] if kind in ("gather", "scatter") else [dma((n,)), dma((n,))]

    def start(self, srcs, dsts, sems):
        if self.kind in ("gather", "scatter"):
            local, out, _ = _ici_copies(srcs, dsts, *sems, self.kind == "scatter")
            for cp in local + out:
                cp.start()
        else:
            for cp in _swap_copies(srcs, dsts, *sems, self.kind == "swap_halves"):
                cp.start()

    def wait(self, srcs, dsts, sems):
        if self.kind in ("gather", "scatter"):
            local, out, inc = _ici_copies(srcs, dsts, *sems, self.kind == "scatter")
            for cp in inc:
                cp.wait_recv()
            for cp in out:
                cp.wait_send()
            for cp in local:
                cp.wait()
        else:
            cps = _swap_copies(srcs, dsts, *sems, self.kind == "swap_halves")
            for cp in cps:
                cp.wait_recv()
            for cp in cps:
                cp.wait_send()


def run_carry(carry, name):
    n = len(carry.srcs)

    def body(*refs):
        srcs, dsts, sems = refs[:n], refs[n:2 * n], refs[2 * n:]
        carry.start(srcs, dsts, sems)
        carry.wait(srcs, dsts, sems)

    return pl.pallas_call(body, name=name, out_shape=carry.out_shape, in_specs=[ANY] * n, out_specs=[ANY] * n,
                          scratch_shapes=carry.sems)(*carry.srcs)


def gather_split(srcs, name):
    n = len(srcs)

    def body(*refs):
        src_refs, dst_refs = refs[:n], refs[n:2 * n]
        send_sems, recv_sems, fwd_send, fwd_recv, local_sems = refs[2 * n:]
        x, y, c = _here()
        my_chip = 2 * x + y
        peers = [(1 - x, y), (x, 1 - y), (1 - x, 1 - y)]

        def half(ref, h):
            r2 = ref.shape[0] // 2
            return ref.at[pl.ds(h * r2, r2), :]

        local, out, landed, passed, arriving = [], [], [], [], []
        for a, (src, dst) in enumerate(zip(src_refs, dst_refs)):
            local.append(pltpu.make_async_copy(src, dst.at[my_chip], local_sems.at[a]))
            for j, (px, py) in enumerate(peers):
                k = 3 * a + j
                theirs = dst.at[2 * px + py]
                ici = dict(send_sem=send_sems.at[k], recv_sem=recv_sems.at[k], device_id=(px, py, c), device_id_type=MESH)
                d2d = dict(send_sem=fwd_send.at[k], recv_sem=fwd_recv.at[k], device_id=(x, y, 1 - c), device_id_type=MESH)
                out.append(pltpu.make_async_remote_copy(src_ref=half(src, c), dst_ref=half(dst.at[my_chip], c), **ici))
                landed.append(pltpu.make_async_remote_copy(src_ref=half(src, c), dst_ref=half(theirs, c), **ici))
                passed.append(pltpu.make_async_remote_copy(src_ref=half(theirs, c), dst_ref=half(theirs, c), **d2d))
                arriving.append(pltpu.make_async_remote_copy(src_ref=half(theirs, c), dst_ref=half(theirs, 1 - c), **d2d))
        for cp in local + out:
            cp.start()
        for got, fwd in zip(landed, passed):
            got.wait_recv()
            fwd.start()
        for cp in arriving:
            cp.wait_recv()
        for cp in out + passed:
            cp.wait_send()
        for cp in local:
            cp.wait()

    dma = pltpu.SemaphoreType.DMA
    return pl.pallas_call(
        body, name=name, out_shape=[jax.ShapeDtypeStruct((NCHIP,) + s.shape, s.dtype) for s in srcs],
        in_specs=[ANY] * n, out_specs=[ANY] * n,
        scratch_shapes=[dma((3 * n,)), dma((3 * n,)), dma((3 * n,)), dma((3 * n,)), dma((n,))],
    )(*srcs)


def _pcall(body, name, grid, in_specs, out_specs, out_shape, sem, args, carry=None, scratch=()):
    if carry is None:
        outs = pl.pallas_call(body, name=name, grid=grid, in_specs=in_specs, out_specs=out_specs,
                              out_shape=out_shape, scratch_shapes=list(scratch), compiler_params=_params(sem))(*args)
        return outs, []
    n_in, n_out, nc, ns = len(in_specs), len(out_specs), len(carry.srcs), len(scratch)

    def wrapped(*refs):
        ins, csrc = refs[:n_in], refs[n_in:n_in + nc]
        outs, cdst = refs[n_in + nc:n_in + nc + n_out], refs[n_in + nc + n_out:n_in + 2 * nc + n_out]
        own = refs[n_in + 2 * nc + n_out:n_in + 2 * nc + n_out + ns]
        sems = refs[n_in + 2 * nc + n_out + ns:]
        ids = [pl.program_id(a) for a in range(len(grid))]
        first = functools.reduce(jnp.logical_and, [i == 0 for i in ids])
        last = functools.reduce(jnp.logical_and, [i == g - 1 for i, g in zip(ids, grid)])

        @pl.when(first)
        def _():
            carry.start(csrc, cdst, sems)

        body(*ins, *outs, *own)

        @pl.when(last)
        def _():
            carry.wait(csrc, cdst, sems)

    res = pl.pallas_call(
        wrapped, name=name, grid=grid,
        in_specs=list(in_specs) + [ANY] * nc, out_specs=list(out_specs) + [ANY] * nc,
        out_shape=list(out_shape) + carry.out_shape,
        scratch_shapes=list(scratch) + carry.sems, compiler_params=_params(sem),
    )(*args, *carry.srcs)
    return res[:n_out], res[n_out:]


def ffn_fwd(x, vec, w1p, w2, gs, name, carry=None):
    S = x.shape[0]
    tm = _row_tile(S, 512)

    def body(x_ref, vec_ref, w1_ref, w2_ref, xn_ref, a_ref, f_ref):
        xt = x_ref[...]
        xhat, _, gain, _, _, sh, gt = _ada(xt, vec_ref)
        h = (xhat * gain + sh).astype(MXU_DTYPE)
        f = jnp.zeros((tm, D), F32)
        for hf in range(2):
            g = jnp.dot(h, w1_ref[hf], preferred_element_type=F32)
            up = jnp.dot(h, w1_ref[2 + hf], preferred_element_type=F32)
            a_ref[:, hf * HALF:(hf + 1) * HALF] = g.astype(a_ref.dtype)
            a_ref[:, DFF + hf * HALF:DFF + (hf + 1) * HALF] = up.astype(a_ref.dtype)
            act = (g * jax.nn.sigmoid(g) * up).astype(MXU_DTYPE)
            f = f + jnp.dot(act, w2_ref[hf * HALF:(hf + 1) * HALF, :], preferred_element_type=F32)
        xn_ref[...] = xt + (gs * gt) * f
        f_ref[...] = f.astype(f_ref.dtype)

    return _pcall(
        body, name, (S // tm,),
        [pl.BlockSpec((tm, D), lambda i: (i, 0)),
         pl.BlockSpec((SUBLANES, D), lambda i: (0, 0)),
         pl.BlockSpec((NCHIP, D, HALF), lambda i: (0, 0, 0), pipeline_mode=pl.Buffered(1)),
         pl.BlockSpec((DFF, D), lambda i: (0, 0), pipeline_mode=pl.Buffered(1))],
        [pl.BlockSpec((tm, D), lambda i: (i, 0)),
         pl.BlockSpec((tm, 2 * DFF), lambda i: (i, 0)),
         pl.BlockSpec((tm, D), lambda i: (i, 0))],
        [jax.ShapeDtypeStruct((S, D), F32),
         jax.ShapeDtypeStruct((S, 2 * DFF), ACT_DTYPE),
         jax.ShapeDtypeStruct((S, D), ACT_DTYPE)],
        ("arbitrary",), (x, vec, w1p, w2), carry)


def ffn_bwd(dxo, x, a, f, vec, w1p, w2, gs, name, carry=None):
    S = x.shape[0]
    tm = _row_tile(S, 256)

    def body(dxo_ref, x_ref, a_ref, f_ref, vec_ref, w1_ref, w2_ref,
             dxi_ref, hb_ref, dfb_ref, act_ref, da_ref, sums_ref):
        xt = x_ref[...]
        dxo = dxo_ref[...]
        xhat, r, gain, ng, sc, sh, gt = _ada(xt, vec_ref)
        hb_ref[...] = (xhat * gain + sh).astype(hb_ref.dtype)
        dgate = gs * jnp.sum(dxo * f_ref[...].astype(F32), axis=0, keepdims=True)
        df = ((gs * gt) * dxo).astype(MXU_DTYPE)
        dfb_ref[...] = df
        dh = jnp.zeros((tm, D), F32)
        for hf in range(2):
            lo, hi = hf * HALF, (hf + 1) * HALF
            dact = lax.dot_general(df, w2_ref[lo:hi, :], NT_DIMS, preferred_element_type=F32)
            g = a_ref[:, lo:hi].astype(F32)
            up = a_ref[:, DFF + lo:DFF + hi].astype(F32)
            sg = jax.nn.sigmoid(g)
            si = g * sg
            act_ref[:, lo:hi] = (si * up).astype(act_ref.dtype)
            dg = (dact * up * (sg * (1.0 + g * (1.0 - sg)))).astype(MXU_DTYPE)
            dup = (dact * si).astype(MXU_DTYPE)
            da_ref[:, lo:hi] = dg
            da_ref[:, DFF + lo:DFF + hi] = dup
            dh = dh + lax.dot_general(dg, w1_ref[hf], NT_DIMS, preferred_element_type=F32)
            dh = dh + lax.dot_general(dup, w1_ref[2 + hf], NT_DIMS, preferred_element_type=F32)
        dx, dshift, dscale, dng = _ada_bwd(dh, xhat, r, gain, ng, sc)
        dxi_ref[...] = dxo + dx
        _acc_rows(sums_ref, pl.program_id(0) == 0, (dshift, dscale, dng, dgate))

    return _pcall(
        body, name, (S // tm,),
        [pl.BlockSpec((tm, D), lambda i: (i, 0)),
         pl.BlockSpec((tm, D), lambda i: (i, 0)),
         pl.BlockSpec((tm, 2 * DFF), lambda i: (i, 0)),
         pl.BlockSpec((tm, D), lambda i: (i, 0)),
         pl.BlockSpec((SUBLANES, D), lambda i: (0, 0)),
         pl.BlockSpec((NCHIP, D, HALF), lambda i: (0, 0, 0), pipeline_mode=pl.Buffered(1)),
         pl.BlockSpec((DFF, D), lambda i: (0, 0), pipeline_mode=pl.Buffered(1))],
        [pl.BlockSpec((tm, D), lambda i: (i, 0)),
         pl.BlockSpec((tm, D), lambda i: (i, 0)),
         pl.BlockSpec((tm, D), lambda i: (i, 0)),
         pl.BlockSpec((tm, DFF), lambda i: (i, 0)),
         pl.BlockSpec((tm, 2 * DFF), lambda i: (i, 0)),
         pl.BlockSpec((SUBLANES, D), lambda i: (0, 0))],
        [jax.ShapeDtypeStruct((S, D), F32),
         jax.ShapeDtypeStruct((S, D), MXU_DTYPE),
         jax.ShapeDtypeStruct((S, D), MXU_DTYPE),
         jax.ShapeDtypeStruct((S, DFF), MXU_DTYPE),
         jax.ShapeDtypeStruct((S, 2 * DFF), MXU_DTYPE),
         jax.ShapeDtypeStruct((SUBLANES, D), F32)],
        ("arbitrary",), (dxo, x, a, f, vec, w1p, w2), carry)


def wgrad(a, b, kt, nt, name, carry=None):
    T, K = a.shape
    N = b.shape[1]
    pk, pn = K // kt, N // nt
    assert pk == 1 or pn == 1
    tt = _row_tile(T, 2048)
    steps = T // tt

    def body(a_ref, b_ref, o_ref):
        @pl.when(pl.program_id(1) == 0)
        def _():
            o_ref[...] = jnp.zeros_like(o_ref)
        o_ref[...] += lax.dot_general(a_ref[...], b_ref[...], TN_DIMS, preferred_element_type=F32)

    a_map = (lambda p, t: (t, p)) if pk > 1 else (lambda p, t: (t, 0))
    b_map = (lambda p, t: (t, p)) if pn > 1 else (lambda p, t: (t, 0))
    (out,), got = _pcall(
        body, name, (pk * pn, steps),
        [pl.BlockSpec((tt, kt), a_map), pl.BlockSpec((tt, nt), b_map)],
        [pl.BlockSpec((None, kt, nt), lambda p, t: (p, 0, 0))],
        [jax.ShapeDtypeStruct((pk * pn, kt, nt), F32)], ("arbitrary", "arbitrary"), (a, b), carry)
    return out, got


def _head_masks(rows):
    lane = lax.broadcasted_iota(jnp.int32, (rows, LANES), 1)
    return lane < HD


def _pair_stat(x, m_a):
    s_a = jnp.sum(jnp.where(m_a, x, 0.0), axis=1, keepdims=True)
    s_b = jnp.sum(jnp.where(m_a, 0.0, x), axis=1, keepdims=True)
    return s_a, s_b


def mixer_in(x, vec, winp, gvec, name):
    S = x.shape[0]
    tm = _row_tile(S, 512)
    pc = INC // NCHIP

    def body(x_ref, vec_ref, w_ref, g_ref, proj_ref, hb_ref, qn_ref, kn_ref, v_ref):
        xt = x_ref[...]
        xhat, _, gain, _, _, sh, _ = _ada(xt, vec_ref)
        h = (xhat * gain + sh).astype(MXU_DTYPE)
        hb_ref[...] = h
        for j in range(NCHIP):
            proj_ref[:, j * pc:(j + 1) * pc] = jnp.dot(h, w_ref[j], preferred_element_type=F32)
        m_a = _head_masks(tm)
        for which, dst in ((0, qn_ref), (1, kn_ref)):
            for p in range(AW // LANES):
                lo = which * AW + p * LANES
                xp = proj_ref[:, lo:lo + LANES]
                s_a, s_b = _pair_stat(xp * xp, m_a)
                rr = jnp.where(m_a, lax.rsqrt(s_a * (1.0 / HD) + EPS), lax.rsqrt(s_b * (1.0 / HD) + EPS))
                gp = g_ref[which:which + 1, p * LANES:(p + 1) * LANES]
                dst[:, p * LANES:(p + 1) * LANES] = (xp * rr * gp).astype(dst.dtype)
        v_ref[...] = proj_ref[:, 2 * AW:3 * AW].astype(v_ref.dtype)

    return pl.pallas_call(
        body, name=name, grid=(S // tm,),
        in_specs=[pl.BlockSpec((tm, D), lambda i: (i, 0)),
                  pl.BlockSpec((SUBLANES, D), lambda i: (0, 0)),
                  pl.BlockSpec((NCHIP, D, pc), lambda i: (0, 0, 0), pipeline_mode=pl.Buffered(1)),
                  pl.BlockSpec((SUBLANES, AW), lambda i: (0, 0))],
        out_specs=[pl.BlockSpec((tm, INC), lambda i: (i, 0)),
                   pl.BlockSpec((tm, D), lambda i: (i, 0)),
                   pl.BlockSpec((tm, AW), lambda i: (i, 0)),
                   pl.BlockSpec((tm, AW), lambda i: (i, 0)),
                   pl.BlockSpec((tm, AW), lambda i: (i, 0))],
        out_shape=[jax.ShapeDtypeStruct((S, INC), F32),
                   jax.ShapeDtypeStruct((S, D), MXU_DTYPE),
                   jax.ShapeDtypeStruct((S, AW), F32),
                   jax.ShapeDtypeStruct((S, AW), F32),
                   jax.ShapeDtypeStruct((S, AW), F32)],
        compiler_params=_params(("arbitrary",)),
    )(x, vec, winp, gvec)


def _band_masks(ncol):
    row = lax.broadcasted_iota(jnp.int32, (2 * QBLK, ncol), 0) & (QBLK - 1)
    col = lax.broadcasted_iota(jnp.int32, (2 * QBLK, ncol), 1)
    return row, col


def _stack_heads(t, m_a):
    zero = jnp.zeros_like(t)
    return jnp.concatenate([jnp.where(m_a, t, zero), jnp.where(m_a, zero, t)], axis=0)


class _AttnLayout:
    def __init__(self, d, S):
        self.d, self.S = d, S
        self.qb = max(1, min(ATTN_QBLOCKS, ATTN_CHUNK_ROWS // (QBLK * d)))
        self.nres = d
        self.nchunk = S // (self.qb * QBLK * d)
        self.grid = (AW // LANES, self.nchunk)
        self.unroll = max(1, min(d, ATTN_INTERLEAVE // self.qb))

    def _spec(self, blocks, row_of):
        return pl.BlockSpec((blocks * QBLK * self.d, LANES), lambda hp, j: (row_of(j), hp))

    def cur(self, chunk_of):
        return self._spec(self.qb, chunk_of)

    def prev(self, chunk_of):
        return self._spec(1, lambda j: jnp.maximum(chunk_of(j) * self.qb - 1, 0))

    def idx(self, b, r):
        if self.d == 1:
            return (pl.ds(b * QBLK, QBLK), slice(None))
        return (pl.ds(b * QBLK * self.d + r, QBLK, stride=self.d), slice(None))

    def per_residue(self, fn):
        if self.nres == 1:
            fn(0)
        else:
            def step(it, carry):
                for k in range(self.unroll):
                    fn(it * self.unroll + k)
                return carry
            lax.fori_loop(0, self.nres // self.unroll, step, 0)


def attn_fwd(qn, kn, v, d, name, carry=None):
    S = qn.shape[0]
    lay = _AttnLayout(d, S)
    qb = lay.qb

    def body(q_ref, kc_ref, kp_ref, vc_ref, vp_ref, o_ref, lse_ref):
        i = pl.program_id(1)
        m_a = _head_masks(QBLK)
        row, col = _band_masks(2 * QBLK)
        dist = row + QBLK - col
        band = (dist >= 0) & (dist <= QBLK)
        first = band & ((i > 0) | (col >= QBLK))

        def residue(r):
            kt = [kp_ref[lay.idx(0, r)].astype(MXU_DTYPE)]
            vt = [vp_ref[lay.idx(0, r)].astype(MXU_DTYPE)]
            for b in range(qb):
                kt.append(kc_ref[lay.idx(b, r)].astype(MXU_DTYPE))
                vt.append(vc_ref[lay.idx(b, r)].astype(MXU_DTYPE))
            for b in range(qb):
                rows = lay.idx(b, r)
                q = (q_ref[rows] * (HD ** -0.5)).astype(MXU_DTYPE)
                kcat = jnp.concatenate([kt[b], kt[b + 1]], axis=0)
                vcat = jnp.concatenate([vt[b], vt[b + 1]], axis=0)
                mask = first if b == 0 else band
                s = lax.dot_general(_stack_heads(q, m_a), kcat, NT_DIMS, preferred_element_type=F32)
                s = jnp.where(mask, s, NEG)
                m = jnp.max(s, axis=1, keepdims=True)
                p = jnp.exp(s - m)
                l = jnp.sum(p, axis=1, keepdims=True)
                o = jnp.dot(p.astype(MXU_DTYPE), vcat, preferred_element_type=F32) / l
                lse = jnp.broadcast_to(m + jnp.log(l), (2 * QBLK, LANES))
                o_ref[rows] = jnp.where(m_a, o[:QBLK], o[QBLK:])
                lse_ref[rows] = jnp.where(m_a, lse[:QBLK], lse[QBLK:])

        lay.per_residue(residue)

    cur, prev = lay.cur(lambda j: j), lay.prev(lambda j: j)
    return _pcall(body, name, lay.grid, [cur, cur, prev, cur, prev], [cur, cur],
                  [jax.ShapeDtypeStruct((S, AW), F32)] * 2, ("arbitrary", "arbitrary"), (qn, kn, kn, v, v), carry)


def _both_heads(t, m_a):
    other = pltpu.roll(t, HD, 1)
    return jnp.concatenate([jnp.where(m_a, t, other), jnp.where(m_a, other, t)], axis=0)


def attn_bwd(qn, kn, v, dycat, lse, delta, d, name, carry=None):
    S = qn.shape[0]
    lay = _AttnLayout(d, S)
    qb, nchunk = lay.qb, lay.nchunk

    def body(q_ref, kc_ref, kp_ref, vc_ref, vp_ref, do_ref, lse_ref, dl_ref,
             dq_ref, dk_ref, dv_ref, ck_ref, cv_ref):
        j = pl.program_id(1)
        i = nchunk - 1 - j
        m_a = _head_masks(QBLK)
        row, col = _band_masks(2 * QBLK)
        dist = row + QBLK - col
        band = (dist >= 0) & (dist <= QBLK)
        first = band & ((i > 0) | (col >= QBLK))

        def residue(r):
            def tiles(ref, cast):
                out = [ref[lay.idx(b, r)] for b in range(qb)]
                return [t.astype(MXU_DTYPE) for t in out] if cast else out

            def ktiles(cur_ref, prev_ref):
                return [prev_ref[lay.idx(0, r)].astype(MXU_DTYPE)] + tiles(cur_ref, True)

            qt = [(t * (HD ** -0.5)).astype(MXU_DTYPE) for t in tiles(q_ref, False)]
            dot_ = tiles(do_ref, True)
            lse_t = tiles(lse_ref, False)
            dl_t = tiles(dl_ref, False)
            kt = ktiles(kc_ref, kp_ref)
            vt = ktiles(vc_ref, vp_ref)
            dk_acc = [jnp.zeros((QBLK, LANES), F32) for _ in range(qb)]
            dv_acc = [jnp.zeros((QBLK, LANES), F32) for _ in range(qb)]
            crow = pl.ds(0, QBLK) if lay.nres == 1 else pl.ds(pl.multiple_of(r * QBLK, QBLK), QBLK)
            dk_acc[qb - 1] = jnp.where(j > 0, ck_ref[crow, :], 0.0)
            dv_acc[qb - 1] = jnp.where(j > 0, cv_ref[crow, :], 0.0)
            for x in range(qb):
                kcat = jnp.concatenate([kt[x], kt[x + 1]], axis=0)
                vcat = jnp.concatenate([vt[x], vt[x + 1]], axis=0)
                q2 = _stack_heads(qt[x], m_a)
                do2 = _stack_heads(dot_[x], m_a)
                lse2 = _both_heads(lse_t[x], m_a)
                dl2 = _both_heads(dl_t[x], m_a)
                lse2 = jnp.concatenate([lse2, lse2], axis=1)
                dl2 = jnp.concatenate([dl2, dl2], axis=1)
                s = lax.dot_general(q2, kcat, NT_DIMS, preferred_element_type=F32)
                p = jnp.exp(jnp.where(first if x == 0 else band, s, NEG) - lse2)
                dp = lax.dot_general(do2, vcat, NT_DIMS, preferred_element_type=F32)
                ds = p * (dp - dl2)
                dq = jnp.dot(ds.astype(MXU_DTYPE), kcat, preferred_element_type=F32)
                dq_ref[lay.idx(x, r)] = jnp.where(m_a, dq[:QBLK], dq[QBLK:]) * (HD ** -0.5)
                dk = jnp.dot(ds.T.astype(MXU_DTYPE), q2, preferred_element_type=F32)
                dv = jnp.dot(p.T.astype(MXU_DTYPE), do2, preferred_element_type=F32)
                if x == 0:
                    ck_ref[crow, :] = dk[:QBLK]
                    cv_ref[crow, :] = dv[:QBLK]
                else:
                    dk_acc[x - 1] = dk_acc[x - 1] + dk[:QBLK]
                    dv_acc[x - 1] = dv_acc[x - 1] + dv[:QBLK]
                dk_acc[x] = dk_acc[x] + dk[QBLK:]
                dv_acc[x] = dv_acc[x] + dv[QBLK:]
            for kb in range(qb):
                dk_ref[lay.idx(kb, r)] = dk_acc[kb]
                dv_ref[lay.idx(kb, r)] = dv_acc[kb]

        lay.per_residue(residue)

    cur, prev = lay.cur(lambda j: nchunk - 1 - j), lay.prev(lambda j: nchunk - 1 - j)
    carried = pltpu.VMEM((lay.nres * QBLK, LANES), F32)
    return _pcall(
        body, name, lay.grid, [cur, cur, prev, cur, prev, cur, cur, cur], [cur, cur, cur],
        [jax.ShapeDtypeStruct((S, AW), F32)] * 3, ("arbitrary", "arbitrary"),
        (qn, kn, kn, v, v, dycat, lse, delta), carry, scratch=[carried, carried])


def _shift_down(x, halo_prev, k, row):
    tm = x.shape[0]
    tail = jnp.concatenate([pltpu.roll(halo_prev, k, 0), jnp.zeros((tm - SUBLANES, x.shape[1]), x.dtype)], axis=0)
    return jnp.where(row < k, tail, pltpu.roll(x, k, 0))


def _shift_up(x, halo_next, k, row):
    tm = x.shape[0]
    head = jnp.concatenate([jnp.zeros((tm - SUBLANES, x.shape[1]), x.dtype), pltpu.roll(halo_next, SUBLANES - k, 0)], axis=0)
    return jnp.where(row >= tm - k, head, pltpu.roll(x, tm - k, 0))


def _conv_fwd(cu, halo_cu, cw_ref, row):
    u1 = _shift_down(cu, halo_cu, 1, row)
    u2 = _shift_down(cu, halo_cu, 2, row)
    cv = cw_ref[0:1, :] * u2 + cw_ref[1:2, :] * u1 + cw_ref[2:3, :] * cu + cw_ref[3:4, :]
    return cv, u1, u2


def combine_conv(os_, lses, proj, cw, name, carry=None):
    S = proj.shape[0]
    tm = _row_tile(S, 512)
    hb = tm // SUBLANES

    def body(o1, o2, o3, l1, l2, l3, pc_ref, ph_ref, cw_ref, ycat_ref, lse_ref):
        i = pl.program_id(0)
        for p in range(AW // LANES):
            cs = slice(p * LANES, (p + 1) * LANES)
            ls = [l[:, cs] for l in (l1, l2, l3)]
            mx = jnp.maximum(jnp.maximum(ls[0], ls[1]), ls[2])
            t = mx + jnp.log(jnp.exp(ls[0] - mx) + jnp.exp(ls[1] - mx) + jnp.exp(ls[2] - mx))
            lse_ref[:, cs] = t
            acc = jnp.zeros((tm, LANES), F32)
            for l, o in zip(ls, (o1, o2, o3)):
                acc = acc + jnp.exp(l - t) * o[:, cs]
            ycat_ref[:, cs] = acc.astype(ycat_ref.dtype)
        row = lax.broadcasted_iota(jnp.int32, (tm, CW), 0)
        gb, gc, u = pc_ref[:, 0:CW], pc_ref[:, CW:2 * CW], pc_ref[:, 2 * CW:3 * CW]
        halo_cu = jnp.where(i > 0, ph_ref[:, CW:2 * CW] * ph_ref[:, 2 * CW:3 * CW], 0.0)
        cv, _, _ = _conv_fwd(gc * u, halo_cu, cw_ref, row)
        ycat_ref[:, AW:AW + CW] = (gb * cv).astype(ycat_ref.dtype)

    ot = pl.BlockSpec((tm, AW), lambda i: (i, 0))
    return _pcall(
        body, name, (S // tm,),
        [ot] * 6 + [pl.BlockSpec((tm, 3 * CW), lambda i: (i, 1)),
                    pl.BlockSpec((SUBLANES, 3 * CW), lambda i: (jnp.maximum(i * hb - 1, 0), 1)),
                    pl.BlockSpec((SUBLANES, CW), lambda i: (0, 0))],
        [pl.BlockSpec((tm, D), lambda i: (i, 0)), ot],
        [jax.ShapeDtypeStruct((S, D), ACT_DTYPE), jax.ShapeDtypeStruct((S, AW), F32)],
        ("arbitrary",), (*os_, *lses, proj, proj, cw), carry)


def out_proj(ycat, x, vec, wout, name):
    S = x.shape[0]
    tm = _row_tile(S, 512)

    def body(yc_ref, x_ref, vec_ref, w_ref, xn_ref, y_ref):
        y = jnp.dot(yc_ref[...].astype(MXU_DTYPE), w_ref[...], preferred_element_type=F32)
        xn_ref[...] = x_ref[...] + vec_ref[3:4, :] * y
        y_ref[...] = y.astype(y_ref.dtype)

    t = pl.BlockSpec((tm, D), lambda i: (i, 0))
    return pl.pallas_call(
        body, name=name, grid=(S // tm,),
        in_specs=[t, t, pl.BlockSpec((SUBLANES, D), lambda i: (0, 0)),
                  pl.BlockSpec((D, D), lambda i: (0, 0))],
        out_specs=[t, t],
        out_shape=[jax.ShapeDtypeStruct((S, D), F32), jax.ShapeDtypeStruct((S, D), ACT_DTYPE)],
        compiler_params=_params(("arbitrary",)),
    )(ycat, x, vec, wout)


def out_proj_bwd(dxo, y, ycat, vec, wout, name, carry=None):
    S = dxo.shape[0]
    tm = _row_tile(S, 512)

    def body(dxo_ref, y_ref, yc_ref, vec_ref, w_ref, dyb_ref, dyc_ref, dl_ref, sums_ref):
        dxo = dxo_ref[...]
        dgate = jnp.sum(dxo * y_ref[...].astype(F32), axis=0, keepdims=True)
        dy = (vec_ref[3:4, :] * dxo).astype(MXU_DTYPE)
        dyb_ref[...] = dy
        dyc_ref[...] = lax.dot_general(dy, w_ref[...], NT_DIMS, preferred_element_type=F32)
        m_a = _head_masks(tm)
        for p in range(AW // LANES):
            cs = slice(p * LANES, (p + 1) * LANES)
            s_a, s_b = _pair_stat(dyc_ref[:, cs] * yc_ref[:, cs].astype(F32), m_a)
            dl_ref[:, cs] = jnp.where(m_a, s_a, s_b)
        _acc_rows(sums_ref, pl.program_id(0) == 0, (dgate,))

    t = pl.BlockSpec((tm, D), lambda i: (i, 0))
    at = pl.BlockSpec((tm, AW), lambda i: (i, 0))
    return _pcall(
        body, name, (S // tm,),
        [t, t, t, pl.BlockSpec((SUBLANES, D), lambda i: (0, 0)), pl.BlockSpec((D, D), lambda i: (0, 0))],
        [t, t, at, pl.BlockSpec((SUBLANES, D), lambda i: (0, 0))],
        [jax.ShapeDtypeStruct((S, D), MXU_DTYPE), jax.ShapeDtypeStruct((S, D), F32),
         jax.ShapeDtypeStruct((S, AW), F32), jax.ShapeDtypeStruct((SUBLANES, D), F32)],
        ("arbitrary",), (dxo, y, ycat, vec, wout), carry)


def mixer_mid_bwd(dqs, dks, dvs, proj, dycat, gvec, cw, name, carry=None):
    S = proj.shape[0]
    tm = _row_tile(S, 512)
    hb = tm // SUBLANES
    nsl = S // SUBLANES
    ntile = S // tm

    def body(dq1, dq2, dq3, dk1, dk2, dk3, dv1, dv2, dv3, pr_ref, pp_ref, pn_ref, dyc_ref, dyn_ref,
             g_ref, cw_ref, dp_ref, sums_ref):
        i = pl.program_id(0)
        m_a = _head_masks(tm)
        gsum = []
        for which, parts in ((0, (dq1, dq2, dq3)), (1, (dk1, dk2, dk3))):
            acc_g = []
            for p in range(AW // LANES):
                lo = which * AW + p * LANES
                cs = slice(p * LANES, (p + 1) * LANES)
                xp = pr_ref[:, lo:lo + LANES]
                s_a, s_b = _pair_stat(xp * xp, m_a)
                rr = jnp.where(m_a, lax.rsqrt(s_a * (1.0 / HD) + EPS), lax.rsqrt(s_b * (1.0 / HD) + EPS))
                xh = xp * rr
                dn = parts[0][:, cs] + parts[1][:, cs] + parts[2][:, cs]
                acc_g.append(jnp.sum(dn * xh, axis=0, keepdims=True))
                t = dn * g_ref[which:which + 1, cs]
                t_a, t_b = _pair_stat(t * xh, m_a)
                mean = jnp.where(m_a, t_a, t_b) * (1.0 / HD)
                dp_ref[:, lo:lo + LANES] = (rr * (t - xh * mean)).astype(dp_ref.dtype)
            gsum.append(jnp.concatenate(acc_g, axis=1))
        dp_ref[:, 2 * AW:3 * AW] = (dv1[...] + dv2[...] + dv3[...]).astype(dp_ref.dtype)
        row = lax.broadcasted_iota(jnp.int32, (tm, CW), 0)
        base = 3 * AW
        gb, gc, u = pr_ref[:, base:base + CW], pr_ref[:, base + CW:base + 2 * CW], pr_ref[:, base + 2 * CW:base + 3 * CW]
        cu = gc * u
        halo_cu = jnp.where(i > 0, pp_ref[:, CW:2 * CW] * pp_ref[:, 2 * CW:3 * CW], 0.0)
        cv, u1, u2 = _conv_fwd(cu, halo_cu, cw_ref, row)
        dyc = dyc_ref[...]
        dp_ref[:, base:base + CW] = (dyc * cv).astype(dp_ref.dtype)
        dcv = dyc * gb
        halo_dcv = jnp.where(i < ntile - 1, dyn_ref[...] * pn_ref[:, 0:CW], 0.0)
        d1 = _shift_up(dcv, halo_dcv, 1, row)
        d2 = _shift_up(dcv, halo_dcv, 2, row)
        dcu = cw_ref[2:3, :] * dcv + cw_ref[1:2, :] * d1 + cw_ref[0:1, :] * d2
        dp_ref[:, base + CW:base + 2 * CW] = (dcu * u).astype(dp_ref.dtype)
        dp_ref[:, base + 2 * CW:base + 3 * CW] = (dcu * gc).astype(dp_ref.dtype)
        rows = (gsum[0], gsum[1],
                jnp.sum(dcv * u2, axis=0, keepdims=True), jnp.sum(dcv * u1, axis=0, keepdims=True),
                jnp.sum(dcv * cu, axis=0, keepdims=True), jnp.sum(dcv, axis=0, keepdims=True))
        _acc_rows(sums_ref, i == 0, rows)

    at = pl.BlockSpec((tm, AW), lambda i: (i, 0))
    return _pcall(
        body, name, (ntile,),
        [at] * 9 + [
            pl.BlockSpec((tm, INC), lambda i: (i, 0)),
            pl.BlockSpec((SUBLANES, 3 * CW), lambda i: (jnp.maximum(i * hb - 1, 0), 1)),
            pl.BlockSpec((SUBLANES, 3 * CW), lambda i: (jnp.minimum((i + 1) * hb, nsl - 1), 1)),
            pl.BlockSpec((tm, CW), lambda i: (i, 1)),
            pl.BlockSpec((SUBLANES, CW), lambda i: (jnp.minimum((i + 1) * hb, nsl - 1), 1)),
            pl.BlockSpec((SUBLANES, AW), lambda i: (0, 0)),
            pl.BlockSpec((SUBLANES, CW), lambda i: (0, 0))],
        [pl.BlockSpec((tm, INC), lambda i: (i, 0)), pl.BlockSpec((SUBLANES, AW), lambda i: (0, 0))],
        [jax.ShapeDtypeStruct((S, INC), MXU_DTYPE), jax.ShapeDtypeStruct((SUBLANES, AW), F32)],
        ("arbitrary",), (*dqs, *dks, *dvs, proj, proj, proj, dycat, dycat, gvec, cw), carry)


def mixer_in_bwd(dxo, x, dproj, vec, winp, name, carry=None):
    S = x.shape[0]
    tm = _row_tile(S, 512)
    pc = INC // NCHIP

    def body(dxo_ref, x_ref, dp_ref, vec_ref, w_ref, dxi_ref, sums_ref):
        xhat, r, gain, ng, sc, _, _ = _ada(x_ref[...], vec_ref)
        dh = jnp.zeros((tm, D), F32)
        for j in range(NCHIP):
            dh = dh + lax.dot_general(dp_ref[:, j * pc:(j + 1) * pc], w_ref[j], NT_DIMS, preferred_element_type=F32)
        dx, dshift, dscale, dng = _ada_bwd(dh, xhat, r, gain, ng, sc)
        dxi_ref[...] = dxo_ref[...] + dx
        _acc_rows(sums_ref, pl.program_id(0) == 0, (dshift, dscale, dng))

    t = pl.BlockSpec((tm, D), lambda i: (i, 0))
    return _pcall(
        body, name, (S // tm,),
        [t, t, pl.BlockSpec((tm, INC), lambda i: (i, 0)),
         pl.BlockSpec((SUBLANES, D), lambda i: (0, 0)),
         pl.BlockSpec((NCHIP, D, pc), lambda i: (0, 0, 0), pipeline_mode=pl.Buffered(1))],
        [t, pl.BlockSpec((SUBLANES, D), lambda i: (0, 0))],
        [jax.ShapeDtypeStruct((S, D), F32), jax.ShapeDtypeStruct((SUBLANES, D), F32)],
        ("arbitrary",), (dxo, x, dproj, vec, winp), carry)


def loss_head(xf, target, name):
    S = xf.shape[0]
    tm = _row_tile(S, 1024)

    def body(x_ref, t_ref, dy_ref, l_ref):
        diff = x_ref[...] - t_ref[...]
        dy_ref[...] = diff * (1.0 / D)
        part = jnp.sum(jnp.sum(diff * diff, axis=0, keepdims=True), axis=1, keepdims=True) * (0.5 / D)

        @pl.when(pl.program_id(0) == 0)
        def _():
            l_ref[...] = jnp.zeros_like(l_ref)
        l_ref[...] += jnp.broadcast_to(part, l_ref.shape)

    t = pl.BlockSpec((tm, D), lambda i: (i, 0))
    return pl.pallas_call(
        body, name=name, grid=(S // tm,),
        in_specs=[t, t],
        out_specs=[t, pl.BlockSpec((SUBLANES, LANES), lambda i: (0, 0))],
        out_shape=[jax.ShapeDtypeStruct((S, D), F32), jax.ShapeDtypeStruct((SUBLANES, LANES), F32)],
        compiler_params=_params(("arbitrary",)),
    )(xf, target)


def _vec(mod_l, ng_l, i):
    m = mod_l.reshape(3, 3, D)
    rows = jnp.stack([ng_l[i], m[i, 1], m[i, 0], m[i, 2]])
    return jnp.concatenate([rows, jnp.zeros((SUBLANES - 4, D), F32)], axis=0)


def local_step(x, target, mods, ngs, gvecs, cws, shards, w_first, cflag):
    saved = []
    weights = [dict(w1=[None, None], w2=[None, None]) for _ in range(2)]
    weights[0]["w1"][0], weights[0]["w2"][0] = w_first[0], w_first[1].reshape(DFF, D)
    h = x
    for l in range(2):
        w, sh = weights[l], shards[l]
        nxt = shards[l + 1] if l == 0 else None
        vecs = [_vec(mods[l], ngs[l], i) for i in range(3)]
        x0 = h
        (x1, a0, f0), (win, wout, w2b) = ffn_fwd(x0, vecs[0], w["w1"][0], w["w2"][0], 0.5, f"ffn_fwd_l{l}a",
                                                 carry=Carry("gather", [sh["win"], sh["wout"], sh["w2"][1]]))
        w["win"], w["wout"], w["w2"][1] = win, wout.reshape(D, D), w2b.reshape(DFF, D)
        proj, h1b, qn, kn, v = mixer_in(x1, vecs[1], w["win"], gvecs[l], f"mixer_in_l{l}")
        os_, lses, w1b = [], [], {}
        for d in DILATIONS:
            rows = {1: slice(0, D // 2), 16: slice(D // 2, D)}.get(d)
            carry = Carry("gather", [sh["w1"][1][rows]]) if rows else None
            (o, lse_d), w1b[d] = attn_fwd(qn, kn, v, d, f"attn_fwd_l{l}_d{d}", carry=carry)
            os_.append(o)
            lses.append(lse_d)
        w["w1"][1] = jnp.concatenate([w1b[1][0], w1b[16][0]], axis=1)
        (ycat, lse), got = combine_conv(os_, lses, proj, cws[l], f"combine_conv_l{l}",
                                        carry=Carry("gather", [nxt["w2"][0]]) if nxt else None)
        if nxt:
            weights[1]["w2"][0] = got[0].reshape(DFF, D)
        x2, y = out_proj(ycat, x1, vecs[1], w["wout"], f"out_proj_l{l}")
        (x3, a2, f2), got = ffn_fwd(x2, vecs[2], w["w1"][1], w["w2"][1], 0.5, f"ffn_fwd_l{l}b",
                                    carry=Carry("gather", [nxt["w1"][0]]) if nxt else None)
        if nxt:
            weights[1]["w1"][0] = got[0]
        saved.append(dict(vecs=vecs, x0=x0, a0=a0, f0=f0, x1=x1, proj=proj, h1b=h1b, qn=qn, kn=kn, v=v,
                          ycat=ycat, lse=lse, y=y, x2=x2, a2=a2, f2=f2))
        h = x3
    dx, loss_blk = loss_head(h, target, "loss_head")
    sums, totals, g_prev = [None, None], [None, None], None
    w2r = DFF // NCHIP
    for l in (1, 0):
        w, s = weights[l], saved[l]
        vecs = s["vecs"]
        ride = g_prev is not None
        own = l == 0
        mine, other = [None] * 6, [None] * 6

        def half_sum(group, recv, k0):
            return [add_half(g, r, cflag, f"add_sibling_l{l}_{k0 + j}") for j, (g, r) in enumerate(zip(group, recv))]

        def chip_sum(landed, k0):
            return [sum_chips(t, f"sum_chips_l{l}_{k0 + j}") for j, t in enumerate(landed)]

        (dx, hb, dfb, act, da, sums2), got = ffn_bwd(
            dx, s["x2"], s["a2"], s["f2"], vecs[2], w["w1"][1], w["w2"][1], 0.5, f"ffn_bwd_l{l}b",
            carry=Carry("swap_halves", g_prev) if ride else None)
        dw1b, _ = wgrad(hb, da, D, HALF, f"wgrad_w1_l{l}b")
        dw2b, _ = wgrad(act, dfb, HALF, D, f"wgrad_w2_l{l}b")
        if ride:
            wire = [add_half(g_prev[k], got[k], cflag, f"add_sibling_l{l + 1}_{k}") for k in range(6)]
        g_ffn_b = [dw1b, dw2b.reshape(NCHIP, w2r, D)]
        (dyb, dycat, delta, sums_o), got = out_proj_bwd(
            dx, s["y"], s["ycat"], vecs[1], w["wout"], f"out_proj_bwd_l{l}",
            carry=Carry("swap_halves", g_ffn_b) if own else None)
        dwout, _ = wgrad(s["ycat"].astype(MXU_DTYPE), dyb, D // 2, D, f"wgrad_wout_l{l}")
        if own:
            wire_ffn_b = half_sum(g_ffn_b, got, 4)
        dqs, dks, dvs, landed = [], [], [], {}
        for d in DILATIONS:
            carry = None
            if ride and d == 1:
                carry = Carry("scatter", wire[3:])
            if ride and d == 16:
                carry = Carry("scatter", wire[:3])
            if own and d == 4:
                carry = Carry("scatter", wire_ffn_b)
            (dq, dk, dv), landed[d] = attn_bwd(s["qn"], s["kn"], s["v"], dycat, s["lse"], delta, d,
                                               f"attn_bwd_l{l}_d{d}", carry=carry)
            dqs.append(dq)
            dks.append(dk)
            dvs.append(dv)
        if ride:
            tot = [sum_chips(t, f"sum_chips_l{l + 1}_{k}") for k, t in enumerate(list(landed[16]) + list(landed[1]))]
        if own:
            mine[4:6] = chip_sum(landed[4], 4)
        ready = (tot if ride else []) + (mine[4:6] if own else [])
        (dproj, sums_m), got = mixer_mid_bwd(dqs, dks, dvs, s["proj"], dycat, gvecs[l], cws[l], f"mixer_mid_bwd_l{l}",
                                             carry=Carry("swap", ready) if ready else None)
        if ride:
            totals[l + 1] = (tot, list(got[:6]))
        if own:
            other[4:6] = list(got[-2:])
        dwin, _ = wgrad(s["h1b"], dproj, D, INC // NCHIP, f"wgrad_win_l{l}")
        g_mixer = [dwin, dwout.reshape(NCHIP, D // NCHIP, D)]
        (dx, sums1), got = mixer_in_bwd(dx, s["x1"], dproj, vecs[1], w["win"], f"mixer_in_bwd_l{l}",
                                        carry=Carry("swap_halves", g_mixer) if own else None)
        if own:
            wire_mixer = half_sum(g_mixer, got, 2)
        (dx, hb, dfb, act, da, sums0), _ = ffn_bwd(
            dx, s["x0"], s["a0"], s["f0"], vecs[0], w["w1"][0], w["w2"][0], 0.5, f"ffn_bwd_l{l}a")
        dw1a, got = wgrad(hb, da, D, HALF, f"wgrad_w1_l{l}a", carry=Carry("scatter", wire_mixer) if own else None)
        if own:
            mine[2:4] = chip_sum(got, 2)
        dw2a, got = wgrad(act, dfb, HALF, D, f"wgrad_w2_l{l}a", carry=Carry("swap", mine[2:4]) if own else None)
        g_ffn_a = [dw1a, dw2a.reshape(NCHIP, w2r, D)]
        if own:
            other[2:4] = list(got)
            wire_ffn_a = half_sum(g_ffn_a, run_carry(Carry("swap_halves", g_ffn_a), "swap_halves_tail"), 0)
            mine[0:2] = chip_sum(run_carry(Carry("scatter", wire_ffn_a), "scatter_grads_tail"), 0)
            other[0:2] = list(run_carry(Carry("swap", mine[0:2]), "swap_totals_tail"))
            totals[l] = (mine, other)
        g_prev = g_ffn_a + g_mixer + g_ffn_b
        sums[l] = (sums0, sums1, sums_o, sums2, sums_m)
    return loss_blk, dx, totals, sums


def small_all_gather(blk, name):
    m_per, n = blk.shape

    def body(x_ref, out_ref, send_sems, recv_sems, local_sem):
        x, y, c = _here()
        me, sibling = (x, y, c), (x, y, 1 - c)
        chips = [(1 - x, y), (x, 1 - y), (1 - x, 1 - y)]

        def rows(px, py, pc):
            return out_ref.at[pl.ds((4 * px + 2 * py + pc) * m_per, m_per), :]

        def copy(k, block, to, src=None):
            return pltpu.make_async_remote_copy(
                src_ref=rows(*block) if src is None else src, dst_ref=rows(*block),
                send_sem=send_sems.at[k], recv_sem=recv_sems.at[k], device_id=to, device_id_type=MESH)

        mine = pltpu.make_async_copy(x_ref, rows(*me), local_sem)
        mine.start()
        first = [copy(0, me, sibling, src=x_ref)]
        first += [copy(1 + j, me, (*chip, c), src=x_ref) for j, chip in enumerate(chips)]
        for cp in first:
            cp.start()
        passed = [copy(4 + j, (*chip, c), sibling) for j, chip in enumerate(chips)]
        for j, chip in enumerate(chips):
            copy(1 + j, (*chip, c), me).wait_recv()
            passed[j].start()
        copy(0, sibling, me).wait_recv()
        for j, chip in enumerate(chips):
            copy(4 + j, (*chip, 1 - c), me).wait_recv()
        for cp in first + passed:
            cp.wait_send()
        mine.wait()

    return pl.pallas_call(
        body, name=name,
        out_shape=jax.ShapeDtypeStruct((NDEV * m_per, n), blk.dtype),
        in_specs=[pl.BlockSpec(memory_space=pltpu.VMEM)],
        out_specs=pl.BlockSpec(memory_space=pltpu.VMEM),
        scratch_shapes=[pltpu.SemaphoreType.DMA((7,)), pltpu.SemaphoreType.DMA((7,)), pltpu.SemaphoreType.DMA],
        compiler_params=pltpu.CompilerParams(vmem_limit_bytes=VMEM_LIMIT),
    )(blk)


EW_BLOCK_BYTES = 1 << 20


def _ew_rows(rows, cols):
    want = max(16, EW_BLOCK_BYTES // (4 * cols))
    best = None
    for t in range(16, rows + 1, 16):
        if rows % t == 0 and t <= want:
            best = t
    return best if best is not None else rows


def add_half(g, recv, cflag, name):
    pieces, r, cols = g.shape
    r2 = r // 2
    tr = _ew_rows(r2, cols)
    nt = r2 // tr

    def body(c_ref, g_ref, r_ref, o_ref):
        o_ref[...] = (g_ref[...] + r_ref[...]).astype(o_ref.dtype)

    half = pl.BlockSpec((None, tr, cols), lambda j, i, c_ref: (j, i, 0))
    return pl.pallas_call(
        body, name=name,
        grid_spec=pltpu.PrefetchScalarGridSpec(
            num_scalar_prefetch=1, grid=(pieces, nt),
            in_specs=[pl.BlockSpec((None, tr, cols), lambda j, i, c_ref: (j, c_ref[0] * nt + i, 0)), half],
            out_specs=half),
        out_shape=jax.ShapeDtypeStruct((pieces, r2, cols), WIRE_DTYPE),
        compiler_params=_params(("arbitrary", "arbitrary")),
    )(cflag, g, recv)


def sum_chips(recv, name):
    _, r, cols = recv.shape
    tr = _ew_rows(r, cols)

    def body(r_ref, o_ref):
        acc = r_ref[0].astype(F32)
        for k in range(1, NCHIP):
            acc = acc + r_ref[k].astype(F32)
        o_ref[...] = acc

    return pl.pallas_call(
        body, name=name, grid=(r // tr,),
        in_specs=[pl.BlockSpec((NCHIP, tr, cols), lambda i: (0, i, 0))],
        out_specs=pl.BlockSpec((tr, cols), lambda i: (i, 0)),
        out_shape=jax.ShapeDtypeStruct((r, cols), F32),
        compiler_params=_params(("arbitrary",)),
    )(recv)


def sum_devices(rows8, name):
    def body(r_ref, o_ref):
        acc = r_ref[0:1, :]
        for k in range(1, NDEV):
            acc = acc + r_ref[k:k + 1, :]
        o_ref[...] = jnp.broadcast_to(acc, o_ref.shape)

    return pl.pallas_call(
        body, name=name, out_shape=jax.ShapeDtypeStruct(rows8.shape, F32),
        in_specs=[pl.BlockSpec(memory_space=pltpu.VMEM)], out_specs=pl.BlockSpec(memory_space=pltpu.VMEM),
        compiler_params=pltpu.CompilerParams(vmem_limit_bytes=VMEM_LIMIT),
    )(rows8)


def adamw(w, m, v, srcs, cflag, name, halves=False):
    planes, r, cols = w.shape
    rh = r // 2 if halves else r
    tr = _ew_rows(rh, cols)
    nth = rh // tr
    flat = [a for s in srcs for a in (s if halves else (s,))]
    ns = len(flat)
    per = ns // planes

    def body(c_ref, w_ref, m_ref, v_ref, *rest):
        s_refs, (g_ref, d_ref, mo_ref, vo_ref) = rest[:ns], rest[ns:]
        p, i = pl.program_id(0), pl.program_id(1)
        if halves:
            mine = jnp.logical_not(jnp.logical_xor(i >= nth, c_ref[0] == 1))
            blocks = [jnp.where(mine, s_refs[2 * k][...], s_refs[2 * k + 1][...]) for k in range(planes)]
        else:
            blocks = [s[...] for s in s_refs]
        g = blocks[0]
        for k in range(1, planes):
            g = jnp.where(p == k, blocks[k], g)
        g_ref[...] = g
        m_new = ADAM_B1 * m_ref[...] + (1.0 - ADAM_B1) * g
        v_new = ADAM_B2 * v_ref[...] + (1.0 - ADAM_B2) * (g * g)
        mo_ref[...] = m_new
        vo_ref[...] = v_new
        m_hat = m_new / (1.0 - ADAM_B1 ** ADAM_STEP)
        v_hat = v_new / (1.0 - ADAM_B2 ** ADAM_STEP)
        d_ref[...] = -ADAM_LR * (m_hat / (jnp.sqrt(v_hat) + ADAM_EPS) + ADAM_WD * w_ref[...])

    pt = pl.BlockSpec((None, tr, cols), lambda p, i: (p, i, 0))
    st = [pl.BlockSpec((tr, cols), functools.partial(lambda k, p, i: (jnp.where(p == k, i % nth, 0), 0), j // per))
          for j in range(ns)]
    return pl.pallas_call(
        body, name=name, grid=(planes, r // tr),
        in_specs=[pl.BlockSpec(memory_space=pltpu.SMEM), pt, pt, pt] + st,
        out_specs=[pt] * 4,
        out_shape=[jax.ShapeDtypeStruct(w.shape, F32)] * 4,
        compiler_params=_params(("arbitrary", "arbitrary")),
    )(cflag, w, m, v, *flat)


ADA_COLS = 9 * D // NCHIP


def mod_fwd(c_all, w_ada, b_shard, name):
    def body(c_ref, w_ref, b_ref, o_ref):
        cc = c_ref[...]
        sc = cc * jax.nn.sigmoid(cc)
        o_ref[...] = jnp.dot(sc, w_ref[...], preferred_element_type=F32,
                             precision=lax.Precision.HIGHEST) + b_ref[...]

    return pl.pallas_call(
        body, name=name, grid=(2,),
        in_specs=[pl.BlockSpec((NDEV, D), lambda l: (0, 0)),
                  pl.BlockSpec((None, D, ADA_COLS), lambda l: (l, 0, 0)),
                  pl.BlockSpec((None, 1, ADA_COLS), lambda l: (l, 0, 0))],
        out_specs=pl.BlockSpec((None, NDEV, ADA_COLS), lambda l: (l, 0, 0)),
        out_shape=jax.ShapeDtypeStruct((2, NDEV, ADA_COLS), F32),
        compiler_params=_params(("arbitrary",)),
    )(c_all, w_ada, b_shard.reshape(2, 1, ADA_COLS))


def wada_grad(c_all_t, dmod, name):
    ct = ADA_COLS // 3

    def body(c_ref, d_ref, o_ref):
        cc = c_ref[...]
        sc = cc * jax.nn.sigmoid(cc)
        acc = sc[:, 0:1] * d_ref[0:1, :]
        for b in range(1, NDEV):
            acc = acc + sc[:, b:b + 1] * d_ref[b:b + 1, :]
        o_ref[...] = acc

    return pl.pallas_call(
        body, name=name, grid=(2, 3),
        in_specs=[pl.BlockSpec((D, LANES), lambda l, j: (0, 0)),
                  pl.BlockSpec((None, NDEV, ct), lambda l, j: (l, 0, j))],
        out_specs=pl.BlockSpec((None, D, ct), lambda l, j: (l, 0, j)),
        out_shape=jax.ShapeDtypeStruct((2, D, ADA_COLS), F32),
        compiler_params=_params(("arbitrary", "arbitrary")),
    )(c_all_t, dmod)


def _pad_rows(row, rows=SUBLANES):
    return jnp.concatenate([row[None, :], jnp.zeros((rows - 1, row.shape[0]), row.dtype)], axis=0)


def kernel(x, c, w_ada, b_ada, norm_g, w_in, q_norm_g, k_norm_g, conv_w, conv_b, w_out, ffn_w1, ffn_w2, loss_target, m_w_ada, m_b_ada, m_norm_g, m_w_in, m_q_norm_g, m_k_norm_g, m_conv_w, m_conv_b, m_w_out, m_ffn_w1, m_ffn_w2, v_w_ada, v_b_ada, v_norm_g, v_w_in, v_q_norm_g, v_k_norm_g, v_conv_w, v_conv_b, v_w_out, v_ffn_w1, v_ffn_w2):
    ix, iy, ic = lax.axis_index("x"), lax.axis_index("y"), lax.axis_index("c")
    chip = 2 * ix + iy
    dev = 2 * chip + ic
    cflag = jnp.reshape(ic, (1,)).astype(jnp.int32)
    ngw = norm_g.shape[-1]
    cww = conv_w.shape[-1]

    pack = jnp.concatenate([c[0], norm_g.reshape(-1), conv_w.reshape(-1)])
    got = small_all_gather(_pad_rows(pack), "gather_c_normg_convw")[::SUBLANES]
    c_all = got[:, :D]
    per_chip = got[::2]
    ng_full = jnp.concatenate([per_chip[j, D:D + 6 * ngw].reshape(2, 3, ngw) for j in range(NCHIP)], axis=-1)
    cw_full = jnp.concatenate([per_chip[j, D + 6 * ngw:].reshape(2, 3, cww) for j in range(NCHIP)], axis=-1)

    b_shard = lax.dynamic_slice_in_dim(b_ada, chip * ADA_COLS, ADA_COLS, axis=1)
    mod_blk = mod_fwd(c_all, w_ada, b_shard, "mod_fwd").reshape(2 * NDEV, ADA_COLS)
    mod_all = small_all_gather(mod_blk, "gather_mod").reshape(NDEV, 2, NDEV, ADA_COLS)[::2]
    mod_mine = lax.dynamic_index_in_dim(mod_all, dev, axis=2, keepdims=False)
    mods = [mod_mine[:, l, :].reshape(-1) for l in range(2)]

    shards, gvecs, cws = [], [], []
    for l in range(2):
        shards.append(dict(w1=[ffn_w1[l, i].astype(MXU_DTYPE) for i in range(2)],
                           w2=[ffn_w2[l, i].astype(MXU_DTYPE) for i in range(2)],
                           win=w_in[l].astype(MXU_DTYPE), wout=w_out[l].astype(MXU_DTYPE)))
        gv = jnp.stack([jnp.tile(q_norm_g[l], AW // HD), jnp.tile(k_norm_g[l], AW // HD)])
        gvecs.append(jnp.concatenate([gv, jnp.zeros((SUBLANES - 2, AW), F32)], axis=0))
        cws.append(jnp.concatenate([cw_full[l], conv_b[l][None, :], jnp.zeros((SUBLANES - 4, CW), F32)], axis=0))
    w_first = gather_split([shards[0]["w1"][0], shards[0]["w2"][0]], "gather_first_ffn")

    loss_blk, dx, totals, sums = local_step(x[0], loss_target[0], mods, [ng_full[0], ng_full[1]], gvecs, cws,
                                            shards, w_first, cflag)

    dmods, dngs, dqg, dkg, dcw, dcb = [], [], [], [], [], []
    for l in range(2):
        s0, s1, so, s2, sm = sums[l]
        dmods.append(jnp.concatenate([s0[0], s0[1], s0[3], s1[0], s1[1], so[0], s2[0], s2[1], s2[3]]))
        dngs.append(jnp.concatenate([s0[2], s1[2], s2[2]]))
        dqg.append(sm[0].reshape(AW // HD, HD).sum(0))
        dkg.append(sm[1].reshape(AW // HD, HD).sum(0))
        dcw.append(sm[2:5].reshape(-1))
        dcb.append(sm[5])
    small = jnp.concatenate(dmods + dngs + dqg + dkg + dcw + dcb + [loss_blk[0]])
    small_all = small_all_gather(_pad_rows(small), "gather_small_grads")[::SUBLANES]
    nm = 9 * D
    dmod_all = small_all[:, :2 * nm].reshape(NDEV, 2, NCHIP, ADA_COLS)
    dmod_mine = lax.dynamic_index_in_dim(dmod_all, chip, axis=2, keepdims=False).transpose(1, 0, 2)
    tot = sum_devices(small_all, "sum_small_grads")[0]
    o = 2 * nm
    g_b_ada = tot[:o].reshape(2, nm)
    g_norm_g = lax.dynamic_slice_in_dim(tot[o:o + 6 * D].reshape(2, 3, D), chip * ngw, ngw, axis=2)
    o += 6 * D
    g_qg = tot[o:o + 2 * HD].reshape(2, HD)
    o += 2 * HD
    g_kg = tot[o:o + 2 * HD].reshape(2, HD)
    o += 2 * HD
    g_cw = lax.dynamic_slice_in_dim(tot[o:o + 6 * CW].reshape(2, 3, CW), chip * cww, cww, axis=2)
    o += 6 * CW
    g_cb = tot[o:o + 2 * CW].reshape(2, CW)
    loss = tot[o + 2 * CW]

    c_all_t = jnp.concatenate([c_all.T, jnp.zeros((D, LANES - NDEV), F32)], axis=1)
    g_wada_src = wada_grad(c_all_t, dmod_mine, "wada_grad")

    def halves(k_of_plane):
        return [(totals[l][0][k], totals[l][1][k]) for l, k in k_of_plane]

    r_wada = adamw(w_ada, m_w_ada, v_w_ada, [g_wada_src[0], g_wada_src[1]], cflag, "adamw_w_ada")
    r_win = adamw(w_in, m_w_in, v_w_in, halves([(0, 2), (1, 2)]), cflag, "adamw_w_in", halves=True)
    r_wout = adamw(w_out, m_w_out, v_w_out, halves([(0, 3), (1, 3)]), cflag, "adamw_w_out", halves=True)
    r_w1 = adamw(ffn_w1.reshape(4, D, HALF), m_ffn_w1.reshape(4, D, HALF), v_ffn_w1.reshape(4, D, HALF),
                 halves([(0, 0), (0, 4), (1, 0), (1, 4)]), cflag, "adamw_ffn_w1", halves=True)
    w2r = DFF // NCHIP
    r_w2 = adamw(ffn_w2.reshape(4, w2r, D), m_ffn_w2.reshape(4, w2r, D), v_ffn_w2.reshape(4, w2r, D),
                 halves([(0, 1), (0, 5), (1, 1), (1, 5)]), cflag, "adamw_ffn_w2", halves=True)
    r_w1 = [t.reshape(ffn_w1.shape) for t in r_w1]
    r_w2 = [t.reshape(ffn_w2.shape) for t in r_w2]

    smalls = [("b_ada", b_ada, m_b_ada, v_b_ada, g_b_ada), ("norm_g", norm_g, m_norm_g, v_norm_g, g_norm_g),
              ("q_norm_g", q_norm_g, m_q_norm_g, v_q_norm_g, g_qg), ("k_norm_g", k_norm_g, m_k_norm_g, v_k_norm_g, g_kg),
              ("conv_w", conv_w, m_conv_w, v_conv_w, g_cw), ("conv_b", conv_b, m_conv_b, v_conv_b, g_cb)]
    n_small = sum(t[1].size for t in smalls)
    pad = (-n_small) % (16 * LANES)

    def packed(idx):
        flat = jnp.concatenate([t[idx].reshape(-1) for t in smalls] + [jnp.zeros((pad,), F32)])
        return flat.reshape(-1, LANES)

    r_small = adamw(packed(1)[None], packed(2)[None], packed(3)[None], [packed(4)], cflag, "adamw_small")
    small_out = {}
    o = 0
    for name_, w_, _, _, _ in smalls:
        small_out[name_] = [t.reshape(-1)[o:o + w_.size].reshape(w_.shape) for t in r_small]
        o += w_.size

    res = {"w_ada": r_wada, "w_in": r_win, "w_out": r_wout, "ffn_w1": r_w1, "ffn_w2": r_w2, **small_out}
    order = ["w_ada", "b_ada", "norm_g", "w_in", "q_norm_g", "k_norm_g", "conv_w", "conv_b", "w_out", "ffn_w1", "ffn_w2"]
    outs = [loss, dx[None]]
    for k in range(4):
        outs += [res[nm_][k] for nm_ in order]
    return tuple(outs)
```

```python
import functools

import jax
import jax.numpy as jnp
from jax import lax
from jax.experimental import pallas as pl
from jax.experimental.pallas import tpu as pltpu

F32 = jnp.float32
MXU_DTYPE = jnp.bfloat16
ACT_DTYPE = jnp.bfloat16
WIRE_DTYPE = jnp.bfloat16

D = 1024
HD = 64
AW = 512
CW = 512
DFF = 2816
HALF = DFF // 2
INC = 3 * AW + 3 * CW
NCHIP = 4
NDEV = 8
QBLK = 128
ATTN_QBLOCKS = 8
ATTN_INTERLEAVE = 8
ATTN_CHUNK_ROWS = 4096
DILATIONS = (1, 4, 16)
EPS = 1e-6
NEG = -1e30
LANES = 128
SUBLANES = 8
VMEM_LIMIT = 56 * 1024 * 1024

ADAM_LR = 0.001
ADAM_B1 = 0.9
ADAM_B2 = 0.999
ADAM_EPS = 1e-08
ADAM_WD = 0.01
ADAM_STEP = 10

NT_DIMS = (((1,), (1,)), ((), ()))
TN_DIMS = (((0,), (0,)), ((), ()))


def _params(sem, vmem=VMEM_LIMIT):
    return pltpu.CompilerParams(dimension_semantics=sem, vmem_limit_bytes=vmem)


def _row_tile(n, want):
    t = min(n, want)
    assert n % t == 0
    return t


def _ada(xt, vec_ref):
    ng, sc, sh, gt = vec_ref[0:1, :], vec_ref[1:2, :], vec_ref[2:3, :], vec_ref[3:4, :]
    r = lax.rsqrt(jnp.mean(xt * xt, axis=-1, keepdims=True) + EPS)
    return xt * r, r, ng * (1.0 + sc), ng, sc, sh, gt


def _ada_bwd(dh, xhat, r, gain, ng, sc):
    dshift = jnp.sum(dh, axis=0, keepdims=True)
    dhx = dh * xhat
    dscale = jnp.sum(dhx, axis=0, keepdims=True) * ng
    dng = jnp.sum(dhx, axis=0, keepdims=True) * (1.0 + sc)
    dxhat = dh * gain
    dx = r * (dxhat - xhat * jnp.mean(dxhat * xhat, axis=-1, keepdims=True))
    return dx, dshift, dscale, dng


def _acc_rows(sums_ref, first, rows):
    @pl.when(first)
    def _():
        sums_ref[...] = jnp.zeros_like(sums_ref)
    for k, row in enumerate(rows):
        sums_ref[k:k + 1, :] += row


MESH = pl.DeviceIdType.MESH
ANY = pl.BlockSpec(memory_space=pl.ANY)


def _here():
    return lax.axis_index("x"), lax.axis_index("y"), lax.axis_index("c")


def _ici_copies(src_refs, dst_refs, send_sems, recv_sems, local_sems, scatter):
    x, y, c = _here()
    my_chip = 2 * x + y
    peers = [(1 - x, y), (x, 1 - y), (1 - x, 1 - y)]
    local, out, inc = [], [], []
    for a, (src, dst) in enumerate(zip(src_refs, dst_refs)):
        local.append(pltpu.make_async_copy(src.at[my_chip] if scatter else src, dst.at[my_chip], local_sems.at[a]))
        for j, (px, py) in enumerate(peers):
            sems = dict(send_sem=send_sems.at[3 * a + j], recv_sem=recv_sems.at[3 * a + j],
                        device_id=(px, py, c), device_id_type=MESH)
            out.append(pltpu.make_async_remote_copy(
                src_ref=src.at[2 * px + py] if scatter else src, dst_ref=dst.at[my_chip], **sems))
            inc.append(pltpu.make_async_remote_copy(
                src_ref=src.at[my_chip] if scatter else src, dst_ref=dst.at[2 * px + py], **sems))
    return local, out, inc


def _swap_copies(src_refs, dst_refs, send_sems, recv_sems, halves):
    x, y, c = _here()
    cps = []
    for k, (src, dst) in enumerate(zip(src_refs, dst_refs)):
        if halves:
            r2 = src.shape[1] // 2
            src = src.at[:, pl.ds((1 - c) * r2, r2), :]
        cps.append(pltpu.make_async_remote_copy(
            src_ref=src, dst_ref=dst, send_sem=send_sems.at[k], recv_sem=recv_sems.at[k],
            device_id=(x, y, 1 - c), device_id_type=MESH))
    return cps


class Carry:
    def __init__(self, kind, srcs):
        self.kind, self.srcs, n = kind, list(srcs), len(srcs)
        if kind == "gather":
            shapes = [(NCHIP,) + s.shape for s in srcs]
        elif kind == "swap_halves":
            shapes = [(s.shape[0], s.shape[1] // 2, s.shape[2]) for s in srcs]
        else:
            shapes = [s.shape for s in srcs]
        self.out_shape = [jax.ShapeDtypeStruct(sh, s.dtype) for sh, s in zip(shapes, srcs)]
        dma = pltpu.SemaphoreType.DMA
        self.sems = [dma((3 * n,)), dma((3 * n,)), dma((n,))] if kind in ("gather", "scatter") else [dma((n,)), dma((n,))]

    def start(self, srcs, dsts, sems):
        if self.kind in ("gather", "scatter"):
            local, out, _ = _ici_copies(srcs, dsts, *sems, self.kind == "scatter")
            for cp in local + out:
                cp.start()
        else:
            for cp in _swap_copies(srcs, dsts, *sems, self.kind == "swap_halves"):
                cp.start()

    def wait(self, srcs, dsts, sems):
        if self.kind in ("gather", "scatter"):
            local, out, inc = _ici_copies(srcs, dsts, *sems, self.kind == "scatter")
            for cp in inc:
                cp.wait_recv()
            for cp in out:
                cp.wait_send()
            for cp in local:
                cp.wait()
        else:
            cps = _swap_copies(srcs, dsts, *sems, self.kind == "swap_halves")
            for cp in cps:
                cp.wait_recv()
            for cp in cps:
                cp.wait_send()


def run_carry(carry, name):
    n = len(carry.srcs)

    def body(*refs):
        srcs, dsts, sems = refs[:n], refs[n:2 * n], refs[2 * n:]
        carry.start(srcs, dsts, sems)
        carry.wait(srcs, dsts, sems)

    return pl.pallas_call(body, name=name, out_shape=carry.out_shape, in_specs=[ANY] * n, out_specs=[ANY] * n,
                          scratch_shapes=carry.sems)(*carry.srcs)


def gather_split(srcs, name):
    n = len(srcs)

    def body(*refs):
        src_refs, dst_refs = refs[:n], refs[n:2 * n]
        send_sems, recv_sems, fwd_send, fwd_recv, local_sems = refs[2 * n:]
        x, y, c = _here()
        my_chip = 2 * x + y
        peers = [(1 - x, y), (x, 1 - y), (1 - x, 1 - y)]

        def half(ref, h):
            r2 = ref.shape[0] // 2
            return ref.at[pl.ds(h * r2, r2), :]

        local, out, landed, passed, arriving = [], [], [], [], []
        for a, (src, dst) in enumerate(zip(src_refs, dst_refs)):
            local.append(pltpu.make_async_copy(src, dst.at[my_chip], local_sems.at[a]))
            for j, (px, py) in enumerate(peers):
                k = 3 * a + j
                theirs = dst.at[2 * px + py]
                ici = dict(send_sem=send_sems.at[k], recv_sem=recv_sems.at[k], device_id=(px, py, c), device_id_type=MESH)
                d2d = dict(send_sem=fwd_send.at[k], recv_sem=fwd_recv.at[k], device_id=(x, y, 1 - c), device_id_type=MESH)
                out.append(pltpu.make_async_remote_copy(src_ref=half(src, c), dst_ref=half(dst.at[my_chip], c), **ici))
                landed.append(pltpu.make_async_remote_copy(src_ref=half(src, c), dst_ref=half(theirs, c), **ici))
                passed.append(pltpu.make_async_remote_copy(src_ref=half(theirs, c), dst_ref=half(theirs, c), **d2d))
                arriving.append(pltpu.make_async_remote_copy(src_ref=half(theirs, c), dst_ref=half(theirs, 1 - c), **d2d))
        for cp in local + out:
            cp.start()
        for got, fwd in zip(landed, passed):
            got.wait_recv()
            fwd.start()
        for cp in arriving:
            cp.wait_recv()
        for cp in out + passed:
            cp.wait_send()
        for cp in local:
            cp.wait()

    dma = pltpu.SemaphoreType.DMA
    return pl.pallas_call(
        body, name=name, out_shape=[jax.ShapeDtypeStruct((NCHIP,) + s.shape, s.dtype) for s in srcs],
        in_specs=[ANY] * n, out_specs=[ANY] * n,
        scratch_shapes=[dma((3 * n,)), dma((3 * n,)), dma((3 * n,)), dma((3 * n,)), dma((n,))],
    )(*srcs)


def _pcall(body, name, grid, in_specs, out_specs, out_shape, sem, args, carry=None, scratch=()):
    if carry is None:
        outs = pl.pallas_call(body, name=name, grid=grid, in_specs=in_specs, out_specs=out_specs,
                              out_shape=out_shape, scratch_shapes=list(scratch), compiler_params=_params(sem))(*args)
        return outs, []
    n_in, n_out, nc, ns = len(in_specs), len(out_specs), len(carry.srcs), len(scratch)

    def wrapped(*refs):
        ins, csrc = refs[:n_in], refs[n_in:n_in + nc]
        outs, cdst = refs[n_in + nc:n_in + nc + n_out], refs[n_in + nc + n_out:n_in + 2 * nc + n_out]
        own = refs[n_in + 2 * nc + n_out:n_in + 2 * nc + n_out + ns]
        sems = refs[n_in + 2 * nc + n_out + ns:]
        ids = [pl.program_id(a) for a in range(len(grid))]
        first = functools.reduce(jnp.logical_and, [i == 0 for i in ids])
        last = functools.reduce(jnp.logical_and, [i == g - 1 for i, g in zip(ids, grid)])

        @pl.when(first)
        def _():
            carry.start(csrc, cdst, sems)

        body(*ins, *outs, *own)

        @pl.when(last)
        def _():
            carry.wait(csrc, cdst, sems)

    res = pl.pallas_call(
        wrapped, name=name, grid=grid,
        in_specs=list(in_specs) + [ANY] * nc, out_specs=list(out_specs) + [ANY] * nc,
        out_shape=list(out_shape) + carry.out_shape,
        scratch_shapes=list(scratch) + carry.sems, compiler_params=_params(sem),
    )(*args, *carry.srcs)
    return res[:n_out], res[n_out:]


def ffn_fwd(x, vec, w1p, w2, gs, name, carry=None):
    S = x.shape[0]
    tm = _row_tile(S, 512)

    def body(x_ref, vec_ref, w1_ref, w2_ref, xn_ref, a_ref, f_ref):
        xt = x_ref[...]
        xhat, _, gain, _, _, sh, gt = _ada(xt, vec_ref)
        h = (xhat * gain + sh).astype(MXU_DTYPE)
        f = jnp.zeros((tm, D), F32)
        for hf in range(2):
            g = jnp.dot(h, w1_ref[hf], preferred_element_type=F32)
            up = jnp.dot(h, w1_ref[2 + hf], preferred_element_type=F32)
            a_ref[:, hf * HALF:(hf + 1) * HALF] = g.astype(a_ref.dtype)
            a_ref[:, DFF + hf * HALF:DFF + (hf + 1) * HALF] = up.astype(a_ref.dtype)
            act = (g * jax.nn.sigmoid(g) * up).astype(MXU_DTYPE)
            f = f + jnp.dot(act, w2_ref[hf * HALF:(hf + 1) * HALF, :], preferred_element_type=F32)
        xn_ref[...] = xt + (gs * gt) * f
        f_ref[...] = f.astype(f_ref.dtype)

    return _pcall(
        body, name, (S // tm,),
        [pl.BlockSpec((tm, D), lambda i: (i, 0)),
         pl.BlockSpec((SUBLANES, D), lambda i: (0, 0)),
         pl.BlockSpec((NCHIP, D, HALF), lambda i: (0, 0, 0), pipeline_mode=pl.Buffered(1)),
         pl.BlockSpec((DFF, D), lambda i: (0, 0), pipeline_mode=pl.Buffered(1))],
        [pl.BlockSpec((tm, D), lambda i: (i, 0)),
         pl.BlockSpec((tm, 2 * DFF), lambda i: (i, 0)),
         pl.BlockSpec((tm, D), lambda i: (i, 0))],
        [jax.ShapeDtypeStruct((S, D), F32),
         jax.ShapeDtypeStruct((S, 2 * DFF), ACT_DTYPE),
         jax.ShapeDtypeStruct((S, D), ACT_DTYPE)],
        ("arbitrary",), (x, vec, w1p, w2), carry)


def ffn_bwd(dxo, x, a, f, vec, w1p, w2, gs, name, carry=None):
    S = x.shape[0]
    tm = _row_tile(S, 256)

    def body(dxo_ref, x_ref, a_ref, f_ref, vec_ref, w1_ref, w2_ref,
             dxi_ref, hb_ref, dfb_ref, act_ref, da_ref, sums_ref):
        xt = x_ref[...]
        dxo = dxo_ref[...]
        xhat, r, gain, ng, sc, sh, gt = _ada(xt, vec_ref)
        hb_ref[...] = (xhat * gain + sh).astype(hb_ref.dtype)
        dgate = gs * jnp.sum(dxo * f_ref[...].astype(F32), axis=0, keepdims=True)
        df = ((gs * gt) * dxo).astype(MXU_DTYPE)
        dfb_ref[...] = df
        dh = jnp.zeros((tm, D), F32)
        for hf in range(2):
            lo, hi = hf * HALF, (hf + 1) * HALF
            dact = lax.dot_general(df, w2_ref[lo:hi, :], NT_DIMS, preferred_element_type=F32)
            g = a_ref[:, lo:hi].astype(F32)
            up = a_ref[:, DFF + lo:DFF + hi].astype(F32)
            sg = jax.nn.sigmoid(g)
            si = g * sg
            act_ref[:, lo:hi] = (si * up).astype(act_ref.dtype)
            dg = (dact * up * (sg * (1.0 + g * (1.0 - sg)))).astype(MXU_DTYPE)
            dup = (dact * si).astype(MXU_DTYPE)
            da_ref[:, lo:hi] = dg
            da_ref[:, DFF + lo:DFF + hi] = dup
            dh = dh + lax.dot_general(dg, w1_ref[hf], NT_DIMS, preferred_element_type=F32)
            dh = dh + lax.dot_general(dup, w1_ref[2 + hf], NT_DIMS, preferred_element_type=F32)
        dx, dshift, dscale, dng = _ada_bwd(dh, xhat, r, gain, ng, sc)
        dxi_ref[...] = dxo + dx
        _acc_rows(sums_ref, pl.program_id(0) == 0, (dshift, dscale, dng, dgate))

    return _pcall(
        body, name, (S // tm,),
        [pl.BlockSpec((tm, D), lambda i: (i, 0)),
         pl.BlockSpec((tm, D), lambda i: (i, 0)),
         pl.BlockSpec((tm, 2 * DFF), lambda i: (i, 0)),
         pl.BlockSpec((tm, D), lambda i: (i, 0)),
         pl.BlockSpec((SUBLANES, D), lambda i: (0, 0)),
         pl.BlockSpec((NCHIP, D, HALF), lambda i: (0, 0, 0), pipeline_mode=pl.Buffered(1)),
         pl.BlockSpec((DFF, D), lambda i: (0, 0), pipeline_mode=pl.Buffered(1))],
        [pl.BlockSpec((tm, D), lambda i: (i, 0)),
         pl.BlockSpec((tm, D), lambda i: (i, 0)),
         pl.BlockSpec((tm, D), lambda i: (i, 0)),
         pl.BlockSpec((tm, DFF), lambda i: (i, 0)),
         pl.BlockSpec((tm, 2 * DFF), lambda i: (i, 0)),
         pl.BlockSpec((SUBLANES, D), lambda i: (0, 0))],
        [jax.ShapeDtypeStruct((S, D), F32),
         jax.ShapeDtypeStruct((S, D), MXU_DTYPE),
         jax.ShapeDtypeStruct((S, D), MXU_DTYPE),
         jax.ShapeDtypeStruct((S, DFF), MXU_DTYPE),
         jax.ShapeDtypeStruct((S, 2 * DFF), MXU_DTYPE),
         jax.ShapeDtypeStruct((SUBLANES, D), F32)],
        ("arbitrary",), (dxo, x, a, f, vec, w1p, w2), carry)


def wgrad(a, b, kt, nt, name, carry=None):
    T, K = a.shape
    N = b.shape[1]
    pk, pn = K // kt, N // nt
    assert pk == 1 or pn == 1
    tt = _row_tile(T, 2048)
    steps = T // tt

    def body(a_ref, b_ref, o_ref):
        @pl.when(pl.program_id(1) == 0)
        def _():
            o_ref[...] = jnp.zeros_like(o_ref)
        o_ref[...] += lax.dot_general(a_ref[...], b_ref[...], TN_DIMS, preferred_element_type=F32)

    a_map = (lambda p, t: (t, p)) if pk > 1 else (lambda p, t: (t, 0))
    b_map = (lambda p, t: (t, p)) if pn > 1 else (lambda p, t: (t, 0))
    (out,), got = _pcall(
        body, name, (pk * pn, steps),
        [pl.BlockSpec((tt, kt), a_map), pl.BlockSpec((tt, nt), b_map)],
        [pl.BlockSpec((None, kt, nt), lambda p, t: (p, 0, 0))],
        [jax.ShapeDtypeStruct((pk * pn, kt, nt), F32)], ("arbitrary", "arbitrary"), (a, b), carry)
    return out, got


def _head_masks(rows):
    lane = lax.broadcasted_iota(jnp.int32, (rows, LANES), 1)
    return lane < HD


def _pair_stat(x, m_a):
    s_a = jnp.sum(jnp.where(m_a, x, 0.0), axis=1, keepdims=True)
    s_b = jnp.sum(jnp.where(m_a, 0.0, x), axis=1, keepdims=True)
    return s_a, s_b


def mixer_in(x, vec, winp, gvec, name):
    S = x.shape[0]
    tm = _row_tile(S, 512)
    pc = INC // NCHIP

    def body(x_ref, vec_ref, w_ref, g_ref, proj_ref, hb_ref, qn_ref, kn_ref, v_ref):
        xt = x_ref[...]
        xhat, _, gain, _, _, sh, _ = _ada(xt, vec_ref)
        h = (xhat * gain + sh).astype(MXU_DTYPE)
        hb_ref[...] = h
        for j in range(NCHIP):
            proj_ref[:, j * pc:(j + 1) * pc] = jnp.dot(h, w_ref[j], preferred_element_type=F32)
        m_a = _head_masks(tm)
        for which, dst in ((0, qn_ref), (1, kn_ref)):
            for p in range(AW // LANES):
                lo = which * AW + p * LANES
                xp = proj_ref[:, lo:lo + LANES]
                s_a, s_b = _pair_stat(xp * xp, m_a)
                rr = jnp.where(m_a, lax.rsqrt(s_a * (1.0 / HD) + EPS), lax.rsqrt(s_b * (1.0 / HD) + EPS))
                gp = g_ref[which:which + 1, p * LANES:(p + 1) * LANES]
                dst[:, p * LANES:(p + 1) * LANES] = (xp * rr * gp).astype(dst.dtype)
        v_ref[...] = proj_ref[:, 2 * AW:3 * AW].astype(v_ref.dtype)

    return pl.pallas_call(
        body, name=name, grid=(S // tm,),
        in_specs=[pl.BlockSpec((tm, D), lambda i: (i, 0)),
                  pl.BlockSpec((SUBLANES, D), lambda i: (0, 0)),
                  pl.BlockSpec((NCHIP, D, pc), lambda i: (0, 0, 0), pipeline_mode=pl.Buffered(1)),
                  pl.BlockSpec((SUBLANES, AW), lambda i: (0, 0))],
        out_specs=[pl.BlockSpec((tm, INC), lambda i: (i, 0)),
                   pl.BlockSpec((tm, D), lambda i: (i, 0)),
                   pl.BlockSpec((tm, AW), lambda i: (i, 0)),
                   pl.BlockSpec((tm, AW), lambda i: (i, 0)),
                   pl.BlockSpec((tm, AW), lambda i: (i, 0))],
        out_shape=[jax.ShapeDtypeStruct((S, INC), F32),
                   jax.ShapeDtypeStruct((S, D), MXU_DTYPE),
                   jax.ShapeDtypeStruct((S, AW), F32),
                   jax.ShapeDtypeStruct((S, AW), F32),
                   jax.ShapeDtypeStruct((S, AW), F32)],
        compiler_params=_params(("arbitrary",)),
    )(x, vec, winp, gvec)


def _band_masks(ncol):
    row = lax.broadcasted_iota(jnp.int32, (2 * QBLK, ncol), 0) & (QBLK - 1)
    col = lax.broadcasted_iota(jnp.int32, (2 * QBLK, ncol), 1)
    return row, col


def _stack_heads(t, m_a):
    zero = jnp.zeros_like(t)
    return jnp.concatenate([jnp.where(m_a, t, zero), jnp.where(m_a, zero, t)], axis=0)


class _AttnLayout:
    def __init__(self, d, S):
        self.d, self.S = d, S
        self.qb = max(1, min(ATTN_QBLOCKS, ATTN_CHUNK_ROWS // (QBLK * d)))
        self.nres = d
        self.nchunk = S // (self.qb * QBLK * d)
        self.grid = (AW // LANES, self.nchunk)
        self.unroll = max(1, min(d, ATTN_INTERLEAVE // self.qb))

    def _spec(self, blocks, row_of):
        return pl.BlockSpec((blocks * QBLK * self.d, LANES), lambda hp, j: (row_of(j), hp))

    def cur(self, chunk_of):
        return self._spec(self.qb, chunk_of)

    def prev(self, chunk_of):
        return self._spec(1, lambda j: jnp.maximum(chunk_of(j) * self.qb - 1, 0))

    def idx(self, b, r):
        if self.d == 1:
            return (pl.ds(b * QBLK, QBLK), slice(None))
        return (pl.ds(b * QBLK * self.d + r, QBLK, stride=self.d), slice(None))

    def per_residue(self, fn):
        if self.nres == 1:
            fn(0)
        else:
            def step(it, carry):
                for k in range(self.unroll):
                    fn(it * self.unroll + k)
                return carry
            lax.fori_loop(0, self.nres // self.unroll, step, 0)


def attn_fwd(qn, kn, v, d, name, carry=None):
    S = qn.shape[0]
    lay = _AttnLayout(d, S)
    qb = lay.qb

    def body(q_ref, kc_ref, kp_ref, vc_ref, vp_ref, o_ref, lse_ref):
        i = pl.program_id(1)
        m_a = _head_masks(QBLK)
        row, col = _band_masks(2 * QBLK)
        dist = row + QBLK - col
        band = (dist >= 0) & (dist <= QBLK)
        first = band & ((i > 0) | (col >= QBLK))

        def residue(r):
            kt = [kp_ref[lay.idx(0, r)].astype(MXU_DTYPE)]
            vt = [vp_ref[lay.idx(0, r)].astype(MXU_DTYPE)]
            for b in range(qb):
                kt.append(kc_ref[lay.idx(b, r)].astype(MXU_DTYPE))
                vt.append(vc_ref[lay.idx(b, r)].astype(MXU_DTYPE))
            for b in range(qb):
                rows = lay.idx(b, r)
                q = (q_ref[rows] * (HD ** -0.5)).astype(MXU_DTYPE)
                kcat = jnp.concatenate([kt[b], kt[b + 1]], axis=0)
                vcat = jnp.concatenate([vt[b], vt[b + 1]], axis=0)
                mask = first if b == 0 else band
                s = lax.dot_general(_stack_heads(q, m_a), kcat, NT_DIMS, preferred_element_type=F32)
                s = jnp.where(mask, s, NEG)
                m = jnp.max(s, axis=1, keepdims=True)
                p = jnp.exp(s - m)
                l = jnp.sum(p, axis=1, keepdims=True)
                o = jnp.dot(p.astype(MXU_DTYPE), vcat, preferred_element_type=F32) / l
                lse = jnp.broadcast_to(m + jnp.log(l), (2 * QBLK, LANES))
                o_ref[rows] = jnp.where(m_a, o[:QBLK], o[QBLK:])
                lse_ref[rows] = jnp.where(m_a, lse[:QBLK], lse[QBLK:])

        lay.per_residue(residue)

    cur, prev = lay.cur(lambda j: j), lay.prev(lambda j: j)
    return _pcall(body, name, lay.grid, [cur, cur, prev, cur, prev], [cur, cur],
                  [jax.ShapeDtypeStruct((S, AW), F32)] * 2, ("arbitrary", "arbitrary"), (qn, kn, kn, v, v), carry)


def _both_heads(t, m_a):
    other = pltpu.roll(t, HD, 1)
    return jnp.concatenate([jnp.where(m_a, t, other), jnp.where(m_a, other, t)], axis=0)


def attn_bwd(qn, kn, v, dycat, lse, delta, d, name, carry=None):
    S = qn.shape[0]
    lay = _AttnLayout(d, S)
    qb, nchunk = lay.qb, lay.nchunk

    def body(q_ref, kc_ref, kp_ref, vc_ref, vp_ref, do_ref, lse_ref, dl_ref,
             dq_ref, dk_ref, dv_ref, ck_ref, cv_ref):
        j = pl.program_id(1)
        i = nchunk - 1 - j
        m_a = _head_masks(QBLK)
        row, col = _band_masks(2 * QBLK)
        dist = row + QBLK - col
        band = (dist >= 0) & (dist <= QBLK)
        first = band & ((i > 0) | (col >= QBLK))

        def residue(r):
            def tiles(ref, cast):
                out = [ref[lay.idx(b, r)] for b in range(qb)]
                return [t.astype(MXU_DTYPE) for t in out] if cast else out

            def ktiles(cur_ref, prev_ref):
                return [prev_ref[lay.idx(0, r)].astype(MXU_DTYPE)] + tiles(cur_ref, True)

            qt = [(t * (HD ** -0.5)).astype(MXU_DTYPE) for t in tiles(q_ref, False)]
            dot_ = tiles(do_ref, True)
            lse_t = tiles(lse_ref, False)
            dl_t = tiles(dl_ref, False)
            kt = ktiles(kc_ref, kp_ref)
            vt = ktiles(vc_ref, vp_ref)
            dk_acc = [jnp.zeros((QBLK, LANES), F32) for _ in range(qb)]
            dv_acc = [jnp.zeros((QBLK, LANES), F32) for _ in range(qb)]
            crow = pl.ds(0, QBLK) if lay.nres == 1 else pl.ds(pl.multiple_of(r * QBLK, QBLK), QBLK)
            dk_acc[qb - 1] = jnp.where(j > 0, ck_ref[crow, :], 0.0)
            dv_acc[qb - 1] = jnp.where(j > 0, cv_ref[crow, :], 0.0)
            for x in range(qb):
                kcat = jnp.concatenate([kt[x], kt[x + 1]], axis=0)
                vcat = jnp.concatenate([vt[x], vt[x + 1]], axis=0)
                q2 = _stack_heads(qt[x], m_a)
                do2 = _stack_heads(dot_[x], m_a)
                lse2 = _both_heads(lse_t[x], m_a)
                dl2 = _both_heads(dl_t[x], m_a)
                lse2 = jnp.concatenate([lse2, lse2], axis=1)
                dl2 = jnp.concatenate([dl2, dl2], axis=1)
                s = lax.dot_general(q2, kcat, NT_DIMS, preferred_element_type=F32)
                p = jnp.exp(jnp.where(first if x == 0 else band, s, NEG) - lse2)
                dp = lax.dot_general(do2, vcat, NT_DIMS, preferred_element_type=F32)
                ds = p * (dp - dl2)
                dq = jnp.dot(ds.astype(MXU_DTYPE), kcat, preferred_element_type=F32)
                dq_ref[lay.idx(x, r)] = jnp.where(m_a, dq[:QBLK], dq[QBLK:]) * (HD ** -0.5)
                dk = jnp.dot(ds.T.astype(MXU_DTYPE), q2, preferred_element_type=F32)
                dv = jnp.dot(p.T.astype(MXU_DTYPE), do2, preferred_element_type=F32)
                if x == 0:
                    ck_ref[crow, :] = dk[:QBLK]
                    cv_ref[crow, :] = dv[:QBLK]
                else:
                    dk_acc[x - 1] = dk_acc[x - 1] + dk[:QBLK]
                    dv_acc[x - 1] = dv_acc[x - 1] + dv[:QBLK]
                dk_acc[x] = dk_acc[x] + dk[QBLK:]
                dv_acc[x] = dv_acc[x] + dv[QBLK:]
            for kb in range(qb):
                dk_ref[lay.idx(kb, r)] = dk_acc[kb]
                dv_ref[lay.idx(kb, r)] = dv_acc[kb]

        lay.per_residue(residue)

    cur, prev = lay.cur(lambda j: nchunk - 1 - j), lay.prev(lambda j: nchunk - 1 - j)
    carried = pltpu.VMEM((lay.nres * QBLK, LANES), F32)
    return _pcall(
        body, name, lay.grid, [cur, cur, prev, cur, prev, cur, cur, cur], [cur, cur, cur],
        [jax.ShapeDtypeStruct((S, AW), F32)] * 3, ("arbitrary", "arbitrary"),
        (qn, kn, kn, v, v, dycat, lse, delta), carry, scratch=[carried, carried])


def _shift_down(x, halo_prev, k, row):
    tm = x.shape[0]
    tail = jnp.concatenate([pltpu.roll(halo_prev, k, 0), jnp.zeros((tm - SUBLANES, x.shape[1]), x.dtype)], axis=0)
    return jnp.where(row < k, tail, pltpu.roll(x, k, 0))


def _shift_up(x, halo_next, k, row):
    tm = x.shape[0]
    head = jnp.concatenate([jnp.zeros((tm - SUBLANES, x.shape[1]), x.dtype), pltpu.roll(halo_next, SUBLANES - k, 0)], axis=0)
    return jnp.where(row >= tm - k, head, pltpu.roll(x, tm - k, 0))


def _conv_fwd(cu, halo_cu, cw_ref, row):
    u1 = _shift_down(cu, halo_cu, 1, row)
    u2 = _shift_down(cu, halo_cu, 2, row)
    cv = cw_ref[0:1, :] * u2 + cw_ref[1:2, :] * u1 + cw_ref[2:3, :] * cu + cw_ref[3:4, :]
    return cv, u1, u2


def combine_conv(os_, lses, proj, cw, name, carry=None):
    S = proj.shape[0]
    tm = _row_tile(S, 512)
    hb = tm // SUBLANES

    def body(o1, o2, o3, l1, l2, l3, pc_ref, ph_ref, cw_ref, ycat_ref, lse_ref):
        i = pl.program_id(0)
        for p in range(AW // LANES):
            cs = slice(p * LANES, (p + 1) * LANES)
            ls = [l[:, cs] for l in (l1, l2, l3)]
            mx = jnp.maximum(jnp.maximum(ls[0], ls[1]), ls[2])
            t = mx + jnp.log(jnp.exp(ls[0] - mx) + jnp.exp(ls[1] - mx) + jnp.exp(ls[2] - mx))
            lse_ref[:, cs] = t
            acc = jnp.zeros((tm, LANES), F32)
            for l, o in zip(ls, (o1, o2, o3)):
                acc = acc + jnp.exp(l - t) * o[:, cs]
            ycat_ref[:, cs] = acc.astype(ycat_ref.dtype)
        row = lax.broadcasted_iota(jnp.int32, (tm, CW), 0)
        gb, gc, u = pc_ref[:, 0:CW], pc_ref[:, CW:2 * CW], pc_ref[:, 2 * CW:3 * CW]
        halo_cu = jnp.where(i > 0, ph_ref[:, CW:2 * CW] * ph_ref[:, 2 * CW:3 * CW], 0.0)
        cv, _, _ = _conv_fwd(gc * u, halo_cu, cw_ref, row)
        ycat_ref[:, AW:AW + CW] = (gb * cv).astype(ycat_ref.dtype)

    ot = pl.BlockSpec((tm, AW), lambda i: (i, 0))
    return _pcall(
        body, name, (S // tm,),
        [ot] * 6 + [pl.BlockSpec((tm, 3 * CW), lambda i: (i, 1)),
                    pl.BlockSpec((SUBLANES, 3 * CW), lambda i: (jnp.maximum(i * hb - 1, 0), 1)),
                    pl.BlockSpec((SUBLANES, CW), lambda i: (0, 0))],
        [pl.BlockSpec((tm, D), lambda i: (i, 0)), ot],
        [jax.ShapeDtypeStruct((S, D), ACT_DTYPE), jax.ShapeDtypeStruct((S, AW), F32)],
        ("arbitrary",), (*os_, *lses, proj, proj, cw), carry)


def out_proj(ycat, x, vec, wout, name):
    S = x.shape[0]
    tm = _row_tile(S, 512)

    def body(yc_ref, x_ref, vec_ref, w_ref, xn_ref, y_ref):
        y = jnp.dot(yc_ref[...].astype(MXU_DTYPE), w_ref[...], preferred_element_type=F32)
        xn_ref[...] = x_ref[...] + vec_ref[3:4, :] * y
        y_ref[...] = y.astype(y_ref.dtype)

    t = pl.BlockSpec((tm, D), lambda i: (i, 0))
    return pl.pallas_call(
        body, name=name, grid=(S // tm,),
        in_specs=[t, t, pl.BlockSpec((SUBLANES, D), lambda i: (0, 0)),
                  pl.BlockSpec((D, D), lambda i: (0, 0))],
        out_specs=[t, t],
        out_shape=[jax.ShapeDtypeStruct((S, D), F32), jax.ShapeDtypeStruct((S, D), ACT_DTYPE)],
        compiler_params=_params(("arbitrary",)),
    )(ycat, x, vec, wout)


def out_proj_bwd(dxo, y, ycat, vec, wout, name, carry=None):
    S = dxo.shape[0]
    tm = _row_tile(S, 512)

    def body(dxo_ref, y_ref, yc_ref, vec_ref, w_ref, dyb_ref, dyc_ref, dl_ref, sums_ref):
        dxo = dxo_ref[...]
        dgate = jnp.sum(dxo * y_ref[...].astype(F32), axis=0, keepdims=True)
        dy = (vec_ref[3:4, :] * dxo).astype(MXU_DTYPE)
        dyb_ref[...] = dy
        dyc_ref[...] = lax.dot_general(dy, w_ref[...], NT_DIMS, preferred_element_type=F32)
        m_a = _head_masks(tm)
        for p in range(AW // LANES):
            cs = slice(p * LANES, (p + 1) * LANES)
            s_a, s_b = _pair_stat(dyc_ref[:, cs] * yc_ref[:, cs].astype(F32), m_a)
            dl_ref[:, cs] = jnp.where(m_a, s_a, s_b)
        _acc_rows(sums_ref, pl.program_id(0) == 0, (dgate,))

    t = pl.BlockSpec((tm, D), lambda i: (i, 0))
    at = pl.BlockSpec((tm, AW), lambda i: (i, 0))
    return _pcall(
        body, name, (S // tm,),
        [t, t, t, pl.BlockSpec((SUBLANES, D), lambda i: (0, 0)), pl.BlockSpec((D, D), lambda i: (0, 0))],
        [t, t, at, pl.BlockSpec((SUBLANES, D), lambda i: (0, 0))],
        [jax.ShapeDtypeStruct((S, D), MXU_DTYPE), jax.ShapeDtypeStruct((S, D), F32),
         jax.ShapeDtypeStruct((S, AW), F32), jax.ShapeDtypeStruct((SUBLANES, D), F32)],
        ("arbitrary",), (dxo, y, ycat, vec, wout), carry)


def mixer_mid_bwd(dqs, dks, dvs, proj, dycat, gvec, cw, name, carry=None):
    S = proj.shape[0]
    tm = _row_tile(S, 512)
    hb = tm // SUBLANES
    nsl = S // SUBLANES
    ntile = S // tm

    def body(dq1, dq2, dq3, dk1, dk2, dk3, dv1, dv2, dv3, pr_ref, pp_ref, pn_ref, dyc_ref, dyn_ref,
             g_ref, cw_ref, dp_ref, sums_ref):
        i = pl.program_id(0)
        m_a = _head_masks(tm)
        gsum = []
        for which, parts in ((0, (dq1, dq2, dq3)), (1, (dk1, dk2, dk3))):
            acc_g = []
            for p in range(AW // LANES):
                lo = which * AW + p * LANES
                cs = slice(p * LANES, (p + 1) * LANES)
                xp = pr_ref[:, lo:lo + LANES]
                s_a, s_b = _pair_stat(xp * xp, m_a)
                rr = jnp.where(m_a, lax.rsqrt(s_a * (1.0 / HD) + EPS), lax.rsqrt(s_b * (1.0 / HD) + EPS))
                xh = xp * rr
                dn = parts[0][:, cs] + parts[1][:, cs] + parts[2][:, cs]
                acc_g.append(jnp.sum(dn * xh, axis=0, keepdims=True))
                t = dn * g_ref[which:which + 1, cs]
                t_a, t_b = _pair_stat(t * xh, m_a)
                mean = jnp.where(m_a, t_a, t_b) * (1.0 / HD)
                dp_ref[:, lo:lo + LANES] = (rr * (t - xh * mean)).astype(dp_ref.dtype)
            gsum.append(jnp.concatenate(acc_g, axis=1))
        dp_ref[:, 2 * AW:3 * AW] = (dv1[...] + dv2[...] + dv3[...]).astype(dp_ref.dtype)
        row = lax.broadcasted_iota(jnp.int32, (tm, CW), 0)
        base = 3 * AW
        gb, gc, u = pr_ref[:, base:base + CW], pr_ref[:, base + CW:base + 2 * CW], pr_ref[:, base + 2 * CW:base + 3 * CW]
        cu = gc * u
        halo_cu = jnp.where(i > 0, pp_ref[:, CW:2 * CW] * pp_ref[:, 2 * CW:3 * CW], 0.0)
        cv, u1, u2 = _conv_fwd(cu, halo_cu, cw_ref, row)
        dyc = dyc_ref[...]
        dp_ref[:, base:base + CW] = (dyc * cv).astype(dp_ref.dtype)
        dcv = dyc * gb
        halo_dcv = jnp.where(i < ntile - 1, dyn_ref[...] * pn_ref[:, 0:CW], 0.0)
        d1 = _shift_up(dcv, halo_dcv, 1, row)
        d2 = _shift_up(dcv, halo_dcv, 2, row)
        dcu = cw_ref[2:3, :] * dcv + cw_ref[1:2, :] * d1 + cw_ref[0:1, :] * d2
        dp_ref[:, base + CW:base + 2 * CW] = (dcu * u).astype(dp_ref.dtype)
        dp_ref[:, base + 2 * CW:base + 3 * CW] = (dcu * gc).astype(dp_ref.dtype)
        rows = (gsum[0], gsum[1],
                jnp.sum(dcv * u2, axis=0, keepdims=True), jnp.sum(dcv * u1, axis=0, keepdims=True),
                jnp.sum(dcv * cu, axis=0, keepdims=True), jnp.sum(dcv, axis=0, keepdims=True))
        _acc_rows(sums_ref, i == 0, rows)

    at = pl.BlockSpec((tm, AW), lambda i: (i, 0))
    return _pcall(
        body, name, (ntile,),
        [at] * 9 + [
            pl.BlockSpec((tm, INC), lambda i: (i, 0)),
            pl.BlockSpec((SUBLANES, 3 * CW), lambda i: (jnp.maximum(i * hb - 1, 0), 1)),
            pl.BlockSpec((SUBLANES, 3 * CW), lambda i: (jnp.minimum((i + 1) * hb, nsl - 1), 1)),
            pl.BlockSpec((tm, CW), lambda i: (i, 1)),
            pl.BlockSpec((SUBLANES, CW), lambda i: (jnp.minimum((i + 1) * hb, nsl - 1), 1)),
            pl.BlockSpec((SUBLANES, AW), lambda i: (0, 0)),
            pl.BlockSpec((SUBLANES, CW), lambda i: (0, 0))],
        [pl.BlockSpec((tm, INC), lambda i: (i, 0)), pl.BlockSpec((SUBLANES, AW), lambda i: (0, 0))],
        [jax.ShapeDtypeStruct((S, INC), MXU_DTYPE), jax.ShapeDtypeStruct((SUBLANES, AW), F32)],
        ("arbitrary",), (*dqs, *dks, *dvs, proj, proj, proj, dycat, dycat, gvec, cw), carry)


def mixer_in_bwd(dxo, x, dproj, vec, winp, name, carry=None):
    S = x.shape[0]
    tm = _row_tile(S, 512)
    pc = INC // NCHIP

    def body(dxo_ref, x_ref, dp_ref, vec_ref, w_ref, dxi_ref, sums_ref):
        xhat, r, gain, ng, sc, _, _ = _ada(x_ref[...], vec_ref)
        dh = jnp.zeros((tm, D), F32)
        for j in range(NCHIP):
            dh = dh + lax.dot_general(dp_ref[:, j * pc:(j + 1) * pc], w_ref[j], NT_DIMS, preferred_element_type=F32)
        dx, dshift, dscale, dng = _ada_bwd(dh, xhat, r, gain, ng, sc)
        dxi_ref[...] = dxo_ref[...] + dx
        _acc_rows(sums_ref, pl.program_id(0) == 0, (dshift, dscale, dng))

    t = pl.BlockSpec((tm, D), lambda i: (i, 0))
    return _pcall(
        body, name, (S // tm,),
        [t, t, pl.BlockSpec((tm, INC), lambda i: (i, 0)),
         pl.BlockSpec((SUBLANES, D), lambda i: (0, 0)),
         pl.BlockSpec((NCHIP, D, pc), lambda i: (0, 0, 0), pipeline_mode=pl.Buffered(1))],
        [t, pl.BlockSpec((SUBLANES, D), lambda i: (0, 0))],
        [jax.ShapeDtypeStruct((S, D), F32), jax.ShapeDtypeStruct((SUBLANES, D), F32)],
        ("arbitrary",), (dxo, x, dproj, vec, winp), carry)


def loss_head(xf, target, name):
    S = xf.shape[0]
    tm = _row_tile(S, 1024)

    def body(x_ref, t_ref, dy_ref, l_ref):
        diff = x_ref[...] - t_ref[...]
        dy_ref[...] = diff * (1.0 / D)
        part = jnp.sum(jnp.sum(diff * diff, axis=0, keepdims=True), axis=1, keepdims=True) * (0.5 / D)

        @pl.when(pl.program_id(0) == 0)
        def _():
            l_ref[...] = jnp.zeros_like(l_ref)
        l_ref[...] += jnp.broadcast_to(part, l_ref.shape)

    t = pl.BlockSpec((tm, D), lambda i: (i, 0))
    return pl.pallas_call(
        body, name=name, grid=(S // tm,),
        in_specs=[t, t],
        out_specs=[t, pl.BlockSpec((SUBLANES, LANES), lambda i: (0, 0))],
        out_shape=[jax.ShapeDtypeStruct((S, D), F32), jax.ShapeDtypeStruct((SUBLANES, LANES), F32)],
        compiler_params=_params(("arbitrary",)),
    )(xf, target)


def _vec(mod_l, ng_l, i):
    m = mod_l.reshape(3, 3, D)
    rows = jnp.stack([ng_l[i], m[i, 1], m[i, 0], m[i, 2]])
    return jnp.concatenate([rows, jnp.zeros((SUBLANES - 4, D), F32)], axis=0)


def local_step(x, target, mods, ngs, gvecs, cws, shards, w_first, cflag):
    saved = []
    weights = [dict(w1=[None, None], w2=[None, None]) for _ in range(2)]
    weights[0]["w1"][0], weights[0]["w2"][0] = w_first[0], w_first[1].reshape(DFF, D)
    h = x
    for l in range(2):
        w, sh = weights[l], shards[l]
        nxt = shards[l + 1] if l == 0 else None
        vecs = [_vec(mods[l], ngs[l], i) for i in range(3)]
        x0 = h
        (x1, a0, f0), (win, wout, w2b) = ffn_fwd(x0, vecs[0], w["w1"][0], w["w2"][0], 0.5, f"ffn_fwd_l{l}a",
                                                 carry=Carry("gather", [sh["win"], sh["wout"], sh["w2"][1]]))
        w["win"], w["wout"], w["w2"][1] = win, wout.reshape(D, D), w2b.reshape(DFF, D)
        proj, h1b, qn, kn, v = mixer_in(x1, vecs[1], w["win"], gvecs[l], f"mixer_in_l{l}")
        os_, lses, w1b = [], [], {}
        for d in DILATIONS:
            rows = {1: slice(0, D // 2), 16: slice(D // 2, D)}.get(d)
            carry = Carry("gather", [sh["w1"][1][rows]]) if rows else None
            (o, lse_d), w1b[d] = attn_fwd(qn, kn, v, d, f"attn_fwd_l{l}_d{d}", carry=carry)
            os_.append(o)
            lses.append(lse_d)
        w["w1"][1] = jnp.concatenate([w1b[1][0], w1b[16][0]], axis=1)
        (ycat, lse), got = combine_conv(os_, lses, proj, cws[l], f"combine_conv_l{l}",
                                        carry=Carry("gather", [nxt["w2"][0]]) if nxt else None)
        if nxt:
            weights[1]["w2"][0] = got[0].reshape(DFF, D)
        x2, y = out_proj(ycat, x1, vecs[1], w["wout"], f"out_proj_l{l}")
        (x3, a2, f2), got = ffn_fwd(x2, vecs[2], w["w1"][1], w["w2"][1], 0.5, f"ffn_fwd_l{l}b",
                                    carry=Carry("gather", [nxt["w1"][0]]) if nxt else None)
        if nxt:
            weights[1]["w1"][0] = got[0]
        saved.append(dict(vecs=vecs, x0=x0, a0=a0, f0=f0, x1=x1, proj=proj, h1b=h1b, qn=qn, kn=kn, v=v,
                          ycat=ycat, lse=lse, y=y, x2=x2, a2=a2, f2=f2))
        h = x3
    dx, loss_blk = loss_head(h, target, "loss_head")
    sums, totals, g_prev = [None, None], [None, None], None
    w2r = DFF // NCHIP
    for l in (1, 0):
        w, s = weights[l], saved[l]
        vecs = s["vecs"]
        ride = g_prev is not None
        own = l == 0
        mine, other = [None] * 6, [None] * 6

        def half_sum(group, recv, k0):
            return [add_half(g, r, cflag, f"add_sibling_l{l}_{k0 + j}") for j, (g, r) in enumerate(zip(group, recv))]

        def chip_sum(landed, k0):
            return [sum_chips(t, f"sum_chips_l{l}_{k0 + j}") for j, t in enumerate(landed)]

        (dx, hb, dfb, act, da, sums2), got = ffn_bwd(
            dx, s["x2"], s["a2"], s["f2"], vecs[2], w["w1"][1], w["w2"][1], 0.5, f"ffn_bwd_l{l}b",
            carry=Carry("swap_halves", g_prev) if ride else None)
        dw1b, _ = wgrad(hb, da, D, HALF, f"wgrad_w1_l{l}b")
        dw2b, _ = wgrad(act, dfb, HALF, D, f"wgrad_w2_l{l}b")
        if ride:
            wire = [add_half(g_prev[k], got[k], cflag, f"add_sibling_l{l + 1}_{k}") for k in range(6)]
        g_ffn_b = [dw1b, dw2b.reshape(NCHIP, w2r, D)]
        (dyb, dycat, delta, sums_o), got = out_proj_bwd(
            dx, s["y"], s["ycat"], vecs[1], w["wout"], f"out_proj_bwd_l{l}",
            carry=Carry("swap_halves", g_ffn_b) if own else None)
        dwout, _ = wgrad(s["ycat"].astype(MXU_DTYPE), dyb, D // 2, D, f"wgrad_wout_l{l}")
        if own:
            wire_ffn_b = half_sum(g_ffn_b, got, 4)
        dqs, dks, dvs, landed = [], [], [], {}
        for d in DILATIONS:
            carry = None
            if ride and d == 1:
                carry = Carry("scatter", wire[3:])
            if ride and d == 16:
                carry = Carry("scatter", wire[:3])
            if own and d == 4:
                carry = Carry("scatter", wire_ffn_b)
            (dq, dk, dv), landed[d] = attn_bwd(s["qn"], s["kn"], s["v"], dycat, s["lse"], delta, d,
                                               f"attn_bwd_l{l}_d{d}", carry=carry)
            dqs.append(dq)
            dks.append(dk)
            dvs.append(dv)
        if ride:
            tot = [sum_chips(t, f"sum_chips_l{l + 1}_{k}") for k, t in enumerate(list(landed[16]) + list(landed[1]))]
        if own:
            mine[4:6] = chip_sum(landed[4], 4)
        ready = (tot if ride else []) + (mine[4:6] if own else [])
        (dproj, sums_m), got = mixer_mid_bwd(dqs, dks, dvs, s["proj"], dycat, gvecs[l], cws[l], f"mixer_mid_bwd_l{l}",
                                             carry=Carry("swap", ready) if ready else None)
        if ride:
            totals[l + 1] = (tot, list(got[:6]))
        if own:
            other[4:6] = list(got[-2:])
        dwin, _ = wgrad(s["h1b"], dproj, D, INC // NCHIP, f"wgrad_win_l{l}")
        g_mixer = [dwin, dwout.reshape(NCHIP, D // NCHIP, D)]
        (dx, sums1), got = mixer_in_bwd(dx, s["x1"], dproj, vecs[1], w["win"], f"mixer_in_bwd_l{l}",
                                        carry=Carry("swap_halves", g_mixer) if own else None)
        if own:
            wire_mixer = half_sum(g_mixer, got, 2)
        (dx, hb, dfb, act, da, sums0), _ = ffn_bwd(
            dx, s["x0"], s["a0"], s["f0"], vecs[0], w["w1"][0], w["w2"][0], 0.5, f"ffn_bwd_l{l}a")
        dw1a, got = wgrad(hb, da, D, HALF, f"wgrad_w1_l{l}a", carry=Carry("scatter", wire_mixer) if own else None)
        if own:
            mine[2:4] = chip_sum(got, 2)
        dw2a, got = wgrad(act, dfb, HALF, D, f"wgrad_w2_l{l}a", carry=Carry("swap", mine[2:4]) if own else None)
        g_ffn_a = [dw1a, dw2a.reshape(NCHIP, w2r, D)]
        if own:
            other[2:4] = list(got)
            wire_ffn_a = half_sum(g_ffn_a, run_carry(Carry("swap_halves", g_ffn_a), "swap_halves_tail"), 0)
            mine[0:2] = chip_sum(run_carry(Carry("scatter", wire_ffn_a), "scatter_grads_tail"), 0)
            other[0:2] = list(run_carry(Carry("swap", mine[0:2]), "swap_totals_tail"))
            totals[l] = (mine, other)
        g_prev = g_ffn_a + g_mixer + g_ffn_b
        sums[l] = (sums0, sums1, sums_o, sums2, sums_m)
    return loss_blk, dx, totals, sums


def small_all_gather(blk, name):
    m_per, n = blk.shape

    def body(x_ref, out_ref, send_sems, recv_sems, local_sem):
        x, y, c = _here()
        me, sibling = (x, y, c), (x, y, 1 - c)
        chips = [(1 - x, y), (x, 1 - y), (1 - x, 1 - y)]

        def rows(px, py, pc):
            return out_ref.at[pl.ds((4 * px + 2 * py + pc) * m_per, m_per), :]

        def copy(k, block, to, src=None):
            return pltpu.make_async_remote_copy(
                src_ref=rows(*block) if src is None else src, dst_ref=rows(*block),
                send_sem=send_sems.at[k], recv_sem=recv_sems.at[k], device_id=to, device_id_type=MESH)

        mine = pltpu.make_async_copy(x_ref, rows(*me), local_sem)
        mine.start()
        first = [copy(0, me, sibling, src=x_ref)]
        first += [copy(1 + j, me, (*chip, c), src=x_ref) for j, chip in enumerate(chips)]
        for cp in first:
            cp.start()
        passed = [copy(4 + j, (*chip, c), sibling) for j, chip in enumerate(chips)]
        for j, chip in enumerate(chips):
            copy(1 + j, (*chip, c), me).wait_recv()
            passed[j].start()
        copy(0, sibling, me).wait_recv()
        for j, chip in enumerate(chips):
            copy(4 + j, (*chip, 1 - c), me).wait_recv()
        for cp in first + passed:
            cp.wait_send()
        mine.wait()

    return pl.pallas_call(
        body, name=name,
        out_shape=jax.ShapeDtypeStruct((NDEV * m_per, n), blk.dtype),
        in_specs=[pl.BlockSpec(memory_space=pltpu.VMEM)],
        out_specs=pl.BlockSpec(memory_space=pltpu.VMEM),
        scratch_shapes=[pltpu.SemaphoreType.DMA((7,)), pltpu.SemaphoreType.DMA((7,)), pltpu.SemaphoreType.DMA],
        compiler_params=pltpu.CompilerParams(vmem_limit_bytes=VMEM_LIMIT),
    )(blk)


EW_BLOCK_BYTES = 1 << 20


def _ew_rows(rows, cols):
    want = max(16, EW_BLOCK_BYTES // (4 * cols))
    best = None
    for t in range(16, rows + 1, 16):
        if rows % t == 0 and t <= want:
            best = t
    return best if best is not None else rows


def add_half(g, recv, cflag, name):
    pieces, r, cols = g.shape
    r2 = r // 2
    tr = _ew_rows(r2, cols)
    nt = r2 // tr

    def body(c_ref, g_ref, r_ref, o_ref):
        o_ref[...] = (g_ref[...] + r_ref[...]).astype(o_ref.dtype)

    half = pl.BlockSpec((None, tr, cols), lambda j, i, c_ref: (j, i, 0))
    return pl.pallas_call(
        body, name=name,
        grid_spec=pltpu.PrefetchScalarGridSpec(
            num_scalar_prefetch=1, grid=(pieces, nt),
            in_specs=[pl.BlockSpec((None, tr, cols), lambda j, i, c_ref: (j, c_ref[0] * nt + i, 0)), half],
            out_specs=half),
        out_shape=jax.ShapeDtypeStruct((pieces, r2, cols), WIRE_DTYPE),
        compiler_params=_params(("arbitrary", "arbitrary")),
    )(cflag, g, recv)


def sum_chips(recv, name):
    _, r, cols = recv.shape
    tr = _ew_rows(r, cols)

    def body(r_ref, o_ref):
        acc = r_ref[0].astype(F32)
        for k in range(1, NCHIP):
            acc = acc + r_ref[k].astype(F32)
        o_ref[...] = acc

    return pl.pallas_call(
        body, name=name, grid=(r // tr,),
        in_specs=[pl.BlockSpec((NCHIP, tr, cols), lambda i: (0, i, 0))],
        out_specs=pl.BlockSpec((tr, cols), lambda i: (i, 0)),
        out_shape=jax.ShapeDtypeStruct((r, cols), F32),
        compiler_params=_params(("arbitrary",)),
    )(recv)


def sum_devices(rows8, name):
    def body(r_ref, o_ref):
        acc = r_ref[0:1, :]
        for k in range(1, NDEV):
            acc = acc + r_ref[k:k + 1, :]
        o_ref[...] = jnp.broadcast_to(acc, o_ref.shape)

    return pl.pallas_call(
        body, name=name, out_shape=jax.ShapeDtypeStruct(rows8.shape, F32),
        in_specs=[pl.BlockSpec(memory_space=pltpu.VMEM)], out_specs=pl.BlockSpec(memory_space=pltpu.VMEM),
        compiler_params=pltpu.CompilerParams(vmem_limit_bytes=VMEM_LIMIT),
    )(rows8)


def adamw(w, m, v, srcs, cflag, name, halves=False):
    planes, r, cols = w.shape
    rh = r // 2 if halves else r
    tr = _ew_rows(rh, cols)
    nth = rh // tr
    flat = [a for s in srcs for a in (s if halves else (s,))]
    ns = len(flat)
    per = ns // planes

    def body(c_ref, w_ref, m_ref, v_ref, *rest):
        s_refs, (g_ref, d_ref, mo_ref, vo_ref) = rest[:ns], rest[ns:]
        p, i = pl.program_id(0), pl.program_id(1)
        if halves:
            mine = jnp.logical_not(jnp.logical_xor(i >= nth, c_ref[0] == 1))
            blocks = [jnp.where(mine, s_refs[2 * k][...], s_refs[2 * k + 1][...]) for k in range(planes)]
        else:
            blocks = [s[...] for s in s_refs]
        g = blocks[0]
        for k in range(1, planes):
            g = jnp.where(p == k, blocks[k], g)
        g_ref[...] = g
        m_new = ADAM_B1 * m_ref[...] + (1.0 - ADAM_B1) * g
        v_new = ADAM_B2 * v_ref[...] + (1.0 - ADAM_B2) * (g * g)
        mo_ref[...] = m_new
        vo_ref[...] = v_new
        m_hat = m_new / (1.0 - ADAM_B1 ** ADAM_STEP)
        v_hat = v_new / (1.0 - ADAM_B2 ** ADAM_STEP)
        d_ref[...] = -ADAM_LR * (m_hat / (jnp.sqrt(v_hat) + ADAM_EPS) + ADAM_WD * w_ref[...])

    pt = pl.BlockSpec((None, tr, cols), lambda p, i: (p, i, 0))
    st = [pl.BlockSpec((tr, cols), functools.partial(lambda k, p, i: (jnp.where(p == k, i % nth, 0), 0), j // per))
          for j in range(ns)]
    return pl.pallas_call(
        body, name=name, grid=(planes, r // tr),
        in_specs=[pl.BlockSpec(memory_space=pltpu.SMEM), pt, pt, pt] + st,
        out_specs=[pt] * 4,
        out_shape=[jax.ShapeDtypeStruct(w.shape, F32)] * 4,
        compiler_params=_params(("arbitrary", "arbitrary")),
    )(cflag, w, m, v, *flat)


ADA_COLS = 9 * D // NCHIP


def mod_fwd(c_all, w_ada, b_shard, name):
    def body(c_ref, w_ref, b_ref, o_ref):
        cc = c_ref[...]
        sc = cc * jax.nn.sigmoid(cc)
        o_ref[...] = jnp.dot(sc, w_ref[...], preferred_element_type=F32,
                             precision=lax.Precision.HIGHEST) + b_ref[...]

    return pl.pallas_call(
        body, name=name, grid=(2,),
        in_specs=[pl.BlockSpec((NDEV, D), lambda l: (0, 0)),
                  pl.BlockSpec((None, D, ADA_COLS), lambda l: (l, 0, 0)),
                  pl.BlockSpec((None, 1, ADA_COLS), lambda l: (l, 0, 0))],
        out_specs=pl.BlockSpec((None, NDEV, ADA_COLS), lambda l: (l, 0, 0)),
        out_shape=jax.ShapeDtypeStruct((2, NDEV, ADA_COLS), F32),
        compiler_params=_params(("arbitrary",)),
    )(c_all, w_ada, b_shard.reshape(2, 1, ADA_COLS))


def wada_grad(c_all_t, dmod, name):
    ct = ADA_COLS // 3

    def body(c_ref, d_ref, o_ref):
        cc = c_ref[...]
        sc = cc * jax.nn.sigmoid(cc)
        acc = sc[:, 0:1] * d_ref[0:1, :]
        for b in range(1, NDEV):
            acc = acc + sc[:, b:b + 1] * d_ref[b:b + 1, :]
        o_ref[...] = acc

    return pl.pallas_call(
        body, name=name, grid=(2, 3),
        in_specs=[pl.BlockSpec((D, LANES), lambda l, j: (0, 0)),
                  pl.BlockSpec((None, NDEV, ct), lambda l, j: (l, 0, j))],
        out_specs=pl.BlockSpec((None, D, ct), lambda l, j: (l, 0, j)),
        out_shape=jax.ShapeDtypeStruct((2, D, ADA_COLS), F32),
        compiler_params=_params(("arbitrary", "arbitrary")),
    )(c_all_t, dmod)


def _pad_rows(row, rows=SUBLANES):
    return jnp.concatenate([row[None, :], jnp.zeros((rows - 1, row.shape[0]), row.dtype)], axis=0)


def kernel(x, c, w_ada, b_ada, norm_g, w_in, q_norm_g, k_norm_g, conv_w, conv_b, w_out, ffn_w1, ffn_w2, loss_target, m_w_ada, m_b_ada, m_norm_g, m_w_in, m_q_norm_g, m_k_norm_g, m_conv_w, m_conv_b, m_w_out, m_ffn_w1, m_ffn_w2, v_w_ada, v_b_ada, v_norm_g, v_w_in, v_q_norm_g, v_k_norm_g, v_conv_w, v_conv_b, v_w_out, v_ffn_w1, v_ffn_w2):
    ix, iy, ic = lax.axis_index("x"), lax.axis_index("y"), lax.axis_index("c")
    chip = 2 * ix + iy
    dev = 2 * chip + ic
    cflag = jnp.reshape(ic, (1,)).astype(jnp.int32)
    ngw = norm_g.shape[-1]
    cww = conv_w.shape[-1]

    pack = jnp.concatenate([c[0], norm_g.reshape(-1), conv_w.reshape(-1)])
    got = small_all_gather(_pad_rows(pack), "gather_c_normg_convw")[::SUBLANES]
    c_all = got[:, :D]
    per_chip = got[::2]
    ng_full = jnp.concatenate([per_chip[j, D:D + 6 * ngw].reshape(2, 3, ngw) for j in range(NCHIP)], axis=-1)
    cw_full = jnp.concatenate([per_chip[j, D + 6 * ngw:].reshape(2, 3, cww) for j in range(NCHIP)], axis=-1)

    b_shard = lax.dynamic_slice_in_dim(b_ada, chip * ADA_COLS, ADA_COLS, axis=1)
    mod_blk = mod_fwd(c_all, w_ada, b_shard, "mod_fwd").reshape(2 * NDEV, ADA_COLS)
    mod_all = small_all_gather(mod_blk, "gather_mod").reshape(NDEV, 2, NDEV, ADA_COLS)[::2]
    mod_mine = lax.dynamic_index_in_dim(mod_all, dev, axis=2, keepdims=False)
    mods = [mod_mine[:, l, :].reshape(-1) for l in range(2)]

    shards, gvecs, cws = [], [], []
    for l in range(2):
        shards.append(dict(w1=[ffn_w1[l, i].astype(MXU_DTYPE) for i in range(2)],
                           w2=[ffn_w2[l, i].astype(MXU_DTYPE) for i in range(2)],
                           win=w_in[l].astype(MXU_DTYPE), wout=w_out[l].astype(MXU_DTYPE)))
        gv = jnp.stack([jnp.tile(q_norm_g[l], AW // HD), jnp.tile(k_norm_g[l], AW // HD)])
        gvecs.append(jnp.concatenate([gv, jnp.zeros((SUBLANES - 2, AW), F32)], axis=0))
        cws.append(jnp.concatenate([cw_full[l], conv_b[l][None, :], jnp.zeros((SUBLANES - 4, CW), F32)], axis=0))
    w_first = gather_split([shards[0]["w1"][0], shards[0]["w2"][0]], "gather_first_ffn")

    loss_blk, dx, totals, sums = local_step(x[0], loss_target[0], mods, [ng_full[0], ng_full[1]], gvecs, cws,
                                            shards, w_first, cflag)

    dmods, dngs, dqg, dkg, dcw, dcb = [], [], [], [], [], []
    for l in range(2):
        s0, s1, so, s2, sm = sums[l]
        dmods.append(jnp.concatenate([s0[0], s0[1], s0[3], s1[0], s1[1], so[0], s2[0], s2[1], s2[3]]))
        dngs.append(jnp.concatenate([s0[2], s1[2], s2[2]]))
        dqg.append(sm[0].reshape(AW // HD, HD).sum(0))
        dkg.append(sm[1].reshape(AW // HD, HD).sum(0))
        dcw.append(sm[2:5].reshape(-1))
        dcb.append(sm[5])
    small = jnp.concatenate(dmods + dngs + dqg + dkg + dcw + dcb + [loss_blk[0]])
    small_all = small_all_gather(_pad_rows(small), "gather_small_grads")[::SUBLANES]
    nm = 9 * D
    dmod_all = small_all[:, :2 * nm].reshape(NDEV, 2, NCHIP, ADA_COLS)
    dmod_mine = lax.dynamic_index_in_dim(dmod_all, chip, axis=2, keepdims=False).transpose(1, 0, 2)
    tot = sum_devices(small_all, "sum_small_grads")[0]
    o = 2 * nm
    g_b_ada = tot[:o].reshape(2, nm)
    g_norm_g = lax.dynamic_slice_in_dim(tot[o:o + 6 * D].reshape(2, 3, D), chip * ngw, ngw, axis=2)
    o += 6 * D
    g_qg = tot[o:o + 2 * HD].reshape(2, HD)
    o += 2 * HD
    g_kg = tot[o:o + 2 * HD].reshape(2, HD)
    o += 2 * HD
    g_cw = lax.dynamic_slice_in_dim(tot[o:o + 6 * CW].reshape(2, 3, CW), chip * cww, cww, axis=2)
    o += 6 * CW
    g_cb = tot[o:o + 2 * CW].reshape(2, CW)
    loss = tot[o + 2 * CW]

    c_all_t = jnp.concatenate([c_all.T, jnp.zeros((D, LANES - NDEV), F32)], axis=1)
    g_wada_src = wada_grad(c_all_t, dmod_mine, "wada_grad")

    def halves(k_of_plane):
        return [(totals[l][0][k], totals[l][1][k]) for l, k in k_of_plane]

    r_wada = adamw(w_ada, m_w_ada, v_w_ada, [g_wada_src[0], g_wada_src[1]], cflag, "adamw_w_ada")
    r_win = adamw(w_in, m_w_in, v_w_in, halves([(0, 2), (1, 2)]), cflag, "adamw_w_in", halves=True)
    r_wout = adamw(w_out, m_w_out, v_w_out, halves([(0, 3), (1, 3)]), cflag, "adamw_w_out", halves=True)
    r_w1 = adamw(ffn_w1.reshape(4, D, HALF), m_ffn_w1.reshape(4, D, HALF), v_ffn_w1.reshape(4, D, HALF),
                 halves([(0, 0), (0, 4), (1, 0), (1, 4)]), cflag, "adamw_ffn_w1", halves=True)
    w2r = DFF // NCHIP
    r_w2 = adamw(ffn_w2.reshape(4, w2r, D), m_ffn_w2.reshape(4, w2r, D), v_ffn_w2.reshape(4, w2r, D),
                 halves([(0, 1), (0, 5), (1, 1), (1, 5)]), cflag, "adamw_ffn_w2", halves=True)
    r_w1 = [t.reshape(ffn_w1.shape) for t in r_w1]
    r_w2 = [t.reshape(ffn_w2.shape) for t in r_w2]

    smalls = [("b_ada", b_ada, m_b_ada, v_b_ada, g_b_ada), ("norm_g", norm_g, m_norm_g, v_norm_g, g_norm_g),
              ("q_norm_g", q_norm_g, m_q_norm_g, v_q_norm_g, g_qg), ("k_norm_g", k_norm_g, m_k_norm_g, v_k_norm_g, g_kg),
              ("conv_w", conv_w, m_conv_w, v_conv_w, g_cw), ("conv_b", conv_b, m_conv_b, v_conv_b, g_cb)]
    n_small = sum(t[1].size for t in smalls)
    pad = (-n_small) % (16 * LANES)

    def packed(idx):
        flat = jnp.concatenate([t[idx].reshape(-1) for t in smalls] + [jnp.zeros((pad,), F32)])
        return flat.reshape(-1, LANES)

    r_small = adamw(packed(1)[None], packed(2)[None], packed(3)[None], [packed(4)], cflag, "adamw_small")
    small_out = {}
    o = 0
    for name_, w_, _, _, _ in smalls:
        small_out[name_] = [t.reshape(-1)[o:o + w_.size].reshape(w_.shape) for t in r_small]
        o += w_.size

    res = {"w_ada": r_wada, "w_in": r_win, "w_out": r_wout, "ffn_w1": r_w1, "ffn_w2": r_w2, **small_out}
    order = ["w_ada", "b_ada", "norm_g", "w_in", "q_norm_g", "k_norm_g", "conv_w", "conv_b", "w_out", "ffn_w1", "ffn_w2"]
    outs = [loss, dx[None]]
    for k in range(4):
        outs += [res[nm_][k] for nm_ in order]
    return tuple(outs)
```

```python
import functools

import jax
import jax.numpy as jnp
from jax import lax
from jax.experimental import pallas as pl
from jax.experimental.pallas import tpu as pltpu

F32 = jnp.float32
MXU_DTYPE = jnp.bfloat16
ACT_DTYPE = jnp.bfloat16
WIRE_DTYPE = jnp.bfloat16

D = 1024
HD = 64
AW = 512
CW = 512
DFF = 2816
HALF = DFF // 2
INC = 3 * AW + 3 * CW
NCHIP = 4
NDEV = 8
QBLK = 128
ATTN_QBLOCKS = 8
ATTN_INTERLEAVE = 8
ATTN_CHUNK_ROWS = 4096
DILATIONS = (1, 4, 16)
EPS = 1e-6
NEG = -1e30
LANES = 128
SUBLANES = 8
HALO_ROWS = 16
VMEM_LIMIT = 56 * 1024 * 1024

ADAM_LR = 0.001
ADAM_B1 = 0.9
ADAM_B2 = 0.999
ADAM_EPS = 1e-08
ADAM_WD = 0.01
ADAM_STEP = 10

NT_DIMS = (((1,), (1,)), ((), ()))
TN_DIMS = (((0,), (0,)), ((), ()))


def _params(sem, vmem=VMEM_LIMIT):
    return pltpu.CompilerParams(dimension_semantics=sem, vmem_limit_bytes=vmem)


def _row_tile(n, want):
    t = min(n, want)
    assert n % t == 0
    return t


def _ada(xt, vec_ref):
    ng, sc, sh, gt = vec_ref[0:1, :], vec_ref[1:2, :], vec_ref[2:3, :], vec_ref[3:4, :]
    r = lax.rsqrt(jnp.mean(xt * xt, axis=-1, keepdims=True) + EPS)
    return xt * r, r, ng * (1.0 + sc), ng, sc, sh, gt


def _ada_bwd(dh, xhat, r, gain, ng, sc):
    dshift = jnp.sum(dh, axis=0, keepdims=True)
    dhx = dh * xhat
    dscale = jnp.sum(dhx, axis=0, keepdims=True) * ng
    dng = jnp.sum(dhx, axis=0, keepdims=True) * (1.0 + sc)
    dxhat = dh * gain
    dx = r * (dxhat - xhat * jnp.mean(dxhat * xhat, axis=-1, keepdims=True))
    return dx, dshift, dscale, dng


def _acc_rows(sums_ref, first, rows):
    @pl.when(first)
    def _():
        sums_ref[...] = jnp.zeros_like(sums_ref)
    for k, row in enumerate(rows):
        sums_ref[k:k + 1, :] += row


MESH = pl.DeviceIdType.MESH
ANY = pl.BlockSpec(memory_space=pl.ANY)


def _here():
    return lax.axis_index("x"), lax.axis_index("y"), lax.axis_index("c")


def _ici_copies(src_refs, dst_refs, send_sems, recv_sems, local_sems, scatter):
    x, y, c = _here()
    my_chip = 2 * x + y
    peers = [(1 - x, y), (x, 1 - y), (1 - x, 1 - y)]
    local, out, inc = [], [], []
    for a, (src, dst) in enumerate(zip(src_refs, dst_refs)):
        local.append(pltpu.make_async_copy(src.at[my_chip] if scatter else src, dst.at[my_chip], local_sems.at[a]))
        for j, (px, py) in enumerate(peers):
            sems = dict(send_sem=send_sems.at[3 * a + j], recv_sem=recv_sems.at[3 * a + j],
                        device_id=(px, py, c), device_id_type=MESH)
            out.append(pltpu.make_async_remote_copy(
                src_ref=src.at[2 * px + py] if scatter else src, dst_ref=dst.at[my_chip], **sems))
            inc.append(pltpu.make_async_remote_copy(
                src_ref=src.at[my_chip] if scatter else src, dst_ref=dst.at[2 * px + py], **sems))
    return local, out, inc


def _swap_copies(src_refs, dst_refs, send_sems, recv_sems, halves):
    x, y, c = _here()
    cps = []
    for k, (src, dst) in enumerate(zip(src_refs, dst_refs)):
        if halves:
            r2 = src.shape[1] // 2
            src = src.at[:, pl.ds((1 - c) * r2, r2), :]
        cps.append(pltpu.make_async_remote_copy(
            src_ref=src, dst_ref=dst, send_sem=send_sems.at[k], recv_sem=recv_sems.at[k],
            device_id=(x, y, 1 - c), device_id_type=MESH))
    return cps


class Carry:
    def __init__(self, kind, srcs):
        self.kind, self.srcs, n = kind, list(srcs), len(srcs)
        if kind == "gather":
            shapes = [(NCHIP,) + s.shape for s in srcs]
        elif kind == "swap_halves":
            shapes = [(s.shape[0], s.shape[1] // 2, s.shape[2]) for s in srcs]
        else:
            shapes = [s.shape for s in srcs]
        self.out_shape = [jax.ShapeDtypeStruct(sh, s.dtype) for sh, s in zip(shapes, srcs)]
        dma = pltpu.SemaphoreType.DMA
        self.sems = [dma((3 * n,)), dma((3 * n,)), dma((n,))] if kind in ("gather", "scatter") else [dma((n,)), dma((n,))]

    def start(self, srcs, dsts, sems):
        if self.kind in ("gather", "scatter"):
            local, out, _ = _ici_copies(srcs, dsts, *sems, self.kind == "scatter")
            for cp in local + out:
                cp.start()
        else:
            for cp in _swap_copies(srcs, dsts, *sems, self.kind == "swap_halves"):
                cp.start()

    def wait(self, srcs, dsts, sems):
        if self.kind in ("gather", "scatter"):
            local, out, inc = _ici_copies(srcs, dsts, *sems, self.kind == "scatter")
            for cp in inc:
                cp.wait_recv()
            for cp in out:
                cp.wait_send()
            for cp in local:
                cp.wait()
        else:
            cps = _swap_copies(srcs, dsts, *sems, self.kind == "swap_halves")
            for cp in cps:
                cp.wait_recv()
            for cp in cps:
                cp.wait_send()


def run_carry(carry, name):
    n = len(carry.srcs)

    def body(*refs):
        srcs, dsts, sems = refs[:n], refs[n:2 * n], refs[2 * n:]
        carry.start(srcs, dsts, sems)
        carry.wait(srcs, dsts, sems)

    return pl.pallas_call(body, name=name, out_shape=carry.out_shape, in_specs=[ANY] * n, out_specs=[ANY] * n,
                          scratch_shapes=carry.sems)(*carry.srcs)


def gather_split(srcs, name):
    n = len(srcs)

    def body(*refs):
        src_refs, dst_refs = refs[:n], refs[n:2 * n]
        send_sems, recv_sems, fwd_send, fwd_recv, local_sems = refs[2 * n:]
        x, y, c = _here()
        my_chip = 2 * x + y
        peers = [(1 - x, y), (x, 1 - y), (1 - x, 1 - y)]

        def half(ref, h):
            r2 = ref.shape[0] // 2
            return ref.at[pl.ds(h * r2, r2), :]

        local, out, landed, passed, arriving = [], [], [], [], []
        for a, (src, dst) in enumerate(zip(src_refs, dst_refs)):
            local.append(pltpu.make_async_copy(src, dst.at[my_chip], local_sems.at[a]))
            for j, (px, py) in enumerate(peers):
                k = 3 * a + j
                theirs = dst.at[2 * px + py]
                ici = dict(send_sem=send_sems.at[k], recv_sem=recv_sems.at[k], device_id=(px, py, c), device_id_type=MESH)
                d2d = dict(send_sem=fwd_send.at[k], recv_sem=fwd_recv.at[k], device_id=(x, y, 1 - c), device_id_type=MESH)
                out.append(pltpu.make_async_remote_copy(src_ref=half(src, c), dst_ref=half(dst.at[my_chip], c), **ici))
                landed.append(pltpu.make_async_remote_copy(src_ref=half(src, c), dst_ref=half(theirs, c), **ici))
                passed.append(pltpu.make_async_remote_copy(src_ref=half(theirs, c), dst_ref=half(theirs, c), **d2d))
                arriving.append(pltpu.make_async_remote_copy(src_ref=half(theirs, c), dst_ref=half(theirs, 1 - c), **d2d))
        for cp in local + out:
            cp.start()
        for got, fwd in zip(landed, passed):
            got.wait_recv()
            fwd.start()
        for cp in arriving:
            cp.wait_recv()
        for cp in out + passed:
            cp.wait_send()
        for cp in local:
            cp.wait()

    dma = pltpu.SemaphoreType.DMA
    return pl.pallas_call(
        body, name=name, out_shape=[jax.ShapeDtypeStruct((NCHIP,) + s.shape, s.dtype) for s in srcs],
        in_specs=[ANY] * n, out_specs=[ANY] * n,
        scratch_shapes=[dma((3 * n,)), dma((3 * n,)), dma((3 * n,)), dma((3 * n,)), dma((n,))],
    )(*srcs)


def _pcall(body, name, grid, in_specs, out_specs, out_shape, sem, args, carry=None, scratch=()):
    if carry is None:
        outs = pl.pallas_call(body, name=name, grid=grid, in_specs=in_specs, out_specs=out_specs,
                              out_shape=out_shape, scratch_shapes=list(scratch), compiler_params=_params(sem))(*args)
        return outs, []
    n_in, n_out, nc, ns = len(in_specs), len(out_specs), len(carry.srcs), len(scratch)

    def wrapped(*refs):
        ins, csrc = refs[:n_in], refs[n_in:n_in + nc]
        outs, cdst = refs[n_in + nc:n_in + nc + n_out], refs[n_in + nc + n_out:n_in + 2 * nc + n_out]
        own = refs[n_in + 2 * nc + n_out:n_in + 2 * nc + n_out + ns]
        sems = refs[n_in + 2 * nc + n_out + ns:]
        ids = [pl.program_id(a) for a in range(len(grid))]
        first = functools.reduce(jnp.logical_and, [i == 0 for i in ids])
        last = functools.reduce(jnp.logical_and, [i == g - 1 for i, g in zip(ids, grid)])

        @pl.when(first)
        def _():
            carry.start(csrc, cdst, sems)

        body(*ins, *outs, *own)

        @pl.when(last)
        def _():
            carry.wait(csrc, cdst, sems)

    res = pl.pallas_call(
        wrapped, name=name, grid=grid,
        in_specs=list(in_specs) + [ANY] * nc, out_specs=list(out_specs) + [ANY] * nc,
        out_shape=list(out_shape) + carry.out_shape,
        scratch_shapes=list(scratch) + carry.sems, compiler_params=_params(sem),
    )(*args, *carry.srcs)
    return res[:n_out], res[n_out:]


def ffn_fwd(x, vec, w1p, w2, gs, name, carry=None):
    S = x.shape[0]
    tm = _row_tile(S, 512)

    def body(x_ref, vec_ref, w1_ref, w2_ref, xn_ref, a_ref, f_ref):
        xt = x_ref[...]
        xhat, _, gain, _, _, sh, gt = _ada(xt, vec_ref)
        h = (xhat * gain + sh).astype(MXU_DTYPE)
        f = jnp.zeros((tm, D), F32)
        for hf in range(2):
            g = jnp.dot(h, w1_ref[hf], preferred_element_type=F32)
            up = jnp.dot(h, w1_ref[2 + hf], preferred_element_type=F32)
            a_ref[:, hf * HALF:(hf + 1) * HALF] = g.astype(a_ref.dtype)
            a_ref[:, DFF + hf * HALF:DFF + (hf + 1) * HALF] = up.astype(a_ref.dtype)
            act = (g * jax.nn.sigmoid(g) * up).astype(MXU_DTYPE)
            f = f + jnp.dot(act, w2_ref[hf * HALF:(hf + 1) * HALF, :], preferred_element_type=F32)
        xn_ref[...] = xt + (gs * gt) * f
        f_ref[...] = f.astype(f_ref.dtype)

    return _pcall(
        body, name, (S // tm,),
        [pl.BlockSpec((tm, D), lambda i: (i, 0)),
         pl.BlockSpec((SUBLANES, D), lambda i: (0, 0)),
         pl.BlockSpec((NCHIP, D, HALF), lambda i: (0, 0, 0), pipeline_mode=pl.Buffered(1)),
         pl.BlockSpec((DFF, D), lambda i: (0, 0), pipeline_mode=pl.Buffered(1))],
        [pl.BlockSpec((tm, D), lambda i: (i, 0)),
         pl.BlockSpec((tm, 2 * DFF), lambda i: (i, 0)),
         pl.BlockSpec((tm, D), lambda i: (i, 0))],
        [jax.ShapeDtypeStruct((S, D), F32),
         jax.ShapeDtypeStruct((S, 2 * DFF), ACT_DTYPE),
         jax.ShapeDtypeStruct((S, D), ACT_DTYPE)],
        ("arbitrary",), (x, vec, w1p, w2), carry)


def ffn_bwd(dxo, x, a, f, vec, w1p, w2, gs, name, carry=None):
    S = x.shape[0]
    tm = _row_tile(S, 256)

    def body(dxo_ref, x_ref, a_ref, f_ref, vec_ref, w1_ref, w2_ref,
             dxi_ref, hb_ref, dfb_ref, act_ref, da_ref, sums_ref):
        xt = x_ref[...]
        dxo = dxo_ref[...]
        xhat, r, gain, ng, sc, sh, gt = _ada(xt, vec_ref)
        hb_ref[...] = (xhat * gain + sh).astype(hb_ref.dtype)
        dgate = gs * jnp.sum(dxo * f_ref[...].astype(F32), axis=0, keepdims=True)
        df = ((gs * gt) * dxo).astype(MXU_DTYPE)
        dfb_ref[...] = df
        dh = jnp.zeros((tm, D), F32)
        for hf in range(2):
            lo, hi = hf * HALF, (hf + 1) * HALF
            dact = lax.dot_general(df, w2_ref[lo:hi, :], NT_DIMS, preferred_element_type=F32)
            g = a_ref[:, lo:hi].astype(F32)
            up = a_ref[:, DFF + lo:DFF + hi].astype(F32)
            sg = jax.nn.sigmoid(g)
            si = g * sg
            act_ref[:, lo:hi] = (si * up).astype(act_ref.dtype)
            dg = (dact * up * (sg * (1.0 + g * (1.0 - sg)))).astype(MXU_DTYPE)
            dup = (dact * si).astype(MXU_DTYPE)
            da_ref[:, lo:hi] = dg
            da_ref[:, DFF + lo:DFF + hi] = dup
            dh = dh + lax.dot_general(dg, w1_ref[hf], NT_DIMS, preferred_element_type=F32)
            dh = dh + lax.dot_general(dup, w1_ref[2 + hf], NT_DIMS, preferred_element_type=F32)
        dx, dshift, dscale, dng = _ada_bwd(dh, xhat, r, gain, ng, sc)
        dxi_ref[...] = dxo + dx
        _acc_rows(sums_ref, pl.program_id(0) == 0, (dshift, dscale, dng, dgate))

    return _pcall(
        body, name, (S // tm,),
        [pl.BlockSpec((tm, D), lambda i: (i, 0)),
         pl.BlockSpec((tm, D), lambda i: (i, 0)),
         pl.BlockSpec((tm, 2 * DFF), lambda i: (i, 0)),
         pl.BlockSpec((tm, D), lambda i: (i, 0)),
         pl.BlockSpec((SUBLANES, D), lambda i: (0, 0)),
         pl.BlockSpec((NCHIP, D, HALF), lambda i: (0, 0, 0), pipeline_mode=pl.Buffered(1)),
         pl.BlockSpec((DFF, D), lambda i: (0, 0), pipeline_mode=pl.Buffered(1))],
        [pl.BlockSpec((tm, D), lambda i: (i, 0)),
         pl.BlockSpec((tm, D), lambda i: (i, 0)),
         pl.BlockSpec((tm, D), lambda i: (i, 0)),
         pl.BlockSpec((tm, DFF), lambda i: (i, 0)),
         pl.BlockSpec((tm, 2 * DFF), lambda i: (i, 0)),
         pl.BlockSpec((SUBLANES, D), lambda i: (0, 0))],
        [jax.ShapeDtypeStruct((S, D), F32),
         jax.ShapeDtypeStruct((S, D), MXU_DTYPE),
         jax.ShapeDtypeStruct((S, D), MXU_DTYPE),
         jax.ShapeDtypeStruct((S, DFF), MXU_DTYPE),
         jax.ShapeDtypeStruct((S, 2 * DFF), MXU_DTYPE),
         jax.ShapeDtypeStruct((SUBLANES, D), F32)],
        ("arbitrary",), (dxo, x, a, f, vec, w1p, w2), carry)


def wgrad(a, b, kt, nt, name, carry=None):
    T, K = a.shape
    N = b.shape[1]
    pk, pn = K // kt, N // nt
    assert pk == 1 or pn == 1
    tt = _row_tile(T, 2048)
    steps = T // tt

    def body(a_ref, b_ref, o_ref):
        @pl.when(pl.program_id(1) == 0)
        def _():
            o_ref[...] = jnp.zeros_like(o_ref)
        o_ref[...] += lax.dot_general(a_ref[...], b_ref[...], TN_DIMS, preferred_element_type=F32)

    a_map = (lambda p, t: (t, p)) if pk > 1 else (lambda p, t: (t, 0))
    b_map = (lambda p, t: (t, p)) if pn > 1 else (lambda p, t: (t, 0))
    (out,), got = _pcall(
        body, name, (pk * pn, steps),
        [pl.BlockSpec((tt, kt), a_map), pl.BlockSpec((tt, nt), b_map)],
        [pl.BlockSpec((None, kt, nt), lambda p, t: (p, 0, 0))],
        [jax.ShapeDtypeStruct((pk * pn, kt, nt), F32)], ("arbitrary", "arbitrary"), (a, b), carry)
    return out, got


def _head_masks(rows):
    lane = lax.broadcasted_iota(jnp.int32, (rows, LANES), 1)
    return lane < HD


def _pair_stat(x, m_a):
    s_a = jnp.sum(jnp.where(m_a, x, 0.0), axis=1, keepdims=True)
    s_b = jnp.sum(jnp.where(m_a, 0.0, x), axis=1, keepdims=True)
    return s_a, s_b


def mixer_in(x, vec, winp, gvec, name):
    S = x.shape[0]
    tm = _row_tile(S, 512)
    pc = INC // NCHIP

    def body(x_ref, vec_ref, w_ref, g_ref, proj_ref, hb_ref, qn_ref, kn_ref, v_ref, qkv_ref):
        xt = x_ref[...]
        xhat, _, gain, _, _, sh, _ = _ada(xt, vec_ref)
        h = (xhat * gain + sh).astype(MXU_DTYPE)
        hb_ref[...] = h
        for j in range(NCHIP):
            piece = jnp.dot(h, w_ref[j], preferred_element_type=F32)
            proj_ref[:, j * pc:(j + 1) * pc] = piece.astype(proj_ref.dtype)
            if (j + 1) * pc <= 3 * AW:
                qkv_ref[:, j * pc:(j + 1) * pc] = piece
        m_a = _head_masks(tm)
        for which, dst in ((0, qn_ref), (1, kn_ref)):
            for p in range(AW // LANES):
                lo = which * AW + p * LANES
                xp = qkv_ref[:, lo:lo + LANES]
                s_a, s_b = _pair_stat(xp * xp, m_a)
                rr = jnp.where(m_a, lax.rsqrt(s_a * (1.0 / HD) + EPS), lax.rsqrt(s_b * (1.0 / HD) + EPS))
                gp = g_ref[which:which + 1, p * LANES:(p + 1) * LANES]
                dst[:, p * LANES:(p + 1) * LANES] = (xp * rr * gp).astype(dst.dtype)
        v_ref[...] = qkv_ref[:, 2 * AW:3 * AW]

    assert 2 * pc == 3 * AW
    return pl.pallas_call(
        body, name=name, grid=(S // tm,), scratch_shapes=[pltpu.VMEM((tm, 3 * AW), F32)],
        in_specs=[pl.BlockSpec((tm, D), lambda i: (i, 0)),
                  pl.BlockSpec((SUBLANES, D), lambda i: (0, 0)),
                  pl.BlockSpec((NCHIP, D, pc), lambda i: (0, 0, 0), pipeline_mode=pl.Buffered(1)),
                  pl.BlockSpec((SUBLANES, AW), lambda i: (0, 0))],
        out_specs=[pl.BlockSpec((tm, INC), lambda i: (i, 0)),
                   pl.BlockSpec((tm, D), lambda i: (i, 0)),
                   pl.BlockSpec((tm, AW), lambda i: (i, 0)),
                   pl.BlockSpec((tm, AW), lambda i: (i, 0)),
                   pl.BlockSpec((tm, AW), lambda i: (i, 0))],
        out_shape=[jax.ShapeDtypeStruct((S, INC), ACT_DTYPE),
                   jax.ShapeDtypeStruct((S, D), MXU_DTYPE),
                   jax.ShapeDtypeStruct((S, AW), F32),
                   jax.ShapeDtypeStruct((S, AW), F32),
                   jax.ShapeDtypeStruct((S, AW), F32)],
        compiler_params=_params(("arbitrary",)),
    )(x, vec, winp, gvec)


def _band_masks(ncol):
    row = lax.broadcasted_iota(jnp.int32, (2 * QBLK, ncol), 0) & (QBLK - 1)
    col = lax.broadcasted_iota(jnp.int32, (2 * QBLK, ncol), 1)
    return row, col


def _stack_heads(t, m_a):
    zero = jnp.zeros_like(t)
    return jnp.concatenate([jnp.where(m_a, t, zero), jnp.where(m_a, zero, t)], axis=0)


class _AttnLayout:
    def __init__(self, d, S):
        self.d, self.S = d, S
        self.qb = max(1, min(ATTN_QBLOCKS, ATTN_CHUNK_ROWS // (QBLK * d)))
        self.nres = d
        self.nchunk = S // (self.qb * QBLK * d)
        self.grid = (AW // LANES, self.nchunk)
        self.unroll = max(1, min(d, ATTN_INTERLEAVE // self.qb))

    def _spec(self, blocks, row_of):
        return pl.BlockSpec((blocks * QBLK * self.d, LANES), lambda hp, j: (row_of(j), hp))

    def cur(self, chunk_of):
        return self._spec(self.qb, chunk_of)

    def prev(self, chunk_of):
        return self._spec(1, lambda j: jnp.maximum(chunk_of(j) * self.qb - 1, 0))

    def idx(self, b, r):
        if self.d == 1:
            return (pl.ds(b * QBLK, QBLK), slice(None))
        return (pl.ds(b * QBLK * self.d + r, QBLK, stride=self.d), slice(None))

    def per_residue(self, fn):
        if self.nres == 1:
            fn(0)
        else:
            def step(it, carry):
                for k in range(self.unroll):
                    fn(it * self.unroll + k)
                return carry
            lax.fori_loop(0, self.nres // self.unroll, step, 0)


def attn_fwd(qn, kn, v, d, name, carry=None):
    S = qn.shape[0]
    lay = _AttnLayout(d, S)
    qb = lay.qb

    def body(q_ref, kc_ref, kp_ref, vc_ref, vp_ref, o_ref, lse_ref):
        i = pl.program_id(1)
        m_a = _head_masks(QBLK)
        row, col = _band_masks(2 * QBLK)
        dist = row + QBLK - col
        band = (dist >= 0) & (dist <= QBLK)
        first = band & ((i > 0) | (col >= QBLK))

        def residue(r):
            kt = [kp_ref[lay.idx(0, r)].astype(MXU_DTYPE)]
            vt = [vp_ref[lay.idx(0, r)].astype(MXU_DTYPE)]
            for b in range(qb):
                kt.append(kc_ref[lay.idx(b, r)].astype(MXU_DTYPE))
                vt.append(vc_ref[lay.idx(b, r)].astype(MXU_DTYPE))
            for b in range(qb):
                rows = lay.idx(b, r)
                q = (q_ref[rows] * (HD ** -0.5)).astype(MXU_DTYPE)
                kcat = jnp.concatenate([kt[b], kt[b + 1]], axis=0)
                vcat = jnp.concatenate([vt[b], vt[b + 1]], axis=0)
                mask = first if b == 0 else band
                s = lax.dot_general(_stack_heads(q, m_a), kcat, NT_DIMS, preferred_element_type=F32)
                s = jnp.where(mask, s, NEG)
                m = jnp.max(s, axis=1, keepdims=True)
                p = jnp.exp(s - m)
                l = jnp.sum(p, axis=1, keepdims=True)
                o = jnp.dot(p.astype(MXU_DTYPE), vcat, preferred_element_type=F32) / l
                lse = jnp.broadcast_to(m + jnp.log(l), (2 * QBLK, LANES))
                o_ref[rows] = jnp.where(m_a, o[:QBLK], o[QBLK:])
                lse_ref[rows] = jnp.where(m_a, lse[:QBLK], lse[QBLK:])

        lay.per_residue(residue)

    cur, prev = lay.cur(lambda j: j), lay.prev(lambda j: j)
    return _pcall(body, name, lay.grid, [cur, cur, prev, cur, prev], [cur, cur],
                  [jax.ShapeDtypeStruct((S, AW), F32)] * 2, ("arbitrary", "arbitrary"), (qn, kn, kn, v, v), carry)


def _both_heads(t, m_a):
    other = pltpu.roll(t, HD, 1)
    return jnp.concatenate([jnp.where(m_a, t, other), jnp.where(m_a, other, t)], axis=0)


def attn_bwd(qn, kn, v, dycat, lse, delta, d, name, carry=None):
    S = qn.shape[0]
    lay = _AttnLayout(d, S)
    qb, nchunk = lay.qb, lay.nchunk

    def body(q_ref, kc_ref, kp_ref, vc_ref, vp_ref, do_ref, lse_ref, dl_ref,
             dq_ref, dk_ref, dv_ref, ck_ref, cv_ref):
        j = pl.program_id(1)
        i = nchunk - 1 - j
        m_a = _head_masks(QBLK)
        row, col = _band_masks(2 * QBLK)
        dist = row + QBLK - col
        band = (dist >= 0) & (dist <= QBLK)
        first = band & ((i > 0) | (col >= QBLK))

        def residue(r):
            def tiles(ref, cast):
                out = [ref[lay.idx(b, r)] for b in range(qb)]
                return [t.astype(MXU_DTYPE) for t in out] if cast else out

            def ktiles(cur_ref, prev_ref):
                return [prev_ref[lay.idx(0, r)].astype(MXU_DTYPE)] + tiles(cur_ref, True)

            qt = [(t * (HD ** -0.5)).astype(MXU_DTYPE) for t in tiles(q_ref, False)]
            dot_ = tiles(do_ref, True)
            lse_t = tiles(lse_ref, False)
            dl_t = tiles(dl_ref, False)
            kt = ktiles(kc_ref, kp_ref)
            vt = ktiles(vc_ref, vp_ref)
            dk_acc = [jnp.zeros((QBLK, LANES), F32) for _ in range(qb)]
            dv_acc = [jnp.zeros((QBLK, LANES), F32) for _ in range(qb)]
            crow = pl.ds(0, QBLK) if lay.nres == 1 else pl.ds(pl.multiple_of(r * QBLK, QBLK), QBLK)
            dk_acc[qb - 1] = jnp.where(j > 0, ck_ref[crow, :], 0.0)
            dv_acc[qb - 1] = jnp.where(j > 0, cv_ref[crow, :], 0.0)
            for x in range(qb):
                kcat = jnp.concatenate([kt[x], kt[x + 1]], axis=0)
                vcat = jnp.concatenate([vt[x], vt[x + 1]], axis=0)
                q2 = _stack_heads(qt[x], m_a)
                do2 = _stack_heads(dot_[x], m_a)
                lse2 = _both_heads(lse_t[x], m_a)
                dl2 = _both_heads(dl_t[x], m_a)
                lse2 = jnp.concatenate([lse2, lse2], axis=1)
                dl2 = jnp.concatenate([dl2, dl2], axis=1)
                s = lax.dot_general(q2, kcat, NT_DIMS, preferred_element_type=F32)
                p = jnp.exp(jnp.where(first if x == 0 else band, s, NEG) - lse2)
                dp = lax.dot_general(do2, vcat, NT_DIMS, preferred_element_type=F32)
                ds = p * (dp - dl2)
                dq = jnp.dot(ds.astype(MXU_DTYPE), kcat, preferred_element_type=F32)
                dq_ref[lay.idx(x, r)] = jnp.where(m_a, dq[:QBLK], dq[QBLK:]) * (HD ** -0.5)
                dk = jnp.dot(ds.T.astype(MXU_DTYPE), q2, preferred_element_type=F32)
                dv = jnp.dot(p.T.astype(MXU_DTYPE), do2, preferred_element_type=F32)
                if x == 0:
                    ck_ref[crow, :] = dk[:QBLK]
                    cv_ref[crow, :] = dv[:QBLK]
                else:
                    dk_acc[x - 1] = dk_acc[x - 1] + dk[:QBLK]
                    dv_acc[x - 1] = dv_acc[x - 1] + dv[:QBLK]
                dk_acc[x] = dk_acc[x] + dk[QBLK:]
                dv_acc[x] = dv_acc[x] + dv[QBLK:]
            for kb in range(qb):
                dk_ref[lay.idx(kb, r)] = dk_acc[kb]
                dv_ref[lay.idx(kb, r)] = dv_acc[kb]

        lay.per_residue(residue)

    cur, prev = lay.cur(lambda j: nchunk - 1 - j), lay.prev(lambda j: nchunk - 1 - j)
    carried = pltpu.VMEM((lay.nres * QBLK, LANES), F32)
    return _pcall(
        body, name, lay.grid, [cur, cur, prev, cur, prev, cur, cur, cur], [cur, cur, cur],
        [jax.ShapeDtypeStruct((S, AW), F32)] * 3, ("arbitrary", "arbitrary"),
        (qn, kn, kn, v, v, dycat, lse, delta), carry, scratch=[carried, carried])


def _shift_down(x, halo_prev, k, row):
    tm = x.shape[0]
    tail = jnp.concatenate([pltpu.roll(halo_prev, k, 0), jnp.zeros((tm - SUBLANES, x.shape[1]), x.dtype)], axis=0)
    return jnp.where(row < k, tail, pltpu.roll(x, k, 0))


def _shift_up(x, halo_next, k, row):
    tm = x.shape[0]
    head = jnp.concatenate([jnp.zeros((tm - SUBLANES, x.shape[1]), x.dtype), pltpu.roll(halo_next, SUBLANES - k, 0)], axis=0)
    return jnp.where(row >= tm - k, head, pltpu.roll(x, tm - k, 0))


def _conv_fwd(cu, halo_cu, cw_ref, row):
    u1 = _shift_down(cu, halo_cu, 1, row)
    u2 = _shift_down(cu, halo_cu, 2, row)
    cv = cw_ref[0:1, :] * u2 + cw_ref[1:2, :] * u1 + cw_ref[2:3, :] * cu + cw_ref[3:4, :]
    return cv, u1, u2


def combine_conv(os_, lses, proj, cw, name, carry=None):
    S = proj.shape[0]
    tm = _row_tile(S, 512)
    hb = tm // HALO_ROWS

    def body(o1, o2, o3, l1, l2, l3, pc_ref, ph_ref, cw_ref, ycat_ref, lse_ref):
        i = pl.program_id(0)
        for p in range(AW // LANES):
            cs = slice(p * LANES, (p + 1) * LANES)
            ls = [l[:, cs] for l in (l1, l2, l3)]
            mx = jnp.maximum(jnp.maximum(ls[0], ls[1]), ls[2])
            t = mx + jnp.log(jnp.exp(ls[0] - mx) + jnp.exp(ls[1] - mx) + jnp.exp(ls[2] - mx))
            lse_ref[:, cs] = t
            acc = jnp.zeros((tm, LANES), F32)
            for l, o in zip(ls, (o1, o2, o3)):
                acc = acc + jnp.exp(l - t) * o[:, cs]
            ycat_ref[:, cs] = acc.astype(ycat_ref.dtype)
        row = lax.broadcasted_iota(jnp.int32, (tm, CW), 0)
        gb, gc, u = (pc_ref[:, k * CW:(k + 1) * CW].astype(F32) for k in range(3))
        ph = ph_ref[...].astype(F32)[HALO_ROWS - SUBLANES:]
        halo_cu = jnp.where(i > 0, ph[:, CW:2 * CW] * ph[:, 2 * CW:3 * CW], 0.0)
        cv, _, _ = _conv_fwd(gc * u, halo_cu, cw_ref, row)
        ycat_ref[:, AW:AW + CW] = (gb * cv).astype(ycat_ref.dtype)

    ot = pl.BlockSpec((tm, AW), lambda i: (i, 0))
    return _pcall(
        body, name, (S // tm,),
        [ot] * 6 + [pl.BlockSpec((tm, 3 * CW), lambda i: (i, 1)),
                    pl.BlockSpec((HALO_ROWS, 3 * CW), lambda i: (jnp.maximum(i * hb - 1, 0), 1)),
                    pl.BlockSpec((SUBLANES, CW), lambda i: (0, 0))],
        [pl.BlockSpec((tm, D), lambda i: (i, 0)), ot],
        [jax.ShapeDtypeStruct((S, D), ACT_DTYPE), jax.ShapeDtypeStruct((S, AW), F32)],
        ("arbitrary",), (*os_, *lses, proj, proj, cw), carry)


def out_proj(ycat, x, vec, wout, name):
    S = x.shape[0]
    tm = _row_tile(S, 512)

    def body(yc_ref, x_ref, vec_ref, w_ref, xn_ref, y_ref):
        y = jnp.dot(yc_ref[...].astype(MXU_DTYPE), w_ref[...], preferred_element_type=F32)
        xn_ref[...] = x_ref[...] + vec_ref[3:4, :] * y
        y_ref[...] = y.astype(y_ref.dtype)

    t = pl.BlockSpec((tm, D), lambda i: (i, 0))
    return pl.pallas_call(
        body, name=name, grid=(S // tm,),
        in_specs=[t, t, pl.BlockSpec((SUBLANES, D), lambda i: (0, 0)),
                  pl.BlockSpec((D, D), lambda i: (0, 0))],
        out_specs=[t, t],
        out_shape=[jax.ShapeDtypeStruct((S, D), F32), jax.ShapeDtypeStruct((S, D), ACT_DTYPE)],
        compiler_params=_params(("arbitrary",)),
    )(ycat, x, vec, wout)


def out_proj_bwd(dxo, y, ycat, vec, wout, name, carry=None):
    S = dxo.shape[0]
    tm = _row_tile(S, 512)

    def body(dxo_ref, y_ref, yc_ref, vec_ref, w_ref, dyb_ref, dyc_ref, dl_ref, sums_ref):
        dxo = dxo_ref[...]
        dgate = jnp.sum(dxo * y_ref[...].astype(F32), axis=0, keepdims=True)
        dy = (vec_ref[3:4, :] * dxo).astype(MXU_DTYPE)
        dyb_ref[...] = dy
        dyc_ref[...] = lax.dot_general(dy, w_ref[...], NT_DIMS, preferred_element_type=F32)
        m_a = _head_masks(tm)
        for p in range(AW // LANES):
            cs = slice(p * LANES, (p + 1) * LANES)
            s_a, s_b = _pair_stat(dyc_ref[:, cs] * yc_ref[:, cs].astype(F32), m_a)
            dl_ref[:, cs] = jnp.where(m_a, s_a, s_b)
        _acc_rows(sums_ref, pl.program_id(0) == 0, (dgate,))

    t = pl.BlockSpec((tm, D), lambda i: (i, 0))
    at = pl.BlockSpec((tm, AW), lambda i: (i, 0))
    return _pcall(
        body, name, (S // tm,),
        [t, t, t, pl.BlockSpec((SUBLANES, D), lambda i: (0, 0)), pl.BlockSpec((D, D), lambda i: (0, 0))],
        [t, t, at, pl.BlockSpec((SUBLANES, D), lambda i: (0, 0))],
        [jax.ShapeDtypeStruct((S, D), MXU_DTYPE), jax.ShapeDtypeStruct((S, D), F32),
         jax.ShapeDtypeStruct((S, AW), F32), jax.ShapeDtypeStruct((SUBLANES, D), F32)],
        ("arbitrary",), (dxo, y, ycat, vec, wout), carry)


def mixer_mid_bwd(dqs, dks, dvs, proj, dycat, gvec, cw, name, carry=None):
    S = proj.shape[0]
    tm = _row_tile(S, 512)
    hb = tm // SUBLANES
    hp = tm // HALO_ROWS
    nsl = S // SUBLANES
    ntile = S // tm

    def body(dq1, dq2, dq3, dk1, dk2, dk3, dv1, dv2, dv3, pr_ref, pp_ref, pn_ref, dyc_ref, dyn_ref,
             g_ref, cw_ref, dp_ref, sums_ref):
        i = pl.program_id(0)
        m_a = _head_masks(tm)
        gsum = []
        for which, parts in ((0, (dq1, dq2, dq3)), (1, (dk1, dk2, dk3))):
            acc_g = []
            for p in range(AW // LANES):
                lo = which * AW + p * LANES
                cs = slice(p * LANES, (p + 1) * LANES)
                xp = pr_ref[:, lo:lo + LANES].astype(F32)
                s_a, s_b = _pair_stat(xp * xp, m_a)
                rr = jnp.where(m_a, lax.rsqrt(s_a * (1.0 / HD) + EPS), lax.rsqrt(s_b * (1.0 / HD) + EPS))
                xh = xp * rr
                dn = parts[0][:, cs] + parts[1][:, cs] + parts[2][:, cs]
                acc_g.append(jnp.sum(dn * xh, axis=0, keepdims=True))
                t = dn * g_ref[which:which + 1, cs]
                t_a, t_b = _pair_stat(t * xh, m_a)
                mean = jnp.where(m_a, t_a, t_b) * (1.0 / HD)
                dp_ref[:, lo:lo + LANES] = (rr * (t - xh * mean)).astype(dp_ref.dtype)
            gsum.append(jnp.concatenate(acc_g, axis=1))
        dp_ref[:, 2 * AW:3 * AW] = (dv1[...] + dv2[...] + dv3[...]).astype(dp_ref.dtype)
        row = lax.broadcasted_iota(jnp.int32, (tm, CW), 0)
        base = 3 * AW
        gb, gc, u = (pr_ref[:, base + k * CW:base + (k + 1) * CW].astype(F32) for k in range(3))
        cu = gc * u
        pp = pp_ref[...].astype(F32)[HALO_ROWS - SUBLANES:]
        halo_cu = jnp.where(i > 0, pp[:, CW:2 * CW] * pp[:, 2 * CW:3 * CW], 0.0)
        cv, u1, u2 = _conv_fwd(cu, halo_cu, cw_ref, row)
        dyc = dyc_ref[...]
        dp_ref[:, base:base + CW] = (dyc * cv).astype(dp_ref.dtype)
        dcv = dyc * gb
        gb_next = pn_ref[:, 0:CW].astype(F32)[:SUBLANES]
        halo_dcv = jnp.where(i < ntile - 1, dyn_ref[...] * gb_next, 0.0)
        d1 = _shift_up(dcv, halo_dcv, 1, row)
        d2 = _shift_up(dcv, halo_dcv, 2, row)
        dcu = cw_ref[2:3, :] * dcv + cw_ref[1:2, :] * d1 + cw_ref[0:1, :] * d2
        dp_ref[:, base + CW:base + 2 * CW] = (dcu * u).astype(dp_ref.dtype)
        dp_ref[:, base + 2 * CW:base + 3 * CW] = (dcu * gc).astype(dp_ref.dtype)
        rows = (gsum[0], gsum[1],
                jnp.sum(dcv * u2, axis=0, keepdims=True), jnp.sum(dcv * u1, axis=0, keepdims=True),
                jnp.sum(dcv * cu, axis=0, keepdims=True), jnp.sum(dcv, axis=0, keepdims=True))
        _acc_rows(sums_ref, i == 0, rows)

    at = pl.BlockSpec((tm, AW), lambda i: (i, 0))
    return _pcall(
        body, name, (ntile,),
        [at] * 9 + [
            pl.BlockSpec((tm, INC), lambda i: (i, 0)),
            pl.BlockSpec((HALO_ROWS, 3 * CW), lambda i: (jnp.maximum(i * hp - 1, 0), 1)),
            pl.BlockSpec((HALO_ROWS, 3 * CW), lambda i: (jnp.minimum((i + 1) * hp, S // HALO_ROWS - 1), 1)),
            pl.BlockSpec((tm, CW), lambda i: (i, 1)),
            pl.BlockSpec((SUBLANES, CW), lambda i: (jnp.minimum((i + 1) * hb, nsl - 1), 1)),
            pl.BlockSpec((SUBLANES, AW), lambda i: (0, 0)),
            pl.BlockSpec((SUBLANES, CW), lambda i: (0, 0))],
        [pl.BlockSpec((tm, INC), lambda i: (i, 0)), pl.BlockSpec((SUBLANES, AW), lambda i: (0, 0))],
        [jax.ShapeDtypeStruct((S, INC), MXU_DTYPE), jax.ShapeDtypeStruct((SUBLANES, AW), F32)],
        ("arbitrary",), (*dqs, *dks, *dvs, proj, proj, proj, dycat, dycat, gvec, cw), carry)


def mixer_in_bwd(dxo, x, dproj, vec, winp, name, carry=None):
    S = x.shape[0]
    tm = _row_tile(S, 512)
    pc = INC // NCHIP

    def body(dxo_ref, x_ref, dp_ref, vec_ref, w_ref, dxi_ref, sums_ref):
        xhat, r, gain, ng, sc, _, _ = _ada(x_ref[...], vec_ref)
        dh = jnp.zeros((tm, D), F32)
        for j in range(NCHIP):
            dh = dh + lax.dot_general(dp_ref[:, j * pc:(j + 1) * pc], w_ref[j], NT_DIMS, preferred_element_type=F32)
        dx, dshift, dscale, dng = _ada_bwd(dh, xhat, r, gain, ng, sc)
        dxi_ref[...] = dxo_ref[...] + dx
        _acc_rows(sums_ref, pl.program_id(0) == 0, (dshift, dscale, dng))

    t = pl.BlockSpec((tm, D), lambda i: (i, 0))
    return _pcall(
        body, name, (S // tm,),
        [t, t, pl.BlockSpec((tm, INC), lambda i: (i, 0)),
         pl.BlockSpec((SUBLANES, D), lambda i: (0, 0)),
         pl.BlockSpec((NCHIP, D, pc), lambda i: (0, 0, 0), pipeline_mode=pl.Buffered(1))],
        [t, pl.BlockSpec((SUBLANES, D), lambda i: (0, 0))],
        [jax.ShapeDtypeStruct((S, D), F32), jax.ShapeDtypeStruct((SUBLANES, D), F32)],
        ("arbitrary",), (dxo, x, dproj, vec, winp), carry)


def loss_head(xf, target, name):
    S = xf.shape[0]
    tm = _row_tile(S, 1024)

    def body(x_ref, t_ref, dy_ref, l_ref):
        diff = x_ref[...] - t_ref[...]
        dy_ref[...] = diff * (1.0 / D)
        part = jnp.sum(jnp.sum(diff * diff, axis=0, keepdims=True), axis=1, keepdims=True) * (0.5 / D)

        @pl.when(pl.program_id(0) == 0)
        def _():
            l_ref[...] = jnp.zeros_like(l_ref)
        l_ref[...] += jnp.broadcast_to(part, l_ref.shape)

    t = pl.BlockSpec((tm, D), lambda i: (i, 0))
    return pl.pallas_call(
        body, name=name, grid=(S // tm,),
        in_specs=[t, t],
        out_specs=[t, pl.BlockSpec((SUBLANES, LANES), lambda i: (0, 0))],
        out_shape=[jax.ShapeDtypeStruct((S, D), F32), jax.ShapeDtypeStruct((SUBLANES, LANES), F32)],
        compiler_params=_params(("arbitrary",)),
    )(xf, target)


def _vec(mod_l, ng_l, i):
    m = mod_l.reshape(3, 3, D)
    rows = jnp.stack([ng_l[i], m[i, 1], m[i, 0], m[i, 2]])
    return jnp.concatenate([rows, jnp.zeros((SUBLANES - 4, D), F32)], axis=0)


def local_step(x, target, mods, ngs, gvecs, cws, shards, w_first, cflag):
    saved = []
    weights = [dict(w1=[None, None], w2=[None, None]) for _ in range(2)]
    weights[0]["w1"][0], weights[0]["w2"][0] = w_first[0], w_first[1].reshape(DFF, D)
    h = x
    for l in range(2):
        w, sh = weights[l], shards[l]
        nxt = shards[l + 1] if l == 0 else None
        vecs = [_vec(mods[l], ngs[l], i) for i in range(3)]
        x0 = h
        (x1, a0, f0), (win, wout, w2b) = ffn_fwd(x0, vecs[0], w["w1"][0], w["w2"][0], 0.5, f"ffn_fwd_l{l}a",
                                                 carry=Carry("gather", [sh["win"], sh["wout"], sh["w2"][1]]))
        w["win"], w["wout"], w["w2"][1] = win, wout.reshape(D, D), w2b.reshape(DFF, D)
        proj, h1b, qn, kn, v = mixer_in(x1, vecs[1], w["win"], gvecs[l], f"mixer_in_l{l}")
        os_, lses, w1b = [], [], {}
        for d in DILATIONS:
            rows = {1: slice(0, D // 2), 16: slice(D // 2, D)}.get(d)
            carry = Carry("gather", [sh["w1"][1][rows]]) if rows else None
            (o, lse_d), w1b[d] = attn_fwd(qn, kn, v, d, f"attn_fwd_l{l}_d{d}", carry=carry)
            os_.append(o)
            lses.append(lse_d)
        w["w1"][1] = jnp.concatenate([w1b[1][0], w1b[16][0]], axis=1)
        (ycat, lse), got = combine_conv(os_, lses, proj, cws[l], f"combine_conv_l{l}",
                                        carry=Carry("gather", [nxt["w2"][0]]) if nxt else None)
        if nxt:
            weights[1]["w2"][0] = got[0].reshape(DFF, D)
        x2, y = out_proj(ycat, x1, vecs[1], w["wout"], f"out_proj_l{l}")
        (x3, a2, f2), got = ffn_fwd(x2, vecs[2], w["w1"][1], w["w2"][1], 0.5, f"ffn_fwd_l{l}b",
                                    carry=Carry("gather", [nxt["w1"][0]]) if nxt else None)
        if nxt:
            weights[1]["w1"][0] = got[0]
        saved.append(dict(vecs=vecs, x0=x0, a0=a0, f0=f0, x1=x1, proj=proj, h1b=h1b, qn=qn, kn=kn, v=v,
                          ycat=ycat, lse=lse, y=y, x2=x2, a2=a2, f2=f2))
        h = x3
    dx, loss_blk = loss_head(h, target, "loss_head")
    sums, totals, g_prev = [None, None], [None, None], None
    w2r = DFF // NCHIP
    for l in (1, 0):
        w, s = weights[l], saved[l]
        vecs = s["vecs"]
        ride = g_prev is not None
        own = l == 0
        mine, other = [None] * 6, [None] * 6

        def half_sum(group, recv, k0):
            return [add_half(g, r, cflag, f"add_sibling_l{l}_{k0 + j}") for j, (g, r) in enumerate(zip(group, recv))]

        def chip_sum(landed, k0):
            return [sum_chips(t, f"sum_chips_l{l}_{k0 + j}") for j, t in enumerate(landed)]

        (dx, hb, dfb, act, da, sums2), got = ffn_bwd(
            dx, s["x2"], s["a2"], s["f2"], vecs[2], w["w1"][1], w["w2"][1], 0.5, f"ffn_bwd_l{l}b",
            carry=Carry("swap_halves", g_prev) if ride else None)
        dw1b, _ = wgrad(hb, da, D, HALF, f"wgrad_w1_l{l}b")
        dw2b, _ = wgrad(act, dfb, HALF, D, f"wgrad_w2_l{l}b")
        if ride:
            wire = [add_half(g_prev[k], got[k], cflag, f"add_sibling_l{l + 1}_{k}") for k in range(6)]
        g_ffn_b = [dw1b, dw2b.reshape(NCHIP, w2r, D)]
        (dyb, dycat, delta, sums_o), got = out_proj_bwd(
            dx, s["y"], s["ycat"], vecs[1], w["wout"], f"out_proj_bwd_l{l}",
            carry=Carry("swap_halves", g_ffn_b) if own else None)
        dwout, _ = wgrad(s["ycat"].astype(MXU_DTYPE), dyb, D // 2, D, f"wgrad_wout_l{l}")
        if own:
            wire_ffn_b = half_sum(g_ffn_b, got, 4)
        dqs, dks, dvs, landed = [], [], [], {}
        for d in DILATIONS:
            carry = None
            if ride and d == 1:
                carry = Carry("scatter", wire[3:])
            if ride and d == 16:
                carry = Carry("scatter", wire[:3])
            if own and d == 4:
                carry = Carry("scatter", wire_ffn_b)
            (dq, dk, dv), landed[d] = attn_bwd(s["qn"], s["kn"], s["v"], dycat, s["lse"], delta, d,
                                               f"attn_bwd_l{l}_d{d}", carry=carry)
            dqs.append(dq)
            dks.append(dk)
            dvs.append(dv)
        if ride:
            tot = [sum_chips(t, f"sum_chips_l{l + 1}_{k}") for k, t in enumerate(list(landed[16]) + list(landed[1]))]
        if own:
            mine[4:6] = chip_sum(landed[4], 4)
        ready = (tot if ride else []) + (mine[4:6] if own else [])
        (dproj, sums_m), got = mixer_mid_bwd(dqs, dks, dvs, s["proj"], dycat, gvecs[l], cws[l], f"mixer_mid_bwd_l{l}",
                                             carry=Carry("swap", ready) if ready else None)
        if ride:
            totals[l + 1] = (tot, list(got[:6]))
        if own:
            other[4:6] = list(got[-2:])
        dwin, _ = wgrad(s["h1b"], dproj, D, INC // NCHIP, f"wgrad_win_l{l}")
        g_mixer = [dwin, dwout.reshape(NCHIP, D // NCHIP, D)]
        (dx, sums1), got = mixer_in_bwd(dx, s["x1"], dproj, vecs[1], w["win"], f"mixer_in_bwd_l{l}",
                                        carry=Carry("swap_halves", g_mixer) if own else None)
        if own:
            wire_mixer = half_sum(g_mixer, got, 2)
        (dx, hb, dfb, act, da, sums0), _ = ffn_bwd(
            dx, s["x0"], s["a0"], s["f0"], vecs[0], w["w1"][0], w["w2"][0], 0.5, f"ffn_bwd_l{l}a")
        dw1a, got = wgrad(hb, da, D, HALF, f"wgrad_w1_l{l}a", carry=Carry("scatter", wire_mixer) if own else None)
        if own:
            mine[2:4] = chip_sum(got, 2)
        dw2a, got = wgrad(act, dfb, HALF, D, f"wgrad_w2_l{l}a", carry=Carry("swap", mine[2:4]) if own else None)
        g_ffn_a = [dw1a, dw2a.reshape(NCHIP, w2r, D)]
        if own:
            other[2:4] = list(got)
            wire_ffn_a = half_sum(g_ffn_a, run_carry(Carry("swap_halves", g_ffn_a), "swap_halves_tail"), 0)
            mine[0:2] = chip_sum(run_carry(Carry("scatter", wire_ffn_a), "scatter_grads_tail"), 0)
            other[0:2] = list(run_carry(Carry("swap", mine[0:2]), "swap_totals_tail"))
            totals[l] = (mine, other)
        g_prev = g_ffn_a + g_mixer + g_ffn_b
        sums[l] = (sums0, sums1, sums_o, sums2, sums_m)
    return loss_blk, dx, totals, sums


def small_all_gather(blk, name):
    m_per, n = blk.shape

    def body(x_ref, out_ref, send_sems, recv_sems, local_sem):
        x, y, c = _here()
        me, sibling = (x, y, c), (x, y, 1 - c)
        chips = [(1 - x, y), (x, 1 - y), (1 - x, 1 - y)]

        def rows(px, py, pc):
            return out_ref.at[pl.ds((4 * px + 2 * py + pc) * m_per, m_per), :]

        def copy(k, block, to, src=None):
            return pltpu.make_async_remote_copy(
                src_ref=rows(*block) if src is None else src, dst_ref=rows(*block),
                send_sem=send_sems.at[k], recv_sem=recv_sems.at[k], device_id=to, device_id_type=MESH)

        mine = pltpu.make_async_copy(x_ref, rows(*me), local_sem)
        mine.start()
        first = [copy(0, me, sibling, src=x_ref)]
        first += [copy(1 + j, me, (*chip, c), src=x_ref) for j, chip in enumerate(chips)]
        for cp in first:
            cp.start()
        passed = [copy(4 + j, (*chip, c), sibling) for j, chip in enumerate(chips)]
        for j, chip in enumerate(chips):
            copy(1 + j, (*chip, c), me).wait_recv()
            passed[j].start()
        copy(0, sibling, me).wait_recv()
        for j, chip in enumerate(chips):
            copy(4 + j, (*chip, 1 - c), me).wait_recv()
        for cp in first + passed:
            cp.wait_send()
        mine.wait()

    return pl.pallas_call(
        body, name=name,
        out_shape=jax.ShapeDtypeStruct((NDEV * m_per, n), blk.dtype),
        in_specs=[pl.BlockSpec(memory_space=pltpu.VMEM)],
        out_specs=pl.BlockSpec(memory_space=pltpu.VMEM),
        scratch_shapes=[pltpu.SemaphoreType.DMA((7,)), pltpu.SemaphoreType.DMA((7,)), pltpu.SemaphoreType.DMA],
        compiler_params=pltpu.CompilerParams(vmem_limit_bytes=VMEM_LIMIT),
    )(blk)


EW_BLOCK_BYTES = 1 << 20


def _ew_rows(rows, cols, refs=8):
    want = max(16, EW_BLOCK_BYTES * (2 if refs <= 4 else 1) // (4 * cols))
    best = None
    for t in range(16, rows + 1, 16):
        if rows % t == 0 and t <= want:
            best = t
    return best if best is not None else rows


def add_half(g, recv, cflag, name):
    pieces, r, cols = g.shape
    r2 = r // 2
    tr = _ew_rows(r2, cols, refs=3)
    nt = r2 // tr

    def body(c_ref, g_ref, r_ref, o_ref):
        o_ref[...] = (g_ref[...] + r_ref[...]).astype(o_ref.dtype)

    half = pl.BlockSpec((None, tr, cols), lambda j, i, c_ref: (j, i, 0))
    return pl.pallas_call(
        body, name=name,
        grid_spec=pltpu.PrefetchScalarGridSpec(
            num_scalar_prefetch=1, grid=(pieces, nt),
            in_specs=[pl.BlockSpec((None, tr, cols), lambda j, i, c_ref: (j, c_ref[0] * nt + i, 0)), half],
            out_specs=half),
        out_shape=jax.ShapeDtypeStruct((pieces, r2, cols), WIRE_DTYPE),
        compiler_params=_params(("arbitrary", "arbitrary")),
    )(cflag, g, recv)


def sum_chips(recv, name):
    _, r, cols = recv.shape
    tr = _ew_rows(r, cols, refs=3)

    def body(r_ref, o_ref):
        acc = r_ref[0].astype(F32)
        for k in range(1, NCHIP):
            acc = acc + r_ref[k].astype(F32)
        o_ref[...] = acc

    return pl.pallas_call(
        body, name=name, grid=(r // tr,),
        in_specs=[pl.BlockSpec((NCHIP, tr, cols), lambda i: (0, i, 0))],
        out_specs=pl.BlockSpec((tr, cols), lambda i: (i, 0)),
        out_shape=jax.ShapeDtypeStruct((r, cols), F32),
        compiler_params=_params(("arbitrary",)),
    )(recv)


def sum_devices(rows8, name):
    def body(r_ref, o_ref):
        acc = r_ref[0:1, :]
        for k in range(1, NDEV):
            acc = acc + r_ref[k:k + 1, :]
        o_ref[...] = jnp.broadcast_to(acc, o_ref.shape)

    return pl.pallas_call(
        body, name=name, out_shape=jax.ShapeDtypeStruct(rows8.shape, F32),
        in_specs=[pl.BlockSpec(memory_space=pltpu.VMEM)], out_specs=pl.BlockSpec(memory_space=pltpu.VMEM),
        compiler_params=pltpu.CompilerParams(vmem_limit_bytes=VMEM_LIMIT),
    )(rows8)


def adamw(w, m, v, srcs, cflag, name, halves=False):
    planes, r, cols = w.shape
    rh = r // 2 if halves else r
    tr = _ew_rows(rh, cols)
    nth = rh // tr
    flat = [a for s in srcs for a in (s if halves else (s,))]
    ns = len(flat)
    per = ns // planes

    def body(c_ref, w_ref, m_ref, v_ref, *rest):
        s_refs, (g_ref, d_ref, mo_ref, vo_ref) = rest[:ns], rest[ns:]
        p, i = pl.program_id(0), pl.program_id(1)
        if halves:
            mine = jnp.logical_not(jnp.logical_xor(i >= nth, c_ref[0] == 1))
            blocks = [jnp.where(mine, s_refs[2 * k][...], s_refs[2 * k + 1][...]) for k in range(planes)]
        else:
            blocks = [s[...] for s in s_refs]
        g = blocks[0]
        for k in range(1, planes):
            g = jnp.where(p == k, blocks[k], g)
        g_ref[...] = g
        m_new = ADAM_B1 * m_ref[...] + (1.0 - ADAM_B1) * g
        v_new = ADAM_B2 * v_ref[...] + (1.0 - ADAM_B2) * (g * g)
        mo_ref[...] = m_new
        vo_ref[...] = v_new
        m_hat = m_new / (1.0 - ADAM_B1 ** ADAM_STEP)
        v_hat = v_new / (1.0 - ADAM_B2 ** ADAM_STEP)
        d_ref[...] = -ADAM_LR * (m_hat / (jnp.sqrt(v_hat) + ADAM_EPS) + ADAM_WD * w_ref[...])

    pt = pl.BlockSpec((None, tr, cols), lambda p, i: (p, i, 0))
    st = [pl.BlockSpec((tr, cols), functools.partial(lambda k, p, i: (jnp.where(p == k, i % nth, 0), 0), j // per))
          for j in range(ns)]
    return pl.pallas_call(
        body, name=name, grid=(planes, r // tr),
        in_specs=[pl.BlockSpec(memory_space=pltpu.SMEM), pt, pt, pt] + st,
        out_specs=[pt] * 4,
        out_shape=[jax.ShapeDtypeStruct(w.shape, F32)] * 4,
        compiler_params=_params(("arbitrary", "arbitrary")),
    )(cflag, w, m, v, *flat)


ADA_COLS = 9 * D // NCHIP


def mod_fwd(c_all, w_ada, b_shard, name):
    def body(c_ref, w_ref, b_ref, o_ref):
        cc = c_ref[...]
        sc = cc * jax.nn.sigmoid(cc)
        o_ref[...] = jnp.dot(sc, w_ref[...], preferred_element_type=F32,
                             precision=lax.Precision.HIGHEST) + b_ref[...]

    return pl.pallas_call(
        body, name=name, grid=(2,),
        in_specs=[pl.BlockSpec((NDEV, D), lambda l: (0, 0)),
                  pl.BlockSpec((None, D, ADA_COLS), lambda l: (l, 0, 0)),
                  pl.BlockSpec((None, 1, ADA_COLS), lambda l: (l, 0, 0))],
        out_specs=pl.BlockSpec((None, NDEV, ADA_COLS), lambda l: (l, 0, 0)),
        out_shape=jax.ShapeDtypeStruct((2, NDEV, ADA_COLS), F32),
        compiler_params=_params(("arbitrary",)),
    )(c_all, w_ada, b_shard.reshape(2, 1, ADA_COLS))


def wada_grad(c_all_t, dmod, name):
    ct = ADA_COLS // 3

    def body(c_ref, d_ref, o_ref):
        cc = c_ref[...]
        sc = cc * jax.nn.sigmoid(cc)
        acc = sc[:, 0:1] * d_ref[0:1, :]
        for b in range(1, NDEV):
            acc = acc + sc[:, b:b + 1] * d_ref[b:b + 1, :]
        o_ref[...] = acc

    return pl.pallas_call(
        body, name=name, grid=(2, 3),
        in_specs=[pl.BlockSpec((D, LANES), lambda l, j: (0, 0)),
                  pl.BlockSpec((None, NDEV, ct), lambda l, j: (l, 0, j))],
        out_specs=pl.BlockSpec((None, D, ct), lambda l, j: (l, 0, j)),
        out_shape=jax.ShapeDtypeStruct((2, D, ADA_COLS), F32),
        compiler_params=_params(("arbitrary", "arbitrary")),
    )(c_all_t, dmod)


def _pad_rows(row, rows=SUBLANES):
    return jnp.concatenate([row[None, :], jnp.zeros((rows - 1, row.shape[0]), row.dtype)], axis=0)


def kernel(x, c, w_ada, b_ada, norm_g, w_in, q_norm_g, k_norm_g, conv_w, conv_b, w_out, ffn_w1, ffn_w2, loss_target, m_w_ada, m_b_ada, m_norm_g, m_w_in, m_q_norm_g, m_k_norm_g, m_conv_w, m_conv_b, m_w_out, m_ffn_w1, m_ffn_w2, v_w_ada, v_b_ada, v_norm_g, v_w_in, v_q_norm_g, v_k_norm_g, v_conv_w, v_conv_b, v_w_out, v_ffn_w1, v_ffn_w2):
    ix, iy, ic = lax.axis_index("x"), lax.axis_index("y"), lax.axis_index("c")
    chip = 2 * ix + iy
    dev = 2 * chip + ic
    cflag = jnp.reshape(ic, (1,)).astype(jnp.int32)
    ngw = norm_g.shape[-1]
    cww = conv_w.shape[-1]

    pack = jnp.concatenate([c[0], norm_g.reshape(-1), conv_w.reshape(-1)])
    got = small_all_gather(_pad_rows(pack), "gather_c_normg_convw")[::SUBLANES]
    c_all = got[:, :D]
    per_chip = got[::2]
    ng_full = jnp.concatenate([per_chip[j, D:D + 6 * ngw].reshape(2, 3, ngw) for j in range(NCHIP)], axis=-1)
    cw_full = jnp.concatenate([per_chip[j, D + 6 * ngw:].reshape(2, 3, cww) for j in range(NCHIP)], axis=-1)

    b_shard = lax.dynamic_slice_in_dim(b_ada, chip * ADA_COLS, ADA_COLS, axis=1)
    mod_blk = mod_fwd(c_all, w_ada, b_shard, "mod_fwd").reshape(2 * NDEV, ADA_COLS)
    mod_all = small_all_gather(mod_blk, "gather_mod").reshape(NDEV, 2, NDEV, ADA_COLS)[::2]
    mod_mine = lax.dynamic_index_in_dim(mod_all, dev, axis=2, keepdims=False)
    mods = [mod_mine[:, l, :].reshape(-1) for l in range(2)]

    shards, gvecs, cws = [], [], []
    for l in range(2):
        shards.append(dict(w1=[ffn_w1[l, i].astype(MXU_DTYPE) for i in range(2)],
                           w2=[ffn_w2[l, i].astype(MXU_DTYPE) for i in range(2)],
                           win=w_in[l].astype(MXU_DTYPE), wout=w_out[l].astype(MXU_DTYPE)))
        gv = jnp.stack([jnp.tile(q_norm_g[l], AW // HD), jnp.tile(k_norm_g[l], AW // HD)])
        gvecs.append(jnp.concatenate([gv, jnp.zeros((SUBLANES - 2, AW), F32)], axis=0))
        cws.append(jnp.concatenate([cw_full[l], conv_b[l][None, :], jnp.zeros((SUBLANES - 4, CW), F32)], axis=0))
    w_first = gather_split([shards[0]["w1"][0], shards[0]["w2"][0]], "gather_first_ffn")

    loss_blk, dx, totals, sums = local_step(x[0], loss_target[0], mods, [ng_full[0], ng_full[1]], gvecs, cws,
                                            shards, w_first, cflag)

    dmods, dngs, dqg, dkg, dcw, dcb = [], [], [], [], [], []
    for l in range(2):
        s0, s1, so, s2, sm = sums[l]
        dmods.append(jnp.concatenate([s0[0], s0[1], s0[3], s1[0], s1[1], so[0], s2[0], s2[1], s2[3]]))
        dngs.append(jnp.concatenate([s0[2], s1[2], s2[2]]))
        dqg.append(sm[0].reshape(AW // HD, HD).sum(0))
        dkg.append(sm[1].reshape(AW // HD, HD).sum(0))
        dcw.append(sm[2:5].reshape(-1))
        dcb.append(sm[5])
    small = jnp.concatenate(dmods + dngs + dqg + dkg + dcw + dcb + [loss_blk[0]])
    small_all = small_all_gather(_pad_rows(small), "gather_small_grads")[::SUBLANES]
    nm = 9 * D
    dmod_all = small_all[:, :2 * nm].reshape(NDEV, 2, NCHIP, ADA_COLS)
    dmod_mine = lax.dynamic_index_in_dim(dmod_all, chip, axis=2, keepdims=False).transpose(1, 0, 2)
    tot = sum_devices(small_all, "sum_small_grads")[0]
    o = 2 * nm
    g_b_ada = tot[:o].reshape(2, nm)
    g_norm_g = lax.dynamic_slice_in_dim(tot[o:o + 6 * D].reshape(2, 3, D), chip * ngw, ngw, axis=2)
    o += 6 * D
    g_qg = tot[o:o + 2 * HD].reshape(2, HD)
    o += 2 * HD
    g_kg = tot[o:o + 2 * HD].reshape(2, HD)
    o += 2 * HD
    g_cw = lax.dynamic_slice_in_dim(tot[o:o + 6 * CW].reshape(2, 3, CW), chip * cww, cww, axis=2)
    o += 6 * CW
    g_cb = tot[o:o + 2 * CW].reshape(2, CW)
    loss = tot[o + 2 * CW]

    c_all_t = jnp.concatenate([c_all.T, jnp.zeros((D, LANES - NDEV), F32)], axis=1)
    g_wada_src = wada_grad(c_all_t, dmod_mine, "wada_grad")

    def halves(k_of_plane):
        return [(totals[l][0][k], totals[l][1][k]) for l, k in k_of_plane]

    r_wada = adamw(w_ada, m_w_ada, v_w_ada, [g_wada_src[0], g_wada_src[1]], cflag, "adamw_w_ada")
    r_win = adamw(w_in, m_w_in, v_w_in, halves([(0, 2), (1, 2)]), cflag, "adamw_w_in", halves=True)
    r_wout = adamw(w_out, m_w_out, v_w_out, halves([(0, 3), (1, 3)]), cflag, "adamw_w_out", halves=True)
    r_w1 = adamw(ffn_w1.reshape(4, D, HALF), m_ffn_w1.reshape(4, D, HALF), v_ffn_w1.reshape(4, D, HALF),
                 halves([(0, 0), (0, 4), (1, 0), (1, 4)]), cflag, "adamw_ffn_w1", halves=True)
    w2r = DFF // NCHIP
    r_w2 = adamw(ffn_w2.reshape(4, w2r, D), m_ffn_w2.reshape(4, w2r, D), v_ffn_w2.reshape(4, w2r, D),
                 halves([(0, 1), (0, 5), (1, 1), (1, 5)]), cflag, "adamw_ffn_w2", halves=True)
    r_w1 = [t.reshape(ffn_w1.shape) for t in r_w1]
    r_w2 = [t.reshape(ffn_w2.shape) for t in r_w2]

    smalls = [("b_ada", b_ada, m_b_ada, v_b_ada, g_b_ada), ("norm_g", norm_g, m_norm_g, v_norm_g, g_norm_g),
              ("q_norm_g", q_norm_g, m_q_norm_g, v_q_norm_g, g_qg), ("k_norm_g", k_norm_g, m_k_norm_g, v_k_norm_g, g_kg),
              ("conv_w", conv_w, m_conv_w, v_conv_w, g_cw), ("conv_b", conv_b, m_conv_b, v_conv_b, g_cb)]
    n_small = sum(t[1].size for t in smalls)
    pad = (-n_small) % (16 * LANES)

    def packed(idx):
        flat = jnp.concatenate([t[idx].reshape(-1) for t in smalls] + [jnp.zeros((pad,), F32)])
        return flat.reshape(-1, LANES)

    r_small = adamw(packed(1)[None], packed(2)[None], packed(3)[None], [packed(4)], cflag, "adamw_small")
    small_out = {}
    o = 0
    for name_, w_, _, _, _ in smalls:
        small_out[name_] = [t.reshape(-1)[o:o + w_.size].reshape(w_.shape) for t in r_small]
        o += w_.size

    res = {"w_ada": r_wada, "w_in": r_win, "w_out": r_wout, "ffn_w1": r_w1, "ffn_w2": r_w2, **small_out}
    order = ["w_ada", "b_ada", "norm_g", "w_in", "q_norm_g", "k_norm_g", "conv_w", "conv_b", "w_out", "ffn_w1", "ffn_w2"]
    outs = [loss, dx[None]]
    for k in range(4):
        outs += [res[nm_][k] for nm_ in order]
    return tuple(outs)
```

```python
import functools

import jax
import jax.numpy as jnp
from jax import lax
from jax.experimental import pallas as pl
from jax.experimental.pallas import tpu as pltpu

F32 = jnp.float32
MXU_DTYPE = jnp.bfloat16
ACT_DTYPE = jnp.bfloat16
WIRE_DTYPE = jnp.bfloat16

D = 1024
HD = 64
AW = 512
CW = 512
DFF = 2816
HALF = DFF // 2
INC = 3 * AW + 3 * CW
NCHIP = 4
NDEV = 8
QBLK = 128
ATTN_QBLOCKS = 8
ATTN_INTERLEAVE = 8
ATTN_CHUNK_ROWS = 4096
DILATIONS = (1, 4, 16)
EPS = 1e-6
NEG = -1e30
LANES = 128
SUBLANES = 8
HALO_ROWS = 16
VMEM_LIMIT = 56 * 1024 * 1024

ADAM_LR = 0.001
ADAM_B1 = 0.9
ADAM_B2 = 0.999
ADAM_EPS = 1e-08
ADAM_WD = 0.01
ADAM_STEP = 10

NT_DIMS = (((1,), (1,)), ((), ()))
TN_DIMS = (((0,), (0,)), ((), ()))


def _params(sem, vmem=VMEM_LIMIT):
    return pltpu.CompilerParams(dimension_semantics=sem, vmem_limit_bytes=vmem)


def _row_tile(n, want):
    t = min(n, want)
    assert n % t == 0
    return t


def _ada(xt, vec_ref):
    ng, sc, sh, gt = vec_ref[0:1, :], vec_ref[1:2, :], vec_ref[2:3, :], vec_ref[3:4, :]
    r = lax.rsqrt(jnp.mean(xt * xt, axis=-1, keepdims=True) + EPS)
    return xt * r, r, ng * (1.0 + sc), ng, sc, sh, gt


def _ada_bwd(dh, xhat, r, gain, ng, sc):
    dshift = jnp.sum(dh, axis=0, keepdims=True)
    dhx = dh * xhat
    dscale = jnp.sum(dhx, axis=0, keepdims=True) * ng
    dng = jnp.sum(dhx, axis=0, keepdims=True) * (1.0 + sc)
    dxhat = dh * gain
    dx = r * (dxhat - xhat * jnp.mean(dxhat * xhat, axis=-1, keepdims=True))
    return dx, dshift, dscale, dng


def _acc_rows(sums_ref, first, rows):
    @pl.when(first)
    def _():
        sums_ref[...] = jnp.zeros_like(sums_ref)
    for k, row in enumerate(rows):
        sums_ref[k:k + 1, :] += row


MESH = pl.DeviceIdType.MESH
ANY = pl.BlockSpec(memory_space=pl.ANY)


def _here():
    return lax.axis_index("x"), lax.axis_index("y"), lax.axis_index("c")


def _ici_copies(src_refs, dst_refs, send_sems, recv_sems, local_sems, scatter):
    x, y, c = _here()
    my_chip = 2 * x + y
    peers = [(1 - x, y), (x, 1 - y), (1 - x, 1 - y)]
    local, out, inc = [], [], []
    for a, (src, dst) in enumerate(zip(src_refs, dst_refs)):
        local.append(pltpu.make_async_copy(src.at[my_chip] if scatter else src, dst.at[my_chip], local_sems.at[a]))
        for j, (px, py) in enumerate(peers):
            sems = dict(send_sem=send_sems.at[3 * a + j], recv_sem=recv_sems.at[3 * a + j],
                        device_id=(px, py, c), device_id_type=MESH)
            out.append(pltpu.make_async_remote_copy(
                src_ref=src.at[2 * px + py] if scatter else src, dst_ref=dst.at[my_chip], **sems))
            inc.append(pltpu.make_async_remote_copy(
                src_ref=src.at[my_chip] if scatter else src, dst_ref=dst.at[2 * px + py], **sems))
    return local, out, inc


def _swap_copies(src_refs, dst_refs, send_sems, recv_sems, halves):
    x, y, c = _here()
    cps = []
    for k, (src, dst) in enumerate(zip(src_refs, dst_refs)):
        if halves:
            r2 = src.shape[1] // 2
            src = src.at[:, pl.ds((1 - c) * r2, r2), :]
        cps.append(pltpu.make_async_remote_copy(
            src_ref=src, dst_ref=dst, send_sem=send_sems.at[k], recv_sem=recv_sems.at[k],
            device_id=(x, y, 1 - c), device_id_type=MESH))
    return cps


class Carry:
    def __init__(self, kind, srcs):
        self.kind, self.srcs, n = kind, list(srcs), len(srcs)
        if kind == "gather":
            shapes = [(NCHIP,) + s.shape for s in srcs]
        elif kind == "swap_halves":
            shapes = [(s.shape[0], s.shape[1] // 2, s.shape[2]) for s in srcs]
        else:
            shapes = [s.shape for s in srcs]
        self.out_shape = [jax.ShapeDtypeStruct(sh, s.dtype) for sh, s in zip(shapes, srcs)]
        dma = pltpu.SemaphoreType.DMA
        self.sems = [dma((3 * n,)), dma((3 * n,)), dma((n,))] if kind in ("gather", "scatter") else [dma((n,)), dma((n,))]

    def start(self, srcs, dsts, sems):
        if self.kind in ("gather", "scatter"):
            local, out, _ = _ici_copies(srcs, dsts, *sems, self.kind == "scatter")
            for cp in local + out:
                cp.start()
        else:
            for cp in _swap_copies(srcs, dsts, *sems, self.kind == "swap_halves"):
                cp.start()

    def wait(self, srcs, dsts, sems):
        if self.kind in ("gather", "scatter"):
            local, out, inc = _ici_copies(srcs, dsts, *sems, self.kind == "scatter")
            for cp in inc:
                cp.wait_recv()
            for cp in out:
                cp.wait_send()
            for cp in local:
                cp.wait()
        else:
            cps = _swap_copies(srcs, dsts, *sems, self.kind == "swap_halves")
            for cp in cps:
                cp.wait_recv()
            for cp in cps:
                cp.wait_send()


def run_carry(carry, name):
    n = len(carry.srcs)

    def body(*refs):
        srcs, dsts, sems = refs[:n], refs[n:2 * n], refs[2 * n:]
        carry.start(srcs, dsts, sems)
        carry.wait(srcs, dsts, sems)

    return pl.pallas_call(body, name=name, out_shape=carry.out_shape, in_specs=[ANY] * n, out_specs=[ANY] * n,
                          scratch_shapes=carry.sems)(*carry.srcs)


def gather_split(srcs, name):
    n = len(srcs)

    def body(*refs):
        src_refs, dst_refs = refs[:n], refs[n:2 * n]
        send_sems, recv_sems, fwd_send, fwd_recv, local_sems = refs[2 * n:]
        x, y, c = _here()
        my_chip = 2 * x + y
        peers = [(1 - x, y), (x, 1 - y), (1 - x, 1 - y)]

        def half(ref, h):
            r2 = ref.shape[0] // 2
            return ref.at[pl.ds(h * r2, r2), :]

        local, out, landed, passed, arriving = [], [], [], [], []
        for a, (src, dst) in enumerate(zip(src_refs, dst_refs)):
            local.append(pltpu.make_async_copy(src, dst.at[my_chip], local_sems.at[a]))
            for j, (px, py) in enumerate(peers):
                k = 3 * a + j
                theirs = dst.at[2 * px + py]
                ici = dict(send_sem=send_sems.at[k], recv_sem=recv_sems.at[k], device_id=(px, py, c), device_id_type=MESH)
                d2d = dict(send_sem=fwd_send.at[k], recv_sem=fwd_recv.at[k], device_id=(x, y, 1 - c), device_id_type=MESH)
                out.append(pltpu.make_async_remote_copy(src_ref=half(src, c), dst_ref=half(dst.at[my_chip], c), **ici))
                landed.append(pltpu.make_async_remote_copy(src_ref=half(src, c), dst_ref=half(theirs, c), **ici))
                passed.append(pltpu.make_async_remote_copy(src_ref=half(theirs, c), dst_ref=half(theirs, c), **d2d))
                arriving.append(pltpu.make_async_remote_copy(src_ref=half(theirs, c), dst_ref=half(theirs, 1 - c), **d2d))
        for cp in local + out:
            cp.start()
        for got, fwd in zip(landed, passed):
            got.wait_recv()
            fwd.start()
        for cp in arriving:
            cp.wait_recv()
        for cp in out + passed:
            cp.wait_send()
        for cp in local:
            cp.wait()

    dma = pltpu.SemaphoreType.DMA
    return pl.pallas_call(
        body, name=name, out_shape=[jax.ShapeDtypeStruct((NCHIP,) + s.shape, s.dtype) for s in srcs],
        in_specs=[ANY] * n, out_specs=[ANY] * n,
        scratch_shapes=[dma((3 * n,)), dma((3 * n,)), dma((3 * n,)), dma((3 * n,)), dma((n,))],
    )(*srcs)


def _pcall(body, name, grid, in_specs, out_specs, out_shape, sem, args, carry=None, scratch=()):
    if carry is None:
        outs = pl.pallas_call(body, name=name, grid=grid, in_specs=in_specs, out_specs=out_specs,
                              out_shape=out_shape, scratch_shapes=list(scratch), compiler_params=_params(sem))(*args)
        return outs, []
    n_in, n_out, nc, ns = len(in_specs), len(out_specs), len(carry.srcs), len(scratch)

    def wrapped(*refs):
        ins, csrc = refs[:n_in], refs[n_in:n_in + nc]
        outs, cdst = refs[n_in + nc:n_in + nc + n_out], refs[n_in + nc + n_out:n_in + 2 * nc + n_out]
        own = refs[n_in + 2 * nc + n_out:n_in + 2 * nc + n_out + ns]
        sems = refs[n_in + 2 * nc + n_out + ns:]
        ids = [pl.program_id(a) for a in range(len(grid))]
        first = functools.reduce(jnp.logical_and, [i == 0 for i in ids])
        last = functools.reduce(jnp.logical_and, [i == g - 1 for i, g in zip(ids, grid)])

        @pl.when(first)
        def _():
            carry.start(csrc, cdst, sems)

        body(*ins, *outs, *own)

        @pl.when(last)
        def _():
            carry.wait(csrc, cdst, sems)

    res = pl.pallas_call(
        wrapped, name=name, grid=grid,
        in_specs=list(in_specs) + [ANY] * nc, out_specs=list(out_specs) + [ANY] * nc,
        out_shape=list(out_shape) + carry.out_shape,
        scratch_shapes=list(scratch) + carry.sems, compiler_params=_params(sem),
    )(*args, *carry.srcs)
    return res[:n_out], res[n_out:]


def ffn_fwd(x, vec, w1p, w2, gs, name, carry=None, target=None):
    S = x.shape[0]
    tm = _row_tile(S, 512)

    def body(x_ref, *refs):
        if target is None:
            vec_ref, w1_ref, w2_ref, xn_ref, a_ref, f_ref = refs
        else:
            t_ref, vec_ref, w1_ref, w2_ref, xn_ref, a_ref, f_ref, l_ref = refs
        xt = x_ref[...]
        xhat, _, gain, _, _, sh, gt = _ada(xt, vec_ref)
        h = (xhat * gain + sh).astype(MXU_DTYPE)
        f = jnp.zeros((tm, D), F32)
        for hf in range(2):
            g = jnp.dot(h, w1_ref[hf], preferred_element_type=F32)
            up = jnp.dot(h, w1_ref[2 + hf], preferred_element_type=F32)
            a_ref[:, hf * HALF:(hf + 1) * HALF] = g.astype(a_ref.dtype)
            a_ref[:, DFF + hf * HALF:DFF + (hf + 1) * HALF] = up.astype(a_ref.dtype)
            act = (g * jax.nn.sigmoid(g) * up).astype(MXU_DTYPE)
            f = f + jnp.dot(act, w2_ref[hf * HALF:(hf + 1) * HALF, :], preferred_element_type=F32)
        f_ref[...] = f.astype(f_ref.dtype)
        xn = xt + (gs * gt) * f
        if target is None:
            xn_ref[...] = xn
        else:
            diff = xn - t_ref[...]
            xn_ref[...] = diff * (1.0 / D)
            part = jnp.sum(jnp.sum(diff * diff, axis=0, keepdims=True), axis=1, keepdims=True) * (0.5 / D)

            @pl.when(pl.program_id(0) == 0)
            def _():
                l_ref[...] = jnp.zeros_like(l_ref)
            l_ref[...] += jnp.broadcast_to(part, l_ref.shape)

    tile = pl.BlockSpec((tm, D), lambda i: (i, 0))
    last = target is not None
    return _pcall(
        body, name, (S // tm,),
        [tile] * (2 if last else 1) + [
            pl.BlockSpec((SUBLANES, D), lambda i: (0, 0)),
            pl.BlockSpec((NCHIP, D, HALF), lambda i: (0, 0, 0), pipeline_mode=pl.Buffered(1)),
            pl.BlockSpec((DFF, D), lambda i: (0, 0), pipeline_mode=pl.Buffered(1))],
        [tile, pl.BlockSpec((tm, 2 * DFF), lambda i: (i, 0)), tile]
        + ([pl.BlockSpec((SUBLANES, LANES), lambda i: (0, 0))] if last else []),
        [jax.ShapeDtypeStruct((S, D), F32),
         jax.ShapeDtypeStruct((S, 2 * DFF), ACT_DTYPE),
         jax.ShapeDtypeStruct((S, D), ACT_DTYPE)]
        + ([jax.ShapeDtypeStruct((SUBLANES, LANES), F32)] if last else []),
        ("arbitrary",), (x, target, vec, w1p, w2) if last else (x, vec, w1p, w2), carry)


def ffn_bwd(dxo, x, a, f, vec, w1p, w2, gs, name, carry=None):
    S = x.shape[0]
    tm = _row_tile(S, 256)

    def body(dxo_ref, x_ref, a_ref, f_ref, vec_ref, w1_ref, w2_ref,
             dxi_ref, hb_ref, dfb_ref, act_ref, da_ref, sums_ref):
        xt = x_ref[...]
        dxo = dxo_ref[...]
        xhat, r, gain, ng, sc, sh, gt = _ada(xt, vec_ref)
        hb_ref[...] = (xhat * gain + sh).astype(hb_ref.dtype)
        dgate = gs * jnp.sum(dxo * f_ref[...].astype(F32), axis=0, keepdims=True)
        df = ((gs * gt) * dxo).astype(MXU_DTYPE)
        dfb_ref[...] = df
        dh = jnp.zeros((tm, D), F32)
        for hf in range(2):
            lo, hi = hf * HALF, (hf + 1) * HALF
            dact = lax.dot_general(df, w2_ref[lo:hi, :], NT_DIMS, preferred_element_type=F32)
            g = a_ref[:, lo:hi].astype(F32)
            up = a_ref[:, DFF + lo:DFF + hi].astype(F32)
            sg = jax.nn.sigmoid(g)
            si = g * sg
            act_ref[:, lo:hi] = (si * up).astype(act_ref.dtype)
            dg = (dact * up * (sg * (1.0 + g * (1.0 - sg)))).astype(MXU_DTYPE)
            dup = (dact * si).astype(MXU_DTYPE)
            da_ref[:, lo:hi] = dg
            da_ref[:, DFF + lo:DFF + hi] = dup
            dh = dh + lax.dot_general(dg, w1_ref[hf], NT_DIMS, preferred_element_type=F32)
            dh = dh + lax.dot_general(dup, w1_ref[2 + hf], NT_DIMS, preferred_element_type=F32)
        dx, dshift, dscale, dng = _ada_bwd(dh, xhat, r, gain, ng, sc)
        dxi_ref[...] = dxo + dx
        _acc_rows(sums_ref, pl.program_id(0) == 0, (dshift, dscale, dng, dgate))

    return _pcall(
        body, name, (S // tm,),
        [pl.BlockSpec((tm, D), lambda i: (i, 0)),
         pl.BlockSpec((tm, D), lambda i: (i, 0)),
         pl.BlockSpec((tm, 2 * DFF), lambda i: (i, 0)),
         pl.BlockSpec((tm, D), lambda i: (i, 0)),
         pl.BlockSpec((SUBLANES, D), lambda i: (0, 0)),
         pl.BlockSpec((NCHIP, D, HALF), lambda i: (0, 0, 0), pipeline_mode=pl.Buffered(1)),
         pl.BlockSpec((DFF, D), lambda i: (0, 0), pipeline_mode=pl.Buffered(1))],
        [pl.BlockSpec((tm, D), lambda i: (i, 0)),
         pl.BlockSpec((tm, D), lambda i: (i, 0)),
         pl.BlockSpec((tm, D), lambda i: (i, 0)),
         pl.BlockSpec((tm, DFF), lambda i: (i, 0)),
         pl.BlockSpec((tm, 2 * DFF), lambda i: (i, 0)),
         pl.BlockSpec((SUBLANES, D), lambda i: (0, 0))],
        [jax.ShapeDtypeStruct((S, D), F32),
         jax.ShapeDtypeStruct((S, D), MXU_DTYPE),
         jax.ShapeDtypeStruct((S, D), MXU_DTYPE),
         jax.ShapeDtypeStruct((S, DFF), MXU_DTYPE),
         jax.ShapeDtypeStruct((S, 2 * DFF), MXU_DTYPE),
         jax.ShapeDtypeStruct((SUBLANES, D), F32)],
        ("arbitrary",), (dxo, x, a, f, vec, w1p, w2), carry)


def wgrad(a, b, kt, nt, name, carry=None):
    T, K = a.shape
    N = b.shape[1]
    pk, pn = K // kt, N // nt
    assert pk == 1 or pn == 1
    tt = _row_tile(T, 2048)
    steps = T // tt

    def body(a_ref, b_ref, o_ref):
        @pl.when(pl.program_id(1) == 0)
        def _():
            o_ref[...] = jnp.zeros_like(o_ref)
        o_ref[...] += lax.dot_general(a_ref[...], b_ref[...], TN_DIMS, preferred_element_type=F32)

    a_map = (lambda p, t: (t, p)) if pk > 1 else (lambda p, t: (t, 0))
    b_map = (lambda p, t: (t, p)) if pn > 1 else (lambda p, t: (t, 0))
    (out,), got = _pcall(
        body, name, (pk * pn, steps),
        [pl.BlockSpec((tt, kt), a_map), pl.BlockSpec((tt, nt), b_map)],
        [pl.BlockSpec((None, kt, nt), lambda p, t: (p, 0, 0))],
        [jax.ShapeDtypeStruct((pk * pn, kt, nt), F32)], ("arbitrary", "arbitrary"), (a, b), carry)
    return out, got


def _head_masks(rows):
    lane = lax.broadcasted_iota(jnp.int32, (rows, LANES), 1)
    return lane < HD


def _pair_stat(x, m_a):
    s_a = jnp.sum(jnp.where(m_a, x, 0.0), axis=1, keepdims=True)
    s_b = jnp.sum(jnp.where(m_a, 0.0, x), axis=1, keepdims=True)
    return s_a, s_b


def mixer_in(x, vec, winp, gvec, name):
    S = x.shape[0]
    tm = _row_tile(S, 512)
    pc = INC // NCHIP

    def body(x_ref, vec_ref, w_ref, g_ref, proj_ref, hb_ref, qn_ref, kn_ref, v_ref, qkv_ref):
        xt = x_ref[...]
        xhat, _, gain, _, _, sh, _ = _ada(xt, vec_ref)
        h = (xhat * gain + sh).astype(MXU_DTYPE)
        hb_ref[...] = h
        for j in range(NCHIP):
            piece = jnp.dot(h, w_ref[j], preferred_element_type=F32)
            proj_ref[:, j * pc:(j + 1) * pc] = piece.astype(proj_ref.dtype)
            if (j + 1) * pc <= 3 * AW:
                qkv_ref[:, j * pc:(j + 1) * pc] = piece
        m_a = _head_masks(tm)
        for which, dst in ((0, qn_ref), (1, kn_ref)):
            for p in range(AW // LANES):
                lo = which * AW + p * LANES
                xp = qkv_ref[:, lo:lo + LANES]
                s_a, s_b = _pair_stat(xp * xp, m_a)
                rr = jnp.where(m_a, lax.rsqrt(s_a * (1.0 / HD) + EPS), lax.rsqrt(s_b * (1.0 / HD) + EPS))
                gp = g_ref[which:which + 1, p * LANES:(p + 1) * LANES]
                dst[:, p * LANES:(p + 1) * LANES] = (xp * rr * gp).astype(dst.dtype)
        v_ref[...] = qkv_ref[:, 2 * AW:3 * AW]

    assert 2 * pc == 3 * AW
    return pl.pallas_call(
        body, name=name, grid=(S // tm,), scratch_shapes=[pltpu.VMEM((tm, 3 * AW), F32)],
        in_specs=[pl.BlockSpec((tm, D), lambda i: (i, 0)),
                  pl.BlockSpec((SUBLANES, D), lambda i: (0, 0)),
                  pl.BlockSpec((NCHIP, D, pc), lambda i: (0, 0, 0), pipeline_mode=pl.Buffered(1)),
                  pl.BlockSpec((SUBLANES, AW), lambda i: (0, 0))],
        out_specs=[pl.BlockSpec((tm, INC), lambda i: (i, 0)),
                   pl.BlockSpec((tm, D), lambda i: (i, 0)),
                   pl.BlockSpec((tm, AW), lambda i: (i, 0)),
                   pl.BlockSpec((tm, AW), lambda i: (i, 0)),
                   pl.BlockSpec((tm, AW), lambda i: (i, 0))],
        out_shape=[jax.ShapeDtypeStruct((S, INC), ACT_DTYPE),
                   jax.ShapeDtypeStruct((S, D), MXU_DTYPE),
                   jax.ShapeDtypeStruct((S, AW), F32),
                   jax.ShapeDtypeStruct((S, AW), F32),
                   jax.ShapeDtypeStruct((S, AW), F32)],
        compiler_params=_params(("arbitrary",)),
    )(x, vec, winp, gvec)


def _band_masks(ncol):
    row = lax.broadcasted_iota(jnp.int32, (2 * QBLK, ncol), 0) & (QBLK - 1)
    col = lax.broadcasted_iota(jnp.int32, (2 * QBLK, ncol), 1)
    return row, col


def _stack_heads(t, m_a):
    zero = jnp.zeros_like(t)
    return jnp.concatenate([jnp.where(m_a, t, zero), jnp.where(m_a, zero, t)], axis=0)


class _AttnLayout:
    def __init__(self, d, S):
        self.d, self.S = d, S
        self.qb = max(1, min(ATTN_QBLOCKS, ATTN_CHUNK_ROWS // (QBLK * d)))
        self.nres = d
        self.nchunk = S // (self.qb * QBLK * d)
        self.grid = (AW // LANES, self.nchunk)
        self.unroll = max(1, min(d, ATTN_INTERLEAVE // self.qb))

    def _spec(self, blocks, row_of):
        return pl.BlockSpec((blocks * QBLK * self.d, LANES), lambda hp, j: (row_of(j), hp))

    def cur(self, chunk_of):
        return self._spec(self.qb, chunk_of)

    def prev(self, chunk_of):
        return self._spec(1, lambda j: jnp.maximum(chunk_of(j) * self.qb - 1, 0))

    def idx(self, b, r):
        if self.d == 1:
            return (pl.ds(b * QBLK, QBLK), slice(None))
        return (pl.ds(b * QBLK * self.d + r, QBLK, stride=self.d), slice(None))

    def per_residue(self, fn):
        if self.nres == 1:
            fn(0)
        else:
            def step(it, carry):
                for k in range(self.unroll):
                    fn(it * self.unroll + k)
                return carry
            lax.fori_loop(0, self.nres // self.unroll, step, 0)


def attn_fwd(qn, kn, v, d, name, carry=None):
    S = qn.shape[0]
    lay = _AttnLayout(d, S)
    qb = lay.qb

    def body(q_ref, kc_ref, kp_ref, vc_ref, vp_ref, o_ref, lse_ref):
        i = pl.program_id(1)
        m_a = _head_masks(QBLK)
        row, col = _band_masks(2 * QBLK)
        dist = row + QBLK - col
        band = (dist >= 0) & (dist <= QBLK)
        first = band & ((i > 0) | (col >= QBLK))

        def residue(r):
            kt = [kp_ref[lay.idx(0, r)].astype(MXU_DTYPE)]
            vt = [vp_ref[lay.idx(0, r)].astype(MXU_DTYPE)]
            for b in range(qb):
                kt.append(kc_ref[lay.idx(b, r)].astype(MXU_DTYPE))
                vt.append(vc_ref[lay.idx(b, r)].astype(MXU_DTYPE))
            for b in range(qb):
                rows = lay.idx(b, r)
                q = (q_ref[rows] * (HD ** -0.5)).astype(MXU_DTYPE)
                kcat = jnp.concatenate([kt[b], kt[b + 1]], axis=0)
                vcat = jnp.concatenate([vt[b], vt[b + 1]], axis=0)
                mask = first if b == 0 else band
                s = lax.dot_general(_stack_heads(q, m_a), kcat, NT_DIMS, preferred_element_type=F32)
                s = jnp.where(mask, s, NEG)
                m = jnp.max(s, axis=1, keepdims=True)
                p = jnp.exp(s - m)
                l = jnp.sum(p, axis=1, keepdims=True)
                o = jnp.dot(p.astype(MXU_DTYPE), vcat, preferred_element_type=F32) / l
                lse = jnp.broadcast_to(m + jnp.log(l), (2 * QBLK, LANES))
                o_ref[rows] = jnp.where(m_a, o[:QBLK], o[QBLK:])
                lse_ref[rows] = jnp.where(m_a, lse[:QBLK], lse[QBLK:])

        lay.per_residue(residue)

    cur, prev = lay.cur(lambda j: j), lay.prev(lambda j: j)
    return _pcall(body, name, lay.grid, [cur, cur, prev, cur, prev], [cur, cur],
                  [jax.ShapeDtypeStruct((S, AW), F32)] * 2, ("arbitrary", "arbitrary"), (qn, kn, kn, v, v), carry)


def _both_heads(t, m_a):
    other = pltpu.roll(t, HD, 1)
    return jnp.concatenate([jnp.where(m_a, t, other), jnp.where(m_a, other, t)], axis=0)


def attn_bwd(qn, kn, v, dycat, lse, delta, d, name, carry=None):
    S = qn.shape[0]
    lay = _AttnLayout(d, S)
    qb, nchunk = lay.qb, lay.nchunk

    def body(q_ref, kc_ref, kp_ref, vc_ref, vp_ref, do_ref, lse_ref, dl_ref,
             dq_ref, dk_ref, dv_ref, ck_ref, cv_ref):
        j = pl.program_id(1)
        i = nchunk - 1 - j
        m_a = _head_masks(QBLK)
        row, col = _band_masks(2 * QBLK)
        dist = row + QBLK - col
        band = (dist >= 0) & (dist <= QBLK)
        first = band & ((i > 0) | (col >= QBLK))

        def residue(r):
            def tiles(ref, cast):
                out = [ref[lay.idx(b, r)] for b in range(qb)]
                return [t.astype(MXU_DTYPE) for t in out] if cast else out

            def ktiles(cur_ref, prev_ref):
                return [prev_ref[lay.idx(0, r)].astype(MXU_DTYPE)] + tiles(cur_ref, True)

            qt = [(t * (HD ** -0.5)).astype(MXU_DTYPE) for t in tiles(q_ref, False)]
            dot_ = tiles(do_ref, True)
            lse_t = tiles(lse_ref, False)
            dl_t = tiles(dl_ref, False)
            kt = ktiles(kc_ref, kp_ref)
            vt = ktiles(vc_ref, vp_ref)
            dk_acc = [jnp.zeros((QBLK, LANES), F32) for _ in range(qb)]
            dv_acc = [jnp.zeros((QBLK, LANES), F32) for _ in range(qb)]
            crow = pl.ds(0, QBLK) if lay.nres == 1 else pl.ds(pl.multiple_of(r * QBLK, QBLK), QBLK)
            dk_acc[qb - 1] = jnp.where(j > 0, ck_ref[crow, :], 0.0)
            dv_acc[qb - 1] = jnp.where(j > 0, cv_ref[crow, :], 0.0)
            for x in range(qb):
                kcat = jnp.concatenate([kt[x], kt[x + 1]], axis=0)
                vcat = jnp.concatenate([vt[x], vt[x + 1]], axis=0)
                q2 = _stack_heads(qt[x], m_a)
                do2 = _stack_heads(dot_[x], m_a)
                lse2 = _both_heads(lse_t[x], m_a)
                dl2 = _both_heads(dl_t[x], m_a)
                lse2 = jnp.concatenate([lse2, lse2], axis=1)
                dl2 = jnp.concatenate([dl2, dl2], axis=1)
                s = lax.dot_general(q2, kcat, NT_DIMS, preferred_element_type=F32)
                p = jnp.exp(jnp.where(first if x == 0 else band, s, NEG) - lse2)
                dp = lax.dot_general(do2, vcat, NT_DIMS, preferred_element_type=F32)
                ds = p * (dp - dl2)
                dq = jnp.dot(ds.astype(MXU_DTYPE), kcat, preferred_element_type=F32)
                dq_ref[lay.idx(x, r)] = jnp.where(m_a, dq[:QBLK], dq[QBLK:]) * (HD ** -0.5)
                dk = jnp.dot(ds.T.astype(MXU_DTYPE), q2, preferred_element_type=F32)
                dv = jnp.dot(p.T.astype(MXU_DTYPE), do2, preferred_element_type=F32)
                if x == 0:
                    ck_ref[crow, :] = dk[:QBLK]
                    cv_ref[crow, :] = dv[:QBLK]
                else:
                    dk_acc[x - 1] = dk_acc[x - 1] + dk[:QBLK]
                    dv_acc[x - 1] = dv_acc[x - 1] + dv[:QBLK]
                dk_acc[x] = dk_acc[x] + dk[QBLK:]
                dv_acc[x] = dv_acc[x] + dv[QBLK:]
            for kb in range(qb):
                dk_ref[lay.idx(kb, r)] = dk_acc[kb]
                dv_ref[lay.idx(kb, r)] = dv_acc[kb]

        lay.per_residue(residue)

    cur, prev = lay.cur(lambda j: nchunk - 1 - j), lay.prev(lambda j: nchunk - 1 - j)
    carried = pltpu.VMEM((lay.nres * QBLK, LANES), F32)
    return _pcall(
        body, name, lay.grid, [cur, cur, prev, cur, prev, cur, cur, cur], [cur, cur, cur],
        [jax.ShapeDtypeStruct((S, AW), F32)] * 3, ("arbitrary", "arbitrary"),
        (qn, kn, kn, v, v, dycat, lse, delta), carry, scratch=[carried, carried])


def _shift_down(x, halo_prev, k, row):
    tm = x.shape[0]
    tail = jnp.concatenate([pltpu.roll(halo_prev, k, 0), jnp.zeros((tm - SUBLANES, x.shape[1]), x.dtype)], axis=0)
    return jnp.where(row < k, tail, pltpu.roll(x, k, 0))


def _shift_up(x, halo_next, k, row):
    tm = x.shape[0]
    head = jnp.concatenate([jnp.zeros((tm - SUBLANES, x.shape[1]), x.dtype), pltpu.roll(halo_next, SUBLANES - k, 0)], axis=0)
    return jnp.where(row >= tm - k, head, pltpu.roll(x, tm - k, 0))


def _conv_fwd(cu, halo_cu, cw_ref, row):
    u1 = _shift_down(cu, halo_cu, 1, row)
    u2 = _shift_down(cu, halo_cu, 2, row)
    cv = cw_ref[0:1, :] * u2 + cw_ref[1:2, :] * u1 + cw_ref[2:3, :] * cu + cw_ref[3:4, :]
    return cv, u1, u2


def combine_conv(os_, lses, proj, cw, name, carry=None):
    S = proj.shape[0]
    tm = _row_tile(S, 512)
    hb = tm // HALO_ROWS

    def body(o1, o2, o3, l1, l2, l3, pc_ref, ph_ref, cw_ref, ycat_ref, lse_ref):
        i = pl.program_id(0)
        for p in range(AW // LANES):
            cs = slice(p * LANES, (p + 1) * LANES)
            ls = [l[:, cs] for l in (l1, l2, l3)]
            mx = jnp.maximum(jnp.maximum(ls[0], ls[1]), ls[2])
            t = mx + jnp.log(jnp.exp(ls[0] - mx) + jnp.exp(ls[1] - mx) + jnp.exp(ls[2] - mx))
            lse_ref[:, cs] = t
            acc = jnp.zeros((tm, LANES), F32)
            for l, o in zip(ls, (o1, o2, o3)):
                acc = acc + jnp.exp(l - t) * o[:, cs]
            ycat_ref[:, cs] = acc.astype(ycat_ref.dtype)
        row = lax.broadcasted_iota(jnp.int32, (tm, CW), 0)
        gb, gc, u = (pc_ref[:, k * CW:(k + 1) * CW].astype(F32) for k in range(3))
        ph = ph_ref[...].astype(F32)[HALO_ROWS - SUBLANES:]
        halo_cu = jnp.where(i > 0, ph[:, CW:2 * CW] * ph[:, 2 * CW:3 * CW], 0.0)
        cv, _, _ = _conv_fwd(gc * u, halo_cu, cw_ref, row)
        ycat_ref[:, AW:AW + CW] = (gb * cv).astype(ycat_ref.dtype)

    ot = pl.BlockSpec((tm, AW), lambda i: (i, 0))
    return _pcall(
        body, name, (S // tm,),
        [ot] * 6 + [pl.BlockSpec((tm, 3 * CW), lambda i: (i, 1)),
                    pl.BlockSpec((HALO_ROWS, 3 * CW), lambda i: (jnp.maximum(i * hb - 1, 0), 1)),
                    pl.BlockSpec((SUBLANES, CW), lambda i: (0, 0))],
        [pl.BlockSpec((tm, D), lambda i: (i, 0)), ot],
        [jax.ShapeDtypeStruct((S, D), ACT_DTYPE), jax.ShapeDtypeStruct((S, AW), F32)],
        ("arbitrary",), (*os_, *lses, proj, proj, cw), carry)


def out_proj(ycat, x, vec, wout, name):
    S = x.shape[0]
    tm = _row_tile(S, 512)

    def body(yc_ref, x_ref, vec_ref, w_ref, xn_ref, y_ref):
        y = jnp.dot(yc_ref[...].astype(MXU_DTYPE), w_ref[...], preferred_element_type=F32)
        xn_ref[...] = x_ref[...] + vec_ref[3:4, :] * y
        y_ref[...] = y.astype(y_ref.dtype)

    t = pl.BlockSpec((tm, D), lambda i: (i, 0))
    return pl.pallas_call(
        body, name=name, grid=(S // tm,),
        in_specs=[t, t, pl.BlockSpec((SUBLANES, D), lambda i: (0, 0)),
                  pl.BlockSpec((D, D), lambda i: (0, 0))],
        out_specs=[t, t],
        out_shape=[jax.ShapeDtypeStruct((S, D), F32), jax.ShapeDtypeStruct((S, D), ACT_DTYPE)],
        compiler_params=_params(("arbitrary",)),
    )(ycat, x, vec, wout)


def out_proj_bwd(dxo, y, ycat, vec, wout, name, carry=None):
    S = dxo.shape[0]
    tm = _row_tile(S, 512)

    def body(dxo_ref, y_ref, yc_ref, vec_ref, w_ref, dyb_ref, dyc_ref, dl_ref, sums_ref):
        dxo = dxo_ref[...]
        dgate = jnp.sum(dxo * y_ref[...].astype(F32), axis=0, keepdims=True)
        dy = (vec_ref[3:4, :] * dxo).astype(MXU_DTYPE)
        dyb_ref[...] = dy
        dyc_ref[...] = lax.dot_general(dy, w_ref[...], NT_DIMS, preferred_element_type=F32)
        m_a = _head_masks(tm)
        for p in range(AW // LANES):
            cs = slice(p * LANES, (p + 1) * LANES)
            s_a, s_b = _pair_stat(dyc_ref[:, cs] * yc_ref[:, cs].astype(F32), m_a)
            dl_ref[:, cs] = jnp.where(m_a, s_a, s_b)
        _acc_rows(sums_ref, pl.program_id(0) == 0, (dgate,))

    t = pl.BlockSpec((tm, D), lambda i: (i, 0))
    at = pl.BlockSpec((tm, AW), lambda i: (i, 0))
    return _pcall(
        body, name, (S // tm,),
        [t, t, t, pl.BlockSpec((SUBLANES, D), lambda i: (0, 0)), pl.BlockSpec((D, D), lambda i: (0, 0))],
        [t, t, at, pl.BlockSpec((SUBLANES, D), lambda i: (0, 0))],
        [jax.ShapeDtypeStruct((S, D), MXU_DTYPE), jax.ShapeDtypeStruct((S, D), F32),
         jax.ShapeDtypeStruct((S, AW), F32), jax.ShapeDtypeStruct((SUBLANES, D), F32)],
        ("arbitrary",), (dxo, y, ycat, vec, wout), carry)


def mixer_mid_bwd(dqs, dks, dvs, proj, dycat, gvec, cw, name, carry=None):
    S = proj.shape[0]
    tm = _row_tile(S, 512)
    hb = tm // SUBLANES
    hp = tm // HALO_ROWS
    nsl = S // SUBLANES
    ntile = S // tm

    def body(dq1, dq2, dq3, dk1, dk2, dk3, dv1, dv2, dv3, pr_ref, pp_ref, pn_ref, dyc_ref, dyn_ref,
             g_ref, cw_ref, dp_ref, sums_ref):
        i = pl.program_id(0)
        m_a = _head_masks(tm)
        gsum = []
        for which, parts in ((0, (dq1, dq2, dq3)), (1, (dk1, dk2, dk3))):
            acc_g = []
            for p in range(AW // LANES):
                lo = which * AW + p * LANES
                cs = slice(p * LANES, (p + 1) * LANES)
                xp = pr_ref[:, lo:lo + LANES].astype(F32)
                s_a, s_b = _pair_stat(xp * xp, m_a)
                rr = jnp.where(m_a, lax.rsqrt(s_a * (1.0 / HD) + EPS), lax.rsqrt(s_b * (1.0 / HD) + EPS))
                xh = xp * rr
                dn = parts[0][:, cs] + parts[1][:, cs] + parts[2][:, cs]
                acc_g.append(jnp.sum(dn * xh, axis=0, keepdims=True))
                t = dn * g_ref[which:which + 1, cs]
                t_a, t_b = _pair_stat(t * xh, m_a)
                mean = jnp.where(m_a, t_a, t_b) * (1.0 / HD)
                dp_ref[:, lo:lo + LANES] = (rr * (t - xh * mean)).astype(dp_ref.dtype)
            gsum.append(jnp.concatenate(acc_g, axis=1))
        dp_ref[:, 2 * AW:3 * AW] = (dv1[...] + dv2[...] + dv3[...]).astype(dp_ref.dtype)
        row = lax.broadcasted_iota(jnp.int32, (tm, CW), 0)
        base = 3 * AW
        gb, gc, u = (pr_ref[:, base + k * CW:base + (k + 1) * CW].astype(F32) for k in range(3))
        cu = gc * u
        pp = pp_ref[...].astype(F32)[HALO_ROWS - SUBLANES:]
        halo_cu = jnp.where(i > 0, pp[:, CW:2 * CW] * pp[:, 2 * CW:3 * CW], 0.0)
        cv, u1, u2 = _conv_fwd(cu, halo_cu, cw_ref, row)
        dyc = dyc_ref[...]
        dp_ref[:, base:base + CW] = (dyc * cv).astype(dp_ref.dtype)
        dcv = dyc * gb
        gb_next = pn_ref[:, 0:CW].astype(F32)[:SUBLANES]
        halo_dcv = jnp.where(i < ntile - 1, dyn_ref[...] * gb_next, 0.0)
        d1 = _shift_up(dcv, halo_dcv, 1, row)
        d2 = _shift_up(dcv, halo_dcv, 2, row)
        dcu = cw_ref[2:3, :] * dcv + cw_ref[1:2, :] * d1 + cw_ref[0:1, :] * d2
        dp_ref[:, base + CW:base + 2 * CW] = (dcu * u).astype(dp_ref.dtype)
        dp_ref[:, base + 2 * CW:base + 3 * CW] = (dcu * gc).astype(dp_ref.dtype)
        rows = (gsum[0], gsum[1],
                jnp.sum(dcv * u2, axis=0, keepdims=True), jnp.sum(dcv * u1, axis=0, keepdims=True),
                jnp.sum(dcv * cu, axis=0, keepdims=True), jnp.sum(dcv, axis=0, keepdims=True))
        _acc_rows(sums_ref, i == 0, rows)

    at = pl.BlockSpec((tm, AW), lambda i: (i, 0))
    return _pcall(
        body, name, (ntile,),
        [at] * 9 + [
            pl.BlockSpec((tm, INC), lambda i: (i, 0)),
            pl.BlockSpec((HALO_ROWS, 3 * CW), lambda i: (jnp.maximum(i * hp - 1, 0), 1)),
            pl.BlockSpec((HALO_ROWS, 3 * CW), lambda i: (jnp.minimum((i + 1) * hp, S // HALO_ROWS - 1), 1)),
            pl.BlockSpec((tm, CW), lambda i: (i, 1)),
            pl.BlockSpec((SUBLANES, CW), lambda i: (jnp.minimum((i + 1) * hb, nsl - 1), 1)),
            pl.BlockSpec((SUBLANES, AW), lambda i: (0, 0)),
            pl.BlockSpec((SUBLANES, CW), lambda i: (0, 0))],
        [pl.BlockSpec((tm, INC), lambda i: (i, 0)), pl.BlockSpec((SUBLANES, AW), lambda i: (0, 0))],
        [jax.ShapeDtypeStruct((S, INC), MXU_DTYPE), jax.ShapeDtypeStruct((SUBLANES, AW), F32)],
        ("arbitrary",), (*dqs, *dks, *dvs, proj, proj, proj, dycat, dycat, gvec, cw), carry)


def mixer_in_bwd(dxo, x, dproj, vec, winp, name, carry=None):
    S = x.shape[0]
    tm = _row_tile(S, 512)
    pc = INC // NCHIP

    def body(dxo_ref, x_ref, dp_ref, vec_ref, w_ref, dxi_ref, sums_ref):
        xhat, r, gain, ng, sc, _, _ = _ada(x_ref[...], vec_ref)
        dh = jnp.zeros((tm, D), F32)
        for j in range(NCHIP):
            dh = dh + lax.dot_general(dp_ref[:, j * pc:(j + 1) * pc], w_ref[j], NT_DIMS, preferred_element_type=F32)
        dx, dshift, dscale, dng = _ada_bwd(dh, xhat, r, gain, ng, sc)
        dxi_ref[...] = dxo_ref[...] + dx
        _acc_rows(sums_ref, pl.program_id(0) == 0, (dshift, dscale, dng))

    t = pl.BlockSpec((tm, D), lambda i: (i, 0))
    return _pcall(
        body, name, (S // tm,),
        [t, t, pl.BlockSpec((tm, INC), lambda i: (i, 0)),
         pl.BlockSpec((SUBLANES, D), lambda i: (0, 0)),
         pl.BlockSpec((NCHIP, D, pc), lambda i: (0, 0, 0), pipeline_mode=pl.Buffered(1))],
        [t, pl.BlockSpec((SUBLANES, D), lambda i: (0, 0))],
        [jax.ShapeDtypeStruct((S, D), F32), jax.ShapeDtypeStruct((SUBLANES, D), F32)],
        ("arbitrary",), (dxo, x, dproj, vec, winp), carry)


def _vec(mod_l, ng_l, i):
    m = mod_l.reshape(3, 3, D)
    rows = jnp.stack([ng_l[i], m[i, 1], m[i, 0], m[i, 2]])
    return jnp.concatenate([rows, jnp.zeros((SUBLANES - 4, D), F32)], axis=0)


def local_step(x, target, mods, ngs, gvecs, cws, shards, w_first, cflag):
    saved = []
    weights = [dict(w1=[None, None], w2=[None, None]) for _ in range(2)]
    weights[0]["w1"][0], weights[0]["w2"][0] = w_first[0], w_first[1].reshape(DFF, D)
    h = x
    for l in range(2):
        w, sh = weights[l], shards[l]
        nxt = shards[l + 1] if l == 0 else None
        vecs = [_vec(mods[l], ngs[l], i) for i in range(3)]
        x0 = h
        (x1, a0, f0), (win, wout, w2b) = ffn_fwd(x0, vecs[0], w["w1"][0], w["w2"][0], 0.5, f"ffn_fwd_l{l}a",
                                                 carry=Carry("gather", [sh["win"], sh["wout"], sh["w2"][1]]))
        w["win"], w["wout"], w["w2"][1] = win, wout.reshape(D, D), w2b.reshape(DFF, D)
        proj, h1b, qn, kn, v = mixer_in(x1, vecs[1], w["win"], gvecs[l], f"mixer_in_l{l}")
        os_, lses, w1b = [], [], {}
        for d in DILATIONS:
            rows = {1: slice(0, D // 2), 16: slice(D // 2, D)}.get(d)
            carry = Carry("gather", [sh["w1"][1][rows]]) if rows else None
            (o, lse_d), w1b[d] = attn_fwd(qn, kn, v, d, f"attn_fwd_l{l}_d{d}", carry=carry)
            os_.append(o)
            lses.append(lse_d)
        w["w1"][1] = jnp.concatenate([w1b[1][0], w1b[16][0]], axis=1)
        (ycat, lse), got = combine_conv(os_, lses, proj, cws[l], f"combine_conv_l{l}",
                                        carry=Carry("gather", [nxt["w2"][0]]) if nxt else None)
        if nxt:
            weights[1]["w2"][0] = got[0].reshape(DFF, D)
        x2, y = out_proj(ycat, x1, vecs[1], w["wout"], f"out_proj_l{l}")
        if nxt:
            (h, a2, f2), got = ffn_fwd(x2, vecs[2], w["w1"][1], w["w2"][1], 0.5, f"ffn_fwd_l{l}b",
                                       carry=Carry("gather", [nxt["w1"][0]]))
            weights[1]["w1"][0] = got[0]
        else:
            (dx, a2, f2, loss_blk), _ = ffn_fwd(x2, vecs[2], w["w1"][1], w["w2"][1], 0.5, f"ffn_fwd_l{l}b",
                                                target=target)
        saved.append(dict(vecs=vecs, x0=x0, a0=a0, f0=f0, x1=x1, proj=proj, h1b=h1b, qn=qn, kn=kn, v=v,
                          ycat=ycat, lse=lse, y=y, x2=x2, a2=a2, f2=f2))
    sums, totals, g_prev = [None, None], [None, None], None
    w2r = DFF // NCHIP
    for l in (1, 0):
        w, s = weights[l], saved[l]
        vecs = s["vecs"]
        ride = g_prev is not None
        own = l == 0
        mine, other = [None] * 6, [None] * 6

        def half_sum(group, recv, k0):
            return [add_half(g, r, cflag, f"add_sibling_l{l}_{k0 + j}") for j, (g, r) in enumerate(zip(group, recv))]

        def chip_sum(landed, k0):
            return [sum_chips(t, f"sum_chips_l{l}_{k0 + j}") for j, t in enumerate(landed)]

        (dx, hb, dfb, act, da, sums2), got = ffn_bwd(
            dx, s["x2"], s["a2"], s["f2"], vecs[2], w["w1"][1], w["w2"][1], 0.5, f"ffn_bwd_l{l}b",
            carry=Carry("swap_halves", g_prev) if ride else None)
        dw1b, _ = wgrad(hb, da, D, HALF, f"wgrad_w1_l{l}b")
        dw2b, _ = wgrad(act, dfb, HALF, D, f"wgrad_w2_l{l}b")
        if ride:
            wire = [add_half(g_prev[k], got[k], cflag, f"add_sibling_l{l + 1}_{k}") for k in range(6)]
        g_ffn_b = [dw1b, dw2b.reshape(NCHIP, w2r, D)]
        (dyb, dycat, delta, sums_o), got = out_proj_bwd(
            dx, s["y"], s["ycat"], vecs[1], w["wout"], f"out_proj_bwd_l{l}",
            carry=Carry("swap_halves", g_ffn_b) if own else None)
        dwout, _ = wgrad(s["ycat"].astype(MXU_DTYPE), dyb, D // 2, D, f"wgrad_wout_l{l}")
        if own:
            wire_ffn_b = half_sum(g_ffn_b, got, 4)
        dqs, dks, dvs, landed = [], [], [], {}
        for d in DILATIONS:
            carry = None
            if ride and d == 1:
                carry = Carry("scatter", wire[3:])
            if ride and d == 16:
                carry = Carry("scatter", wire[:3])
            if own and d == 4:
                carry = Carry("scatter", wire_ffn_b)
            (dq, dk, dv), landed[d] = attn_bwd(s["qn"], s["kn"], s["v"], dycat, s["lse"], delta, d,
                                               f"attn_bwd_l{l}_d{d}", carry=carry)
            dqs.append(dq)
            dks.append(dk)
            dvs.append(dv)
        if ride:
            tot = [sum_chips(t, f"sum_chips_l{l + 1}_{k}") for k, t in enumerate(list(landed[16]) + list(landed[1]))]
        if own:
            mine[4:6] = chip_sum(landed[4], 4)
        ready = (tot if ride else []) + (mine[4:6] if own else [])
        (dproj, sums_m), got = mixer_mid_bwd(dqs, dks, dvs, s["proj"], dycat, gvecs[l], cws[l], f"mixer_mid_bwd_l{l}",
                                             carry=Carry("swap", ready) if ready else None)
        if ride:
            totals[l + 1] = (tot, list(got[:6]))
        if own:
            other[4:6] = list(got[-2:])
        dwin, _ = wgrad(s["h1b"], dproj, D, INC // NCHIP, f"wgrad_win_l{l}")
        g_mixer = [dwin, dwout.reshape(NCHIP, D // NCHIP, D)]
        (dx, sums1), got = mixer_in_bwd(dx, s["x1"], dproj, vecs[1], w["win"], f"mixer_in_bwd_l{l}",
                                        carry=Carry("swap_halves", g_mixer) if own else None)
        if own:
            wire_mixer = half_sum(g_mixer, got, 2)
        (dx, hb, dfb, act, da, sums0), _ = ffn_bwd(
            dx, s["x0"], s["a0"], s["f0"], vecs[0], w["w1"][0], w["w2"][0], 0.5, f"ffn_bwd_l{l}a")
        dw1a, got = wgrad(hb, da, D, HALF, f"wgrad_w1_l{l}a", carry=Carry("scatter", wire_mixer) if own else None)
        if own:
            mine[2:4] = chip_sum(got, 2)
        dw2a, got = wgrad(act, dfb, HALF, D, f"wgrad_w2_l{l}a", carry=Carry("swap", mine[2:4]) if own else None)
        g_ffn_a = [dw1a, dw2a.reshape(NCHIP, w2r, D)]
        if own:
            other[2:4] = list(got)
            wire_ffn_a = half_sum(g_ffn_a, run_carry(Carry("swap_halves", g_ffn_a), "swap_halves_tail"), 0)
            mine[0:2] = chip_sum(run_carry(Carry("scatter", wire_ffn_a), "scatter_grads_tail"), 0)
            other[0:2] = list(run_carry(Carry("swap", mine[0:2]), "swap_totals_tail"))
            totals[l] = (mine, other)
        g_prev = g_ffn_a + g_mixer + g_ffn_b
        sums[l] = (sums0, sums1, sums_o, sums2, sums_m)
    return loss_blk, dx, totals, sums


def small_all_gather(blk, name):
    m_per, n = blk.shape

    def body(x_ref, out_ref, send_sems, recv_sems, local_sem):
        x, y, c = _here()
        me, sibling = (x, y, c), (x, y, 1 - c)
        chips = [(1 - x, y), (x, 1 - y), (1 - x, 1 - y)]

        def rows(px, py, pc):
            return out_ref.at[pl.ds((4 * px + 2 * py + pc) * m_per, m_per), :]

        def copy(k, block, to, src=None):
            return pltpu.make_async_remote_copy(
                src_ref=rows(*block) if src is None else src, dst_ref=rows(*block),
                send_sem=send_sems.at[k], recv_sem=recv_sems.at[k], device_id=to, device_id_type=MESH)

        mine = pltpu.make_async_copy(x_ref, rows(*me), local_sem)
        mine.start()
        first = [copy(0, me, sibling, src=x_ref)]
        first += [copy(1 + j, me, (*chip, c), src=x_ref) for j, chip in enumerate(chips)]
        for cp in first:
            cp.start()
        passed = [copy(4 + j, (*chip, c), sibling) for j, chip in enumerate(chips)]
        for j, chip in enumerate(chips):
            copy(1 + j, (*chip, c), me).wait_recv()
            passed[j].start()
        copy(0, sibling, me).wait_recv()
        for j, chip in enumerate(chips):
            copy(4 + j, (*chip, 1 - c), me).wait_recv()
        for cp in first + passed:
            cp.wait_send()
        mine.wait()

    return pl.pallas_call(
        body, name=name,
        out_shape=jax.ShapeDtypeStruct((NDEV * m_per, n), blk.dtype),
        in_specs=[pl.BlockSpec(memory_space=pltpu.VMEM)],
        out_specs=pl.BlockSpec(memory_space=pltpu.VMEM),
        scratch_shapes=[pltpu.SemaphoreType.DMA((7,)), pltpu.SemaphoreType.DMA((7,)), pltpu.SemaphoreType.DMA],
        compiler_params=pltpu.CompilerParams(vmem_limit_bytes=VMEM_LIMIT),
    )(blk)


EW_BLOCK_BYTES = 1 << 20


def _ew_rows(rows, cols, refs=8):
    want = max(16, EW_BLOCK_BYTES * (2 if refs <= 4 else 1) // (4 * cols))
    best = None
    for t in range(16, rows + 1, 16):
        if rows % t == 0 and t <= want:
            best = t
    return best if best is not None else rows


def add_half(g, recv, cflag, name):
    pieces, r, cols = g.shape
    r2 = r // 2
    tr = _ew_rows(r2, cols, refs=3)
    nt = r2 // tr

    def body(c_ref, g_ref, r_ref, o_ref):
        o_ref[...] = (g_ref[...] + r_ref[...]).astype(o_ref.dtype)

    half = pl.BlockSpec((None, tr, cols), lambda j, i, c_ref: (j, i, 0))
    return pl.pallas_call(
        body, name=name,
        grid_spec=pltpu.PrefetchScalarGridSpec(
            num_scalar_prefetch=1, grid=(pieces, nt),
            in_specs=[pl.BlockSpec((None, tr, cols), lambda j, i, c_ref: (j, c_ref[0] * nt + i, 0)), half],
            out_specs=half),
        out_shape=jax.ShapeDtypeStruct((pieces, r2, cols), WIRE_DTYPE),
        compiler_params=_params(("arbitrary", "arbitrary")),
    )(cflag, g, recv)


def sum_chips(recv, name):
    _, r, cols = recv.shape
    tr = _ew_rows(r, cols, refs=3)

    def body(r_ref, o_ref):
        acc = r_ref[0].astype(F32)
        for k in range(1, NCHIP):
            acc = acc + r_ref[k].astype(F32)
        o_ref[...] = acc

    return pl.pallas_call(
        body, name=name, grid=(r // tr,),
        in_specs=[pl.BlockSpec((NCHIP, tr, cols), lambda i: (0, i, 0))],
        out_specs=pl.BlockSpec((tr, cols), lambda i: (i, 0)),
        out_shape=jax.ShapeDtypeStruct((r, cols), F32),
        compiler_params=_params(("arbitrary",)),
    )(recv)


def sum_devices(rows8, name):
    def body(r_ref, o_ref):
        acc = r_ref[0:1, :]
        for k in range(1, NDEV):
            acc = acc + r_ref[k:k + 1, :]
        o_ref[...] = jnp.broadcast_to(acc, o_ref.shape)

    return pl.pallas_call(
        body, name=name, out_shape=jax.ShapeDtypeStruct(rows8.shape, F32),
        in_specs=[pl.BlockSpec(memory_space=pltpu.VMEM)], out_specs=pl.BlockSpec(memory_space=pltpu.VMEM),
        compiler_params=pltpu.CompilerParams(vmem_limit_bytes=VMEM_LIMIT),
    )(rows8)


def adamw(w, m, v, srcs, cflag, name, halves=False):
    planes, r, cols = w.shape
    rh = r // 2 if halves else r
    tr = _ew_rows(rh, cols)
    nth = rh // tr
    flat = [a for s in srcs for a in (s if halves else (s,))]
    ns = len(flat)
    per = ns // planes

    def body(c_ref, w_ref, m_ref, v_ref, *rest):
        s_refs, (g_ref, d_ref, mo_ref, vo_ref) = rest[:ns], rest[ns:]
        p, i = pl.program_id(0), pl.program_id(1)
        if halves:
            mine = jnp.logical_not(jnp.logical_xor(i >= nth, c_ref[0] == 1))
            blocks = [jnp.where(mine, s_refs[2 * k][...], s_refs[2 * k + 1][...]) for k in range(planes)]
        else:
            blocks = [s[...] for s in s_refs]
        g = blocks[0]
        for k in range(1, planes):
            g = jnp.where(p == k, blocks[k], g)
        g_ref[...] = g
        m_new = ADAM_B1 * m_ref[...] + (1.0 - ADAM_B1) * g
        v_new = ADAM_B2 * v_ref[...] + (1.0 - ADAM_B2) * (g * g)
        mo_ref[...] = m_new
        vo_ref[...] = v_new
        m_hat = m_new / (1.0 - ADAM_B1 ** ADAM_STEP)
        v_hat = v_new / (1.0 - ADAM_B2 ** ADAM_STEP)
        d_ref[...] = -ADAM_LR * (m_hat / (jnp.sqrt(v_hat) + ADAM_EPS) + ADAM_WD * w_ref[...])

    pt = pl.BlockSpec((None, tr, cols), lambda p, i: (p, i, 0))
    st = [pl.BlockSpec((tr, cols), functools.partial(lambda k, p, i: (jnp.where(p == k, i % nth, 0), 0), j // per))
          for j in range(ns)]
    return pl.pallas_call(
        body, name=name, grid=(planes, r // tr),
        in_specs=[pl.BlockSpec(memory_space=pltpu.SMEM), pt, pt, pt] + st,
        out_specs=[pt] * 4,
        out_shape=[jax.ShapeDtypeStruct(w.shape, F32)] * 4,
        compiler_params=_params(("arbitrary", "arbitrary")),
    )(cflag, w, m, v, *flat)


ADA_COLS = 9 * D // NCHIP


def mod_fwd(c_all, w_ada, b_shard, name):
    def body(c_ref, w_ref, b_ref, o_ref):
        cc = c_ref[...]
        sc = cc * jax.nn.sigmoid(cc)
        o_ref[...] = jnp.dot(sc, w_ref[...], preferred_element_type=F32,
                             precision=lax.Precision.HIGHEST) + b_ref[...]

    return pl.pallas_call(
        body, name=name, grid=(2,),
        in_specs=[pl.BlockSpec((NDEV, D), lambda l: (0, 0)),
                  pl.BlockSpec((None, D, ADA_COLS), lambda l: (l, 0, 0)),
                  pl.BlockSpec((None, 1, ADA_COLS), lambda l: (l, 0, 0))],
        out_specs=pl.BlockSpec((None, NDEV, ADA_COLS), lambda l: (l, 0, 0)),
        out_shape=jax.ShapeDtypeStruct((2, NDEV, ADA_COLS), F32),
        compiler_params=_params(("arbitrary",)),
    )(c_all, w_ada, b_shard.reshape(2, 1, ADA_COLS))


def wada_grad(c_all_t, dmod, name):
    ct = ADA_COLS // 3

    def body(c_ref, d_ref, o_ref):
        cc = c_ref[...]
        sc = cc * jax.nn.sigmoid(cc)
        acc = sc[:, 0:1] * d_ref[0:1, :]
        for b in range(1, NDEV):
            acc = acc + sc[:, b:b + 1] * d_ref[b:b + 1, :]
        o_ref[...] = acc

    return pl.pallas_call(
        body, name=name, grid=(2, 3),
        in_specs=[pl.BlockSpec((D, LANES), lambda l, j: (0, 0)),
                  pl.BlockSpec((None, NDEV, ct), lambda l, j: (l, 0, j))],
        out_specs=pl.BlockSpec((None, D, ct), lambda l, j: (l, 0, j)),
        out_shape=jax.ShapeDtypeStruct((2, D, ADA_COLS), F32),
        compiler_params=_params(("arbitrary", "arbitrary")),
    )(c_all_t, dmod)


def _pad_rows(row, rows=SUBLANES):
    return jnp.concatenate([row[None, :], jnp.zeros((rows - 1, row.shape[0]), row.dtype)], axis=0)


def kernel(x, c, w_ada, b_ada, norm_g, w_in, q_norm_g, k_norm_g, conv_w, conv_b, w_out, ffn_w1, ffn_w2, loss_target, m_w_ada, m_b_ada, m_norm_g, m_w_in, m_q_norm_g, m_k_norm_g, m_conv_w, m_conv_b, m_w_out, m_ffn_w1, m_ffn_w2, v_w_ada, v_b_ada, v_norm_g, v_w_in, v_q_norm_g, v_k_norm_g, v_conv_w, v_conv_b, v_w_out, v_ffn_w1, v_ffn_w2):
    ix, iy, ic = lax.axis_index("x"), lax.axis_index("y"), lax.axis_index("c")
    chip = 2 * ix + iy
    dev = 2 * chip + ic
    cflag = jnp.reshape(ic, (1,)).astype(jnp.int32)
    ngw = norm_g.shape[-1]
    cww = conv_w.shape[-1]

    pack = jnp.concatenate([c[0], norm_g.reshape(-1), conv_w.reshape(-1)])
    got = small_all_gather(_pad_rows(pack), "gather_c_normg_convw")[::SUBLANES]
    c_all = got[:, :D]
    per_chip = got[::2]
    ng_full = jnp.concatenate([per_chip[j, D:D + 6 * ngw].reshape(2, 3, ngw) for j in range(NCHIP)], axis=-1)
    cw_full = jnp.concatenate([per_chip[j, D + 6 * ngw:].reshape(2, 3, cww) for j in range(NCHIP)], axis=-1)

    b_shard = lax.dynamic_slice_in_dim(b_ada, chip * ADA_COLS, ADA_COLS, axis=1)
    mod_blk = mod_fwd(c_all, w_ada, b_shard, "mod_fwd").reshape(2 * NDEV, ADA_COLS)
    mod_all = small_all_gather(mod_blk, "gather_mod").reshape(NDEV, 2, NDEV, ADA_COLS)[::2]
    mod_mine = lax.dynamic_index_in_dim(mod_all, dev, axis=2, keepdims=False)
    mods = [mod_mine[:, l, :].reshape(-1) for l in range(2)]

    shards, gvecs, cws = [], [], []
    for l in range(2):
        shards.append(dict(w1=[ffn_w1[l, i].astype(MXU_DTYPE) for i in range(2)],
                           w2=[ffn_w2[l, i].astype(MXU_DTYPE) for i in range(2)],
                           win=w_in[l].astype(MXU_DTYPE), wout=w_out[l].astype(MXU_DTYPE)))
        gv = jnp.stack([jnp.tile(q_norm_g[l], AW // HD), jnp.tile(k_norm_g[l], AW // HD)])
        gvecs.append(jnp.concatenate([gv, jnp.zeros((SUBLANES - 2, AW), F32)], axis=0))
        cws.append(jnp.concatenate([cw_full[l], conv_b[l][None, :], jnp.zeros((SUBLANES - 4, CW), F32)], axis=0))
    w_first = gather_split([shards[0]["w1"][0], shards[0]["w2"][0]], "gather_first_ffn")

    loss_blk, dx, totals, sums = local_step(x[0], loss_target[0], mods, [ng_full[0], ng_full[1]], gvecs, cws,
                                            shards, w_first, cflag)

    dmods, dngs, dqg, dkg, dcw, dcb = [], [], [], [], [], []
    for l in range(2):
        s0, s1, so, s2, sm = sums[l]
        dmods.append(jnp.concatenate([s0[0], s0[1], s0[3], s1[0], s1[1], so[0], s2[0], s2[1], s2[3]]))
        dngs.append(jnp.concatenate([s0[2], s1[2], s2[2]]))
        dqg.append(sm[0].reshape(AW // HD, HD).sum(0))
        dkg.append(sm[1].reshape(AW // HD, HD).sum(0))
        dcw.append(sm[2:5].reshape(-1))
        dcb.append(sm[5])
    small = jnp.concatenate(dmods + dngs + dqg + dkg + dcw + dcb + [loss_blk[0]])
    small_all = small_all_gather(_pad_rows(small), "gather_small_grads")[::SUBLANES]
    nm = 9 * D
    dmod_all = small_all[:, :2 * nm].reshape(NDEV, 2, NCHIP, ADA_COLS)
    dmod_mine = lax.dynamic_index_in_dim(dmod_all, chip, axis=2, keepdims=False).transpose(1, 0, 2)
    tot = sum_devices(small_all, "sum_small_grads")[0]
    o = 2 * nm
    g_b_ada = tot[:o].reshape(2, nm)
    g_norm_g = lax.dynamic_slice_in_dim(tot[o:o + 6 * D].reshape(2, 3, D), chip * ngw, ngw, axis=2)
    o += 6 * D
    g_qg = tot[o:o + 2 * HD].reshape(2, HD)
    o += 2 * HD
    g_kg = tot[o:o + 2 * HD].reshape(2, HD)
    o += 2 * HD
    g_cw = lax.dynamic_slice_in_dim(tot[o:o + 6 * CW].reshape(2, 3, CW), chip * cww, cww, axis=2)
    o += 6 * CW
    g_cb = tot[o:o + 2 * CW].reshape(2, CW)
    loss = tot[o + 2 * CW]

    c_all_t = jnp.concatenate([c_all.T, jnp.zeros((D, LANES - NDEV), F32)], axis=1)
    g_wada_src = wada_grad(c_all_t, dmod_mine, "wada_grad")

    def halves(k_of_plane):
        return [(totals[l][0][k], totals[l][1][k]) for l, k in k_of_plane]

    r_wada = adamw(w_ada, m_w_ada, v_w_ada, [g_wada_src[0], g_wada_src[1]], cflag, "adamw_w_ada")
    r_win = adamw(w_in, m_w_in, v_w_in, halves([(0, 2), (1, 2)]), cflag, "adamw_w_in", halves=True)
    r_wout = adamw(w_out, m_w_out, v_w_out, halves([(0, 3), (1, 3)]), cflag, "adamw_w_out", halves=True)
    r_w1 = adamw(ffn_w1.reshape(4, D, HALF), m_ffn_w1.reshape(4, D, HALF), v_ffn_w1.reshape(4, D, HALF),
                 halves([(0, 0), (0, 4), (1, 0), (1, 4)]), cflag, "adamw_ffn_w1", halves=True)
    w2r = DFF // NCHIP
    r_w2 = adamw(ffn_w2.reshape(4, w2r, D), m_ffn_w2.reshape(4, w2r, D), v_ffn_w2.reshape(4, w2r, D),
                 halves([(0, 1), (0, 5), (1, 1), (1, 5)]), cflag, "adamw_ffn_w2", halves=True)
    r_w1 = [t.reshape(ffn_w1.shape) for t in r_w1]
    r_w2 = [t.reshape(ffn_w2.shape) for t in r_w2]

    smalls = [("b_ada", b_ada, m_b_ada, v_b_ada, g_b_ada), ("norm_g", norm_g, m_norm_g, v_norm_g, g_norm_g),
              ("q_norm_g", q_norm_g, m_q_norm_g, v_q_norm_g, g_qg), ("k_norm_g", k_norm_g, m_k_norm_g, v_k_norm_g, g_kg),
              ("conv_w", conv_w, m_conv_w, v_conv_w, g_cw), ("conv_b", conv_b, m_conv_b, v_conv_b, g_cb)]
    n_small = sum(t[1].size for t in smalls)
    pad = (-n_small) % (16 * LANES)

    def packed(idx):
        flat = jnp.concatenate([t[idx].reshape(-1) for t in smalls] + [jnp.zeros((pad,), F32)])
        return flat.reshape(-1, LANES)

    r_small = adamw(packed(1)[None], packed(2)[None], packed(3)[None], [packed(4)], cflag, "adamw_small")
    small_out = {}
    o = 0
    for name_, w_, _, _, _ in smalls:
        small_out[name_] = [t.reshape(-1)[o:o + w_.size].reshape(w_.shape) for t in r_small]
        o += w_.size

    res = {"w_ada": r_wada, "w_in": r_win, "w_out": r_wout, "ffn_w1": r_w1, "ffn_w2": r_w2, **small_out}
    order = ["w_ada", "b_ada", "norm_g", "w_in", "q_norm_g", "k_norm_g", "conv_w", "conv_b", "w_out", "ffn_w1", "ffn_w2"]
    outs = [loss, dx[None]]
    for k in range(4):
        outs += [res[nm_][k] for nm_ in order]
    return tuple(outs)
```

```python
import functools

import jax
import jax.numpy as jnp
from jax import lax
from jax.experimental import pallas as pl
from jax.experimental.pallas import tpu as pltpu

F32 = jnp.float32
MXU_DTYPE = jnp.bfloat16
ACT_DTYPE = jnp.bfloat16
WIRE_DTYPE = jnp.bfloat16

D = 1024
HD = 64
AW = 512
CW = 512
DFF = 2816
HALF = DFF // 2
INC = 3 * AW + 3 * CW
NCHIP = 4
NDEV = 8
QBLK = 128
ATTN_QBLOCKS = 8
ATTN_INTERLEAVE = 8
ATTN_CHUNK_ROWS = 4096
DILATIONS = (1, 4, 16)
EPS = 1e-6
NEG = -1e30
LANES = 128
SUBLANES = 8
HALO_ROWS = 16
VMEM_LIMIT = 56 * 1024 * 1024

ADAM_LR = 0.001
ADAM_B1 = 0.9
ADAM_B2 = 0.999
ADAM_EPS = 1e-08
ADAM_WD = 0.01
ADAM_STEP = 10

NT_DIMS = (((1,), (1,)), ((), ()))
TN_DIMS = (((0,), (0,)), ((), ()))


def _params(sem, vmem=VMEM_LIMIT):
    return pltpu.CompilerParams(dimension_semantics=sem, vmem_limit_bytes=vmem)


def _row_tile(n, want):
    t = min(n, want)
    assert n % t == 0
    return t


def _ada(xt, vec_ref):
    ng, sc, sh, gt = vec_ref[0:1, :], vec_ref[1:2, :], vec_ref[2:3, :], vec_ref[3:4, :]
    r = lax.rsqrt(jnp.mean(xt * xt, axis=-1, keepdims=True) + EPS)
    return xt * r, r, ng * (1.0 + sc), ng, sc, sh, gt


def _ada_bwd(dh, xhat, r, gain, ng, sc):
    dshift = jnp.sum(dh, axis=0, keepdims=True)
    dhx = dh * xhat
    dscale = jnp.sum(dhx, axis=0, keepdims=True) * ng
    dng = jnp.sum(dhx, axis=0, keepdims=True) * (1.0 + sc)
    dxhat = dh * gain
    dx = r * (dxhat - xhat * jnp.mean(dxhat * xhat, axis=-1, keepdims=True))
    return dx, dshift, dscale, dng


def _acc_rows(sums_ref, first, rows):
    @pl.when(first)
    def _():
        sums_ref[...] = jnp.zeros_like(sums_ref)
    for k, row in enumerate(rows):
        sums_ref[k:k + 1, :] += row


MESH = pl.DeviceIdType.MESH
ANY = pl.BlockSpec(memory_space=pl.ANY)


def _here():
    return lax.axis_index("x"), lax.axis_index("y"), lax.axis_index("c")


def _ici_copies(src_refs, dst_refs, send_sems, recv_sems, local_sems, scatter):
    x, y, c = _here()
    my_chip = 2 * x + y
    peers = [(1 - x, y), (x, 1 - y), (1 - x, 1 - y)]
    local, out, inc = [], [], []
    for a, (src, dst) in enumerate(zip(src_refs, dst_refs)):
        local.append(pltpu.make_async_copy(src.at[my_chip] if scatter else src, dst.at[my_chip], local_sems.at[a]))
        for j, (px, py) in enumerate(peers):
            sems = dict(send_sem=send_sems.at[3 * a + j], recv_sem=recv_sems.at[3 * a + j],
                        device_id=(px, py, c), device_id_type=MESH)
            out.append(pltpu.make_async_remote_copy(
                src_ref=src.at[2 * px + py] if scatter else src, dst_ref=dst.at[my_chip], **sems))
            inc.append(pltpu.make_async_remote_copy(
                src_ref=src.at[my_chip] if scatter else src, dst_ref=dst.at[2 * px + py], **sems))
    return local, out, inc


def _swap_copies(src_refs, dst_refs, send_sems, recv_sems, halves):
    x, y, c = _here()
    cps = []
    for k, (src, dst) in enumerate(zip(src_refs, dst_refs)):
        if halves:
            r2 = src.shape[1] // 2
            src = src.at[:, pl.ds((1 - c) * r2, r2), :]
        cps.append(pltpu.make_async_remote_copy(
            src_ref=src, dst_ref=dst, send_sem=send_sems.at[k], recv_sem=recv_sems.at[k],
            device_id=(x, y, 1 - c), device_id_type=MESH))
    return cps


class Carry:
    def __init__(self, kind, srcs):
        self.kind, self.srcs, n = kind, list(srcs), len(srcs)
        if kind == "gather":
            shapes = [(NCHIP,) + s.shape for s in srcs]
        elif kind == "swap_halves":
            shapes = [(s.shape[0], s.shape[1] // 2, s.shape[2]) for s in srcs]
        else:
            shapes = [s.shape for s in srcs]
        self.out_shape = [jax.ShapeDtypeStruct(sh, s.dtype) for sh, s in zip(shapes, srcs)]
        dma = pltpu.SemaphoreType.DMA
        self.sems = [dma((3 * n,)), dma((3 * n,)), dma((n,))] if kind in ("gather", "scatter") else [dma((n,)), dma((n,))]

    def start(self, srcs, dsts, sems):
        if self.kind in ("gather", "scatter"):
            local, out, _ = _ici_copies(srcs, dsts, *sems, self.kind == "scatter")
            for cp in local + out:
                cp.start()
        else:
            for cp in _swap_copies(srcs, dsts, *sems, self.kind == "swap_halves"):
                cp.start()

    def wait(self, srcs, dsts, sems):
        if self.kind in ("gather", "scatter"):
            local, out, inc = _ici_copies(srcs, dsts, *sems, self.kind == "scatter")
            for cp in inc:
                cp.wait_recv()
            for cp in out:
                cp.wait_send()
            for cp in local:
                cp.wait()
        else:
            cps = _swap_copies(srcs, dsts, *sems, self.kind == "swap_halves")
            for cp in cps:
                cp.wait_recv()
            for cp in cps:
                cp.wait_send()


def run_carry(carry, name):
    n = len(carry.srcs)

    def body(*refs):
        srcs, dsts, sems = refs[:n], refs[n:2 * n], refs[2 * n:]
        carry.start(srcs, dsts, sems)
        carry.wait(srcs, dsts, sems)

    return pl.pallas_call(body, name=name, out_shape=carry.out_shape, in_specs=[ANY] * n, out_specs=[ANY] * n,
                          scratch_shapes=carry.sems)(*carry.srcs)


def gather_split(srcs, name):
    n = len(srcs)

    def body(*refs):
        src_refs, dst_refs = refs[:n], refs[n:2 * n]
        send_sems, recv_sems, fwd_send, fwd_recv, local_sems = refs[2 * n:]
        x, y, c = _here()
        my_chip = 2 * x + y
        peers = [(1 - x, y), (x, 1 - y), (1 - x, 1 - y)]

        def half(ref, h):
            r2 = ref.shape[0] // 2
            return ref.at[pl.ds(h * r2, r2), :]

        local, out, landed, passed, arriving = [], [], [], [], []
        for a, (src, dst) in enumerate(zip(src_refs, dst_refs)):
            local.append(pltpu.make_async_copy(src, dst.at[my_chip], local_sems.at[a]))
            for j, (px, py) in enumerate(peers):
                k = 3 * a + j
                theirs = dst.at[2 * px + py]
                ici = dict(send_sem=send_sems.at[k], recv_sem=recv_sems.at[k], device_id=(px, py, c), device_id_type=MESH)
                d2d = dict(send_sem=fwd_send.at[k], recv_sem=fwd_recv.at[k], device_id=(x, y, 1 - c), device_id_type=MESH)
                out.append(pltpu.make_async_remote_copy(src_ref=half(src, c), dst_ref=half(dst.at[my_chip], c), **ici))
                landed.append(pltpu.make_async_remote_copy(src_ref=half(src, c), dst_ref=half(theirs, c), **ici))
                passed.append(pltpu.make_async_remote_copy(src_ref=half(theirs, c), dst_ref=half(theirs, c), **d2d))
                arriving.append(pltpu.make_async_remote_copy(src_ref=half(theirs, c), dst_ref=half(theirs, 1 - c), **d2d))
        for cp in local + out:
            cp.start()
        for got, fwd in zip(landed, passed):
            got.wait_recv()
            fwd.start()
        for cp in arriving:
            cp.wait_recv()
        for cp in out + passed:
            cp.wait_send()
        for cp in local:
            cp.wait()

    dma = pltpu.SemaphoreType.DMA
    return pl.pallas_call(
        body, name=name, out_shape=[jax.ShapeDtypeStruct((NCHIP,) + s.shape, s.dtype) for s in srcs],
        in_specs=[ANY] * n, out_specs=[ANY] * n,
        scratch_shapes=[dma((3 * n,)), dma((3 * n,)), dma((3 * n,)), dma((3 * n,)), dma((n,))],
    )(*srcs)


def _pcall(body, name, grid, in_specs, out_specs, out_shape, sem, args, carry=None, scratch=()):
    if carry is None:
        outs = pl.pallas_call(body, name=name, grid=grid, in_specs=in_specs, out_specs=out_specs,
                              out_shape=out_shape, scratch_shapes=list(scratch), compiler_params=_params(sem))(*args)
        return outs, []
    n_in, n_out, nc, ns = len(in_specs), len(out_specs), len(carry.srcs), len(scratch)

    def wrapped(*refs):
        ins, csrc = refs[:n_in], refs[n_in:n_in + nc]
        outs, cdst = refs[n_in + nc:n_in + nc + n_out], refs[n_in + nc + n_out:n_in + 2 * nc + n_out]
        own = refs[n_in + 2 * nc + n_out:n_in + 2 * nc + n_out + ns]
        sems = refs[n_in + 2 * nc + n_out + ns:]
        ids = [pl.program_id(a) for a in range(len(grid))]
        first = functools.reduce(jnp.logical_and, [i == 0 for i in ids])
        last = functools.reduce(jnp.logical_and, [i == g - 1 for i, g in zip(ids, grid)])

        @pl.when(first)
        def _():
            carry.start(csrc, cdst, sems)

        body(*ins, *outs, *own)

        @pl.when(last)
        def _():
            carry.wait(csrc, cdst, sems)

    res = pl.pallas_call(
        wrapped, name=name, grid=grid,
        in_specs=list(in_specs) + [ANY] * nc, out_specs=list(out_specs) + [ANY] * nc,
        out_shape=list(out_shape) + carry.out_shape,
        scratch_shapes=list(scratch) + carry.sems, compiler_params=_params(sem),
    )(*args, *carry.srcs)
    return res[:n_out], res[n_out:]


def ffn_fwd(x, vec, w1p, w2, gs, name, carry=None, target=None):
    S = x.shape[0]
    tm = _row_tile(S, 512)

    def body(x_ref, *refs):
        if target is None:
            vec_ref, w1_ref, w2_ref, xn_ref, a_ref, f_ref = refs
        else:
            t_ref, vec_ref, w1_ref, w2_ref, xn_ref, a_ref, f_ref, l_ref = refs
        xt = x_ref[...]
        xhat, _, gain, _, _, sh, gt = _ada(xt, vec_ref)
        h = (xhat * gain + sh).astype(MXU_DTYPE)
        f = jnp.zeros((tm, D), F32)
        for hf in range(2):
            g = jnp.dot(h, w1_ref[hf], preferred_element_type=F32)
            up = jnp.dot(h, w1_ref[2 + hf], preferred_element_type=F32)
            a_ref[:, hf * HALF:(hf + 1) * HALF] = g.astype(a_ref.dtype)
            a_ref[:, DFF + hf * HALF:DFF + (hf + 1) * HALF] = up.astype(a_ref.dtype)
            act = (g * jax.nn.sigmoid(g) * up).astype(MXU_DTYPE)
            f = f + jnp.dot(act, w2_ref[hf * HALF:(hf + 1) * HALF, :], preferred_element_type=F32)
        f_ref[...] = f.astype(f_ref.dtype)
        xn = xt + (gs * gt) * f
        if target is None:
            xn_ref[...] = xn
        else:
            diff = xn - t_ref[...]
            xn_ref[...] = diff * (1.0 / D)
            part = jnp.sum(jnp.sum(diff * diff, axis=0, keepdims=True), axis=1, keepdims=True) * (0.5 / D)

            @pl.when(pl.program_id(0) == 0)
            def _():
                l_ref[...] = jnp.zeros_like(l_ref)
            l_ref[...] += jnp.broadcast_to(part, l_ref.shape)

    tile = pl.BlockSpec((tm, D), lambda i: (i, 0))
    last = target is not None
    return _pcall(
        body, name, (S // tm,),
        [tile] * (2 if last else 1) + [
            pl.BlockSpec((SUBLANES, D), lambda i: (0, 0)),
            pl.BlockSpec((NCHIP, D, HALF), lambda i: (0, 0, 0), pipeline_mode=pl.Buffered(1)),
            pl.BlockSpec((DFF, D), lambda i: (0, 0), pipeline_mode=pl.Buffered(1))],
        [tile, pl.BlockSpec((tm, 2 * DFF), lambda i: (i, 0)), tile]
        + ([pl.BlockSpec((SUBLANES, LANES), lambda i: (0, 0))] if last else []),
        [jax.ShapeDtypeStruct((S, D), F32),
         jax.ShapeDtypeStruct((S, 2 * DFF), ACT_DTYPE),
         jax.ShapeDtypeStruct((S, D), ACT_DTYPE)]
        + ([jax.ShapeDtypeStruct((SUBLANES, LANES), F32)] if last else []),
        ("arbitrary",), (x, target, vec, w1p, w2) if last else (x, vec, w1p, w2), carry)


def ffn_bwd(dxo, x, a, f, vec, w1p, w2, gs, name, carry=None):
    S = x.shape[0]
    tm = _row_tile(S, 256)

    def body(dxo_ref, x_ref, a_ref, f_ref, vec_ref, w1_ref, w2_ref,
             dxi_ref, hb_ref, dfb_ref, act_ref, da_ref, sums_ref):
        xt = x_ref[...]
        dxo = dxo_ref[...]
        xhat, r, gain, ng, sc, sh, gt = _ada(xt, vec_ref)
        hb_ref[...] = (xhat * gain + sh).astype(hb_ref.dtype)
        dgate = gs * jnp.sum(dxo * f_ref[...].astype(F32), axis=0, keepdims=True)
        df = ((gs * gt) * dxo).astype(MXU_DTYPE)
        dfb_ref[...] = df
        dh = jnp.zeros((tm, D), F32)
        for hf in range(2):
            lo, hi = hf * HALF, (hf + 1) * HALF
            dact = lax.dot_general(df, w2_ref[lo:hi, :], NT_DIMS, preferred_element_type=F32)
            g = a_ref[:, lo:hi].astype(F32)
            up = a_ref[:, DFF + lo:DFF + hi].astype(F32)
            sg = jax.nn.sigmoid(g)
            si = g * sg
            act_ref[:, lo:hi] = (si * up).astype(act_ref.dtype)
            dg = (dact * up * (sg * (1.0 + g * (1.0 - sg)))).astype(MXU_DTYPE)
            dup = (dact * si).astype(MXU_DTYPE)
            da_ref[:, lo:hi] = dg
            da_ref[:, DFF + lo:DFF + hi] = dup
            dh = dh + lax.dot_general(dg, w1_ref[hf], NT_DIMS, preferred_element_type=F32)
            dh = dh + lax.dot_general(dup, w1_ref[2 + hf], NT_DIMS, preferred_element_type=F32)
        dx, dshift, dscale, dng = _ada_bwd(dh, xhat, r, gain, ng, sc)
        dxi_ref[...] = dxo + dx
        _acc_rows(sums_ref, pl.program_id(0) == 0, (dshift, dscale, dng, dgate))

    return _pcall(
        body, name, (S // tm,),
        [pl.BlockSpec((tm, D), lambda i: (i, 0)),
         pl.BlockSpec((tm, D), lambda i: (i, 0)),
         pl.BlockSpec((tm, 2 * DFF), lambda i: (i, 0)),
         pl.BlockSpec((tm, D), lambda i: (i, 0)),
         pl.BlockSpec((SUBLANES, D), lambda i: (0, 0)),
         pl.BlockSpec((NCHIP, D, HALF), lambda i: (0, 0, 0), pipeline_mode=pl.Buffered(1)),
         pl.BlockSpec((DFF, D), lambda i: (0, 0), pipeline_mode=pl.Buffered(1))],
        [pl.BlockSpec((tm, D), lambda i: (i, 0)),
         pl.BlockSpec((tm, D), lambda i: (i, 0)),
         pl.BlockSpec((tm, D), lambda i: (i, 0)),
         pl.BlockSpec((tm, DFF), lambda i: (i, 0)),
         pl.BlockSpec((tm, 2 * DFF), lambda i: (i, 0)),
         pl.BlockSpec((SUBLANES, D), lambda i: (0, 0))],
        [jax.ShapeDtypeStruct((S, D), F32),
         jax.ShapeDtypeStruct((S, D), MXU_DTYPE),
         jax.ShapeDtypeStruct((S, D), MXU_DTYPE),
         jax.ShapeDtypeStruct((S, DFF), MXU_DTYPE),
         jax.ShapeDtypeStruct((S, 2 * DFF), MXU_DTYPE),
         jax.ShapeDtypeStruct((SUBLANES, D), F32)],
        ("arbitrary",), (dxo, x, a, f, vec, w1p, w2), carry)


def wgrad(a, b, kt, nt, name, carry=None):
    T, K = a.shape
    N = b.shape[1]
    pk, pn = K // kt, N // nt
    assert pk == 1 or pn == 1
    tt = _row_tile(T, 2048)
    steps = T // tt

    def body(a_ref, b_ref, o_ref):
        @pl.when(pl.program_id(1) == 0)
        def _():
            o_ref[...] = jnp.zeros_like(o_ref)
        o_ref[...] += lax.dot_general(a_ref[...], b_ref[...], TN_DIMS, preferred_element_type=F32)

    a_map = (lambda p, t: (t, p)) if pk > 1 else (lambda p, t: (t, 0))
    b_map = (lambda p, t: (t, p)) if pn > 1 else (lambda p, t: (t, 0))
    (out,), got = _pcall(
        body, name, (pk * pn, steps),
        [pl.BlockSpec((tt, kt), a_map), pl.BlockSpec((tt, nt), b_map)],
        [pl.BlockSpec((None, kt, nt), lambda p, t: (p, 0, 0))],
        [jax.ShapeDtypeStruct((pk * pn, kt, nt), F32)], ("arbitrary", "arbitrary"), (a, b), carry)
    return out, got


def _head_masks(rows):
    lane = lax.broadcasted_iota(jnp.int32, (rows, LANES), 1)
    return lane < HD


def _pair_stat(x, m_a):
    s_a = jnp.sum(jnp.where(m_a, x, 0.0), axis=1, keepdims=True)
    s_b = jnp.sum(jnp.where(m_a, 0.0, x), axis=1, keepdims=True)
    return s_a, s_b


def mixer_in(x, vec, winp, gvec, name):
    S = x.shape[0]
    tm = _row_tile(S, 512)
    pc = INC // NCHIP

    def body(x_ref, vec_ref, w_ref, g_ref, proj_ref, hb_ref, qn_ref, kn_ref, v_ref, qkv_ref):
        xt = x_ref[...]
        xhat, _, gain, _, _, sh, _ = _ada(xt, vec_ref)
        h = (xhat * gain + sh).astype(MXU_DTYPE)
        hb_ref[...] = h
        for j in range(NCHIP):
            piece = jnp.dot(h, w_ref[j], preferred_element_type=F32)
            proj_ref[:, j * pc:(j + 1) * pc] = piece.astype(proj_ref.dtype)
            if (j + 1) * pc <= 3 * AW:
                qkv_ref[:, j * pc:(j + 1) * pc] = piece
        m_a = _head_masks(tm)
        for which, dst in ((0, qn_ref), (1, kn_ref)):
            for p in range(AW // LANES):
                lo = which * AW + p * LANES
                xp = qkv_ref[:, lo:lo + LANES]
                s_a, s_b = _pair_stat(xp * xp, m_a)
                rr = jnp.where(m_a, lax.rsqrt(s_a * (1.0 / HD) + EPS), lax.rsqrt(s_b * (1.0 / HD) + EPS))
                gp = g_ref[which:which + 1, p * LANES:(p + 1) * LANES]
                dst[:, p * LANES:(p + 1) * LANES] = (xp * rr * gp).astype(dst.dtype)
        v_ref[...] = qkv_ref[:, 2 * AW:3 * AW]

    assert 2 * pc == 3 * AW
    return pl.pallas_call(
        body, name=name, grid=(S // tm,), scratch_shapes=[pltpu.VMEM((tm, 3 * AW), F32)],
        in_specs=[pl.BlockSpec((tm, D), lambda i: (i, 0)),
                  pl.BlockSpec((SUBLANES, D), lambda i: (0, 0)),
                  pl.BlockSpec((NCHIP, D, pc), lambda i: (0, 0, 0), pipeline_mode=pl.Buffered(1)),
                  pl.BlockSpec((SUBLANES, AW), lambda i: (0, 0))],
        out_specs=[pl.BlockSpec((tm, INC), lambda i: (i, 0)),
                   pl.BlockSpec((tm, D), lambda i: (i, 0)),
                   pl.BlockSpec((tm, AW), lambda i: (i, 0)),
                   pl.BlockSpec((tm, AW), lambda i: (i, 0)),
                   pl.BlockSpec((tm, AW), lambda i: (i, 0))],
        out_shape=[jax.ShapeDtypeStruct((S, INC), ACT_DTYPE),
                   jax.ShapeDtypeStruct((S, D), MXU_DTYPE),
                   jax.ShapeDtypeStruct((S, AW), F32),
                   jax.ShapeDtypeStruct((S, AW), F32),
                   jax.ShapeDtypeStruct((S, AW), F32)],
        compiler_params=_params(("arbitrary",)),
    )(x, vec, winp, gvec)


def _band_masks(ncol):
    row = lax.broadcasted_iota(jnp.int32, (2 * QBLK, ncol), 0) & (QBLK - 1)
    col = lax.broadcasted_iota(jnp.int32, (2 * QBLK, ncol), 1)
    return row, col


def _stack_heads(t, m_a):
    zero = jnp.zeros_like(t)
    return jnp.concatenate([jnp.where(m_a, t, zero), jnp.where(m_a, zero, t)], axis=0)


class _AttnLayout:
    def __init__(self, d, S):
        self.d, self.S = d, S
        self.qb = max(1, min(ATTN_QBLOCKS, ATTN_CHUNK_ROWS // (QBLK * d)))
        self.nres = d
        self.nchunk = S // (self.qb * QBLK * d)
        self.grid = (AW // LANES, self.nchunk)
        self.unroll = max(1, min(d, ATTN_INTERLEAVE // self.qb))

    def _spec(self, blocks, row_of):
        return pl.BlockSpec((blocks * QBLK * self.d, LANES), lambda hp, j: (row_of(j), hp))

    def cur(self, chunk_of):
        return self._spec(self.qb, chunk_of)

    def prev(self, chunk_of):
        return self._spec(1, lambda j: jnp.maximum(chunk_of(j) * self.qb - 1, 0))

    def idx(self, b, r):
        if self.d == 1:
            return (pl.ds(b * QBLK, QBLK), slice(None))
        return (pl.ds(b * QBLK * self.d + r, QBLK, stride=self.d), slice(None))

    def per_residue(self, fn):
        if self.nres == 1:
            fn(0)
        else:
            def step(it, carry):
                for k in range(self.unroll):
                    fn(it * self.unroll + k)
                return carry
            lax.fori_loop(0, self.nres // self.unroll, step, 0)


def attn_fwd(qn, kn, v, d, name, carry=None):
    S = qn.shape[0]
    lay = _AttnLayout(d, S)
    qb = lay.qb

    def body(q_ref, kc_ref, kp_ref, vc_ref, vp_ref, o_ref, lse_ref):
        i = pl.program_id(1)
        m_a = _head_masks(QBLK)
        row, col = _band_masks(2 * QBLK)
        dist = row + QBLK - col
        band = (dist >= 0) & (dist <= QBLK)
        first = band & ((i > 0) | (col >= QBLK))

        def residue(r):
            kt = [kp_ref[lay.idx(0, r)].astype(MXU_DTYPE)]
            vt = [vp_ref[lay.idx(0, r)].astype(MXU_DTYPE)]
            for b in range(qb):
                kt.append(kc_ref[lay.idx(b, r)].astype(MXU_DTYPE))
                vt.append(vc_ref[lay.idx(b, r)].astype(MXU_DTYPE))
            for b in range(qb):
                rows = lay.idx(b, r)
                q = (q_ref[rows] * (HD ** -0.5)).astype(MXU_DTYPE)
                kcat = jnp.concatenate([kt[b], kt[b + 1]], axis=0)
                vcat = jnp.concatenate([vt[b], vt[b + 1]], axis=0)
                mask = first if b == 0 else band
                s = lax.dot_general(_stack_heads(q, m_a), kcat, NT_DIMS, preferred_element_type=F32)
                s = jnp.where(mask, s, NEG)
                m = jnp.max(s, axis=1, keepdims=True)
                p = jnp.exp(s - m)
                l = jnp.sum(p, axis=1, keepdims=True)
                o = jnp.dot(p.astype(MXU_DTYPE), vcat, preferred_element_type=F32) / l
                lse = jnp.broadcast_to(m + jnp.log(l), (2 * QBLK, LANES))
                o_ref[rows] = jnp.where(m_a, o[:QBLK], o[QBLK:])
                lse_ref[rows] = jnp.where(m_a, lse[:QBLK], lse[QBLK:])

        lay.per_residue(residue)

    cur, prev = lay.cur(lambda j: j), lay.prev(lambda j: j)
    return _pcall(body, name, lay.grid, [cur, cur, prev, cur, prev], [cur, cur],
                  [jax.ShapeDtypeStruct((S, AW), F32)] * 2, ("arbitrary", "arbitrary"), (qn, kn, kn, v, v), carry)


def _both_heads(t, m_a):
    other = pltpu.roll(t, HD, 1)
    return jnp.concatenate([jnp.where(m_a, t, other), jnp.where(m_a, other, t)], axis=0)


def attn_bwd(qn, kn, v, dycat, lse, delta, d, name, carry=None):
    S = qn.shape[0]
    lay = _AttnLayout(d, S)
    qb, nchunk = lay.qb, lay.nchunk

    def body(q_ref, kc_ref, kp_ref, vc_ref, vp_ref, do_ref, lse_ref, dl_ref,
             dq_ref, dk_ref, dv_ref, ck_ref, cv_ref):
        j = pl.program_id(1)
        i = nchunk - 1 - j
        m_a = _head_masks(QBLK)
        row, col = _band_masks(2 * QBLK)
        dist = row + QBLK - col
        band = (dist >= 0) & (dist <= QBLK)
        first = band & ((i > 0) | (col >= QBLK))

        def residue(r):
            def tiles(ref, cast):
                out = [ref[lay.idx(b, r)] for b in range(qb)]
                return [t.astype(MXU_DTYPE) for t in out] if cast else out

            def ktiles(cur_ref, prev_ref):
                return [prev_ref[lay.idx(0, r)].astype(MXU_DTYPE)] + tiles(cur_ref, True)

            qt = [(t * (HD ** -0.5)).astype(MXU_DTYPE) for t in tiles(q_ref, False)]
            dot_ = tiles(do_ref, True)
            lse_t = tiles(lse_ref, False)
            dl_t = tiles(dl_ref, False)
            kt = ktiles(kc_ref, kp_ref)
            vt = ktiles(vc_ref, vp_ref)
            dk_acc = [jnp.zeros((QBLK, LANES), F32) for _ in range(qb)]
            dv_acc = [jnp.zeros((QBLK, LANES), F32) for _ in range(qb)]
            crow = pl.ds(0, QBLK) if lay.nres == 1 else pl.ds(pl.multiple_of(r * QBLK, QBLK), QBLK)
            dk_acc[qb - 1] = jnp.where(j > 0, ck_ref[crow, :], 0.0)
            dv_acc[qb - 1] = jnp.where(j > 0, cv_ref[crow, :], 0.0)
            for x in range(qb):
                kcat = jnp.concatenate([kt[x], kt[x + 1]], axis=0)
                vcat = jnp.concatenate([vt[x], vt[x + 1]], axis=0)
                q2 = _stack_heads(qt[x], m_a)
                do2 = _stack_heads(dot_[x], m_a)
                lse2 = _both_heads(lse_t[x], m_a)
                dl2 = _both_heads(dl_t[x], m_a)
                lse2 = jnp.concatenate([lse2, lse2], axis=1)
                dl2 = jnp.concatenate([dl2, dl2], axis=1)
                s = lax.dot_general(q2, kcat, NT_DIMS, preferred_element_type=F32)
                p = jnp.exp(jnp.where(first if x == 0 else band, s, NEG) - lse2)
                dp = lax.dot_general(do2, vcat, NT_DIMS, preferred_element_type=F32)
                ds = p * (dp - dl2)
                dq = jnp.dot(ds.astype(MXU_DTYPE), kcat, preferred_element_type=F32)
                dq_ref[lay.idx(x, r)] = jnp.where(m_a, dq[:QBLK], dq[QBLK:]) * (HD ** -0.5)
                dk = jnp.dot(ds.T.astype(MXU_DTYPE), q2, preferred_element_type=F32)
                dv = jnp.dot(p.T.astype(MXU_DTYPE), do2, preferred_element_type=F32)
                if x == 0:
                    ck_ref[crow, :] = dk[:QBLK]
                    cv_ref[crow, :] = dv[:QBLK]
                else:
                    dk_acc[x - 1] = dk_acc[x - 1] + dk[:QBLK]
                    dv_acc[x - 1] = dv_acc[x - 1] + dv[:QBLK]
                dk_acc[x] = dk_acc[x] + dk[QBLK:]
                dv_acc[x] = dv_acc[x] + dv[QBLK:]
            for kb in range(qb):
                dk_ref[lay.idx(kb, r)] = dk_acc[kb]
                dv_ref[lay.idx(kb, r)] = dv_acc[kb]

        lay.per_residue(residue)

    cur, prev = lay.cur(lambda j: nchunk - 1 - j), lay.prev(lambda j: nchunk - 1 - j)
    carried = pltpu.VMEM((lay.nres * QBLK, LANES), F32)
    return _pcall(
        body, name, lay.grid, [cur, cur, prev, cur, prev, cur, cur, cur], [cur, cur, cur],
        [jax.ShapeDtypeStruct((S, AW), F32)] * 3, ("arbitrary", "arbitrary"),
        (qn, kn, kn, v, v, dycat, lse, delta), carry, scratch=[carried, carried])


def _shift_down(x, halo_prev, k, row):
    tm = x.shape[0]
    tail = jnp.concatenate([pltpu.roll(halo_prev, k, 0), jnp.zeros((tm - SUBLANES, x.shape[1]), x.dtype)], axis=0)
    return jnp.where(row < k, tail, pltpu.roll(x, k, 0))


def _shift_up(x, halo_next, k, row):
    tm = x.shape[0]
    head = jnp.concatenate([jnp.zeros((tm - SUBLANES, x.shape[1]), x.dtype), pltpu.roll(halo_next, SUBLANES - k, 0)], axis=0)
    return jnp.where(row >= tm - k, head, pltpu.roll(x, tm - k, 0))


def _conv_fwd(cu, halo_cu, cw_ref, row):
    u1 = _shift_down(cu, halo_cu, 1, row)
    u2 = _shift_down(cu, halo_cu, 2, row)
    cv = cw_ref[0:1, :] * u2 + cw_ref[1:2, :] * u1 + cw_ref[2:3, :] * cu + cw_ref[3:4, :]
    return cv, u1, u2


def mixer_out(os_, lses, proj, cw, x, vec, wout, name, carry=None):
    S = proj.shape[0]
    tm = _row_tile(S, 512)
    hb = tm // HALO_ROWS

    def body(o1, o2, o3, l1, l2, l3, pc_ref, ph_ref, cw_ref, x_ref, vec_ref, w_ref, ycat_ref, lse_ref, xn_ref, y_ref):
        i = pl.program_id(0)
        for p in range(AW // LANES):
            cs = slice(p * LANES, (p + 1) * LANES)
            ls = [l[:, cs] for l in (l1, l2, l3)]
            mx = jnp.maximum(jnp.maximum(ls[0], ls[1]), ls[2])
            t = mx + jnp.log(jnp.exp(ls[0] - mx) + jnp.exp(ls[1] - mx) + jnp.exp(ls[2] - mx))
            lse_ref[:, cs] = t
            acc = jnp.zeros((tm, LANES), F32)
            for l, o in zip(ls, (o1, o2, o3)):
                acc = acc + jnp.exp(l - t) * o[:, cs]
            ycat_ref[:, cs] = acc.astype(ycat_ref.dtype)
        row = lax.broadcasted_iota(jnp.int32, (tm, CW), 0)
        gb, gc, u = (pc_ref[:, k * CW:(k + 1) * CW].astype(F32) for k in range(3))
        ph = ph_ref[...].astype(F32)[HALO_ROWS - SUBLANES:]
        halo_cu = jnp.where(i > 0, ph[:, CW:2 * CW] * ph[:, 2 * CW:3 * CW], 0.0)
        cv, _, _ = _conv_fwd(gc * u, halo_cu, cw_ref, row)
        ycat_ref[:, AW:AW + CW] = (gb * cv).astype(ycat_ref.dtype)
        y = jnp.dot(ycat_ref[...].astype(MXU_DTYPE), w_ref[...], preferred_element_type=F32)
        xn_ref[...] = x_ref[...] + vec_ref[3:4, :] * y
        y_ref[...] = y.astype(y_ref.dtype)

    ot = pl.BlockSpec((tm, AW), lambda i: (i, 0))
    t = pl.BlockSpec((tm, D), lambda i: (i, 0))
    return _pcall(
        body, name, (S // tm,),
        [ot] * 6 + [pl.BlockSpec((tm, 3 * CW), lambda i: (i, 1)),
                    pl.BlockSpec((HALO_ROWS, 3 * CW), lambda i: (jnp.maximum(i * hb - 1, 0), 1)),
                    pl.BlockSpec((SUBLANES, CW), lambda i: (0, 0)),
                    t, pl.BlockSpec((SUBLANES, D), lambda i: (0, 0)), pl.BlockSpec((D, D), lambda i: (0, 0))],
        [t, ot, t, t],
        [jax.ShapeDtypeStruct((S, D), ACT_DTYPE), jax.ShapeDtypeStruct((S, AW), F32),
         jax.ShapeDtypeStruct((S, D), F32), jax.ShapeDtypeStruct((S, D), ACT_DTYPE)],
        ("arbitrary",), (*os_, *lses, proj, proj, cw, x, vec, wout), carry)


def out_proj_bwd(dxo, y, ycat, vec, wout, name, carry=None):
    S = dxo.shape[0]
    tm = _row_tile(S, 512)

    def body(dxo_ref, y_ref, yc_ref, vec_ref, w_ref, dyb_ref, dyc_ref, dl_ref, sums_ref):
        dxo = dxo_ref[...]
        dgate = jnp.sum(dxo * y_ref[...].astype(F32), axis=0, keepdims=True)
        dy = (vec_ref[3:4, :] * dxo).astype(MXU_DTYPE)
        dyb_ref[...] = dy
        dyc_ref[...] = lax.dot_general(dy, w_ref[...], NT_DIMS, preferred_element_type=F32)
        m_a = _head_masks(tm)
        for p in range(AW // LANES):
            cs = slice(p * LANES, (p + 1) * LANES)
            s_a, s_b = _pair_stat(dyc_ref[:, cs] * yc_ref[:, cs].astype(F32), m_a)
            dl_ref[:, cs] = jnp.where(m_a, s_a, s_b)
        _acc_rows(sums_ref, pl.program_id(0) == 0, (dgate,))

    t = pl.BlockSpec((tm, D), lambda i: (i, 0))
    at = pl.BlockSpec((tm, AW), lambda i: (i, 0))
    return _pcall(
        body, name, (S // tm,),
        [t, t, t, pl.BlockSpec((SUBLANES, D), lambda i: (0, 0)), pl.BlockSpec((D, D), lambda i: (0, 0))],
        [t, t, at, pl.BlockSpec((SUBLANES, D), lambda i: (0, 0))],
        [jax.ShapeDtypeStruct((S, D), MXU_DTYPE), jax.ShapeDtypeStruct((S, D), F32),
         jax.ShapeDtypeStruct((S, AW), F32), jax.ShapeDtypeStruct((SUBLANES, D), F32)],
        ("arbitrary",), (dxo, y, ycat, vec, wout), carry)


def mixer_mid_bwd(dqs, dks, dvs, proj, dycat, gvec, cw, name, carry=None):
    S = proj.shape[0]
    tm = _row_tile(S, 512)
    hb = tm // SUBLANES
    hp = tm // HALO_ROWS
    nsl = S // SUBLANES
    ntile = S // tm

    def body(dq1, dq2, dq3, dk1, dk2, dk3, dv1, dv2, dv3, pr_ref, pp_ref, pn_ref, dyc_ref, dyn_ref,
             g_ref, cw_ref, dp_ref, sums_ref):
        i = pl.program_id(0)
        m_a = _head_masks(tm)
        gsum = []
        for which, parts in ((0, (dq1, dq2, dq3)), (1, (dk1, dk2, dk3))):
            acc_g = []
            for p in range(AW // LANES):
                lo = which * AW + p * LANES
                cs = slice(p * LANES, (p + 1) * LANES)
                xp = pr_ref[:, lo:lo + LANES].astype(F32)
                s_a, s_b = _pair_stat(xp * xp, m_a)
                rr = jnp.where(m_a, lax.rsqrt(s_a * (1.0 / HD) + EPS), lax.rsqrt(s_b * (1.0 / HD) + EPS))
                xh = xp * rr
                dn = parts[0][:, cs] + parts[1][:, cs] + parts[2][:, cs]
                acc_g.append(jnp.sum(dn * xh, axis=0, keepdims=True))
                t = dn * g_ref[which:which + 1, cs]
                t_a, t_b = _pair_stat(t * xh, m_a)
                mean = jnp.where(m_a, t_a, t_b) * (1.0 / HD)
                dp_ref[:, lo:lo + LANES] = (rr * (t - xh * mean)).astype(dp_ref.dtype)
            gsum.append(jnp.concatenate(acc_g, axis=1))
        dp_ref[:, 2 * AW:3 * AW] = (dv1[...] + dv2[...] + dv3[...]).astype(dp_ref.dtype)
        row = lax.broadcasted_iota(jnp.int32, (tm, CW), 0)
        base = 3 * AW
        gb, gc, u = (pr_ref[:, base + k * CW:base + (k + 1) * CW].astype(F32) for k in range(3))
        cu = gc * u
        pp = pp_ref[...].astype(F32)[HALO_ROWS - SUBLANES:]
        halo_cu = jnp.where(i > 0, pp[:, CW:2 * CW] * pp[:, 2 * CW:3 * CW], 0.0)
        cv, u1, u2 = _conv_fwd(cu, halo_cu, cw_ref, row)
        dyc = dyc_ref[...]
        dp_ref[:, base:base + CW] = (dyc * cv).astype(dp_ref.dtype)
        dcv = dyc * gb
        gb_next = pn_ref[:, 0:CW].astype(F32)[:SUBLANES]
        halo_dcv = jnp.where(i < ntile - 1, dyn_ref[...] * gb_next, 0.0)
        d1 = _shift_up(dcv, halo_dcv, 1, row)
        d2 = _shift_up(dcv, halo_dcv, 2, row)
        dcu = cw_ref[2:3, :] * dcv + cw_ref[1:2, :] * d1 + cw_ref[0:1, :] * d2
        dp_ref[:, base + CW:base + 2 * CW] = (dcu * u).astype(dp_ref.dtype)
        dp_ref[:, base + 2 * CW:base + 3 * CW] = (dcu * gc).astype(dp_ref.dtype)
        rows = (gsum[0], gsum[1],
                jnp.sum(dcv * u2, axis=0, keepdims=True), jnp.sum(dcv * u1, axis=0, keepdims=True),
                jnp.sum(dcv * cu, axis=0, keepdims=True), jnp.sum(dcv, axis=0, keepdims=True))
        _acc_rows(sums_ref, i == 0, rows)

    at = pl.BlockSpec((tm, AW), lambda i: (i, 0))
    return _pcall(
        body, name, (ntile,),
        [at] * 9 + [
            pl.BlockSpec((tm, INC), lambda i: (i, 0)),
            pl.BlockSpec((HALO_ROWS, 3 * CW), lambda i: (jnp.maximum(i * hp - 1, 0), 1)),
            pl.BlockSpec((HALO_ROWS, 3 * CW), lambda i: (jnp.minimum((i + 1) * hp, S // HALO_ROWS - 1), 1)),
            pl.BlockSpec((tm, CW), lambda i: (i, 1)),
            pl.BlockSpec((SUBLANES, CW), lambda i: (jnp.minimum((i + 1) * hb, nsl - 1), 1)),
            pl.BlockSpec((SUBLANES, AW), lambda i: (0, 0)),
            pl.BlockSpec((SUBLANES, CW), lambda i: (0, 0))],
        [pl.BlockSpec((tm, INC), lambda i: (i, 0)), pl.BlockSpec((SUBLANES, AW), lambda i: (0, 0))],
        [jax.ShapeDtypeStruct((S, INC), MXU_DTYPE), jax.ShapeDtypeStruct((SUBLANES, AW), F32)],
        ("arbitrary",), (*dqs, *dks, *dvs, proj, proj, proj, dycat, dycat, gvec, cw), carry)


def mixer_in_bwd(dxo, x, dproj, vec, winp, name, carry=None):
    S = x.shape[0]
    tm = _row_tile(S, 512)
    pc = INC // NCHIP

    def body(dxo_ref, x_ref, dp_ref, vec_ref, w_ref, dxi_ref, sums_ref):
        xhat, r, gain, ng, sc, _, _ = _ada(x_ref[...], vec_ref)
        dh = jnp.zeros((tm, D), F32)
        for j in range(NCHIP):
            dh = dh + lax.dot_general(dp_ref[:, j * pc:(j + 1) * pc], w_ref[j], NT_DIMS, preferred_element_type=F32)
        dx, dshift, dscale, dng = _ada_bwd(dh, xhat, r, gain, ng, sc)
        dxi_ref[...] = dxo_ref[...] + dx
        _acc_rows(sums_ref, pl.program_id(0) == 0, (dshift, dscale, dng))

    t = pl.BlockSpec((tm, D), lambda i: (i, 0))
    return _pcall(
        body, name, (S // tm,),
        [t, t, pl.BlockSpec((tm, INC), lambda i: (i, 0)),
         pl.BlockSpec((SUBLANES, D), lambda i: (0, 0)),
         pl.BlockSpec((NCHIP, D, pc), lambda i: (0, 0, 0), pipeline_mode=pl.Buffered(1))],
        [t, pl.BlockSpec((SUBLANES, D), lambda i: (0, 0))],
        [jax.ShapeDtypeStruct((S, D), F32), jax.ShapeDtypeStruct((SUBLANES, D), F32)],
        ("arbitrary",), (dxo, x, dproj, vec, winp), carry)


def _vec(mod_l, ng_l, i):
    m = mod_l.reshape(3, 3, D)
    rows = jnp.stack([ng_l[i], m[i, 1], m[i, 0], m[i, 2]])
    return jnp.concatenate([rows, jnp.zeros((SUBLANES - 4, D), F32)], axis=0)


def local_step(x, target, mods, ngs, gvecs, cws, shards, w_first, cflag):
    saved = []
    weights = [dict(w1=[None, None], w2=[None, None]) for _ in range(2)]
    weights[0]["w1"][0], weights[0]["w2"][0] = w_first[0], w_first[1].reshape(DFF, D)
    h = x
    for l in range(2):
        w, sh = weights[l], shards[l]
        nxt = shards[l + 1] if l == 0 else None
        vecs = [_vec(mods[l], ngs[l], i) for i in range(3)]
        x0 = h
        (x1, a0, f0), (win, wout, w2b) = ffn_fwd(x0, vecs[0], w["w1"][0], w["w2"][0], 0.5, f"ffn_fwd_l{l}a",
                                                 carry=Carry("gather", [sh["win"], sh["wout"], sh["w2"][1]]))
        w["win"], w["wout"], w["w2"][1] = win, wout.reshape(D, D), w2b.reshape(DFF, D)
        proj, h1b, qn, kn, v = mixer_in(x1, vecs[1], w["win"], gvecs[l], f"mixer_in_l{l}")
        os_, lses, w1b = [], [], {}
        for d in DILATIONS:
            rows = {1: slice(0, D // 2), 16: slice(D // 2, D)}.get(d)
            carry = Carry("gather", [sh["w1"][1][rows]]) if rows else None
            (o, lse_d), w1b[d] = attn_fwd(qn, kn, v, d, f"attn_fwd_l{l}_d{d}", carry=carry)
            os_.append(o)
            lses.append(lse_d)
        w["w1"][1] = jnp.concatenate([w1b[1][0], w1b[16][0]], axis=1)
        (ycat, lse, x2, y), got = mixer_out(os_, lses, proj, cws[l], x1, vecs[1], w["wout"], f"mixer_out_l{l}",
                                            carry=Carry("gather", [nxt["w2"][0]]) if nxt else None)
        if nxt:
            weights[1]["w2"][0] = got[0].reshape(DFF, D)
        if nxt:
            (h, a2, f2), got = ffn_fwd(x2, vecs[2], w["w1"][1], w["w2"][1], 0.5, f"ffn_fwd_l{l}b",
                                       carry=Carry("gather", [nxt["w1"][0]]))
            weights[1]["w1"][0] = got[0]
        else:
            (dx, a2, f2, loss_blk), _ = ffn_fwd(x2, vecs[2], w["w1"][1], w["w2"][1], 0.5, f"ffn_fwd_l{l}b",
                                                target=target)
        saved.append(dict(vecs=vecs, x0=x0, a0=a0, f0=f0, x1=x1, proj=proj, h1b=h1b, qn=qn, kn=kn, v=v,
                          ycat=ycat, lse=lse, y=y, x2=x2, a2=a2, f2=f2))
    sums, totals, g_prev = [None, None], [None, None], None
    w2r = DFF // NCHIP
    for l in (1, 0):
        w, s = weights[l], saved[l]
        vecs = s["vecs"]
        ride = g_prev is not None
        own = l == 0
        mine, other = [None] * 6, [None] * 6

        def half_sum(group, recv, k0):
            return [add_half(g, r, cflag, f"add_sibling_l{l}_{k0 + j}") for j, (g, r) in enumerate(zip(group, recv))]

        def chip_sum(landed, k0):
            return [sum_chips(t, f"sum_chips_l{l}_{k0 + j}") for j, t in enumerate(landed)]

        (dx, hb, dfb, act, da, sums2), got = ffn_bwd(
            dx, s["x2"], s["a2"], s["f2"], vecs[2], w["w1"][1], w["w2"][1], 0.5, f"ffn_bwd_l{l}b",
            carry=Carry("swap_halves", g_prev) if ride else None)
        dw1b, _ = wgrad(hb, da, D, HALF, f"wgrad_w1_l{l}b")
        dw2b, _ = wgrad(act, dfb, HALF, D, f"wgrad_w2_l{l}b")
        if ride:
            wire = [add_half(g_prev[k], got[k], cflag, f"add_sibling_l{l + 1}_{k}") for k in range(6)]
        g_ffn_b = [dw1b, dw2b.reshape(NCHIP, w2r, D)]
        (dyb, dycat, delta, sums_o), got = out_proj_bwd(
            dx, s["y"], s["ycat"], vecs[1], w["wout"], f"out_proj_bwd_l{l}",
            carry=Carry("swap_halves", g_ffn_b) if own else None)
        dwout, _ = wgrad(s["ycat"].astype(MXU_DTYPE), dyb, D // 2, D, f"wgrad_wout_l{l}")
        if own:
            wire_ffn_b = half_sum(g_ffn_b, got, 4)
        dqs, dks, dvs, landed = [], [], [], {}
        for d in DILATIONS:
            carry = None
            if ride and d == 1:
                carry = Carry("scatter", wire[3:])
            if ride and d == 16:
                carry = Carry("scatter", wire[:3])
            if own and d == 4:
                carry = Carry("scatter", wire_ffn_b)
            (dq, dk, dv), landed[d] = attn_bwd(s["qn"], s["kn"], s["v"], dycat, s["lse"], delta, d,
                                               f"attn_bwd_l{l}_d{d}", carry=carry)
            dqs.append(dq)
            dks.append(dk)
            dvs.append(dv)
        if ride:
            tot = [sum_chips(t, f"sum_chips_l{l + 1}_{k}") for k, t in enumerate(list(landed[16]) + list(landed[1]))]
        if own:
            mine[4:6] = chip_sum(landed[4], 4)
        ready = (tot if ride else []) + (mine[4:6] if own else [])
        (dproj, sums_m), got = mixer_mid_bwd(dqs, dks, dvs, s["proj"], dycat, gvecs[l], cws[l], f"mixer_mid_bwd_l{l}",
                                             carry=Carry("swap", ready) if ready else None)
        if ride:
            totals[l + 1] = (tot, list(got[:6]))
        if own:
            other[4:6] = list(got[-2:])
        dwin, _ = wgrad(s["h1b"], dproj, D, INC // NCHIP, f"wgrad_win_l{l}")
        g_mixer = [dwin, dwout.reshape(NCHIP, D // NCHIP, D)]
        (dx, sums1), got = mixer_in_bwd(dx, s["x1"], dproj, vecs[1], w["win"], f"mixer_in_bwd_l{l}",
                                        carry=Carry("swap_halves", g_mixer) if own else None)
        if own:
            wire_mixer = half_sum(g_mixer, got, 2)
        (dx, hb, dfb, act, da, sums0), _ = ffn_bwd(
            dx, s["x0"], s["a0"], s["f0"], vecs[0], w["w1"][0], w["w2"][0], 0.5, f"ffn_bwd_l{l}a")
        dw1a, got = wgrad(hb, da, D, HALF, f"wgrad_w1_l{l}a", carry=Carry("scatter", wire_mixer) if own else None)
        if own:
            mine[2:4] = chip_sum(got, 2)
        dw2a, got = wgrad(act, dfb, HALF, D, f"wgrad_w2_l{l}a", carry=Carry("swap", mine[2:4]) if own else None)
        g_ffn_a = [dw1a, dw2a.reshape(NCHIP, w2r, D)]
        if own:
            other[2:4] = list(got)
            wire_ffn_a = half_sum(g_ffn_a, run_carry(Carry("swap_halves", g_ffn_a), "swap_halves_tail"), 0)
            mine[0:2] = chip_sum(run_carry(Carry("scatter", wire_ffn_a), "scatter_grads_tail"), 0)
            other[0:2] = list(run_carry(Carry("swap", mine[0:2]), "swap_totals_tail"))
            totals[l] = (mine, other)
        g_prev = g_ffn_a + g_mixer + g_ffn_b
        sums[l] = (sums0, sums1, sums_o, sums2, sums_m)
    return loss_blk, dx, totals, sums


def small_all_gather(blk, name):
    m_per, n = blk.shape

    def body(x_ref, out_ref, send_sems, recv_sems, local_sem):
        x, y, c = _here()
        me, sibling = (x, y, c), (x, y, 1 - c)
        chips = [(1 - x, y), (x, 1 - y), (1 - x, 1 - y)]

        def rows(px, py, pc):
            return out_ref.at[pl.ds((4 * px + 2 * py + pc) * m_per, m_per), :]

        def copy(k, block, to, src=None):
            return pltpu.make_async_remote_copy(
                src_ref=rows(*block) if src is None else src, dst_ref=rows(*block),
                send_sem=send_sems.at[k], recv_sem=recv_sems.at[k], device_id=to, device_id_type=MESH)

        mine = pltpu.make_async_copy(x_ref, rows(*me), local_sem)
        mine.start()
        first = [copy(0, me, sibling, src=x_ref)]
        first += [copy(1 + j, me, (*chip, c), src=x_ref) for j, chip in enumerate(chips)]
        for cp in first:
            cp.start()
        passed = [copy(4 + j, (*chip, c), sibling) for j, chip in enumerate(chips)]
        for j, chip in enumerate(chips):
            copy(1 + j, (*chip, c), me).wait_recv()
            passed[j].start()
        copy(0, sibling, me).wait_recv()
        for j, chip in enumerate(chips):
            copy(4 + j, (*chip, 1 - c), me).wait_recv()
        for cp in first + passed:
            cp.wait_send()
        mine.wait()

    return pl.pallas_call(
        body, name=name,
        out_shape=jax.ShapeDtypeStruct((NDEV * m_per, n), blk.dtype),
        in_specs=[pl.BlockSpec(memory_space=pltpu.VMEM)],
        out_specs=pl.BlockSpec(memory_space=pltpu.VMEM),
        scratch_shapes=[pltpu.SemaphoreType.DMA((7,)), pltpu.SemaphoreType.DMA((7,)), pltpu.SemaphoreType.DMA],
        compiler_params=pltpu.CompilerParams(vmem_limit_bytes=VMEM_LIMIT),
    )(blk)


EW_BLOCK_BYTES = 1 << 20


def _ew_rows(rows, cols, refs=8):
    want = max(16, EW_BLOCK_BYTES * (2 if refs <= 4 else 1) // (4 * cols))
    best = None
    for t in range(16, rows + 1, 16):
        if rows % t == 0 and t <= want:
            best = t
    return best if best is not None else rows


def add_half(g, recv, cflag, name):
    pieces, r, cols = g.shape
    r2 = r // 2
    tr = _ew_rows(r2, cols, refs=3)
    nt = r2 // tr

    def body(c_ref, g_ref, r_ref, o_ref):
        o_ref[...] = (g_ref[...] + r_ref[...]).astype(o_ref.dtype)

    half = pl.BlockSpec((None, tr, cols), lambda j, i, c_ref: (j, i, 0))
    return pl.pallas_call(
        body, name=name,
        grid_spec=pltpu.PrefetchScalarGridSpec(
            num_scalar_prefetch=1, grid=(pieces, nt),
            in_specs=[pl.BlockSpec((None, tr, cols), lambda j, i, c_ref: (j, c_ref[0] * nt + i, 0)), half],
            out_specs=half),
        out_shape=jax.ShapeDtypeStruct((pieces, r2, cols), WIRE_DTYPE),
        compiler_params=_params(("arbitrary", "arbitrary")),
    )(cflag, g, recv)


def sum_chips(recv, name):
    _, r, cols = recv.shape
    tr = _ew_rows(r, cols, refs=3)

    def body(r_ref, o_ref):
        acc = r_ref[0].astype(F32)
        for k in range(1, NCHIP):
            acc = acc + r_ref[k].astype(F32)
        o_ref[...] = acc

    return pl.pallas_call(
        body, name=name, grid=(r // tr,),
        in_specs=[pl.BlockSpec((NCHIP, tr, cols), lambda i: (0, i, 0))],
        out_specs=pl.BlockSpec((tr, cols), lambda i: (i, 0)),
        out_shape=jax.ShapeDtypeStruct((r, cols), F32),
        compiler_params=_params(("arbitrary",)),
    )(recv)


def sum_devices(rows8, name):
    def body(r_ref, o_ref):
        acc = r_ref[0:1, :]
        for k in range(1, NDEV):
            acc = acc + r_ref[k:k + 1, :]
        o_ref[...] = jnp.broadcast_to(acc, o_ref.shape)

    return pl.pallas_call(
        body, name=name, out_shape=jax.ShapeDtypeStruct(rows8.shape, F32),
        in_specs=[pl.BlockSpec(memory_space=pltpu.VMEM)], out_specs=pl.BlockSpec(memory_space=pltpu.VMEM),
        compiler_params=pltpu.CompilerParams(vmem_limit_bytes=VMEM_LIMIT),
    )(rows8)


def adamw(w, m, v, srcs, cflag, name, halves=False):
    planes, r, cols = w.shape
    rh = r // 2 if halves else r
    tr = _ew_rows(rh, cols)
    nth = rh // tr
    flat = [a for s in srcs for a in (s if halves else (s,))]
    ns = len(flat)
    per = ns // planes

    def body(c_ref, w_ref, m_ref, v_ref, *rest):
        s_refs, (g_ref, d_ref, mo_ref, vo_ref) = rest[:ns], rest[ns:]
        p, i = pl.program_id(0), pl.program_id(1)
        if halves:
            mine = jnp.logical_not(jnp.logical_xor(i >= nth, c_ref[0] == 1))
            blocks = [jnp.where(mine, s_refs[2 * k][...], s_refs[2 * k + 1][...]) for k in range(planes)]
        else:
            blocks = [s[...] for s in s_refs]
        g = blocks[0]
        for k in range(1, planes):
            g = jnp.where(p == k, blocks[k], g)
        g_ref[...] = g
        m_new = ADAM_B1 * m_ref[...] + (1.0 - ADAM_B1) * g
        v_new = ADAM_B2 * v_ref[...] + (1.0 - ADAM_B2) * (g * g)
        mo_ref[...] = m_new
        vo_ref[...] = v_new
        m_hat = m_new / (1.0 - ADAM_B1 ** ADAM_STEP)
        v_hat = v_new / (1.0 - ADAM_B2 ** ADAM_STEP)
        d_ref[...] = -ADAM_LR * (m_hat / (jnp.sqrt(v_hat) + ADAM_EPS) + ADAM_WD * w_ref[...])

    pt = pl.BlockSpec((None, tr, cols), lambda p, i: (p, i, 0))
    st = [pl.BlockSpec((tr, cols), functools.partial(lambda k, p, i: (jnp.where(p == k, i % nth, 0), 0), j // per))
          for j in range(ns)]
    return pl.pallas_call(
        body, name=name, grid=(planes, r // tr),
        in_specs=[pl.BlockSpec(memory_space=pltpu.SMEM), pt, pt, pt] + st,
        out_specs=[pt] * 4,
        out_shape=[jax.ShapeDtypeStruct(w.shape, F32)] * 4,
        compiler_params=_params(("arbitrary", "arbitrary")),
    )(cflag, w, m, v, *flat)


ADA_COLS = 9 * D // NCHIP


def mod_fwd(c_all, w_ada, b_shard, name):
    def body(c_ref, w_ref, b_ref, o_ref):
        cc = c_ref[...]
        sc = cc * jax.nn.sigmoid(cc)
        o_ref[...] = jnp.dot(sc, w_ref[...], preferred_element_type=F32,
                             precision=lax.Precision.HIGHEST) + b_ref[...]

    return pl.pallas_call(
        body, name=name, grid=(2,),
        in_specs=[pl.BlockSpec((NDEV, D), lambda l: (0, 0)),
                  pl.BlockSpec((None, D, ADA_COLS), lambda l: (l, 0, 0)),
                  pl.BlockSpec((None, 1, ADA_COLS), lambda l: (l, 0, 0))],
        out_specs=pl.BlockSpec((None, NDEV, ADA_COLS), lambda l: (l, 0, 0)),
        out_shape=jax.ShapeDtypeStruct((2, NDEV, ADA_COLS), F32),
        compiler_params=_params(("arbitrary",)),
    )(c_all, w_ada, b_shard.reshape(2, 1, ADA_COLS))


def wada_grad(c_all_t, dmod, name):
    ct = ADA_COLS // 3

    def body(c_ref, d_ref, o_ref):
        cc = c_ref[...]
        sc = cc * jax.nn.sigmoid(cc)
        acc = sc[:, 0:1] * d_ref[0:1, :]
        for b in range(1, NDEV):
            acc = acc + sc[:, b:b + 1] * d_ref[b:b + 1, :]
        o_ref[...] = acc

    return pl.pallas_call(
        body, name=name, grid=(2, 3),
        in_specs=[pl.BlockSpec((D, LANES), lambda l, j: (0, 0)),
                  pl.BlockSpec((None, NDEV, ct), lambda l, j: (l, 0, j))],
        out_specs=pl.BlockSpec((None, D, ct), lambda l, j: (l, 0, j)),
        out_shape=jax.ShapeDtypeStruct((2, D, ADA_COLS), F32),
        compiler_params=_params(("arbitrary", "arbitrary")),
    )(c_all_t, dmod)


def _pad_rows(row, rows=SUBLANES):
    return jnp.concatenate([row[None, :], jnp.zeros((rows - 1, row.shape[0]), row.dtype)], axis=0)


def kernel(x, c, w_ada, b_ada, norm_g, w_in, q_norm_g, k_norm_g, conv_w, conv_b, w_out, ffn_w1, ffn_w2, loss_target, m_w_ada, m_b_ada, m_norm_g, m_w_in, m_q_norm_g, m_k_norm_g, m_conv_w, m_conv_b, m_w_out, m_ffn_w1, m_ffn_w2, v_w_ada, v_b_ada, v_norm_g, v_w_in, v_q_norm_g, v_k_norm_g, v_conv_w, v_conv_b, v_w_out, v_ffn_w1, v_ffn_w2):
    ix, iy, ic = lax.axis_index("x"), lax.axis_index("y"), lax.axis_index("c")
    chip = 2 * ix + iy
    dev = 2 * chip + ic
    cflag = jnp.reshape(ic, (1,)).astype(jnp.int32)
    ngw = norm_g.shape[-1]
    cww = conv_w.shape[-1]

    pack = jnp.concatenate([c[0], norm_g.reshape(-1), conv_w.reshape(-1)])
    got = small_all_gather(_pad_rows(pack), "gather_c_normg_convw")[::SUBLANES]
    c_all = got[:, :D]
    per_chip = got[::2]
    ng_full = jnp.concatenate([per_chip[j, D:D + 6 * ngw].reshape(2, 3, ngw) for j in range(NCHIP)], axis=-1)
    cw_full = jnp.concatenate([per_chip[j, D + 6 * ngw:].reshape(2, 3, cww) for j in range(NCHIP)], axis=-1)

    b_shard = lax.dynamic_slice_in_dim(b_ada, chip * ADA_COLS, ADA_COLS, axis=1)
    mod_blk = mod_fwd(c_all, w_ada, b_shard, "mod_fwd").reshape(2 * NDEV, ADA_COLS)
    mod_all = small_all_gather(mod_blk, "gather_mod").reshape(NDEV, 2, NDEV, ADA_COLS)[::2]
    mod_mine = lax.dynamic_index_in_dim(mod_all, dev, axis=2, keepdims=False)
    mods = [mod_mine[:, l, :].reshape(-1) for l in range(2)]

    shards, gvecs, cws = [], [], []
    for l in range(2):
        shards.append(dict(w1=[ffn_w1[l, i].astype(MXU_DTYPE) for i in range(2)],
                           w2=[ffn_w2[l, i].astype(MXU_DTYPE) for i in range(2)],
                           win=w_in[l].astype(MXU_DTYPE), wout=w_out[l].astype(MXU_DTYPE)))
        gv = jnp.stack([jnp.tile(q_norm_g[l], AW // HD), jnp.tile(k_norm_g[l], AW // HD)])
        gvecs.append(jnp.concatenate([gv, jnp.zeros((SUBLANES - 2, AW), F32)], axis=0))
        cws.append(jnp.concatenate([cw_full[l], conv_b[l][None, :], jnp.zeros((SUBLANES - 4, CW), F32)], axis=0))
    w_first = gather_split([shards[0]["w1"][0], shards[0]["w2"][0]], "gather_first_ffn")

    loss_blk, dx, totals, sums = local_step(x[0], loss_target[0], mods, [ng_full[0], ng_full[1]], gvecs, cws,
                                            shards, w_first, cflag)

    dmods, dngs, dqg, dkg, dcw, dcb = [], [], [], [], [], []
    for l in range(2):
        s0, s1, so, s2, sm = sums[l]
        dmods.append(jnp.concatenate([s0[0], s0[1], s0[3], s1[0], s1[1], so[0], s2[0], s2[1], s2[3]]))
        dngs.append(jnp.concatenate([s0[2], s1[2], s2[2]]))
        dqg.append(sm[0].reshape(AW // HD, HD).sum(0))
        dkg.append(sm[1].reshape(AW // HD, HD).sum(0))
        dcw.append(sm[2:5].reshape(-1))
        dcb.append(sm[5])
    small = jnp.concatenate(dmods + dngs + dqg + dkg + dcw + dcb + [loss_blk[0]])
    small_all = small_all_gather(_pad_rows(small), "gather_small_grads")[::SUBLANES]
    nm = 9 * D
    dmod_all = small_all[:, :2 * nm].reshape(NDEV, 2, NCHIP, ADA_COLS)
    dmod_mine = lax.dynamic_index_in_dim(dmod_all, chip, axis=2, keepdims=False).transpose(1, 0, 2)
    tot = sum_devices(small_all, "sum_small_grads")[0]
    o = 2 * nm
    g_b_ada = tot[:o].reshape(2, nm)
    g_norm_g = lax.dynamic_slice_in_dim(tot[o:o + 6 * D].reshape(2, 3, D), chip * ngw, ngw, axis=2)
    o += 6 * D
    g_qg = tot[o:o + 2 * HD].reshape(2, HD)
    o += 2 * HD
    g_kg = tot[o:o + 2 * HD].reshape(2, HD)
    o += 2 * HD
    g_cw = lax.dynamic_slice_in_dim(tot[o:o + 6 * CW].reshape(2, 3, CW), chip * cww, cww, axis=2)
    o += 6 * CW
    g_cb = tot[o:o + 2 * CW].reshape(2, CW)
    loss = tot[o + 2 * CW]

    c_all_t = jnp.concatenate([c_all.T, jnp.zeros((D, LANES - NDEV), F32)], axis=1)
    g_wada_src = wada_grad(c_all_t, dmod_mine, "wada_grad")

    def halves(k_of_plane):
        return [(totals[l][0][k], totals[l][1][k]) for l, k in k_of_plane]

    r_wada = adamw(w_ada, m_w_ada, v_w_ada, [g_wada_src[0], g_wada_src[1]], cflag, "adamw_w_ada")
    r_win = adamw(w_in, m_w_in, v_w_in, halves([(0, 2), (1, 2)]), cflag, "adamw_w_in", halves=True)
    r_wout = adamw(w_out, m_w_out, v_w_out, halves([(0, 3), (1, 3)]), cflag, "adamw_w_out", halves=True)
    r_w1 = adamw(ffn_w1.reshape(4, D, HALF), m_ffn_w1.reshape(4, D, HALF), v_ffn_w1.reshape(4, D, HALF),
                 halves([(0, 0), (0, 4), (1, 0), (1, 4)]), cflag, "adamw_ffn_w1", halves=True)
    w2r = DFF // NCHIP
    r_w2 = adamw(ffn_w2.reshape(4, w2r, D), m_ffn_w2.reshape(4, w2r, D), v_ffn_w2.reshape(4, w2r, D),
                 halves([(0, 1), (0, 5), (1, 1), (1, 5)]), cflag, "adamw_ffn_w2", halves=True)
    r_w1 = [t.reshape(ffn_w1.shape) for t in r_w1]
    r_w2 = [t.reshape(ffn_w2.shape) for t in r_w2]

    smalls = [("b_ada", b_ada, m_b_ada, v_b_ada, g_b_ada), ("norm_g", norm_g, m_norm_g, v_norm_g, g_norm_g),
              ("q_norm_g", q_norm_g, m_q_norm_g, v_q_norm_g, g_qg), ("k_norm_g", k_norm_g, m_k_norm_g, v_k_norm_g, g_kg),
              ("conv_w", conv_w, m_conv_w, v_conv_w, g_cw), ("conv_b", conv_b, m_conv_b, v_conv_b, g_cb)]
    n_small = sum(t[1].size for t in smalls)
    pad = (-n_small) % (16 * LANES)

    def packed(idx):
        flat = jnp.concatenate([t[idx].reshape(-1) for t in smalls] + [jnp.zeros((pad,), F32)])
        return flat.reshape(-1, LANES)

    r_small = adamw(packed(1)[None], packed(2)[None], packed(3)[None], [packed(4)], cflag, "adamw_small")
    small_out = {}
    o = 0
    for name_, w_, _, _, _ in smalls:
        small_out[name_] = [t.reshape(-1)[o:o + w_.size].reshape(w_.shape) for t in r_small]
        o += w_.size

    res = {"w_ada": r_wada, "w_in": r_win, "w_out": r_wout, "ffn_w1": r_w1, "ffn_w2": r_w2, **small_out}
    order = ["w_ada", "b_ada", "norm_g", "w_in", "q_norm_g", "k_norm_g", "conv_w", "conv_b", "w_out", "ffn_w1", "ffn_w2"]
    outs = [loss, dx[None]]
    for k in range(4):
        outs += [res[nm_][k] for nm_ in order]
    return tuple(outs)
```

```python
import functools

import jax
import jax.numpy as jnp
from jax import lax
from jax.experimental import pallas as pl
from jax.experimental.pallas import tpu as pltpu

F32 = jnp.float32
MXU_DTYPE = jnp.bfloat16
ACT_DTYPE = jnp.bfloat16
WIRE_DTYPE = jnp.bfloat16

D = 1024
HD = 64
AW = 512
CW = 512
DFF = 2816
HALF = DFF // 2
INC = 3 * AW + 3 * CW
NCHIP = 4
NDEV = 8
QBLK = 128
ATTN_QBLOCKS = 16
ATTN_INTERLEAVE = 8
ATTN_CHUNK_ROWS = 4096
DILATIONS = (1, 4, 16)
EPS = 1e-6
NEG = -1e30
LANES = 128
SUBLANES = 8
HALO_ROWS = 16
VMEM_LIMIT = 56 * 1024 * 1024

ADAM_LR = 0.001
ADAM_B1 = 0.9
ADAM_B2 = 0.999
ADAM_EPS = 1e-08
ADAM_WD = 0.01
ADAM_STEP = 10

NT_DIMS = (((1,), (1,)), ((), ()))
TN_DIMS = (((0,), (0,)), ((), ()))


def _params(sem, vmem=VMEM_LIMIT):
    return pltpu.CompilerParams(dimension_semantics=sem, vmem_limit_bytes=vmem)


def _row_tile(n, want):
    t = min(n, want)
    assert n % t == 0
    return t


def _ada(xt, vec_ref):
    ng, sc, sh, gt = vec_ref[0:1, :], vec_ref[1:2, :], vec_ref[2:3, :], vec_ref[3:4, :]
    r = lax.rsqrt(jnp.mean(xt * xt, axis=-1, keepdims=True) + EPS)
    return xt * r, r, ng * (1.0 + sc), ng, sc, sh, gt


def _ada_bwd(dh, xhat, r, gain, ng, sc):
    dshift = jnp.sum(dh, axis=0, keepdims=True)
    dhx = dh * xhat
    dscale = jnp.sum(dhx, axis=0, keepdims=True) * ng
    dng = jnp.sum(dhx, axis=0, keepdims=True) * (1.0 + sc)
    dxhat = dh * gain
    dx = r * (dxhat - xhat * jnp.mean(dxhat * xhat, axis=-1, keepdims=True))
    return dx, dshift, dscale, dng


def _acc_rows(sums_ref, first, rows):
    @pl.when(first)
    def _():
        sums_ref[...] = jnp.zeros_like(sums_ref)
    for k, row in enumerate(rows):
        sums_ref[k:k + 1, :] += row


MESH = pl.DeviceIdType.MESH
ANY = pl.BlockSpec(memory_space=pl.ANY)


def _here():
    return lax.axis_index("x"), lax.axis_index("y"), lax.axis_index("c")


def _ici_copies(src_refs, dst_refs, send_sems, recv_sems, local_sems, scatter):
    x, y, c = _here()
    my_chip = 2 * x + y
    peers = [(1 - x, y), (x, 1 - y), (1 - x, 1 - y)]
    local, out, inc = [], [], []
    for a, (src, dst) in enumerate(zip(src_refs, dst_refs)):
        local.append(pltpu.make_async_copy(src.at[my_chip] if scatter else src, dst.at[my_chip], local_sems.at[a]))
        for j, (px, py) in enumerate(peers):
            sems = dict(send_sem=send_sems.at[3 * a + j], recv_sem=recv_sems.at[3 * a + j],
                        device_id=(px, py, c), device_id_type=MESH)
            out.append(pltpu.make_async_remote_copy(
                src_ref=src.at[2 * px + py] if scatter else src, dst_ref=dst.at[my_chip], **sems))
            inc.append(pltpu.make_async_remote_copy(
                src_ref=src.at[my_chip] if scatter else src, dst_ref=dst.at[2 * px + py], **sems))
    return local, out, inc


def _swap_copies(src_refs, dst_refs, send_sems, recv_sems, halves):
    x, y, c = _here()
    cps = []
    for k, (src, dst) in enumerate(zip(src_refs, dst_refs)):
        if halves:
            r2 = src.shape[1] // 2
            src = src.at[:, pl.ds((1 - c) * r2, r2), :]
        cps.append(pltpu.make_async_remote_copy(
            src_ref=src, dst_ref=dst, send_sem=send_sems.at[k], recv_sem=recv_sems.at[k],
            device_id=(x, y, 1 - c), device_id_type=MESH))
    return cps


class Carry:
    def __init__(self, kind, srcs):
        self.kind, self.srcs, n = kind, list(srcs), len(srcs)
        if kind == "gather":
            shapes = [(NCHIP,) + s.shape for s in srcs]
        elif kind == "swap_halves":
            shapes = [(s.shape[0], s.shape[1] // 2, s.shape[2]) for s in srcs]
        else:
            shapes = [s.shape for s in srcs]
        self.out_shape = [jax.ShapeDtypeStruct(sh, s.dtype) for sh, s in zip(shapes, srcs)]
        dma = pltpu.SemaphoreType.DMA
        self.sems = [dma((3 * n,)), dma((3 * n,)), dma((n,))] if kind in ("gather", "scatter") else [dma((n,)), dma((n,))]

    def start(self, srcs, dsts, sems):
        if self.kind in ("gather", "scatter"):
            local, out, _ = _ici_copies(srcs, dsts, *sems, self.kind == "scatter")
            for cp in local + out:
                cp.start()
        else:
            for cp in _swap_copies(srcs, dsts, *sems, self.kind == "swap_halves"):
                cp.start()

    def wait(self, srcs, dsts, sems):
        if self.kind in ("gather", "scatter"):
            local, out, inc = _ici_copies(srcs, dsts, *sems, self.kind == "scatter")
            for cp in inc:
                cp.wait_recv()
            for cp in out:
                cp.wait_send()
            for cp in local:
                cp.wait()
        else:
            cps = _swap_copies(srcs, dsts, *sems, self.kind == "swap_halves")
            for cp in cps:
                cp.wait_recv()
            for cp in cps:
                cp.wait_send()


def run_carry(carry, name):
    n = len(carry.srcs)

    def body(*refs):
        srcs, dsts, sems = refs[:n], refs[n:2 * n], refs[2 * n:]
        carry.start(srcs, dsts, sems)
        carry.wait(srcs, dsts, sems)

    return pl.pallas_call(body, name=name, out_shape=carry.out_shape, in_specs=[ANY] * n, out_specs=[ANY] * n,
                          scratch_shapes=carry.sems)(*carry.srcs)


def gather_split(srcs, name):
    n = len(srcs)

    def body(*refs):
        src_refs, dst_refs = refs[:n], refs[n:2 * n]
        send_sems, recv_sems, fwd_send, fwd_recv, local_sems = refs[2 * n:]
        x, y, c = _here()
        my_chip = 2 * x + y
        peers = [(1 - x, y), (x, 1 - y), (1 - x, 1 - y)]

        def half(ref, h):
            r2 = ref.shape[0] // 2
            return ref.at[pl.ds(h * r2, r2), :]

        local, out, landed, passed, arriving = [], [], [], [], []
        for a, (src, dst) in enumerate(zip(src_refs, dst_refs)):
            local.append(pltpu.make_async_copy(src, dst.at[my_chip], local_sems.at[a]))
            for j, (px, py) in enumerate(peers):
                k = 3 * a + j
                theirs = dst.at[2 * px + py]
                ici = dict(send_sem=send_sems.at[k], recv_sem=recv_sems.at[k], device_id=(px, py, c), device_id_type=MESH)
                d2d = dict(send_sem=fwd_send.at[k], recv_sem=fwd_recv.at[k], device_id=(x, y, 1 - c), device_id_type=MESH)
                out.append(pltpu.make_async_remote_copy(src_ref=half(src, c), dst_ref=half(dst.at[my_chip], c), **ici))
                landed.append(pltpu.make_async_remote_copy(src_ref=half(src, c), dst_ref=half(theirs, c), **ici))
                passed.append(pltpu.make_async_remote_copy(src_ref=half(theirs, c), dst_ref=half(theirs, c), **d2d))
                arriving.append(pltpu.make_async_remote_copy(src_ref=half(theirs, c), dst_ref=half(theirs, 1 - c), **d2d))
        for cp in local + out:
            cp.start()
        for got, fwd in zip(landed, passed):
            got.wait_recv()
            fwd.start()
        for cp in arriving:
            cp.wait_recv()
        for cp in out + passed:
            cp.wait_send()
        for cp in local:
            cp.wait()

    dma = pltpu.SemaphoreType.DMA
    return pl.pallas_call(
        body, name=name, out_shape=[jax.ShapeDtypeStruct((NCHIP,) + s.shape, s.dtype) for s in srcs],
        in_specs=[ANY] * n, out_specs=[ANY] * n,
        scratch_shapes=[dma((3 * n,)), dma((3 * n,)), dma((3 * n,)), dma((3 * n,)), dma((n,))],
    )(*srcs)


def _pcall(body, name, grid, in_specs, out_specs, out_shape, sem, args, carry=None, scratch=()):
    if carry is None:
        outs = pl.pallas_call(body, name=name, grid=grid, in_specs=in_specs, out_specs=out_specs,
                              out_shape=out_shape, scratch_shapes=list(scratch), compiler_params=_params(sem))(*args)
        return outs, []
    n_in, n_out, nc, ns = len(in_specs), len(out_specs), len(carry.srcs), len(scratch)

    def wrapped(*refs):
        ins, csrc = refs[:n_in], refs[n_in:n_in + nc]
        outs, cdst = refs[n_in + nc:n_in + nc + n_out], refs[n_in + nc + n_out:n_in + 2 * nc + n_out]
        own = refs[n_in + 2 * nc + n_out:n_in + 2 * nc + n_out + ns]
        sems = refs[n_in + 2 * nc + n_out + ns:]
        ids = [pl.program_id(a) for a in range(len(grid))]
        first = functools.reduce(jnp.logical_and, [i == 0 for i in ids])
        last = functools.reduce(jnp.logical_and, [i == g - 1 for i, g in zip(ids, grid)])

        @pl.when(first)
        def _():
            carry.start(csrc, cdst, sems)

        body(*ins, *outs, *own)

        @pl.when(last)
        def _():
            carry.wait(csrc, cdst, sems)

    res = pl.pallas_call(
        wrapped, name=name, grid=grid,
        in_specs=list(in_specs) + [ANY] * nc, out_specs=list(out_specs) + [ANY] * nc,
        out_shape=list(out_shape) + carry.out_shape,
        scratch_shapes=list(scratch) + carry.sems, compiler_params=_params(sem),
    )(*args, *carry.srcs)
    return res[:n_out], res[n_out:]


def ffn_fwd(x, vec, w1p, w2, gs, name, carry=None, target=None):
    S = x.shape[0]
    tm = _row_tile(S, 512)

    def body(x_ref, *refs):
        if target is None:
            vec_ref, w1_ref, w2_ref, xn_ref, a_ref, f_ref = refs
        else:
            t_ref, vec_ref, w1_ref, w2_ref, xn_ref, a_ref, f_ref, l_ref = refs
        xt = x_ref[...]
        xhat, _, gain, _, _, sh, gt = _ada(xt, vec_ref)
        h = (xhat * gain + sh).astype(MXU_DTYPE)
        f = jnp.zeros((tm, D), F32)
        for hf in range(2):
            g = jnp.dot(h, w1_ref[hf], preferred_element_type=F32)
            up = jnp.dot(h, w1_ref[2 + hf], preferred_element_type=F32)
            a_ref[:, hf * HALF:(hf + 1) * HALF] = g.astype(a_ref.dtype)
            a_ref[:, DFF + hf * HALF:DFF + (hf + 1) * HALF] = up.astype(a_ref.dtype)
            act = (g * jax.nn.sigmoid(g) * up).astype(MXU_DTYPE)
            f = f + jnp.dot(act, w2_ref[hf * HALF:(hf + 1) * HALF, :], preferred_element_type=F32)
        f_ref[...] = f.astype(f_ref.dtype)
        xn = xt + (gs * gt) * f
        if target is None:
            xn_ref[...] = xn
        else:
            diff = xn - t_ref[...]
            xn_ref[...] = diff * (1.0 / D)
            part = jnp.sum(jnp.sum(diff * diff, axis=0, keepdims=True), axis=1, keepdims=True) * (0.5 / D)

            @pl.when(pl.program_id(0) == 0)
            def _():
                l_ref[...] = jnp.zeros_like(l_ref)
            l_ref[...] += jnp.broadcast_to(part, l_ref.shape)

    tile = pl.BlockSpec((tm, D), lambda i: (i, 0))
    last = target is not None
    return _pcall(
        body, name, (S // tm,),
        [tile] * (2 if last else 1) + [
            pl.BlockSpec((SUBLANES, D), lambda i: (0, 0)),
            pl.BlockSpec((NCHIP, D, HALF), lambda i: (0, 0, 0), pipeline_mode=pl.Buffered(1)),
            pl.BlockSpec((DFF, D), lambda i: (0, 0), pipeline_mode=pl.Buffered(1))],
        [tile, pl.BlockSpec((tm, 2 * DFF), lambda i: (i, 0)), tile]
        + ([pl.BlockSpec((SUBLANES, LANES), lambda i: (0, 0))] if last else []),
        [jax.ShapeDtypeStruct((S, D), F32),
         jax.ShapeDtypeStruct((S, 2 * DFF), ACT_DTYPE),
         jax.ShapeDtypeStruct((S, D), ACT_DTYPE)]
        + ([jax.ShapeDtypeStruct((SUBLANES, LANES), F32)] if last else []),
        ("arbitrary",), (x, target, vec, w1p, w2) if last else (x, vec, w1p, w2), carry)


def ffn_bwd(dxo, x, a, f, vec, w1p, w2, gs, name, carry=None):
    S = x.shape[0]
    tm = _row_tile(S, 256)

    def body(dxo_ref, x_ref, a_ref, f_ref, vec_ref, w1_ref, w2_ref,
             dxi_ref, hb_ref, dfb_ref, act_ref, da_ref, sums_ref):
        xt = x_ref[...]
        dxo = dxo_ref[...]
        xhat, r, gain, ng, sc, sh, gt = _ada(xt, vec_ref)
        hb_ref[...] = (xhat * gain + sh).astype(hb_ref.dtype)
        dgate = gs * jnp.sum(dxo * f_ref[...].astype(F32), axis=0, keepdims=True)
        df = ((gs * gt) * dxo).astype(MXU_DTYPE)
        dfb_ref[...] = df
        dh = jnp.zeros((tm, D), F32)
        for hf in range(2):
            lo, hi = hf * HALF, (hf + 1) * HALF
            dact = lax.dot_general(df, w2_ref[lo:hi, :], NT_DIMS, preferred_element_type=F32)
            g = a_ref[:, lo:hi].astype(F32)
            up = a_ref[:, DFF + lo:DFF + hi].astype(F32)
            sg = jax.nn.sigmoid(g)
            si = g * sg
            act_ref[:, lo:hi] = (si * up).astype(act_ref.dtype)
            dg = (dact * up * (sg * (1.0 + g * (1.0 - sg)))).astype(MXU_DTYPE)
            dup = (dact * si).astype(MXU_DTYPE)
            da_ref[:, lo:hi] = dg
            da_ref[:, DFF + lo:DFF + hi] = dup
            dh = dh + lax.dot_general(dg, w1_ref[hf], NT_DIMS, preferred_element_type=F32)
            dh = dh + lax.dot_general(dup, w1_ref[2 + hf], NT_DIMS, preferred_element_type=F32)
        dx, dshift, dscale, dng = _ada_bwd(dh, xhat, r, gain, ng, sc)
        dxi_ref[...] = dxo + dx
        _acc_rows(sums_ref, pl.program_id(0) == 0, (dshift, dscale, dng, dgate))

    return _pcall(
        body, name, (S // tm,),
        [pl.BlockSpec((tm, D), lambda i: (i, 0)),
         pl.BlockSpec((tm, D), lambda i: (i, 0)),
         pl.BlockSpec((tm, 2 * DFF), lambda i: (i, 0)),
         pl.BlockSpec((tm, D), lambda i: (i, 0)),
         pl.BlockSpec((SUBLANES, D), lambda i: (0, 0)),
         pl.BlockSpec((NCHIP, D, HALF), lambda i: (0, 0, 0), pipeline_mode=pl.Buffered(1)),
         pl.BlockSpec((DFF, D), lambda i: (0, 0), pipeline_mode=pl.Buffered(1))],
        [pl.BlockSpec((tm, D), lambda i: (i, 0)),
         pl.BlockSpec((tm, D), lambda i: (i, 0)),
         pl.BlockSpec((tm, D), lambda i: (i, 0)),
         pl.BlockSpec((tm, DFF), lambda i: (i, 0)),
         pl.BlockSpec((tm, 2 * DFF), lambda i: (i, 0)),
         pl.BlockSpec((SUBLANES, D), lambda i: (0, 0))],
        [jax.ShapeDtypeStruct((S, D), F32),
         jax.ShapeDtypeStruct((S, D), MXU_DTYPE),
         jax.ShapeDtypeStruct((S, D), MXU_DTYPE),
         jax.ShapeDtypeStruct((S, DFF), MXU_DTYPE),
         jax.ShapeDtypeStruct((S, 2 * DFF), MXU_DTYPE),
         jax.ShapeDtypeStruct((SUBLANES, D), F32)],
        ("arbitrary",), (dxo, x, a, f, vec, w1p, w2), carry)


def wgrad(a, b, kt, nt, name, carry=None):
    T, K = a.shape
    N = b.shape[1]
    pk, pn = K // kt, N // nt
    assert pk == 1 or pn == 1
    tt = _row_tile(T, 2048)
    steps = T // tt

    def body(a_ref, b_ref, o_ref):
        @pl.when(pl.program_id(1) == 0)
        def _():
            o_ref[...] = jnp.zeros_like(o_ref)
        o_ref[...] += lax.dot_general(a_ref[...], b_ref[...], TN_DIMS, preferred_element_type=F32)

    a_map = (lambda p, t: (t, p)) if pk > 1 else (lambda p, t: (t, 0))
    b_map = (lambda p, t: (t, p)) if pn > 1 else (lambda p, t: (t, 0))
    (out,), got = _pcall(
        body, name, (pk * pn, steps),
        [pl.BlockSpec((tt, kt), a_map), pl.BlockSpec((tt, nt), b_map)],
        [pl.BlockSpec((None, kt, nt), lambda p, t: (p, 0, 0))],
        [jax.ShapeDtypeStruct((pk * pn, kt, nt), F32)], ("arbitrary", "arbitrary"), (a, b), carry)
    return out, got


def _head_masks(rows):
    lane = lax.broadcasted_iota(jnp.int32, (rows, LANES), 1)
    return lane < HD


def _pair_stat(x, m_a):
    s_a = jnp.sum(jnp.where(m_a, x, 0.0), axis=1, keepdims=True)
    s_b = jnp.sum(jnp.where(m_a, 0.0, x), axis=1, keepdims=True)
    return s_a, s_b


def mixer_in(x, vec, winp, gvec, name):
    S = x.shape[0]
    tm = _row_tile(S, 512)
    pc = INC // NCHIP

    def body(x_ref, vec_ref, w_ref, g_ref, proj_ref, hb_ref, qn_ref, kn_ref, v_ref, qkv_ref):
        xt = x_ref[...]
        xhat, _, gain, _, _, sh, _ = _ada(xt, vec_ref)
        h = (xhat * gain + sh).astype(MXU_DTYPE)
        hb_ref[...] = h
        for j in range(NCHIP):
            piece = jnp.dot(h, w_ref[j], preferred_element_type=F32)
            proj_ref[:, j * pc:(j + 1) * pc] = piece.astype(proj_ref.dtype)
            if (j + 1) * pc <= 3 * AW:
                qkv_ref[:, j * pc:(j + 1) * pc] = piece
        m_a = _head_masks(tm)
        for which, dst in ((0, qn_ref), (1, kn_ref)):
            for p in range(AW // LANES):
                lo = which * AW + p * LANES
                xp = qkv_ref[:, lo:lo + LANES]
                s_a, s_b = _pair_stat(xp * xp, m_a)
                rr = jnp.where(m_a, lax.rsqrt(s_a * (1.0 / HD) + EPS), lax.rsqrt(s_b * (1.0 / HD) + EPS))
                gp = g_ref[which:which + 1, p * LANES:(p + 1) * LANES]
                dst[:, p * LANES:(p + 1) * LANES] = (xp * rr * gp).astype(dst.dtype)
        v_ref[...] = qkv_ref[:, 2 * AW:3 * AW]

    assert 2 * pc == 3 * AW
    return pl.pallas_call(
        body, name=name, grid=(S // tm,), scratch_shapes=[pltpu.VMEM((tm, 3 * AW), F32)],
        in_specs=[pl.BlockSpec((tm, D), lambda i: (i, 0)),
                  pl.BlockSpec((SUBLANES, D), lambda i: (0, 0)),
                  pl.BlockSpec((NCHIP, D, pc), lambda i: (0, 0, 0), pipeline_mode=pl.Buffered(1)),
                  pl.BlockSpec((SUBLANES, AW), lambda i: (0, 0))],
        out_specs=[pl.BlockSpec((tm, INC), lambda i: (i, 0)),
                   pl.BlockSpec((tm, D), lambda i: (i, 0)),
                   pl.BlockSpec((tm, AW), lambda i: (i, 0)),
                   pl.BlockSpec((tm, AW), lambda i: (i, 0)),
                   pl.BlockSpec((tm, AW), lambda i: (i, 0))],
        out_shape=[jax.ShapeDtypeStruct((S, INC), ACT_DTYPE),
                   jax.ShapeDtypeStruct((S, D), MXU_DTYPE),
                   jax.ShapeDtypeStruct((S, AW), F32),
                   jax.ShapeDtypeStruct((S, AW), F32),
                   jax.ShapeDtypeStruct((S, AW), F32)],
        compiler_params=_params(("arbitrary",)),
    )(x, vec, winp, gvec)


def _band_masks(ncol):
    row = lax.broadcasted_iota(jnp.int32, (2 * QBLK, ncol), 0) & (QBLK - 1)
    col = lax.broadcasted_iota(jnp.int32, (2 * QBLK, ncol), 1)
    return row, col


def _stack_heads(t, m_a):
    zero = jnp.zeros_like(t)
    return jnp.concatenate([jnp.where(m_a, t, zero), jnp.where(m_a, zero, t)], axis=0)


class _AttnLayout:
    def __init__(self, d, S):
        self.d, self.S = d, S
        self.qb = max(1, min(ATTN_QBLOCKS, ATTN_CHUNK_ROWS // (QBLK * d)))
        self.nres = d
        self.nchunk = S // (self.qb * QBLK * d)
        self.grid = (AW // LANES, self.nchunk)
        self.unroll = max(1, min(d, ATTN_INTERLEAVE // self.qb))

    def _spec(self, blocks, row_of):
        return pl.BlockSpec((blocks * QBLK * self.d, LANES), lambda hp, j: (row_of(j), hp))

    def cur(self, chunk_of):
        return self._spec(self.qb, chunk_of)

    def prev(self, chunk_of):
        return self._spec(1, lambda j: jnp.maximum(chunk_of(j) * self.qb - 1, 0))

    def idx(self, b, r):
        if self.d == 1:
            return (pl.ds(b * QBLK, QBLK), slice(None))
        return (pl.ds(b * QBLK * self.d + r, QBLK, stride=self.d), slice(None))

    def per_residue(self, fn):
        if self.nres == 1:
            fn(0)
        else:
            def step(it, carry):
                for k in range(self.unroll):
                    fn(it * self.unroll + k)
                return carry
            lax.fori_loop(0, self.nres // self.unroll, step, 0)


def attn_fwd(qn, kn, v, d, name, carry=None):
    S = qn.shape[0]
    lay = _AttnLayout(d, S)
    qb = lay.qb

    def body(q_ref, kc_ref, kp_ref, vc_ref, vp_ref, o_ref, lse_ref):
        i = pl.program_id(1)
        m_a = _head_masks(QBLK)
        row, col = _band_masks(2 * QBLK)
        dist = row + QBLK - col
        band = (dist >= 0) & (dist <= QBLK)
        first = band & ((i > 0) | (col >= QBLK))

        def residue(r):
            kt = [kp_ref[lay.idx(0, r)].astype(MXU_DTYPE)]
            vt = [vp_ref[lay.idx(0, r)].astype(MXU_DTYPE)]
            for b in range(qb):
                kt.append(kc_ref[lay.idx(b, r)].astype(MXU_DTYPE))
                vt.append(vc_ref[lay.idx(b, r)].astype(MXU_DTYPE))
            for b in range(qb):
                rows = lay.idx(b, r)
                q = (q_ref[rows] * (HD ** -0.5)).astype(MXU_DTYPE)
                kcat = jnp.concatenate([kt[b], kt[b + 1]], axis=0)
                vcat = jnp.concatenate([vt[b], vt[b + 1]], axis=0)
                mask = first if b == 0 else band
                s = lax.dot_general(_stack_heads(q, m_a), kcat, NT_DIMS, preferred_element_type=F32)
                s = jnp.where(mask, s, NEG)
                m = jnp.max(s, axis=1, keepdims=True)
                p = jnp.exp(s - m)
                l = jnp.sum(p, axis=1, keepdims=True)
                o = jnp.dot(p.astype(MXU_DTYPE), vcat, preferred_element_type=F32) / l
                lse = jnp.broadcast_to(m + jnp.log(l), (2 * QBLK, LANES))
                o_ref[rows] = jnp.where(m_a, o[:QBLK], o[QBLK:])
                lse_ref[rows] = jnp.where(m_a, lse[:QBLK], lse[QBLK:])

        lay.per_residue(residue)

    cur, prev = lay.cur(lambda j: j), lay.prev(lambda j: j)
    return _pcall(body, name, lay.grid, [cur, cur, prev, cur, prev], [cur, cur],
                  [jax.ShapeDtypeStruct((S, AW), F32)] * 2, ("arbitrary", "arbitrary"), (qn, kn, kn, v, v), carry)


def _both_heads(t, m_a):
    other = pltpu.roll(t, HD, 1)
    return jnp.concatenate([jnp.where(m_a, t, other), jnp.where(m_a, other, t)], axis=0)


def attn_bwd(qn, kn, v, dycat, lse, delta, d, name, carry=None):
    S = qn.shape[0]
    lay = _AttnLayout(d, S)
    qb, nchunk = lay.qb, lay.nchunk

    def body(q_ref, kc_ref, kp_ref, vc_ref, vp_ref, do_ref, lse_ref, dl_ref,
             dq_ref, dk_ref, dv_ref, ck_ref, cv_ref):
        j = pl.program_id(1)
        i = nchunk - 1 - j
        m_a = _head_masks(QBLK)
        row, col = _band_masks(2 * QBLK)
        dist = row + QBLK - col
        band = (dist >= 0) & (dist <= QBLK)
        first = band & ((i > 0) | (col >= QBLK))

        def residue(r):
            def tiles(ref, cast):
                out = [ref[lay.idx(b, r)] for b in range(qb)]
                return [t.astype(MXU_DTYPE) for t in out] if cast else out

            def ktiles(cur_ref, prev_ref):
                return [prev_ref[lay.idx(0, r)].astype(MXU_DTYPE)] + tiles(cur_ref, True)

            qt = [(t * (HD ** -0.5)).astype(MXU_DTYPE) for t in tiles(q_ref, False)]
            dot_ = tiles(do_ref, True)
            lse_t = tiles(lse_ref, False)
            dl_t = tiles(dl_ref, False)
            kt = ktiles(kc_ref, kp_ref)
            vt = ktiles(vc_ref, vp_ref)
            dk_acc = [jnp.zeros((QBLK, LANES), F32) for _ in range(qb)]
            dv_acc = [jnp.zeros((QBLK, LANES), F32) for _ in range(qb)]
            crow = pl.ds(0, QBLK) if lay.nres == 1 else pl.ds(pl.multiple_of(r * QBLK, QBLK), QBLK)
            dk_acc[qb - 1] = jnp.where(j > 0, ck_ref[crow, :], 0.0)
            dv_acc[qb - 1] = jnp.where(j > 0, cv_ref[crow, :], 0.0)
            for x in range(qb):
                kcat = jnp.concatenate([kt[x], kt[x + 1]], axis=0)
                vcat = jnp.concatenate([vt[x], vt[x + 1]], axis=0)
                q2 = _stack_heads(qt[x], m_a)
                do2 = _stack_heads(dot_[x], m_a)
                lse2 = _both_heads(lse_t[x], m_a)
                dl2 = _both_heads(dl_t[x], m_a)
                lse2 = jnp.concatenate([lse2, lse2], axis=1)
                dl2 = jnp.concatenate([dl2, dl2], axis=1)
                s = lax.dot_general(q2, kcat, NT_DIMS, preferred_element_type=F32)
                p = jnp.exp(jnp.where(first if x == 0 else band, s, NEG) - lse2)
                dp = lax.dot_general(do2, vcat, NT_DIMS, preferred_element_type=F32)
                ds = p * (dp - dl2)
                dq = jnp.dot(ds.astype(MXU_DTYPE), kcat, preferred_element_type=F32)
                dq_ref[lay.idx(x, r)] = jnp.where(m_a, dq[:QBLK], dq[QBLK:]) * (HD ** -0.5)
                dk = jnp.dot(ds.T.astype(MXU_DTYPE), q2, preferred_element_type=F32)
                dv = jnp.dot(p.T.astype(MXU_DTYPE), do2, preferred_element_type=F32)
                if x == 0:
                    ck_ref[crow, :] = dk[:QBLK]
                    cv_ref[crow, :] = dv[:QBLK]
                else:
                    dk_acc[x - 1] = dk_acc[x - 1] + dk[:QBLK]
                    dv_acc[x - 1] = dv_acc[x - 1] + dv[:QBLK]
                dk_acc[x] = dk_acc[x] + dk[QBLK:]
                dv_acc[x] = dv_acc[x] + dv[QBLK:]
            for kb in range(qb):
                dk_ref[lay.idx(kb, r)] = dk_acc[kb]
                dv_ref[lay.idx(kb, r)] = dv_acc[kb]

        lay.per_residue(residue)

    cur, prev = lay.cur(lambda j: nchunk - 1 - j), lay.prev(lambda j: nchunk - 1 - j)
    carried = pltpu.VMEM((lay.nres * QBLK, LANES), F32)
    return _pcall(
        body, name, lay.grid, [cur, cur, prev, cur, prev, cur, cur, cur], [cur, cur, cur],
        [jax.ShapeDtypeStruct((S, AW), F32)] * 3, ("arbitrary", "arbitrary"),
        (qn, kn, kn, v, v, dycat, lse, delta), carry, scratch=[carried, carried])


def _shift_down(x, halo_prev, k, row):
    tm = x.shape[0]
    tail = jnp.concatenate([pltpu.roll(halo_prev, k, 0), jnp.zeros((tm - SUBLANES, x.shape[1]), x.dtype)], axis=0)
    return jnp.where(row < k, tail, pltpu.roll(x, k, 0))


def _shift_up(x, halo_next, k, row):
    tm = x.shape[0]
    head = jnp.concatenate([jnp.zeros((tm - SUBLANES, x.shape[1]), x.dtype), pltpu.roll(halo_next, SUBLANES - k, 0)], axis=0)
    return jnp.where(row >= tm - k, head, pltpu.roll(x, tm - k, 0))


def _conv_fwd(cu, halo_cu, cw_ref, row):
    u1 = _shift_down(cu, halo_cu, 1, row)
    u2 = _shift_down(cu, halo_cu, 2, row)
    cv = cw_ref[0:1, :] * u2 + cw_ref[1:2, :] * u1 + cw_ref[2:3, :] * cu + cw_ref[3:4, :]
    return cv, u1, u2


def mixer_out(os_, lses, proj, cw, x, vec, wout, name, carry=None):
    S = proj.shape[0]
    tm = _row_tile(S, 512)
    hb = tm // HALO_ROWS

    def body(o1, o2, o3, l1, l2, l3, pc_ref, ph_ref, cw_ref, x_ref, vec_ref, w_ref, ycat_ref, lse_ref, xn_ref, y_ref):
        i = pl.program_id(0)
        for p in range(AW // LANES):
            cs = slice(p * LANES, (p + 1) * LANES)
            ls = [l[:, cs] for l in (l1, l2, l3)]
            mx = jnp.maximum(jnp.maximum(ls[0], ls[1]), ls[2])
            t = mx + jnp.log(jnp.exp(ls[0] - mx) + jnp.exp(ls[1] - mx) + jnp.exp(ls[2] - mx))
            lse_ref[:, cs] = t
            acc = jnp.zeros((tm, LANES), F32)
            for l, o in zip(ls, (o1, o2, o3)):
                acc = acc + jnp.exp(l - t) * o[:, cs]
            ycat_ref[:, cs] = acc.astype(ycat_ref.dtype)
        row = lax.broadcasted_iota(jnp.int32, (tm, CW), 0)
        gb, gc, u = (pc_ref[:, k * CW:(k + 1) * CW].astype(F32) for k in range(3))
        ph = ph_ref[...].astype(F32)[HALO_ROWS - SUBLANES:]
        halo_cu = jnp.where(i > 0, ph[:, CW:2 * CW] * ph[:, 2 * CW:3 * CW], 0.0)
        cv, _, _ = _conv_fwd(gc * u, halo_cu, cw_ref, row)
        ycat_ref[:, AW:AW + CW] = (gb * cv).astype(ycat_ref.dtype)
        y = jnp.dot(ycat_ref[...].astype(MXU_DTYPE), w_ref[...], preferred_element_type=F32)
        xn_ref[...] = x_ref[...] + vec_ref[3:4, :] * y
        y_ref[...] = y.astype(y_ref.dtype)

    ot = pl.BlockSpec((tm, AW), lambda i: (i, 0))
    t = pl.BlockSpec((tm, D), lambda i: (i, 0))
    return _pcall(
        body, name, (S // tm,),
        [ot] * 6 + [pl.BlockSpec((tm, 3 * CW), lambda i: (i, 1)),
                    pl.BlockSpec((HALO_ROWS, 3 * CW), lambda i: (jnp.maximum(i * hb - 1, 0), 1)),
                    pl.BlockSpec((SUBLANES, CW), lambda i: (0, 0)),
                    t, pl.BlockSpec((SUBLANES, D), lambda i: (0, 0)), pl.BlockSpec((D, D), lambda i: (0, 0))],
        [t, ot, t, t],
        [jax.ShapeDtypeStruct((S, D), ACT_DTYPE), jax.ShapeDtypeStruct((S, AW), F32),
         jax.ShapeDtypeStruct((S, D), F32), jax.ShapeDtypeStruct((S, D), ACT_DTYPE)],
        ("arbitrary",), (*os_, *lses, proj, proj, cw, x, vec, wout), carry)


def out_proj_bwd(dxo, y, ycat, vec, wout, name, carry=None):
    S = dxo.shape[0]
    tm = _row_tile(S, 512)

    def body(dxo_ref, y_ref, yc_ref, vec_ref, w_ref, dyb_ref, dyc_ref, dl_ref, sums_ref):
        dxo = dxo_ref[...]
        dgate = jnp.sum(dxo * y_ref[...].astype(F32), axis=0, keepdims=True)
        dy = (vec_ref[3:4, :] * dxo).astype(MXU_DTYPE)
        dyb_ref[...] = dy
        dyc_ref[...] = lax.dot_general(dy, w_ref[...], NT_DIMS, preferred_element_type=F32)
        m_a = _head_masks(tm)
        for p in range(AW // LANES):
            cs = slice(p * LANES, (p + 1) * LANES)
            s_a, s_b = _pair_stat(dyc_ref[:, cs] * yc_ref[:, cs].astype(F32), m_a)
            dl_ref[:, cs] = jnp.where(m_a, s_a, s_b)
        _acc_rows(sums_ref, pl.program_id(0) == 0, (dgate,))

    t = pl.BlockSpec((tm, D), lambda i: (i, 0))
    at = pl.BlockSpec((tm, AW), lambda i: (i, 0))
    return _pcall(
        body, name, (S // tm,),
        [t, t, t, pl.BlockSpec((SUBLANES, D), lambda i: (0, 0)), pl.BlockSpec((D, D), lambda i: (0, 0))],
        [t, t, at, pl.BlockSpec((SUBLANES, D), lambda i: (0, 0))],
        [jax.ShapeDtypeStruct((S, D), MXU_DTYPE), jax.ShapeDtypeStruct((S, D), F32),
         jax.ShapeDtypeStruct((S, AW), F32), jax.ShapeDtypeStruct((SUBLANES, D), F32)],
        ("arbitrary",), (dxo, y, ycat, vec, wout), carry)


def mixer_mid_bwd(dqs, dks, dvs, proj, dycat, gvec, cw, name, carry=None):
    S = proj.shape[0]
    tm = _row_tile(S, 512)
    hb = tm // SUBLANES
    hp = tm // HALO_ROWS
    nsl = S // SUBLANES
    ntile = S // tm

    def body(dq1, dq2, dq3, dk1, dk2, dk3, dv1, dv2, dv3, pr_ref, pp_ref, pn_ref, dyc_ref, dyn_ref,
             g_ref, cw_ref, dp_ref, sums_ref):
        i = pl.program_id(0)
        m_a = _head_masks(tm)
        gsum = []
        for which, parts in ((0, (dq1, dq2, dq3)), (1, (dk1, dk2, dk3))):
            acc_g = []
            for p in range(AW // LANES):
                lo = which * AW + p * LANES
                cs = slice(p * LANES, (p + 1) * LANES)
                xp = pr_ref[:, lo:lo + LANES].astype(F32)
                s_a, s_b = _pair_stat(xp * xp, m_a)
                rr = jnp.where(m_a, lax.rsqrt(s_a * (1.0 / HD) + EPS), lax.rsqrt(s_b * (1.0 / HD) + EPS))
                xh = xp * rr
                dn = parts[0][:, cs] + parts[1][:, cs] + parts[2][:, cs]
                acc_g.append(jnp.sum(dn * xh, axis=0, keepdims=True))
                t = dn * g_ref[which:which + 1, cs]
                t_a, t_b = _pair_stat(t * xh, m_a)
                mean = jnp.where(m_a, t_a, t_b) * (1.0 / HD)
                dp_ref[:, lo:lo + LANES] = (rr * (t - xh * mean)).astype(dp_ref.dtype)
            gsum.append(jnp.concatenate(acc_g, axis=1))
        dp_ref[:, 2 * AW:3 * AW] = (dv1[...] + dv2[...] + dv3[...]).astype(dp_ref.dtype)
        row = lax.broadcasted_iota(jnp.int32, (tm, CW), 0)
        base = 3 * AW
        gb, gc, u = (pr_ref[:, base + k * CW:base + (k + 1) * CW].astype(F32) for k in range(3))
        cu = gc * u
        pp = pp_ref[...].astype(F32)[HALO_ROWS - SUBLANES:]
        halo_cu = jnp.where(i > 0, pp[:, CW:2 * CW] * pp[:, 2 * CW:3 * CW], 0.0)
        cv, u1, u2 = _conv_fwd(cu, halo_cu, cw_ref, row)
        dyc = dyc_ref[...]
        dp_ref[:, base:base + CW] = (dyc * cv).astype(dp_ref.dtype)
        dcv = dyc * gb
        gb_next = pn_ref[:, 0:CW].astype(F32)[:SUBLANES]
        halo_dcv = jnp.where(i < ntile - 1, dyn_ref[...] * gb_next, 0.0)
        d1 = _shift_up(dcv, halo_dcv, 1, row)
        d2 = _shift_up(dcv, halo_dcv, 2, row)
        dcu = cw_ref[2:3, :] * dcv + cw_ref[1:2, :] * d1 + cw_ref[0:1, :] * d2
        dp_ref[:, base + CW:base + 2 * CW] = (dcu * u).astype(dp_ref.dtype)
        dp_ref[:, base + 2 * CW:base + 3 * CW] = (dcu * gc).astype(dp_ref.dtype)
        rows = (gsum[0], gsum[1],
                jnp.sum(dcv * u2, axis=0, keepdims=True), jnp.sum(dcv * u1, axis=0, keepdims=True),
                jnp.sum(dcv * cu, axis=0, keepdims=True), jnp.sum(dcv, axis=0, keepdims=True))
        _acc_rows(sums_ref, i == 0, rows)

    at = pl.BlockSpec((tm, AW), lambda i: (i, 0))
    return _pcall(
        body, name, (ntile,),
        [at] * 9 + [
            pl.BlockSpec((tm, INC), lambda i: (i, 0)),
            pl.BlockSpec((HALO_ROWS, 3 * CW), lambda i: (jnp.maximum(i * hp - 1, 0), 1)),
            pl.BlockSpec((HALO_ROWS, 3 * CW), lambda i: (jnp.minimum((i + 1) * hp, S // HALO_ROWS - 1), 1)),
            pl.BlockSpec((tm, CW), lambda i: (i, 1)),
            pl.BlockSpec((SUBLANES, CW), lambda i: (jnp.minimum((i + 1) * hb, nsl - 1), 1)),
            pl.BlockSpec((SUBLANES, AW), lambda i: (0, 0)),
            pl.BlockSpec((SUBLANES, CW), lambda i: (0, 0))],
        [pl.BlockSpec((tm, INC), lambda i: (i, 0)), pl.BlockSpec((SUBLANES, AW), lambda i: (0, 0))],
        [jax.ShapeDtypeStruct((S, INC), MXU_DTYPE), jax.ShapeDtypeStruct((SUBLANES, AW), F32)],
        ("arbitrary",), (*dqs, *dks, *dvs, proj, proj, proj, dycat, dycat, gvec, cw), carry)


def mixer_in_bwd(dxo, x, dproj, vec, winp, name, carry=None):
    S = x.shape[0]
    tm = _row_tile(S, 512)
    pc = INC // NCHIP

    def body(dxo_ref, x_ref, dp_ref, vec_ref, w_ref, dxi_ref, sums_ref):
        xhat, r, gain, ng, sc, _, _ = _ada(x_ref[...], vec_ref)
        dh = jnp.zeros((tm, D), F32)
        for j in range(NCHIP):
            dh = dh + lax.dot_general(dp_ref[:, j * pc:(j + 1) * pc], w_ref[j], NT_DIMS, preferred_element_type=F32)
        dx, dshift, dscale, dng = _ada_bwd(dh, xhat, r, gain, ng, sc)
        dxi_ref[...] = dxo_ref[...] + dx
        _acc_rows(sums_ref, pl.program_id(0) == 0, (dshift, dscale, dng))

    t = pl.BlockSpec((tm, D), lambda i: (i, 0))
    return _pcall(
        body, name, (S // tm,),
        [t, t, pl.BlockSpec((tm, INC), lambda i: (i, 0)),
         pl.BlockSpec((SUBLANES, D), lambda i: (0, 0)),
         pl.BlockSpec((NCHIP, D, pc), lambda i: (0, 0, 0), pipeline_mode=pl.Buffered(1))],
        [t, pl.BlockSpec((SUBLANES, D), lambda i: (0, 0))],
        [jax.ShapeDtypeStruct((S, D), F32), jax.ShapeDtypeStruct((SUBLANES, D), F32)],
        ("arbitrary",), (dxo, x, dproj, vec, winp), carry)


def _vec(mod_l, ng_l, i):
    m = mod_l.reshape(3, 3, D)
    rows = jnp.stack([ng_l[i], m[i, 1], m[i, 0], m[i, 2]])
    return jnp.concatenate([rows, jnp.zeros((SUBLANES - 4, D), F32)], axis=0)


def local_step(x, target, mods, ngs, gvecs, cws, shards, w_first, cflag):
    saved = []
    weights = [dict(w1=[None, None], w2=[None, None]) for _ in range(2)]
    weights[0]["w1"][0], weights[0]["w2"][0] = w_first[0], w_first[1].reshape(DFF, D)
    h = x
    for l in range(2):
        w, sh = weights[l], shards[l]
        nxt = shards[l + 1] if l == 0 else None
        vecs = [_vec(mods[l], ngs[l], i) for i in range(3)]
        x0 = h
        (x1, a0, f0), (win, wout, w2b) = ffn_fwd(x0, vecs[0], w["w1"][0], w["w2"][0], 0.5, f"ffn_fwd_l{l}a",
                                                 carry=Carry("gather", [sh["win"], sh["wout"], sh["w2"][1]]))
        w["win"], w["wout"], w["w2"][1] = win, wout.reshape(D, D), w2b.reshape(DFF, D)
        proj, h1b, qn, kn, v = mixer_in(x1, vecs[1], w["win"], gvecs[l], f"mixer_in_l{l}")
        os_, lses, w1b = [], [], {}
        for d in DILATIONS:
            rows = {1: slice(0, D // 2), 16: slice(D // 2, D)}.get(d)
            carry = Carry("gather", [sh["w1"][1][rows]]) if rows else None
            (o, lse_d), w1b[d] = attn_fwd(qn, kn, v, d, f"attn_fwd_l{l}_d{d}", carry=carry)
            os_.append(o)
            lses.append(lse_d)
        w["w1"][1] = jnp.concatenate([w1b[1][0], w1b[16][0]], axis=1)
        (ycat, lse, x2, y), got = mixer_out(os_, lses, proj, cws[l], x1, vecs[1], w["wout"], f"mixer_out_l{l}",
                                            carry=Carry("gather", [nxt["w2"][0]]) if nxt else None)
        if nxt:
            weights[1]["w2"][0] = got[0].reshape(DFF, D)
        if nxt:
            (h, a2, f2), got = ffn_fwd(x2, vecs[2], w["w1"][1], w["w2"][1], 0.5, f"ffn_fwd_l{l}b",
                                       carry=Carry("gather", [nxt["w1"][0]]))
            weights[1]["w1"][0] = got[0]
        else:
            (dx, a2, f2, loss_blk), _ = ffn_fwd(x2, vecs[2], w["w1"][1], w["w2"][1], 0.5, f"ffn_fwd_l{l}b",
                                                target=target)
        saved.append(dict(vecs=vecs, x0=x0, a0=a0, f0=f0, x1=x1, proj=proj, h1b=h1b, qn=qn, kn=kn, v=v,
                          ycat=ycat, lse=lse, y=y, x2=x2, a2=a2, f2=f2))
    sums, totals, g_prev = [None, None], [None, None], None
    w2r = DFF // NCHIP
    for l in (1, 0):
        w, s = weights[l], saved[l]
        vecs = s["vecs"]
        ride = g_prev is not None
        own = l == 0
        mine, other = [None] * 6, [None] * 6

        def half_sum(group, recv, k0):
            return [add_half(g, r, cflag, f"add_sibling_l{l}_{k0 + j}") for j, (g, r) in enumerate(zip(group, recv))]

        def chip_sum(landed, k0):
            return [sum_chips(t, f"sum_chips_l{l}_{k0 + j}") for j, t in enumerate(landed)]

        (dx, hb, dfb, act, da, sums2), got = ffn_bwd(
            dx, s["x2"], s["a2"], s["f2"], vecs[2], w["w1"][1], w["w2"][1], 0.5, f"ffn_bwd_l{l}b",
            carry=Carry("swap_halves", g_prev) if ride else None)
        dw1b, _ = wgrad(hb, da, D, HALF, f"wgrad_w1_l{l}b")
        dw2b, _ = wgrad(act, dfb, HALF, D, f"wgrad_w2_l{l}b")
        if ride:
            wire = [add_half(g_prev[k], got[k], cflag, f"add_sibling_l{l + 1}_{k}") for k in range(6)]
        g_ffn_b = [dw1b, dw2b.reshape(NCHIP, w2r, D)]
        (dyb, dycat, delta, sums_o), got = out_proj_bwd(
            dx, s["y"], s["ycat"], vecs[1], w["wout"], f"out_proj_bwd_l{l}",
            carry=Carry("swap_halves", g_ffn_b) if own else None)
        dwout, _ = wgrad(s["ycat"].astype(MXU_DTYPE), dyb, D // 2, D, f"wgrad_wout_l{l}")
        if own:
            wire_ffn_b = half_sum(g_ffn_b, got, 4)
        dqs, dks, dvs, landed = [], [], [], {}
        for d in DILATIONS:
            carry = None
            if ride and d == 1:
                carry = Carry("scatter", wire[3:])
            if ride and d == 16:
                carry = Carry("scatter", wire[:3])
            if own and d == 4:
                carry = Carry("scatter", wire_ffn_b)
            (dq, dk, dv), landed[d] = attn_bwd(s["qn"], s["kn"], s["v"], dycat, s["lse"], delta, d,
                                               f"attn_bwd_l{l}_d{d}", carry=carry)
            dqs.append(dq)
            dks.append(dk)
            dvs.append(dv)
        if ride:
            tot = [sum_chips(t, f"sum_chips_l{l + 1}_{k}") for k, t in enumerate(list(landed[16]) + list(landed[1]))]
        if own:
            mine[4:6] = chip_sum(landed[4], 4)
        ready = (tot if ride else []) + (mine[4:6] if own else [])
        (dproj, sums_m), got = mixer_mid_bwd(dqs, dks, dvs, s["proj"], dycat, gvecs[l], cws[l], f"mixer_mid_bwd_l{l}",
                                             carry=Carry("swap", ready) if ready else None)
        if ride:
            totals[l + 1] = (tot, list(got[:6]))
        if own:
            other[4:6] = list(got[-2:])
        dwin, _ = wgrad(s["h1b"], dproj, D, INC // NCHIP, f"wgrad_win_l{l}")
        g_mixer = [dwin, dwout.reshape(NCHIP, D // NCHIP, D)]
        (dx, sums1), got = mixer_in_bwd(dx, s["x1"], dproj, vecs[1], w["win"], f"mixer_in_bwd_l{l}",
                                        carry=Carry("swap_halves", g_mixer) if own else None)
        if own:
            wire_mixer = half_sum(g_mixer, got, 2)
        (dx, hb, dfb, act, da, sums0), _ = ffn_bwd(
            dx, s["x0"], s["a0"], s["f0"], vecs[0], w["w1"][0], w["w2"][0], 0.5, f"ffn_bwd_l{l}a")
        dw1a, got = wgrad(hb, da, D, HALF, f"wgrad_w1_l{l}a", carry=Carry("scatter", wire_mixer) if own else None)
        if own:
            mine[2:4] = chip_sum(got, 2)
        dw2a, got = wgrad(act, dfb, HALF, D, f"wgrad_w2_l{l}a", carry=Carry("swap", mine[2:4]) if own else None)
        g_ffn_a = [dw1a, dw2a.reshape(NCHIP, w2r, D)]
        if own:
            other[2:4] = list(got)
            wire_ffn_a = half_sum(g_ffn_a, run_carry(Carry("swap_halves", g_ffn_a), "swap_halves_tail"), 0)
            mine[0:2] = chip_sum(run_carry(Carry("scatter", wire_ffn_a), "scatter_grads_tail"), 0)
            other[0:2] = list(run_carry(Carry("swap", mine[0:2]), "swap_totals_tail"))
            totals[l] = (mine, other)
        g_prev = g_ffn_a + g_mixer + g_ffn_b
        sums[l] = (sums0, sums1, sums_o, sums2, sums_m)
    return loss_blk, dx, totals, sums


def small_all_gather(blk, name):
    m_per, n = blk.shape

    def body(x_ref, out_ref, send_sems, recv_sems, local_sem):
        x, y, c = _here()
        me, sibling = (x, y, c), (x, y, 1 - c)
        chips = [(1 - x, y), (x, 1 - y), (1 - x, 1 - y)]

        def rows(px, py, pc):
            return out_ref.at[pl.ds((4 * px + 2 * py + pc) * m_per, m_per), :]

        def copy(k, block, to, src=None):
            return pltpu.make_async_remote_copy(
                src_ref=rows(*block) if src is None else src, dst_ref=rows(*block),
                send_sem=send_sems.at[k], recv_sem=recv_sems.at[k], device_id=to, device_id_type=MESH)

        mine = pltpu.make_async_copy(x_ref, rows(*me), local_sem)
        mine.start()
        first = [copy(0, me, sibling, src=x_ref)]
        first += [copy(1 + j, me, (*chip, c), src=x_ref) for j, chip in enumerate(chips)]
        for cp in first:
            cp.start()
        passed = [copy(4 + j, (*chip, c), sibling) for j, chip in enumerate(chips)]
        for j, chip in enumerate(chips):
            copy(1 + j, (*chip, c), me).wait_recv()
            passed[j].start()
        copy(0, sibling, me).wait_recv()
        for j, chip in enumerate(chips):
            copy(4 + j, (*chip, 1 - c), me).wait_recv()
        for cp in first + passed:
            cp.wait_send()
        mine.wait()

    return pl.pallas_call(
        body, name=name,
        out_shape=jax.ShapeDtypeStruct((NDEV * m_per, n), blk.dtype),
        in_specs=[pl.BlockSpec(memory_space=pltpu.VMEM)],
        out_specs=pl.BlockSpec(memory_space=pltpu.VMEM),
        scratch_shapes=[pltpu.SemaphoreType.DMA((7,)), pltpu.SemaphoreType.DMA((7,)), pltpu.SemaphoreType.DMA],
        compiler_params=pltpu.CompilerParams(vmem_limit_bytes=VMEM_LIMIT),
    )(blk)


EW_BLOCK_BYTES = 1 << 20


def _ew_rows(rows, cols, refs=8):
    want = max(16, EW_BLOCK_BYTES * (2 if refs <= 4 else 1) // (4 * cols))
    best = None
    for t in range(16, rows + 1, 16):
        if rows % t == 0 and t <= want:
            best = t
    return best if best is not None else rows


def add_half(g, recv, cflag, name):
    pieces, r, cols = g.shape
    r2 = r // 2
    tr = _ew_rows(r2, cols, refs=3)
    nt = r2 // tr

    def body(c_ref, g_ref, r_ref, o_ref):
        o_ref[...] = (g_ref[...] + r_ref[...]).astype(o_ref.dtype)

    half = pl.BlockSpec((None, tr, cols), lambda j, i, c_ref: (j, i, 0))
    return pl.pallas_call(
        body, name=name,
        grid_spec=pltpu.PrefetchScalarGridSpec(
            num_scalar_prefetch=1, grid=(pieces, nt),
            in_specs=[pl.BlockSpec((None, tr, cols), lambda j, i, c_ref: (j, c_ref[0] * nt + i, 0)), half],
            out_specs=half),
        out_shape=jax.ShapeDtypeStruct((pieces, r2, cols), WIRE_DTYPE),
        compiler_params=_params(("arbitrary", "arbitrary")),
    )(cflag, g, recv)


def sum_chips(recv, name):
    _, r, cols = recv.shape
    tr = _ew_rows(r, cols, refs=3)

    def body(r_ref, o_ref):
        acc = r_ref[0].astype(F32)
        for k in range(1, NCHIP):
            acc = acc + r_ref[k].astype(F32)
        o_ref[...] = acc

    return pl.pallas_call(
        body, name=name, grid=(r // tr,),
        in_specs=[pl.BlockSpec((NCHIP, tr, cols), lambda i: (0, i, 0))],
        out_specs=pl.BlockSpec((tr, cols), lambda i: (i, 0)),
        out_shape=jax.ShapeDtypeStruct((r, cols), F32),
        compiler_params=_params(("arbitrary",)),
    )(recv)


def sum_devices(rows8, name):
    def body(r_ref, o_ref):
        acc = r_ref[0:1, :]
        for k in range(1, NDEV):
            acc = acc + r_ref[k:k + 1, :]
        o_ref[...] = jnp.broadcast_to(acc, o_ref.shape)

    return pl.pallas_call(
        body, name=name, out_shape=jax.ShapeDtypeStruct(rows8.shape, F32),
        in_specs=[pl.BlockSpec(memory_space=pltpu.VMEM)], out_specs=pl.BlockSpec(memory_space=pltpu.VMEM),
        compiler_params=pltpu.CompilerParams(vmem_limit_bytes=VMEM_LIMIT),
    )(rows8)


def adamw(w, m, v, srcs, cflag, name, halves=False):
    planes, r, cols = w.shape
    rh = r // 2 if halves else r
    tr = _ew_rows(rh, cols)
    nth = rh // tr
    flat = [a for s in srcs for a in (s if halves else (s,))]
    ns = len(flat)
    per = ns // planes

    def body(c_ref, w_ref, m_ref, v_ref, *rest):
        s_refs, (g_ref, d_ref, mo_ref, vo_ref) = rest[:ns], rest[ns:]
        p, i = pl.program_id(0), pl.program_id(1)
        if halves:
            mine = jnp.logical_not(jnp.logical_xor(i >= nth, c_ref[0] == 1))
            blocks = [jnp.where(mine, s_refs[2 * k][...], s_refs[2 * k + 1][...]) for k in range(planes)]
        else:
            blocks = [s[...] for s in s_refs]
        g = blocks[0]
        for k in range(1, planes):
            g = jnp.where(p == k, blocks[k], g)
        g_ref[...] = g
        m_new = ADAM_B1 * m_ref[...] + (1.0 - ADAM_B1) * g
        v_new = ADAM_B2 * v_ref[...] + (1.0 - ADAM_B2) * (g * g)
        mo_ref[...] = m_new
        vo_ref[...] = v_new
        m_hat = m_new / (1.0 - ADAM_B1 ** ADAM_STEP)
        v_hat = v_new / (1.0 - ADAM_B2 ** ADAM_STEP)
        d_ref[...] = -ADAM_LR * (m_hat / (jnp.sqrt(v_hat) + ADAM_EPS) + ADAM_WD * w_ref[...])

    pt = pl.BlockSpec((None, tr, cols), lambda p, i: (p, i, 0))
    st = [pl.BlockSpec((tr, cols), functools.partial(lambda k, p, i: (jnp.where(p == k, i % nth, 0), 0), j // per))
          for j in range(ns)]
    return pl.pallas_call(
        body, name=name, grid=(planes, r // tr),
        in_specs=[pl.BlockSpec(memory_space=pltpu.SMEM), pt, pt, pt] + st,
        out_specs=[pt] * 4,
        out_shape=[jax.ShapeDtypeStruct(w.shape, F32)] * 4,
        compiler_params=_params(("arbitrary", "arbitrary")),
    )(cflag, w, m, v, *flat)


ADA_COLS = 9 * D // NCHIP


def mod_fwd(c_all, w_ada, b_shard, name):
    def body(c_ref, w_ref, b_ref, o_ref):
        cc = c_ref[...]
        sc = cc * jax.nn.sigmoid(cc)
        o_ref[...] = jnp.dot(sc, w_ref[...], preferred_element_type=F32,
                             precision=lax.Precision.HIGHEST) + b_ref[...]

    return pl.pallas_call(
        body, name=name, grid=(2,),
        in_specs=[pl.BlockSpec((NDEV, D), lambda l: (0, 0)),
                  pl.BlockSpec((None, D, ADA_COLS), lambda l: (l, 0, 0)),
                  pl.BlockSpec((None, 1, ADA_COLS), lambda l: (l, 0, 0))],
        out_specs=pl.BlockSpec((None, NDEV, ADA_COLS), lambda l: (l, 0, 0)),
        out_shape=jax.ShapeDtypeStruct((2, NDEV, ADA_COLS), F32),
        compiler_params=_params(("arbitrary",)),
    )(c_all, w_ada, b_shard.reshape(2, 1, ADA_COLS))


def wada_grad(c_all_t, dmod, name):
    ct = ADA_COLS // 3

    def body(c_ref, d_ref, o_ref):
        cc = c_ref[...]
        sc = cc * jax.nn.sigmoid(cc)
        acc = sc[:, 0:1] * d_ref[0:1, :]
        for b in range(1, NDEV):
            acc = acc + sc[:, b:b + 1] * d_ref[b:b + 1, :]
        o_ref[...] = acc

    return pl.pallas_call(
        body, name=name, grid=(2, 3),
        in_specs=[pl.BlockSpec((D, LANES), lambda l, j: (0, 0)),
                  pl.BlockSpec((None, NDEV, ct), lambda l, j: (l, 0, j))],
        out_specs=pl.BlockSpec((None, D, ct), lambda l, j: (l, 0, j)),
        out_shape=jax.ShapeDtypeStruct((2, D, ADA_COLS), F32),
        compiler_params=_params(("arbitrary", "arbitrary")),
    )(c_all_t, dmod)


def _pad_rows(row, rows=SUBLANES):
    return jnp.concatenate([row[None, :], jnp.zeros((rows - 1, row.shape[0]), row.dtype)], axis=0)


def kernel(x, c, w_ada, b_ada, norm_g, w_in, q_norm_g, k_norm_g, conv_w, conv_b, w_out, ffn_w1, ffn_w2, loss_target, m_w_ada, m_b_ada, m_norm_g, m_w_in, m_q_norm_g, m_k_norm_g, m_conv_w, m_conv_b, m_w_out, m_ffn_w1, m_ffn_w2, v_w_ada, v_b_ada, v_norm_g, v_w_in, v_q_norm_g, v_k_norm_g, v_conv_w, v_conv_b, v_w_out, v_ffn_w1, v_ffn_w2):
    ix, iy, ic = lax.axis_index("x"), lax.axis_index("y"), lax.axis_index("c")
    chip = 2 * ix + iy
    dev = 2 * chip + ic
    cflag = jnp.reshape(ic, (1,)).astype(jnp.int32)
    ngw = norm_g.shape[-1]
    cww = conv_w.shape[-1]

    pack = jnp.concatenate([c[0], norm_g.reshape(-1), conv_w.reshape(-1)])
    got = small_all_gather(_pad_rows(pack), "gather_c_normg_convw")[::SUBLANES]
    c_all = got[:, :D]
    per_chip = got[::2]
    ng_full = jnp.concatenate([per_chip[j, D:D + 6 * ngw].reshape(2, 3, ngw) for j in range(NCHIP)], axis=-1)
    cw_full = jnp.concatenate([per_chip[j, D + 6 * ngw:].reshape(2, 3, cww) for j in range(NCHIP)], axis=-1)

    b_shard = lax.dynamic_slice_in_dim(b_ada, chip * ADA_COLS, ADA_COLS, axis=1)
    mod_blk = mod_fwd(c_all, w_ada, b_shard, "mod_fwd").reshape(2 * NDEV, ADA_COLS)
    mod_all = small_all_gather(mod_blk, "gather_mod").reshape(NDEV, 2, NDEV, ADA_COLS)[::2]
    mod_mine = lax.dynamic_index_in_dim(mod_all, dev, axis=2, keepdims=False)
    mods = [mod_mine[:, l, :].reshape(-1) for l in range(2)]

    shards, gvecs, cws = [], [], []
    for l in range(2):
        shards.append(dict(w1=[ffn_w1[l, i].astype(MXU_DTYPE) for i in range(2)],
                           w2=[ffn_w2[l, i].astype(MXU_DTYPE) for i in range(2)],
                           win=w_in[l].astype(MXU_DTYPE), wout=w_out[l].astype(MXU_DTYPE)))
        gv = jnp.stack([jnp.tile(q_norm_g[l], AW // HD), jnp.tile(k_norm_g[l], AW // HD)])
        gvecs.append(jnp.concatenate([gv, jnp.zeros((SUBLANES - 2, AW), F32)], axis=0))
        cws.append(jnp.concatenate([cw_full[l], conv_b[l][None, :], jnp.zeros((SUBLANES - 4, CW), F32)], axis=0))
    w_first = gather_split([shards[0]["w1"][0], shards[0]["w2"][0]], "gather_first_ffn")

    loss_blk, dx, totals, sums = local_step(x[0], loss_target[0], mods, [ng_full[0], ng_full[1]], gvecs, cws,
                                            shards, w_first, cflag)

    dmods, dngs, dqg, dkg, dcw, dcb = [], [], [], [], [], []
    for l in range(2):
        s0, s1, so, s2, sm = sums[l]
        dmods.append(jnp.concatenate([s0[0], s0[1], s0[3], s1[0], s1[1], so[0], s2[0], s2[1], s2[3]]))
        dngs.append(jnp.concatenate([s0[2], s1[2], s2[2]]))
        dqg.append(sm[0].reshape(AW // HD, HD).sum(0))
        dkg.append(sm[1].reshape(AW // HD, HD).sum(0))
        dcw.append(sm[2:5].reshape(-1))
        dcb.append(sm[5])
    small = jnp.concatenate(dmods + dngs + dqg + dkg + dcw + dcb + [loss_blk[0]])
    small_all = small_all_gather(_pad_rows(small), "gather_small_grads")[::SUBLANES]
    nm = 9 * D
    dmod_all = small_all[:, :2 * nm].reshape(NDEV, 2, NCHIP, ADA_COLS)
    dmod_mine = lax.dynamic_index_in_dim(dmod_all, chip, axis=2, keepdims=False).transpose(1, 0, 2)
    tot = sum_devices(small_all, "sum_small_grads")[0]
    o = 2 * nm
    g_b_ada = tot[:o].reshape(2, nm)
    g_norm_g = lax.dynamic_slice_in_dim(tot[o:o + 6 * D].reshape(2, 3, D), chip * ngw, ngw, axis=2)
    o += 6 * D
    g_qg = tot[o:o + 2 * HD].reshape(2, HD)
    o += 2 * HD
    g_kg = tot[o:o + 2 * HD].reshape(2, HD)
    o += 2 * HD
    g_cw = lax.dynamic_slice_in_dim(tot[o:o + 6 * CW].reshape(2, 3, CW), chip * cww, cww, axis=2)
    o += 6 * CW
    g_cb = tot[o:o + 2 * CW].reshape(2, CW)
    loss = tot[o + 2 * CW]

    c_all_t = jnp.concatenate([c_all.T, jnp.zeros((D, LANES - NDEV), F32)], axis=1)
    g_wada_src = wada_grad(c_all_t, dmod_mine, "wada_grad")

    def halves(k_of_plane):
        return [(totals[l][0][k], totals[l][1][k]) for l, k in k_of_plane]

    r_wada = adamw(w_ada, m_w_ada, v_w_ada, [g_wada_src[0], g_wada_src[1]], cflag, "adamw_w_ada")
    r_win = adamw(w_in, m_w_in, v_w_in, halves([(0, 2), (1, 2)]), cflag, "adamw_w_in", halves=True)
    r_wout = adamw(w_out, m_w_out, v_w_out, halves([(0, 3), (1, 3)]), cflag, "adamw_w_out", halves=True)
    r_w1 = adamw(ffn_w1.reshape(4, D, HALF), m_ffn_w1.reshape(4, D, HALF), v_ffn_w1.reshape(4, D, HALF),
                 halves([(0, 0), (0, 4), (1, 0), (1, 4)]), cflag, "adamw_ffn_w1", halves=True)
    w2r = DFF // NCHIP
    r_w2 = adamw(ffn_w2.reshape(4, w2r, D), m_ffn_w2.reshape(4, w2r, D), v_ffn_w2.reshape(4, w2r, D),
                 halves([(0, 1), (0, 5), (1, 1), (1, 5)]), cflag, "adamw_ffn_w2", halves=True)
    r_w1 = [t.reshape(ffn_w1.shape) for t in r_w1]
    r_w2 = [t.reshape(ffn_w2.shape) for t in r_w2]

    smalls = [("b_ada", b_ada, m_b_ada, v_b_ada, g_b_ada), ("norm_g", norm_g, m_norm_g, v_norm_g, g_norm_g),
              ("q_norm_g", q_norm_g, m_q_norm_g, v_q_norm_g, g_qg), ("k_norm_g", k_norm_g, m_k_norm_g, v_k_norm_g, g_kg),
              ("conv_w", conv_w, m_conv_w, v_conv_w, g_cw), ("conv_b", conv_b, m_conv_b, v_conv_b, g_cb)]
    n_small = sum(t[1].size for t in smalls)
    pad = (-n_small) % (16 * LANES)

    def packed(idx):
        flat = jnp.concatenate([t[idx].reshape(-1) for t in smalls] + [jnp.zeros((pad,), F32)])
        return flat.reshape(-1, LANES)

    r_small = adamw(packed(1)[None], packed(2)[None], packed(3)[None], [packed(4)], cflag, "adamw_small")
    small_out = {}
    o = 0
    for name_, w_, _, _, _ in smalls:
        small_out[name_] = [t.reshape(-1)[o:o + w_.size].reshape(w_.shape) for t in r_small]
        o += w_.size

    res = {"w_ada": r_wada, "w_in": r_win, "w_out": r_wout, "ffn_w1": r_w1, "ffn_w2": r_w2, **small_out}
    order = ["w_ada", "b_ada", "norm_g", "w_in", "q_norm_g", "k_norm_g", "conv_w", "conv_b", "w_out", "ffn_w1", "ffn_w2"]
    outs = [loss, dx[None]]
    for k in range(4):
        outs += [res[nm_][k] for nm_ in order]
    return tuple(outs)
```

```python
import functools

import jax
import jax.numpy as jnp
from jax import lax
from jax.experimental import pallas as pl
from jax.experimental.pallas import tpu as pltpu

F32 = jnp.float32
MXU_DTYPE = jnp.bfloat16
ACT_DTYPE = jnp.bfloat16
WIRE_DTYPE = jnp.bfloat16

D = 1024
HD = 64
AW = 512
CW = 512
DFF = 2816
HALF = DFF // 2
INC = 3 * AW + 3 * CW
NCHIP = 4
NDEV = 8
QBLK = 128
ATTN_QBLOCKS = 16
ATTN_INTERLEAVE = 8
ATTN_CHUNK_ROWS = 4096
DILATIONS = (1, 4, 16)
EPS = 1e-6
NEG = -1e30
LANES = 128
SUBLANES = 8
HALO_ROWS = 16
VMEM_LIMIT = 56 * 1024 * 1024

ADAM_LR = 0.001
ADAM_B1 = 0.9
ADAM_B2 = 0.999
ADAM_EPS = 1e-08
ADAM_WD = 0.01
ADAM_STEP = 10

NT_DIMS = (((1,), (1,)), ((), ()))
TN_DIMS = (((0,), (0,)), ((), ()))


def _params(sem, vmem=VMEM_LIMIT):
    return pltpu.CompilerParams(dimension_semantics=sem, vmem_limit_bytes=vmem)


def _row_tile(n, want):
    t = min(n, want)
    assert n % t == 0
    return t


def _ada(xt, vec_ref):
    ng, sc, sh, gt = vec_ref[0:1, :], vec_ref[1:2, :], vec_ref[2:3, :], vec_ref[3:4, :]
    r = lax.rsqrt(jnp.mean(xt * xt, axis=-1, keepdims=True) + EPS)
    return xt * r, r, ng * (1.0 + sc), ng, sc, sh, gt


def _ada_bwd(dh, xhat, r, gain, ng, sc):
    dshift = jnp.sum(dh, axis=0, keepdims=True)
    dhx = dh * xhat
    dscale = jnp.sum(dhx, axis=0, keepdims=True) * ng
    dng = jnp.sum(dhx, axis=0, keepdims=True) * (1.0 + sc)
    dxhat = dh * gain
    dx = r * (dxhat - xhat * jnp.mean(dxhat * xhat, axis=-1, keepdims=True))
    return dx, dshift, dscale, dng


def _acc_rows(sums_ref, first, rows):
    @pl.when(first)
    def _():
        sums_ref[...] = jnp.zeros_like(sums_ref)
    for k, row in enumerate(rows):
        sums_ref[k:k + 1, :] += row


MESH = pl.DeviceIdType.MESH
ANY = pl.BlockSpec(memory_space=pl.ANY)


def _here():
    return lax.axis_index("x"), lax.axis_index("y"), lax.axis_index("c")


def _ici_copies(src_refs, dst_refs, send_sems, recv_sems, local_sems, scatter):
    x, y, c = _here()
    my_chip = 2 * x + y
    peers = [(1 - x, y), (x, 1 - y), (1 - x, 1 - y)]
    local, out, inc = [], [], []
    for a, (src, dst) in enumerate(zip(src_refs, dst_refs)):
        local.append(pltpu.make_async_copy(src.at[my_chip] if scatter else src, dst.at[my_chip], local_sems.at[a]))
        for j, (px, py) in enumerate(peers):
            sems = dict(send_sem=send_sems.at[3 * a + j], recv_sem=recv_sems.at[3 * a + j],
                        device_id=(px, py, c), device_id_type=MESH)
            out.append(pltpu.make_async_remote_copy(
                src_ref=src.at[2 * px + py] if scatter else src, dst_ref=dst.at[my_chip], **sems))
            inc.append(pltpu.make_async_remote_copy(
                src_ref=src.at[my_chip] if scatter else src, dst_ref=dst.at[2 * px + py], **sems))
    return local, out, inc


def _swap_copies(src_refs, dst_refs, send_sems, recv_sems, halves):
    x, y, c = _here()
    cps = []
    for k, (src, dst) in enumerate(zip(src_refs, dst_refs)):
        if halves:
            r2 = src.shape[1] // 2
            src = src.at[:, pl.ds((1 - c) * r2, r2), :]
        cps.append(pltpu.make_async_remote_copy(
            src_ref=src, dst_ref=dst, send_sem=send_sems.at[k], recv_sem=recv_sems.at[k],
            device_id=(x, y, 1 - c), device_id_type=MESH))
    return cps


class Carry:
    def __init__(self, kind, srcs):
        self.kind, self.srcs, n = kind, list(srcs), len(srcs)
        if kind == "gather":
            shapes = [(NCHIP,) + s.shape for s in srcs]
        elif kind == "swap_halves":
            shapes = [(s.shape[0], s.shape[1] // 2, s.shape[2]) for s in srcs]
        else:
            shapes = [s.shape for s in srcs]
        self.out_shape = [jax.ShapeDtypeStruct(sh, s.dtype) for sh, s in zip(shapes, srcs)]
        dma = pltpu.SemaphoreType.DMA
        self.sems = [dma((3 * n,)), dma((3 * n,)), dma((n,))] if kind in ("gather", "scatter") else [dma((n,)), dma((n,))]

    def start(self, srcs, dsts, sems):
        if self.kind in ("gather", "scatter"):
            local, out, _ = _ici_copies(srcs, dsts, *sems, self.kind == "scatter")
            for cp in local + out:
                cp.start()
        else:
            for cp in _swap_copies(srcs, dsts, *sems, self.kind == "swap_halves"):
                cp.start()

    def wait(self, srcs, dsts, sems):
        if self.kind in ("gather", "scatter"):
            local, out, inc = _ici_copies(srcs, dsts, *sems, self.kind == "scatter")
            for cp in inc:
                cp.wait_recv()
            for cp in out:
                cp.wait_send()
            for cp in local:
                cp.wait()
        else:
            cps = _swap_copies(srcs, dsts, *sems, self.kind == "swap_halves")
            for cp in cps:
                cp.wait_recv()
            for cp in cps:
                cp.wait_send()


def run_carry(carry, name):
    n = len(carry.srcs)

    def body(*refs):
        srcs, dsts, sems = refs[:n], refs[n:2 * n], refs[2 * n:]
        carry.start(srcs, dsts, sems)
        carry.wait(srcs, dsts, sems)

    return pl.pallas_call(body, name=name, out_shape=carry.out_shape, in_specs=[ANY] * n, out_specs=[ANY] * n,
                          scratch_shapes=carry.sems)(*carry.srcs)


def gather_split(srcs, name):
    n = len(srcs)

    def body(*refs):
        src_refs, dst_refs = refs[:n], refs[n:2 * n]
        send_sems, recv_sems, fwd_send, fwd_recv, local_sems = refs[2 * n:]
        x, y, c = _here()
        my_chip = 2 * x + y
        peers = [(1 - x, y), (x, 1 - y), (1 - x, 1 - y)]

        def half(ref, h):
            r2 = ref.shape[0] // 2
            return ref.at[pl.ds(h * r2, r2), :]

        local, out, landed, passed, arriving = [], [], [], [], []
        for a, (src, dst) in enumerate(zip(src_refs, dst_refs)):
            local.append(pltpu.make_async_copy(src, dst.at[my_chip], local_sems.at[a]))
            for j, (px, py) in enumerate(peers):
                k = 3 * a + j
                theirs = dst.at[2 * px + py]
                ici = dict(send_sem=send_sems.at[k], recv_sem=recv_sems.at[k], device_id=(px, py, c), device_id_type=MESH)
                d2d = dict(send_sem=fwd_send.at[k], recv_sem=fwd_recv.at[k], device_id=(x, y, 1 - c), device_id_type=MESH)
                out.append(pltpu.make_async_remote_copy(src_ref=half(src, c), dst_ref=half(dst.at[my_chip], c), **ici))
                landed.append(pltpu.make_async_remote_copy(src_ref=half(src, c), dst_ref=half(theirs, c), **ici))
                passed.append(pltpu.make_async_remote_copy(src_ref=half(theirs, c), dst_ref=half(theirs, c), **d2d))
                arriving.append(pltpu.make_async_remote_copy(src_ref=half(theirs, c), dst_ref=half(theirs, 1 - c), **d2d))
        for cp in local + out:
            cp.start()
        for got, fwd in zip(landed, passed):
            got.wait_recv()
            fwd.start()
        for cp in arriving:
            cp.wait_recv()
        for cp in out + passed:
            cp.wait_send()
        for cp in local:
            cp.wait()

    dma = pltpu.SemaphoreType.DMA
    return pl.pallas_call(
        body, name=name, out_shape=[jax.ShapeDtypeStruct((NCHIP,) + s.shape, s.dtype) for s in srcs],
        in_specs=[ANY] * n, out_specs=[ANY] * n,
        scratch_shapes=[dma((3 * n,)), dma((3 * n,)), dma((3 * n,)), dma((3 * n,)), dma((n,))],
    )(*srcs)


def _pcall(body, name, grid, in_specs, out_specs, out_shape, sem, args, carry=None, scratch=()):
    if carry is None:
        outs = pl.pallas_call(body, name=name, grid=grid, in_specs=in_specs, out_specs=out_specs,
                              out_shape=out_shape, scratch_shapes=list(scratch), compiler_params=_params(sem))(*args)
        return outs, []
    n_in, n_out, nc, ns = len(in_specs), len(out_specs), len(carry.srcs), len(scratch)

    def wrapped(*refs):
        ins, csrc = refs[:n_in], refs[n_in:n_in + nc]
        outs, cdst = refs[n_in + nc:n_in + nc + n_out], refs[n_in + nc + n_out:n_in + 2 * nc + n_out]
        own = refs[n_in + 2 * nc + n_out:n_in + 2 * nc + n_out + ns]
        sems = refs[n_in + 2 * nc + n_out + ns:]
        ids = [pl.program_id(a) for a in range(len(grid))]
        first = functools.reduce(jnp.logical_and, [i == 0 for i in ids])
        last = functools.reduce(jnp.logical_and, [i == g - 1 for i, g in zip(ids, grid)])

        @pl.when(first)
        def _():
            carry.start(csrc, cdst, sems)

        body(*ins, *outs, *own)

        @pl.when(last)
        def _():
            carry.wait(csrc, cdst, sems)

    res = pl.pallas_call(
        wrapped, name=name, grid=grid,
        in_specs=list(in_specs) + [ANY] * nc, out_specs=list(out_specs) + [ANY] * nc,
        out_shape=list(out_shape) + carry.out_shape,
        scratch_shapes=list(scratch) + carry.sems, compiler_params=_params(sem),
    )(*args, *carry.srcs)
    return res[:n_out], res[n_out:]


def ffn_fwd(x, vec, w1p, w2, gs, name, carry=None, target=None):
    S = x.shape[0]
    tm = _row_tile(S, 512)

    def body(x_ref, *refs):
        if target is None:
            vec_ref, w1_ref, w2_ref, xn_ref, a_ref, f_ref = refs
        else:
            t_ref, vec_ref, w1_ref, w2_ref, xn_ref, a_ref, f_ref, l_ref = refs
        xt = x_ref[...]
        xhat, _, gain, _, _, sh, gt = _ada(xt, vec_ref)
        h = (xhat * gain + sh).astype(MXU_DTYPE)
        f = jnp.zeros((tm, D), F32)
        for hf in range(2):
            g = jnp.dot(h, w1_ref[hf], preferred_element_type=F32)
            up = jnp.dot(h, w1_ref[2 + hf], preferred_element_type=F32)
            a_ref[:, hf * HALF:(hf + 1) * HALF] = g.astype(a_ref.dtype)
            a_ref[:, DFF + hf * HALF:DFF + (hf + 1) * HALF] = up.astype(a_ref.dtype)
            act = (g * jax.nn.sigmoid(g) * up).astype(MXU_DTYPE)
            f = f + jnp.dot(act, w2_ref[hf * HALF:(hf + 1) * HALF, :], preferred_element_type=F32)
        f_ref[...] = f.astype(f_ref.dtype)
        xn = xt + (gs * gt) * f
        if target is None:
            xn_ref[...] = xn
        else:
            diff = xn - t_ref[...]
            xn_ref[...] = diff * (1.0 / D)
            part = jnp.sum(jnp.sum(diff * diff, axis=0, keepdims=True), axis=1, keepdims=True) * (0.5 / D)

            @pl.when(pl.program_id(0) == 0)
            def _():
                l_ref[...] = jnp.zeros_like(l_ref)
            l_ref[...] += jnp.broadcast_to(part, l_ref.shape)

    tile = pl.BlockSpec((tm, D), lambda i: (i, 0))
    last = target is not None
    return _pcall(
        body, name, (S // tm,),
        [tile] * (2 if last else 1) + [
            pl.BlockSpec((SUBLANES, D), lambda i: (0, 0)),
            pl.BlockSpec((NCHIP, D, HALF), lambda i: (0, 0, 0), pipeline_mode=pl.Buffered(1)),
            pl.BlockSpec((DFF, D), lambda i: (0, 0), pipeline_mode=pl.Buffered(1))],
        [tile, pl.BlockSpec((tm, 2 * DFF), lambda i: (i, 0)), tile]
        + ([pl.BlockSpec((SUBLANES, LANES), lambda i: (0, 0))] if last else []),
        [jax.ShapeDtypeStruct((S, D), F32),
         jax.ShapeDtypeStruct((S, 2 * DFF), ACT_DTYPE),
         jax.ShapeDtypeStruct((S, D), ACT_DTYPE)]
        + ([jax.ShapeDtypeStruct((SUBLANES, LANES), F32)] if last else []),
        ("arbitrary",), (x, target, vec, w1p, w2) if last else (x, vec, w1p, w2), carry)


def ffn_bwd(dxo, x, a, f, vec, w1p, w2, gs, name, carry=None):
    S = x.shape[0]
    tm = _row_tile(S, 256)

    def body(dxo_ref, x_ref, a_ref, f_ref, vec_ref, w1_ref, w2_ref,
             dxi_ref, hb_ref, dfb_ref, act_ref, da_ref, sums_ref):
        xt = x_ref[...]
        dxo = dxo_ref[...]
        xhat, r, gain, ng, sc, sh, gt = _ada(xt, vec_ref)
        hb_ref[...] = (xhat * gain + sh).astype(hb_ref.dtype)
        dgate = gs * jnp.sum(dxo * f_ref[...].astype(F32), axis=0, keepdims=True)
        df = ((gs * gt) * dxo).astype(MXU_DTYPE)
        dfb_ref[...] = df
        dh = jnp.zeros((tm, D), F32)
        for hf in range(2):
            lo, hi = hf * HALF, (hf + 1) * HALF
            dact = lax.dot_general(df, w2_ref[lo:hi, :], NT_DIMS, preferred_element_type=F32)
            g = a_ref[:, lo:hi].astype(F32)
            up = a_ref[:, DFF + lo:DFF + hi].astype(F32)
            sg = jax.nn.sigmoid(g)
            si = g * sg
            act_ref[:, lo:hi] = (si * up).astype(act_ref.dtype)
            dg = (dact * up * (sg * (1.0 + g * (1.0 - sg)))).astype(MXU_DTYPE)
            dup = (dact * si).astype(MXU_DTYPE)
            da_ref[:, lo:hi] = dg
            da_ref[:, DFF + lo:DFF + hi] = dup
            dh = dh + lax.dot_general(dg, w1_ref[hf], NT_DIMS, preferred_element_type=F32)
            dh = dh + lax.dot_general(dup, w1_ref[2 + hf], NT_DIMS, preferred_element_type=F32)
        dx, dshift, dscale, dng = _ada_bwd(dh, xhat, r, gain, ng, sc)
        dxi_ref[...] = dxo + dx
        _acc_rows(sums_ref, pl.program_id(0) == 0, (dshift, dscale, dng, dgate))

    return _pcall(
        body, name, (S // tm,),
        [pl.BlockSpec((tm, D), lambda i: (i, 0)),
         pl.BlockSpec((tm, D), lambda i: (i, 0)),
         pl.BlockSpec((tm, 2 * DFF), lambda i: (i, 0)),
         pl.BlockSpec((tm, D), lambda i: (i, 0)),
         pl.BlockSpec((SUBLANES, D), lambda i: (0, 0)),
         pl.BlockSpec((NCHIP, D, HALF), lambda i: (0, 0, 0), pipeline_mode=pl.Buffered(1)),
         pl.BlockSpec((DFF, D), lambda i: (0, 0), pipeline_mode=pl.Buffered(1))],
        [pl.BlockSpec((tm, D), lambda i: (i, 0)),
         pl.BlockSpec((tm, D), lambda i: (i, 0)),
         pl.BlockSpec((tm, D), lambda i: (i, 0)),
         pl.BlockSpec((tm, DFF), lambda i: (i, 0)),
         pl.BlockSpec((tm, 2 * DFF), lambda i: (i, 0)),
         pl.BlockSpec((SUBLANES, D), lambda i: (0, 0))],
        [jax.ShapeDtypeStruct((S, D), F32),
         jax.ShapeDtypeStruct((S, D), MXU_DTYPE),
         jax.ShapeDtypeStruct((S, D), MXU_DTYPE),
         jax.ShapeDtypeStruct((S, DFF), MXU_DTYPE),
         jax.ShapeDtypeStruct((S, 2 * DFF), MXU_DTYPE),
         jax.ShapeDtypeStruct((SUBLANES, D), F32)],
        ("arbitrary",), (dxo, x, a, f, vec, w1p, w2), carry)


def wgrad(a, b, kt, nt, name, carry=None):
    T, K = a.shape
    N = b.shape[1]
    pk, pn = K // kt, N // nt
    assert pk == 1 or pn == 1
    tt = _row_tile(T, 2048)
    steps = T // tt

    def body(a_ref, b_ref, o_ref):
        @pl.when(pl.program_id(1) == 0)
        def _():
            o_ref[...] = jnp.zeros_like(o_ref)
        o_ref[...] += lax.dot_general(a_ref[...], b_ref[...], TN_DIMS, preferred_element_type=F32)

    a_map = (lambda p, t: (t, p)) if pk > 1 else (lambda p, t: (t, 0))
    b_map = (lambda p, t: (t, p)) if pn > 1 else (lambda p, t: (t, 0))
    (out,), got = _pcall(
        body, name, (pk * pn, steps),
        [pl.BlockSpec((tt, kt), a_map), pl.BlockSpec((tt, nt), b_map)],
        [pl.BlockSpec((None, kt, nt), lambda p, t: (p, 0, 0))],
        [jax.ShapeDtypeStruct((pk * pn, kt, nt), F32)], ("arbitrary", "arbitrary"), (a, b), carry)
    return out, got


def _head_masks(rows):
    lane = lax.broadcasted_iota(jnp.int32, (rows, LANES), 1)
    return lane < HD


def _pair_stat(x, m_a):
    s_a = jnp.sum(jnp.where(m_a, x, 0.0), axis=1, keepdims=True)
    s_b = jnp.sum(jnp.where(m_a, 0.0, x), axis=1, keepdims=True)
    return s_a, s_b


def mixer_in(x, vec, winp, gvec, name, carry=None):
    S = x.shape[0]
    tm = _row_tile(S, 512)
    pc = INC // NCHIP

    def body(x_ref, vec_ref, w_ref, g_ref, proj_ref, hb_ref, qn_ref, kn_ref, v_ref, qkv_ref):
        xt = x_ref[...]
        xhat, _, gain, _, _, sh, _ = _ada(xt, vec_ref)
        h = (xhat * gain + sh).astype(MXU_DTYPE)
        hb_ref[...] = h
        for j in range(NCHIP):
            piece = jnp.dot(h, w_ref[j], preferred_element_type=F32)
            proj_ref[:, j * pc:(j + 1) * pc] = piece.astype(proj_ref.dtype)
            if (j + 1) * pc <= 3 * AW:
                qkv_ref[:, j * pc:(j + 1) * pc] = piece
        m_a = _head_masks(tm)
        for which, dst in ((0, qn_ref), (1, kn_ref)):
            for p in range(AW // LANES):
                lo = which * AW + p * LANES
                xp = qkv_ref[:, lo:lo + LANES]
                s_a, s_b = _pair_stat(xp * xp, m_a)
                rr = jnp.where(m_a, lax.rsqrt(s_a * (1.0 / HD) + EPS), lax.rsqrt(s_b * (1.0 / HD) + EPS))
                gp = g_ref[which:which + 1, p * LANES:(p + 1) * LANES]
                dst[:, p * LANES:(p + 1) * LANES] = (xp * rr * gp).astype(dst.dtype)
        v_ref[...] = qkv_ref[:, 2 * AW:3 * AW]

    assert 2 * pc == 3 * AW
    return _pcall(
        body, name, (S // tm,),
        [pl.BlockSpec((tm, D), lambda i: (i, 0)),
         pl.BlockSpec((SUBLANES, D), lambda i: (0, 0)),
         pl.BlockSpec((NCHIP, D, pc), lambda i: (0, 0, 0), pipeline_mode=pl.Buffered(1)),
         pl.BlockSpec((SUBLANES, AW), lambda i: (0, 0))],
        [pl.BlockSpec((tm, INC), lambda i: (i, 0)),
         pl.BlockSpec((tm, D), lambda i: (i, 0)),
         pl.BlockSpec((tm, AW), lambda i: (i, 0)),
         pl.BlockSpec((tm, AW), lambda i: (i, 0)),
         pl.BlockSpec((tm, AW), lambda i: (i, 0))],
        [jax.ShapeDtypeStruct((S, INC), ACT_DTYPE),
         jax.ShapeDtypeStruct((S, D), MXU_DTYPE),
         jax.ShapeDtypeStruct((S, AW), F32),
         jax.ShapeDtypeStruct((S, AW), F32),
         jax.ShapeDtypeStruct((S, AW), F32)],
        ("arbitrary",), (x, vec, winp, gvec), carry, scratch=[pltpu.VMEM((tm, 3 * AW), F32)])


def _band_masks(ncol):
    row = lax.broadcasted_iota(jnp.int32, (2 * QBLK, ncol), 0) & (QBLK - 1)
    col = lax.broadcasted_iota(jnp.int32, (2 * QBLK, ncol), 1)
    return row, col


def _stack_heads(t, m_a):
    zero = jnp.zeros_like(t)
    return jnp.concatenate([jnp.where(m_a, t, zero), jnp.where(m_a, zero, t)], axis=0)


class _AttnLayout:
    def __init__(self, d, S):
        self.d, self.S = d, S
        self.qb = max(1, min(ATTN_QBLOCKS, ATTN_CHUNK_ROWS // (QBLK * d)))
        self.nres = d
        self.nchunk = S // (self.qb * QBLK * d)
        self.grid = (AW // LANES, self.nchunk)
        self.unroll = max(1, min(d, ATTN_INTERLEAVE // self.qb))

    def _spec(self, blocks, row_of):
        return pl.BlockSpec((blocks * QBLK * self.d, LANES), lambda hp, j: (row_of(j), hp))

    def cur(self, chunk_of):
        return self._spec(self.qb, chunk_of)

    def prev(self, chunk_of):
        return self._spec(1, lambda j: jnp.maximum(chunk_of(j) * self.qb - 1, 0))

    def idx(self, b, r):
        if self.d == 1:
            return (pl.ds(b * QBLK, QBLK), slice(None))
        return (pl.ds(b * QBLK * self.d + r, QBLK, stride=self.d), slice(None))

    def per_residue(self, fn):
        if self.nres == 1:
            fn(0)
        else:
            def step(it, carry):
                for k in range(self.unroll):
                    fn(it * self.unroll + k)
                return carry
            lax.fori_loop(0, self.nres // self.unroll, step, 0)


def attn_fwd(qn, kn, v, d, name, carry=None):
    S = qn.shape[0]
    lay = _AttnLayout(d, S)
    qb = lay.qb

    def body(q_ref, kc_ref, kp_ref, vc_ref, vp_ref, o_ref, lse_ref):
        i = pl.program_id(1)
        m_a = _head_masks(QBLK)
        row, col = _band_masks(2 * QBLK)
        dist = row + QBLK - col
        band = (dist >= 0) & (dist <= QBLK)
        first = band & ((i > 0) | (col >= QBLK))

        def residue(r):
            kt = [kp_ref[lay.idx(0, r)].astype(MXU_DTYPE)]
            vt = [vp_ref[lay.idx(0, r)].astype(MXU_DTYPE)]
            for b in range(qb):
                kt.append(kc_ref[lay.idx(b, r)].astype(MXU_DTYPE))
                vt.append(vc_ref[lay.idx(b, r)].astype(MXU_DTYPE))
            for b in range(qb):
                rows = lay.idx(b, r)
                q = (q_ref[rows] * (HD ** -0.5)).astype(MXU_DTYPE)
                kcat = jnp.concatenate([kt[b], kt[b + 1]], axis=0)
                vcat = jnp.concatenate([vt[b], vt[b + 1]], axis=0)
                mask = first if b == 0 else band
                s = lax.dot_general(_stack_heads(q, m_a), kcat, NT_DIMS, preferred_element_type=F32)
                s = jnp.where(mask, s, NEG)
                m = jnp.max(s, axis=1, keepdims=True)
                p = jnp.exp(s - m)
                l = jnp.sum(p, axis=1, keepdims=True)
                o = jnp.dot(p.astype(MXU_DTYPE), vcat, preferred_element_type=F32) / l
                lse = jnp.broadcast_to(m + jnp.log(l), (2 * QBLK, LANES))
                o_ref[rows] = jnp.where(m_a, o[:QBLK], o[QBLK:])
                lse_ref[rows] = jnp.where(m_a, lse[:QBLK], lse[QBLK:])

        lay.per_residue(residue)

    cur, prev = lay.cur(lambda j: j), lay.prev(lambda j: j)
    return _pcall(body, name, lay.grid, [cur, cur, prev, cur, prev], [cur, cur],
                  [jax.ShapeDtypeStruct((S, AW), F32)] * 2, ("arbitrary", "arbitrary"), (qn, kn, kn, v, v), carry)


def _both_heads(t, m_a):
    other = pltpu.roll(t, HD, 1)
    return jnp.concatenate([jnp.where(m_a, t, other), jnp.where(m_a, other, t)], axis=0)


def attn_bwd(qn, kn, v, dycat, lse, delta, d, name, carry=None):
    S = qn.shape[0]
    lay = _AttnLayout(d, S)
    qb, nchunk = lay.qb, lay.nchunk

    def body(q_ref, kc_ref, kp_ref, vc_ref, vp_ref, do_ref, lse_ref, dl_ref,
             dq_ref, dk_ref, dv_ref, ck_ref, cv_ref):
        j = pl.program_id(1)
        i = nchunk - 1 - j
        m_a = _head_masks(QBLK)
        row, col = _band_masks(2 * QBLK)
        dist = row + QBLK - col
        band = (dist >= 0) & (dist <= QBLK)
        first = band & ((i > 0) | (col >= QBLK))

        def residue(r):
            def tiles(ref, cast):
                out = [ref[lay.idx(b, r)] for b in range(qb)]
                return [t.astype(MXU_DTYPE) for t in out] if cast else out

            def ktiles(cur_ref, prev_ref):
                return [prev_ref[lay.idx(0, r)].astype(MXU_DTYPE)] + tiles(cur_ref, True)

            qt = [(t * (HD ** -0.5)).astype(MXU_DTYPE) for t in tiles(q_ref, False)]
            dot_ = tiles(do_ref, True)
            lse_t = tiles(lse_ref, False)
            dl_t = tiles(dl_ref, False)
            kt = ktiles(kc_ref, kp_ref)
            vt = ktiles(vc_ref, vp_ref)
            dk_acc = [jnp.zeros((QBLK, LANES), F32) for _ in range(qb)]
            dv_acc = [jnp.zeros((QBLK, LANES), F32) for _ in range(qb)]
            crow = pl.ds(0, QBLK) if lay.nres == 1 else pl.ds(pl.multiple_of(r * QBLK, QBLK), QBLK)
            dk_acc[qb - 1] = jnp.where(j > 0, ck_ref[crow, :], 0.0)
            dv_acc[qb - 1] = jnp.where(j > 0, cv_ref[crow, :], 0.0)
            for x in range(qb):
                kcat = jnp.concatenate([kt[x], kt[x + 1]], axis=0)
                vcat = jnp.concatenate([vt[x], vt[x + 1]], axis=0)
                q2 = _stack_heads(qt[x], m_a)
                do2 = _stack_heads(dot_[x], m_a)
                lse2 = _both_heads(lse_t[x], m_a)
                dl2 = _both_heads(dl_t[x], m_a)
                lse2 = jnp.concatenate([lse2, lse2], axis=1)
                dl2 = jnp.concatenate([dl2, dl2], axis=1)
                s = lax.dot_general(q2, kcat, NT_DIMS, preferred_element_type=F32)
                p = jnp.exp(jnp.where(first if x == 0 else band, s, NEG) - lse2)
                dp = lax.dot_general(do2, vcat, NT_DIMS, preferred_element_type=F32)
                ds = p * (dp - dl2)
                dq = jnp.dot(ds.astype(MXU_DTYPE), kcat, preferred_element_type=F32)
                dq_ref[lay.idx(x, r)] = jnp.where(m_a, dq[:QBLK], dq[QBLK:]) * (HD ** -0.5)
                dk = jnp.dot(ds.T.astype(MXU_DTYPE), q2, preferred_element_type=F32)
                dv = jnp.dot(p.T.astype(MXU_DTYPE), do2, preferred_element_type=F32)
                if x == 0:
                    ck_ref[crow, :] = dk[:QBLK]
                    cv_ref[crow, :] = dv[:QBLK]
                else:
                    dk_acc[x - 1] = dk_acc[x - 1] + dk[:QBLK]
                    dv_acc[x - 1] = dv_acc[x - 1] + dv[:QBLK]
                dk_acc[x] = dk_acc[x] + dk[QBLK:]
                dv_acc[x] = dv_acc[x] + dv[QBLK:]
            for kb in range(qb):
                dk_ref[lay.idx(kb, r)] = dk_acc[kb]
                dv_ref[lay.idx(kb, r)] = dv_acc[kb]

        lay.per_residue(residue)

    cur, prev = lay.cur(lambda j: nchunk - 1 - j), lay.prev(lambda j: nchunk - 1 - j)
    carried = pltpu.VMEM((lay.nres * QBLK, LANES), F32)
    return _pcall(
        body, name, lay.grid, [cur, cur, prev, cur, prev, cur, cur, cur], [cur, cur, cur],
        [jax.ShapeDtypeStruct((S, AW), F32)] * 3, ("arbitrary", "arbitrary"),
        (qn, kn, kn, v, v, dycat, lse, delta), carry, scratch=[carried, carried])


def _shift_down(x, halo_prev, k, row):
    tm = x.shape[0]
    tail = jnp.concatenate([pltpu.roll(halo_prev, k, 0), jnp.zeros((tm - SUBLANES, x.shape[1]), x.dtype)], axis=0)
    return jnp.where(row < k, tail, pltpu.roll(x, k, 0))


def _shift_up(x, halo_next, k, row):
    tm = x.shape[0]
    head = jnp.concatenate([jnp.zeros((tm - SUBLANES, x.shape[1]), x.dtype), pltpu.roll(halo_next, SUBLANES - k, 0)], axis=0)
    return jnp.where(row >= tm - k, head, pltpu.roll(x, tm - k, 0))


def _conv_fwd(cu, halo_cu, cw_ref, row):
    u1 = _shift_down(cu, halo_cu, 1, row)
    u2 = _shift_down(cu, halo_cu, 2, row)
    cv = cw_ref[0:1, :] * u2 + cw_ref[1:2, :] * u1 + cw_ref[2:3, :] * cu + cw_ref[3:4, :]
    return cv, u1, u2


def mixer_out(os_, lses, proj, cw, x, vec, wout, name, carry=None):
    S = proj.shape[0]
    tm = _row_tile(S, 512)
    hb = tm // HALO_ROWS

    def body(o1, o2, o3, l1, l2, l3, pc_ref, ph_ref, cw_ref, x_ref, vec_ref, w_ref, ycat_ref, lse_ref, xn_ref, y_ref):
        i = pl.program_id(0)
        for p in range(AW // LANES):
            cs = slice(p * LANES, (p + 1) * LANES)
            ls = [l[:, cs] for l in (l1, l2, l3)]
            mx = jnp.maximum(jnp.maximum(ls[0], ls[1]), ls[2])
            t = mx + jnp.log(jnp.exp(ls[0] - mx) + jnp.exp(ls[1] - mx) + jnp.exp(ls[2] - mx))
            lse_ref[:, cs] = t
            acc = jnp.zeros((tm, LANES), F32)
            for l, o in zip(ls, (o1, o2, o3)):
                acc = acc + jnp.exp(l - t) * o[:, cs]
            ycat_ref[:, cs] = acc.astype(ycat_ref.dtype)
        row = lax.broadcasted_iota(jnp.int32, (tm, CW), 0)
        gb, gc, u = (pc_ref[:, k * CW:(k + 1) * CW].astype(F32) for k in range(3))
        ph = ph_ref[...].astype(F32)[HALO_ROWS - SUBLANES:]
        halo_cu = jnp.where(i > 0, ph[:, CW:2 * CW] * ph[:, 2 * CW:3 * CW], 0.0)
        cv, _, _ = _conv_fwd(gc * u, halo_cu, cw_ref, row)
        ycat_ref[:, AW:AW + CW] = (gb * cv).astype(ycat_ref.dtype)
        y = jnp.dot(ycat_ref[...].astype(MXU_DTYPE), w_ref[...], preferred_element_type=F32)
        xn_ref[...] = x_ref[...] + vec_ref[3:4, :] * y
        y_ref[...] = y.astype(y_ref.dtype)

    ot = pl.BlockSpec((tm, AW), lambda i: (i, 0))
    t = pl.BlockSpec((tm, D), lambda i: (i, 0))
    return _pcall(
        body, name, (S // tm,),
        [ot] * 6 + [pl.BlockSpec((tm, 3 * CW), lambda i: (i, 1)),
                    pl.BlockSpec((HALO_ROWS, 3 * CW), lambda i: (jnp.maximum(i * hb - 1, 0), 1)),
                    pl.BlockSpec((SUBLANES, CW), lambda i: (0, 0)),
                    t, pl.BlockSpec((SUBLANES, D), lambda i: (0, 0)), pl.BlockSpec((D, D), lambda i: (0, 0))],
        [t, ot, t, t],
        [jax.ShapeDtypeStruct((S, D), ACT_DTYPE), jax.ShapeDtypeStruct((S, AW), F32),
         jax.ShapeDtypeStruct((S, D), F32), jax.ShapeDtypeStruct((S, D), ACT_DTYPE)],
        ("arbitrary",), (*os_, *lses, proj, proj, cw, x, vec, wout), carry)


def out_proj_bwd(dxo, y, ycat, vec, wout, name, carry=None):
    S = dxo.shape[0]
    tm = _row_tile(S, 512)

    def body(dxo_ref, y_ref, yc_ref, vec_ref, w_ref, dyb_ref, dyc_ref, dl_ref, sums_ref):
        dxo = dxo_ref[...]
        dgate = jnp.sum(dxo * y_ref[...].astype(F32), axis=0, keepdims=True)
        dy = (vec_ref[3:4, :] * dxo).astype(MXU_DTYPE)
        dyb_ref[...] = dy
        dyc_ref[...] = lax.dot_general(dy, w_ref[...], NT_DIMS, preferred_element_type=F32)
        m_a = _head_masks(tm)
        for p in range(AW // LANES):
            cs = slice(p * LANES, (p + 1) * LANES)
            s_a, s_b = _pair_stat(dyc_ref[:, cs] * yc_ref[:, cs].astype(F32), m_a)
            dl_ref[:, cs] = jnp.where(m_a, s_a, s_b)
        _acc_rows(sums_ref, pl.program_id(0) == 0, (dgate,))

    t = pl.BlockSpec((tm, D), lambda i: (i, 0))
    at = pl.BlockSpec((tm, AW), lambda i: (i, 0))
    return _pcall(
        body, name, (S // tm,),
        [t, t, t, pl.BlockSpec((SUBLANES, D), lambda i: (0, 0)), pl.BlockSpec((D, D), lambda i: (0, 0))],
        [t, t, at, pl.BlockSpec((SUBLANES, D), lambda i: (0, 0))],
        [jax.ShapeDtypeStruct((S, D), MXU_DTYPE), jax.ShapeDtypeStruct((S, D), F32),
         jax.ShapeDtypeStruct((S, AW), F32), jax.ShapeDtypeStruct((SUBLANES, D), F32)],
        ("arbitrary",), (dxo, y, ycat, vec, wout), carry)


def mixer_mid_bwd(dqs, dks, dvs, proj, dycat, gvec, cw, name, carry=None):
    S = proj.shape[0]
    tm = _row_tile(S, 512)
    hb = tm // SUBLANES
    hp = tm // HALO_ROWS
    nsl = S // SUBLANES
    ntile = S // tm

    def body(dq1, dq2, dq3, dk1, dk2, dk3, dv1, dv2, dv3, pr_ref, pp_ref, pn_ref, dyc_ref, dyn_ref,
             g_ref, cw_ref, dp_ref, sums_ref):
        i = pl.program_id(0)
        m_a = _head_masks(tm)
        gsum = []
        for which, parts in ((0, (dq1, dq2, dq3)), (1, (dk1, dk2, dk3))):
            acc_g = []
            for p in range(AW // LANES):
                lo = which * AW + p * LANES
                cs = slice(p * LANES, (p + 1) * LANES)
                xp = pr_ref[:, lo:lo + LANES].astype(F32)
                s_a, s_b = _pair_stat(xp * xp, m_a)
                rr = jnp.where(m_a, lax.rsqrt(s_a * (1.0 / HD) + EPS), lax.rsqrt(s_b * (1.0 / HD) + EPS))
                xh = xp * rr
                dn = parts[0][:, cs] + parts[1][:, cs] + parts[2][:, cs]
                acc_g.append(jnp.sum(dn * xh, axis=0, keepdims=True))
                t = dn * g_ref[which:which + 1, cs]
                t_a, t_b = _pair_stat(t * xh, m_a)
                mean = jnp.where(m_a, t_a, t_b) * (1.0 / HD)
                dp_ref[:, lo:lo + LANES] = (rr * (t - xh * mean)).astype(dp_ref.dtype)
            gsum.append(jnp.concatenate(acc_g, axis=1))
        dp_ref[:, 2 * AW:3 * AW] = (dv1[...] + dv2[...] + dv3[...]).astype(dp_ref.dtype)
        row = lax.broadcasted_iota(jnp.int32, (tm, CW), 0)
        base = 3 * AW
        gb, gc, u = (pr_ref[:, base + k * CW:base + (k + 1) * CW].astype(F32) for k in range(3))
        cu = gc * u
        pp = pp_ref[...].astype(F32)[HALO_ROWS - SUBLANES:]
        halo_cu = jnp.where(i > 0, pp[:, CW:2 * CW] * pp[:, 2 * CW:3 * CW], 0.0)
        cv, u1, u2 = _conv_fwd(cu, halo_cu, cw_ref, row)
        dyc = dyc_ref[...]
        dp_ref[:, base:base + CW] = (dyc * cv).astype(dp_ref.dtype)
        dcv = dyc * gb
        gb_next = pn_ref[:, 0:CW].astype(F32)[:SUBLANES]
        halo_dcv = jnp.where(i < ntile - 1, dyn_ref[...] * gb_next, 0.0)
        d1 = _shift_up(dcv, halo_dcv, 1, row)
        d2 = _shift_up(dcv, halo_dcv, 2, row)
        dcu = cw_ref[2:3, :] * dcv + cw_ref[1:2, :] * d1 + cw_ref[0:1, :] * d2
        dp_ref[:, base + CW:base + 2 * CW] = (dcu * u).astype(dp_ref.dtype)
        dp_ref[:, base + 2 * CW:base + 3 * CW] = (dcu * gc).astype(dp_ref.dtype)
        rows = (gsum[0], gsum[1],
                jnp.sum(dcv * u2, axis=0, keepdims=True), jnp.sum(dcv * u1, axis=0, keepdims=True),
                jnp.sum(dcv * cu, axis=0, keepdims=True), jnp.sum(dcv, axis=0, keepdims=True))
        _acc_rows(sums_ref, i == 0, rows)

    at = pl.BlockSpec((tm, AW), lambda i: (i, 0))
    return _pcall(
        body, name, (ntile,),
        [at] * 9 + [
            pl.BlockSpec((tm, INC), lambda i: (i, 0)),
            pl.BlockSpec((HALO_ROWS, 3 * CW), lambda i: (jnp.maximum(i * hp - 1, 0), 1)),
            pl.BlockSpec((HALO_ROWS, 3 * CW), lambda i: (jnp.minimum((i + 1) * hp, S // HALO_ROWS - 1), 1)),
            pl.BlockSpec((tm, CW), lambda i: (i, 1)),
            pl.BlockSpec((SUBLANES, CW), lambda i: (jnp.minimum((i + 1) * hb, nsl - 1), 1)),
            pl.BlockSpec((SUBLANES, AW), lambda i: (0, 0)),
            pl.BlockSpec((SUBLANES, CW), lambda i: (0, 0))],
        [pl.BlockSpec((tm, INC), lambda i: (i, 0)), pl.BlockSpec((SUBLANES, AW), lambda i: (0, 0))],
        [jax.ShapeDtypeStruct((S, INC), MXU_DTYPE), jax.ShapeDtypeStruct((SUBLANES, AW), F32)],
        ("arbitrary",), (*dqs, *dks, *dvs, proj, proj, proj, dycat, dycat, gvec, cw), carry)


def mixer_in_bwd(dxo, x, dproj, vec, winp, name, carry=None):
    S = x.shape[0]
    tm = _row_tile(S, 512)
    pc = INC // NCHIP

    def body(dxo_ref, x_ref, dp_ref, vec_ref, w_ref, dxi_ref, sums_ref):
        xhat, r, gain, ng, sc, _, _ = _ada(x_ref[...], vec_ref)
        dh = jnp.zeros((tm, D), F32)
        for j in range(NCHIP):
            dh = dh + lax.dot_general(dp_ref[:, j * pc:(j + 1) * pc], w_ref[j], NT_DIMS, preferred_element_type=F32)
        dx, dshift, dscale, dng = _ada_bwd(dh, xhat, r, gain, ng, sc)
        dxi_ref[...] = dxo_ref[...] + dx
        _acc_rows(sums_ref, pl.program_id(0) == 0, (dshift, dscale, dng))

    t = pl.BlockSpec((tm, D), lambda i: (i, 0))
    return _pcall(
        body, name, (S // tm,),
        [t, t, pl.BlockSpec((tm, INC), lambda i: (i, 0)),
         pl.BlockSpec((SUBLANES, D), lambda i: (0, 0)),
         pl.BlockSpec((NCHIP, D, pc), lambda i: (0, 0, 0), pipeline_mode=pl.Buffered(1))],
        [t, pl.BlockSpec((SUBLANES, D), lambda i: (0, 0))],
        [jax.ShapeDtypeStruct((S, D), F32), jax.ShapeDtypeStruct((SUBLANES, D), F32)],
        ("arbitrary",), (dxo, x, dproj, vec, winp), carry)


def _vec(mod_l, ng_l, i):
    m = mod_l.reshape(3, 3, D)
    rows = jnp.stack([ng_l[i], m[i, 1], m[i, 0], m[i, 2]])
    return jnp.concatenate([rows, jnp.zeros((SUBLANES - 4, D), F32)], axis=0)


def local_step(x, target, mods, ngs, gvecs, cws, shards, w_first, cflag):
    saved = []
    weights = [dict(w1=[None, None], w2=[None, None]) for _ in range(2)]
    weights[0]["w1"][0], weights[0]["w2"][0] = w_first[0], w_first[1].reshape(DFF, D)
    h = x
    for l in range(2):
        w, sh = weights[l], shards[l]
        nxt = shards[l + 1] if l == 0 else None
        vecs = [_vec(mods[l], ngs[l], i) for i in range(3)]
        x0 = h
        (x1, a0, f0), (win, wout, w2b) = ffn_fwd(x0, vecs[0], w["w1"][0], w["w2"][0], 0.5, f"ffn_fwd_l{l}a",
                                                 carry=Carry("gather", [sh["win"], sh["wout"], sh["w2"][1]]))
        w["win"], w["wout"], w["w2"][1] = win, wout.reshape(D, D), w2b.reshape(DFF, D)
        quarter = [sh["w1"][1][k * (D // 4):(k + 1) * (D // 4)] for k in range(4)]
        (proj, h1b, qn, kn, v), w1b = mixer_in(x1, vecs[1], w["win"], gvecs[l], f"mixer_in_l{l}",
                                               carry=Carry("gather", [quarter[0]]))
        os_, lses, w1b = [], [], list(w1b)
        for k, d in enumerate(DILATIONS):
            (o, lse_d), got = attn_fwd(qn, kn, v, d, f"attn_fwd_l{l}_d{d}", carry=Carry("gather", [quarter[1 + k]]))
            w1b.append(got[0])
            os_.append(o)
            lses.append(lse_d)
        w["w1"][1] = jnp.concatenate(w1b, axis=1)
        (ycat, lse, x2, y), got = mixer_out(os_, lses, proj, cws[l], x1, vecs[1], w["wout"], f"mixer_out_l{l}",
                                            carry=Carry("gather", [nxt["w2"][0]]) if nxt else None)
        if nxt:
            weights[1]["w2"][0] = got[0].reshape(DFF, D)
        if nxt:
            (h, a2, f2), got = ffn_fwd(x2, vecs[2], w["w1"][1], w["w2"][1], 0.5, f"ffn_fwd_l{l}b",
                                       carry=Carry("gather", [nxt["w1"][0]]))
            weights[1]["w1"][0] = got[0]
        else:
            (dx, a2, f2, loss_blk), _ = ffn_fwd(x2, vecs[2], w["w1"][1], w["w2"][1], 0.5, f"ffn_fwd_l{l}b",
                                                target=target)
        saved.append(dict(vecs=vecs, x0=x0, a0=a0, f0=f0, x1=x1, proj=proj, h1b=h1b, qn=qn, kn=kn, v=v,
                          ycat=ycat, lse=lse, y=y, x2=x2, a2=a2, f2=f2))
    sums, totals, g_prev = [None, None], [None, None], None
    w2r = DFF // NCHIP
    for l in (1, 0):
        w, s = weights[l], saved[l]
        vecs = s["vecs"]
        ride = g_prev is not None
        own = l == 0
        mine, other = [None] * 6, [None] * 6

        def half_sum(group, recv, k0):
            return [add_half(g, r, cflag, f"add_sibling_l{l}_{k0 + j}") for j, (g, r) in enumerate(zip(group, recv))]

        def chip_sum(landed, k0):
            return [sum_chips(t, f"sum_chips_l{l}_{k0 + j}") for j, t in enumerate(landed)]

        (dx, hb, dfb, act, da, sums2), got = ffn_bwd(
            dx, s["x2"], s["a2"], s["f2"], vecs[2], w["w1"][1], w["w2"][1], 0.5, f"ffn_bwd_l{l}b",
            carry=Carry("swap_halves", g_prev) if ride else None)
        dw1b, _ = wgrad(hb, da, D, HALF, f"wgrad_w1_l{l}b")
        dw2b, _ = wgrad(act, dfb, HALF, D, f"wgrad_w2_l{l}b")
        if ride:
            wire = [add_half(g_prev[k], got[k], cflag, f"add_sibling_l{l + 1}_{k}") for k in range(6)]
        g_ffn_b = [dw1b, dw2b.reshape(NCHIP, w2r, D)]
        (dyb, dycat, delta, sums_o), got = out_proj_bwd(
            dx, s["y"], s["ycat"], vecs[1], w["wout"], f"out_proj_bwd_l{l}",
            carry=Carry("swap_halves", g_ffn_b) if own else None)
        dwout, _ = wgrad(s["ycat"].astype(MXU_DTYPE), dyb, D // 2, D, f"wgrad_wout_l{l}")
        if own:
            wire_ffn_b = half_sum(g_ffn_b, got, 4)
        dqs, dks, dvs, landed = [], [], [], {}
        for d in DILATIONS:
            carry = None
            if ride and d == 1:
                carry = Carry("scatter", wire[3:])
            if ride and d == 16:
                carry = Carry("scatter", wire[:3])
            if own and d == 4:
                carry = Carry("scatter", wire_ffn_b)
            (dq, dk, dv), landed[d] = attn_bwd(s["qn"], s["kn"], s["v"], dycat, s["lse"], delta, d,
                                               f"attn_bwd_l{l}_d{d}", carry=carry)
            dqs.append(dq)
            dks.append(dk)
            dvs.append(dv)
        if ride:
            tot = [sum_chips(t, f"sum_chips_l{l + 1}_{k}") for k, t in enumerate(list(landed[16]) + list(landed[1]))]
        if own:
            mine[4:6] = chip_sum(landed[4], 4)
        ready = (tot if ride else []) + (mine[4:6] if own else [])
        (dproj, sums_m), got = mixer_mid_bwd(dqs, dks, dvs, s["proj"], dycat, gvecs[l], cws[l], f"mixer_mid_bwd_l{l}",
                                             carry=Carry("swap", ready) if ready else None)
        if ride:
            totals[l + 1] = (tot, list(got[:6]))
        if own:
            other[4:6] = list(got[-2:])
        dwin, _ = wgrad(s["h1b"], dproj, D, INC // NCHIP, f"wgrad_win_l{l}")
        g_mixer = [dwin, dwout.reshape(NCHIP, D // NCHIP, D)]
        (dx, sums1), got = mixer_in_bwd(dx, s["x1"], dproj, vecs[1], w["win"], f"mixer_in_bwd_l{l}",
                                        carry=Carry("swap_halves", g_mixer) if own else None)
        if own:
            wire_mixer = half_sum(g_mixer, got, 2)
        (dx, hb, dfb, act, da, sums0), _ = ffn_bwd(
            dx, s["x0"], s["a0"], s["f0"], vecs[0], w["w1"][0], w["w2"][0], 0.5, f"ffn_bwd_l{l}a")
        dw1a, got = wgrad(hb, da, D, HALF, f"wgrad_w1_l{l}a", carry=Carry("scatter", wire_mixer) if own else None)
        if own:
            mine[2:4] = chip_sum(got, 2)
        dw2a, got = wgrad(act, dfb, HALF, D, f"wgrad_w2_l{l}a", carry=Carry("swap", mine[2:4]) if own else None)
        g_ffn_a = [dw1a, dw2a.reshape(NCHIP, w2r, D)]
        if own:
            other[2:4] = list(got)
            wire_ffn_a = half_sum(g_ffn_a, run_carry(Carry("swap_halves", g_ffn_a), "swap_halves_tail"), 0)
            mine[0:2] = chip_sum(run_carry(Carry("scatter", wire_ffn_a), "scatter_grads_tail"), 0)
            other[0:2] = list(run_carry(Carry("swap", mine[0:2]), "swap_totals_tail"))
            totals[l] = (mine, other)
        g_prev = g_ffn_a + g_mixer + g_ffn_b
        sums[l] = (sums0, sums1, sums_o, sums2, sums_m)
    return loss_blk, dx, totals, sums


def small_all_gather(blk, name):
    m_per, n = blk.shape

    def body(x_ref, out_ref, send_sems, recv_sems, local_sem):
        x, y, c = _here()
        me, sibling = (x, y, c), (x, y, 1 - c)
        chips = [(1 - x, y), (x, 1 - y), (1 - x, 1 - y)]

        def rows(px, py, pc):
            return out_ref.at[pl.ds((4 * px + 2 * py + pc) * m_per, m_per), :]

        def copy(k, block, to, src=None):
            return pltpu.make_async_remote_copy(
                src_ref=rows(*block) if src is None else src, dst_ref=rows(*block),
                send_sem=send_sems.at[k], recv_sem=recv_sems.at[k], device_id=to, device_id_type=MESH)

        mine = pltpu.make_async_copy(x_ref, rows(*me), local_sem)
        mine.start()
        first = [copy(0, me, sibling, src=x_ref)]
        first += [copy(1 + j, me, (*chip, c), src=x_ref) for j, chip in enumerate(chips)]
        for cp in first:
            cp.start()
        passed = [copy(4 + j, (*chip, c), sibling) for j, chip in enumerate(chips)]
        for j, chip in enumerate(chips):
            copy(1 + j, (*chip, c), me).wait_recv()
            passed[j].start()
        copy(0, sibling, me).wait_recv()
        for j, chip in enumerate(chips):
            copy(4 + j, (*chip, 1 - c), me).wait_recv()
        for cp in first + passed:
            cp.wait_send()
        mine.wait()

    return pl.pallas_call(
        body, name=name,
        out_shape=jax.ShapeDtypeStruct((NDEV * m_per, n), blk.dtype),
        in_specs=[pl.BlockSpec(memory_space=pltpu.VMEM)],
        out_specs=pl.BlockSpec(memory_space=pltpu.VMEM),
        scratch_shapes=[pltpu.SemaphoreType.DMA((7,)), pltpu.SemaphoreType.DMA((7,)), pltpu.SemaphoreType.DMA],
        compiler_params=pltpu.CompilerParams(vmem_limit_bytes=VMEM_LIMIT),
    )(blk)


EW_BLOCK_BYTES = 1 << 20


def _ew_rows(rows, cols, refs=8):
    want = max(16, EW_BLOCK_BYTES * (2 if refs <= 4 else 1) // (4 * cols))
    best = None
    for t in range(16, rows + 1, 16):
        if rows % t == 0 and t <= want:
            best = t
    return best if best is not None else rows


def add_half(g, recv, cflag, name):
    pieces, r, cols = g.shape
    r2 = r // 2
    tr = _ew_rows(r2, cols, refs=3)
    nt = r2 // tr

    def body(c_ref, g_ref, r_ref, o_ref):
        o_ref[...] = (g_ref[...] + r_ref[...]).astype(o_ref.dtype)

    half = pl.BlockSpec((None, tr, cols), lambda j, i, c_ref: (j, i, 0))
    return pl.pallas_call(
        body, name=name,
        grid_spec=pltpu.PrefetchScalarGridSpec(
            num_scalar_prefetch=1, grid=(pieces, nt),
            in_specs=[pl.BlockSpec((None, tr, cols), lambda j, i, c_ref: (j, c_ref[0] * nt + i, 0)), half],
            out_specs=half),
        out_shape=jax.ShapeDtypeStruct((pieces, r2, cols), WIRE_DTYPE),
        compiler_params=_params(("arbitrary", "arbitrary")),
    )(cflag, g, recv)


def sum_chips(recv, name):
    _, r, cols = recv.shape
    tr = _ew_rows(r, cols, refs=3)

    def body(r_ref, o_ref):
        acc = r_ref[0].astype(F32)
        for k in range(1, NCHIP):
            acc = acc + r_ref[k].astype(F32)
        o_ref[...] = acc

    return pl.pallas_call(
        body, name=name, grid=(r // tr,),
        in_specs=[pl.BlockSpec((NCHIP, tr, cols), lambda i: (0, i, 0))],
        out_specs=pl.BlockSpec((tr, cols), lambda i: (i, 0)),
        out_shape=jax.ShapeDtypeStruct((r, cols), F32),
        compiler_params=_params(("arbitrary",)),
    )(recv)


def sum_devices(rows8, name):
    def body(r_ref, o_ref):
        acc = r_ref[0:1, :]
        for k in range(1, NDEV):
            acc = acc + r_ref[k:k + 1, :]
        o_ref[...] = jnp.broadcast_to(acc, o_ref.shape)

    return pl.pallas_call(
        body, name=name, out_shape=jax.ShapeDtypeStruct(rows8.shape, F32),
        in_specs=[pl.BlockSpec(memory_space=pltpu.VMEM)], out_specs=pl.BlockSpec(memory_space=pltpu.VMEM),
        compiler_params=pltpu.CompilerParams(vmem_limit_bytes=VMEM_LIMIT),
    )(rows8)


def adamw(w, m, v, srcs, cflag, name, halves=False):
    planes, r, cols = w.shape
    rh = r // 2 if halves else r
    tr = _ew_rows(rh, cols)
    nth = rh // tr
    flat = [a for s in srcs for a in (s if halves else (s,))]
    ns = len(flat)
    per = ns // planes

    def body(c_ref, w_ref, m_ref, v_ref, *rest):
        s_refs, (g_ref, d_ref, mo_ref, vo_ref) = rest[:ns], rest[ns:]
        p, i = pl.program_id(0), pl.program_id(1)
        if halves:
            mine = jnp.logical_not(jnp.logical_xor(i >= nth, c_ref[0] == 1))
            blocks = [jnp.where(mine, s_refs[2 * k][...], s_refs[2 * k + 1][...]) for k in range(planes)]
        else:
            blocks = [s[...] for s in s_refs]
        g = blocks[0]
        for k in range(1, planes):
            g = jnp.where(p == k, blocks[k], g)
        g_ref[...] = g
        m_new = ADAM_B1 * m_ref[...] + (1.0 - ADAM_B1) * g
        v_new = ADAM_B2 * v_ref[...] + (1.0 - ADAM_B2) * (g * g)
        mo_ref[...] = m_new
        vo_ref[...] = v_new
        m_hat = m_new / (1.0 - ADAM_B1 ** ADAM_STEP)
        v_hat = v_new / (1.0 - ADAM_B2 ** ADAM_STEP)
        d_ref[...] = -ADAM_LR * (m_hat / (jnp.sqrt(v_hat) + ADAM_EPS) + ADAM_WD * w_ref[...])

    pt = pl.BlockSpec((None, tr, cols), lambda p, i: (p, i, 0))
    st = [pl.BlockSpec((tr, cols), functools.partial(lambda k, p, i: (jnp.where(p == k, i % nth, 0), 0), j // per))
          for j in range(ns)]
    return pl.pallas_call(
        body, name=name, grid=(planes, r // tr),
        in_specs=[pl.BlockSpec(memory_space=pltpu.SMEM), pt, pt, pt] + st,
        out_specs=[pt] * 4,
        out_shape=[jax.ShapeDtypeStruct(w.shape, F32)] * 4,
        compiler_params=_params(("arbitrary", "arbitrary")),
    )(cflag, w, m, v, *flat)


ADA_COLS = 9 * D // NCHIP


def mod_fwd(c_all, w_ada, b_shard, name):
    def body(c_ref, w_ref, b_ref, o_ref):
        cc = c_ref[...]
        sc = cc * jax.nn.sigmoid(cc)
        o_ref[...] = jnp.dot(sc, w_ref[...], preferred_element_type=F32,
                             precision=lax.Precision.HIGHEST) + b_ref[...]

    return pl.pallas_call(
        body, name=name, grid=(2,),
        in_specs=[pl.BlockSpec((NDEV, D), lambda l: (0, 0)),
                  pl.BlockSpec((None, D, ADA_COLS), lambda l: (l, 0, 0)),
                  pl.BlockSpec((None, 1, ADA_COLS), lambda l: (l, 0, 0))],
        out_specs=pl.BlockSpec((None, NDEV, ADA_COLS), lambda l: (l, 0, 0)),
        out_shape=jax.ShapeDtypeStruct((2, NDEV, ADA_COLS), F32),
        compiler_params=_params(("arbitrary",)),
    )(c_all, w_ada, b_shard.reshape(2, 1, ADA_COLS))


def wada_grad(c_all_t, dmod, name):
    ct = ADA_COLS // 3

    def body(c_ref, d_ref, o_ref):
        cc = c_ref[...]
        sc = cc * jax.nn.sigmoid(cc)
        acc = sc[:, 0:1] * d_ref[0:1, :]
        for b in range(1, NDEV):
            acc = acc + sc[:, b:b + 1] * d_ref[b:b + 1, :]
        o_ref[...] = acc

    return pl.pallas_call(
        body, name=name, grid=(2, 3),
        in_specs=[pl.BlockSpec((D, LANES), lambda l, j: (0, 0)),
                  pl.BlockSpec((None, NDEV, ct), lambda l, j: (l, 0, j))],
        out_specs=pl.BlockSpec((None, D, ct), lambda l, j: (l, 0, j)),
        out_shape=jax.ShapeDtypeStruct((2, D, ADA_COLS), F32),
        compiler_params=_params(("arbitrary", "arbitrary")),
    )(c_all_t, dmod)


def _pad_rows(row, rows=SUBLANES):
    return jnp.concatenate([row[None, :], jnp.zeros((rows - 1, row.shape[0]), row.dtype)], axis=0)


def kernel(x, c, w_ada, b_ada, norm_g, w_in, q_norm_g, k_norm_g, conv_w, conv_b, w_out, ffn_w1, ffn_w2, loss_target, m_w_ada, m_b_ada, m_norm_g, m_w_in, m_q_norm_g, m_k_norm_g, m_conv_w, m_conv_b, m_w_out, m_ffn_w1, m_ffn_w2, v_w_ada, v_b_ada, v_norm_g, v_w_in, v_q_norm_g, v_k_norm_g, v_conv_w, v_conv_b, v_w_out, v_ffn_w1, v_ffn_w2):
    ix, iy, ic = lax.axis_index("x"), lax.axis_index("y"), lax.axis_index("c")
    chip = 2 * ix + iy
    dev = 2 * chip + ic
    cflag = jnp.reshape(ic, (1,)).astype(jnp.int32)
    ngw = norm_g.shape[-1]
    cww = conv_w.shape[-1]

    pack = jnp.concatenate([c[0], norm_g.reshape(-1), conv_w.reshape(-1)])
    got = small_all_gather(_pad_rows(pack), "gather_c_normg_convw")[::SUBLANES]
    c_all = got[:, :D]
    per_chip = got[::2]
    ng_full = jnp.concatenate([per_chip[j, D:D + 6 * ngw].reshape(2, 3, ngw) for j in range(NCHIP)], axis=-1)
    cw_full = jnp.concatenate([per_chip[j, D + 6 * ngw:].reshape(2, 3, cww) for j in range(NCHIP)], axis=-1)

    b_shard = lax.dynamic_slice_in_dim(b_ada, chip * ADA_COLS, ADA_COLS, axis=1)
    mod_blk = mod_fwd(c_all, w_ada, b_shard, "mod_fwd").reshape(2 * NDEV, ADA_COLS)
    mod_all = small_all_gather(mod_blk, "gather_mod").reshape(NDEV, 2, NDEV, ADA_COLS)[::2]
    mod_mine = lax.dynamic_index_in_dim(mod_all, dev, axis=2, keepdims=False)
    mods = [mod_mine[:, l, :].reshape(-1) for l in range(2)]

    shards, gvecs, cws = [], [], []
    for l in range(2):
        shards.append(dict(w1=[ffn_w1[l, i].astype(MXU_DTYPE) for i in range(2)],
                           w2=[ffn_w2[l, i].astype(MXU_DTYPE) for i in range(2)],
                           win=w_in[l].astype(MXU_DTYPE), wout=w_out[l].astype(MXU_DTYPE)))
        gv = jnp.stack([jnp.tile(q_norm_g[l], AW // HD), jnp.tile(k_norm_g[l], AW // HD)])
        gvecs.append(jnp.concatenate([gv, jnp.zeros((SUBLANES - 2, AW), F32)], axis=0))
        cws.append(jnp.concatenate([cw_full[l], conv_b[l][None, :], jnp.zeros((SUBLANES - 4, CW), F32)], axis=0))
    w_first = gather_split([shards[0]["w1"][0], shards[0]["w2"][0]], "gather_first_ffn")

    loss_blk, dx, totals, sums = local_step(x[0], loss_target[0], mods, [ng_full[0], ng_full[1]], gvecs, cws,
                                            shards, w_first, cflag)

    dmods, dngs, dqg, dkg, dcw, dcb = [], [], [], [], [], []
    for l in range(2):
        s0, s1, so, s2, sm = sums[l]
        dmods.append(jnp.concatenate([s0[0], s0[1], s0[3], s1[0], s1[1], so[0], s2[0], s2[1], s2[3]]))
        dngs.append(jnp.concatenate([s0[2], s1[2], s2[2]]))
        dqg.append(sm[0].reshape(AW // HD, HD).sum(0))
        dkg.append(sm[1].reshape(AW // HD, HD).sum(0))
        dcw.append(sm[2:5].reshape(-1))
        dcb.append(sm[5])
    small = jnp.concatenate(dmods + dngs + dqg + dkg + dcw + dcb + [loss_blk[0]])
    small_all = small_all_gather(_pad_rows(small), "gather_small_grads")[::SUBLANES]
    nm = 9 * D
    dmod_all = small_all[:, :2 * nm].reshape(NDEV, 2, NCHIP, ADA_COLS)
    dmod_mine = lax.dynamic_index_in_dim(dmod_all, chip, axis=2, keepdims=False).transpose(1, 0, 2)
    tot = sum_devices(small_all, "sum_small_grads")[0]
    o = 2 * nm
    g_b_ada = tot[:o].reshape(2, nm)
    g_norm_g = lax.dynamic_slice_in_dim(tot[o:o + 6 * D].reshape(2, 3, D), chip * ngw, ngw, axis=2)
    o += 6 * D
    g_qg = tot[o:o + 2 * HD].reshape(2, HD)
    o += 2 * HD
    g_kg = tot[o:o + 2 * HD].reshape(2, HD)
    o += 2 * HD
    g_cw = lax.dynamic_slice_in_dim(tot[o:o + 6 * CW].reshape(2, 3, CW), chip * cww, cww, axis=2)
    o += 6 * CW
    g_cb = tot[o:o + 2 * CW].reshape(2, CW)
    loss = tot[o + 2 * CW]

    c_all_t = jnp.concatenate([c_all.T, jnp.zeros((D, LANES - NDEV), F32)], axis=1)
    g_wada_src = wada_grad(c_all_t, dmod_mine, "wada_grad")

    def halves(k_of_plane):
        return [(totals[l][0][k], totals[l][1][k]) for l, k in k_of_plane]

    r_wada = adamw(w_ada, m_w_ada, v_w_ada, [g_wada_src[0], g_wada_src[1]], cflag, "adamw_w_ada")
    r_win = adamw(w_in, m_w_in, v_w_in, halves([(0, 2), (1, 2)]), cflag, "adamw_w_in", halves=True)
    r_wout = adamw(w_out, m_w_out, v_w_out, halves([(0, 3), (1, 3)]), cflag, "adamw_w_out", halves=True)
    r_w1 = adamw(ffn_w1.reshape(4, D, HALF), m_ffn_w1.reshape(4, D, HALF), v_ffn_w1.reshape(4, D, HALF),
                 halves([(0, 0), (0, 4), (1, 0), (1, 4)]), cflag, "adamw_ffn_w1", halves=True)
    w2r = DFF // NCHIP
    r_w2 = adamw(ffn_w2.reshape(4, w2r, D), m_ffn_w2.reshape(4, w2r, D), v_ffn_w2.reshape(4, w2r, D),
                 halves([(0, 1), (0, 5), (1, 1), (1, 5)]), cflag, "adamw_ffn_w2", halves=True)
    r_w1 = [t.reshape(ffn_w1.shape) for t in r_w1]
    r_w2 = [t.reshape(ffn_w2.shape) for t in r_w2]

    smalls = [("b_ada", b_ada, m_b_ada, v_b_ada, g_b_ada), ("norm_g", norm_g, m_norm_g, v_norm_g, g_norm_g),
              ("q_norm_g", q_norm_g, m_q_norm_g, v_q_norm_g, g_qg), ("k_norm_g", k_norm_g, m_k_norm_g, v_k_norm_g, g_kg),
              ("conv_w", conv_w, m_conv_w, v_conv_w, g_cw), ("conv_b", conv_b, m_conv_b, v_conv_b, g_cb)]
    n_small = sum(t[1].size for t in smalls)
    pad = (-n_small) % (16 * LANES)

    def packed(idx):
        flat = jnp.concatenate([t[idx].reshape(-1) for t in smalls] + [jnp.zeros((pad,), F32)])
        return flat.reshape(-1, LANES)

    r_small = adamw(packed(1)[None], packed(2)[None], packed(3)[None], [packed(4)], cflag, "adamw_small")
    small_out = {}
    o = 0
    for name_, w_, _, _, _ in smalls:
        small_out[name_] = [t.reshape(-1)[o:o + w_.size].reshape(w_.shape) for t in r_small]
        o += w_.size

    res = {"w_ada": r_wada, "w_in": r_win, "w_out": r_wout, "ffn_w1": r_w1, "ffn_w2": r_w2, **small_out}
    order = ["w_ada", "b_ada", "norm_g", "w_in", "q_norm_g", "k_norm_g", "conv_w", "conv_b", "w_out", "ffn_w1", "ffn_w2"]
    outs = [loss, dx[None]]
    for k in range(4):
        outs += [res[nm_][k] for nm_ in order]
    return tuple(outs)
```

```python
import functools

import jax
import jax.numpy as jnp
from jax import lax
from jax.experimental import pallas as pl
from jax.experimental.pallas import tpu as pltpu

F32 = jnp.float32
MXU_DTYPE = jnp.bfloat16
ACT_DTYPE = jnp.bfloat16
WIRE_DTYPE = jnp.bfloat16

D = 1024
HD = 64
AW = 512
CW = 512
DFF = 2816
HALF = DFF // 2
INC = 3 * AW + 3 * CW
NCHIP = 4
NDEV = 8
QBLK = 128
ATTN_QBLOCKS = 16
ATTN_INTERLEAVE = 16
ATTN_CHUNK_ROWS = 4096
DILATIONS = (1, 4, 16)
EPS = 1e-6
NEG = -1e30
LANES = 128
SUBLANES = 8
HALO_ROWS = 16
VMEM_LIMIT = 56 * 1024 * 1024

ADAM_LR = 0.001
ADAM_B1 = 0.9
ADAM_B2 = 0.999
ADAM_EPS = 1e-08
ADAM_WD = 0.01
ADAM_STEP = 10

NT_DIMS = (((1,), (1,)), ((), ()))
TN_DIMS = (((0,), (0,)), ((), ()))


def _params(sem, vmem=VMEM_LIMIT):
    return pltpu.CompilerParams(dimension_semantics=sem, vmem_limit_bytes=vmem)


def _row_tile(n, want):
    t = min(n, want)
    assert n % t == 0
    return t


def _ada(xt, vec_ref):
    ng, sc, sh, gt = vec_ref[0:1, :], vec_ref[1:2, :], vec_ref[2:3, :], vec_ref[3:4, :]
    r = lax.rsqrt(jnp.mean(xt * xt, axis=-1, keepdims=True) + EPS)
    return xt * r, r, ng * (1.0 + sc), ng, sc, sh, gt


def _ada_bwd(dh, xhat, r, gain, ng, sc):
    dshift = jnp.sum(dh, axis=0, keepdims=True)
    dhx = dh * xhat
    dscale = jnp.sum(dhx, axis=0, keepdims=True) * ng
    dng = jnp.sum(dhx, axis=0, keepdims=True) * (1.0 + sc)
    dxhat = dh * gain
    dx = r * (dxhat - xhat * jnp.mean(dxhat * xhat, axis=-1, keepdims=True))
    return dx, dshift, dscale, dng


def _acc_rows(sums_ref, first, rows):
    @pl.when(first)
    def _():
        sums_ref[...] = jnp.zeros_like(sums_ref)
    for k, row in enumerate(rows):
        sums_ref[k:k + 1, :] += row


MESH = pl.DeviceIdType.MESH
ANY = pl.BlockSpec(memory_space=pl.ANY)


def _here():
    return lax.axis_index("x"), lax.axis_index("y"), lax.axis_index("c")


def _ici_copies(src_refs, dst_refs, send_sems, recv_sems, local_sems, scatter):
    x, y, c = _here()
    my_chip = 2 * x + y
    peers = [(1 - x, y), (x, 1 - y), (1 - x, 1 - y)]
    local, out, inc = [], [], []
    for a, (src, dst) in enumerate(zip(src_refs, dst_refs)):
        local.append(pltpu.make_async_copy(src.at[my_chip] if scatter else src, dst.at[my_chip], local_sems.at[a]))
        for j, (px, py) in enumerate(peers):
            sems = dict(send_sem=send_sems.at[3 * a + j], recv_sem=recv_sems.at[3 * a + j],
                        device_id=(px, py, c), device_id_type=MESH)
            out.append(pltpu.make_async_remote_copy(
                src_ref=src.at[2 * px + py] if scatter else src, dst_ref=dst.at[my_chip], **sems))
            inc.append(pltpu.make_async_remote_copy(
                src_ref=src.at[my_chip] if scatter else src, dst_ref=dst.at[2 * px + py], **sems))
    return local, out, inc


def _swap_copies(src_refs, dst_refs, send_sems, recv_sems, halves):
    x, y, c = _here()
    cps = []
    for k, (src, dst) in enumerate(zip(src_refs, dst_refs)):
        if halves:
            r2 = src.shape[1] // 2
            src = src.at[:, pl.ds((1 - c) * r2, r2), :]
        cps.append(pltpu.make_async_remote_copy(
            src_ref=src, dst_ref=dst, send_sem=send_sems.at[k], recv_sem=recv_sems.at[k],
            device_id=(x, y, 1 - c), device_id_type=MESH))
    return cps


class Carry:
    def __init__(self, kind, srcs):
        self.kind, self.srcs, n = kind, list(srcs), len(srcs)
        if kind == "gather":
            shapes = [(NCHIP,) + s.shape for s in srcs]
        elif kind == "swap_halves":
            shapes = [(s.shape[0], s.shape[1] // 2, s.shape[2]) for s in srcs]
        else:
            shapes = [s.shape for s in srcs]
        self.out_shape = [jax.ShapeDtypeStruct(sh, s.dtype) for sh, s in zip(shapes, srcs)]
        dma = pltpu.SemaphoreType.DMA
        self.sems = [dma((3 * n,)), dma((3 * n,)), dma((n,))] if kind in ("gather", "scatter") else [dma((n,)), dma((n,))]

    def start(self, srcs, dsts, sems):
        if self.kind in ("gather", "scatter"):
            local, out, _ = _ici_copies(srcs, dsts, *sems, self.kind == "scatter")
            for cp in local + out:
                cp.start()
        else:
            for cp in _swap_copies(srcs, dsts, *sems, self.kind == "swap_halves"):
                cp.start()

    def wait(self, srcs, dsts, sems):
        if self.kind in ("gather", "scatter"):
            local, out, inc = _ici_copies(srcs, dsts, *sems, self.kind == "scatter")
            for cp in inc:
                cp.wait_recv()
            for cp in out:
                cp.wait_send()
            for cp in local:
                cp.wait()
        else:
            cps = _swap_copies(srcs, dsts, *sems, self.kind == "swap_halves")
            for cp in cps:
                cp.wait_recv()
            for cp in cps:
                cp.wait_send()


def run_carry(carry, name):
    n = len(carry.srcs)

    def body(*refs):
        srcs, dsts, sems = refs[:n], refs[n:2 * n], refs[2 * n:]
        carry.start(srcs, dsts, sems)
        carry.wait(srcs, dsts, sems)

    return pl.pallas_call(body, name=name, out_shape=carry.out_shape, in_specs=[ANY] * n, out_specs=[ANY] * n,
                          scratch_shapes=carry.sems)(*carry.srcs)


def gather_split(srcs, name):
    n = len(srcs)

    def body(*refs):
        src_refs, dst_refs = refs[:n], refs[n:2 * n]
        send_sems, recv_sems, fwd_send, fwd_recv, local_sems = refs[2 * n:]
        x, y, c = _here()
        my_chip = 2 * x + y
        peers = [(1 - x, y), (x, 1 - y), (1 - x, 1 - y)]

        def half(ref, h):
            r2 = ref.shape[0] // 2
            return ref.at[pl.ds(h * r2, r2), :]

        local, out, landed, passed, arriving = [], [], [], [], []
        for a, (src, dst) in enumerate(zip(src_refs, dst_refs)):
            local.append(pltpu.make_async_copy(src, dst.at[my_chip], local_sems.at[a]))
            for j, (px, py) in enumerate(peers):
                k = 3 * a + j
                theirs = dst.at[2 * px + py]
                ici = dict(send_sem=send_sems.at[k], recv_sem=recv_sems.at[k], device_id=(px, py, c), device_id_type=MESH)
                d2d = dict(send_sem=fwd_send.at[k], recv_sem=fwd_recv.at[k], device_id=(x, y, 1 - c), device_id_type=MESH)
                out.append(pltpu.make_async_remote_copy(src_ref=half(src, c), dst_ref=half(dst.at[my_chip], c), **ici))
                landed.append(pltpu.make_async_remote_copy(src_ref=half(src, c), dst_ref=half(theirs, c), **ici))
                passed.append(pltpu.make_async_remote_copy(src_ref=half(theirs, c), dst_ref=half(theirs, c), **d2d))
                arriving.append(pltpu.make_async_remote_copy(src_ref=half(theirs, c), dst_ref=half(theirs, 1 - c), **d2d))
        for cp in local + out:
            cp.start()
        for got, fwd in zip(landed, passed):
            got.wait_recv()
            fwd.start()
        for cp in arriving:
            cp.wait_recv()
        for cp in out + passed:
            cp.wait_send()
        for cp in local:
            cp.wait()

    dma = pltpu.SemaphoreType.DMA
    return pl.pallas_call(
        body, name=name, out_shape=[jax.ShapeDtypeStruct((NCHIP,) + s.shape, s.dtype) for s in srcs],
        in_specs=[ANY] * n, out_specs=[ANY] * n,
        scratch_shapes=[dma((3 * n,)), dma((3 * n,)), dma((3 * n,)), dma((3 * n,)), dma((n,))],
    )(*srcs)


def _pcall(body, name, grid, in_specs, out_specs, out_shape, sem, args, carry=None, scratch=()):
    if carry is None:
        outs = pl.pallas_call(body, name=name, grid=grid, in_specs=in_specs, out_specs=out_specs,
                              out_shape=out_shape, scratch_shapes=list(scratch), compiler_params=_params(sem))(*args)
        return outs, []
    n_in, n_out, nc, ns = len(in_specs), len(out_specs), len(carry.srcs), len(scratch)

    def wrapped(*refs):
        ins, csrc = refs[:n_in], refs[n_in:n_in + nc]
        outs, cdst = refs[n_in + nc:n_in + nc + n_out], refs[n_in + nc + n_out:n_in + 2 * nc + n_out]
        own = refs[n_in + 2 * nc + n_out:n_in + 2 * nc + n_out + ns]
        sems = refs[n_in + 2 * nc + n_out + ns:]
        ids = [pl.program_id(a) for a in range(len(grid))]
        first = functools.reduce(jnp.logical_and, [i == 0 for i in ids])
        last = functools.reduce(jnp.logical_and, [i == g - 1 for i, g in zip(ids, grid)])

        @pl.when(first)
        def _():
            carry.start(csrc, cdst, sems)

        body(*ins, *outs, *own)

        @pl.when(last)
        def _():
            carry.wait(csrc, cdst, sems)

    res = pl.pallas_call(
        wrapped, name=name, grid=grid,
        in_specs=list(in_specs) + [ANY] * nc, out_specs=list(out_specs) + [ANY] * nc,
        out_shape=list(out_shape) + carry.out_shape,
        scratch_shapes=list(scratch) + carry.sems, compiler_params=_params(sem),
    )(*args, *carry.srcs)
    return res[:n_out], res[n_out:]


def ffn_fwd(x, vec, w1p, w2, gs, name, carry=None, target=None):
    S = x.shape[0]
    tm = _row_tile(S, 512)

    def body(x_ref, *refs):
        if target is None:
            vec_ref, w1_ref, w2_ref, xn_ref, a_ref, f_ref = refs
        else:
            t_ref, vec_ref, w1_ref, w2_ref, xn_ref, a_ref, f_ref, l_ref = refs
        xt = x_ref[...]
        xhat, _, gain, _, _, sh, gt = _ada(xt, vec_ref)
        h = (xhat * gain + sh).astype(MXU_DTYPE)
        f = jnp.zeros((tm, D), F32)
        for hf in range(2):
            g = jnp.dot(h, w1_ref[hf], preferred_element_type=F32)
            up = jnp.dot(h, w1_ref[2 + hf], preferred_element_type=F32)
            a_ref[:, hf * HALF:(hf + 1) * HALF] = g.astype(a_ref.dtype)
            a_ref[:, DFF + hf * HALF:DFF + (hf + 1) * HALF] = up.astype(a_ref.dtype)
            act = (g * jax.nn.sigmoid(g) * up).astype(MXU_DTYPE)
            f = f + jnp.dot(act, w2_ref[hf * HALF:(hf + 1) * HALF, :], preferred_element_type=F32)
        f_ref[...] = f.astype(f_ref.dtype)
        xn = xt + (gs * gt) * f
        if target is None:
            xn_ref[...] = xn
        else:
            diff = xn - t_ref[...]
            xn_ref[...] = diff * (1.0 / D)
            part = jnp.sum(jnp.sum(diff * diff, axis=0, keepdims=True), axis=1, keepdims=True) * (0.5 / D)

            @pl.when(pl.program_id(0) == 0)
            def _():
                l_ref[...] = jnp.zeros_like(l_ref)
            l_ref[...] += jnp.broadcast_to(part, l_ref.shape)

    tile = pl.BlockSpec((tm, D), lambda i: (i, 0))
    last = target is not None
    return _pcall(
        body, name, (S // tm,),
        [tile] * (2 if last else 1) + [
            pl.BlockSpec((SUBLANES, D), lambda i: (0, 0)),
            pl.BlockSpec((NCHIP, D, HALF), lambda i: (0, 0, 0), pipeline_mode=pl.Buffered(1)),
            pl.BlockSpec((DFF, D), lambda i: (0, 0), pipeline_mode=pl.Buffered(1))],
        [tile, pl.BlockSpec((tm, 2 * DFF), lambda i: (i, 0)), tile]
        + ([pl.BlockSpec((SUBLANES, LANES), lambda i: (0, 0))] if last else []),
        [jax.ShapeDtypeStruct((S, D), F32),
         jax.ShapeDtypeStruct((S, 2 * DFF), ACT_DTYPE),
         jax.ShapeDtypeStruct((S, D), ACT_DTYPE)]
        + ([jax.ShapeDtypeStruct((SUBLANES, LANES), F32)] if last else []),
        ("arbitrary",), (x, target, vec, w1p, w2) if last else (x, vec, w1p, w2), carry)


def ffn_bwd(dxo, x, a, f, vec, w1p, w2, gs, name, carry=None):
    S = x.shape[0]
    tm = _row_tile(S, 256)

    def body(dxo_ref, x_ref, a_ref, f_ref, vec_ref, w1_ref, w2_ref,
             dxi_ref, hb_ref, dfb_ref, act_ref, da_ref, sums_ref):
        xt = x_ref[...]
        dxo = dxo_ref[...]
        xhat, r, gain, ng, sc, sh, gt = _ada(xt, vec_ref)
        hb_ref[...] = (xhat * gain + sh).astype(hb_ref.dtype)
        dgate = gs * jnp.sum(dxo * f_ref[...].astype(F32), axis=0, keepdims=True)
        df = ((gs * gt) * dxo).astype(MXU_DTYPE)
        dfb_ref[...] = df
        dh = jnp.zeros((tm, D), F32)
        for hf in range(2):
            lo, hi = hf * HALF, (hf + 1) * HALF
            dact = lax.dot_general(df, w2_ref[lo:hi, :], NT_DIMS, preferred_element_type=F32)
            g = a_ref[:, lo:hi].astype(F32)
            up = a_ref[:, DFF + lo:DFF + hi].astype(F32)
            sg = jax.nn.sigmoid(g)
            si = g * sg
            act_ref[:, lo:hi] = (si * up).astype(act_ref.dtype)
            dg = (dact * up * (sg * (1.0 + g * (1.0 - sg)))).astype(MXU_DTYPE)
            dup = (dact * si).astype(MXU_DTYPE)
            da_ref[:, lo:hi] = dg
            da_ref[:, DFF + lo:DFF + hi] = dup
            dh = dh + lax.dot_general(dg, w1_ref[hf], NT_DIMS, preferred_element_type=F32)
            dh = dh + lax.dot_general(dup, w1_ref[2 + hf], NT_DIMS, preferred_element_type=F32)
        dx, dshift, dscale, dng = _ada_bwd(dh, xhat, r, gain, ng, sc)
        dxi_ref[...] = dxo + dx
        _acc_rows(sums_ref, pl.program_id(0) == 0, (dshift, dscale, dng, dgate))

    return _pcall(
        body, name, (S // tm,),
        [pl.BlockSpec((tm, D), lambda i: (i, 0)),
         pl.BlockSpec((tm, D), lambda i: (i, 0)),
         pl.BlockSpec((tm, 2 * DFF), lambda i: (i, 0)),
         pl.BlockSpec((tm, D), lambda i: (i, 0)),
         pl.BlockSpec((SUBLANES, D), lambda i: (0, 0)),
         pl.BlockSpec((NCHIP, D, HALF), lambda i: (0, 0, 0), pipeline_mode=pl.Buffered(1)),
         pl.BlockSpec((DFF, D), lambda i: (0, 0), pipeline_mode=pl.Buffered(1))],
        [pl.BlockSpec((tm, D), lambda i: (i, 0)),
         pl.BlockSpec((tm, D), lambda i: (i, 0)),
         pl.BlockSpec((tm, D), lambda i: (i, 0)),
         pl.BlockSpec((tm, DFF), lambda i: (i, 0)),
         pl.BlockSpec((tm, 2 * DFF), lambda i: (i, 0)),
         pl.BlockSpec((SUBLANES, D), lambda i: (0, 0))],
        [jax.ShapeDtypeStruct((S, D), F32),
         jax.ShapeDtypeStruct((S, D), MXU_DTYPE),
         jax.ShapeDtypeStruct((S, D), MXU_DTYPE),
         jax.ShapeDtypeStruct((S, DFF), MXU_DTYPE),
         jax.ShapeDtypeStruct((S, 2 * DFF), MXU_DTYPE),
         jax.ShapeDtypeStruct((SUBLANES, D), F32)],
        ("arbitrary",), (dxo, x, a, f, vec, w1p, w2), carry)


def wgrad(a, b, kt, nt, name, carry=None):
    T, K = a.shape
    N = b.shape[1]
    pk, pn = K // kt, N // nt
    assert pk == 1 or pn == 1
    tt = _row_tile(T, 2048)
    steps = T // tt

    def body(a_ref, b_ref, o_ref):
        @pl.when(pl.program_id(1) == 0)
        def _():
            o_ref[...] = jnp.zeros_like(o_ref)
        o_ref[...] += lax.dot_general(a_ref[...], b_ref[...], TN_DIMS, preferred_element_type=F32)

    a_map = (lambda p, t: (t, p)) if pk > 1 else (lambda p, t: (t, 0))
    b_map = (lambda p, t: (t, p)) if pn > 1 else (lambda p, t: (t, 0))
    (out,), got = _pcall(
        body, name, (pk * pn, steps),
        [pl.BlockSpec((tt, kt), a_map), pl.BlockSpec((tt, nt), b_map)],
        [pl.BlockSpec((None, kt, nt), lambda p, t: (p, 0, 0))],
        [jax.ShapeDtypeStruct((pk * pn, kt, nt), F32)], ("arbitrary", "arbitrary"), (a, b), carry)
    return out, got


def _head_masks(rows):
    lane = lax.broadcasted_iota(jnp.int32, (rows, LANES), 1)
    return lane < HD


def _pair_stat(x, m_a):
    s_a = jnp.sum(jnp.where(m_a, x, 0.0), axis=1, keepdims=True)
    s_b = jnp.sum(jnp.where(m_a, 0.0, x), axis=1, keepdims=True)
    return s_a, s_b


def mixer_in(x, vec, winp, gvec, name, carry=None):
    S = x.shape[0]
    tm = _row_tile(S, 512)
    pc = INC // NCHIP

    def body(x_ref, vec_ref, w_ref, g_ref, proj_ref, hb_ref, qn_ref, kn_ref, v_ref, qkv_ref):
        xt = x_ref[...]
        xhat, _, gain, _, _, sh, _ = _ada(xt, vec_ref)
        h = (xhat * gain + sh).astype(MXU_DTYPE)
        hb_ref[...] = h
        for j in range(NCHIP):
            piece = jnp.dot(h, w_ref[j], preferred_element_type=F32)
            proj_ref[:, j * pc:(j + 1) * pc] = piece.astype(proj_ref.dtype)
            if (j + 1) * pc <= 3 * AW:
                qkv_ref[:, j * pc:(j + 1) * pc] = piece
        m_a = _head_masks(tm)
        for which, dst in ((0, qn_ref), (1, kn_ref)):
            for p in range(AW // LANES):
                lo = which * AW + p * LANES
                xp = qkv_ref[:, lo:lo + LANES]
                s_a, s_b = _pair_stat(xp * xp, m_a)
                rr = jnp.where(m_a, lax.rsqrt(s_a * (1.0 / HD) + EPS), lax.rsqrt(s_b * (1.0 / HD) + EPS))
                gp = g_ref[which:which + 1, p * LANES:(p + 1) * LANES]
                dst[:, p * LANES:(p + 1) * LANES] = (xp * rr * gp).astype(dst.dtype)
        v_ref[...] = qkv_ref[:, 2 * AW:3 * AW]

    assert 2 * pc == 3 * AW
    return _pcall(
        body, name, (S // tm,),
        [pl.BlockSpec((tm, D), lambda i: (i, 0)),
         pl.BlockSpec((SUBLANES, D), lambda i: (0, 0)),
         pl.BlockSpec((NCHIP, D, pc), lambda i: (0, 0, 0), pipeline_mode=pl.Buffered(1)),
         pl.BlockSpec((SUBLANES, AW), lambda i: (0, 0))],
        [pl.BlockSpec((tm, INC), lambda i: (i, 0)),
         pl.BlockSpec((tm, D), lambda i: (i, 0)),
         pl.BlockSpec((tm, AW), lambda i: (i, 0)),
         pl.BlockSpec((tm, AW), lambda i: (i, 0)),
         pl.BlockSpec((tm, AW), lambda i: (i, 0))],
        [jax.ShapeDtypeStruct((S, INC), ACT_DTYPE),
         jax.ShapeDtypeStruct((S, D), MXU_DTYPE),
         jax.ShapeDtypeStruct((S, AW), F32),
         jax.ShapeDtypeStruct((S, AW), F32),
         jax.ShapeDtypeStruct((S, AW), F32)],
        ("arbitrary",), (x, vec, winp, gvec), carry, scratch=[pltpu.VMEM((tm, 3 * AW), F32)])


def _band_masks(ncol):
    row = lax.broadcasted_iota(jnp.int32, (2 * QBLK, ncol), 0) & (QBLK - 1)
    col = lax.broadcasted_iota(jnp.int32, (2 * QBLK, ncol), 1)
    return row, col


def _stack_heads(t, m_a):
    zero = jnp.zeros_like(t)
    return jnp.concatenate([jnp.where(m_a, t, zero), jnp.where(m_a, zero, t)], axis=0)


class _AttnLayout:
    def __init__(self, d, S):
        self.d, self.S = d, S
        self.qb = max(1, min(ATTN_QBLOCKS, ATTN_CHUNK_ROWS // (QBLK * d)))
        self.nres = d
        self.nchunk = S // (self.qb * QBLK * d)
        self.grid = (AW // LANES, self.nchunk)
        self.unroll = max(1, min(d, ATTN_INTERLEAVE // self.qb))

    def _spec(self, blocks, row_of):
        return pl.BlockSpec((blocks * QBLK * self.d, LANES), lambda hp, j: (row_of(j), hp))

    def cur(self, chunk_of):
        return self._spec(self.qb, chunk_of)

    def prev(self, chunk_of):
        return self._spec(1, lambda j: jnp.maximum(chunk_of(j) * self.qb - 1, 0))

    def idx(self, b, r):
        if self.d == 1:
            return (pl.ds(b * QBLK, QBLK), slice(None))
        return (pl.ds(b * QBLK * self.d + r, QBLK, stride=self.d), slice(None))

    def per_residue(self, fn):
        if self.nres == 1:
            fn(0)
        else:
            def step(it, carry):
                for k in range(self.unroll):
                    fn(it * self.unroll + k)
                return carry
            lax.fori_loop(0, self.nres // self.unroll, step, 0)


def attn_fwd(qn, kn, v, d, name, carry=None):
    S = qn.shape[0]
    lay = _AttnLayout(d, S)
    qb = lay.qb

    def body(q_ref, kc_ref, kp_ref, vc_ref, vp_ref, o_ref, lse_ref):
        i = pl.program_id(1)
        m_a = _head_masks(QBLK)
        row, col = _band_masks(2 * QBLK)
        dist = row + QBLK - col
        band = (dist >= 0) & (dist <= QBLK)
        first = band & ((i > 0) | (col >= QBLK))

        def residue(r):
            kt = [kp_ref[lay.idx(0, r)].astype(MXU_DTYPE)]
            vt = [vp_ref[lay.idx(0, r)].astype(MXU_DTYPE)]
            for b in range(qb):
                kt.append(kc_ref[lay.idx(b, r)].astype(MXU_DTYPE))
                vt.append(vc_ref[lay.idx(b, r)].astype(MXU_DTYPE))
            for b in range(qb):
                rows = lay.idx(b, r)
                q = (q_ref[rows] * (HD ** -0.5)).astype(MXU_DTYPE)
                kcat = jnp.concatenate([kt[b], kt[b + 1]], axis=0)
                vcat = jnp.concatenate([vt[b], vt[b + 1]], axis=0)
                mask = first if b == 0 else band
                s = lax.dot_general(_stack_heads(q, m_a), kcat, NT_DIMS, preferred_element_type=F32)
                s = jnp.where(mask, s, NEG)
                m = jnp.max(s, axis=1, keepdims=True)
                p = jnp.exp(s - m)
                l = jnp.sum(p, axis=1, keepdims=True)
                o = jnp.dot(p.astype(MXU_DTYPE), vcat, preferred_element_type=F32) / l
                lse = jnp.broadcast_to(m + jnp.log(l), (2 * QBLK, LANES))
                o_ref[rows] = jnp.where(m_a, o[:QBLK], o[QBLK:])
                lse_ref[rows] = jnp.where(m_a, lse[:QBLK], lse[QBLK:])

        lay.per_residue(residue)

    cur, prev = lay.cur(lambda j: j), lay.prev(lambda j: j)
    return _pcall(body, name, lay.grid, [cur, cur, prev, cur, prev], [cur, cur],
                  [jax.ShapeDtypeStruct((S, AW), F32)] * 2, ("arbitrary", "arbitrary"), (qn, kn, kn, v, v), carry)


def _both_heads(t, m_a):
    other = pltpu.roll(t, HD, 1)
    return jnp.concatenate([jnp.where(m_a, t, other), jnp.where(m_a, other, t)], axis=0)


def attn_bwd(qn, kn, v, dycat, lse, delta, d, name, carry=None):
    S = qn.shape[0]
    lay = _AttnLayout(d, S)
    qb, nchunk = lay.qb, lay.nchunk

    def body(q_ref, kc_ref, kp_ref, vc_ref, vp_ref, do_ref, lse_ref, dl_ref,
             dq_ref, dk_ref, dv_ref, ck_ref, cv_ref):
        j = pl.program_id(1)
        i = nchunk - 1 - j
        m_a = _head_masks(QBLK)
        row, col = _band_masks(2 * QBLK)
        dist = row + QBLK - col
        band = (dist >= 0) & (dist <= QBLK)
        first = band & ((i > 0) | (col >= QBLK))

        def residue(r):
            def tiles(ref, cast):
                out = [ref[lay.idx(b, r)] for b in range(qb)]
                return [t.astype(MXU_DTYPE) for t in out] if cast else out

            def ktiles(cur_ref, prev_ref):
                return [prev_ref[lay.idx(0, r)].astype(MXU_DTYPE)] + tiles(cur_ref, True)

            qt = [(t * (HD ** -0.5)).astype(MXU_DTYPE) for t in tiles(q_ref, False)]
            dot_ = tiles(do_ref, True)
            lse_t = tiles(lse_ref, False)
            dl_t = tiles(dl_ref, False)
            kt = ktiles(kc_ref, kp_ref)
            vt = ktiles(vc_ref, vp_ref)
            dk_acc = [jnp.zeros((QBLK, LANES), F32) for _ in range(qb)]
            dv_acc = [jnp.zeros((QBLK, LANES), F32) for _ in range(qb)]
            crow = pl.ds(0, QBLK) if lay.nres == 1 else pl.ds(pl.multiple_of(r * QBLK, QBLK), QBLK)
            dk_acc[qb - 1] = jnp.where(j > 0, ck_ref[crow, :], 0.0)
            dv_acc[qb - 1] = jnp.where(j > 0, cv_ref[crow, :], 0.0)
            for x in range(qb):
                kcat = jnp.concatenate([kt[x], kt[x + 1]], axis=0)
                vcat = jnp.concatenate([vt[x], vt[x + 1]], axis=0)
                q2 = _stack_heads(qt[x], m_a)
                do2 = _stack_heads(dot_[x], m_a)
                lse2 = _both_heads(lse_t[x], m_a)
                dl2 = _both_heads(dl_t[x], m_a)
                lse2 = jnp.concatenate([lse2, lse2], axis=1)
                dl2 = jnp.concatenate([dl2, dl2], axis=1)
                s = lax.dot_general(q2, kcat, NT_DIMS, preferred_element_type=F32)
                p = jnp.exp(jnp.where(first if x == 0 else band, s, NEG) - lse2)
                dp = lax.dot_general(do2, vcat, NT_DIMS, preferred_element_type=F32)
                ds = p * (dp - dl2)
                dq = jnp.dot(ds.astype(MXU_DTYPE), kcat, preferred_element_type=F32)
                dq_ref[lay.idx(x, r)] = jnp.where(m_a, dq[:QBLK], dq[QBLK:]) * (HD ** -0.5)
                dk = jnp.dot(ds.T.astype(MXU_DTYPE), q2, preferred_element_type=F32)
                dv = jnp.dot(p.T.astype(MXU_DTYPE), do2, preferred_element_type=F32)
                if x == 0:
                    ck_ref[crow, :] = dk[:QBLK]
                    cv_ref[crow, :] = dv[:QBLK]
                else:
                    dk_acc[x - 1] = dk_acc[x - 1] + dk[:QBLK]
                    dv_acc[x - 1] = dv_acc[x - 1] + dv[:QBLK]
                dk_acc[x] = dk_acc[x] + dk[QBLK:]
                dv_acc[x] = dv_acc[x] + dv[QBLK:]
            for kb in range(qb):
                dk_ref[lay.idx(kb, r)] = dk_acc[kb]
                dv_ref[lay.idx(kb, r)] = dv_acc[kb]

        lay.per_residue(residue)

    cur, prev = lay.cur(lambda j: nchunk - 1 - j), lay.prev(lambda j: nchunk - 1 - j)
    carried = pltpu.VMEM((lay.nres * QBLK, LANES), F32)
    return _pcall(
        body, name, lay.grid, [cur, cur, prev, cur, prev, cur, cur, cur], [cur, cur, cur],
        [jax.ShapeDtypeStruct((S, AW), F32)] * 3, ("arbitrary", "arbitrary"),
        (qn, kn, kn, v, v, dycat, lse, delta), carry, scratch=[carried, carried])


def _shift_down(x, halo_prev, k, row):
    tm = x.shape[0]
    tail = jnp.concatenate([pltpu.roll(halo_prev, k, 0), jnp.zeros((tm - SUBLANES, x.shape[1]), x.dtype)], axis=0)
    return jnp.where(row < k, tail, pltpu.roll(x, k, 0))


def _shift_up(x, halo_next, k, row):
    tm = x.shape[0]
    head = jnp.concatenate([jnp.zeros((tm - SUBLANES, x.shape[1]), x.dtype), pltpu.roll(halo_next, SUBLANES - k, 0)], axis=0)
    return jnp.where(row >= tm - k, head, pltpu.roll(x, tm - k, 0))


def _conv_fwd(cu, halo_cu, cw_ref, row):
    u1 = _shift_down(cu, halo_cu, 1, row)
    u2 = _shift_down(cu, halo_cu, 2, row)
    cv = cw_ref[0:1, :] * u2 + cw_ref[1:2, :] * u1 + cw_ref[2:3, :] * cu + cw_ref[3:4, :]
    return cv, u1, u2


def mixer_out(os_, lses, proj, cw, x, vec, wout, name, carry=None):
    S = proj.shape[0]
    tm = _row_tile(S, 512)
    hb = tm // HALO_ROWS

    def body(o1, o2, o3, l1, l2, l3, pc_ref, ph_ref, cw_ref, x_ref, vec_ref, w_ref, ycat_ref, lse_ref, xn_ref, y_ref):
        i = pl.program_id(0)
        for p in range(AW // LANES):
            cs = slice(p * LANES, (p + 1) * LANES)
            ls = [l[:, cs] for l in (l1, l2, l3)]
            mx = jnp.maximum(jnp.maximum(ls[0], ls[1]), ls[2])
            t = mx + jnp.log(jnp.exp(ls[0] - mx) + jnp.exp(ls[1] - mx) + jnp.exp(ls[2] - mx))
            lse_ref[:, cs] = t
            acc = jnp.zeros((tm, LANES), F32)
            for l, o in zip(ls, (o1, o2, o3)):
                acc = acc + jnp.exp(l - t) * o[:, cs]
            ycat_ref[:, cs] = acc.astype(ycat_ref.dtype)
        row = lax.broadcasted_iota(jnp.int32, (tm, CW), 0)
        gb, gc, u = (pc_ref[:, k * CW:(k + 1) * CW].astype(F32) for k in range(3))
        ph = ph_ref[...].astype(F32)[HALO_ROWS - SUBLANES:]
        halo_cu = jnp.where(i > 0, ph[:, CW:2 * CW] * ph[:, 2 * CW:3 * CW], 0.0)
        cv, _, _ = _conv_fwd(gc * u, halo_cu, cw_ref, row)
        ycat_ref[:, AW:AW + CW] = (gb * cv).astype(ycat_ref.dtype)
        y = jnp.dot(ycat_ref[...].astype(MXU_DTYPE), w_ref[...], preferred_element_type=F32)
        xn_ref[...] = x_ref[...] + vec_ref[3:4, :] * y
        y_ref[...] = y.astype(y_ref.dtype)

    ot = pl.BlockSpec((tm, AW), lambda i: (i, 0))
    t = pl.BlockSpec((tm, D), lambda i: (i, 0))
    return _pcall(
        body, name, (S // tm,),
        [ot] * 6 + [pl.BlockSpec((tm, 3 * CW), lambda i: (i, 1)),
                    pl.BlockSpec((HALO_ROWS, 3 * CW), lambda i: (jnp.maximum(i * hb - 1, 0), 1)),
                    pl.BlockSpec((SUBLANES, CW), lambda i: (0, 0)),
                    t, pl.BlockSpec((SUBLANES, D), lambda i: (0, 0)), pl.BlockSpec((D, D), lambda i: (0, 0))],
        [t, ot, t, t],
        [jax.ShapeDtypeStruct((S, D), ACT_DTYPE), jax.ShapeDtypeStruct((S, AW), F32),
         jax.ShapeDtypeStruct((S, D), F32), jax.ShapeDtypeStruct((S, D), ACT_DTYPE)],
        ("arbitrary",), (*os_, *lses, proj, proj, cw, x, vec, wout), carry)


def out_proj_bwd(dxo, y, ycat, vec, wout, name, carry=None):
    S = dxo.shape[0]
    tm = _row_tile(S, 512)

    def body(dxo_ref, y_ref, yc_ref, vec_ref, w_ref, dyb_ref, dyc_ref, dl_ref, sums_ref):
        dxo = dxo_ref[...]
        dgate = jnp.sum(dxo * y_ref[...].astype(F32), axis=0, keepdims=True)
        dy = (vec_ref[3:4, :] * dxo).astype(MXU_DTYPE)
        dyb_ref[...] = dy
        dyc_ref[...] = lax.dot_general(dy, w_ref[...], NT_DIMS, preferred_element_type=F32)
        m_a = _head_masks(tm)
        for p in range(AW // LANES):
            cs = slice(p * LANES, (p + 1) * LANES)
            s_a, s_b = _pair_stat(dyc_ref[:, cs] * yc_ref[:, cs].astype(F32), m_a)
            dl_ref[:, cs] = jnp.where(m_a, s_a, s_b)
        _acc_rows(sums_ref, pl.program_id(0) == 0, (dgate,))

    t = pl.BlockSpec((tm, D), lambda i: (i, 0))
    at = pl.BlockSpec((tm, AW), lambda i: (i, 0))
    return _pcall(
        body, name, (S // tm,),
        [t, t, t, pl.BlockSpec((SUBLANES, D), lambda i: (0, 0)), pl.BlockSpec((D, D), lambda i: (0, 0))],
        [t, t, at, pl.BlockSpec((SUBLANES, D), lambda i: (0, 0))],
        [jax.ShapeDtypeStruct((S, D), MXU_DTYPE), jax.ShapeDtypeStruct((S, D), F32),
         jax.ShapeDtypeStruct((S, AW), F32), jax.ShapeDtypeStruct((SUBLANES, D), F32)],
        ("arbitrary",), (dxo, y, ycat, vec, wout), carry)


def mixer_mid_bwd(dqs, dks, dvs, proj, dycat, gvec, cw, name, carry=None):
    S = proj.shape[0]
    tm = _row_tile(S, 512)
    hb = tm // SUBLANES
    hp = tm // HALO_ROWS
    nsl = S // SUBLANES
    ntile = S // tm

    def body(dq1, dq2, dq3, dk1, dk2, dk3, dv1, dv2, dv3, pr_ref, pp_ref, pn_ref, dyc_ref, dyn_ref,
             g_ref, cw_ref, dp_ref, sums_ref):
        i = pl.program_id(0)
        m_a = _head_masks(tm)
        gsum = []
        for which, parts in ((0, (dq1, dq2, dq3)), (1, (dk1, dk2, dk3))):
            acc_g = []
            for p in range(AW // LANES):
                lo = which * AW + p * LANES
                cs = slice(p * LANES, (p + 1) * LANES)
                xp = pr_ref[:, lo:lo + LANES].astype(F32)
                s_a, s_b = _pair_stat(xp * xp, m_a)
                rr = jnp.where(m_a, lax.rsqrt(s_a * (1.0 / HD) + EPS), lax.rsqrt(s_b * (1.0 / HD) + EPS))
                xh = xp * rr
                dn = parts[0][:, cs] + parts[1][:, cs] + parts[2][:, cs]
                acc_g.append(jnp.sum(dn * xh, axis=0, keepdims=True))
                t = dn * g_ref[which:which + 1, cs]
                t_a, t_b = _pair_stat(t * xh, m_a)
                mean = jnp.where(m_a, t_a, t_b) * (1.0 / HD)
                dp_ref[:, lo:lo + LANES] = (rr * (t - xh * mean)).astype(dp_ref.dtype)
            gsum.append(jnp.concatenate(acc_g, axis=1))
        dp_ref[:, 2 * AW:3 * AW] = (dv1[...] + dv2[...] + dv3[...]).astype(dp_ref.dtype)
        row = lax.broadcasted_iota(jnp.int32, (tm, CW), 0)
        base = 3 * AW
        gb, gc, u = (pr_ref[:, base + k * CW:base + (k + 1) * CW].astype(F32) for k in range(3))
        cu = gc * u
        pp = pp_ref[...].astype(F32)[HALO_ROWS - SUBLANES:]
        halo_cu = jnp.where(i > 0, pp[:, CW:2 * CW] * pp[:, 2 * CW:3 * CW], 0.0)
        cv, u1, u2 = _conv_fwd(cu, halo_cu, cw_ref, row)
        dyc = dyc_ref[...]
        dp_ref[:, base:base + CW] = (dyc * cv).astype(dp_ref.dtype)
        dcv = dyc * gb
        gb_next = pn_ref[:, 0:CW].astype(F32)[:SUBLANES]
        halo_dcv = jnp.where(i < ntile - 1, dyn_ref[...] * gb_next, 0.0)
        d1 = _shift_up(dcv, halo_dcv, 1, row)
        d2 = _shift_up(dcv, halo_dcv, 2, row)
        dcu = cw_ref[2:3, :] * dcv + cw_ref[1:2, :] * d1 + cw_ref[0:1, :] * d2
        dp_ref[:, base + CW:base + 2 * CW] = (dcu * u).astype(dp_ref.dtype)
        dp_ref[:, base + 2 * CW:base + 3 * CW] = (dcu * gc).astype(dp_ref.dtype)
        rows = (gsum[0], gsum[1],
                jnp.sum(dcv * u2, axis=0, keepdims=True), jnp.sum(dcv * u1, axis=0, keepdims=True),
                jnp.sum(dcv * cu, axis=0, keepdims=True), jnp.sum(dcv, axis=0, keepdims=True))
        _acc_rows(sums_ref, i == 0, rows)

    at = pl.BlockSpec((tm, AW), lambda i: (i, 0))
    return _pcall(
        body, name, (ntile,),
        [at] * 9 + [
            pl.BlockSpec((tm, INC), lambda i: (i, 0)),
            pl.BlockSpec((HALO_ROWS, 3 * CW), lambda i: (jnp.maximum(i * hp - 1, 0), 1)),
            pl.BlockSpec((HALO_ROWS, 3 * CW), lambda i: (jnp.minimum((i + 1) * hp, S // HALO_ROWS - 1), 1)),
            pl.BlockSpec((tm, CW), lambda i: (i, 1)),
            pl.BlockSpec((SUBLANES, CW), lambda i: (jnp.minimum((i + 1) * hb, nsl - 1), 1)),
            pl.BlockSpec((SUBLANES, AW), lambda i: (0, 0)),
            pl.BlockSpec((SUBLANES, CW), lambda i: (0, 0))],
        [pl.BlockSpec((tm, INC), lambda i: (i, 0)), pl.BlockSpec((SUBLANES, AW), lambda i: (0, 0))],
        [jax.ShapeDtypeStruct((S, INC), MXU_DTYPE), jax.ShapeDtypeStruct((SUBLANES, AW), F32)],
        ("arbitrary",), (*dqs, *dks, *dvs, proj, proj, proj, dycat, dycat, gvec, cw), carry)


def mixer_in_bwd(dxo, x, dproj, vec, winp, name, carry=None):
    S = x.shape[0]
    tm = _row_tile(S, 512)
    pc = INC // NCHIP

    def body(dxo_ref, x_ref, dp_ref, vec_ref, w_ref, dxi_ref, sums_ref):
        xhat, r, gain, ng, sc, _, _ = _ada(x_ref[...], vec_ref)
        dh = jnp.zeros((tm, D), F32)
        for j in range(NCHIP):
            dh = dh + lax.dot_general(dp_ref[:, j * pc:(j + 1) * pc], w_ref[j], NT_DIMS, preferred_element_type=F32)
        dx, dshift, dscale, dng = _ada_bwd(dh, xhat, r, gain, ng, sc)
        dxi_ref[...] = dxo_ref[...] + dx
        _acc_rows(sums_ref, pl.program_id(0) == 0, (dshift, dscale, dng))

    t = pl.BlockSpec((tm, D), lambda i: (i, 0))
    return _pcall(
        body, name, (S // tm,),
        [t, t, pl.BlockSpec((tm, INC), lambda i: (i, 0)),
         pl.BlockSpec((SUBLANES, D), lambda i: (0, 0)),
         pl.BlockSpec((NCHIP, D, pc), lambda i: (0, 0, 0), pipeline_mode=pl.Buffered(1))],
        [t, pl.BlockSpec((SUBLANES, D), lambda i: (0, 0))],
        [jax.ShapeDtypeStruct((S, D), F32), jax.ShapeDtypeStruct((SUBLANES, D), F32)],
        ("arbitrary",), (dxo, x, dproj, vec, winp), carry)


def _vec(mod_l, ng_l, i):
    m = mod_l.reshape(3, 3, D)
    rows = jnp.stack([ng_l[i], m[i, 1], m[i, 0], m[i, 2]])
    return jnp.concatenate([rows, jnp.zeros((SUBLANES - 4, D), F32)], axis=0)


def local_step(x, target, mods, ngs, gvecs, cws, shards, w_first, cflag):
    saved = []
    weights = [dict(w1=[None, None], w2=[None, None]) for _ in range(2)]
    weights[0]["w1"][0], weights[0]["w2"][0] = w_first[0], w_first[1].reshape(DFF, D)
    h = x
    for l in range(2):
        w, sh = weights[l], shards[l]
        nxt = shards[l + 1] if l == 0 else None
        vecs = [_vec(mods[l], ngs[l], i) for i in range(3)]
        x0 = h
        (x1, a0, f0), (win, wout, w2b) = ffn_fwd(x0, vecs[0], w["w1"][0], w["w2"][0], 0.5, f"ffn_fwd_l{l}a",
                                                 carry=Carry("gather", [sh["win"], sh["wout"], sh["w2"][1]]))
        w["win"], w["wout"], w["w2"][1] = win, wout.reshape(D, D), w2b.reshape(DFF, D)
        quarter = [sh["w1"][1][k * (D // 4):(k + 1) * (D // 4)] for k in range(4)]
        (proj, h1b, qn, kn, v), w1b = mixer_in(x1, vecs[1], w["win"], gvecs[l], f"mixer_in_l{l}",
                                               carry=Carry("gather", [quarter[0]]))
        os_, lses, w1b = [], [], list(w1b)
        for k, d in enumerate(DILATIONS):
            (o, lse_d), got = attn_fwd(qn, kn, v, d, f"attn_fwd_l{l}_d{d}", carry=Carry("gather", [quarter[1 + k]]))
            w1b.append(got[0])
            os_.append(o)
            lses.append(lse_d)
        w["w1"][1] = jnp.concatenate(w1b, axis=1)
        (ycat, lse, x2, y), got = mixer_out(os_, lses, proj, cws[l], x1, vecs[1], w["wout"], f"mixer_out_l{l}",
                                            carry=Carry("gather", [nxt["w2"][0]]) if nxt else None)
        if nxt:
            weights[1]["w2"][0] = got[0].reshape(DFF, D)
        if nxt:
            (h, a2, f2), got = ffn_fwd(x2, vecs[2], w["w1"][1], w["w2"][1], 0.5, f"ffn_fwd_l{l}b",
                                       carry=Carry("gather", [nxt["w1"][0]]))
            weights[1]["w1"][0] = got[0]
        else:
            (dx, a2, f2, loss_blk), _ = ffn_fwd(x2, vecs[2], w["w1"][1], w["w2"][1], 0.5, f"ffn_fwd_l{l}b",
                                                target=target)
        saved.append(dict(vecs=vecs, x0=x0, a0=a0, f0=f0, x1=x1, proj=proj, h1b=h1b, qn=qn, kn=kn, v=v,
                          ycat=ycat, lse=lse, y=y, x2=x2, a2=a2, f2=f2))
    sums, totals, g_prev = [None, None], [None, None], None
    w2r = DFF // NCHIP
    for l in (1, 0):
        w, s = weights[l], saved[l]
        vecs = s["vecs"]
        ride = g_prev is not None
        own = l == 0
        mine, other = [None] * 6, [None] * 6

        def half_sum(group, recv, k0):
            return [add_half(g, r, cflag, f"add_sibling_l{l}_{k0 + j}") for j, (g, r) in enumerate(zip(group, recv))]

        def chip_sum(landed, k0):
            return [sum_chips(t, f"sum_chips_l{l}_{k0 + j}") for j, t in enumerate(landed)]

        (dx, hb, dfb, act, da, sums2), got = ffn_bwd(
            dx, s["x2"], s["a2"], s["f2"], vecs[2], w["w1"][1], w["w2"][1], 0.5, f"ffn_bwd_l{l}b",
            carry=Carry("swap_halves", g_prev) if ride else None)
        dw1b, _ = wgrad(hb, da, D, HALF, f"wgrad_w1_l{l}b")
        dw2b, _ = wgrad(act, dfb, HALF, D, f"wgrad_w2_l{l}b")
        if ride:
            wire = [add_half(g_prev[k], got[k], cflag, f"add_sibling_l{l + 1}_{k}") for k in range(6)]
        g_ffn_b = [dw1b, dw2b.reshape(NCHIP, w2r, D)]
        (dyb, dycat, delta, sums_o), got = out_proj_bwd(
            dx, s["y"], s["ycat"], vecs[1], w["wout"], f"out_proj_bwd_l{l}",
            carry=Carry("swap_halves", g_ffn_b) if own else None)
        dwout, _ = wgrad(s["ycat"].astype(MXU_DTYPE), dyb, D // 2, D, f"wgrad_wout_l{l}")
        if own:
            wire_ffn_b = half_sum(g_ffn_b, got, 4)
        dqs, dks, dvs, landed = [], [], [], {}
        for d in DILATIONS:
            carry = None
            if ride and d == 1:
                carry = Carry("scatter", wire[3:])
            if ride and d == 16:
                carry = Carry("scatter", wire[:3])
            if own and d == 4:
                carry = Carry("scatter", wire_ffn_b)
            (dq, dk, dv), landed[d] = attn_bwd(s["qn"], s["kn"], s["v"], dycat, s["lse"], delta, d,
                                               f"attn_bwd_l{l}_d{d}", carry=carry)
            dqs.append(dq)
            dks.append(dk)
            dvs.append(dv)
        if ride:
            tot = [sum_chips(t, f"sum_chips_l{l + 1}_{k}") for k, t in enumerate(list(landed[16]) + list(landed[1]))]
        if own:
            mine[4:6] = chip_sum(landed[4], 4)
        ready = (tot if ride else []) + (mine[4:6] if own else [])
        (dproj, sums_m), got = mixer_mid_bwd(dqs, dks, dvs, s["proj"], dycat, gvecs[l], cws[l], f"mixer_mid_bwd_l{l}",
                                             carry=Carry("swap", ready) if ready else None)
        if ride:
            totals[l + 1] = (tot, list(got[:6]))
        if own:
            other[4:6] = list(got[-2:])
        dwin, _ = wgrad(s["h1b"], dproj, D, INC // NCHIP, f"wgrad_win_l{l}")
        g_mixer = [dwin, dwout.reshape(NCHIP, D // NCHIP, D)]
        (dx, sums1), got = mixer_in_bwd(dx, s["x1"], dproj, vecs[1], w["win"], f"mixer_in_bwd_l{l}",
                                        carry=Carry("swap_halves", g_mixer) if own else None)
        if own:
            wire_mixer = half_sum(g_mixer, got, 2)
        (dx, hb, dfb, act, da, sums0), _ = ffn_bwd(
            dx, s["x0"], s["a0"], s["f0"], vecs[0], w["w1"][0], w["w2"][0], 0.5, f"ffn_bwd_l{l}a")
        dw1a, got = wgrad(hb, da, D, HALF, f"wgrad_w1_l{l}a", carry=Carry("scatter", wire_mixer) if own else None)
        if own:
            mine[2:4] = chip_sum(got, 2)
        dw2a, got = wgrad(act, dfb, HALF, D, f"wgrad_w2_l{l}a", carry=Carry("swap", mine[2:4]) if own else None)
        g_ffn_a = [dw1a, dw2a.reshape(NCHIP, w2r, D)]
        if own:
            other[2:4] = list(got)
            wire_ffn_a = half_sum(g_ffn_a, run_carry(Carry("swap_halves", g_ffn_a), "swap_halves_tail"), 0)
            mine[0:2] = chip_sum(run_carry(Carry("scatter", wire_ffn_a), "scatter_grads_tail"), 0)
            other[0:2] = list(run_carry(Carry("swap", mine[0:2]), "swap_totals_tail"))
            totals[l] = (mine, other)
        g_prev = g_ffn_a + g_mixer + g_ffn_b
        sums[l] = (sums0, sums1, sums_o, sums2, sums_m)
    return loss_blk, dx, totals, sums


def small_all_gather(blk, name):
    m_per, n = blk.shape

    def body(x_ref, out_ref, send_sems, recv_sems, local_sem):
        x, y, c = _here()
        me, sibling = (x, y, c), (x, y, 1 - c)
        chips = [(1 - x, y), (x, 1 - y), (1 - x, 1 - y)]

        def rows(px, py, pc):
            return out_ref.at[pl.ds((4 * px + 2 * py + pc) * m_per, m_per), :]

        def copy(k, block, to, src=None):
            return pltpu.make_async_remote_copy(
                src_ref=rows(*block) if src is None else src, dst_ref=rows(*block),
                send_sem=send_sems.at[k], recv_sem=recv_sems.at[k], device_id=to, device_id_type=MESH)

        mine = pltpu.make_async_copy(x_ref, rows(*me), local_sem)
        mine.start()
        first = [copy(0, me, sibling, src=x_ref)]
        first += [copy(1 + j, me, (*chip, c), src=x_ref) for j, chip in enumerate(chips)]
        for cp in first:
            cp.start()
        passed = [copy(4 + j, (*chip, c), sibling) for j, chip in enumerate(chips)]
        for j, chip in enumerate(chips):
            copy(1 + j, (*chip, c), me).wait_recv()
            passed[j].start()
        copy(0, sibling, me).wait_recv()
        for j, chip in enumerate(chips):
            copy(4 + j, (*chip, 1 - c), me).wait_recv()
        for cp in first + passed:
            cp.wait_send()
        mine.wait()

    return pl.pallas_call(
        body, name=name,
        out_shape=jax.ShapeDtypeStruct((NDEV * m_per, n), blk.dtype),
        in_specs=[pl.BlockSpec(memory_space=pltpu.VMEM)],
        out_specs=pl.BlockSpec(memory_space=pltpu.VMEM),
        scratch_shapes=[pltpu.SemaphoreType.DMA((7,)), pltpu.SemaphoreType.DMA((7,)), pltpu.SemaphoreType.DMA],
        compiler_params=pltpu.CompilerParams(vmem_limit_bytes=VMEM_LIMIT),
    )(blk)


EW_BLOCK_BYTES = 1 << 20


def _ew_rows(rows, cols, refs=8):
    want = max(16, EW_BLOCK_BYTES * (2 if refs <= 4 else 1) // (4 * cols))
    best = None
    for t in range(16, rows + 1, 16):
        if rows % t == 0 and t <= want:
            best = t
    return best if best is not None else rows


def add_half(g, recv, cflag, name):
    pieces, r, cols = g.shape
    r2 = r // 2
    tr = _ew_rows(r2, cols, refs=3)
    nt = r2 // tr

    def body(c_ref, g_ref, r_ref, o_ref):
        o_ref[...] = (g_ref[...] + r_ref[...]).astype(o_ref.dtype)

    half = pl.BlockSpec((None, tr, cols), lambda j, i, c_ref: (j, i, 0))
    return pl.pallas_call(
        body, name=name,
        grid_spec=pltpu.PrefetchScalarGridSpec(
            num_scalar_prefetch=1, grid=(pieces, nt),
            in_specs=[pl.BlockSpec((None, tr, cols), lambda j, i, c_ref: (j, c_ref[0] * nt + i, 0)), half],
            out_specs=half),
        out_shape=jax.ShapeDtypeStruct((pieces, r2, cols), WIRE_DTYPE),
        compiler_params=_params(("arbitrary", "arbitrary")),
    )(cflag, g, recv)


def sum_chips(recv, name):
    _, r, cols = recv.shape
    tr = _ew_rows(r, cols, refs=3)

    def body(r_ref, o_ref):
        acc = r_ref[0].astype(F32)
        for k in range(1, NCHIP):
            acc = acc + r_ref[k].astype(F32)
        o_ref[...] = acc

    return pl.pallas_call(
        body, name=name, grid=(r // tr,),
        in_specs=[pl.BlockSpec((NCHIP, tr, cols), lambda i: (0, i, 0))],
        out_specs=pl.BlockSpec((tr, cols), lambda i: (i, 0)),
        out_shape=jax.ShapeDtypeStruct((r, cols), F32),
        compiler_params=_params(("arbitrary",)),
    )(recv)


def sum_devices(rows8, name):
    def body(r_ref, o_ref):
        acc = r_ref[0:1, :]
        for k in range(1, NDEV):
            acc = acc + r_ref[k:k + 1, :]
        o_ref[...] = jnp.broadcast_to(acc, o_ref.shape)

    return pl.pallas_call(
        body, name=name, out_shape=jax.ShapeDtypeStruct(rows8.shape, F32),
        in_specs=[pl.BlockSpec(memory_space=pltpu.VMEM)], out_specs=pl.BlockSpec(memory_space=pltpu.VMEM),
        compiler_params=pltpu.CompilerParams(vmem_limit_bytes=VMEM_LIMIT),
    )(rows8)


def adamw(w, m, v, srcs, cflag, name, halves=False):
    planes, r, cols = w.shape
    rh = r // 2 if halves else r
    tr = _ew_rows(rh, cols)
    nth = rh // tr
    flat = [a for s in srcs for a in (s if halves else (s,))]
    ns = len(flat)
    per = ns // planes

    def body(c_ref, w_ref, m_ref, v_ref, *rest):
        s_refs, (g_ref, d_ref, mo_ref, vo_ref) = rest[:ns], rest[ns:]
        p, i = pl.program_id(0), pl.program_id(1)
        if halves:
            mine = jnp.logical_not(jnp.logical_xor(i >= nth, c_ref[0] == 1))
            blocks = [jnp.where(mine, s_refs[2 * k][...], s_refs[2 * k + 1][...]) for k in range(planes)]
        else:
            blocks = [s[...] for s in s_refs]
        g = blocks[0]
        for k in range(1, planes):
            g = jnp.where(p == k, blocks[k], g)
        g_ref[...] = g
        m_new = ADAM_B1 * m_ref[...] + (1.0 - ADAM_B1) * g
        v_new = ADAM_B2 * v_ref[...] + (1.0 - ADAM_B2) * (g * g)
        mo_ref[...] = m_new
        vo_ref[...] = v_new
        m_hat = m_new / (1.0 - ADAM_B1 ** ADAM_STEP)
        v_hat = v_new / (1.0 - ADAM_B2 ** ADAM_STEP)
        d_ref[...] = -ADAM_LR * (m_hat / (jnp.sqrt(v_hat) + ADAM_EPS) + ADAM_WD * w_ref[...])

    pt = pl.BlockSpec((None, tr, cols), lambda p, i: (p, i, 0))
    st = [pl.BlockSpec((tr, cols), functools.partial(lambda k, p, i: (jnp.where(p == k, i % nth, 0), 0), j // per))
          for j in range(ns)]
    return pl.pallas_call(
        body, name=name, grid=(planes, r // tr),
        in_specs=[pl.BlockSpec(memory_space=pltpu.SMEM), pt, pt, pt] + st,
        out_specs=[pt] * 4,
        out_shape=[jax.ShapeDtypeStruct(w.shape, F32)] * 4,
        compiler_params=_params(("arbitrary", "arbitrary")),
    )(cflag, w, m, v, *flat)


ADA_COLS = 9 * D // NCHIP


def mod_fwd(c_all, w_ada, b_shard, name):
    def body(c_ref, w_ref, b_ref, o_ref):
        cc = c_ref[...]
        sc = cc * jax.nn.sigmoid(cc)
        o_ref[...] = jnp.dot(sc, w_ref[...], preferred_element_type=F32,
                             precision=lax.Precision.HIGHEST) + b_ref[...]

    return pl.pallas_call(
        body, name=name, grid=(2,),
        in_specs=[pl.BlockSpec((NDEV, D), lambda l: (0, 0)),
                  pl.BlockSpec((None, D, ADA_COLS), lambda l: (l, 0, 0)),
                  pl.BlockSpec((None, 1, ADA_COLS), lambda l: (l, 0, 0))],
        out_specs=pl.BlockSpec((None, NDEV, ADA_COLS), lambda l: (l, 0, 0)),
        out_shape=jax.ShapeDtypeStruct((2, NDEV, ADA_COLS), F32),
        compiler_params=_params(("arbitrary",)),
    )(c_all, w_ada, b_shard.reshape(2, 1, ADA_COLS))


def wada_grad(c_all_t, dmod, name):
    ct = ADA_COLS // 3

    def body(c_ref, d_ref, o_ref):
        cc = c_ref[...]
        sc = cc * jax.nn.sigmoid(cc)
        acc = sc[:, 0:1] * d_ref[0:1, :]
        for b in range(1, NDEV):
            acc = acc + sc[:, b:b + 1] * d_ref[b:b + 1, :]
        o_ref[...] = acc

    return pl.pallas_call(
        body, name=name, grid=(2, 3),
        in_specs=[pl.BlockSpec((D, LANES), lambda l, j: (0, 0)),
                  pl.BlockSpec((None, NDEV, ct), lambda l, j: (l, 0, j))],
        out_specs=pl.BlockSpec((None, D, ct), lambda l, j: (l, 0, j)),
        out_shape=jax.ShapeDtypeStruct((2, D, ADA_COLS), F32),
        compiler_params=_params(("arbitrary", "arbitrary")),
    )(c_all_t, dmod)


def _pad_rows(row, rows=SUBLANES):
    return jnp.concatenate([row[None, :], jnp.zeros((rows - 1, row.shape[0]), row.dtype)], axis=0)


def kernel(x, c, w_ada, b_ada, norm_g, w_in, q_norm_g, k_norm_g, conv_w, conv_b, w_out, ffn_w1, ffn_w2, loss_target, m_w_ada, m_b_ada, m_norm_g, m_w_in, m_q_norm_g, m_k_norm_g, m_conv_w, m_conv_b, m_w_out, m_ffn_w1, m_ffn_w2, v_w_ada, v_b_ada, v_norm_g, v_w_in, v_q_norm_g, v_k_norm_g, v_conv_w, v_conv_b, v_w_out, v_ffn_w1, v_ffn_w2):
    ix, iy, ic = lax.axis_index("x"), lax.axis_index("y"), lax.axis_index("c")
    chip = 2 * ix + iy
    dev = 2 * chip + ic
    cflag = jnp.reshape(ic, (1,)).astype(jnp.int32)
    ngw = norm_g.shape[-1]
    cww = conv_w.shape[-1]

    pack = jnp.concatenate([c[0], norm_g.reshape(-1), conv_w.reshape(-1)])
    got = small_all_gather(_pad_rows(pack), "gather_c_normg_convw")[::SUBLANES]
    c_all = got[:, :D]
    per_chip = got[::2]
    ng_full = jnp.concatenate([per_chip[j, D:D + 6 * ngw].reshape(2, 3, ngw) for j in range(NCHIP)], axis=-1)
    cw_full = jnp.concatenate([per_chip[j, D + 6 * ngw:].reshape(2, 3, cww) for j in range(NCHIP)], axis=-1)

    b_shard = lax.dynamic_slice_in_dim(b_ada, chip * ADA_COLS, ADA_COLS, axis=1)
    mod_blk = mod_fwd(c_all, w_ada, b_shard, "mod_fwd").reshape(2 * NDEV, ADA_COLS)
    mod_all = small_all_gather(mod_blk, "gather_mod").reshape(NDEV, 2, NDEV, ADA_COLS)[::2]
    mod_mine = lax.dynamic_index_in_dim(mod_all, dev, axis=2, keepdims=False)
    mods = [mod_mine[:, l, :].reshape(-1) for l in range(2)]

    shards, gvecs, cws = [], [], []
    for l in range(2):
        shards.append(dict(w1=[ffn_w1[l, i].astype(MXU_DTYPE) for i in range(2)],
                           w2=[ffn_w2[l, i].astype(MXU_DTYPE) for i in range(2)],
                           win=w_in[l].astype(MXU_DTYPE), wout=w_out[l].astype(MXU_DTYPE)))
        gv = jnp.stack([jnp.tile(q_norm_g[l], AW // HD), jnp.tile(k_norm_g[l], AW // HD)])
        gvecs.append(jnp.concatenate([gv, jnp.zeros((SUBLANES - 2, AW), F32)], axis=0))
        cws.append(jnp.concatenate([cw_full[l], conv_b[l][None, :], jnp.zeros((SUBLANES - 4, CW), F32)], axis=0))
    w_first = gather_split([shards[0]["w1"][0], shards[0]["w2"][0]], "gather_first_ffn")

    loss_blk, dx, totals, sums = local_step(x[0], loss_target[0], mods, [ng_full[0], ng_full[1]], gvecs, cws,
                                            shards, w_first, cflag)

    dmods, dngs, dqg, dkg, dcw, dcb = [], [], [], [], [], []
    for l in range(2):
        s0, s1, so, s2, sm = sums[l]
        dmods.append(jnp.concatenate([s0[0], s0[1], s0[3], s1[0], s1[1], so[0], s2[0], s2[1], s2[3]]))
        dngs.append(jnp.concatenate([s0[2], s1[2], s2[2]]))
        dqg.append(sm[0].reshape(AW // HD, HD).sum(0))
        dkg.append(sm[1].reshape(AW // HD, HD).sum(0))
        dcw.append(sm[2:5].reshape(-1))
        dcb.append(sm[5])
    small = jnp.concatenate(dmods + dngs + dqg + dkg + dcw + dcb + [loss_blk[0]])
    small_all = small_all_gather(_pad_rows(small), "gather_small_grads")[::SUBLANES]
    nm = 9 * D
    dmod_all = small_all[:, :2 * nm].reshape(NDEV, 2, NCHIP, ADA_COLS)
    dmod_mine = lax.dynamic_index_in_dim(dmod_all, chip, axis=2, keepdims=False).transpose(1, 0, 2)
    tot = sum_devices(small_all, "sum_small_grads")[0]
    o = 2 * nm
    g_b_ada = tot[:o].reshape(2, nm)
    g_norm_g = lax.dynamic_slice_in_dim(tot[o:o + 6 * D].reshape(2, 3, D), chip * ngw, ngw, axis=2)
    o += 6 * D
    g_qg = tot[o:o + 2 * HD].reshape(2, HD)
    o += 2 * HD
    g_kg = tot[o:o + 2 * HD].reshape(2, HD)
    o += 2 * HD
    g_cw = lax.dynamic_slice_in_dim(tot[o:o + 6 * CW].reshape(2, 3, CW), chip * cww, cww, axis=2)
    o += 6 * CW
    g_cb = tot[o:o + 2 * CW].reshape(2, CW)
    loss = tot[o + 2 * CW]

    c_all_t = jnp.concatenate([c_all.T, jnp.zeros((D, LANES - NDEV), F32)], axis=1)
    g_wada_src = wada_grad(c_all_t, dmod_mine, "wada_grad")

    def halves(k_of_plane):
        return [(totals[l][0][k], totals[l][1][k]) for l, k in k_of_plane]

    r_wada = adamw(w_ada, m_w_ada, v_w_ada, [g_wada_src[0], g_wada_src[1]], cflag, "adamw_w_ada")
    r_win = adamw(w_in, m_w_in, v_w_in, halves([(0, 2), (1, 2)]), cflag, "adamw_w_in", halves=True)
    r_wout = adamw(w_out, m_w_out, v_w_out, halves([(0, 3), (1, 3)]), cflag, "adamw_w_out", halves=True)
    r_w1 = adamw(ffn_w1.reshape(4, D, HALF), m_ffn_w1.reshape(4, D, HALF), v_ffn_w1.reshape(4, D, HALF),
                 halves([(0, 0), (0, 4), (1, 0), (1, 4)]), cflag, "adamw_ffn_w1", halves=True)
    w2r = DFF // NCHIP
    r_w2 = adamw(ffn_w2.reshape(4, w2r, D), m_ffn_w2.reshape(4, w2r, D), v_ffn_w2.reshape(4, w2r, D),
                 halves([(0, 1), (0, 5), (1, 1), (1, 5)]), cflag, "adamw_ffn_w2", halves=True)
    r_w1 = [t.reshape(ffn_w1.shape) for t in r_w1]
    r_w2 = [t.reshape(ffn_w2.shape) for t in r_w2]

    smalls = [("b_ada", b_ada, m_b_ada, v_b_ada, g_b_ada), ("norm_g", norm_g, m_norm_g, v_norm_g, g_norm_g),
              ("q_norm_g", q_norm_g, m_q_norm_g, v_q_norm_g, g_qg), ("k_norm_g", k_norm_g, m_k_norm_g, v_k_norm_g, g_kg),
              ("conv_w", conv_w, m_conv_w, v_conv_w, g_cw), ("conv_b", conv_b, m_conv_b, v_conv_b, g_cb)]
    n_small = sum(t[1].size for t in smalls)
    pad = (-n_small) % (16 * LANES)

    def packed(idx):
        flat = jnp.concatenate([t[idx].reshape(-1) for t in smalls] + [jnp.zeros((pad,), F32)])
        return flat.reshape(-1, LANES)

    r_small = adamw(packed(1)[None], packed(2)[None], packed(3)[None], [packed(4)], cflag, "adamw_small")
    small_out = {}
    o = 0
    for name_, w_, _, _, _ in smalls:
        small_out[name_] = [t.reshape(-1)[o:o + w_.size].reshape(w_.shape) for t in r_small]
        o += w_.size

    res = {"w_ada": r_wada, "w_in": r_win, "w_out": r_wout, "ffn_w1": r_w1, "ffn_w2": r_w2, **small_out}
    order = ["w_ada", "b_ada", "norm_g", "w_in", "q_norm_g", "k_norm_g", "conv_w", "conv_b", "w_out", "ffn_w1", "ffn_w2"]
    outs = [loss, dx[None]]
    for k in range(4):
        outs += [res[nm_][k] for nm_ in order]
    return tuple(outs)
```

```python
import functools

import jax
import jax.numpy as jnp
from jax import lax
from jax.experimental import pallas as pl
from jax.experimental.pallas import tpu as pltpu

F32 = jnp.float32
MXU_DTYPE = jnp.bfloat16
ACT_DTYPE = jnp.bfloat16
WIRE_DTYPE = jnp.bfloat16

D = 1024
HD = 64
AW = 512
CW = 512
DFF = 2816
HALF = DFF // 2
INC = 3 * AW + 3 * CW
NCHIP = 4
NDEV = 8
QBLK = 128
ATTN_QBLOCKS = 16
ATTN_INTERLEAVE = 16
ATTN_CHUNK_ROWS = 4096
DILATIONS = (1, 4, 16)
EPS = 1e-6
NEG = -1e30
LANES = 128
SUBLANES = 8
HALO_ROWS = 16
VMEM_LIMIT = 56 * 1024 * 1024

ADAM_LR = 0.001
ADAM_B1 = 0.9
ADAM_B2 = 0.999
ADAM_EPS = 1e-08
ADAM_WD = 0.01
ADAM_STEP = 10

NT_DIMS = (((1,), (1,)), ((), ()))
TN_DIMS = (((0,), (0,)), ((), ()))


def _params(sem, vmem=VMEM_LIMIT):
    return pltpu.CompilerParams(dimension_semantics=sem, vmem_limit_bytes=vmem)


def _row_tile(n, want):
    t = min(n, want)
    assert n % t == 0
    return t


def _ada(xt, vec_ref):
    ng, sc, sh, gt = vec_ref[0:1, :], vec_ref[1:2, :], vec_ref[2:3, :], vec_ref[3:4, :]
    r = lax.rsqrt(jnp.mean(xt * xt, axis=-1, keepdims=True) + EPS)
    return xt * r, r, ng * (1.0 + sc), ng, sc, sh, gt


def _ada_bwd(dh, xhat, r, gain, ng, sc):
    dshift = jnp.sum(dh, axis=0, keepdims=True)
    dhx = dh * xhat
    dscale = jnp.sum(dhx, axis=0, keepdims=True) * ng
    dng = jnp.sum(dhx, axis=0, keepdims=True) * (1.0 + sc)
    dxhat = dh * gain
    dx = r * (dxhat - xhat * jnp.mean(dxhat * xhat, axis=-1, keepdims=True))
    return dx, dshift, dscale, dng


def _acc_rows(sums_ref, first, rows):
    @pl.when(first)
    def _():
        sums_ref[...] = jnp.zeros_like(sums_ref)
    for k, row in enumerate(rows):
        sums_ref[k:k + 1, :] += row


MESH = pl.DeviceIdType.MESH
ANY = pl.BlockSpec(memory_space=pl.ANY)


def _here():
    return lax.axis_index("x"), lax.axis_index("y"), lax.axis_index("c")


def _ici_copies(src_refs, dst_refs, send_sems, recv_sems, local_sems, scatter):
    x, y, c = _here()
    my_chip = 2 * x + y
    peers = [(1 - x, y), (x, 1 - y), (1 - x, 1 - y)]
    local, out, inc = [], [], []
    for a, (src, dst) in enumerate(zip(src_refs, dst_refs)):
        local.append(pltpu.make_async_copy(src.at[my_chip] if scatter else src, dst.at[my_chip], local_sems.at[a]))
        for j, (px, py) in enumerate(peers):
            sems = dict(send_sem=send_sems.at[3 * a + j], recv_sem=recv_sems.at[3 * a + j],
                        device_id=(px, py, c), device_id_type=MESH)
            out.append(pltpu.make_async_remote_copy(
                src_ref=src.at[2 * px + py] if scatter else src, dst_ref=dst.at[my_chip], **sems))
            inc.append(pltpu.make_async_remote_copy(
                src_ref=src.at[my_chip] if scatter else src, dst_ref=dst.at[2 * px + py], **sems))
    return local, out, inc


def _swap_copies(src_refs, dst_refs, send_sems, recv_sems, halves):
    x, y, c = _here()
    cps = []
    for k, (src, dst) in enumerate(zip(src_refs, dst_refs)):
        if halves:
            r2 = src.shape[1] // 2
            src = src.at[:, pl.ds((1 - c) * r2, r2), :]
        cps.append(pltpu.make_async_remote_copy(
            src_ref=src, dst_ref=dst, send_sem=send_sems.at[k], recv_sem=recv_sems.at[k],
            device_id=(x, y, 1 - c), device_id_type=MESH))
    return cps


class Carry:
    def __init__(self, kind, srcs):
        self.kind, self.srcs, n = kind, list(srcs), len(srcs)
        if kind == "gather":
            shapes = [(NCHIP,) + s.shape for s in srcs]
        elif kind == "swap_halves":
            shapes = [(s.shape[0], s.shape[1] // 2, s.shape[2]) for s in srcs]
        else:
            shapes = [s.shape for s in srcs]
        self.out_shape = [jax.ShapeDtypeStruct(sh, s.dtype) for sh, s in zip(shapes, srcs)]
        dma = pltpu.SemaphoreType.DMA
        self.sems = [dma((3 * n,)), dma((3 * n,)), dma((n,))] if kind in ("gather", "scatter") else [dma((n,)), dma((n,))]

    def start(self, srcs, dsts, sems):
        if self.kind in ("gather", "scatter"):
            local, out, _ = _ici_copies(srcs, dsts, *sems, self.kind == "scatter")
            for cp in local + out:
                cp.start()
        else:
            for cp in _swap_copies(srcs, dsts, *sems, self.kind == "swap_halves"):
                cp.start()

    def wait(self, srcs, dsts, sems):
        if self.kind in ("gather", "scatter"):
            local, out, inc = _ici_copies(srcs, dsts, *sems, self.kind == "scatter")
            for cp in inc:
                cp.wait_recv()
            for cp in out:
                cp.wait_send()
            for cp in local:
                cp.wait()
        else:
            cps = _swap_copies(srcs, dsts, *sems, self.kind == "swap_halves")
            for cp in cps:
                cp.wait_recv()
            for cp in cps:
                cp.wait_send()


def run_carry(carry, name):
    n = len(carry.srcs)

    def body(*refs):
        srcs, dsts, sems = refs[:n], refs[n:2 * n], refs[2 * n:]
        carry.start(srcs, dsts, sems)
        carry.wait(srcs, dsts, sems)

    return pl.pallas_call(body, name=name, out_shape=carry.out_shape, in_specs=[ANY] * n, out_specs=[ANY] * n,
                          scratch_shapes=carry.sems)(*carry.srcs)


def gather_split(srcs, name):
    n = len(srcs)

    def body(*refs):
        src_refs, dst_refs = refs[:n], refs[n:2 * n]
        send_sems, recv_sems, fwd_send, fwd_recv, local_sems = refs[2 * n:]
        x, y, c = _here()
        my_chip = 2 * x + y
        peers = [(1 - x, y), (x, 1 - y), (1 - x, 1 - y)]

        def half(ref, h):
            r2 = ref.shape[0] // 2
            return ref.at[pl.ds(h * r2, r2), :]

        local, out, landed, passed, arriving = [], [], [], [], []
        for a, (src, dst) in enumerate(zip(src_refs, dst_refs)):
            local.append(pltpu.make_async_copy(src, dst.at[my_chip], local_sems.at[a]))
            for j, (px, py) in enumerate(peers):
                k = 3 * a + j
                theirs = dst.at[2 * px + py]
                ici = dict(send_sem=send_sems.at[k], recv_sem=recv_sems.at[k], device_id=(px, py, c), device_id_type=MESH)
                d2d = dict(send_sem=fwd_send.at[k], recv_sem=fwd_recv.at[k], device_id=(x, y, 1 - c), device_id_type=MESH)
                out.append(pltpu.make_async_remote_copy(src_ref=half(src, c), dst_ref=half(dst.at[my_chip], c), **ici))
                landed.append(pltpu.make_async_remote_copy(src_ref=half(src, c), dst_ref=half(theirs, c), **ici))
                passed.append(pltpu.make_async_remote_copy(src_ref=half(theirs, c), dst_ref=half(theirs, c), **d2d))
                arriving.append(pltpu.make_async_remote_copy(src_ref=half(theirs, c), dst_ref=half(theirs, 1 - c), **d2d))
        for cp in local + out:
            cp.start()
        for got, fwd in zip(landed, passed):
            got.wait_recv()
            fwd.start()
        for cp in arriving:
            cp.wait_recv()
        for cp in out + passed:
            cp.wait_send()
        for cp in local:
            cp.wait()

    dma = pltpu.SemaphoreType.DMA
    return pl.pallas_call(
        body, name=name, out_shape=[jax.ShapeDtypeStruct((NCHIP,) + s.shape, s.dtype) for s in srcs],
        in_specs=[ANY] * n, out_specs=[ANY] * n,
        scratch_shapes=[dma((3 * n,)), dma((3 * n,)), dma((3 * n,)), dma((3 * n,)), dma((n,))],
    )(*srcs)


def _pcall(body, name, grid, in_specs, out_specs, out_shape, sem, args, carry=None, scratch=()):
    if carry is None:
        outs = pl.pallas_call(body, name=name, grid=grid, in_specs=in_specs, out_specs=out_specs,
                              out_shape=out_shape, scratch_shapes=list(scratch), compiler_params=_params(sem))(*args)
        return outs, []
    n_in, n_out, nc, ns = len(in_specs), len(out_specs), len(carry.srcs), len(scratch)

    def wrapped(*refs):
        ins, csrc = refs[:n_in], refs[n_in:n_in + nc]
        outs, cdst = refs[n_in + nc:n_in + nc + n_out], refs[n_in + nc + n_out:n_in + 2 * nc + n_out]
        own = refs[n_in + 2 * nc + n_out:n_in + 2 * nc + n_out + ns]
        sems = refs[n_in + 2 * nc + n_out + ns:]
        ids = [pl.program_id(a) for a in range(len(grid))]
        first = functools.reduce(jnp.logical_and, [i == 0 for i in ids])
        last = functools.reduce(jnp.logical_and, [i == g - 1 for i, g in zip(ids, grid)])

        @pl.when(first)
        def _():
            carry.start(csrc, cdst, sems)

        body(*ins, *outs, *own)

        @pl.when(last)
        def _():
            carry.wait(csrc, cdst, sems)

    res = pl.pallas_call(
        wrapped, name=name, grid=grid,
        in_specs=list(in_specs) + [ANY] * nc, out_specs=list(out_specs) + [ANY] * nc,
        out_shape=list(out_shape) + carry.out_shape,
        scratch_shapes=list(scratch) + carry.sems, compiler_params=_params(sem),
    )(*args, *carry.srcs)
    return res[:n_out], res[n_out:]


def ffn_fwd(x, vec, w1p, w2, gs, name, carry=None, target=None):
    S = x.shape[0]
    tm = _row_tile(S, 512)

    def body(x_ref, *refs):
        if target is None:
            vec_ref, w1_ref, w2_ref, xn_ref, a_ref, f_ref = refs
        else:
            t_ref, vec_ref, w1_ref, w2_ref, xn_ref, a_ref, f_ref, l_ref = refs
        xt = x_ref[...]
        xhat, _, gain, _, _, sh, gt = _ada(xt, vec_ref)
        h = (xhat * gain + sh).astype(MXU_DTYPE)
        f = jnp.zeros((tm, D), F32)
        for hf in range(2):
            g = jnp.dot(h, w1_ref[hf], preferred_element_type=F32)
            up = jnp.dot(h, w1_ref[2 + hf], preferred_element_type=F32)
            a_ref[:, hf * HALF:(hf + 1) * HALF] = g.astype(a_ref.dtype)
            a_ref[:, DFF + hf * HALF:DFF + (hf + 1) * HALF] = up.astype(a_ref.dtype)
            act = (g * jax.nn.sigmoid(g) * up).astype(MXU_DTYPE)
            f = f + jnp.dot(act, w2_ref[hf * HALF:(hf + 1) * HALF, :], preferred_element_type=F32)
        f_ref[...] = f.astype(f_ref.dtype)
        xn = xt + (gs * gt) * f
        if target is None:
            xn_ref[...] = xn
        else:
            diff = xn - t_ref[...]
            xn_ref[...] = diff * (1.0 / D)
            part = jnp.sum(jnp.sum(diff * diff, axis=0, keepdims=True), axis=1, keepdims=True) * (0.5 / D)

            @pl.when(pl.program_id(0) == 0)
            def _():
                l_ref[...] = jnp.zeros_like(l_ref)
            l_ref[...] += jnp.broadcast_to(part, l_ref.shape)

    tile = pl.BlockSpec((tm, D), lambda i: (i, 0))
    last = target is not None
    return _pcall(
        body, name, (S // tm,),
        [tile] * (2 if last else 1) + [
            pl.BlockSpec((SUBLANES, D), lambda i: (0, 0)),
            pl.BlockSpec((NCHIP, D, HALF), lambda i: (0, 0, 0), pipeline_mode=pl.Buffered(1)),
            pl.BlockSpec((DFF, D), lambda i: (0, 0), pipeline_mode=pl.Buffered(1))],
        [tile, pl.BlockSpec((tm, 2 * DFF), lambda i: (i, 0)), tile]
        + ([pl.BlockSpec((SUBLANES, LANES), lambda i: (0, 0))] if last else []),
        [jax.ShapeDtypeStruct((S, D), F32),
         jax.ShapeDtypeStruct((S, 2 * DFF), ACT_DTYPE),
         jax.ShapeDtypeStruct((S, D), ACT_DTYPE)]
        + ([jax.ShapeDtypeStruct((SUBLANES, LANES), F32)] if last else []),
        ("arbitrary",), (x, target, vec, w1p, w2) if last else (x, vec, w1p, w2), carry)


def ffn_bwd(dxo, x, a, f, vec, w1p, w2, gs, name, carry=None):
    S = x.shape[0]
    tm = _row_tile(S, 256)

    def body(dxo_ref, x_ref, a_ref, f_ref, vec_ref, w1_ref, w2_ref,
             dxi_ref, hb_ref, dfb_ref, act_ref, da_ref, sums_ref):
        xt = x_ref[...]
        dxo = dxo_ref[...]
        xhat, r, gain, ng, sc, sh, gt = _ada(xt, vec_ref)
        hb_ref[...] = (xhat * gain + sh).astype(hb_ref.dtype)
        dgate = gs * jnp.sum(dxo * f_ref[...].astype(F32), axis=0, keepdims=True)
        df = ((gs * gt) * dxo).astype(MXU_DTYPE)
        dfb_ref[...] = df
        dh = jnp.zeros((tm, D), F32)
        for hf in range(2):
            lo, hi = hf * HALF, (hf + 1) * HALF
            dact = lax.dot_general(df, w2_ref[lo:hi, :], NT_DIMS, preferred_element_type=F32)
            g = a_ref[:, lo:hi].astype(F32)
            up = a_ref[:, DFF + lo:DFF + hi].astype(F32)
            sg = jax.nn.sigmoid(g)
            si = g * sg
            act_ref[:, lo:hi] = (si * up).astype(act_ref.dtype)
            dg = (dact * up * (sg * (1.0 + g * (1.0 - sg)))).astype(MXU_DTYPE)
            dup = (dact * si).astype(MXU_DTYPE)
            da_ref[:, lo:hi] = dg
            da_ref[:, DFF + lo:DFF + hi] = dup
            dh = dh + lax.dot_general(dg, w1_ref[hf], NT_DIMS, preferred_element_type=F32)
            dh = dh + lax.dot_general(dup, w1_ref[2 + hf], NT_DIMS, preferred_element_type=F32)
        dx, dshift, dscale, dng = _ada_bwd(dh, xhat, r, gain, ng, sc)
        dxi_ref[...] = dxo + dx
        _acc_rows(sums_ref, pl.program_id(0) == 0, (dshift, dscale, dng, dgate))

    return _pcall(
        body, name, (S // tm,),
        [pl.BlockSpec((tm, D), lambda i: (i, 0)),
         pl.BlockSpec((tm, D), lambda i: (i, 0)),
         pl.BlockSpec((tm, 2 * DFF), lambda i: (i, 0)),
         pl.BlockSpec((tm, D), lambda i: (i, 0)),
         pl.BlockSpec((SUBLANES, D), lambda i: (0, 0)),
         pl.BlockSpec((NCHIP, D, HALF), lambda i: (0, 0, 0), pipeline_mode=pl.Buffered(1)),
         pl.BlockSpec((DFF, D), lambda i: (0, 0), pipeline_mode=pl.Buffered(1))],
        [pl.BlockSpec((tm, D), lambda i: (i, 0)),
         pl.BlockSpec((tm, D), lambda i: (i, 0)),
         pl.BlockSpec((tm, D), lambda i: (i, 0)),
         pl.BlockSpec((tm, DFF), lambda i: (i, 0)),
         pl.BlockSpec((tm, 2 * DFF), lambda i: (i, 0)),
         pl.BlockSpec((SUBLANES, D), lambda i: (0, 0))],
        [jax.ShapeDtypeStruct((S, D), F32),
         jax.ShapeDtypeStruct((S, D), MXU_DTYPE),
         jax.ShapeDtypeStruct((S, D), MXU_DTYPE),
         jax.ShapeDtypeStruct((S, DFF), MXU_DTYPE),
         jax.ShapeDtypeStruct((S, 2 * DFF), MXU_DTYPE),
         jax.ShapeDtypeStruct((SUBLANES, D), F32)],
        ("arbitrary",), (dxo, x, a, f, vec, w1p, w2), carry)


def wgrad(a, b, kt, nt, name, carry=None):
    T, K = a.shape
    N = b.shape[1]
    pk, pn = K // kt, N // nt
    assert pk == 1 or pn == 1
    tt = _row_tile(T, 2048)
    steps = T // tt

    def body(a_ref, b_ref, o_ref):
        @pl.when(pl.program_id(1) == 0)
        def _():
            o_ref[...] = jnp.zeros_like(o_ref)
        o_ref[...] += lax.dot_general(a_ref[...], b_ref[...], TN_DIMS, preferred_element_type=F32)

    a_map = (lambda p, t: (t, p)) if pk > 1 else (lambda p, t: (t, 0))
    b_map = (lambda p, t: (t, p)) if pn > 1 else (lambda p, t: (t, 0))
    (out,), got = _pcall(
        body, name, (pk * pn, steps),
        [pl.BlockSpec((tt, kt), a_map), pl.BlockSpec((tt, nt), b_map)],
        [pl.BlockSpec((None, kt, nt), lambda p, t: (p, 0, 0))],
        [jax.ShapeDtypeStruct((pk * pn, kt, nt), F32)], ("arbitrary", "arbitrary"), (a, b), carry)
    return out, got


def _head_masks(rows):
    lane = lax.broadcasted_iota(jnp.int32, (rows, LANES), 1)
    return lane < HD


def _pair_stat(x, m_a):
    s_a = jnp.sum(jnp.where(m_a, x, 0.0), axis=1, keepdims=True)
    s_b = jnp.sum(jnp.where(m_a, 0.0, x), axis=1, keepdims=True)
    return s_a, s_b


def mixer_in(x, vec, winp, gvec, name, carry=None):
    S = x.shape[0]
    tm = _row_tile(S, 512)
    pc = INC // NCHIP

    def body(x_ref, vec_ref, w_ref, g_ref, proj_ref, hb_ref, qn_ref, kn_ref, v_ref, qkv_ref):
        xt = x_ref[...]
        xhat, _, gain, _, _, sh, _ = _ada(xt, vec_ref)
        h = (xhat * gain + sh).astype(MXU_DTYPE)
        hb_ref[...] = h
        for j in range(NCHIP):
            piece = jnp.dot(h, w_ref[j], preferred_element_type=F32)
            proj_ref[:, j * pc:(j + 1) * pc] = piece.astype(proj_ref.dtype)
            if (j + 1) * pc <= 3 * AW:
                qkv_ref[:, j * pc:(j + 1) * pc] = piece
        m_a = _head_masks(tm)
        for which, dst in ((0, qn_ref), (1, kn_ref)):
            for p in range(AW // LANES):
                lo = which * AW + p * LANES
                xp = qkv_ref[:, lo:lo + LANES]
                s_a, s_b = _pair_stat(xp * xp, m_a)
                rr = jnp.where(m_a, lax.rsqrt(s_a * (1.0 / HD) + EPS), lax.rsqrt(s_b * (1.0 / HD) + EPS))
                gp = g_ref[which:which + 1, p * LANES:(p + 1) * LANES]
                dst[:, p * LANES:(p + 1) * LANES] = (xp * rr * gp).astype(dst.dtype)
        v_ref[...] = qkv_ref[:, 2 * AW:3 * AW]

    assert 2 * pc == 3 * AW
    return _pcall(
        body, name, (S // tm,),
        [pl.BlockSpec((tm, D), lambda i: (i, 0)),
         pl.BlockSpec((SUBLANES, D), lambda i: (0, 0)),
         pl.BlockSpec((NCHIP, D, pc), lambda i: (0, 0, 0), pipeline_mode=pl.Buffered(1)),
         pl.BlockSpec((SUBLANES, AW), lambda i: (0, 0))],
        [pl.BlockSpec((tm, INC), lambda i: (i, 0)),
         pl.BlockSpec((tm, D), lambda i: (i, 0)),
         pl.BlockSpec((tm, AW), lambda i: (i, 0)),
         pl.BlockSpec((tm, AW), lambda i: (i, 0)),
         pl.BlockSpec((tm, AW), lambda i: (i, 0))],
        [jax.ShapeDtypeStruct((S, INC), ACT_DTYPE),
         jax.ShapeDtypeStruct((S, D), MXU_DTYPE),
         jax.ShapeDtypeStruct((S, AW), F32),
         jax.ShapeDtypeStruct((S, AW), F32),
         jax.ShapeDtypeStruct((S, AW), F32)],
        ("arbitrary",), (x, vec, winp, gvec), carry, scratch=[pltpu.VMEM((tm, 3 * AW), F32)])


def _band_masks(ncol):
    row = lax.broadcasted_iota(jnp.int32, (2 * QBLK, ncol), 0) & (QBLK - 1)
    col = lax.broadcasted_iota(jnp.int32, (2 * QBLK, ncol), 1)
    return row, col


def _stack_heads(t, m_a):
    zero = jnp.zeros_like(t)
    return jnp.concatenate([jnp.where(m_a, t, zero), jnp.where(m_a, zero, t)], axis=0)


class _AttnLayout:
    def __init__(self, d, S):
        self.d, self.S = d, S
        self.qb = max(1, min(ATTN_QBLOCKS, ATTN_CHUNK_ROWS // (QBLK * d)))
        self.nres = d
        self.nchunk = S // (self.qb * QBLK * d)
        self.grid = (AW // LANES, self.nchunk)
        self.unroll = max(1, min(d, ATTN_INTERLEAVE // self.qb))

    def _spec(self, blocks, row_of):
        return pl.BlockSpec((blocks * QBLK * self.d, LANES), lambda hp, j: (row_of(j), hp))

    def cur(self, chunk_of):
        return self._spec(self.qb, chunk_of)

    def prev(self, chunk_of):
        return self._spec(1, lambda j: jnp.maximum(chunk_of(j) * self.qb - 1, 0))

    def idx(self, b, r):
        if self.d == 1:
            return (pl.ds(b * QBLK, QBLK), slice(None))
        return (pl.ds(b * QBLK * self.d + r, QBLK, stride=self.d), slice(None))

    def per_residue(self, fn):
        if self.nres == 1:
            fn(0)
        else:
            def step(it, carry):
                for k in range(self.unroll):
                    fn(it * self.unroll + k)
                return carry
            lax.fori_loop(0, self.nres // self.unroll, step, 0)


def attn_fwd(qn, kn, v, d, name, carry=None):
    S = qn.shape[0]
    lay = _AttnLayout(d, S)
    qb = lay.qb

    def body(q_ref, kc_ref, kp_ref, vc_ref, vp_ref, o_ref, lse_ref):
        i = pl.program_id(1)
        m_a = _head_masks(QBLK)
        row, col = _band_masks(2 * QBLK)
        dist = row + QBLK - col
        band = (dist >= 0) & (dist <= QBLK)
        first = band & ((i > 0) | (col >= QBLK))

        def residue(r):
            kt = [kp_ref[lay.idx(0, r)].astype(MXU_DTYPE)]
            vt = [vp_ref[lay.idx(0, r)].astype(MXU_DTYPE)]
            for b in range(qb):
                kt.append(kc_ref[lay.idx(b, r)].astype(MXU_DTYPE))
                vt.append(vc_ref[lay.idx(b, r)].astype(MXU_DTYPE))
            for b in range(qb):
                rows = lay.idx(b, r)
                q = (q_ref[rows] * (HD ** -0.5)).astype(MXU_DTYPE)
                kcat = jnp.concatenate([kt[b], kt[b + 1]], axis=0)
                vcat = jnp.concatenate([vt[b], vt[b + 1]], axis=0)
                mask = first if b == 0 else band
                s = lax.dot_general(_stack_heads(q, m_a), kcat, NT_DIMS, preferred_element_type=F32)
                s = jnp.where(mask, s, NEG)
                m = jnp.max(s, axis=1, keepdims=True)
                p = jnp.exp(s - m)
                l = jnp.sum(p, axis=1, keepdims=True)
                o = jnp.dot(p.astype(MXU_DTYPE), vcat, preferred_element_type=F32) / l
                lse = jnp.broadcast_to(m + jnp.log(l), (2 * QBLK, LANES))
                o_ref[rows] = jnp.where(m_a, o[:QBLK], o[QBLK:])
                lse_ref[rows] = jnp.where(m_a, lse[:QBLK], lse[QBLK:])

        lay.per_residue(residue)

    cur, prev = lay.cur(lambda j: j), lay.prev(lambda j: j)
    return _pcall(body, name, lay.grid, [cur, cur, prev, cur, prev], [cur, cur],
                  [jax.ShapeDtypeStruct((S, AW), F32)] * 2, ("arbitrary", "arbitrary"), (qn, kn, kn, v, v), carry)


def _both_heads(t, m_a):
    other = pltpu.roll(t, HD, 1)
    return jnp.concatenate([jnp.where(m_a, t, other), jnp.where(m_a, other, t)], axis=0)


def attn_bwd(qn, kn, v, dycat, lse, delta, d, name, carry=None):
    S = qn.shape[0]
    lay = _AttnLayout(d, S)
    qb, nchunk = lay.qb, lay.nchunk

    def body(q_ref, kc_ref, kp_ref, vc_ref, vp_ref, do_ref, lse_ref, dl_ref,
             dq_ref, dk_ref, dv_ref, ck_ref, cv_ref):
        j = pl.program_id(1)
        i = nchunk - 1 - j
        m_a = _head_masks(QBLK)
        row, col = _band_masks(2 * QBLK)
        dist = row + QBLK - col
        band = (dist >= 0) & (dist <= QBLK)
        first = band & ((i > 0) | (col >= QBLK))

        def residue(r):
            def tiles(ref, cast):
                out = [ref[lay.idx(b, r)] for b in range(qb)]
                return [t.astype(MXU_DTYPE) for t in out] if cast else out

            def ktiles(cur_ref, prev_ref):
                return [prev_ref[lay.idx(0, r)].astype(MXU_DTYPE)] + tiles(cur_ref, True)

            qt = [(t * (HD ** -0.5)).astype(MXU_DTYPE) for t in tiles(q_ref, False)]
            dot_ = tiles(do_ref, True)
            lse_t = tiles(lse_ref, False)
            dl_t = tiles(dl_ref, False)
            kt = ktiles(kc_ref, kp_ref)
            vt = ktiles(vc_ref, vp_ref)
            dk_acc = [jnp.zeros((QBLK, LANES), F32) for _ in range(qb)]
            dv_acc = [jnp.zeros((QBLK, LANES), F32) for _ in range(qb)]
            crow = pl.ds(0, QBLK) if lay.nres == 1 else pl.ds(pl.multiple_of(r * QBLK, QBLK), QBLK)
            dk_acc[qb - 1] = jnp.where(j > 0, ck_ref[crow, :], 0.0)
            dv_acc[qb - 1] = jnp.where(j > 0, cv_ref[crow, :], 0.0)
            for x in range(qb):
                kcat = jnp.concatenate([kt[x], kt[x + 1]], axis=0)
                vcat = jnp.concatenate([vt[x], vt[x + 1]], axis=0)
                q2 = _stack_heads(qt[x], m_a)
                do2 = _stack_heads(dot_[x], m_a)
                lse2 = _both_heads(lse_t[x], m_a)
                dl2 = _both_heads(dl_t[x], m_a)
                lse2 = jnp.concatenate([lse2, lse2], axis=1)
                dl2 = jnp.concatenate([dl2, dl2], axis=1)
                s = lax.dot_general(q2, kcat, NT_DIMS, preferred_element_type=F32)
                p = jnp.exp(jnp.where(first if x == 0 else band, s, NEG) - lse2)
                dp = lax.dot_general(do2, vcat, NT_DIMS, preferred_element_type=F32)
                ds = p * (dp - dl2)
                dq = jnp.dot(ds.astype(MXU_DTYPE), kcat, preferred_element_type=F32)
                dq_ref[lay.idx(x, r)] = jnp.where(m_a, dq[:QBLK], dq[QBLK:]) * (HD ** -0.5)
                dk = jnp.dot(ds.T.astype(MXU_DTYPE), q2, preferred_element_type=F32)
                dv = jnp.dot(p.T.astype(MXU_DTYPE), do2, preferred_element_type=F32)
                if x == 0:
                    ck_ref[crow, :] = dk[:QBLK]
                    cv_ref[crow, :] = dv[:QBLK]
                else:
                    dk_acc[x - 1] = dk_acc[x - 1] + dk[:QBLK]
                    dv_acc[x - 1] = dv_acc[x - 1] + dv[:QBLK]
                dk_acc[x] = dk_acc[x] + dk[QBLK:]
                dv_acc[x] = dv_acc[x] + dv[QBLK:]
            for kb in range(qb):
                dk_ref[lay.idx(kb, r)] = dk_acc[kb]
                dv_ref[lay.idx(kb, r)] = dv_acc[kb]

        lay.per_residue(residue)

    cur, prev = lay.cur(lambda j: nchunk - 1 - j), lay.prev(lambda j: nchunk - 1 - j)
    carried = pltpu.VMEM((lay.nres * QBLK, LANES), F32)
    return _pcall(
        body, name, lay.grid, [cur, cur, prev, cur, prev, cur, cur, cur], [cur, cur, cur],
        [jax.ShapeDtypeStruct((S, AW), F32)] * 3, ("arbitrary", "arbitrary"),
        (qn, kn, kn, v, v, dycat, lse, delta), carry, scratch=[carried, carried])


def _shift_down(x, halo_prev, k, row):
    tm = x.shape[0]
    tail = jnp.concatenate([pltpu.roll(halo_prev, k, 0), jnp.zeros((tm - SUBLANES, x.shape[1]), x.dtype)], axis=0)
    return jnp.where(row < k, tail, pltpu.roll(x, k, 0))


def _shift_up(x, halo_next, k, row):
    tm = x.shape[0]
    head = jnp.concatenate([jnp.zeros((tm - SUBLANES, x.shape[1]), x.dtype), pltpu.roll(halo_next, SUBLANES - k, 0)], axis=0)
    return jnp.where(row >= tm - k, head, pltpu.roll(x, tm - k, 0))


def _conv_fwd(cu, halo_cu, cw_ref, row):
    u1 = _shift_down(cu, halo_cu, 1, row)
    u2 = _shift_down(cu, halo_cu, 2, row)
    cv = cw_ref[0:1, :] * u2 + cw_ref[1:2, :] * u1 + cw_ref[2:3, :] * cu + cw_ref[3:4, :]
    return cv, u1, u2


def mixer_out(os_, lses, proj, cw, x, vec, wout, name, carry=None):
    S = proj.shape[0]
    tm = _row_tile(S, 512)
    hb = tm // HALO_ROWS

    def body(o1, o2, o3, l1, l2, l3, pc_ref, ph_ref, cw_ref, x_ref, vec_ref, w_ref, ycat_ref, lse_ref, xn_ref, y_ref):
        i = pl.program_id(0)
        for p in range(AW // LANES):
            cs = slice(p * LANES, (p + 1) * LANES)
            ls = [l[:, cs] for l in (l1, l2, l3)]
            mx = jnp.maximum(jnp.maximum(ls[0], ls[1]), ls[2])
            t = mx + jnp.log(jnp.exp(ls[0] - mx) + jnp.exp(ls[1] - mx) + jnp.exp(ls[2] - mx))
            lse_ref[:, cs] = t
            acc = jnp.zeros((tm, LANES), F32)
            for l, o in zip(ls, (o1, o2, o3)):
                acc = acc + jnp.exp(l - t) * o[:, cs]
            ycat_ref[:, cs] = acc.astype(ycat_ref.dtype)
        row = lax.broadcasted_iota(jnp.int32, (tm, CW), 0)
        gb, gc, u = (pc_ref[:, k * CW:(k + 1) * CW].astype(F32) for k in range(3))
        ph = ph_ref[...].astype(F32)[HALO_ROWS - SUBLANES:]
        halo_cu = jnp.where(i > 0, ph[:, CW:2 * CW] * ph[:, 2 * CW:3 * CW], 0.0)
        cv, _, _ = _conv_fwd(gc * u, halo_cu, cw_ref, row)
        ycat_ref[:, AW:AW + CW] = (gb * cv).astype(ycat_ref.dtype)
        y = jnp.dot(ycat_ref[...].astype(MXU_DTYPE), w_ref[...], preferred_element_type=F32)
        xn_ref[...] = x_ref[...] + vec_ref[3:4, :] * y
        y_ref[...] = y.astype(y_ref.dtype)

    ot = pl.BlockSpec((tm, AW), lambda i: (i, 0))
    t = pl.BlockSpec((tm, D), lambda i: (i, 0))
    return _pcall(
        body, name, (S // tm,),
        [ot] * 6 + [pl.BlockSpec((tm, 3 * CW), lambda i: (i, 1)),
                    pl.BlockSpec((HALO_ROWS, 3 * CW), lambda i: (jnp.maximum(i * hb - 1, 0), 1)),
                    pl.BlockSpec((SUBLANES, CW), lambda i: (0, 0)),
                    t, pl.BlockSpec((SUBLANES, D), lambda i: (0, 0)), pl.BlockSpec((D, D), lambda i: (0, 0))],
        [t, ot, t, t],
        [jax.ShapeDtypeStruct((S, D), ACT_DTYPE), jax.ShapeDtypeStruct((S, AW), F32),
         jax.ShapeDtypeStruct((S, D), F32), jax.ShapeDtypeStruct((S, D), ACT_DTYPE)],
        ("arbitrary",), (*os_, *lses, proj, proj, cw, x, vec, wout), carry)


def out_proj_bwd(dxo, y, ycat, vec, wout, name, carry=None):
    S = dxo.shape[0]
    tm = _row_tile(S, 512)

    def body(dxo_ref, y_ref, yc_ref, vec_ref, w_ref, dyb_ref, dyc_ref, dl_ref, sums_ref):
        dxo = dxo_ref[...]
        dgate = jnp.sum(dxo * y_ref[...].astype(F32), axis=0, keepdims=True)
        dy = (vec_ref[3:4, :] * dxo).astype(MXU_DTYPE)
        dyb_ref[...] = dy
        dyc_ref[...] = lax.dot_general(dy, w_ref[...], NT_DIMS, preferred_element_type=F32)
        m_a = _head_masks(tm)
        for p in range(AW // LANES):
            cs = slice(p * LANES, (p + 1) * LANES)
            s_a, s_b = _pair_stat(dyc_ref[:, cs] * yc_ref[:, cs].astype(F32), m_a)
            dl_ref[:, cs] = jnp.where(m_a, s_a, s_b)
        _acc_rows(sums_ref, pl.program_id(0) == 0, (dgate,))

    t = pl.BlockSpec((tm, D), lambda i: (i, 0))
    at = pl.BlockSpec((tm, AW), lambda i: (i, 0))
    return _pcall(
        body, name, (S // tm,),
        [t, t, t, pl.BlockSpec((SUBLANES, D), lambda i: (0, 0)), pl.BlockSpec((D, D), lambda i: (0, 0))],
        [t, t, at, pl.BlockSpec((SUBLANES, D), lambda i: (0, 0))],
        [jax.ShapeDtypeStruct((S, D), MXU_DTYPE), jax.ShapeDtypeStruct((S, D), F32),
         jax.ShapeDtypeStruct((S, AW), F32), jax.ShapeDtypeStruct((SUBLANES, D), F32)],
        ("arbitrary",), (dxo, y, ycat, vec, wout), carry)


def mixer_mid_bwd(dqs, dks, dvs, proj, dycat, gvec, cw, name, carry=None):
    S = proj.shape[0]
    tm = _row_tile(S, 512)
    hb = tm // SUBLANES
    hp = tm // HALO_ROWS
    nsl = S // SUBLANES
    ntile = S // tm

    def body(dq1, dq2, dq3, dk1, dk2, dk3, dv1, dv2, dv3, pr_ref, pp_ref, pn_ref, dyc_ref, dyn_ref,
             g_ref, cw_ref, dp_ref, sums_ref):
        i = pl.program_id(0)
        m_a = _head_masks(tm)
        gsum = []
        for which, parts in ((0, (dq1, dq2, dq3)), (1, (dk1, dk2, dk3))):
            acc_g = []
            for p in range(AW // LANES):
                lo = which * AW + p * LANES
                cs = slice(p * LANES, (p + 1) * LANES)
                xp = pr_ref[:, lo:lo + LANES].astype(F32)
                s_a, s_b = _pair_stat(xp * xp, m_a)
                rr = jnp.where(m_a, lax.rsqrt(s_a * (1.0 / HD) + EPS), lax.rsqrt(s_b * (1.0 / HD) + EPS))
                xh = xp * rr
                dn = parts[0][:, cs] + parts[1][:, cs] + parts[2][:, cs]
                acc_g.append(jnp.sum(dn * xh, axis=0, keepdims=True))
                t = dn * g_ref[which:which + 1, cs]
                t_a, t_b = _pair_stat(t * xh, m_a)
                mean = jnp.where(m_a, t_a, t_b) * (1.0 / HD)
                dp_ref[:, lo:lo + LANES] = (rr * (t - xh * mean)).astype(dp_ref.dtype)
            gsum.append(jnp.concatenate(acc_g, axis=1))
        dp_ref[:, 2 * AW:3 * AW] = (dv1[...] + dv2[...] + dv3[...]).astype(dp_ref.dtype)
        row = lax.broadcasted_iota(jnp.int32, (tm, CW), 0)
        base = 3 * AW
        gb, gc, u = (pr_ref[:, base + k * CW:base + (k + 1) * CW].astype(F32) for k in range(3))
        cu = gc * u
        pp = pp_ref[...].astype(F32)[HALO_ROWS - SUBLANES:]
        halo_cu = jnp.where(i > 0, pp[:, CW:2 * CW] * pp[:, 2 * CW:3 * CW], 0.0)
        cv, u1, u2 = _conv_fwd(cu, halo_cu, cw_ref, row)
        dyc = dyc_ref[...]
        dp_ref[:, base:base + CW] = (dyc * cv).astype(dp_ref.dtype)
        dcv = dyc * gb
        gb_next = pn_ref[:, 0:CW].astype(F32)[:SUBLANES]
        halo_dcv = jnp.where(i < ntile - 1, dyn_ref[...] * gb_next, 0.0)
        d1 = _shift_up(dcv, halo_dcv, 1, row)
        d2 = _shift_up(dcv, halo_dcv, 2, row)
        dcu = cw_ref[2:3, :] * dcv + cw_ref[1:2, :] * d1 + cw_ref[0:1, :] * d2
        dp_ref[:, base + CW:base + 2 * CW] = (dcu * u).astype(dp_ref.dtype)
        dp_ref[:, base + 2 * CW:base + 3 * CW] = (dcu * gc).astype(dp_ref.dtype)
        rows = (gsum[0], gsum[1],
                jnp.sum(dcv * u2, axis=0, keepdims=True), jnp.sum(dcv * u1, axis=0, keepdims=True),
                jnp.sum(dcv * cu, axis=0, keepdims=True), jnp.sum(dcv, axis=0, keepdims=True))
        _acc_rows(sums_ref, i == 0, rows)

    at = pl.BlockSpec((tm, AW), lambda i: (i, 0))
    return _pcall(
        body, name, (ntile,),
        [at] * 9 + [
            pl.BlockSpec((tm, INC), lambda i: (i, 0)),
            pl.BlockSpec((HALO_ROWS, 3 * CW), lambda i: (jnp.maximum(i * hp - 1, 0), 1)),
            pl.BlockSpec((HALO_ROWS, 3 * CW), lambda i: (jnp.minimum((i + 1) * hp, S // HALO_ROWS - 1), 1)),
            pl.BlockSpec((tm, CW), lambda i: (i, 1)),
            pl.BlockSpec((SUBLANES, CW), lambda i: (jnp.minimum((i + 1) * hb, nsl - 1), 1)),
            pl.BlockSpec((SUBLANES, AW), lambda i: (0, 0)),
            pl.BlockSpec((SUBLANES, CW), lambda i: (0, 0))],
        [pl.BlockSpec((tm, INC), lambda i: (i, 0)), pl.BlockSpec((SUBLANES, AW), lambda i: (0, 0))],
        [jax.ShapeDtypeStruct((S, INC), MXU_DTYPE), jax.ShapeDtypeStruct((SUBLANES, AW), F32)],
        ("arbitrary",), (*dqs, *dks, *dvs, proj, proj, proj, dycat, dycat, gvec, cw), carry)


def mixer_in_bwd(dxo, x, dproj, vec, winp, name, carry=None):
    S = x.shape[0]
    tm = _row_tile(S, 512)
    pc = INC // NCHIP

    def body(dxo_ref, x_ref, dp_ref, vec_ref, w_ref, dxi_ref, sums_ref):
        xhat, r, gain, ng, sc, _, _ = _ada(x_ref[...], vec_ref)
        dh = jnp.zeros((tm, D), F32)
        for j in range(NCHIP):
            dh = dh + lax.dot_general(dp_ref[:, j * pc:(j + 1) * pc], w_ref[j], NT_DIMS, preferred_element_type=F32)
        dx, dshift, dscale, dng = _ada_bwd(dh, xhat, r, gain, ng, sc)
        dxi_ref[...] = dxo_ref[...] + dx
        _acc_rows(sums_ref, pl.program_id(0) == 0, (dshift, dscale, dng))

    t = pl.BlockSpec((tm, D), lambda i: (i, 0))
    return _pcall(
        body, name, (S // tm,),
        [t, t, pl.BlockSpec((tm, INC), lambda i: (i, 0)),
         pl.BlockSpec((SUBLANES, D), lambda i: (0, 0)),
         pl.BlockSpec((NCHIP, D, pc), lambda i: (0, 0, 0), pipeline_mode=pl.Buffered(1))],
        [t, pl.BlockSpec((SUBLANES, D), lambda i: (0, 0))],
        [jax.ShapeDtypeStruct((S, D), F32), jax.ShapeDtypeStruct((SUBLANES, D), F32)],
        ("arbitrary",), (dxo, x, dproj, vec, winp), carry)


def _vec(mod_l, ng_l, i):
    m = mod_l.reshape(3, 3, D)
    rows = jnp.stack([ng_l[i], m[i, 1], m[i, 0], m[i, 2]])
    return jnp.concatenate([rows, jnp.zeros((SUBLANES - 4, D), F32)], axis=0)


def local_step(x, target, mods, ngs, gvecs, cws, shards, w_first, cflag):
    saved = []
    weights = [dict(w1=[None, None], w2=[None, None]) for _ in range(2)]
    weights[0]["w1"][0], weights[0]["w2"][0] = w_first[0], w_first[1].reshape(DFF, D)
    h = x
    for l in range(2):
        w, sh = weights[l], shards[l]
        nxt = shards[l + 1] if l == 0 else None
        vecs = [_vec(mods[l], ngs[l], i) for i in range(3)]
        x0 = h
        (x1, a0, f0), (win, wout, w2b) = ffn_fwd(x0, vecs[0], w["w1"][0], w["w2"][0], 0.5, f"ffn_fwd_l{l}a",
                                                 carry=Carry("gather", [sh["win"], sh["wout"], sh["w2"][1]]))
        w["win"], w["wout"], w["w2"][1] = win, wout.reshape(D, D), w2b.reshape(DFF, D)
        quarter = [sh["w1"][1][k * (D // 4):(k + 1) * (D // 4)] for k in range(4)]
        (proj, h1b, qn, kn, v), w1b = mixer_in(x1, vecs[1], w["win"], gvecs[l], f"mixer_in_l{l}",
                                               carry=Carry("gather", [quarter[0]]))
        os_, lses, w1b = [], [], list(w1b)
        for k, d in enumerate(DILATIONS):
            (o, lse_d), got = attn_fwd(qn, kn, v, d, f"attn_fwd_l{l}_d{d}", carry=Carry("gather", [quarter[1 + k]]))
            w1b.append(got[0])
            os_.append(o)
            lses.append(lse_d)
        w["w1"][1] = jnp.concatenate(w1b, axis=1)
        (ycat, lse, x2, y), got = mixer_out(os_, lses, proj, cws[l], x1, vecs[1], w["wout"], f"mixer_out_l{l}",
                                            carry=Carry("gather", [nxt["w2"][0]]) if nxt else None)
        if nxt:
            weights[1]["w2"][0] = got[0].reshape(DFF, D)
        if nxt:
            (h, a2, f2), got = ffn_fwd(x2, vecs[2], w["w1"][1], w["w2"][1], 0.5, f"ffn_fwd_l{l}b",
                                       carry=Carry("gather", [nxt["w1"][0]]))
            weights[1]["w1"][0] = got[0]
        else:
            (dx, a2, f2, loss_blk), _ = ffn_fwd(x2, vecs[2], w["w1"][1], w["w2"][1], 0.5, f"ffn_fwd_l{l}b",
                                                target=target)
        saved.append(dict(vecs=vecs, x0=x0, a0=a0, f0=f0, x1=x1, proj=proj, h1b=h1b, qn=qn, kn=kn, v=v,
                          ycat=ycat, lse=lse, y=y, x2=x2, a2=a2, f2=f2))
    sums, totals, g_prev = [None, None], [None, None], None
    w2r = DFF // NCHIP
    for l in (1, 0):
        w, s = weights[l], saved[l]
        vecs = s["vecs"]
        ride = g_prev is not None
        own = l == 0
        mine, other = [None] * 6, [None] * 6

        def half_sum(group, recv, k0):
            return [add_half(g, r, cflag, f"add_sibling_l{l}_{k0 + j}") for j, (g, r) in enumerate(zip(group, recv))]

        def chip_sum(landed, k0):
            return [sum_chips(t, f"sum_chips_l{l}_{k0 + j}") for j, t in enumerate(landed)]

        (dx, hb, dfb, act, da, sums2), got = ffn_bwd(
            dx, s["x2"], s["a2"], s["f2"], vecs[2], w["w1"][1], w["w2"][1], 0.5, f"ffn_bwd_l{l}b",
            carry=Carry("swap_halves", g_prev) if ride else None)
        dw1b, _ = wgrad(hb, da, D, HALF, f"wgrad_w1_l{l}b")
        dw2b, _ = wgrad(act, dfb, HALF, D, f"wgrad_w2_l{l}b")
        if ride:
            wire = [add_half(g_prev[k], got[k], cflag, f"add_sibling_l{l + 1}_{k}") for k in range(6)]
        g_ffn_b = [dw1b, dw2b.reshape(NCHIP, w2r, D)]
        (dyb, dycat, delta, sums_o), got = out_proj_bwd(
            dx, s["y"], s["ycat"], vecs[1], w["wout"], f"out_proj_bwd_l{l}",
            carry=Carry("swap_halves", g_ffn_b) if own else None)
        dwout, _ = wgrad(s["ycat"].astype(MXU_DTYPE), dyb, D // 2, D, f"wgrad_wout_l{l}")
        if own:
            wire_ffn_b = half_sum(g_ffn_b, got, 4)
        dqs, dks, dvs, landed = [], [], [], {}
        for d in DILATIONS:
            carry = None
            if ride and d == 1:
                carry = Carry("scatter", wire[4:])
            if ride and d == 16:
                carry = Carry("scatter", wire[:4])
            if own and d == 4:
                carry = Carry("scatter", wire_ffn_b)
            (dq, dk, dv), landed[d] = attn_bwd(s["qn"], s["kn"], s["v"], dycat, s["lse"], delta, d,
                                               f"attn_bwd_l{l}_d{d}", carry=carry)
            dqs.append(dq)
            dks.append(dk)
            dvs.append(dv)
        if ride:
            tot = [sum_chips(t, f"sum_chips_l{l + 1}_{k}") for k, t in enumerate(list(landed[16]) + list(landed[1]))]
        if own:
            mine[4:6] = chip_sum(landed[4], 4)
        ready = (tot if ride else []) + (mine[4:6] if own else [])
        (dproj, sums_m), got = mixer_mid_bwd(dqs, dks, dvs, s["proj"], dycat, gvecs[l], cws[l], f"mixer_mid_bwd_l{l}",
                                             carry=Carry("swap", ready) if ready else None)
        if ride:
            totals[l + 1] = (tot, list(got[:6]))
        if own:
            other[4:6] = list(got[-2:])
        dwin, _ = wgrad(s["h1b"], dproj, D, INC // NCHIP, f"wgrad_win_l{l}")
        g_mixer = [dwin, dwout.reshape(NCHIP, D // NCHIP, D)]
        (dx, sums1), got = mixer_in_bwd(dx, s["x1"], dproj, vecs[1], w["win"], f"mixer_in_bwd_l{l}",
                                        carry=Carry("swap_halves", g_mixer) if own else None)
        if own:
            wire_mixer = half_sum(g_mixer, got, 2)
        (dx, hb, dfb, act, da, sums0), _ = ffn_bwd(
            dx, s["x0"], s["a0"], s["f0"], vecs[0], w["w1"][0], w["w2"][0], 0.5, f"ffn_bwd_l{l}a")
        dw1a, got = wgrad(hb, da, D, HALF, f"wgrad_w1_l{l}a", carry=Carry("scatter", wire_mixer) if own else None)
        if own:
            mine[2:4] = chip_sum(got, 2)
        dw2a, got = wgrad(act, dfb, HALF, D, f"wgrad_w2_l{l}a", carry=Carry("swap", mine[2:4]) if own else None)
        g_ffn_a = [dw1a, dw2a.reshape(NCHIP, w2r, D)]
        if own:
            other[2:4] = list(got)
            wire_ffn_a = half_sum(g_ffn_a, run_carry(Carry("swap_halves", g_ffn_a), "swap_halves_tail"), 0)
            mine[0:2] = chip_sum(run_carry(Carry("scatter", wire_ffn_a), "scatter_grads_tail"), 0)
            other[0:2] = list(run_carry(Carry("swap", mine[0:2]), "swap_totals_tail"))
            totals[l] = (mine, other)
        g_prev = g_ffn_a + g_mixer + g_ffn_b
        sums[l] = (sums0, sums1, sums_o, sums2, sums_m)
    return loss_blk, dx, totals, sums


def small_all_gather(blk, name):
    m_per, n = blk.shape

    def body(x_ref, out_ref, send_sems, recv_sems, local_sem):
        x, y, c = _here()
        me, sibling = (x, y, c), (x, y, 1 - c)
        chips = [(1 - x, y), (x, 1 - y), (1 - x, 1 - y)]

        def rows(px, py, pc):
            return out_ref.at[pl.ds((4 * px + 2 * py + pc) * m_per, m_per), :]

        def copy(k, block, to, src=None):
            return pltpu.make_async_remote_copy(
                src_ref=rows(*block) if src is None else src, dst_ref=rows(*block),
                send_sem=send_sems.at[k], recv_sem=recv_sems.at[k], device_id=to, device_id_type=MESH)

        mine = pltpu.make_async_copy(x_ref, rows(*me), local_sem)
        mine.start()
        first = [copy(0, me, sibling, src=x_ref)]
        first += [copy(1 + j, me, (*chip, c), src=x_ref) for j, chip in enumerate(chips)]
        for cp in first:
            cp.start()
        passed = [copy(4 + j, (*chip, c), sibling) for j, chip in enumerate(chips)]
        for j, chip in enumerate(chips):
            copy(1 + j, (*chip, c), me).wait_recv()
            passed[j].start()
        copy(0, sibling, me).wait_recv()
        for j, chip in enumerate(chips):
            copy(4 + j, (*chip, 1 - c), me).wait_recv()
        for cp in first + passed:
            cp.wait_send()
        mine.wait()

    return pl.pallas_call(
        body, name=name,
        out_shape=jax.ShapeDtypeStruct((NDEV * m_per, n), blk.dtype),
        in_specs=[pl.BlockSpec(memory_space=pltpu.VMEM)],
        out_specs=pl.BlockSpec(memory_space=pltpu.VMEM),
        scratch_shapes=[pltpu.SemaphoreType.DMA((7,)), pltpu.SemaphoreType.DMA((7,)), pltpu.SemaphoreType.DMA],
        compiler_params=pltpu.CompilerParams(vmem_limit_bytes=VMEM_LIMIT),
    )(blk)


EW_BLOCK_BYTES = 1 << 20


def _ew_rows(rows, cols, refs=8):
    want = max(16, EW_BLOCK_BYTES * (2 if refs <= 4 else 1) // (4 * cols))
    best = None
    for t in range(16, rows + 1, 16):
        if rows % t == 0 and t <= want:
            best = t
    return best if best is not None else rows


def add_half(g, recv, cflag, name):
    pieces, r, cols = g.shape
    r2 = r // 2
    tr = _ew_rows(r2, cols, refs=3)
    nt = r2 // tr

    def body(c_ref, g_ref, r_ref, o_ref):
        o_ref[...] = (g_ref[...] + r_ref[...]).astype(o_ref.dtype)

    half = pl.BlockSpec((None, tr, cols), lambda j, i, c_ref: (j, i, 0))
    return pl.pallas_call(
        body, name=name,
        grid_spec=pltpu.PrefetchScalarGridSpec(
            num_scalar_prefetch=1, grid=(pieces, nt),
            in_specs=[pl.BlockSpec((None, tr, cols), lambda j, i, c_ref: (j, c_ref[0] * nt + i, 0)), half],
            out_specs=half),
        out_shape=jax.ShapeDtypeStruct((pieces, r2, cols), WIRE_DTYPE),
        compiler_params=_params(("arbitrary", "arbitrary")),
    )(cflag, g, recv)


def sum_chips(recv, name):
    _, r, cols = recv.shape
    tr = _ew_rows(r, cols, refs=3)

    def body(r_ref, o_ref):
        acc = r_ref[0].astype(F32)
        for k in range(1, NCHIP):
            acc = acc + r_ref[k].astype(F32)
        o_ref[...] = acc

    return pl.pallas_call(
        body, name=name, grid=(r // tr,),
        in_specs=[pl.BlockSpec((NCHIP, tr, cols), lambda i: (0, i, 0))],
        out_specs=pl.BlockSpec((tr, cols), lambda i: (i, 0)),
        out_shape=jax.ShapeDtypeStruct((r, cols), F32),
        compiler_params=_params(("arbitrary",)),
    )(recv)


def sum_devices(rows8, name):
    def body(r_ref, o_ref):
        acc = r_ref[0:1, :]
        for k in range(1, NDEV):
            acc = acc + r_ref[k:k + 1, :]
        o_ref[...] = jnp.broadcast_to(acc, o_ref.shape)

    return pl.pallas_call(
        body, name=name, out_shape=jax.ShapeDtypeStruct(rows8.shape, F32),
        in_specs=[pl.BlockSpec(memory_space=pltpu.VMEM)], out_specs=pl.BlockSpec(memory_space=pltpu.VMEM),
        compiler_params=pltpu.CompilerParams(vmem_limit_bytes=VMEM_LIMIT),
    )(rows8)


def adamw(w, m, v, srcs, cflag, name, halves=False):
    planes, r, cols = w.shape
    rh = r // 2 if halves else r
    tr = _ew_rows(rh, cols)
    nth = rh // tr
    flat = [a for s in srcs for a in (s if halves else (s,))]
    ns = len(flat)
    per = ns // planes

    def body(c_ref, w_ref, m_ref, v_ref, *rest):
        s_refs, (g_ref, d_ref, mo_ref, vo_ref) = rest[:ns], rest[ns:]
        p, i = pl.program_id(0), pl.program_id(1)
        if halves:
            mine = jnp.logical_not(jnp.logical_xor(i >= nth, c_ref[0] == 1))
            blocks = [jnp.where(mine, s_refs[2 * k][...], s_refs[2 * k + 1][...]) for k in range(planes)]
        else:
            blocks = [s[...] for s in s_refs]
        g = blocks[0]
        for k in range(1, planes):
            g = jnp.where(p == k, blocks[k], g)
        g_ref[...] = g
        m_new = ADAM_B1 * m_ref[...] + (1.0 - ADAM_B1) * g
        v_new = ADAM_B2 * v_ref[...] + (1.0 - ADAM_B2) * (g * g)
        mo_ref[...] = m_new
        vo_ref[...] = v_new
        m_hat = m_new / (1.0 - ADAM_B1 ** ADAM_STEP)
        v_hat = v_new / (1.0 - ADAM_B2 ** ADAM_STEP)
        d_ref[...] = -ADAM_LR * (m_hat / (jnp.sqrt(v_hat) + ADAM_EPS) + ADAM_WD * w_ref[...])

    pt = pl.BlockSpec((None, tr, cols), lambda p, i: (p, i, 0))
    st = [pl.BlockSpec((tr, cols), functools.partial(lambda k, p, i: (jnp.where(p == k, i % nth, 0), 0), j // per))
          for j in range(ns)]
    return pl.pallas_call(
        body, name=name, grid=(planes, r // tr),
        in_specs=[pl.BlockSpec(memory_space=pltpu.SMEM), pt, pt, pt] + st,
        out_specs=[pt] * 4,
        out_shape=[jax.ShapeDtypeStruct(w.shape, F32)] * 4,
        compiler_params=_params(("arbitrary", "arbitrary")),
    )(cflag, w, m, v, *flat)


ADA_COLS = 9 * D // NCHIP


def mod_fwd(c_all, w_ada, b_shard, name):
    def body(c_ref, w_ref, b_ref, o_ref):
        cc = c_ref[...]
        sc = cc * jax.nn.sigmoid(cc)
        o_ref[...] = jnp.dot(sc, w_ref[...], preferred_element_type=F32,
                             precision=lax.Precision.HIGHEST) + b_ref[...]

    return pl.pallas_call(
        body, name=name, grid=(2,),
        in_specs=[pl.BlockSpec((NDEV, D), lambda l: (0, 0)),
                  pl.BlockSpec((None, D, ADA_COLS), lambda l: (l, 0, 0)),
                  pl.BlockSpec((None, 1, ADA_COLS), lambda l: (l, 0, 0))],
        out_specs=pl.BlockSpec((None, NDEV, ADA_COLS), lambda l: (l, 0, 0)),
        out_shape=jax.ShapeDtypeStruct((2, NDEV, ADA_COLS), F32),
        compiler_params=_params(("arbitrary",)),
    )(c_all, w_ada, b_shard.reshape(2, 1, ADA_COLS))


def wada_grad(c_all_t, dmod, name):
    ct = ADA_COLS // 3

    def body(c_ref, d_ref, o_ref):
        cc = c_ref[...]
        sc = cc * jax.nn.sigmoid(cc)
        acc = sc[:, 0:1] * d_ref[0:1, :]
        for b in range(1, NDEV):
            acc = acc + sc[:, b:b + 1] * d_ref[b:b + 1, :]
        o_ref[...] = acc

    return pl.pallas_call(
        body, name=name, grid=(2, 3),
        in_specs=[pl.BlockSpec((D, LANES), lambda l, j: (0, 0)),
                  pl.BlockSpec((None, NDEV, ct), lambda l, j: (l, 0, j))],
        out_specs=pl.BlockSpec((None, D, ct), lambda l, j: (l, 0, j)),
        out_shape=jax.ShapeDtypeStruct((2, D, ADA_COLS), F32),
        compiler_params=_params(("arbitrary", "arbitrary")),
    )(c_all_t, dmod)


def _pad_rows(row, rows=SUBLANES):
    return jnp.concatenate([row[None, :], jnp.zeros((rows - 1, row.shape[0]), row.dtype)], axis=0)


def kernel(x, c, w_ada, b_ada, norm_g, w_in, q_norm_g, k_norm_g, conv_w, conv_b, w_out, ffn_w1, ffn_w2, loss_target, m_w_ada, m_b_ada, m_norm_g, m_w_in, m_q_norm_g, m_k_norm_g, m_conv_w, m_conv_b, m_w_out, m_ffn_w1, m_ffn_w2, v_w_ada, v_b_ada, v_norm_g, v_w_in, v_q_norm_g, v_k_norm_g, v_conv_w, v_conv_b, v_w_out, v_ffn_w1, v_ffn_w2):
    ix, iy, ic = lax.axis_index("x"), lax.axis_index("y"), lax.axis_index("c")
    chip = 2 * ix + iy
    dev = 2 * chip + ic
    cflag = jnp.reshape(ic, (1,)).astype(jnp.int32)
    ngw = norm_g.shape[-1]
    cww = conv_w.shape[-1]

    pack = jnp.concatenate([c[0], norm_g.reshape(-1), conv_w.reshape(-1)])
    got = small_all_gather(_pad_rows(pack), "gather_c_normg_convw")[::SUBLANES]
    c_all = got[:, :D]
    per_chip = got[::2]
    ng_full = jnp.concatenate([per_chip[j, D:D + 6 * ngw].reshape(2, 3, ngw) for j in range(NCHIP)], axis=-1)
    cw_full = jnp.concatenate([per_chip[j, D + 6 * ngw:].reshape(2, 3, cww) for j in range(NCHIP)], axis=-1)

    b_shard = lax.dynamic_slice_in_dim(b_ada, chip * ADA_COLS, ADA_COLS, axis=1)
    mod_blk = mod_fwd(c_all, w_ada, b_shard, "mod_fwd").reshape(2 * NDEV, ADA_COLS)
    mod_all = small_all_gather(mod_blk, "gather_mod").reshape(NDEV, 2, NDEV, ADA_COLS)[::2]
    mod_mine = lax.dynamic_index_in_dim(mod_all, dev, axis=2, keepdims=False)
    mods = [mod_mine[:, l, :].reshape(-1) for l in range(2)]

    shards, gvecs, cws = [], [], []
    for l in range(2):
        shards.append(dict(w1=[ffn_w1[l, i].astype(MXU_DTYPE) for i in range(2)],
                           w2=[ffn_w2[l, i].astype(MXU_DTYPE) for i in range(2)],
                           win=w_in[l].astype(MXU_DTYPE), wout=w_out[l].astype(MXU_DTYPE)))
        gv = jnp.stack([jnp.tile(q_norm_g[l], AW // HD), jnp.tile(k_norm_g[l], AW // HD)])
        gvecs.append(jnp.concatenate([gv, jnp.zeros((SUBLANES - 2, AW), F32)], axis=0))
        cws.append(jnp.concatenate([cw_full[l], conv_b[l][None, :], jnp.zeros((SUBLANES - 4, CW), F32)], axis=0))
    w_first = gather_split([shards[0]["w1"][0], shards[0]["w2"][0]], "gather_first_ffn")

    loss_blk, dx, totals, sums = local_step(x[0], loss_target[0], mods, [ng_full[0], ng_full[1]], gvecs, cws,
                                            shards, w_first, cflag)

    dmods, dngs, dqg, dkg, dcw, dcb = [], [], [], [], [], []
    for l in range(2):
        s0, s1, so, s2, sm = sums[l]
        dmods.append(jnp.concatenate([s0[0], s0[1], s0[3], s1[0], s1[1], so[0], s2[0], s2[1], s2[3]]))
        dngs.append(jnp.concatenate([s0[2], s1[2], s2[2]]))
        dqg.append(sm[0].reshape(AW // HD, HD).sum(0))
        dkg.append(sm[1].reshape(AW // HD, HD).sum(0))
        dcw.append(sm[2:5].reshape(-1))
        dcb.append(sm[5])
    small = jnp.concatenate(dmods + dngs + dqg + dkg + dcw + dcb + [loss_blk[0]])
    small_all = small_all_gather(_pad_rows(small), "gather_small_grads")[::SUBLANES]
    nm = 9 * D
    dmod_all = small_all[:, :2 * nm].reshape(NDEV, 2, NCHIP, ADA_COLS)
    dmod_mine = lax.dynamic_index_in_dim(dmod_all, chip, axis=2, keepdims=False).transpose(1, 0, 2)
    tot = sum_devices(small_all, "sum_small_grads")[0]
    o = 2 * nm
    g_b_ada = tot[:o].reshape(2, nm)
    g_norm_g = lax.dynamic_slice_in_dim(tot[o:o + 6 * D].reshape(2, 3, D), chip * ngw, ngw, axis=2)
    o += 6 * D
    g_qg = tot[o:o + 2 * HD].reshape(2, HD)
    o += 2 * HD
    g_kg = tot[o:o + 2 * HD].reshape(2, HD)
    o += 2 * HD
    g_cw = lax.dynamic_slice_in_dim(tot[o:o + 6 * CW].reshape(2, 3, CW), chip * cww, cww, axis=2)
    o += 6 * CW
    g_cb = tot[o:o + 2 * CW].reshape(2, CW)
    loss = tot[o + 2 * CW]

    c_all_t = jnp.concatenate([c_all.T, jnp.zeros((D, LANES - NDEV), F32)], axis=1)
    g_wada_src = wada_grad(c_all_t, dmod_mine, "wada_grad")

    def halves(k_of_plane):
        return [(totals[l][0][k], totals[l][1][k]) for l, k in k_of_plane]

    r_wada = adamw(w_ada, m_w_ada, v_w_ada, [g_wada_src[0], g_wada_src[1]], cflag, "adamw_w_ada")
    r_win = adamw(w_in, m_w_in, v_w_in, halves([(0, 2), (1, 2)]), cflag, "adamw_w_in", halves=True)
    r_wout = adamw(w_out, m_w_out, v_w_out, halves([(0, 3), (1, 3)]), cflag, "adamw_w_out", halves=True)
    r_w1 = adamw(ffn_w1.reshape(4, D, HALF), m_ffn_w1.reshape(4, D, HALF), v_ffn_w1.reshape(4, D, HALF),
                 halves([(0, 0), (0, 4), (1, 0), (1, 4)]), cflag, "adamw_ffn_w1", halves=True)
    w2r = DFF // NCHIP
    r_w2 = adamw(ffn_w2.reshape(4, w2r, D), m_ffn_w2.reshape(4, w2r, D), v_ffn_w2.reshape(4, w2r, D),
                 halves([(0, 1), (0, 5), (1, 1), (1, 5)]), cflag, "adamw_ffn_w2", halves=True)
    r_w1 = [t.reshape(ffn_w1.shape) for t in r_w1]
    r_w2 = [t.reshape(ffn_w2.shape) for t in r_w2]

    smalls = [("b_ada", b_ada, m_b_ada, v_b_ada, g_b_ada), ("norm_g", norm_g, m_norm_g, v_norm_g, g_norm_g),
              ("q_norm_g", q_norm_g, m_q_norm_g, v_q_norm_g, g_qg), ("k_norm_g", k_norm_g, m_k_norm_g, v_k_norm_g, g_kg),
              ("conv_w", conv_w, m_conv_w, v_conv_w, g_cw), ("conv_b", conv_b, m_conv_b, v_conv_b, g_cb)]
    n_small = sum(t[1].size for t in smalls)
    pad = (-n_small) % (16 * LANES)

    def packed(idx):
        flat = jnp.concatenate([t[idx].reshape(-1) for t in smalls] + [jnp.zeros((pad,), F32)])
        return flat.reshape(-1, LANES)

    r_small = adamw(packed(1)[None], packed(2)[None], packed(3)[None], [packed(4)], cflag, "adamw_small")
    small_out = {}
    o = 0
    for name_, w_, _, _, _ in smalls:
        small_out[name_] = [t.reshape(-1)[o:o + w_.size].reshape(w_.shape) for t in r_small]
        o += w_.size

    res = {"w_ada": r_wada, "w_in": r_win, "w_out": r_wout, "ffn_w1": r_w1, "ffn_w2": r_w2, **small_out}
    order = ["w_ada", "b_ada", "norm_g", "w_in", "q_norm_g", "k_norm_g", "conv_w", "conv_b", "w_out", "ffn_w1", "ffn_w2"]
    outs = [loss, dx[None]]
    for k in range(4):
        outs += [res[nm_][k] for nm_ in order]
    return tuple(outs)
```

```python
import functools

import jax
import jax.numpy as jnp
from jax import lax
from jax.experimental import pallas as pl
from jax.experimental.pallas import tpu as pltpu

F32 = jnp.float32
MXU_DTYPE = jnp.bfloat16
ACT_DTYPE = jnp.bfloat16
WIRE_DTYPE = jnp.bfloat16

D = 1024
HD = 64
AW = 512
CW = 512
DFF = 2816
HALF = DFF // 2
INC = 3 * AW + 3 * CW
NCHIP = 4
NDEV = 8
QBLK = 128
ATTN_QBLOCKS = 16
ATTN_INTERLEAVE = 16
ATTN_CHUNK_ROWS = 4096
DILATIONS = (1, 4, 16)
EPS = 1e-6
NEG = -1e30
LANES = 128
SUBLANES = 8
HALO_ROWS = 16
VMEM_LIMIT = 56 * 1024 * 1024

ADAM_LR = 0.001
ADAM_B1 = 0.9
ADAM_B2 = 0.999
ADAM_EPS = 1e-08
ADAM_WD = 0.01
ADAM_STEP = 10

NT_DIMS = (((1,), (1,)), ((), ()))
TN_DIMS = (((0,), (0,)), ((), ()))


def _params(sem, vmem=VMEM_LIMIT):
    return pltpu.CompilerParams(dimension_semantics=sem, vmem_limit_bytes=vmem)


def _row_tile(n, want):
    t = min(n, want)
    assert n % t == 0
    return t


def _ada(xt, vec_ref):
    ng, sc, sh, gt = vec_ref[0:1, :], vec_ref[1:2, :], vec_ref[2:3, :], vec_ref[3:4, :]
    r = lax.rsqrt(jnp.mean(xt * xt, axis=-1, keepdims=True) + EPS)
    return xt * r, r, ng * (1.0 + sc), ng, sc, sh, gt


def _ada_bwd(dh, xhat, r, gain, ng, sc):
    dshift = jnp.sum(dh, axis=0, keepdims=True)
    dhx = dh * xhat
    dscale = jnp.sum(dhx, axis=0, keepdims=True) * ng
    dng = jnp.sum(dhx, axis=0, keepdims=True) * (1.0 + sc)
    dxhat = dh * gain
    dx = r * (dxhat - xhat * jnp.mean(dxhat * xhat, axis=-1, keepdims=True))
    return dx, dshift, dscale, dng


def _acc_rows(sums_ref, first, rows):
    @pl.when(first)
    def _():
        sums_ref[...] = jnp.zeros_like(sums_ref)
    for k, row in enumerate(rows):
        sums_ref[k:k + 1, :] += row


MESH = pl.DeviceIdType.MESH
ANY = pl.BlockSpec(memory_space=pl.ANY)


def _here():
    return lax.axis_index("x"), lax.axis_index("y"), lax.axis_index("c")


def _ici_copies(src_refs, dst_refs, send_sems, recv_sems, local_sems, scatter):
    x, y, c = _here()
    my_chip = 2 * x + y
    peers = [(1 - x, y), (x, 1 - y), (1 - x, 1 - y)]
    local, out, inc = [], [], []
    for a, (src, dst) in enumerate(zip(src_refs, dst_refs)):
        local.append(pltpu.make_async_copy(src.at[my_chip] if scatter else src, dst.at[my_chip], local_sems.at[a]))
        for j, (px, py) in enumerate(peers):
            sems = dict(send_sem=send_sems.at[3 * a + j], recv_sem=recv_sems.at[3 * a + j],
                        device_id=(px, py, c), device_id_type=MESH)
            out.append(pltpu.make_async_remote_copy(
                src_ref=src.at[2 * px + py] if scatter else src, dst_ref=dst.at[my_chip], **sems))
            inc.append(pltpu.make_async_remote_copy(
                src_ref=src.at[my_chip] if scatter else src, dst_ref=dst.at[2 * px + py], **sems))
    return local, out, inc


def _swap_copies(src_refs, dst_refs, send_sems, recv_sems, halves):
    x, y, c = _here()
    cps = []
    for k, (src, dst) in enumerate(zip(src_refs, dst_refs)):
        if halves:
            r2 = src.shape[1] // 2
            src = src.at[:, pl.ds((1 - c) * r2, r2), :]
        cps.append(pltpu.make_async_remote_copy(
            src_ref=src, dst_ref=dst, send_sem=send_sems.at[k], recv_sem=recv_sems.at[k],
            device_id=(x, y, 1 - c), device_id_type=MESH))
    return cps


class Carry:
    def __init__(self, kind, srcs):
        self.kind, self.srcs, n = kind, list(srcs), len(srcs)
        if kind == "gather":
            shapes = [(NCHIP,) + s.shape for s in srcs]
        elif kind == "swap_halves":
            shapes = [(s.shape[0], s.shape[1] // 2, s.shape[2]) for s in srcs]
        else:
            shapes = [s.shape for s in srcs]
        self.out_shape = [jax.ShapeDtypeStruct(sh, s.dtype) for sh, s in zip(shapes, srcs)]
        dma = pltpu.SemaphoreType.DMA
        self.sems = [dma((3 * n,)), dma((3 * n,)), dma((n,))] if kind in ("gather", "scatter") else [dma((n,)), dma((n,))]

    def start(self, srcs, dsts, sems):
        if self.kind in ("gather", "scatter"):
            local, out, _ = _ici_copies(srcs, dsts, *sems, self.kind == "scatter")
            for cp in local + out:
                cp.start()
        else:
            for cp in _swap_copies(srcs, dsts, *sems, self.kind == "swap_halves"):
                cp.start()

    def wait(self, srcs, dsts, sems):
        if self.kind in ("gather", "scatter"):
            local, out, inc = _ici_copies(srcs, dsts, *sems, self.kind == "scatter")
            for cp in inc:
                cp.wait_recv()
            for cp in out:
                cp.wait_send()
            for cp in local:
                cp.wait()
        else:
            cps = _swap_copies(srcs, dsts, *sems, self.kind == "swap_halves")
            for cp in cps:
                cp.wait_recv()
            for cp in cps:
                cp.wait_send()


def run_carry(carry, name):
    n = len(carry.srcs)

    def body(*refs):
        srcs, dsts, sems = refs[:n], refs[n:2 * n], refs[2 * n:]
        carry.start(srcs, dsts, sems)
        carry.wait(srcs, dsts, sems)

    return pl.pallas_call(body, name=name, out_shape=carry.out_shape, in_specs=[ANY] * n, out_specs=[ANY] * n,
                          scratch_shapes=carry.sems)(*carry.srcs)


def gather_split(srcs, name):
    n = len(srcs)

    def body(*refs):
        src_refs, dst_refs = refs[:n], refs[n:2 * n]
        send_sems, recv_sems, fwd_send, fwd_recv, local_sems = refs[2 * n:]
        x, y, c = _here()
        my_chip = 2 * x + y
        peers = [(1 - x, y), (x, 1 - y), (1 - x, 1 - y)]

        def half(ref, h):
            r2 = ref.shape[0] // 2
            return ref.at[pl.ds(h * r2, r2), :]

        local, out, landed, passed, arriving = [], [], [], [], []
        for a, (src, dst) in enumerate(zip(src_refs, dst_refs)):
            local.append(pltpu.make_async_copy(src, dst.at[my_chip], local_sems.at[a]))
            for j, (px, py) in enumerate(peers):
                k = 3 * a + j
                theirs = dst.at[2 * px + py]
                ici = dict(send_sem=send_sems.at[k], recv_sem=recv_sems.at[k], device_id=(px, py, c), device_id_type=MESH)
                d2d = dict(send_sem=fwd_send.at[k], recv_sem=fwd_recv.at[k], device_id=(x, y, 1 - c), device_id_type=MESH)
                out.append(pltpu.make_async_remote_copy(src_ref=half(src, c), dst_ref=half(dst.at[my_chip], c), **ici))
                landed.append(pltpu.make_async_remote_copy(src_ref=half(src, c), dst_ref=half(theirs, c), **ici))
                passed.append(pltpu.make_async_remote_copy(src_ref=half(theirs, c), dst_ref=half(theirs, c), **d2d))
                arriving.append(pltpu.make_async_remote_copy(src_ref=half(theirs, c), dst_ref=half(theirs, 1 - c), **d2d))
        for cp in local + out:
            cp.start()
        for got, fwd in zip(landed, passed):
            got.wait_recv()
            fwd.start()
        for cp in arriving:
            cp.wait_recv()
        for cp in out + passed:
            cp.wait_send()
        for cp in local:
            cp.wait()

    dma = pltpu.SemaphoreType.DMA
    return pl.pallas_call(
        body, name=name, out_shape=[jax.ShapeDtypeStruct((NCHIP,) + s.shape, s.dtype) for s in srcs],
        in_specs=[ANY] * n, out_specs=[ANY] * n,
        scratch_shapes=[dma((3 * n,)), dma((3 * n,)), dma((3 * n,)), dma((3 * n,)), dma((n,))],
    )(*srcs)


def _pcall(body, name, grid, in_specs, out_specs, out_shape, sem, args, carry=None, scratch=()):
    if carry is None:
        outs = pl.pallas_call(body, name=name, grid=grid, in_specs=in_specs, out_specs=out_specs,
                              out_shape=out_shape, scratch_shapes=list(scratch), compiler_params=_params(sem))(*args)
        return outs, []
    n_in, n_out, nc, ns = len(in_specs), len(out_specs), len(carry.srcs), len(scratch)

    def wrapped(*refs):
        ins, csrc = refs[:n_in], refs[n_in:n_in + nc]
        outs, cdst = refs[n_in + nc:n_in + nc + n_out], refs[n_in + nc + n_out:n_in + 2 * nc + n_out]
        own = refs[n_in + 2 * nc + n_out:n_in + 2 * nc + n_out + ns]
        sems = refs[n_in + 2 * nc + n_out + ns:]
        ids = [pl.program_id(a) for a in range(len(grid))]
        first = functools.reduce(jnp.logical_and, [i == 0 for i in ids])
        last = functools.reduce(jnp.logical_and, [i == g - 1 for i, g in zip(ids, grid)])

        @pl.when(first)
        def _():
            carry.start(csrc, cdst, sems)

        body(*ins, *outs, *own)

        @pl.when(last)
        def _():
            carry.wait(csrc, cdst, sems)

    res = pl.pallas_call(
        wrapped, name=name, grid=grid,
        in_specs=list(in_specs) + [ANY] * nc, out_specs=list(out_specs) + [ANY] * nc,
        out_shape=list(out_shape) + carry.out_shape,
        scratch_shapes=list(scratch) + carry.sems, compiler_params=_params(sem),
    )(*args, *carry.srcs)
    return res[:n_out], res[n_out:]


def ffn_fwd(x, vec, w1p, w2, gs, name, carry=None, target=None):
    S = x.shape[0]
    tm = _row_tile(S, 512)

    def body(x_ref, *refs):
        if target is None:
            vec_ref, w1_ref, w2_ref, xn_ref, a_ref, f_ref = refs
        else:
            t_ref, vec_ref, w1_ref, w2_ref, xn_ref, a_ref, f_ref, l_ref = refs
        xt = x_ref[...]
        xhat, _, gain, _, _, sh, gt = _ada(xt, vec_ref)
        h = (xhat * gain + sh).astype(MXU_DTYPE)
        f = jnp.zeros((tm, D), F32)
        for hf in range(2):
            g = jnp.dot(h, w1_ref[hf], preferred_element_type=F32)
            up = jnp.dot(h, w1_ref[2 + hf], preferred_element_type=F32)
            a_ref[:, hf * HALF:(hf + 1) * HALF] = g.astype(a_ref.dtype)
            a_ref[:, DFF + hf * HALF:DFF + (hf + 1) * HALF] = up.astype(a_ref.dtype)
            act = (g * jax.nn.sigmoid(g) * up).astype(MXU_DTYPE)
            f = f + jnp.dot(act, w2_ref[hf * HALF:(hf + 1) * HALF, :], preferred_element_type=F32)
        f_ref[...] = f.astype(f_ref.dtype)
        xn = xt + (gs * gt) * f
        if target is None:
            xn_ref[...] = xn
        else:
            diff = xn - t_ref[...]
            xn_ref[...] = diff * (1.0 / D)
            part = jnp.sum(jnp.sum(diff * diff, axis=0, keepdims=True), axis=1, keepdims=True) * (0.5 / D)

            @pl.when(pl.program_id(0) == 0)
            def _():
                l_ref[...] = jnp.zeros_like(l_ref)
            l_ref[...] += jnp.broadcast_to(part, l_ref.shape)

    tile = pl.BlockSpec((tm, D), lambda i: (i, 0))
    last = target is not None
    return _pcall(
        body, name, (S // tm,),
        [tile] * (2 if last else 1) + [
            pl.BlockSpec((SUBLANES, D), lambda i: (0, 0)),
            pl.BlockSpec((NCHIP, D, HALF), lambda i: (0, 0, 0), pipeline_mode=pl.Buffered(1)),
            pl.BlockSpec((DFF, D), lambda i: (0, 0), pipeline_mode=pl.Buffered(1))],
        [tile, pl.BlockSpec((tm, 2 * DFF), lambda i: (i, 0)), tile]
        + ([pl.BlockSpec((SUBLANES, LANES), lambda i: (0, 0))] if last else []),
        [jax.ShapeDtypeStruct((S, D), F32),
         jax.ShapeDtypeStruct((S, 2 * DFF), ACT_DTYPE),
         jax.ShapeDtypeStruct((S, D), ACT_DTYPE)]
        + ([jax.ShapeDtypeStruct((SUBLANES, LANES), F32)] if last else []),
        ("arbitrary",), (x, target, vec, w1p, w2) if last else (x, vec, w1p, w2), carry)


def ffn_bwd(dxo, x, a, f, vec, w1p, w2, gs, name, carry=None):
    S = x.shape[0]
    tm = _row_tile(S, 256)

    def body(dxo_ref, x_ref, a_ref, f_ref, vec_ref, w1_ref, w2_ref,
             dxi_ref, hb_ref, dfb_ref, act_ref, da_ref, sums_ref):
        xt = x_ref[...]
        dxo = dxo_ref[...]
        xhat, r, gain, ng, sc, sh, gt = _ada(xt, vec_ref)
        hb_ref[...] = (xhat * gain + sh).astype(hb_ref.dtype)
        dgate = gs * jnp.sum(dxo * f_ref[...].astype(F32), axis=0, keepdims=True)
        df = ((gs * gt) * dxo).astype(MXU_DTYPE)
        dfb_ref[...] = df
        dh = jnp.zeros((tm, D), F32)
        for hf in range(2):
            lo, hi = hf * HALF, (hf + 1) * HALF
            dact = lax.dot_general(df, w2_ref[lo:hi, :], NT_DIMS, preferred_element_type=F32)
            g = a_ref[:, lo:hi].astype(F32)
            up = a_ref[:, DFF + lo:DFF + hi].astype(F32)
            sg = jax.nn.sigmoid(g)
            si = g * sg
            act_ref[:, lo:hi] = (si * up).astype(act_ref.dtype)
            dg = (dact * up * (sg * (1.0 + g * (1.0 - sg)))).astype(MXU_DTYPE)
            dup = (dact * si).astype(MXU_DTYPE)
            da_ref[:, lo:hi] = dg
            da_ref[:, DFF + lo:DFF + hi] = dup
            dh = dh + lax.dot_general(dg, w1_ref[hf], NT_DIMS, preferred_element_type=F32)
            dh = dh + lax.dot_general(dup, w1_ref[2 + hf], NT_DIMS, preferred_element_type=F32)
        dx, dshift, dscale, dng = _ada_bwd(dh, xhat, r, gain, ng, sc)
        dxi_ref[...] = dxo + dx
        _acc_rows(sums_ref, pl.program_id(0) == 0, (dshift, dscale, dng, dgate))

    return _pcall(
        body, name, (S // tm,),
        [pl.BlockSpec((tm, D), lambda i: (i, 0)),
         pl.BlockSpec((tm, D), lambda i: (i, 0)),
         pl.BlockSpec((tm, 2 * DFF), lambda i: (i, 0)),
         pl.BlockSpec((tm, D), lambda i: (i, 0)),
         pl.BlockSpec((SUBLANES, D), lambda i: (0, 0)),
         pl.BlockSpec((NCHIP, D, HALF), lambda i: (0, 0, 0), pipeline_mode=pl.Buffered(1)),
         pl.BlockSpec((DFF, D), lambda i: (0, 0), pipeline_mode=pl.Buffered(1))],
        [pl.BlockSpec((tm, D), lambda i: (i, 0)),
         pl.BlockSpec((tm, D), lambda i: (i, 0)),
         pl.BlockSpec((tm, D), lambda i: (i, 0)),
         pl.BlockSpec((tm, DFF), lambda i: (i, 0)),
         pl.BlockSpec((tm, 2 * DFF), lambda i: (i, 0)),
         pl.BlockSpec((SUBLANES, D), lambda i: (0, 0))],
        [jax.ShapeDtypeStruct((S, D), F32),
         jax.ShapeDtypeStruct((S, D), MXU_DTYPE),
         jax.ShapeDtypeStruct((S, D), MXU_DTYPE),
         jax.ShapeDtypeStruct((S, DFF), MXU_DTYPE),
         jax.ShapeDtypeStruct((S, 2 * DFF), MXU_DTYPE),
         jax.ShapeDtypeStruct((SUBLANES, D), F32)],
        ("arbitrary",), (dxo, x, a, f, vec, w1p, w2), carry)


def wgrad(a, b, kt, nt, name, carry=None):
    T, K = a.shape
    N = b.shape[1]
    pk, pn = K // kt, N // nt
    assert pk == 1 or pn == 1
    tt = _row_tile(T, 2048)
    steps = T // tt

    def body(a_ref, b_ref, o_ref):
        @pl.when(pl.program_id(1) == 0)
        def _():
            o_ref[...] = jnp.zeros_like(o_ref)
        o_ref[...] += lax.dot_general(a_ref[...], b_ref[...], TN_DIMS, preferred_element_type=F32)

    a_map = (lambda p, t: (t, p)) if pk > 1 else (lambda p, t: (t, 0))
    b_map = (lambda p, t: (t, p)) if pn > 1 else (lambda p, t: (t, 0))
    (out,), got = _pcall(
        body, name, (pk * pn, steps),
        [pl.BlockSpec((tt, kt), a_map), pl.BlockSpec((tt, nt), b_map)],
        [pl.BlockSpec((None, kt, nt), lambda p, t: (p, 0, 0))],
        [jax.ShapeDtypeStruct((pk * pn, kt, nt), F32)], ("arbitrary", "arbitrary"), (a, b), carry)
    return out, got


def _head_masks(rows):
    lane = lax.broadcasted_iota(jnp.int32, (rows, LANES), 1)
    return lane < HD


def _pair_stat(x, m_a):
    s_a = jnp.sum(jnp.where(m_a, x, 0.0), axis=1, keepdims=True)
    s_b = jnp.sum(jnp.where(m_a, 0.0, x), axis=1, keepdims=True)
    return s_a, s_b


def mixer_in(x, vec, winp, gvec, name, carry=None):
    S = x.shape[0]
    tm = _row_tile(S, 512)
    pc = INC // NCHIP

    def body(x_ref, vec_ref, w_ref, g_ref, proj_ref, hb_ref, qn_ref, kn_ref, v_ref, qkv_ref):
        xt = x_ref[...]
        xhat, _, gain, _, _, sh, _ = _ada(xt, vec_ref)
        h = (xhat * gain + sh).astype(MXU_DTYPE)
        hb_ref[...] = h
        for j in range(NCHIP):
            piece = jnp.dot(h, w_ref[j], preferred_element_type=F32)
            proj_ref[:, j * pc:(j + 1) * pc] = piece.astype(proj_ref.dtype)
            if (j + 1) * pc <= 3 * AW:
                qkv_ref[:, j * pc:(j + 1) * pc] = piece
        m_a = _head_masks(tm)
        for which, dst in ((0, qn_ref), (1, kn_ref)):
            for p in range(AW // LANES):
                lo = which * AW + p * LANES
                xp = qkv_ref[:, lo:lo + LANES]
                s_a, s_b = _pair_stat(xp * xp, m_a)
                rr = jnp.where(m_a, lax.rsqrt(s_a * (1.0 / HD) + EPS), lax.rsqrt(s_b * (1.0 / HD) + EPS))
                gp = g_ref[which:which + 1, p * LANES:(p + 1) * LANES]
                dst[:, p * LANES:(p + 1) * LANES] = (xp * rr * gp).astype(dst.dtype)
        v_ref[...] = qkv_ref[:, 2 * AW:3 * AW]

    assert 2 * pc == 3 * AW
    return _pcall(
        body, name, (S // tm,),
        [pl.BlockSpec((tm, D), lambda i: (i, 0)),
         pl.BlockSpec((SUBLANES, D), lambda i: (0, 0)),
         pl.BlockSpec((NCHIP, D, pc), lambda i: (0, 0, 0), pipeline_mode=pl.Buffered(1)),
         pl.BlockSpec((SUBLANES, AW), lambda i: (0, 0))],
        [pl.BlockSpec((tm, INC), lambda i: (i, 0)),
         pl.BlockSpec((tm, D), lambda i: (i, 0)),
         pl.BlockSpec((tm, AW), lambda i: (i, 0)),
         pl.BlockSpec((tm, AW), lambda i: (i, 0)),
         pl.BlockSpec((tm, AW), lambda i: (i, 0))],
        [jax.ShapeDtypeStruct((S, INC), ACT_DTYPE),
         jax.ShapeDtypeStruct((S, D), MXU_DTYPE),
         jax.ShapeDtypeStruct((S, AW), F32),
         jax.ShapeDtypeStruct((S, AW), F32),
         jax.ShapeDtypeStruct((S, AW), F32)],
        ("arbitrary",), (x, vec, winp, gvec), carry, scratch=[pltpu.VMEM((tm, 3 * AW), F32)])


def _band_masks(ncol):
    row = lax.broadcasted_iota(jnp.int32, (2 * QBLK, ncol), 0) & (QBLK - 1)
    col = lax.broadcasted_iota(jnp.int32, (2 * QBLK, ncol), 1)
    return row, col


def _stack_heads(t, m_a):
    zero = jnp.zeros_like(t)
    return jnp.concatenate([jnp.where(m_a, t, zero), jnp.where(m_a, zero, t)], axis=0)


class _AttnLayout:
    def __init__(self, d, S):
        self.d, self.S = d, S
        self.qb = max(1, min(ATTN_QBLOCKS, ATTN_CHUNK_ROWS // (QBLK * d)))
        self.nres = d
        self.nchunk = S // (self.qb * QBLK * d)
        self.grid = (AW // LANES, self.nchunk)
        self.unroll = max(1, min(d, ATTN_INTERLEAVE // self.qb))

    def _spec(self, blocks, row_of):
        return pl.BlockSpec((blocks * QBLK * self.d, LANES), lambda hp, j: (row_of(j), hp))

    def cur(self, chunk_of):
        return self._spec(self.qb, chunk_of)

    def prev(self, chunk_of):
        return self._spec(1, lambda j: jnp.maximum(chunk_of(j) * self.qb - 1, 0))

    def idx(self, b, r):
        if self.d == 1:
            return (pl.ds(b * QBLK, QBLK), slice(None))
        return (pl.ds(b * QBLK * self.d + r, QBLK, stride=self.d), slice(None))

    def per_residue(self, fn):
        if self.nres == 1:
            fn(0)
        else:
            def step(it, carry):
                for k in range(self.unroll):
                    fn(it * self.unroll + k)
                return carry
            lax.fori_loop(0, self.nres // self.unroll, step, 0)


def attn_fwd(qn, kn, v, d, name, carry=None):
    S = qn.shape[0]
    lay = _AttnLayout(d, S)
    qb = lay.qb

    def body(q_ref, kc_ref, kp_ref, vc_ref, vp_ref, o_ref, lse_ref):
        i = pl.program_id(1)
        m_a = _head_masks(QBLK)
        row, col = _band_masks(2 * QBLK)
        dist = row + QBLK - col
        band = (dist >= 0) & (dist <= QBLK)
        first = band & ((i > 0) | (col >= QBLK))

        def residue(r):
            kt = [kp_ref[lay.idx(0, r)].astype(MXU_DTYPE)]
            vt = [vp_ref[lay.idx(0, r)].astype(MXU_DTYPE)]
            for b in range(qb):
                kt.append(kc_ref[lay.idx(b, r)].astype(MXU_DTYPE))
                vt.append(vc_ref[lay.idx(b, r)].astype(MXU_DTYPE))
            for b in range(qb):
                rows = lay.idx(b, r)
                q = (q_ref[rows] * (HD ** -0.5)).astype(MXU_DTYPE)
                kcat = jnp.concatenate([kt[b], kt[b + 1]], axis=0)
                vcat = jnp.concatenate([vt[b], vt[b + 1]], axis=0)
                mask = first if b == 0 else band
                s = lax.dot_general(_stack_heads(q, m_a), kcat, NT_DIMS, preferred_element_type=F32)
                s = jnp.where(mask, s, NEG)
                m = jnp.max(s, axis=1, keepdims=True)
                p = jnp.exp(s - m)
                l = jnp.sum(p, axis=1, keepdims=True)
                o = jnp.dot(p.astype(MXU_DTYPE), vcat, preferred_element_type=F32) / l
                lse = jnp.broadcast_to(m + jnp.log(l), (2 * QBLK, LANES))
                o_ref[rows] = jnp.where(m_a, o[:QBLK], o[QBLK:])
                lse_ref[rows] = jnp.where(m_a, lse[:QBLK], lse[QBLK:])

        lay.per_residue(residue)

    cur, prev = lay.cur(lambda j: j), lay.prev(lambda j: j)
    return _pcall(body, name, lay.grid, [cur, cur, prev, cur, prev], [cur, cur],
                  [jax.ShapeDtypeStruct((S, AW), F32)] * 2, ("arbitrary", "arbitrary"), (qn, kn, kn, v, v), carry)


def _both_heads(t, m_a):
    other = pltpu.roll(t, HD, 1)
    return jnp.concatenate([jnp.where(m_a, t, other), jnp.where(m_a, other, t)], axis=0)


def attn_bwd(qn, kn, v, dycat, lse, delta, d, name, carry=None):
    S = qn.shape[0]
    lay = _AttnLayout(d, S)
    qb, nchunk = lay.qb, lay.nchunk

    def body(q_ref, kc_ref, kp_ref, vc_ref, vp_ref, do_ref, lse_ref, dl_ref,
             dq_ref, dk_ref, dv_ref, ck_ref, cv_ref):
        j = pl.program_id(1)
        i = nchunk - 1 - j
        m_a = _head_masks(QBLK)
        row, col = _band_masks(2 * QBLK)
        dist = row + QBLK - col
        band = (dist >= 0) & (dist <= QBLK)
        first = band & ((i > 0) | (col >= QBLK))

        def residue(r):
            def tiles(ref, cast):
                out = [ref[lay.idx(b, r)] for b in range(qb)]
                return [t.astype(MXU_DTYPE) for t in out] if cast else out

            def ktiles(cur_ref, prev_ref):
                return [prev_ref[lay.idx(0, r)].astype(MXU_DTYPE)] + tiles(cur_ref, True)

            qt = [(t * (HD ** -0.5)).astype(MXU_DTYPE) for t in tiles(q_ref, False)]
            dot_ = tiles(do_ref, True)
            lse_t = tiles(lse_ref, False)
            dl_t = tiles(dl_ref, False)
            kt = ktiles(kc_ref, kp_ref)
            vt = ktiles(vc_ref, vp_ref)
            dk_acc = [jnp.zeros((QBLK, LANES), F32) for _ in range(qb)]
            dv_acc = [jnp.zeros((QBLK, LANES), F32) for _ in range(qb)]
            crow = pl.ds(0, QBLK) if lay.nres == 1 else pl.ds(pl.multiple_of(r * QBLK, QBLK), QBLK)
            dk_acc[qb - 1] = jnp.where(j > 0, ck_ref[crow, :], 0.0)
            dv_acc[qb - 1] = jnp.where(j > 0, cv_ref[crow, :], 0.0)
            for x in range(qb):
                kcat = jnp.concatenate([kt[x], kt[x + 1]], axis=0)
                vcat = jnp.concatenate([vt[x], vt[x + 1]], axis=0)
                q2 = _stack_heads(qt[x], m_a)
                do2 = _stack_heads(dot_[x], m_a)
                lse2 = _both_heads(lse_t[x], m_a)
                dl2 = _both_heads(dl_t[x], m_a)
                lse2 = jnp.concatenate([lse2, lse2], axis=1)
                dl2 = jnp.concatenate([dl2, dl2], axis=1)
                s = lax.dot_general(q2, kcat, NT_DIMS, preferred_element_type=F32)
                p = jnp.exp(jnp.where(first if x == 0 else band, s, NEG) - lse2)
                dp = lax.dot_general(do2, vcat, NT_DIMS, preferred_element_type=F32)
                ds = p * (dp - dl2)
                dq = jnp.dot(ds.astype(MXU_DTYPE), kcat, preferred_element_type=F32)
                dq_ref[lay.idx(x, r)] = jnp.where(m_a, dq[:QBLK], dq[QBLK:]) * (HD ** -0.5)
                dk = jnp.dot(ds.T.astype(MXU_DTYPE), q2, preferred_element_type=F32)
                dv = jnp.dot(p.T.astype(MXU_DTYPE), do2, preferred_element_type=F32)
                if x == 0:
                    ck_ref[crow, :] = dk[:QBLK]
                    cv_ref[crow, :] = dv[:QBLK]
                else:
                    dk_acc[x - 1] = dk_acc[x - 1] + dk[:QBLK]
                    dv_acc[x - 1] = dv_acc[x - 1] + dv[:QBLK]
                dk_acc[x] = dk_acc[x] + dk[QBLK:]
                dv_acc[x] = dv_acc[x] + dv[QBLK:]
            for kb in range(qb):
                dk_ref[lay.idx(kb, r)] = dk_acc[kb]
                dv_ref[lay.idx(kb, r)] = dv_acc[kb]

        lay.per_residue(residue)

    cur, prev = lay.cur(lambda j: nchunk - 1 - j), lay.prev(lambda j: nchunk - 1 - j)
    carried = pltpu.VMEM((lay.nres * QBLK, LANES), F32)
    return _pcall(
        body, name, lay.grid, [cur, cur, prev, cur, prev, cur, cur, cur], [cur, cur, cur],
        [jax.ShapeDtypeStruct((S, AW), F32)] * 3, ("arbitrary", "arbitrary"),
        (qn, kn, kn, v, v, dycat, lse, delta), carry, scratch=[carried, carried])


def _shift_down(x, halo_prev, k, row):
    tm = x.shape[0]
    tail = jnp.concatenate([pltpu.roll(halo_prev, k, 0), jnp.zeros((tm - SUBLANES, x.shape[1]), x.dtype)], axis=0)
    return jnp.where(row < k, tail, pltpu.roll(x, k, 0))


def _shift_up(x, halo_next, k, row):
    tm = x.shape[0]
    head = jnp.concatenate([jnp.zeros((tm - SUBLANES, x.shape[1]), x.dtype), pltpu.roll(halo_next, SUBLANES - k, 0)], axis=0)
    return jnp.where(row >= tm - k, head, pltpu.roll(x, tm - k, 0))


def _conv_fwd(cu, halo_cu, cw_ref, row):
    u1 = _shift_down(cu, halo_cu, 1, row)
    u2 = _shift_down(cu, halo_cu, 2, row)
    cv = cw_ref[0:1, :] * u2 + cw_ref[1:2, :] * u1 + cw_ref[2:3, :] * cu + cw_ref[3:4, :]
    return cv, u1, u2


def mixer_out(os_, lses, proj, cw, x, vec, wout, name, carry=None):
    S = proj.shape[0]
    tm = _row_tile(S, 512)
    hb = tm // HALO_ROWS

    def body(o1, o2, o3, l1, l2, l3, pc_ref, ph_ref, cw_ref, x_ref, vec_ref, w_ref, ycat_ref, lse_ref, xn_ref, y_ref):
        i = pl.program_id(0)
        for p in range(AW // LANES):
            cs = slice(p * LANES, (p + 1) * LANES)
            ls = [l[:, cs] for l in (l1, l2, l3)]
            mx = jnp.maximum(jnp.maximum(ls[0], ls[1]), ls[2])
            t = mx + jnp.log(jnp.exp(ls[0] - mx) + jnp.exp(ls[1] - mx) + jnp.exp(ls[2] - mx))
            lse_ref[:, cs] = t
            acc = jnp.zeros((tm, LANES), F32)
            for l, o in zip(ls, (o1, o2, o3)):
                acc = acc + jnp.exp(l - t) * o[:, cs]
            ycat_ref[:, cs] = acc.astype(ycat_ref.dtype)
        row = lax.broadcasted_iota(jnp.int32, (tm, CW), 0)
        gb, gc, u = (pc_ref[:, k * CW:(k + 1) * CW].astype(F32) for k in range(3))
        ph = ph_ref[...].astype(F32)[HALO_ROWS - SUBLANES:]
        halo_cu = jnp.where(i > 0, ph[:, CW:2 * CW] * ph[:, 2 * CW:3 * CW], 0.0)
        cv, _, _ = _conv_fwd(gc * u, halo_cu, cw_ref, row)
        ycat_ref[:, AW:AW + CW] = (gb * cv).astype(ycat_ref.dtype)
        y = jnp.dot(ycat_ref[...].astype(MXU_DTYPE), w_ref[...], preferred_element_type=F32)
        xn_ref[...] = x_ref[...] + vec_ref[3:4, :] * y
        y_ref[...] = y.astype(y_ref.dtype)

    ot = pl.BlockSpec((tm, AW), lambda i: (i, 0))
    t = pl.BlockSpec((tm, D), lambda i: (i, 0))
    return _pcall(
        body, name, (S // tm,),
        [ot] * 6 + [pl.BlockSpec((tm, 3 * CW), lambda i: (i, 1)),
                    pl.BlockSpec((HALO_ROWS, 3 * CW), lambda i: (jnp.maximum(i * hb - 1, 0), 1)),
                    pl.BlockSpec((SUBLANES, CW), lambda i: (0, 0)),
                    t, pl.BlockSpec((SUBLANES, D), lambda i: (0, 0)), pl.BlockSpec((D, D), lambda i: (0, 0))],
        [t, ot, t, t],
        [jax.ShapeDtypeStruct((S, D), ACT_DTYPE), jax.ShapeDtypeStruct((S, AW), F32),
         jax.ShapeDtypeStruct((S, D), F32), jax.ShapeDtypeStruct((S, D), ACT_DTYPE)],
        ("arbitrary",), (*os_, *lses, proj, proj, cw, x, vec, wout), carry)


def out_proj_bwd(dxo, y, ycat, vec, wout, name, carry=None):
    S = dxo.shape[0]
    tm = _row_tile(S, 512)

    def body(dxo_ref, y_ref, yc_ref, vec_ref, w_ref, dyb_ref, dyc_ref, dl_ref, sums_ref):
        dxo = dxo_ref[...]
        dgate = jnp.sum(dxo * y_ref[...].astype(F32), axis=0, keepdims=True)
        dy = (vec_ref[3:4, :] * dxo).astype(MXU_DTYPE)
        dyb_ref[...] = dy
        dyc_ref[...] = lax.dot_general(dy, w_ref[...], NT_DIMS, preferred_element_type=F32)
        m_a = _head_masks(tm)
        for p in range(AW // LANES):
            cs = slice(p * LANES, (p + 1) * LANES)
            s_a, s_b = _pair_stat(dyc_ref[:, cs] * yc_ref[:, cs].astype(F32), m_a)
            dl_ref[:, cs] = jnp.where(m_a, s_a, s_b)
        _acc_rows(sums_ref, pl.program_id(0) == 0, (dgate,))

    t = pl.BlockSpec((tm, D), lambda i: (i, 0))
    at = pl.BlockSpec((tm, AW), lambda i: (i, 0))
    return _pcall(
        body, name, (S // tm,),
        [t, t, t, pl.BlockSpec((SUBLANES, D), lambda i: (0, 0)), pl.BlockSpec((D, D), lambda i: (0, 0))],
        [t, t, at, pl.BlockSpec((SUBLANES, D), lambda i: (0, 0))],
        [jax.ShapeDtypeStruct((S, D), MXU_DTYPE), jax.ShapeDtypeStruct((S, D), F32),
         jax.ShapeDtypeStruct((S, AW), F32), jax.ShapeDtypeStruct((SUBLANES, D), F32)],
        ("arbitrary",), (dxo, y, ycat, vec, wout), carry)


def mixer_mid_bwd(dqs, dks, dvs, proj, dycat, gvec, cw, name, carry=None):
    S = proj.shape[0]
    tm = _row_tile(S, 512)
    hb = tm // SUBLANES
    hp = tm // HALO_ROWS
    nsl = S // SUBLANES
    ntile = S // tm

    def body(dq1, dq2, dq3, dk1, dk2, dk3, dv1, dv2, dv3, pr_ref, pp_ref, pn_ref, dyc_ref, dyn_ref,
             g_ref, cw_ref, dp_ref, sums_ref):
        i = pl.program_id(0)
        m_a = _head_masks(tm)
        gsum = []
        for which, parts in ((0, (dq1, dq2, dq3)), (1, (dk1, dk2, dk3))):
            acc_g = []
            for p in range(AW // LANES):
                lo = which * AW + p * LANES
                cs = slice(p * LANES, (p + 1) * LANES)
                xp = pr_ref[:, lo:lo + LANES].astype(F32)
                s_a, s_b = _pair_stat(xp * xp, m_a)
                rr = jnp.where(m_a, lax.rsqrt(s_a * (1.0 / HD) + EPS), lax.rsqrt(s_b * (1.0 / HD) + EPS))
                xh = xp * rr
                dn = parts[0][:, cs] + parts[1][:, cs] + parts[2][:, cs]
                acc_g.append(jnp.sum(dn * xh, axis=0, keepdims=True))
                t = dn * g_ref[which:which + 1, cs]
                t_a, t_b = _pair_stat(t * xh, m_a)
                mean = jnp.where(m_a, t_a, t_b) * (1.0 / HD)
                dp_ref[:, lo:lo + LANES] = (rr * (t - xh * mean)).astype(dp_ref.dtype)
            gsum.append(jnp.concatenate(acc_g, axis=1))
        dp_ref[:, 2 * AW:3 * AW] = (dv1[...] + dv2[...] + dv3[...]).astype(dp_ref.dtype)
        row = lax.broadcasted_iota(jnp.int32, (tm, CW), 0)
        base = 3 * AW
        gb, gc, u = (pr_ref[:, base + k * CW:base + (k + 1) * CW].astype(F32) for k in range(3))
        cu = gc * u
        pp = pp_ref[...].astype(F32)[HALO_ROWS - SUBLANES:]
        halo_cu = jnp.where(i > 0, pp[:, CW:2 * CW] * pp[:, 2 * CW:3 * CW], 0.0)
        cv, u1, u2 = _conv_fwd(cu, halo_cu, cw_ref, row)
        dyc = dyc_ref[...]
        dp_ref[:, base:base + CW] = (dyc * cv).astype(dp_ref.dtype)
        dcv = dyc * gb
        gb_next = pn_ref[:, 0:CW].astype(F32)[:SUBLANES]
        halo_dcv = jnp.where(i < ntile - 1, dyn_ref[...] * gb_next, 0.0)
        d1 = _shift_up(dcv, halo_dcv, 1, row)
        d2 = _shift_up(dcv, halo_dcv, 2, row)
        dcu = cw_ref[2:3, :] * dcv + cw_ref[1:2, :] * d1 + cw_ref[0:1, :] * d2
        dp_ref[:, base + CW:base + 2 * CW] = (dcu * u).astype(dp_ref.dtype)
        dp_ref[:, base + 2 * CW:base + 3 * CW] = (dcu * gc).astype(dp_ref.dtype)
        rows = (gsum[0], gsum[1],
                jnp.sum(dcv * u2, axis=0, keepdims=True), jnp.sum(dcv * u1, axis=0, keepdims=True),
                jnp.sum(dcv * cu, axis=0, keepdims=True), jnp.sum(dcv, axis=0, keepdims=True))
        _acc_rows(sums_ref, i == 0, rows)

    at = pl.BlockSpec((tm, AW), lambda i: (i, 0))
    return _pcall(
        body, name, (ntile,),
        [at] * 9 + [
            pl.BlockSpec((tm, INC), lambda i: (i, 0)),
            pl.BlockSpec((HALO_ROWS, 3 * CW), lambda i: (jnp.maximum(i * hp - 1, 0), 1)),
            pl.BlockSpec((HALO_ROWS, 3 * CW), lambda i: (jnp.minimum((i + 1) * hp, S // HALO_ROWS - 1), 1)),
            pl.BlockSpec((tm, CW), lambda i: (i, 1)),
            pl.BlockSpec((SUBLANES, CW), lambda i: (jnp.minimum((i + 1) * hb, nsl - 1), 1)),
            pl.BlockSpec((SUBLANES, AW), lambda i: (0, 0)),
            pl.BlockSpec((SUBLANES, CW), lambda i: (0, 0))],
        [pl.BlockSpec((tm, INC), lambda i: (i, 0)), pl.BlockSpec((SUBLANES, AW), lambda i: (0, 0))],
        [jax.ShapeDtypeStruct((S, INC), MXU_DTYPE), jax.ShapeDtypeStruct((SUBLANES, AW), F32)],
        ("arbitrary",), (*dqs, *dks, *dvs, proj, proj, proj, dycat, dycat, gvec, cw), carry)


def mixer_in_bwd(dxo, x, dproj, vec, winp, name, carry=None):
    S = x.shape[0]
    tm = _row_tile(S, 512)
    pc = INC // NCHIP

    def body(dxo_ref, x_ref, dp_ref, vec_ref, w_ref, dxi_ref, sums_ref):
        xhat, r, gain, ng, sc, _, _ = _ada(x_ref[...], vec_ref)
        dh = jnp.zeros((tm, D), F32)
        for j in range(NCHIP):
            dh = dh + lax.dot_general(dp_ref[:, j * pc:(j + 1) * pc], w_ref[j], NT_DIMS, preferred_element_type=F32)
        dx, dshift, dscale, dng = _ada_bwd(dh, xhat, r, gain, ng, sc)
        dxi_ref[...] = dxo_ref[...] + dx
        _acc_rows(sums_ref, pl.program_id(0) == 0, (dshift, dscale, dng))

    t = pl.BlockSpec((tm, D), lambda i: (i, 0))
    return _pcall(
        body, name, (S // tm,),
        [t, t, pl.BlockSpec((tm, INC), lambda i: (i, 0)),
         pl.BlockSpec((SUBLANES, D), lambda i: (0, 0)),
         pl.BlockSpec((NCHIP, D, pc), lambda i: (0, 0, 0), pipeline_mode=pl.Buffered(1))],
        [t, pl.BlockSpec((SUBLANES, D), lambda i: (0, 0))],
        [jax.ShapeDtypeStruct((S, D), F32), jax.ShapeDtypeStruct((SUBLANES, D), F32)],
        ("arbitrary",), (dxo, x, dproj, vec, winp), carry)


def _vec(mod_l, ng_l, i):
    m = mod_l.reshape(3, 3, D)
    rows = jnp.stack([ng_l[i], m[i, 1], m[i, 0], m[i, 2]])
    return jnp.concatenate([rows, jnp.zeros((SUBLANES - 4, D), F32)], axis=0)


def local_step(x, target, mods, ngs, gvecs, cws, shards, w_first, cflag):
    saved = []
    weights = [dict(w1=[None, None], w2=[None, None]) for _ in range(2)]
    weights[0]["w1"][0], weights[0]["w2"][0] = w_first[0], w_first[1].reshape(DFF, D)
    h = x
    for l in range(2):
        w, sh = weights[l], shards[l]
        nxt = shards[l + 1] if l == 0 else None
        vecs = [_vec(mods[l], ngs[l], i) for i in range(3)]
        x0 = h
        (x1, a0, f0), (win, wout, w2b) = ffn_fwd(x0, vecs[0], w["w1"][0], w["w2"][0], 0.5, f"ffn_fwd_l{l}a",
                                                 carry=Carry("gather", [sh["win"], sh["wout"], sh["w2"][1]]))
        w["win"], w["wout"], w["w2"][1] = win, wout.reshape(D, D), w2b.reshape(DFF, D)
        quarter = [sh["w1"][1][k * (D // 4):(k + 1) * (D // 4)] for k in range(4)]
        (proj, h1b, qn, kn, v), w1b = mixer_in(x1, vecs[1], w["win"], gvecs[l], f"mixer_in_l{l}",
                                               carry=Carry("gather", [quarter[0]]))
        os_, lses, w1b = [], [], list(w1b)
        for k, d in enumerate(DILATIONS):
            (o, lse_d), got = attn_fwd(qn, kn, v, d, f"attn_fwd_l{l}_d{d}", carry=Carry("gather", [quarter[1 + k]]))
            w1b.append(got[0])
            os_.append(o)
            lses.append(lse_d)
        w["w1"][1] = jnp.concatenate(w1b, axis=1)
        (ycat, lse, x2, y), got = mixer_out(os_, lses, proj, cws[l], x1, vecs[1], w["wout"], f"mixer_out_l{l}",
                                            carry=Carry("gather", [nxt["w2"][0]]) if nxt else None)
        if nxt:
            weights[1]["w2"][0] = got[0].reshape(DFF, D)
        if nxt:
            (h, a2, f2), got = ffn_fwd(x2, vecs[2], w["w1"][1], w["w2"][1], 0.5, f"ffn_fwd_l{l}b",
                                       carry=Carry("gather", [nxt["w1"][0]]))
            weights[1]["w1"][0] = got[0]
        else:
            (dx, a2, f2, loss_blk), _ = ffn_fwd(x2, vecs[2], w["w1"][1], w["w2"][1], 0.5, f"ffn_fwd_l{l}b",
                                                target=target)
        saved.append(dict(vecs=vecs, x0=x0, a0=a0, f0=f0, x1=x1, proj=proj, h1b=h1b, qn=qn, kn=kn, v=v,
                          ycat=ycat, lse=lse, y=y, x2=x2, a2=a2, f2=f2))
    sums, totals, g_prev = [None, None], [None, None], None
    w2r = DFF // NCHIP
    for l in (1, 0):
        w, s = weights[l], saved[l]
        vecs = s["vecs"]
        ride = g_prev is not None
        own = l == 0
        mine, other = [None] * 6, [None] * 6

        def half_sum(group, recv, k0):
            return [add_half(g, r, cflag, f"add_sibling_l{l}_{k0 + j}") for j, (g, r) in enumerate(zip(group, recv))]

        def chip_sum(landed, k0):
            return [sum_chips(t, f"sum_chips_l{l}_{k0 + j}") for j, t in enumerate(landed)]

        (dx, hb, dfb, act, da, sums2), got = ffn_bwd(
            dx, s["x2"], s["a2"], s["f2"], vecs[2], w["w1"][1], w["w2"][1], 0.5, f"ffn_bwd_l{l}b",
            carry=Carry("swap_halves", g_prev) if ride else None)
        dw1b, _ = wgrad(hb, da, D, HALF, f"wgrad_w1_l{l}b")
        dw2b, _ = wgrad(act, dfb, HALF, D, f"wgrad_w2_l{l}b")
        if ride:
            wire = [add_half(g_prev[k], got[k], cflag, f"add_sibling_l{l + 1}_{k}") for k in range(6)]
        g_ffn_b = [dw1b, dw2b.reshape(NCHIP, w2r, D)]
        (dyb, dycat, delta, sums_o), got = out_proj_bwd(
            dx, s["y"], s["ycat"], vecs[1], w["wout"], f"out_proj_bwd_l{l}",
            carry=Carry("swap_halves", g_ffn_b) if own else None)
        dwout, _ = wgrad(s["ycat"].astype(MXU_DTYPE), dyb, D // 2, D, f"wgrad_wout_l{l}")
        if own:
            wire_ffn_b = half_sum(g_ffn_b, got, 4)
        dqs, dks, dvs, landed = [], [], [], {}
        for d in DILATIONS:
            carry = None
            if ride and d == 1:
                carry = Carry("scatter", wire[3:])
            if ride and d == 16:
                carry = Carry("scatter", wire[:3])
            if own and d == 4:
                carry = Carry("scatter", wire_ffn_b)
            (dq, dk, dv), landed[d] = attn_bwd(s["qn"], s["kn"], s["v"], dycat, s["lse"], delta, d,
                                               f"attn_bwd_l{l}_d{d}", carry=carry)
            dqs.append(dq)
            dks.append(dk)
            dvs.append(dv)
        if ride:
            tot = [sum_chips(t, f"sum_chips_l{l + 1}_{k}") for k, t in enumerate(list(landed[16]) + list(landed[1]))]
        if own:
            mine[4:6] = chip_sum(landed[4], 4)
        ready = (tot if ride else []) + (mine[4:6] if own else [])
        (dproj, sums_m), got = mixer_mid_bwd(dqs, dks, dvs, s["proj"], dycat, gvecs[l], cws[l], f"mixer_mid_bwd_l{l}",
                                             carry=Carry("swap", ready) if ready else None)
        if ride:
            totals[l + 1] = (tot, list(got[:6]))
        if own:
            other[4:6] = list(got[-2:])
        dwin, _ = wgrad(s["h1b"], dproj, D, INC // NCHIP, f"wgrad_win_l{l}")
        g_mixer = [dwin, dwout.reshape(NCHIP, D // NCHIP, D)]
        (dx, sums1), got = mixer_in_bwd(dx, s["x1"], dproj, vecs[1], w["win"], f"mixer_in_bwd_l{l}",
                                        carry=Carry("swap_halves", g_mixer) if own else None)
        if own:
            wire_mixer = half_sum(g_mixer, got, 2)
        (dx, hb, dfb, act, da, sums0), _ = ffn_bwd(
            dx, s["x0"], s["a0"], s["f0"], vecs[0], w["w1"][0], w["w2"][0], 0.5, f"ffn_bwd_l{l}a")
        dw1a, got = wgrad(hb, da, D, HALF, f"wgrad_w1_l{l}a", carry=Carry("scatter", wire_mixer) if own else None)
        if own:
            mine[2:4] = chip_sum(got, 2)
        dw2a, got = wgrad(act, dfb, HALF, D, f"wgrad_w2_l{l}a", carry=Carry("swap", mine[2:4]) if own else None)
        g_ffn_a = [dw1a, dw2a.reshape(NCHIP, w2r, D)]
        if own:
            other[2:4] = list(got)
            wire_ffn_a = half_sum(g_ffn_a, run_carry(Carry("swap_halves", g_ffn_a), "swap_halves_tail"), 0)
            mine[0:2] = chip_sum(run_carry(Carry("scatter", wire_ffn_a), "scatter_grads_tail"), 0)
            other[0:2] = list(run_carry(Carry("swap", mine[0:2]), "swap_totals_tail"))
            totals[l] = (mine, other)
        g_prev = g_ffn_a + g_mixer + g_ffn_b
        sums[l] = (sums0, sums1, sums_o, sums2, sums_m)
    return loss_blk, dx, totals, sums


def small_all_gather(blk, name):
    m_per, n = blk.shape

    def body(x_ref, out_ref, send_sems, recv_sems, local_sem):
        x, y, c = _here()
        me, sibling = (x, y, c), (x, y, 1 - c)
        chips = [(1 - x, y), (x, 1 - y), (1 - x, 1 - y)]

        def rows(px, py, pc):
            return out_ref.at[pl.ds((4 * px + 2 * py + pc) * m_per, m_per), :]

        def copy(k, block, to, src=None):
            return pltpu.make_async_remote_copy(
                src_ref=rows(*block) if src is None else src, dst_ref=rows(*block),
                send_sem=send_sems.at[k], recv_sem=recv_sems.at[k], device_id=to, device_id_type=MESH)

        mine = pltpu.make_async_copy(x_ref, rows(*me), local_sem)
        mine.start()
        first = [copy(0, me, sibling, src=x_ref)]
        first += [copy(1 + j, me, (*chip, c), src=x_ref) for j, chip in enumerate(chips)]
        for cp in first:
            cp.start()
        passed = [copy(4 + j, (*chip, c), sibling) for j, chip in enumerate(chips)]
        for j, chip in enumerate(chips):
            copy(1 + j, (*chip, c), me).wait_recv()
            passed[j].start()
        copy(0, sibling, me).wait_recv()
        for j, chip in enumerate(chips):
            copy(4 + j, (*chip, 1 - c), me).wait_recv()
        for cp in first + passed:
            cp.wait_send()
        mine.wait()

    return pl.pallas_call(
        body, name=name,
        out_shape=jax.ShapeDtypeStruct((NDEV * m_per, n), blk.dtype),
        in_specs=[pl.BlockSpec(memory_space=pltpu.VMEM)],
        out_specs=pl.BlockSpec(memory_space=pltpu.VMEM),
        scratch_shapes=[pltpu.SemaphoreType.DMA((7,)), pltpu.SemaphoreType.DMA((7,)), pltpu.SemaphoreType.DMA],
        compiler_params=pltpu.CompilerParams(vmem_limit_bytes=VMEM_LIMIT),
    )(blk)


EW_BLOCK_BYTES = 1 << 20


def _ew_rows(rows, cols, refs=8):
    want = max(16, EW_BLOCK_BYTES * (2 if refs <= 12 else 1) // (4 * cols))
    best = None
    for t in range(16, rows + 1, 16):
        if rows % t == 0 and t <= want:
            best = t
    return best if best is not None else rows


def add_half(g, recv, cflag, name):
    pieces, r, cols = g.shape
    r2 = r // 2
    tr = _ew_rows(r2, cols, refs=3)
    nt = r2 // tr

    def body(c_ref, g_ref, r_ref, o_ref):
        o_ref[...] = (g_ref[...] + r_ref[...]).astype(o_ref.dtype)

    half = pl.BlockSpec((None, tr, cols), lambda j, i, c_ref: (j, i, 0))
    return pl.pallas_call(
        body, name=name,
        grid_spec=pltpu.PrefetchScalarGridSpec(
            num_scalar_prefetch=1, grid=(pieces, nt),
            in_specs=[pl.BlockSpec((None, tr, cols), lambda j, i, c_ref: (j, c_ref[0] * nt + i, 0)), half],
            out_specs=half),
        out_shape=jax.ShapeDtypeStruct((pieces, r2, cols), WIRE_DTYPE),
        compiler_params=_params(("arbitrary", "arbitrary")),
    )(cflag, g, recv)


def sum_chips(recv, name):
    _, r, cols = recv.shape
    tr = _ew_rows(r, cols, refs=3)

    def body(r_ref, o_ref):
        acc = r_ref[0].astype(F32)
        for k in range(1, NCHIP):
            acc = acc + r_ref[k].astype(F32)
        o_ref[...] = acc

    return pl.pallas_call(
        body, name=name, grid=(r // tr,),
        in_specs=[pl.BlockSpec((NCHIP, tr, cols), lambda i: (0, i, 0))],
        out_specs=pl.BlockSpec((tr, cols), lambda i: (i, 0)),
        out_shape=jax.ShapeDtypeStruct((r, cols), F32),
        compiler_params=_params(("arbitrary",)),
    )(recv)


def sum_devices(rows8, name):
    def body(r_ref, o_ref):
        acc = r_ref[0:1, :]
        for k in range(1, NDEV):
            acc = acc + r_ref[k:k + 1, :]
        o_ref[...] = jnp.broadcast_to(acc, o_ref.shape)

    return pl.pallas_call(
        body, name=name, out_shape=jax.ShapeDtypeStruct(rows8.shape, F32),
        in_specs=[pl.BlockSpec(memory_space=pltpu.VMEM)], out_specs=pl.BlockSpec(memory_space=pltpu.VMEM),
        compiler_params=pltpu.CompilerParams(vmem_limit_bytes=VMEM_LIMIT),
    )(rows8)


def adamw(w, m, v, srcs, cflag, name, halves=False):
    planes, r, cols = w.shape
    rh = r // 2 if halves else r
    tr = _ew_rows(rh, cols, refs=7 + (2 if halves else 1) * planes)
    nth = rh // tr
    flat = [a for s in srcs for a in (s if halves else (s,))]
    ns = len(flat)
    per = ns // planes

    def body(c_ref, w_ref, m_ref, v_ref, *rest):
        s_refs, (g_ref, d_ref, mo_ref, vo_ref) = rest[:ns], rest[ns:]
        p, i = pl.program_id(0), pl.program_id(1)
        if halves:
            mine = jnp.logical_not(jnp.logical_xor(i >= nth, c_ref[0] == 1))
            blocks = [jnp.where(mine, s_refs[2 * k][...], s_refs[2 * k + 1][...]) for k in range(planes)]
        else:
            blocks = [s[...] for s in s_refs]
        g = blocks[0]
        for k in range(1, planes):
            g = jnp.where(p == k, blocks[k], g)
        g_ref[...] = g
        m_new = ADAM_B1 * m_ref[...] + (1.0 - ADAM_B1) * g
        v_new = ADAM_B2 * v_ref[...] + (1.0 - ADAM_B2) * (g * g)
        mo_ref[...] = m_new
        vo_ref[...] = v_new
        m_hat = m_new / (1.0 - ADAM_B1 ** ADAM_STEP)
        v_hat = v_new / (1.0 - ADAM_B2 ** ADAM_STEP)
        d_ref[...] = -ADAM_LR * (m_hat / (jnp.sqrt(v_hat) + ADAM_EPS) + ADAM_WD * w_ref[...])

    pt = pl.BlockSpec((None, tr, cols), lambda p, i: (p, i, 0))
    st = [pl.BlockSpec((tr, cols), functools.partial(lambda k, p, i: (jnp.where(p == k, i % nth, 0), 0), j // per))
          for j in range(ns)]
    return pl.pallas_call(
        body, name=name, grid=(planes, r // tr),
        in_specs=[pl.BlockSpec(memory_space=pltpu.SMEM), pt, pt, pt] + st,
        out_specs=[pt] * 4,
        out_shape=[jax.ShapeDtypeStruct(w.shape, F32)] * 4,
        compiler_params=_params(("arbitrary", "arbitrary")),
    )(cflag, w, m, v, *flat)


ADA_COLS = 9 * D // NCHIP


def mod_fwd(c_all, w_ada, b_shard, name):
    def body(c_ref, w_ref, b_ref, o_ref):
        cc = c_ref[...]
        sc = cc * jax.nn.sigmoid(cc)
        o_ref[...] = jnp.dot(sc, w_ref[...], preferred_element_type=F32,
                             precision=lax.Precision.HIGHEST) + b_ref[...]

    return pl.pallas_call(
        body, name=name, grid=(2,),
        in_specs=[pl.BlockSpec((NDEV, D), lambda l: (0, 0)),
                  pl.BlockSpec((None, D, ADA_COLS), lambda l: (l, 0, 0)),
                  pl.BlockSpec((None, 1, ADA_COLS), lambda l: (l, 0, 0))],
        out_specs=pl.BlockSpec((None, NDEV, ADA_COLS), lambda l: (l, 0, 0)),
        out_shape=jax.ShapeDtypeStruct((2, NDEV, ADA_COLS), F32),
        compiler_params=_params(("arbitrary",)),
    )(c_all, w_ada, b_shard.reshape(2, 1, ADA_COLS))


def wada_grad(c_all_t, dmod, name):
    ct = ADA_COLS // 3

    def body(c_ref, d_ref, o_ref):
        cc = c_ref[...]
        sc = cc * jax.nn.sigmoid(cc)
        acc = sc[:, 0:1] * d_ref[0:1, :]
        for b in range(1, NDEV):
            acc = acc + sc[:, b:b + 1] * d_ref[b:b + 1, :]
        o_ref[...] = acc

    return pl.pallas_call(
        body, name=name, grid=(2, 3),
        in_specs=[pl.BlockSpec((D, LANES), lambda l, j: (0, 0)),
                  pl.BlockSpec((None, NDEV, ct), lambda l, j: (l, 0, j))],
        out_specs=pl.BlockSpec((None, D, ct), lambda l, j: (l, 0, j)),
        out_shape=jax.ShapeDtypeStruct((2, D, ADA_COLS), F32),
        compiler_params=_params(("arbitrary", "arbitrary")),
    )(c_all_t, dmod)


def _pad_rows(row, rows=SUBLANES):
    return jnp.concatenate([row[None, :], jnp.zeros((rows - 1, row.shape[0]), row.dtype)], axis=0)


def kernel(x, c, w_ada, b_ada, norm_g, w_in, q_norm_g, k_norm_g, conv_w, conv_b, w_out, ffn_w1, ffn_w2, loss_target, m_w_ada, m_b_ada, m_norm_g, m_w_in, m_q_norm_g, m_k_norm_g, m_conv_w, m_conv_b, m_w_out, m_ffn_w1, m_ffn_w2, v_w_ada, v_b_ada, v_norm_g, v_w_in, v_q_norm_g, v_k_norm_g, v_conv_w, v_conv_b, v_w_out, v_ffn_w1, v_ffn_w2):
    ix, iy, ic = lax.axis_index("x"), lax.axis_index("y"), lax.axis_index("c")
    chip = 2 * ix + iy
    dev = 2 * chip + ic
    cflag = jnp.reshape(ic, (1,)).astype(jnp.int32)
    ngw = norm_g.shape[-1]
    cww = conv_w.shape[-1]

    pack = jnp.concatenate([c[0], norm_g.reshape(-1), conv_w.reshape(-1)])
    got = small_all_gather(_pad_rows(pack), "gather_c_normg_convw")[::SUBLANES]
    c_all = got[:, :D]
    per_chip = got[::2]
    ng_full = jnp.concatenate([per_chip[j, D:D + 6 * ngw].reshape(2, 3, ngw) for j in range(NCHIP)], axis=-1)
    cw_full = jnp.concatenate([per_chip[j, D + 6 * ngw:].reshape(2, 3, cww) for j in range(NCHIP)], axis=-1)

    b_shard = lax.dynamic_slice_in_dim(b_ada, chip * ADA_COLS, ADA_COLS, axis=1)
    mod_blk = mod_fwd(c_all, w_ada, b_shard, "mod_fwd").reshape(2 * NDEV, ADA_COLS)
    mod_all = small_all_gather(mod_blk, "gather_mod").reshape(NDEV, 2, NDEV, ADA_COLS)[::2]
    mod_mine = lax.dynamic_index_in_dim(mod_all, dev, axis=2, keepdims=False)
    mods = [mod_mine[:, l, :].reshape(-1) for l in range(2)]

    shards, gvecs, cws = [], [], []
    for l in range(2):
        shards.append(dict(w1=[ffn_w1[l, i].astype(MXU_DTYPE) for i in range(2)],
                           w2=[ffn_w2[l, i].astype(MXU_DTYPE) for i in range(2)],
                           win=w_in[l].astype(MXU_DTYPE), wout=w_out[l].astype(MXU_DTYPE)))
        gv = jnp.stack([jnp.tile(q_norm_g[l], AW // HD), jnp.tile(k_norm_g[l], AW // HD)])
        gvecs.append(jnp.concatenate([gv, jnp.zeros((SUBLANES - 2, AW), F32)], axis=0))
        cws.append(jnp.concatenate([cw_full[l], conv_b[l][None, :], jnp.zeros((SUBLANES - 4, CW), F32)], axis=0))
    w_first = gather_split([shards[0]["w1"][0], shards[0]["w2"][0]], "gather_first_ffn")

    loss_blk, dx, totals, sums = local_step(x[0], loss_target[0], mods, [ng_full[0], ng_full[1]], gvecs, cws,
                                            shards, w_first, cflag)

    dmods, dngs, dqg, dkg, dcw, dcb = [], [], [], [], [], []
    for l in range(2):
        s0, s1, so, s2, sm = sums[l]
        dmods.append(jnp.concatenate([s0[0], s0[1], s0[3], s1[0], s1[1], so[0], s2[0], s2[1], s2[3]]))
        dngs.append(jnp.concatenate([s0[2], s1[2], s2[2]]))
        dqg.append(sm[0].reshape(AW // HD, HD).sum(0))
        dkg.append(sm[1].reshape(AW // HD, HD).sum(0))
        dcw.append(sm[2:5].reshape(-1))
        dcb.append(sm[5])
    small = jnp.concatenate(dmods + dngs + dqg + dkg + dcw + dcb + [loss_blk[0]])
    small_all = small_all_gather(_pad_rows(small), "gather_small_grads")[::SUBLANES]
    nm = 9 * D
    dmod_all = small_all[:, :2 * nm].reshape(NDEV, 2, NCHIP, ADA_COLS)
    dmod_mine = lax.dynamic_index_in_dim(dmod_all, chip, axis=2, keepdims=False).transpose(1, 0, 2)
    tot = sum_devices(small_all, "sum_small_grads")[0]
    o = 2 * nm
    g_b_ada = tot[:o].reshape(2, nm)
    g_norm_g = lax.dynamic_slice_in_dim(tot[o:o + 6 * D].reshape(2, 3, D), chip * ngw, ngw, axis=2)
    o += 6 * D
    g_qg = tot[o:o + 2 * HD].reshape(2, HD)
    o += 2 * HD
    g_kg = tot[o:o + 2 * HD].reshape(2, HD)
    o += 2 * HD
    g_cw = lax.dynamic_slice_in_dim(tot[o:o + 6 * CW].reshape(2, 3, CW), chip * cww, cww, axis=2)
    o += 6 * CW
    g_cb = tot[o:o + 2 * CW].reshape(2, CW)
    loss = tot[o + 2 * CW]

    c_all_t = jnp.concatenate([c_all.T, jnp.zeros((D, LANES - NDEV), F32)], axis=1)
    g_wada_src = wada_grad(c_all_t, dmod_mine, "wada_grad")

    def halves(k_of_plane):
        return [(totals[l][0][k], totals[l][1][k]) for l, k in k_of_plane]

    r_wada = adamw(w_ada, m_w_ada, v_w_ada, [g_wada_src[0], g_wada_src[1]], cflag, "adamw_w_ada")
    r_win = adamw(w_in, m_w_in, v_w_in, halves([(0, 2), (1, 2)]), cflag, "adamw_w_in", halves=True)
    r_wout = adamw(w_out, m_w_out, v_w_out, halves([(0, 3), (1, 3)]), cflag, "adamw_w_out", halves=True)
    r_w1 = adamw(ffn_w1.reshape(4, D, HALF), m_ffn_w1.reshape(4, D, HALF), v_ffn_w1.reshape(4, D, HALF),
                 halves([(0, 0), (0, 4), (1, 0), (1, 4)]), cflag, "adamw_ffn_w1", halves=True)
    w2r = DFF // NCHIP
    r_w2 = adamw(ffn_w2.reshape(4, w2r, D), m_ffn_w2.reshape(4, w2r, D), v_ffn_w2.reshape(4, w2r, D),
                 halves([(0, 1), (0, 5), (1, 1), (1, 5)]), cflag, "adamw_ffn_w2", halves=True)
    r_w1 = [t.reshape(ffn_w1.shape) for t in r_w1]
    r_w2 = [t.reshape(ffn_w2.shape) for t in r_w2]

    smalls = [("b_ada", b_ada, m_b_ada, v_b_ada, g_b_ada), ("norm_g", norm_g, m_norm_g, v_norm_g, g_norm_g),
              ("q_norm_g", q_norm_g, m_q_norm_g, v_q_norm_g, g_qg), ("k_norm_g", k_norm_g, m_k_norm_g, v_k_norm_g, g_kg),
              ("conv_w", conv_w, m_conv_w, v_conv_w, g_cw), ("conv_b", conv_b, m_conv_b, v_conv_b, g_cb)]
    n_small = sum(t[1].size for t in smalls)
    pad = (-n_small) % (16 * LANES)

    def packed(idx):
        flat = jnp.concatenate([t[idx].reshape(-1) for t in smalls] + [jnp.zeros((pad,), F32)])
        return flat.reshape(-1, LANES)

    r_small = adamw(packed(1)[None], packed(2)[None], packed(3)[None], [packed(4)], cflag, "adamw_small")
    small_out = {}
    o = 0
    for name_, w_, _, _, _ in smalls:
        small_out[name_] = [t.reshape(-1)[o:o + w_.size].reshape(w_.shape) for t in r_small]
        o += w_.size

    res = {"w_ada": r_wada, "w_in": r_win, "w_out": r_wout, "ffn_w1": r_w1, "ffn_w2": r_w2, **small_out}
    order = ["w_ada", "b_ada", "norm_g", "w_in", "q_norm_g", "k_norm_g", "conv_w", "conv_b", "w_out", "ffn_w1", "ffn_w2"]
    outs = [loss, dx[None]]
    for k in range(4):
        outs += [res[nm_][k] for nm_ in order]
    return tuple(outs)
```
